```python
import math
import jax, jax.numpy as jnp
from jax import lax
import numpy as np

D_MODEL = 2048
BATCH = 8
SEQ = 8192
DEPTH = 2

A_HEADS = 8
A_KV_HEADS = 2
A_HEAD_DIM = 64
A_WIDTH = A_HEADS * A_HEAD_DIM
A_KV_WIDTH = A_KV_HEADS * A_HEAD_DIM
WINDOW = 128
A_BLOCK = 128
ROPE_THETA = 10000.0
R_WIDTH = 1024
R_BLOCKS = 8
R_BLOCK_DIM = R_WIDTH // R_BLOCKS
R_C = 8.0
CONV_WIDTH = 4
G_HEADS = 4
G_HEAD_DIM = 128
G_WIDTH = G_HEADS * G_HEAD_DIM
G_CHUNK = 64
MIX_WIDTH = A_WIDTH + R_WIDTH + G_WIDTH
IN_SIZES = (A_WIDTH, A_KV_WIDTH, A_KV_WIDTH, A_WIDTH, R_WIDTH, R_WIDTH,
            G_WIDTH, G_WIDTH, G_WIDTH, G_WIDTH, G_HEADS, G_HEADS)
N_IN = sum(IN_SIZES)
DEEPNORM_ALPHA = (2 * DEPTH) ** 0.25
DEEPNORM_BETA = (8 * DEPTH) ** -0.25
LN_EPS = 1e-5
RMS_EPS = 1e-6

kernel_name = "hybrid_swa_rglru_gdn_deepnorm"


def layer_norm(x, g, b):
    xf = x.astype(jnp.float32)
    mu = jnp.mean(xf, -1, keepdims=True)
    var = jnp.mean(jnp.square(xf - mu), -1, keepdims=True)
    return ((xf - mu) * lax.rsqrt(var + LN_EPS) * g.astype(jnp.float32) + b.astype(jnp.float32)).astype(x.dtype)


def rope_tables(seq, dim):
    inv = 1.0 / (ROPE_THETA ** (jnp.arange(0, dim, 2, dtype=jnp.float32) / dim))
    ang = jnp.arange(seq, dtype=jnp.float32)[:, None] * inv[None, :]
    return jnp.cos(ang), jnp.sin(ang)


def apply_rope(x, cos, sin):
    xf = x.astype(jnp.float32)
    x1, x2 = jnp.split(xf, 2, axis=-1)
    c = cos[None, :, None, :]
    s = sin[None, :, None, :]
    return jnp.concatenate([x1 * c - x2 * s, x2 * c + x1 * s], axis=-1).astype(x.dtype)


def causal_depthwise_conv(x, w):
    return lax.conv_general_dilated(
        x, w[:, None, :].astype(x.dtype), window_strides=(1,),
        padding=[(CONV_WIDTH - 1, 0)], dimension_numbers=("NWC", "WIO", "NWC"),
        feature_group_count=x.shape[-1])


def sliding_window_attention(q, k, v, sinks):
    B, S, _, D = q.shape
    nb = S // A_BLOCK
    grp = A_HEADS // A_KV_HEADS
    qb = q.reshape(B, nb, A_BLOCK, A_KV_HEADS, grp, D)
    pad = ((0, 0), (A_BLOCK, 0), (0, 0), (0, 0))
    kp = jnp.pad(k, pad).reshape(B, nb + 1, A_BLOCK, A_KV_HEADS, D)
    vp = jnp.pad(v, pad).reshape(B, nb + 1, A_BLOCK, A_KV_HEADS, D)
    kw = jnp.concatenate([kp[:, :-1], kp[:, 1:]], axis=2)
    vw = jnp.concatenate([vp[:, :-1], vp[:, 1:]], axis=2)
    scores = jnp.einsum("bnqhgd,bnkhd->bhgnqk", qb, kw).astype(jnp.float32) * (D ** -0.5)
    i = jnp.arange(A_BLOCK)[:, None]
    j = jnp.arange(2 * A_BLOCK)[None, :]
    diff = i - j + A_BLOCK
    band = (diff >= 0) & (diff < WINDOW)
    kpos = (jnp.arange(nb)[:, None, None] - 1) * A_BLOCK + j[None]
    mask = band[None] & (kpos >= 0)
    scores = jnp.where(mask, scores, -jnp.inf)
    sink = jnp.broadcast_to(
        sinks.astype(jnp.float32).reshape(1, A_KV_HEADS, grp, 1, 1, 1),
        scores.shape[:-1] + (1,))
    probs = jax.nn.softmax(jnp.concatenate([scores, sink], axis=-1), axis=-1)[..., :-1]
    out = jnp.einsum("bhgnqk,bnkhd->bnqhgd", probs.astype(v.dtype), vw)
    return out.reshape(B, S, A_HEADS * D)


def rg_lru(x, w_a, b_a, w_x, b_x, lam):
    B, S, _ = x.shape
    xb = x.reshape(B, S, R_BLOCKS, R_BLOCK_DIM)
    r = jax.nn.sigmoid(jnp.einsum("bsnc,ncd->bsnd", xb, w_a).reshape(B, S, R_WIDTH) + b_a)
    ig = jax.nn.sigmoid(jnp.einsum("bsnc,ncd->bsnd", xb, w_x).reshape(B, S, R_WIDTH) + b_x)
    log_a = -R_C * r.astype(jnp.float32) * jax.nn.softplus(-lam.astype(jnp.float32))
    a = jnp.exp(log_a)
    u = jnp.sqrt(-jnp.expm1(2.0 * log_a)) * (ig * x).astype(jnp.float32)

    def combine(left, right):
        a1, b1 = left
        a2, b2 = right
        return a1 * a2, a2 * b1 + b2

    _, h = lax.associative_scan(combine, (a, u), axis=1)
    return h.astype(x.dtype)


def gated_delta_chunked(q, k, v, g, beta):
    B, S, H, Dk = q.shape
    Dv = v.shape[-1]
    N = S // G_CHUNK
    C = G_CHUNK

    def chunks(t):
        return t.reshape(B, N, C, H, -1).transpose(0, 3, 1, 2, 4)

    q, k, v = chunks(q), chunks(k), chunks(v)
    g = jnp.cumsum(g.reshape(B, N, C, H).transpose(0, 3, 1, 2), axis=-1)
    beta = beta.reshape(B, N, C, H).transpose(0, 3, 1, 2)
    tril = jnp.tril(jnp.ones((C, C), dtype=bool))
    strict = jnp.tril(jnp.ones((C, C), dtype=bool), -1)
    decay = jnp.exp(jnp.where(tril, g[..., :, None] - g[..., None, :], -jnp.inf))
    kb = k * beta[..., None]
    vb = v * beta[..., None]
    m = jnp.where(strict, jnp.einsum("bhncd,bhnjd->bhncj", kb, k) * decay, 0.0)
    lhs = jnp.eye(C, dtype=jnp.float32) + m
    u = lax.linalg.triangular_solve(lhs, vb, left_side=True, lower=True, unit_diagonal=True)
    w = lax.linalg.triangular_solve(lhs, kb * jnp.exp(g)[..., None], left_side=True,
                                    lower=True, unit_diagonal=True)
    qk = jnp.where(tril, jnp.einsum("bhncd,bhnjd->bhncj", q, k) * decay, 0.0)
    q_dec = q * jnp.exp(g)[..., None]
    k_dec = k * jnp.exp(g[..., -1:] - g)[..., None]
    g_last = jnp.exp(g[..., -1])

    def step(state, inp):
        qk_i, qd_i, kd_i, u_i, w_i, gl_i = inp
        v_new = u_i - jnp.einsum("bhcd,bhde->bhce", w_i, state)
        o = jnp.einsum("bhcd,bhde->bhce", qd_i, state) + jnp.einsum("bhcj,bhje->bhce", qk_i, v_new)
        state = state * gl_i[..., None, None] + jnp.einsum("bhcd,bhce->bhde", kd_i, v_new)
        return state, o

    xs = tuple(jnp.moveaxis(t, 2, 0) for t in (qk, q_dec, k_dec, u, w, g_last))
    state0 = jnp.zeros((B, H, Dk, Dv), jnp.float32)
    _, o = lax.scan(step, state0, xs)
    return o.transpose(1, 0, 3, 2, 4).reshape(B, S, H, Dv)


def l2norm(x):
    return x * lax.rsqrt(jnp.sum(x * x, -1, keepdims=True) + RMS_EPS)


def hybrid_layer(x, cos, sin, w_in, sinks, r_conv_w, r_conv_b, r_wa, r_ba, r_wx, r_bx, r_lam,
                 g_conv_w, g_a_log, g_dt_bias, g_norm_w, w_out, ln_g, ln_b):
    B, S, _ = x.shape
    proj = x @ w_in
    points = [int(p) for p in np.cumsum(IN_SIZES)[:-1]]
    aq, ak, av, az, rx, rz, gq, gk, gv, gz, gb, ga = jnp.split(proj, points, axis=-1)

    q = apply_rope(aq.reshape(B, S, A_HEADS, A_HEAD_DIM), cos, sin)
    k = apply_rope(ak.reshape(B, S, A_KV_HEADS, A_HEAD_DIM), cos, sin)
    v = av.reshape(B, S, A_KV_HEADS, A_HEAD_DIM)
    y_a = sliding_window_attention(q, k, v, sinks) * jax.nn.silu(az)

    xr = causal_depthwise_conv(rx, r_conv_w) + r_conv_b
    y_r = rg_lru(xr, r_wa, r_ba, r_wx, r_bx, r_lam) * jax.nn.silu(rz)

    qkv = jax.nn.silu(causal_depthwise_conv(jnp.concatenate([gq, gk, gv], axis=-1), g_conv_w))
    cq, ck, cv = jnp.split(qkv.astype(jnp.float32), 3, axis=-1)
    cq = l2norm(cq.reshape(B, S, G_HEADS, G_HEAD_DIM)) * (G_HEAD_DIM ** -0.5)
    ck = l2norm(ck.reshape(B, S, G_HEADS, G_HEAD_DIM))
    cv = cv.reshape(B, S, G_HEADS, G_HEAD_DIM)
    beta = jax.nn.sigmoid(gb.astype(jnp.float32))
    g = -jnp.exp(g_a_log.astype(jnp.float32)) * jax.nn.softplus(
        ga.astype(jnp.float32) + g_dt_bias.astype(jnp.float32))
    o = gated_delta_chunked(cq, ck, cv, g, beta)
    o = o * lax.rsqrt(jnp.mean(o * o, -1, keepdims=True) + RMS_EPS) * g_norm_w.astype(jnp.float32)
    y_g = o.reshape(B, S, G_WIDTH).astype(x.dtype) * jax.nn.silu(gz)

    y = jnp.concatenate([y_a, y_r, y_g], axis=-1) @ w_out
    return layer_norm(DEEPNORM_ALPHA * x + y, ln_g, ln_b)


def _fwd_setup_inputs(seed: int = 0) -> dict:
    key = jax.random.key(seed)
    ks = jax.random.split(key, 20)
    f32 = jnp.float32
    x = jax.random.normal(ks[0], (BATCH, SEQ, D_MODEL), f32)
    w_in = jax.random.normal(ks[1], (DEPTH, D_MODEL, N_IN), f32) * D_MODEL ** -0.5
    sinks = jax.random.normal(ks[2], (DEPTH, A_HEADS), f32)
    r_conv_w = jax.random.normal(ks[3], (DEPTH, CONV_WIDTH, R_WIDTH), f32) * CONV_WIDTH ** -0.5
    r_conv_b = jax.random.normal(ks[4], (DEPTH, R_WIDTH), f32) * 0.01
    r_wa = jax.random.normal(ks[5], (DEPTH, R_BLOCKS, R_BLOCK_DIM, R_BLOCK_DIM), f32) * R_BLOCK_DIM ** -0.5
    r_ba = jax.random.normal(ks[6], (DEPTH, R_WIDTH), f32) * 0.01
    r_wx = jax.random.normal(ks[7], (DEPTH, R_BLOCKS, R_BLOCK_DIM, R_BLOCK_DIM), f32) * R_BLOCK_DIM ** -0.5
    r_bx = jax.random.normal(ks[8], (DEPTH, R_WIDTH), f32) * 0.01
    a_c = jax.random.uniform(ks[9], (DEPTH, R_WIDTH), f32, minval=0.9, maxval=0.999)
    a0 = a_c ** (1.0 / R_C)
    r_lam = jnp.log(a0) - jnp.log1p(-a0)
    g_conv_w = jax.random.normal(ks[10], (DEPTH, CONV_WIDTH, 3 * G_WIDTH), f32) * CONV_WIDTH ** -0.5
    g_a_log = jnp.log(jax.random.uniform(ks[11], (DEPTH, G_HEADS), f32, minval=1.0, maxval=16.0))
    dt = jnp.exp(jax.random.uniform(ks[12], (DEPTH, G_HEADS), f32,
                                    minval=math.log(1e-3), maxval=math.log(1e-1)))
    g_dt_bias = dt + jnp.log(-jnp.expm1(-dt))
    g_norm_w = 1.0 + 0.01 * jax.random.normal(ks[13], (DEPTH, G_HEAD_DIM), f32)
    w_out = jax.random.normal(ks[14], (DEPTH, MIX_WIDTH, D_MODEL), f32) * (MIX_WIDTH ** -0.5) * DEEPNORM_BETA
    ln_g = 1.0 + 0.01 * jax.random.normal(ks[15], (DEPTH, D_MODEL), f32)
    ln_b = 0.01 * jax.random.normal(ks[16], (DEPTH, D_MODEL), f32)
    return {"x": x, "w_in": w_in, "sinks": sinks, "r_conv_w": r_conv_w, "r_conv_b": r_conv_b,
            "r_wa": r_wa, "r_ba": r_ba, "r_wx": r_wx, "r_bx": r_bx, "r_lam": r_lam,
            "g_conv_w": g_conv_w, "g_a_log": g_a_log, "g_dt_bias": g_dt_bias, "g_norm_w": g_norm_w,
            "w_out": w_out, "ln_g": ln_g, "ln_b": ln_b}


def _fwd_reference(x, w_in, sinks, r_conv_w, r_conv_b, r_wa, r_ba, r_wx, r_bx, r_lam,
              g_conv_w, g_a_log, g_dt_bias, g_norm_w, w_out, ln_g, ln_b):
    cos, sin = rope_tables(x.shape[1], A_HEAD_DIM)
    for l in range(DEPTH):
        x = hybrid_layer(x, cos, sin, w_in[l], sinks[l], r_conv_w[l], r_conv_b[l], r_wa[l], r_ba[l],
                         r_wx[l], r_bx[l], r_lam[l], g_conv_w[l], g_a_log[l], g_dt_bias[l],
                         g_norm_w[l], w_out[l], ln_g[l], ln_b[l])
    return x


import jax as _jax
import jax.numpy as _jnp

TWIN_FORMAT = 'train_step'
FWD_PARAMS = ['x', 'w_in', 'sinks', 'r_conv_w', 'r_conv_b', 'r_wa', 'r_ba', 'r_wx', 'r_bx', 'r_lam', 'g_conv_w', 'g_a_log', 'g_dt_bias', 'g_norm_w', 'w_out', 'ln_g', 'ln_b']
TWIN_WEIGHTS = ['w_in', 'sinks', 'r_conv_w', 'r_conv_b', 'r_wa', 'r_ba', 'r_wx', 'r_bx', 'r_lam', 'g_conv_w', 'g_a_log', 'g_dt_bias', 'g_norm_w', 'w_out', 'ln_g', 'ln_b']
TWIN_DIFF_INPUT = 'x'
TWIN_INPUTS = ['x', 'w_in', 'sinks', 'r_conv_w', 'r_conv_b', 'r_wa', 'r_ba', 'r_wx', 'r_bx', 'r_lam', 'g_conv_w', 'g_a_log', 'g_dt_bias', 'g_norm_w', 'w_out', 'ln_g', 'ln_b', 'loss_target', 'm_w_in', 'm_sinks', 'm_r_conv_w', 'm_r_conv_b', 'm_r_wa', 'm_r_ba', 'm_r_wx', 'm_r_bx', 'm_r_lam', 'm_g_conv_w', 'm_g_a_log', 'm_g_dt_bias', 'm_g_norm_w', 'm_w_out', 'm_ln_g', 'm_ln_b', 'v_w_in', 'v_sinks', 'v_r_conv_w', 'v_r_conv_b', 'v_r_wa', 'v_r_ba', 'v_r_wx', 'v_r_bx', 'v_r_lam', 'v_g_conv_w', 'v_g_a_log', 'v_g_dt_bias', 'v_g_norm_w', 'v_w_out', 'v_ln_g', 'v_ln_b']
TWIN_OUTPUTS = ['loss', 'grad_x', 'grad_w_in', 'grad_sinks', 'grad_r_conv_w', 'grad_r_conv_b', 'grad_r_wa', 'grad_r_ba', 'grad_r_wx', 'grad_r_bx', 'grad_r_lam', 'grad_g_conv_w', 'grad_g_a_log', 'grad_g_dt_bias', 'grad_g_norm_w', 'grad_w_out', 'grad_ln_g', 'grad_ln_b', 'delta_w_in', 'delta_sinks', 'delta_r_conv_w', 'delta_r_conv_b', 'delta_r_wa', 'delta_r_ba', 'delta_r_wx', 'delta_r_bx', 'delta_r_lam', 'delta_g_conv_w', 'delta_g_a_log', 'delta_g_dt_bias', 'delta_g_norm_w', 'delta_w_out', 'delta_ln_g', 'delta_ln_b', 'new_m_w_in', 'new_m_sinks', 'new_m_r_conv_w', 'new_m_r_conv_b', 'new_m_r_wa', 'new_m_r_ba', 'new_m_r_wx', 'new_m_r_bx', 'new_m_r_lam', 'new_m_g_conv_w', 'new_m_g_a_log', 'new_m_g_dt_bias', 'new_m_g_norm_w', 'new_m_w_out', 'new_m_ln_g', 'new_m_ln_b', 'new_v_w_in', 'new_v_sinks', 'new_v_r_conv_w', 'new_v_r_conv_b', 'new_v_r_wa', 'new_v_r_ba', 'new_v_r_wx', 'new_v_r_bx', 'new_v_r_lam', 'new_v_g_conv_w', 'new_v_g_a_log', 'new_v_g_dt_bias', 'new_v_g_norm_w', 'new_v_w_out', 'new_v_ln_g', 'new_v_ln_b']
TWIN_LEAF_KINDS = {'loss': 'loss', 'grad_x': 'grad_x', 'grad_w_in': 'grad_w', 'grad_sinks': 'grad_w', 'grad_r_conv_w': 'grad_w', 'grad_r_conv_b': 'grad_w', 'grad_r_wa': 'grad_w', 'grad_r_ba': 'grad_w', 'grad_r_wx': 'grad_w', 'grad_r_bx': 'grad_w', 'grad_r_lam': 'grad_w', 'grad_g_conv_w': 'grad_w', 'grad_g_a_log': 'grad_w', 'grad_g_dt_bias': 'grad_w', 'grad_g_norm_w': 'grad_w', 'grad_w_out': 'grad_w', 'grad_ln_g': 'grad_w', 'grad_ln_b': 'grad_w', 'delta_w_in': 'delta_w', 'delta_sinks': 'delta_w', 'delta_r_conv_w': 'delta_w', 'delta_r_conv_b': 'delta_w', 'delta_r_wa': 'delta_w', 'delta_r_ba': 'delta_w', 'delta_r_wx': 'delta_w', 'delta_r_bx': 'delta_w', 'delta_r_lam': 'delta_w', 'delta_g_conv_w': 'delta_w', 'delta_g_a_log': 'delta_w', 'delta_g_dt_bias': 'delta_w', 'delta_g_norm_w': 'delta_w', 'delta_w_out': 'delta_w', 'delta_ln_g': 'delta_w', 'delta_ln_b': 'delta_w', 'new_m_w_in': 'new_m', 'new_m_sinks': 'new_m', 'new_m_r_conv_w': 'new_m', 'new_m_r_conv_b': 'new_m', 'new_m_r_wa': 'new_m', 'new_m_r_ba': 'new_m', 'new_m_r_wx': 'new_m', 'new_m_r_bx': 'new_m', 'new_m_r_lam': 'new_m', 'new_m_g_conv_w': 'new_m', 'new_m_g_a_log': 'new_m', 'new_m_g_dt_bias': 'new_m', 'new_m_g_norm_w': 'new_m', 'new_m_w_out': 'new_m', 'new_m_ln_g': 'new_m', 'new_m_ln_b': 'new_m', 'new_v_w_in': 'new_v', 'new_v_sinks': 'new_v', 'new_v_r_conv_w': 'new_v', 'new_v_r_conv_b': 'new_v', 'new_v_r_wa': 'new_v', 'new_v_r_ba': 'new_v', 'new_v_r_wx': 'new_v', 'new_v_r_bx': 'new_v', 'new_v_r_lam': 'new_v', 'new_v_g_conv_w': 'new_v', 'new_v_g_a_log': 'new_v', 'new_v_g_dt_bias': 'new_v', 'new_v_g_norm_w': 'new_v', 'new_v_w_out': 'new_v', 'new_v_ln_g': 'new_v', 'new_v_ln_b': 'new_v'}


def _forward(args):
    return _fwd_reference(*[args[k] for k in FWD_PARAMS])


def _output_shape():
    def fwd():
        inp = _fwd_setup_inputs(0)
        return _fwd_reference(*[inp[k] for k in FWD_PARAMS])
    out = _jax.eval_shape(fwd)
    return out.shape, out.dtype

N_MICROBATCH = 1
ADAM_LR = 0.001
ADAM_B1 = 0.9
ADAM_B2 = 0.999
ADAM_EPS = 1e-08
ADAM_WD = 0.01
ADAM_STEP = 10
PER_EXAMPLE_BATCH_AXIS = {'x': 0, 'loss_target': 0}
SHARED_INPUTS = []
_WEIGHT_DTYPES = {'w_in': _jnp.float32, 'sinks': _jnp.float32, 'r_conv_w': _jnp.float32, 'r_conv_b': _jnp.float32, 'r_wa': _jnp.float32, 'r_ba': _jnp.float32, 'r_wx': _jnp.float32, 'r_bx': _jnp.float32, 'r_lam': _jnp.float32, 'g_conv_w': _jnp.float32, 'g_a_log': _jnp.float32, 'g_dt_bias': _jnp.float32, 'g_norm_w': _jnp.float32, 'w_out': _jnp.float32, 'ln_g': _jnp.float32, 'ln_b': _jnp.float32}
MOMENT_SCALE = {'w_in': 1.714941e-02, 'sinks': 3.827522e-03, 'r_conv_w': 1.897005e-02, 'r_conv_b': 2.459237e-01, 'r_wa': 6.610676e-03, 'r_ba': 5.598429e-03, 'r_wx': 1.183145e-02, 'r_bx': 6.459921e-03, 'r_lam': 1.020981e-02, 'g_conv_w': 1.967762e-02, 'g_a_log': 1.404077e-01, 'g_dt_bias': 1.296809e-01, 'g_norm_w': 5.230345e-02, 'w_out': 3.783241e-02, 'ln_g': 2.259868e+01, 'ln_b': 7.865197e-01}


def _to_microbatches(a, axis):
    t = _jnp.moveaxis(a, axis, 0)
    t = t.reshape((N_MICROBATCH, t.shape[0] // N_MICROBATCH) + t.shape[1:])
    return _jnp.moveaxis(t, 1, axis + 1)


def setup_inputs(seed: int = 0) -> dict:
    inp = _fwd_setup_inputs(seed)
    key = _jax.random.fold_in(_jax.random.key(seed), 7919)
    shape, _ = _output_shape()
    out = dict(inp)
    out["loss_target"] = _jax.random.normal(_jax.random.fold_in(key, 0), shape, _jnp.float32)
    for i, name in enumerate(TWIN_WEIGHTS):
        w = inp[name].astype(_jnp.float32)
        if MOMENT_SCALE is None:
            s = _jnp.sqrt(_jnp.mean(_jnp.square(w)) + 1e-30)
        else:
            s = MOMENT_SCALE[name]
        km, kv = _jax.random.split(_jax.random.fold_in(key, i + 1))
        out[name] = w
        out["m_" + name] = s * _jax.random.normal(km, w.shape, _jnp.float32)
        out["v_" + name] = (s * s) * _jax.random.uniform(kv, w.shape, _jnp.float32, 0.5, 1.5)
    if N_MICROBATCH > 1:
        for name, axis in PER_EXAMPLE_BATCH_AXIS.items():
            out[name] = _to_microbatches(out[name], axis)
    return {'x': out['x'], 'w_in': out['w_in'], 'sinks': out['sinks'], 'r_conv_w': out['r_conv_w'], 'r_conv_b': out['r_conv_b'], 'r_wa': out['r_wa'], 'r_ba': out['r_ba'], 'r_wx': out['r_wx'], 'r_bx': out['r_bx'], 'r_lam': out['r_lam'], 'g_conv_w': out['g_conv_w'], 'g_a_log': out['g_a_log'], 'g_dt_bias': out['g_dt_bias'], 'g_norm_w': out['g_norm_w'], 'w_out': out['w_out'], 'ln_g': out['ln_g'], 'ln_b': out['ln_b'], 'loss_target': out['loss_target'], 'm_w_in': out['m_w_in'], 'm_sinks': out['m_sinks'], 'm_r_conv_w': out['m_r_conv_w'], 'm_r_conv_b': out['m_r_conv_b'], 'm_r_wa': out['m_r_wa'], 'm_r_ba': out['m_r_ba'], 'm_r_wx': out['m_r_wx'], 'm_r_bx': out['m_r_bx'], 'm_r_lam': out['m_r_lam'], 'm_g_conv_w': out['m_g_conv_w'], 'm_g_a_log': out['m_g_a_log'], 'm_g_dt_bias': out['m_g_dt_bias'], 'm_g_norm_w': out['m_g_norm_w'], 'm_w_out': out['m_w_out'], 'm_ln_g': out['m_ln_g'], 'm_ln_b': out['m_ln_b'], 'v_w_in': out['v_w_in'], 'v_sinks': out['v_sinks'], 'v_r_conv_w': out['v_r_conv_w'], 'v_r_conv_b': out['v_r_conv_b'], 'v_r_wa': out['v_r_wa'], 'v_r_ba': out['v_r_ba'], 'v_r_wx': out['v_r_wx'], 'v_r_bx': out['v_r_bx'], 'v_r_lam': out['v_r_lam'], 'v_g_conv_w': out['v_g_conv_w'], 'v_g_a_log': out['v_g_a_log'], 'v_g_dt_bias': out['v_g_dt_bias'], 'v_g_norm_w': out['v_g_norm_w'], 'v_w_out': out['v_w_out'], 'v_ln_g': out['v_ln_g'], 'v_ln_b': out['v_ln_b']}


def _loss(weights, diff, rest, loss_target):
    with _jax.named_scope("forward"):
        args = {**rest, TWIN_DIFF_INPUT: diff, **{k: w.astype(_WEIGHT_DTYPES[k]) for k, w in weights.items()}}
        y = _forward(args)
    with _jax.named_scope("loss_head"):
        err = _jnp.square(y.astype(_jnp.float32) - loss_target)
        return 0.5 * _jnp.sum(_jnp.mean(err, axis=-1)) if err.ndim else 0.5 * err


def _adamw(w, g, m, v):
    m = ADAM_B1 * m + (1.0 - ADAM_B1) * g
    v = ADAM_B2 * v + (1.0 - ADAM_B2) * _jnp.square(g)
    m_hat = m / (1.0 - ADAM_B1 ** ADAM_STEP)
    v_hat = v / (1.0 - ADAM_B2 ** ADAM_STEP)
    delta = -ADAM_LR * (m_hat / (_jnp.sqrt(v_hat) + ADAM_EPS) + ADAM_WD * w)
    return delta, m, v


def reference(x, w_in, sinks, r_conv_w, r_conv_b, r_wa, r_ba, r_wx, r_bx, r_lam, g_conv_w, g_a_log, g_dt_bias, g_norm_w, w_out, ln_g, ln_b, loss_target, m_w_in, m_sinks, m_r_conv_w, m_r_conv_b, m_r_wa, m_r_ba, m_r_wx, m_r_bx, m_r_lam, m_g_conv_w, m_g_a_log, m_g_dt_bias, m_g_norm_w, m_w_out, m_ln_g, m_ln_b, v_w_in, v_sinks, v_r_conv_w, v_r_conv_b, v_r_wa, v_r_ba, v_r_wx, v_r_bx, v_r_lam, v_g_conv_w, v_g_a_log, v_g_dt_bias, v_g_norm_w, v_w_out, v_ln_g, v_ln_b):
    given = dict(x=x, w_in=w_in, sinks=sinks, r_conv_w=r_conv_w, r_conv_b=r_conv_b, r_wa=r_wa, r_ba=r_ba, r_wx=r_wx, r_bx=r_bx, r_lam=r_lam, g_conv_w=g_conv_w, g_a_log=g_a_log, g_dt_bias=g_dt_bias, g_norm_w=g_norm_w, w_out=w_out, ln_g=ln_g, ln_b=ln_b, loss_target=loss_target, m_w_in=m_w_in, m_sinks=m_sinks, m_r_conv_w=m_r_conv_w, m_r_conv_b=m_r_conv_b, m_r_wa=m_r_wa, m_r_ba=m_r_ba, m_r_wx=m_r_wx, m_r_bx=m_r_bx, m_r_lam=m_r_lam, m_g_conv_w=m_g_conv_w, m_g_a_log=m_g_a_log, m_g_dt_bias=m_g_dt_bias, m_g_norm_w=m_g_norm_w, m_w_out=m_w_out, m_ln_g=m_ln_g, m_ln_b=m_ln_b, v_w_in=v_w_in, v_sinks=v_sinks, v_r_conv_w=v_r_conv_w, v_r_conv_b=v_r_conv_b, v_r_wa=v_r_wa, v_r_ba=v_r_ba, v_r_wx=v_r_wx, v_r_bx=v_r_bx, v_r_lam=v_r_lam, v_g_conv_w=v_g_conv_w, v_g_a_log=v_g_a_log, v_g_dt_bias=v_g_dt_bias, v_g_norm_w=v_g_norm_w, v_w_out=v_w_out, v_ln_g=v_ln_g, v_ln_b=v_ln_b)
    weights = {n: given[n] for n in TWIN_WEIGHTS}
    shared = {n: given[n] for n in SHARED_INPUTS}
    per_example = {n: given[n] for n in ['x']}
    grad_fn = _jax.value_and_grad(_loss, argnums=(0, 1))

    def one_microbatch(ex, loss_target):
        ex = dict(ex)
        diff = ex.pop(TWIN_DIFF_INPUT)
        return grad_fn(weights, diff, {**shared, **ex}, loss_target)

    if N_MICROBATCH == 1:
        loss, (grad_w, grad_x) = one_microbatch(per_example, given["loss_target"])
    else:
        def body(carry, xs):
            loss_sum, grad_sum = carry
            l_k, (gw_k, gx_k) = one_microbatch(xs[0], xs[1])
            with _jax.named_scope("update"):
                return (loss_sum + l_k, _jax.tree.map(_jnp.add, grad_sum, gw_k)), gx_k

        init = (_jnp.zeros((), _jnp.float32), _jax.tree.map(_jnp.zeros_like, weights))
        (loss, grad_w), grad_x = _jax.lax.scan(body, init, (per_example, given["loss_target"]))
    with _jax.named_scope("update"):
        delta_w, new_m, new_v = {}, {}, {}
        for n in TWIN_WEIGHTS:
            delta_w[n], new_m[n], new_v[n] = _adamw(weights[n], grad_w[n], given["m_" + n], given["v_" + n])
    return (loss, grad_x, *[grad_w[n] for n in TWIN_WEIGHTS], *[delta_w[n] for n in TWIN_WEIGHTS],
            *[new_m[n] for n in TWIN_WEIGHTS], *[new_v[n] for n in TWIN_WEIGHTS])
```

```python
import functools
import math

import jax
import jax.numpy as jnp
import numpy as np
from jax import lax
from jax.experimental import pallas as pl
from jax.experimental.pallas import tpu as pltpu

F32 = jnp.float32
BF16 = jnp.bfloat16
MESH = pl.DeviceIdType.MESH

DEPTH = 2
A_HEADS, A_KV_HEADS, A_HEAD_DIM = 8, 2, 64
A_WIDTH, A_KV_WIDTH = 512, 128
WINDOW = 128
ROPE_THETA = 10000.0
R_WIDTH, R_BLOCKS, R_BLOCK_DIM, R_C = 1024, 8, 128, 8.0
CONV_WIDTH = 4
G_HEADS, G_HEAD_DIM, G_WIDTH, G_CHUNK = 4, 128, 512, 64
MIX_WIDTH = 2048
IN_SIZES = (512, 128, 128, 512, 1024, 1024, 512, 512, 512, 512, 4, 4)
N_IN = 5384
DEEPNORM_ALPHA = (2 * DEPTH) ** 0.25
LN_EPS = 1e-5
RMS_EPS = 1e-6
ADAM_LR, ADAM_B1, ADAM_B2, ADAM_EPS, ADAM_WD, ADAM_STEP = 0.001, 0.9, 0.999, 1e-08, 0.01, 10

NP = 5632
OFF_RX, OFF_RZ, OFF_AQ, OFF_AZ, OFF_GQKV, OFF_GZ, OFF_AK, OFF_AV, OFF_GBA = (
    0, 1024, 2048, 2560, 3072, 4608, 5120, 5248, 5376)
_ORIG_OFF = np.concatenate([[0], np.cumsum(IN_SIZES)])[:-1]
_PIECES = ((4, OFF_RX), (5, OFF_RZ), (0, OFF_AQ), (3, OFF_AZ), (6, OFF_GQKV), (7, OFF_GQKV + 512),
           (8, OFF_GQKV + 1024), (9, OFF_GZ), (1, OFF_AK), (2, OFF_AV), (10, OFF_GBA), (11, OFF_GBA + 4))
MIX_R, MIX_A, MIX_G = 0, 1024, 1536
VMEM_LIMIT = 56 * 1024 * 1024


def _pcall(body, **kw):
    return pl.pallas_call(body, **kw)


def _cp(sem, limit=VMEM_LIMIT):
    return pltpu.CompilerParams(dimension_semantics=sem, vmem_limit_bytes=limit)


def _sigmoid(x):
    return 1.0 / (1.0 + jnp.exp(-x))


def _silu(x):
    return x * _sigmoid(x)


def _dsilu(x):
    s = _sigmoid(x)
    return s * (1.0 + x * (1.0 - s))


def _log1p(x):
    u = 1.0 + x
    d = jnp.where(u == 1.0, 1.0, u - 1.0)
    return jnp.where(u == 1.0, x, jnp.log(u) * (x / d))


def _softplus(x):
    return jnp.maximum(x, 0.0) + _log1p(jnp.exp(-jnp.abs(x)))


def _expm1(x):
    u = jnp.exp(x)
    lu = jnp.log(u)
    safe = jnp.where((u == 1.0) | (lu == 0.0), 1.0, lu)
    r = (u - 1.0) * (x / safe)
    r = jnp.where(u == 1.0, x, r)
    return jnp.where(u - 1.0 == -1.0, -1.0, r)


def _nn(a, b):
    return lax.dot_general(a, b, (((1,), (0,)), ((), ())), preferred_element_type=F32)


def _nt(a, b):
    return lax.dot_general(a, b, (((1,), (1,)), ((), ())), preferred_element_type=F32)


def _tn(a, b):
    return lax.dot_general(a, b, (((0,), (0,)), ((), ())), preferred_element_type=F32)


def _b(x):
    return x.astype(BF16)


def _split3(x):
    hi = x.astype(BF16)
    r1 = x - hi.astype(F32)
    mid = r1.astype(BF16)
    lo = (r1 - mid.astype(F32)).astype(BF16)
    return hi, mid, lo


def _dot3(f, a, b):
    ah, am, _ = _split3(a)
    bh, bm, _ = _split3(b)
    return f(ah, bh) + (f(ah, bm) + f(am, bh))


def _dot_exact_lhs(f, a_bf16, b):
    bh, bm, bl = _split3(b)
    return f(a_bf16, bh) + (f(a_bf16, bm) + f(a_bf16, bl))


def _rot(x):
    w = x.shape[-1]
    lane = lax.broadcasted_iota(jnp.int32, (1, w), 1)
    return jnp.where((lane & 63) < 32, pltpu.roll(x, w - 32, 1), pltpu.roll(x, 32, 1))


def _conv_taps(ext, n):
    return [pltpu.roll(ext, 3 - k, 0)[8:8 + n] if k < 3 else ext[8:8 + n] for k in range(CONV_WIDTH)]


def _conv_taps_t(ext, n):
    m = ext.shape[0]
    return [pltpu.roll(ext, m - (3 - k), 0)[0:n] if k < 3 else ext[0:n] for k in range(CONV_WIDTH)]


def _scan_lin(a, b, reverse):
    n = a.shape[0]
    row = lax.broadcasted_iota(jnp.int32, (n, 1), 0)
    s = 1
    while s < n:
        if reverse:
            a_sh = pltpu.roll(a, n - s, 0)
            b_sh = pltpu.roll(b, n - s, 0)
            ok = row < (n - s)
        else:
            a_sh = pltpu.roll(a, s, 0)
            b_sh = pltpu.roll(b, s, 0)
            ok = row >= s
        b = jnp.where(ok, a * b_sh + b, b)
        a = jnp.where(ok, a * a_sh, a)
        s *= 2
    return a, b


def _matmul(a, b, *, ta, tb, tm, tn, tk, out_dtype, name, extra=None, alpha=0.0):
    if ta:
        K, M = a.shape
    else:
        M, K = a.shape
    if tb:
        N, K2 = b.shape
    else:
        K2, N = b.shape
    assert K == K2 and M % tm == 0 and N % tn == 0 and K % tk == 0, (a.shape, b.shape, tm, tn, tk)
    nk = K // tk
    ca = 0 if ta else 1
    cb = 1 if tb else 0
    has_extra = extra is not None

    def body(*refs):
        if has_extra:
            a_ref, b_ref, e_ref, o_ref, acc = refs
        else:
            a_ref, b_ref, o_ref, acc = refs
            e_ref = None
        k = pl.program_id(2)

        @pl.when(k == 0)
        def _():
            acc[...] = jnp.zeros_like(acc)

        acc[...] += lax.dot_general(a_ref[...], b_ref[...], (((ca,), (cb,)), ((), ())),
                                    preferred_element_type=F32)

        @pl.when(k == nk - 1)
        def _():
            r = acc[...]
            if e_ref is not None:
                r = r + alpha * e_ref[...]
            o_ref[...] = r.astype(o_ref.dtype)

    a_spec = (pl.BlockSpec((tk, tm), lambda i, j, k: (k, i)) if ta
              else pl.BlockSpec((tm, tk), lambda i, j, k: (i, k)))
    b_spec = (pl.BlockSpec((tn, tk), lambda i, j, k: (j, k)) if tb
              else pl.BlockSpec((tk, tn), lambda i, j, k: (k, j)))
    o_spec = pl.BlockSpec((tm, tn), lambda i, j, k: (i, j))
    in_specs = [a_spec, b_spec] + ([o_spec] if has_extra else [])
    args = (a, b) + ((extra,) if has_extra else ())
    return _pcall(
        body, name=name, grid=(M // tm, N // tn, nk),
        in_specs=in_specs, out_specs=o_spec,
        out_shape=jax.ShapeDtypeStruct((M, N), out_dtype),
        scratch_shapes=[pltpu.VMEM((tm, tn), F32)],
        compiler_params=_cp(("parallel", "parallel", "arbitrary")),
    )(*args)


def _outproj(ymix, wo, x, *, tm, name):
    S_, D = x.shape

    def body(y_ref, w_ref, x_ref, z_ref):
        z_ref[...] = DEEPNORM_ALPHA * x_ref[...] + _nn(y_ref[...], w_ref[...])

    return _pcall(
        body, name=name, grid=(S_ // tm,),
        in_specs=[pl.BlockSpec((tm, MIX_WIDTH), lambda i: (i, 0)),
                  pl.BlockSpec((MIX_WIDTH, D), lambda i: (0, 0)),
                  pl.BlockSpec((tm, D), lambda i: (i, 0))],
        out_specs=pl.BlockSpec((tm, D), lambda i: (i, 0)),
        out_shape=jax.ShapeDtypeStruct((S_, D), F32),
        compiler_params=_cp(("parallel",)),
    )(ymix, wo, x)


def _ln_stats(z):
    mu = jnp.mean(z, -1, keepdims=True)
    zc = z - mu
    var = jnp.mean(zc * zc, -1, keepdims=True)
    rstd = lax.rsqrt(var + LN_EPS)
    return zc * rstd, rstd


def _ln_fwd(z, g, b, *, tm, name):
    S_, D = z.shape

    def body(z_ref, g_ref, b_ref, y_ref, yb_ref):
        xh, _ = _ln_stats(z_ref[...])
        y = xh * g_ref[...] + b_ref[...]
        y_ref[...] = y
        yb_ref[...] = y.astype(BF16)

    row = pl.BlockSpec((tm, D), lambda i: (i, 0))
    vec = pl.BlockSpec((1, D), lambda i: (0, 0))
    return _pcall(
        body, name=name, grid=(S_ // tm,), in_specs=[row, vec, vec], out_specs=[row, row],
        out_shape=[jax.ShapeDtypeStruct((S_, D), F32), jax.ShapeDtypeStruct((S_, D), BF16)],
        compiler_params=_cp(("parallel",)),
    )(z, g.reshape(1, D), b.reshape(1, D))


def _ln_bwd(z, g, b, other, *, from_target, tm, name):
    S_, D = z.shape

    def body(z_ref, g_ref, b_ref, o_ref, dz_ref, dzb_ref, dg_ref, db_ref, loss_ref):
        i = pl.program_id(0)

        @pl.when(i == 0)
        def _():
            dg_ref[...] = jnp.zeros_like(dg_ref)
            db_ref[...] = jnp.zeros_like(db_ref)
            loss_ref[...] = jnp.zeros_like(loss_ref)

        xh, rstd = _ln_stats(z_ref[...])
        gam = g_ref[...]
        if from_target:
            err = xh * gam + b_ref[...] - o_ref[...]
            per_tok = jnp.mean(err * err, -1, keepdims=True)
            loss_ref[...] += 0.5 * jnp.sum(per_tok, 0, keepdims=True)
            dy = err * (1.0 / D)
        else:
            dy = o_ref[...]
        dxh = dy * gam
        m1 = jnp.mean(dxh, -1, keepdims=True)
        m2 = jnp.mean(dxh * xh, -1, keepdims=True)
        dz = rstd * (dxh - m1 - xh * m2)
        dz_ref[...] = dz
        dzb_ref[...] = dz.astype(BF16)
        dg_ref[...] += jnp.sum(dy * xh, 0, keepdims=True)
        db_ref[...] += jnp.sum(dy, 0, keepdims=True)

    row = pl.BlockSpec((tm, D), lambda i: (i, 0))
    vec = pl.BlockSpec((1, D), lambda i: (0, 0))
    one = pl.BlockSpec((1, 1), lambda i: (0, 0))
    return _pcall(
        body, name=name, grid=(S_ // tm,), in_specs=[row, vec, vec, row],
        out_specs=[row, row, vec, vec, one],
        out_shape=[jax.ShapeDtypeStruct((S_, D), F32), jax.ShapeDtypeStruct((S_, D), BF16),
                   jax.ShapeDtypeStruct((1, D), F32), jax.ShapeDtypeStruct((1, D), F32),
                   jax.ShapeDtypeStruct((1, 1), F32)],
        compiler_params=_cp(("arbitrary",)),
    )(z, g.reshape(1, D), b.reshape(1, D), other)


def _attn_common(i, T, b, h, qr, kd, vd, sk_ref):
    lane = lax.broadcasted_iota(jnp.int32, (1, 128), 1)
    lof = (lane < 64).astype(F32)
    hif = 1.0 - lof
    r0 = b * 128
    ri = lax.broadcasted_iota(jnp.int32, (512, 256), 0)
    cj = lax.broadcasted_iota(jnp.int32, (512, 256), 1)
    diff = (ri & 127) - cj + 128
    valid = (diff >= 0) & (diff < WINDOW) & ((i * T + r0 - 128 + cj) >= 0)
    grp = lax.broadcasted_iota(jnp.int32, (512, 1), 0) >> 7
    skv = jnp.zeros((512, 1), F32)
    for g in range(4):
        skv = jnp.where(grp == g, sk_ref[h * 4 + g], skv)
    pairs = [qr[r0:r0 + 128, h * 256 + p * 128:h * 256 + (p + 1) * 128] for p in (0, 1)]
    qs = _b(jnp.concatenate([pairs[0] * lof, pairs[0] * hif, pairs[1] * lof, pairs[1] * hif], 0))
    k2 = kd[h][r0:r0 + 256]
    v2 = vd[h][r0:r0 + 256]
    s = jnp.where(valid, _nt(qs, k2) * (A_HEAD_DIM ** -0.5), -jnp.inf)
    m = jnp.maximum(jnp.max(s, 1, keepdims=True), skv)
    p = jnp.exp(s - m)
    esk = jnp.exp(skv - m)
    rz = 1.0 / (jnp.sum(p, 1, keepdims=True) + esk)
    prob = p * rz
    o4 = _nn(_b(prob), v2)
    return lof, hif, qs, k2, v2, prob, esk * rz, o4


def _attn_prep(T, q_ref, k_ref, v_ref, c_ref, s_ref, kprev, vprev):
    C = c_ref[...]
    Sg = s_ref[...]
    C4 = jnp.concatenate([C] * 4, 1)
    S4 = jnp.concatenate([Sg] * 4, 1)
    q = q_ref[...]
    qr = q * C4 + _rot(q) * S4
    k = k_ref[...]
    kr = k * C + _rot(k) * Sg
    v = v_ref[...]
    kext = jnp.concatenate([kprev[...], kr], 0)
    vext = jnp.concatenate([vprev[...], v], 0)
    kprev[...] = kr[T - 128:]
    vprev[...] = v[T - 128:]
    lo = lax.broadcasted_iota(jnp.int32, (1, 128), 1) < 64
    kroll = pltpu.roll(kext, 64, 1)
    vroll = pltpu.roll(vext, 64, 1)
    kd = [_b(jnp.where(lo, kext, kroll)), _b(jnp.where(lo, kroll, kext))]
    vd = [_b(jnp.where(lo, vext, vroll)), _b(jnp.where(lo, vroll, vext))]
    return C, Sg, C4, S4, qr, kd, vd


def _attn_specs(T):
    return [pl.BlockSpec(memory_space=pltpu.SMEM),
            pl.BlockSpec((T, 512), lambda i: (i, OFF_AQ // 512)),
            pl.BlockSpec((T, 512), lambda i: (i, OFF_AZ // 512)),
            pl.BlockSpec((T, 128), lambda i: (i, OFF_AK // 128)),
            pl.BlockSpec((T, 128), lambda i: (i, OFF_AV // 128)),
            pl.BlockSpec((T, 128), lambda i: (i, 0)),
            pl.BlockSpec((T, 128), lambda i: (i, 0))]


def _attn_fwd(proj, rope_c, rope_s, sinks, *, T, name):
    S_ = proj.shape[0]
    nb = T // 128

    def body(sk_ref, q_ref, z_ref, k_ref, v_ref, c_ref, s_ref, y_ref, kprev, vprev):
        i = pl.program_id(0)

        @pl.when(i == 0)
        def _():
            kprev[...] = jnp.zeros_like(kprev)
            vprev[...] = jnp.zeros_like(vprev)

        _, _, _, _, qr, kd, vd = _attn_prep(T, q_ref, k_ref, v_ref, c_ref, s_ref, kprev, vprev)
        for b in range(nb):
            r0 = b * 128
            for h in range(2):
                lof, hif, _, _, _, _, _, o4 = _attn_common(i, T, b, h, qr, kd, vd, sk_ref)
                for p in range(2):
                    cs = slice(h * 256 + p * 128, h * 256 + (p + 1) * 128)
                    o = o4[2 * p * 128:(2 * p + 1) * 128] * lof + o4[(2 * p + 1) * 128:(2 * p + 2) * 128] * hif
                    y_ref[r0:r0 + 128, cs] = (o * _silu(z_ref[r0:r0 + 128, cs])).astype(BF16)

    return _pcall(
        body, name=name, grid=(S_ // T,), in_specs=_attn_specs(T),
        out_specs=pl.BlockSpec((T, 512), lambda i: (i, 0)),
        out_shape=jax.ShapeDtypeStruct((S_, 512), BF16),
        scratch_shapes=[pltpu.VMEM((128, 128), F32), pltpu.VMEM((128, 128), F32)],
        compiler_params=_cp(("arbitrary",)),
    )(sinks, proj, proj, proj, proj, rope_c, rope_s)


def _attn_bwd(proj, rope_c, rope_s, sinks, dymix, *, T, name):
    S_ = proj.shape[0]
    nb = T // 128
    nt = S_ // T

    def body(sk_ref, q_ref, z_ref, k_ref, v_ref, c_ref, s_ref, dy_ref,
             dq_ref, dz_ref, dk_ref, dv_ref, dkt_ref, dvt_ref, dsk_ref,
             kprev, vprev, cprev, sprev, dkacc, dvacc, dqacc):
        i = pl.program_id(0)

        @pl.when(i == 0)
        def _():
            kprev[...] = jnp.zeros_like(kprev)
            vprev[...] = jnp.zeros_like(vprev)
            cprev[...] = jnp.zeros_like(cprev)
            sprev[...] = jnp.zeros_like(sprev)
            dkacc[...] = jnp.zeros_like(dkacc)
            dvacc[...] = jnp.zeros_like(dvacc)
            dsk_ref[...] = jnp.zeros_like(dsk_ref)

        @pl.when(i > 0)
        def _():
            dkacc[0:128, :] = dkacc[T:T + 128, :]
            dvacc[0:128, :] = dvacc[T:T + 128, :]
            dkacc[128:, :] = jnp.zeros((T, 128), F32)
            dvacc[128:, :] = jnp.zeros((T, 128), F32)

        C, Sg, C4, S4, qr, kd, vd = _attn_prep(T, q_ref, k_ref, v_ref, c_ref, s_ref, kprev, vprev)
        lane = lax.broadcasted_iota(jnp.int32, (1, 128), 1)
        for b in range(nb):
            r0 = b * 128
            for h in range(2):
                lof, hif, qs, k2, v2, prob, psink, o4 = _attn_common(i, T, b, h, qr, kd, vd, sk_ref)
                dos = []
                for p in range(2):
                    cs = slice(h * 256 + p * 128, h * 256 + (p + 1) * 128)
                    o = o4[2 * p * 128:(2 * p + 1) * 128] * lof + o4[(2 * p + 1) * 128:(2 * p + 2) * 128] * hif
                    zc = z_ref[r0:r0 + 128, cs]
                    dyc = dy_ref[r0:r0 + 128, cs]
                    dz_ref[r0:r0 + 128, cs] = (dyc * o * _dsilu(zc)).astype(BF16)
                    do = dyc * _silu(zc)
                    dos += [do * lof, do * hif]
                dos = jnp.concatenate(dos, 0)
                os_ = jnp.concatenate([o4[0:128] * lof, o4[128:256] * hif, o4[256:384] * lof, o4[384:512] * hif], 0)
                delta = jnp.sum(dos * os_, 1, keepdims=True)
                dosb = _b(dos)
                dp = _nt(dosb, v2)
                ds = prob * (dp - delta)
                dsv = -psink * delta
                for g in range(4):
                    sg = jnp.sum(dsv[g * 128:(g + 1) * 128], 0, keepdims=True)
                    hd = h * 4 + g
                    dsk_ref[hd:hd + 1, :] += jnp.broadcast_to(sg, (1, 128))
                dsb = _b(ds * (A_HEAD_DIM ** -0.5))
                dqs = _nn(dsb, k2)
                for p in range(2):
                    cs = slice(h * 256 + p * 128, h * 256 + (p + 1) * 128)
                    dqacc[r0:r0 + 128, cs] = (dqs[2 * p * 128:(2 * p + 1) * 128] * lof
                                              + dqs[(2 * p + 1) * 128:(2 * p + 2) * 128] * hif)
                dkdup = _tn(dsb, qs)
                dvdup = _tn(_b(prob), dosb)
                half = (lane < 64) if h == 0 else (lane >= 64)
                dkacc[r0:r0 + 256, :] += jnp.where(half, dkdup + pltpu.roll(dkdup, 64, 1), 0.0)
                dvacc[r0:r0 + 256, :] += jnp.where(half, dvdup + pltpu.roll(dvdup, 64, 1), 0.0)
        dqr = dqacc[...]
        dq_ref[...] = (dqr * C4 + _rot(dqr * S4)).astype(BF16)
        cext = jnp.concatenate([cprev[...], C], 0)
        sext = jnp.concatenate([sprev[...], Sg], 0)
        dke = dkacc[...]
        dkp = dke * cext + _rot(dke * sext)
        dk_ref[...] = dkp[0:T].astype(BF16)
        dkt_ref[...] = dkp[T:T + 128].astype(BF16)
        dve = dvacc[...]
        dv_ref[...] = dve[0:T].astype(BF16)
        dvt_ref[...] = dve[T:T + 128].astype(BF16)
        cprev[...] = C[T - 128:]
        sprev[...] = Sg[T - 128:]

    wide = pl.BlockSpec((T, 512), lambda i: (i, 0))
    nar = pl.BlockSpec((T, 128), lambda i: (i, 0))
    tail = pl.BlockSpec((128, 128), lambda i: (0, 0))
    return _pcall(
        body, name=name, grid=(nt,),
        in_specs=_attn_specs(T) + [pl.BlockSpec((T, 512), lambda i: (i, MIX_A // 512))],
        out_specs=[wide, wide, nar, nar, tail, tail, pl.BlockSpec((8, 128), lambda i: (0, 0))],
        out_shape=[jax.ShapeDtypeStruct((S_, 512), BF16), jax.ShapeDtypeStruct((S_, 512), BF16),
                   jax.ShapeDtypeStruct((S_, 128), BF16), jax.ShapeDtypeStruct((S_, 128), BF16),
                   jax.ShapeDtypeStruct((128, 128), BF16), jax.ShapeDtypeStruct((128, 128), BF16),
                   jax.ShapeDtypeStruct((8, 128), F32)],
        scratch_shapes=[pltpu.VMEM((128, 128), F32)] * 4
        + [pltpu.VMEM((T + 128, 128), F32), pltpu.VMEM((T + 128, 128), F32), pltpu.VMEM((T, 512), F32)],
        compiler_params=_cp(("arbitrary",)),
    )(sinks, proj, proj, proj, proj, rope_c, rope_s, dymix)


def _rg_gates(xr, wa_ref, ba_ref, wx_ref, bx_ref, lam_ref):
    xb = _b(xr)
    pre_a = jnp.concatenate([_nn(xb[:, n * 128:(n + 1) * 128], wa_ref[n]) for n in range(R_BLOCKS)], 1) + ba_ref[...]
    pre_x = jnp.concatenate([_nn(xb[:, n * 128:(n + 1) * 128], wx_ref[n]) for n in range(R_BLOCKS)], 1) + bx_ref[...]
    r = _sigmoid(pre_a)
    ig = _sigmoid(pre_x)
    sp = _softplus(-lam_ref[...])
    log_a = -R_C * r * sp
    a = jnp.exp(log_a)
    mult = jnp.sqrt(-_expm1(2.0 * log_a))
    return xb, r, ig, sp, a, mult


def _rg_param_specs():
    C = R_WIDTH
    vec = pl.BlockSpec((1, C), lambda i: (0, 0))
    blk = pl.BlockSpec((R_BLOCKS, 128, 128), lambda i: (0, 0, 0))
    return [pl.BlockSpec((CONV_WIDTH, C), lambda i: (0, 0)), vec, blk, vec, blk, vec, vec]


def _rglru_fwd(proj, cw, cb, wa, ba, wx, bx, lam, *, T, name):
    S_ = proj.shape[0]
    C = R_WIDTH

    def body(rx_ref, rz_ref, cw_ref, cb_ref, wa_ref, ba_ref, wx_ref, bx_ref, lam_ref,
             h_ref, y_ref, halo, hcar):
        i = pl.program_id(0)

        @pl.when(i == 0)
        def _():
            halo[...] = jnp.zeros_like(halo)
            hcar[...] = jnp.zeros_like(hcar)

        rx = rx_ref[...]
        ext = jnp.concatenate([halo[...], rx], 0)
        halo[...] = rx[T - 8:]
        taps = _conv_taps(ext, T)
        xr = cb_ref[...] + sum(cw_ref[k:k + 1, :] * taps[k] for k in range(CONV_WIDTH))
        _, _, ig, _, a, mult = _rg_gates(xr, wa_ref, ba_ref, wx_ref, bx_ref, lam_ref)
        u = mult * (ig * xr)
        acum, hloc = _scan_lin(a, u, False)
        h = hloc + acum * hcar[0:1, :]
        hcar[...] = jnp.broadcast_to(h[T - 1:T, :], (8, C))
        h_ref[...] = h
        y_ref[...] = (h * _silu(rz_ref[...])).astype(BF16)

    row = pl.BlockSpec((T, C), lambda i: (i, 0))
    return _pcall(
        body, name=name, grid=(S_ // T,),
        in_specs=[pl.BlockSpec((T, C), lambda i: (i, OFF_RX // C)),
                  pl.BlockSpec((T, C), lambda i: (i, OFF_RZ // C))] + _rg_param_specs(),
        out_specs=[row, row],
        out_shape=[jax.ShapeDtypeStruct((S_, C), F32), jax.ShapeDtypeStruct((S_, C), BF16)],
        scratch_shapes=[pltpu.VMEM((8, C), F32), pltpu.VMEM((8, C), F32)],
        compiler_params=_cp(("arbitrary",)),
    )(proj, proj, cw, cb.reshape(1, C), _b(wa), ba.reshape(1, C), _b(wx), bx.reshape(1, C), lam.reshape(1, C))


def _rglru_bwd(proj, h, dymix, cw, cb, wa, ba, wx, bx, lam, *, T, name):
    S_ = proj.shape[0]
    C = R_WIDTH
    nt = S_ // T
    t8 = T // 8

    def body(rx_ref, rxp_ref, rz_ref, h_ref, hp_ref, dy_ref,
             cw_ref, cb_ref, wa_ref, ba_ref, wx_ref, bx_ref, lam_ref, wat_ref, wxt_ref,
             drx_ref, drz_ref, dcw_ref, dcb_ref, dwa_ref, dba_ref, dwx_ref, dbx_ref, dlam_ref,
             afirst, gfirst, dhalo):
        i = pl.program_id(0)
        first_tile = (i == nt - 1)

        @pl.when(i == 0)
        def _():
            afirst[...] = jnp.zeros_like(afirst)
            gfirst[...] = jnp.zeros_like(gfirst)
            dhalo[...] = jnp.zeros_like(dhalo)
            for r in (dcw_ref, dcb_ref, dwa_ref, dba_ref, dwx_ref, dbx_ref, dlam_ref):
                r[...] = jnp.zeros_like(r)

        keep = jnp.where(first_tile, 0.0, 1.0)
        rx = rx_ref[...]
        ext = jnp.concatenate([rxp_ref[...] * keep, rx], 0)
        taps = _conv_taps(ext, T)
        xr = cb_ref[...] + sum(cw_ref[k:k + 1, :] * taps[k] for k in range(CONV_WIDTH))
        xb, r, ig, sp, a, mult = _rg_gates(xr, wa_ref, ba_ref, wx_ref, bx_ref, lam_ref)
        hh = h_ref[...]
        rz = rz_ref[...]
        dy = dy_ref[...]
        drz_ref[...] = (dy * hh * _dsilu(rz)).astype(BF16)
        dh = dy * _silu(rz)
        row = lax.broadcasted_iota(jnp.int32, (T, 1), 0)
        c = jnp.where(row == T - 1, afirst[0:1, :], pltpu.roll(a, T - 1, 0))
        ccum, gloc = _scan_lin(c, dh, True)
        g = gloc + ccum * gfirst[0:1, :]
        afirst[...] = jnp.broadcast_to(a[0:1, :], (8, C))
        gfirst[...] = jnp.broadcast_to(g[0:1, :], (8, C))
        hprev = jnp.where(row == 0, hp_ref[7:8, :] * keep, pltpu.roll(hh, 1, 0))
        da = g * hprev
        gx = ig * xr
        dgx = g * mult
        dmult = g * gx
        dlog_a = da * a - dmult * (a * a) / mult
        dpre_a = dlog_a * (-R_C * sp) * r * (1.0 - r)
        dpre_x = dgx * xr * ig * (1.0 - ig)
        dlam_ref[...] += jnp.sum(dlog_a * (-R_C * r), 0, keepdims=True) * (-_sigmoid(-lam_ref[...]))
        dab = _b(dpre_a)
        dxb = _b(dpre_x)
        dxr = dgx * ig + jnp.concatenate(
            [_nn(dab[:, n * 128:(n + 1) * 128], wat_ref[n]) + _nn(dxb[:, n * 128:(n + 1) * 128], wxt_ref[n])
             for n in range(R_BLOCKS)], 1)
        for n in range(R_BLOCKS):
            cs = slice(n * 128, (n + 1) * 128)
            dwa_ref[n] += _tn(xb[:, cs], dab[:, cs])
            dwx_ref[n] += _tn(xb[:, cs], dxb[:, cs])
        dba_ref[...] += jnp.sum(dpre_a, 0, keepdims=True)
        dbx_ref[...] += jnp.sum(dpre_x, 0, keepdims=True)
        dcb_ref[...] += jnp.sum(dxr, 0, keepdims=True)
        for k in range(CONV_WIDTH):
            dcw_ref[k:k + 1, :] += jnp.sum(dxr * taps[k], 0, keepdims=True)
        ext2 = jnp.concatenate([dxr, dhalo[...]], 0)
        tt = _conv_taps_t(ext2, T)
        drx_ref[...] = sum(cw_ref[k:k + 1, :] * tt[k] for k in range(CONV_WIDTH)).astype(BF16)
        dhalo[...] = dxr[0:8]

    def rev(i):
        return nt - 1 - i

    def prev8(i):
        return jnp.maximum(rev(i) * t8 - 1, 0)

    vec = pl.BlockSpec((1, C), lambda i: (0, 0))
    blk = pl.BlockSpec((R_BLOCKS, 128, 128), lambda i: (0, 0, 0))
    row = pl.BlockSpec((T, C), lambda i: (rev(i), 0))
    wat = _b(jnp.swapaxes(wa, 1, 2))
    wxt = _b(jnp.swapaxes(wx, 1, 2))
    return _pcall(
        body, name=name, grid=(nt,),
        in_specs=[pl.BlockSpec((T, C), lambda i: (rev(i), OFF_RX // C)),
                  pl.BlockSpec((8, C), lambda i: (prev8(i), OFF_RX // C)),
                  pl.BlockSpec((T, C), lambda i: (rev(i), OFF_RZ // C)),
                  row,
                  pl.BlockSpec((8, C), lambda i: (prev8(i), 0)),
                  pl.BlockSpec((T, C), lambda i: (rev(i), MIX_R // C)),
                  ] + _rg_param_specs() + [blk, blk],
        out_specs=[row, row, pl.BlockSpec((CONV_WIDTH, C), lambda i: (0, 0)), vec, blk, vec, blk, vec, vec],
        out_shape=[jax.ShapeDtypeStruct((S_, C), BF16), jax.ShapeDtypeStruct((S_, C), BF16),
                   jax.ShapeDtypeStruct((CONV_WIDTH, C), F32), jax.ShapeDtypeStruct((1, C), F32),
                   jax.ShapeDtypeStruct((R_BLOCKS, 128, 128), F32), jax.ShapeDtypeStruct((1, C), F32),
                   jax.ShapeDtypeStruct((R_BLOCKS, 128, 128), F32), jax.ShapeDtypeStruct((1, C), F32),
                   jax.ShapeDtypeStruct((1, C), F32)],
        scratch_shapes=[pltpu.VMEM((8, C), F32)] * 3,
        compiler_params=_cp(("arbitrary",)),
    )(proj, proj, proj, h, h, dymix, cw, cb.reshape(1, C), _b(wa), ba.reshape(1, C), _b(wx), bx.reshape(1, C),
      lam.reshape(1, C), wat, wxt)


GW3 = 3 * G_WIDTH


def _lane_col(x, lane_idx):
    lane = lax.broadcasted_iota(jnp.int32, (1, x.shape[1]), 1)
    return jnp.sum(jnp.where(lane == lane_idx, x, 0.0), 1, keepdims=True)


def _gdn_pre(ext, T, cw_ref, gba, pv_ref):
    taps = _conv_taps(ext, T)
    c = sum(cw_ref[k:k + 1, :] * taps[k] for k in range(CONV_WIDTH))
    qkv = _silu(c)
    beta = _sigmoid(gba)
    sarg = gba + pv_ref[1:2, :]
    nea = -jnp.exp(pv_ref[0:1, :])
    gdec = nea * _softplus(sarg)
    ri = lax.broadcasted_iota(jnp.int32, (T, T), 0)
    cj = lax.broadcasted_iota(jnp.int32, (T, T), 1)
    same = (ri >> 6) == (cj >> 6)
    ltri = jnp.where((ri >= cj) & same, 1.0, 0.0).astype(BF16)
    gc = _dot_exact_lhs(_nn, ltri, gdec)
    return taps, c, qkv, beta, sarg, nea, gdec, gc


def _gdn_masks():
    ri = lax.broadcasted_iota(jnp.int32, (128, 128), 0)
    cj = lax.broadcasted_iota(jnp.int32, (128, 128), 1)
    same = (ri >> 6) == (cj >> 6)
    return (ri >= cj) & same, (ri > cj) & same, ri == cj


def _gdn_chunk(qkv, beta, gc, rs, h, tm=None):
    tril, strict, eye = _gdn_masks()
    rowi = lax.broadcasted_iota(jnp.int32, (128, 1), 0)
    lane = lax.broadcasted_iota(jnp.int32, (1, 128), 1)
    qh = qkv[rs, h * 128:(h + 1) * 128]
    kh = qkv[rs, 512 + h * 128:512 + (h + 1) * 128]
    vh = qkv[rs, 1024 + h * 128:1024 + (h + 1) * 128]
    rq = lax.rsqrt(jnp.sum(qh * qh, 1, keepdims=True) + RMS_EPS)
    rk = lax.rsqrt(jnp.sum(kh * kh, 1, keepdims=True) + RMS_EPS)
    qn = qh * (rq * (G_HEAD_DIM ** -0.5))
    kn = kh * rk
    gcb = gc[rs]
    gcol = _lane_col(gcb, 4 + h)
    bcol = _lane_col(beta[rs], h)
    grow = _dot_exact_lhs(_nt, jnp.ones((128, 128), BF16), jnp.where(lane == 4 + h, gcb, 0.0))
    D = jnp.where(tril, jnp.exp(jnp.minimum(gcol - grow, 0.0)), 0.0)
    kb = kn * bcol
    vb = vh * bcol
    knb = _b(kn)
    A = _nt(_b(kb), knb)
    Bm = _nt(_b(qn), knb)
    if tm is None:
        N = jnp.where(strict, -(A * D), 0.0)
        tm = jnp.where(eye, 1.0, 0.0) + N
        npow = N
        for _ in range(5):
            npow = _dot3(_nn, npow, npow)
            tm = tm + _dot3(_nn, tm, npow)
    eg = jnp.exp(gcol)
    u = _dot3(_nn, tm, vb)
    w = _dot3(_nn, tm, kb * eg)
    qk = jnp.where(tril, Bm * D, 0.0)
    qd = qn * eg
    gla = jnp.sum(jnp.where(rowi == 63, gcol, 0.0), 0, keepdims=True)
    glb = jnp.sum(jnp.where(rowi == 127, gcol, 0.0), 0, keepdims=True)
    ed = jnp.exp(jnp.where(rowi < 64, gla, glb) - gcol)
    kd = kn * ed
    return dict(qh=qh, kh=kh, vh=vh, rq=rq, rk=rk, qn=qn, kn=kn, gcol=gcol, bcol=bcol, D=D, A=A, Bm=Bm,
                tm=tm, eg=eg, ed=ed, u=u, w=w, qk=qk, qd=qd, kd=kd, kb=kb, vb=vb,
                gla=jnp.exp(gla), glb=jnp.exp(glb))


def _gdn_scan(q, sa):
    sab = _b(sa)
    wb = _b(q["w"])
    vna = q["u"] - _nn(wb, sab)
    sb = sa * q["gla"] + _tn(_b(q["kd"][0:64]), _b(vna[0:64]))
    sbb = _b(sb)
    vnb = q["u"] - _nn(wb, sbb)
    sn = sb * q["glb"] + _tn(_b(q["kd"][64:128]), _b(vnb[64:128]))
    vn = jnp.concatenate([vna[0:64], vnb[64:128]], 0)
    qdb = _b(q["qd"])
    o = jnp.concatenate([_nn(qdb[0:64], sab), _nn(qdb[64:128], sbb)], 0) + _nn(_b(q["qk"]), _b(vn))
    return sb, sn, vn, o


def _gdn_param_specs():
    return [pl.BlockSpec((CONV_WIDTH, GW3), lambda i: (0, 0)),
            pl.BlockSpec((8, 128), lambda i: (0, 0)),
            pl.BlockSpec((1, 128), lambda i: (0, 0))]


def _gdn_pvec(a_log, dt_bias):
    z = jnp.zeros((8, 128), F32)
    return z.at[0, 4:8].set(a_log).at[1, 4:8].set(dt_bias)


def _gdn_fwd(proj, cw, a_log, dt_bias, nw, *, T, name):
    S_ = proj.shape[0]
    nu = T // 128

    def body(x_ref, z_ref, g_ref, cw_ref, pv_ref, nw_ref, y_ref, st_ref, tm_ref, halo, state):
        i = pl.program_id(0)

        @pl.when(i == 0)
        def _():
            halo[...] = jnp.zeros_like(halo)
            state[...] = jnp.zeros_like(state)

        x = x_ref[...]
        ext = jnp.concatenate([halo[...], x], 0)
        halo[...] = x[T - 8:]
        _, _, qkv, beta, _, _, _, gc = _gdn_pre(ext, T, cw_ref, g_ref[...], pv_ref)
        for dc in range(nu):
            rs = slice(dc * 128, (dc + 1) * 128)
            for h in range(G_HEADS):
                q = _gdn_chunk(qkv, beta, gc, rs, h)
                sa = state[h]
                sb, sn, _, o = _gdn_scan(q, sa)
                st_ref[2 * dc, h] = sa
                st_ref[2 * dc + 1, h] = sb
                tm_ref[dc, h] = q["tm"]
                state[h] = sn
                rn = lax.rsqrt(jnp.mean(o * o, 1, keepdims=True) + RMS_EPS)
                cs = slice(h * 128, (h + 1) * 128)
                y_ref[rs, cs] = (o * rn * nw_ref[...] * _silu(z_ref[rs, cs])).astype(BF16)

    return _pcall(
        body, name=name, grid=(S_ // T,),
        in_specs=[pl.BlockSpec((T, GW3), lambda i: (i, OFF_GQKV // GW3)),
                  pl.BlockSpec((T, 512), lambda i: (i, OFF_GZ // 512)),
                  pl.BlockSpec((T, 128), lambda i: (i, OFF_GBA // 128))] + _gdn_param_specs(),
        out_specs=[pl.BlockSpec((T, 512), lambda i: (i, 0)),
                   pl.BlockSpec((2 * nu, G_HEADS, 128, 128), lambda i: (i, 0, 0, 0)),
                   pl.BlockSpec((nu, G_HEADS, 128, 128), lambda i: (i, 0, 0, 0))],
        out_shape=[jax.ShapeDtypeStruct((S_, 512), BF16),
                   jax.ShapeDtypeStruct((S_ // 64, G_HEADS, 128, 128), F32),
                   jax.ShapeDtypeStruct((S_ // 128, G_HEADS, 128, 128), F32)],
        scratch_shapes=[pltpu.VMEM((8, GW3), F32), pltpu.VMEM((G_HEADS, 128, 128), F32)],
        compiler_params=_cp(("arbitrary",)),
    )(proj, proj, proj, cw, _gdn_pvec(a_log, dt_bias), nw.reshape(1, 128))


def _gdn_bwd(proj, states, tms, dymix, cw, a_log, dt_bias, nw, *, T, name):
    S_ = proj.shape[0]
    nt = S_ // T
    nu = T // 128
    t8 = T // 8

    def body(x_ref, xp_ref, z_ref, g_ref, st_ref, tm_ref, dy_ref, cw_ref, pv_ref, nw_ref,
             dx_ref, dz_ref, dg_ref, dcw_ref, dpv_ref, dnw_ref, dstate, dhalo, dqkv, dbg):
        i = pl.program_id(0)
        first_tile = (i == nt - 1)

        @pl.when(i == 0)
        def _():
            dstate[...] = jnp.zeros_like(dstate)
            dhalo[...] = jnp.zeros_like(dhalo)
            dcw_ref[...] = jnp.zeros_like(dcw_ref)
            dpv_ref[...] = jnp.zeros_like(dpv_ref)
            dnw_ref[...] = jnp.zeros_like(dnw_ref)

        keep = jnp.where(first_tile, 0.0, 1.0)
        ext = jnp.concatenate([xp_ref[...] * keep, x_ref[...]], 0)
        G = g_ref[...]
        taps, c, qkv, beta, sarg, nea, gdec, gc = _gdn_pre(ext, T, cw_ref, G, pv_ref)
        tril, strict, _ = _gdn_masks()
        rowi = lax.broadcasted_iota(jnp.int32, (128, 1), 0)
        lane = lax.broadcasted_iota(jnp.int32, (1, 128), 1)
        ones_b = jnp.ones((128, 128), BF16)
        nwv = nw_ref[...]
        for dc in reversed(range(nu)):
            rs = slice(dc * 128, (dc + 1) * 128)
            dbg_blk = jnp.zeros((128, 128), F32)
            for h in range(G_HEADS):
                q = _gdn_chunk(qkv, beta, gc, rs, h, tm=tm_ref[dc, h])
                sa = st_ref[2 * dc, h]
                sb, _, vn, o = _gdn_scan(q, sa)
                cs = slice(h * 128, (h + 1) * 128)
                zg = z_ref[rs, cs]
                dy = dy_ref[rs, cs]
                rn = lax.rsqrt(jnp.mean(o * o, 1, keepdims=True) + RMS_EPS)
                don = dy * _silu(zg)
                dz_ref[rs, cs] = (dy * (o * rn * nwv) * _dsilu(zg)).astype(BF16)
                dnw_ref[...] += jnp.sum(don * o * rn, 0, keepdims=True)
                tt = don * nwv
                do = rn * (tt - o * (rn * rn) * jnp.mean(tt * o, 1, keepdims=True))
                dob = _b(do)
                sab, sbb = _b(sa), _b(sb)
                vnb16 = _b(vn)
                dqk = jnp.where(tril, _nt(dob, vnb16), 0.0)
                dvn_o = _tn(_b(q["qk"]), dob)
                dS = dstate[h]
                dSb16 = _b(dS)
                kdb = _b(q["kd"])
                wb = _b(q["w"])
                qdb = _b(q["qd"])
                dvn_b = dvn_o[64:128] + _nn(kdb[64:128], dSb16)
                dkd_b = _nt(vnb16[64:128], dSb16)
                dgl_b = jnp.sum(jnp.sum(dS * sb, 1, keepdims=True), 0, keepdims=True)
                dvn_b16 = _b(dvn_b)
                dw_b = -_nt(dvn_b16, sbb)
                dqd_b = _nt(dob[64:128], sbb)
                dSm = q["glb"] * dS + _tn(qdb[64:128], dob[64:128]) - _tn(wb[64:128], dvn_b16)
                dSm16 = _b(dSm)
                dvn_a = dvn_o[0:64] + _nn(kdb[0:64], dSm16)
                dkd_a = _nt(vnb16[0:64], dSm16)
                dgl_a = jnp.sum(jnp.sum(dSm * sa, 1, keepdims=True), 0, keepdims=True)
                dvn_a16 = _b(dvn_a)
                dw_a = -_nt(dvn_a16, sab)
                dqd_a = _nt(dob[0:64], sab)
                dstate[h] = q["gla"] * dSm + _tn(qdb[0:64], dob[0:64]) - _tn(wb[0:64], dvn_a16)
                du = jnp.concatenate([dvn_a, dvn_b], 0)
                dw = jnp.concatenate([dw_a, dw_b], 0)
                dkd = jnp.concatenate([dkd_a, dkd_b], 0)
                dqd = jnp.concatenate([dqd_a, dqd_b], 0)
                dvb = _dot3(_tn, q["tm"], du)
                dkbe = _dot3(_tn, q["tm"], dw)
                dM = jnp.where(strict, -(_nt(_b(dvb), _b(q["u"])) + _nt(_b(dkbe), _b(q["w"]))), 0.0)
                D = q["D"]
                dA = dM * D
                dB = dqk * D
                dDD = (dM * q["A"] + dqk * q["Bm"]) * D
                dh_, dm_, dl_ = _split3(dDD)
                colsum = _tn(dh_, ones_b) + (_tn(dm_, ones_b) + _tn(dl_, ones_b))
                dgc = jnp.sum(dDD, 1, keepdims=True) - _lane_col(colsum, 0)
                dA16, dB16 = _b(dA), _b(dB)
                knb, kbb, qnb = _b(q["kn"]), _b(q["kb"]), _b(q["qn"])
                eg, ed = q["eg"], q["ed"]
                dkb = _nn(dA16, knb) + dkbe * eg
                dkn = _tn(dA16, kbb) + _tn(dB16, qnb) + dkd * ed + dkb * q["bcol"]
                dqn = _nn(dB16, knb) + dqd * eg
                deg = jnp.sum(dkbe * q["kb"], 1, keepdims=True) + jnp.sum(dqd * q["qn"], 1, keepdims=True)
                ded = jnp.sum(dkd * q["kn"], 1, keepdims=True) * ed
                dgc = dgc + deg * eg - ded
                tail_a = jnp.sum(jnp.where(rowi < 64, ded, 0.0), 0, keepdims=True) + dgl_a * q["gla"]
                tail_b = jnp.sum(jnp.where(rowi >= 64, ded, 0.0), 0, keepdims=True) + dgl_b * q["glb"]
                dgc = dgc + jnp.where(rowi == 63, tail_a, 0.0) + jnp.where(rowi == 127, tail_b, 0.0)
                dbeta = jnp.sum(dkb * q["kn"], 1, keepdims=True) + jnp.sum(dvb * q["vh"], 1, keepdims=True)
                bcol = q["bcol"]
                dbg_blk = dbg_blk + jnp.where(lane == h, dbeta * bcol * (1.0 - bcol), 0.0) \
                    + jnp.where(lane == 4 + h, dgc, 0.0)
                sc = G_HEAD_DIM ** -0.5
                rq, rk, qh, kh = q["rq"], q["rk"], q["qh"], q["kh"]
                dqh = sc * (dqn * rq - qh * (rq * rq * rq) * jnp.sum(dqn * qh, 1, keepdims=True))
                dkh = dkn * rk - kh * (rk * rk * rk) * jnp.sum(dkn * kh, 1, keepdims=True)
                dqkv[rs, h * 128:(h + 1) * 128] = dqh
                dqkv[rs, 512 + h * 128:512 + (h + 1) * 128] = dkh
                dqkv[rs, 1024 + h * 128:1024 + (h + 1) * 128] = dvb * bcol
            dbg[rs, :] = dbg_blk
        ri = lax.broadcasted_iota(jnp.int32, (T, T), 0)
        cj = lax.broadcasted_iota(jnp.int32, (T, T), 1)
        utri = jnp.where((ri <= cj) & ((ri >> 6) == (cj >> 6)), 1.0, 0.0).astype(BF16)
        dbgv = dbg[...]
        dgd = _dot_exact_lhs(_nn, utri, dbgv)
        is_g = (lane >= 4) & (lane < 8)
        dga = jnp.where(is_g, dgd * nea * _sigmoid(sarg), 0.0)
        dg_ref[...] = jnp.where(lane < 4, dbgv, dga).astype(BF16)
        dpv_ref[0:1, :] += jnp.sum(jnp.where(is_g, dgd * gdec, 0.0), 0, keepdims=True)
        dpv_ref[1:2, :] += jnp.sum(dga, 0, keepdims=True)
        dc_ = dqkv[...] * _dsilu(c)
        for k in range(CONV_WIDTH):
            dcw_ref[k:k + 1, :] += jnp.sum(dc_ * taps[k], 0, keepdims=True)
        ext2 = jnp.concatenate([dc_, dhalo[...]], 0)
        tt2 = _conv_taps_t(ext2, T)
        dx_ref[...] = sum(cw_ref[k:k + 1, :] * tt2[k] for k in range(CONV_WIDTH)).astype(BF16)
        dhalo[...] = dc_[0:8]

    def rev(i):
        return nt - 1 - i

    def prev8(i):
        return jnp.maximum(rev(i) * t8 - 1, 0)

    return _pcall(
        body, name=name, grid=(nt,),
        in_specs=[pl.BlockSpec((T, GW3), lambda i: (rev(i), OFF_GQKV // GW3)),
                  pl.BlockSpec((8, GW3), lambda i: (prev8(i), OFF_GQKV // GW3)),
                  pl.BlockSpec((T, 512), lambda i: (rev(i), OFF_GZ // 512)),
                  pl.BlockSpec((T, 128), lambda i: (rev(i), OFF_GBA // 128)),
                  pl.BlockSpec((2 * nu, G_HEADS, 128, 128), lambda i: (rev(i), 0, 0, 0)),
                  pl.BlockSpec((nu, G_HEADS, 128, 128), lambda i: (rev(i), 0, 0, 0)),
                  pl.BlockSpec((T, 512), lambda i: (rev(i), MIX_G // 512))] + _gdn_param_specs(),
        out_specs=[pl.BlockSpec((T, GW3), lambda i: (rev(i), 0)),
                   pl.BlockSpec((T, 512), lambda i: (rev(i), 0)),
                   pl.BlockSpec((T, 128), lambda i: (rev(i), 0)),
                   pl.BlockSpec((CONV_WIDTH, GW3), lambda i: (0, 0)),
                   pl.BlockSpec((8, 128), lambda i: (0, 0)),
                   pl.BlockSpec((1, 128), lambda i: (0, 0))],
        out_shape=[jax.ShapeDtypeStruct((S_, GW3), BF16), jax.ShapeDtypeStruct((S_, 512), BF16),
                   jax.ShapeDtypeStruct((S_, 128), BF16), jax.ShapeDtypeStruct((CONV_WIDTH, GW3), F32),
                   jax.ShapeDtypeStruct((8, 128), F32), jax.ShapeDtypeStruct((1, 128), F32)],
        scratch_shapes=[pltpu.VMEM((G_HEADS, 128, 128), F32), pltpu.VMEM((8, GW3), F32),
                        pltpu.VMEM((T, GW3), F32), pltpu.VMEM((T, 128), F32)],
        compiler_params=_cp(("arbitrary",)),
    )(proj, proj, proj, proj, states, tms, dymix, cw, _gdn_pvec(a_log, dt_bias), nw.reshape(1, 128))


def _add2(a, b, *, out_dtype, tr, name):
    R_, C = a.shape

    def body(a_ref, b_ref, o_ref):
        o_ref[...] = (a_ref[...] + b_ref[...]).astype(o_ref.dtype)

    spec = pl.BlockSpec((tr, C), lambda i: (i, 0))
    return _pcall(body, name=name, grid=(R_ // tr,), in_specs=[spec, spec], out_specs=spec,
                  out_shape=jax.ShapeDtypeStruct((R_, C), out_dtype), compiler_params=_cp(("parallel",)))(a, b)


def _sum4(a, *, tr, name):
    _, R_, C = a.shape

    def body(a_ref, o_ref):
        o_ref[...] = ((a_ref[0].astype(F32) + a_ref[1].astype(F32)) + a_ref[2].astype(F32)) + a_ref[3].astype(F32)

    return _pcall(body, name=name, grid=(R_ // tr,),
                  in_specs=[pl.BlockSpec((4, tr, C), lambda i: (0, i, 0))],
                  out_specs=pl.BlockSpec((tr, C), lambda i: (i, 0)),
                  out_shape=jax.ShapeDtypeStruct((R_, C), F32), compiler_params=_cp(("parallel",)))(a)


def _adamw(w, g, m, v, *, tr, name):
    R_, C = w.shape
    c1 = 1.0 / (1.0 - ADAM_B1 ** ADAM_STEP)
    c2 = 1.0 / (1.0 - ADAM_B2 ** ADAM_STEP)

    def body(w_ref, g_ref, m_ref, v_ref, d_ref, mo_ref, vo_ref):
        gg = g_ref[...]
        mn = ADAM_B1 * m_ref[...] + (1.0 - ADAM_B1) * gg
        vn = ADAM_B2 * v_ref[...] + (1.0 - ADAM_B2) * (gg * gg)
        mo_ref[...] = mn
        vo_ref[...] = vn
        d_ref[...] = -ADAM_LR * ((mn * c1) / (jnp.sqrt(vn * c2) + ADAM_EPS) + ADAM_WD * w_ref[...])

    spec = pl.BlockSpec((tr, C), lambda i: (i, 0))
    shp = jax.ShapeDtypeStruct((R_, C), F32)
    return _pcall(body, name=name, grid=(R_ // tr,), in_specs=[spec] * 4, out_specs=[spec] * 3,
                  out_shape=[shp] * 3, compiler_params=_cp(("parallel",)))(w, g, m, v)


HBM_SPEC = pl.BlockSpec(memory_space=pltpu.HBM)


def _place():
    x, y, c = lax.axis_index("x"), lax.axis_index("y"), lax.axis_index("c")
    chips = [(1 - x, y), (x, 1 - y), (1 - x, 1 - y)]
    return x, y, c, 2 * x + y, chips, [2 * cx + cy for cx, cy in chips], (x, y, 1 - c)


def _remote(src, dst, ssem, rsem, dev):
    return pltpu.make_async_remote_copy(src_ref=src, dst_ref=dst, send_sem=ssem, recv_sem=rsem,
                                        device_id=dev, device_id_type=MESH)


def _gather_weights(win, wout, conv):
    hin = win.shape[1] // 2
    hout = wout.shape[1] // 2

    def body(win_ref, wout_ref, cv_ref, gin_ref, gout_ref, gcv_ref, ssem, rsem, lsem):
        x, y, c, s, chips, sid, sib = _place()

        def in_half(slot, hc):
            return gin_ref.at[slot, :, pl.ds(hc * hin, hin), :]

        def out_half(slot, hc):
            return gout_ref.at[slot, :, pl.ds(hc * hout, hout), :]

        local = [pltpu.make_async_copy(win_ref, gin_ref.at[s], lsem.at[0]),
                 pltpu.make_async_copy(wout_ref, gout_ref.at[s], lsem.at[1]),
                 pltpu.make_async_copy(cv_ref, gcv_ref.at[s], lsem.at[2])]
        for cp in local:
            cp.start()
        sends = []
        for j, chip in enumerate(chips):
            dev = (*chip, c)
            sends.append(_remote(win_ref.at[:, pl.ds(c * hin, hin), :], in_half(s, c), ssem.at[j], rsem.at[j], dev))
            sends.append(_remote(wout_ref.at[:, pl.ds(c * hout, hout), :], out_half(s, c),
                                 ssem.at[3 + j], rsem.at[3 + j], dev))
            sends.append(_remote(cv_ref, gcv_ref.at[s], ssem.at[6 + j], rsem.at[6 + j], dev))
        for cp in sends:
            cp.start()
        for j in range(3):
            _remote(in_half(sid[j], c), in_half(sid[j], c), ssem.at[j], rsem.at[j], sib).wait_recv()
            f = _remote(in_half(sid[j], c), in_half(sid[j], c), ssem.at[9 + j], rsem.at[9 + j], sib)
            f.start()
            sends.append(f)
            _remote(out_half(sid[j], c), out_half(sid[j], c), ssem.at[3 + j], rsem.at[3 + j], sib).wait_recv()
            f = _remote(out_half(sid[j], c), out_half(sid[j], c), ssem.at[12 + j], rsem.at[12 + j], sib)
            f.start()
            sends.append(f)
        for j in range(3):
            _remote(in_half(sid[j], 1 - c), in_half(sid[j], 1 - c), ssem.at[9 + j], rsem.at[9 + j], sib).wait_recv()
            _remote(out_half(sid[j], 1 - c), out_half(sid[j], 1 - c), ssem.at[12 + j], rsem.at[12 + j], sib).wait_recv()
            _remote(gcv_ref.at[sid[j]], gcv_ref.at[sid[j]], ssem.at[6 + j], rsem.at[6 + j], sib).wait_recv()
        for cp in sends:
            cp.wait_send()
        for cp in local:
            cp.wait()

    return _pcall(
        body, name="gather_weights",
        in_specs=[HBM_SPEC] * 3, out_specs=[HBM_SPEC] * 3,
        out_shape=[jax.ShapeDtypeStruct((4,) + win.shape, win.dtype),
                   jax.ShapeDtypeStruct((4,) + wout.shape, wout.dtype),
                   jax.ShapeDtypeStruct((4,) + conv.shape, conv.dtype)],
        scratch_shapes=[pltpu.SemaphoreType.DMA((15,)), pltpu.SemaphoreType.DMA((15,)),
                        pltpu.SemaphoreType.DMA((3,))],
    )(win, wout, conv)


def _swap_sibling(arrs, name):
    n = len(arrs)

    def body(*refs):
        src, dst, ssem, rsem = refs[:n], refs[n:2 * n], refs[2 * n], refs[2 * n + 1]
        *_, sib = _place()
        cps = [_remote(src[k], dst[k], ssem.at[k], rsem.at[k], sib) for k in range(n)]
        for cp in cps:
            cp.start()
        for cp in cps:
            cp.wait()

    return _pcall(
        body, name=name, in_specs=[HBM_SPEC] * n, out_specs=[HBM_SPEC] * n,
        out_shape=[jax.ShapeDtypeStruct(a.shape, a.dtype) for a in arrs],
        scratch_shapes=[pltpu.SemaphoreType.DMA((n,)), pltpu.SemaphoreType.DMA((n,))],
    )(*arrs)


def _scatter_chips(arrs, per_target, name):
    n = len(arrs)

    def body(*refs):
        src, dst = refs[:n], refs[n:2 * n]
        ssem, rsem, lsem = refs[2 * n], refs[2 * n + 1], refs[2 * n + 2]
        x, y, c, s, chips, sid, sib = _place()
        local = []
        sends = []
        for k in range(n):
            mine = src[k].at[s] if per_target[k] else src[k]
            local.append(pltpu.make_async_copy(mine, dst[k].at[s], lsem.at[k]))
            for j, chip in enumerate(chips):
                piece = src[k].at[sid[j]] if per_target[k] else src[k]
                sends.append(_remote(piece, dst[k].at[s], ssem.at[3 * k + j], rsem.at[3 * k + j], (*chip, c)))
        for cp in local + sends:
            cp.start()
        for k in range(n):
            for j in range(3):
                _remote(dst[k].at[sid[j]], dst[k].at[sid[j]], ssem.at[3 * k + j], rsem.at[3 * k + j], sib).wait_recv()
        for cp in sends:
            cp.wait_send()
        for cp in local:
            cp.wait()

    outs = [jax.ShapeDtypeStruct(a.shape if pt else (4,) + a.shape, a.dtype) for a, pt in zip(arrs, per_target)]
    return _pcall(
        body, name=name, in_specs=[HBM_SPEC] * n, out_specs=[HBM_SPEC] * n, out_shape=outs,
        scratch_shapes=[pltpu.SemaphoreType.DMA((3 * n,)), pltpu.SemaphoreType.DMA((3 * n,)),
                        pltpu.SemaphoreType.DMA((n,))],
    )(*arrs)


def _join_halves(arrs, axes, name):
    n = len(arrs)

    def body(*refs):
        src, dst = refs[:n], refs[n:2 * n]
        ssem, rsem, lsem = refs[2 * n], refs[2 * n + 1], refs[2 * n + 2]
        x, y, c, s, chips, sid, sib = _place()

        def half(k, hc):
            hl = src[k].shape[axes[k]]
            idx = [slice(None)] * len(dst[k].shape)
            idx[axes[k]] = pl.ds(hc * hl, hl)
            return dst[k].at[tuple(idx)]

        local = [pltpu.make_async_copy(src[k], half(k, c), lsem.at[k]) for k in range(n)]
        sends = [_remote(src[k], half(k, c), ssem.at[k], rsem.at[k], sib) for k in range(n)]
        for cp in local + sends:
            cp.start()
        for k in range(n):
            _remote(half(k, 1 - c), half(k, 1 - c), ssem.at[k], rsem.at[k], sib).wait_recv()
        for cp in sends:
            cp.wait_send()
        for cp in local:
            cp.wait()

    outs = []
    for a, ax in zip(arrs, axes):
        shp = list(a.shape)
        shp[ax] *= 2
        outs.append(jax.ShapeDtypeStruct(tuple(shp), a.dtype))
    return _pcall(
        body, name=name, in_specs=[HBM_SPEC] * n, out_specs=[HBM_SPEC] * n, out_shape=outs,
        scratch_shapes=[pltpu.SemaphoreType.DMA((n,)), pltpu.SemaphoreType.DMA((n,)),
                        pltpu.SemaphoreType.DMA((n,))],
    )(*arrs)


def _perm_cols(w):
    parts = [w[..., int(_ORIG_OFF[oi]):int(_ORIG_OFF[oi]) + IN_SIZES[oi]] for oi, _ in _PIECES]
    parts.append(jnp.zeros(w.shape[:-1] + (NP - N_IN,), w.dtype))
    return jnp.concatenate(parts, -1)


def _unperm_cols(g):
    parts = [None] * len(IN_SIZES)
    for oi, off in _PIECES:
        parts[oi] = g[..., off:off + IN_SIZES[oi]]
    return jnp.concatenate(parts, -1)


def _perm_rows(w):
    return jnp.concatenate([w[..., 512:1536, :], w[..., 0:512, :], w[..., 1536:2048, :]], -2)


def _unperm_rows(g):
    return jnp.concatenate([g[..., 1024:1536, :], g[..., 0:1024, :], g[..., 1536:2048, :]], -2)


_SMALL = ("sinks", "r_conv_b", "r_wa", "r_ba", "r_wx", "r_bx", "r_lam", "g_a_log", "g_dt_bias", "g_norm_w",
          "ln_g", "ln_b", "r_conv_w", "g_conv_w")
_PACK_ALIGN = 2048


def _pack(arrs):
    parts = []
    for a in arrs:
        flat = a.reshape(-1)
        pad = (-flat.shape[0]) % _PACK_ALIGN
        parts.append(jnp.concatenate([flat, jnp.zeros((pad,), flat.dtype)]) if pad else flat)
    return jnp.concatenate(parts).reshape(-1, 128)


def _unpack(packed, shapes):
    flat = packed.reshape(-1)
    out = []
    off = 0
    for shp in shapes:
        n = int(np.prod(shp))
        out.append(flat[off:off + n].reshape(shp))
        off += n + ((-n) % _PACK_ALIGN)
    return out


def _tile(n, t):
    return min(n, t)


def _layer_fwd(l, x, xb, wb, wob, rope_c, rope_s, p):
    S_ = x.shape[0]
    proj = _matmul(xb, wb, ta=False, tb=False, tm=_tile(S_, 1024), tn=512, tk=wb.shape[0], out_dtype=F32,
                   name=f"in_proj_{l}")
    ya = _attn_fwd(proj, rope_c, rope_s, p["sinks"], T=_tile(S_, 512), name=f"attn_fwd_{l}")
    h, yr = _rglru_fwd(proj, p["r_conv_w"], p["r_conv_b"], p["r_wa"], p["r_ba"], p["r_wx"], p["r_bx"], p["r_lam"],
                       T=_tile(S_, 256), name=f"rglru_fwd_{l}")
    yg, st, tms = _gdn_fwd(proj, p["g_conv_w"], p["g_a_log"], p["g_dt_bias"], p["g_norm_w"],
                           T=_tile(S_, 256), name=f"gdn_fwd_{l}")
    ymix = jnp.concatenate([yr, ya, yg], 1)
    z = _outproj(ymix, wob, x, tm=_tile(S_, 256), name=f"out_proj_{l}")
    return dict(proj=proj, h=h, st=st, tms=tms, ymix=ymix, z=z)


def _layer_bwd(l, sv, x_b, dz, dzb, wb, wob, rope_c, rope_s, p):
    S_, D = dz.shape
    proj = sv["proj"]
    dymix = _matmul(dzb, wob, ta=False, tb=True, tm=_tile(S_, 1024), tn=512, tk=D, out_dtype=F32,
                    name=f"dmix_{l}")
    dwo = _matmul(sv["ymix"], dzb, ta=True, tb=False, tm=1024, tn=_tile(D, 1024), tk=_tile(S_, 1024),
                  out_dtype=F32, name=f"dw_out_{l}")
    dq, daz, dk, dv, dkt, dvt, dsk = _attn_bwd(proj, rope_c, rope_s, p["sinks"], dymix, T=_tile(S_, 512),
                                               name=f"attn_bwd_{l}")
    (drx, drz, dcw_r, dcb_r, dwa, dba, dwx, dbx, dlam) = _rglru_bwd(
        proj, sv["h"], dymix, p["r_conv_w"], p["r_conv_b"], p["r_wa"], p["r_ba"], p["r_wx"], p["r_bx"], p["r_lam"],
        T=_tile(S_, 256), name=f"rglru_bwd_{l}")
    dqkv, dgz, dgba, dcw_g, dpv, dnw = _gdn_bwd(proj, sv["st"], sv["tms"], dymix, p["g_conv_w"], p["g_a_log"],
                                                p["g_dt_bias"], p["g_norm_w"], T=_tile(S_, 256), name=f"gdn_bwd_{l}")
    dproj = jnp.concatenate([drx, drz, dq, daz, dqkv, dgz,
                             jnp.concatenate([dk[128:], dkt], 0), jnp.concatenate([dv[128:], dvt], 0),
                             dgba, jnp.zeros((S_, NP - OFF_GBA - 128), BF16)], 1)
    dx = _matmul(dproj, wb, ta=False, tb=True, tm=_tile(S_, 512), tn=_tile(D, 1024), tk=NP // 2, out_dtype=F32,
                 name=f"dx_{l}", extra=dz, alpha=DEEPNORM_ALPHA)
    dwin = _matmul(x_b, dproj, ta=True, tb=False, tm=_tile(D, 1024), tn=512, tk=_tile(S_, 1024), out_dtype=F32,
                   name=f"dw_in_{l}")
    small = dict(sinks=dsk[:, 0], r_conv_b=dcb_r[0], r_wa=dwa, r_ba=dba[0], r_wx=dwx, r_bx=dbx[0], r_lam=dlam[0],
                 g_a_log=dpv[0, 4:8], g_dt_bias=dpv[1, 4:8], g_norm_w=dnw[0], r_conv_w=dcw_r, g_conv_w=dcw_g)
    return dx, dwin, dwo, small


def kernel(x, w_in, sinks, r_conv_w, r_conv_b, r_wa, r_ba, r_wx, r_bx, r_lam, g_conv_w, g_a_log, g_dt_bias, g_norm_w, w_out, ln_g, ln_b, loss_target, m_w_in, m_sinks, m_r_conv_w, m_r_conv_b, m_r_wa, m_r_ba, m_r_wx, m_r_bx, m_r_lam, m_g_conv_w, m_g_a_log, m_g_dt_bias, m_g_norm_w, m_w_out, m_ln_g, m_ln_b, v_w_in, v_sinks, v_r_conv_w, v_r_conv_b, v_r_wa, v_r_ba, v_r_wx, v_r_bx, v_r_lam, v_g_conv_w, v_g_a_log, v_g_dt_bias, v_g_norm_w, v_w_out, v_ln_g, v_ln_b):
    S_, D = x.shape[1], x.shape[2]
    nsh = w_in.shape[2]
    rsh = w_out.shape[1]
    cx, cy, cc = lax.axis_index("x"), lax.axis_index("y"), lax.axis_index("c")
    chip = 2 * cx + cy
    rcw_n, gcw_n = r_conv_w.shape[2], g_conv_w.shape[2]

    conv_pack = jnp.concatenate([r_conv_w, g_conv_w], 2)
    g_in, g_out, g_conv = _gather_weights(w_in.astype(BF16), w_out.astype(BF16), conv_pack)
    w_full = jnp.concatenate([g_in[t] for t in range(4)], 2)
    wb = _perm_cols(w_full)
    wob = _perm_rows(jnp.concatenate([g_out[t] for t in range(4)], 1))
    rcw = jnp.concatenate([g_conv[t][:, :, :rcw_n] for t in range(4)], 2)
    gcw = jnp.concatenate([g_conv[t][:, :, rcw_n:] for t in range(4)], 2)

    pos = jnp.arange(S_, dtype=F32)[:, None]
    inv = 1.0 / (ROPE_THETA ** (jnp.arange(0, A_HEAD_DIM, 2, dtype=F32) / A_HEAD_DIM))
    ang = pos * inv[None, :]
    cos, sin = jnp.cos(ang), jnp.sin(ang)
    rope_c = jnp.concatenate([cos, cos, cos, cos], 1)
    rope_s = jnp.concatenate([-sin, sin, -sin, sin], 1)

    def params(l):
        return dict(sinks=sinks[l], r_conv_w=rcw[l], r_conv_b=r_conv_b[l], r_wa=r_wa[l], r_ba=r_ba[l],
                    r_wx=r_wx[l], r_bx=r_bx[l], r_lam=r_lam[l], g_conv_w=gcw[l], g_a_log=g_a_log[l],
                    g_dt_bias=g_dt_bias[l], g_norm_w=g_norm_w[l])

    xs, xbs, saved = [x[0]], [x[0].astype(BF16)], []
    for l in range(DEPTH):
        sv = _layer_fwd(l, xs[l], xbs[l], wb[l], wob[l], rope_c, rope_s, params(l))
        saved.append(sv)
        if l + 1 < DEPTH:
            xn, xnb = _ln_fwd(sv["z"], ln_g[l], ln_b[l], tm=_tile(S_, 256), name=f"ln_fwd_{l}")
            xs.append(xn)
            xbs.append(xnb)

    tm_ln = _tile(S_, 256)
    dz, dzb, dg_l, db_l, loss_part = _ln_bwd(saved[-1]["z"], ln_g[-1], ln_b[-1], loss_target[0], from_target=True,
                                             tm=tm_ln, name=f"ln_bwd_{DEPTH - 1}")
    dwin, dwo, small = [None] * DEPTH, [None] * DEPTH, [None] * DEPTH
    dlng, dlnb = [None] * DEPTH, [None] * DEPTH
    for l in reversed(range(DEPTH)):
        dlng[l], dlnb[l] = dg_l[0], db_l[0]
        dx, dwin[l], dwo[l], small[l] = _layer_bwd(l, saved[l], xbs[l], dz, dzb, wb[l], wob[l], rope_c, rope_s,
                                                   params(l))
        if l > 0:
            dz, dzb, dg_l, db_l, _ = _ln_bwd(saved[l - 1]["z"], ln_g[l - 1], ln_b[l - 1], dx, from_target=False,
                                             tm=tm_ln, name=f"ln_bwd_{l - 1}")
    grad_x = dx[None]
    loss = lax.psum(loss_part[0, 0], ("x", "y", "c"))

    gw_in = _unperm_cols(jnp.stack(dwin))
    gw_in = jnp.stack([gw_in[:, :, t * nsh:(t + 1) * nsh] for t in range(4)])
    gw_out = _unperm_rows(jnp.stack(dwo))
    gw_out = jnp.stack([gw_out[:, t * rsh:(t + 1) * rsh, :] for t in range(4)])
    sm = {k: jnp.stack([small[l][k] for l in range(DEPTH)]) for k in small[0]}
    sm["ln_g"], sm["ln_b"] = jnp.stack(dlng), jnp.stack(dlnb)
    names = list(_SMALL)
    gs = _pack([sm[n] for n in names])
    hs = gs.shape[0] // 2
    hd, ho = D // 2, rsh // 2

    def halves(a, axis, hl):
        return (lax.dynamic_slice_in_dim(a, cc * hl, hl, axis), lax.dynamic_slice_in_dim(a, (1 - cc) * hl, hl, axis))

    in_keep, in_give = halves(gw_in, 2, hd)
    out_keep, out_give = halves(gw_out, 2, ho)
    s_keep, s_give = halves(gs, 0, hs)
    in_got, out_got, s_got = _swap_sibling([in_give, out_give, s_give], "reduce_pair")
    in_cp = _add2(in_keep.reshape(-1, nsh), in_got.reshape(-1, nsh), out_dtype=BF16, tr=256,
                  name="pair_sum_w_in").reshape(in_keep.shape)
    out_cp = _add2(out_keep.reshape(-1, D), out_got.reshape(-1, D), out_dtype=BF16, tr=256,
                   name="pair_sum_w_out").reshape(out_keep.shape)
    s_cp = _add2(s_keep, s_got, out_dtype=F32, tr=s_keep.shape[0], name="pair_sum_small")
    in_all, out_all, s_all = _scatter_chips([in_cp, out_cp, s_cp], [True, True, False], "reduce_chips")
    in_sum = _sum4(in_all.reshape(4, -1, nsh), tr=256, name="chip_sum_w_in").reshape(in_cp.shape[1:])
    out_sum = _sum4(out_all.reshape(4, -1, D), tr=256, name="chip_sum_w_out").reshape(out_cp.shape[1:])
    s_sum = _sum4(s_all, tr=s_cp.shape[0], name="chip_sum_small")
    g_w_in, g_w_out, g_small = _join_halves([in_sum, out_sum, s_sum], [1, 1, 0], "reduce_join")

    gsm = dict(zip(names, _unpack(g_small, [sm[n].shape for n in names])))
    gsm["r_conv_w"] = lax.dynamic_slice_in_dim(gsm["r_conv_w"], chip * rcw_n, rcw_n, 2)
    gsm["g_conv_w"] = lax.dynamic_slice_in_dim(gsm["g_conv_w"], chip * gcw_n, gcw_n, 2)
    wts = dict(sinks=sinks, r_conv_w=r_conv_w, r_conv_b=r_conv_b, r_wa=r_wa, r_ba=r_ba, r_wx=r_wx, r_bx=r_bx,
               r_lam=r_lam, g_conv_w=g_conv_w, g_a_log=g_a_log, g_dt_bias=g_dt_bias, g_norm_w=g_norm_w,
               ln_g=ln_g, ln_b=ln_b)
    mom = dict(sinks=m_sinks, r_conv_w=m_r_conv_w, r_conv_b=m_r_conv_b, r_wa=m_r_wa, r_ba=m_r_ba, r_wx=m_r_wx,
               r_bx=m_r_bx, r_lam=m_r_lam, g_conv_w=m_g_conv_w, g_a_log=m_g_a_log, g_dt_bias=m_g_dt_bias,
               g_norm_w=m_g_norm_w, ln_g=m_ln_g, ln_b=m_ln_b)
    vel = dict(sinks=v_sinks, r_conv_w=v_r_conv_w, r_conv_b=v_r_conv_b, r_wa=v_r_wa, r_ba=v_r_ba, r_wx=v_r_wx,
               r_bx=v_r_bx, r_lam=v_r_lam, g_conv_w=v_g_conv_w, g_a_log=v_g_a_log, g_dt_bias=v_g_dt_bias,
               g_norm_w=v_g_norm_w, ln_g=v_ln_g, ln_b=v_ln_b)
    pk = [_pack([d[n] for n in names]) for d in (wts, gsm, mom, vel)]
    sshapes = [wts[n].shape for n in names]
    d_s, m_s, v_s = _adamw(*pk, tr=pk[0].shape[0], name="adamw_small")
    d_sm, m_sm, v_sm = (dict(zip(names, _unpack(a, sshapes))) for a in (d_s, m_s, v_s))

    def big(w, g, m, v, name):
        C = w.shape[-1]
        outs = _adamw(w.reshape(-1, C), g.reshape(-1, C), m.reshape(-1, C), v.reshape(-1, C), tr=256, name=name)
        return [o.reshape(w.shape) for o in outs]

    d_in, m_in, v_in = big(w_in, g_w_in, m_w_in, v_w_in, "adamw_w_in")
    d_out, m_out, v_out = big(w_out, g_w_out, m_w_out, v_w_out, "adamw_w_out")

    order = ["w_in", "sinks", "r_conv_w", "r_conv_b", "r_wa", "r_ba", "r_wx", "r_bx", "r_lam", "g_conv_w",
             "g_a_log", "g_dt_bias", "g_norm_w", "w_out", "ln_g", "ln_b"]
    grads = dict(gsm, w_in=g_w_in, w_out=g_w_out)
    deltas = dict(d_sm, w_in=d_in, w_out=d_out)
    new_m = dict(m_sm, w_in=m_in, w_out=m_out)
    new_v = dict(v_sm, w_in=v_in, w_out=v_out)
    return (loss, grad_x, *[grads[n] for n in order], *[deltas[n] for n in order],
            *[new_m[n] for n in order], *[new_v[n] for n in order])
```

```python
import functools
import math

import jax
import jax.numpy as jnp
import numpy as np
from jax import lax
from jax.experimental import pallas as pl
from jax.experimental.pallas import tpu as pltpu

F32 = jnp.float32
BF16 = jnp.bfloat16
MESH = pl.DeviceIdType.MESH

DEPTH = 2
A_HEADS, A_KV_HEADS, A_HEAD_DIM = 8, 2, 64
A_WIDTH, A_KV_WIDTH = 512, 128
WINDOW = 128
ROPE_THETA = 10000.0
R_WIDTH, R_BLOCKS, R_BLOCK_DIM, R_C = 1024, 8, 128, 8.0
CONV_WIDTH = 4
G_HEADS, G_HEAD_DIM, G_WIDTH, G_CHUNK = 4, 128, 512, 64
MIX_WIDTH = 2048
IN_SIZES = (512, 128, 128, 512, 1024, 1024, 512, 512, 512, 512, 4, 4)
N_IN = 5384
DEEPNORM_ALPHA = (2 * DEPTH) ** 0.25
LN_EPS = 1e-5
RMS_EPS = 1e-6
ADAM_LR, ADAM_B1, ADAM_B2, ADAM_EPS, ADAM_WD, ADAM_STEP = 0.001, 0.9, 0.999, 1e-08, 0.01, 10

NP = 5632
OFF_RX, OFF_RZ, OFF_AQ, OFF_AZ, OFF_GQKV, OFF_GZ, OFF_AK, OFF_AV, OFF_GBA = (
    0, 1024, 2048, 2560, 3072, 4608, 5120, 5248, 5376)
_ORIG_OFF = np.concatenate([[0], np.cumsum(IN_SIZES)])[:-1]
_PIECES = ((4, OFF_RX), (5, OFF_RZ), (0, OFF_AQ), (3, OFF_AZ), (6, OFF_GQKV), (7, OFF_GQKV + 512),
           (8, OFF_GQKV + 1024), (9, OFF_GZ), (1, OFF_AK), (2, OFF_AV), (10, OFF_GBA), (11, OFF_GBA + 4))
MIX_R, MIX_A, MIX_G = 0, 1024, 1536
VMEM_LIMIT = 56 * 1024 * 1024


def _pcall(body, **kw):
    return pl.pallas_call(body, **kw)


def _cp(sem, limit=VMEM_LIMIT):
    return pltpu.CompilerParams(dimension_semantics=sem, vmem_limit_bytes=limit)


def _sigmoid(x):
    return 1.0 / (1.0 + jnp.exp(-x))


def _silu(x):
    return x * _sigmoid(x)


def _dsilu(x):
    s = _sigmoid(x)
    return s * (1.0 + x * (1.0 - s))


def _log1p(x):
    u = 1.0 + x
    d = jnp.where(u == 1.0, 1.0, u - 1.0)
    return jnp.where(u == 1.0, x, jnp.log(u) * (x / d))


def _softplus(x):
    return jnp.maximum(x, 0.0) + _log1p(jnp.exp(-jnp.abs(x)))


def _expm1(x):
    u = jnp.exp(x)
    lu = jnp.log(u)
    safe = jnp.where((u == 1.0) | (lu == 0.0), 1.0, lu)
    r = (u - 1.0) * (x / safe)
    r = jnp.where(u == 1.0, x, r)
    return jnp.where(u - 1.0 == -1.0, -1.0, r)


def _nn(a, b):
    return lax.dot_general(a, b, (((1,), (0,)), ((), ())), preferred_element_type=F32)


def _nt(a, b):
    return lax.dot_general(a, b, (((1,), (1,)), ((), ())), preferred_element_type=F32)


def _tn(a, b):
    return lax.dot_general(a, b, (((0,), (0,)), ((), ())), preferred_element_type=F32)


def _b(x):
    return x.astype(BF16)


def _split3(x):
    hi = x.astype(BF16)
    r1 = x - hi.astype(F32)
    mid = r1.astype(BF16)
    lo = (r1 - mid.astype(F32)).astype(BF16)
    return hi, mid, lo


def _dot3(f, a, b):
    ah, am, _ = _split3(a)
    bh, bm, _ = _split3(b)
    return f(ah, bh) + (f(ah, bm) + f(am, bh))


def _dot_exact_lhs(f, a_bf16, b):
    bh, bm, bl = _split3(b)
    return f(a_bf16, bh) + (f(a_bf16, bm) + f(a_bf16, bl))


def _rot(x):
    w = x.shape[-1]
    lane = lax.broadcasted_iota(jnp.int32, (1, w), 1)
    return jnp.where((lane & 63) < 32, pltpu.roll(x, w - 32, 1), pltpu.roll(x, 32, 1))


def _conv_taps(ext, n):
    return [pltpu.roll(ext, 3 - k, 0)[8:8 + n] if k < 3 else ext[8:8 + n] for k in range(CONV_WIDTH)]


def _conv_taps_t(ext, n):
    m = ext.shape[0]
    return [pltpu.roll(ext, m - (3 - k), 0)[0:n] if k < 3 else ext[0:n] for k in range(CONV_WIDTH)]


def _scan_lin(a, b, reverse):
    n = a.shape[0]
    row = lax.broadcasted_iota(jnp.int32, (n, 1), 0)
    s = 1
    while s < n:
        if reverse:
            a_sh = pltpu.roll(a, n - s, 0)
            b_sh = pltpu.roll(b, n - s, 0)
            ok = row < (n - s)
        else:
            a_sh = pltpu.roll(a, s, 0)
            b_sh = pltpu.roll(b, s, 0)
            ok = row >= s
        b = jnp.where(ok, a * b_sh + b, b)
        a = jnp.where(ok, a * a_sh, a)
        s *= 2
    return a, b


def _matmul(a, b, *, ta, tb, tm, tn, tk, out_dtype, name, extra=None, alpha=0.0, out_blocks=None):
    if ta:
        K, M = a.shape
    else:
        M, K = a.shape
    if tb:
        N, K2 = b.shape
    else:
        K2, N = b.shape
    assert K == K2 and M % tm == 0 and N % tn == 0 and K % tk == 0, (a.shape, b.shape, tm, tn, tk)
    nk = K // tk
    ca = 0 if ta else 1
    cb = 1 if tb else 0
    has_extra = extra is not None

    def body(*refs):
        if has_extra:
            a_ref, b_ref, e_ref, o_ref, acc = refs
        else:
            a_ref, b_ref, o_ref, acc = refs
            e_ref = None
        k = pl.program_id(2)

        @pl.when(k == 0)
        def _():
            acc[...] = jnp.zeros_like(acc)

        acc[...] += lax.dot_general(a_ref[...], b_ref[...], (((ca,), (cb,)), ((), ())),
                                    preferred_element_type=F32)

        @pl.when(k == nk - 1)
        def _():
            r = acc[...]
            if e_ref is not None:
                r = r + alpha * e_ref[...]
            o_ref[...] = r.astype(o_ref.dtype)

    a_spec = (pl.BlockSpec((tk, tm), lambda i, j, k: (k, i)) if ta
              else pl.BlockSpec((tm, tk), lambda i, j, k: (i, k)))
    b_spec = (pl.BlockSpec((tn, tk), lambda i, j, k: (j, k)) if tb
              else pl.BlockSpec((tk, tn), lambda i, j, k: (k, j)))
    e_spec = pl.BlockSpec((tm, tn), lambda i, j, k: (i, j))
    if out_blocks is None:
        o_spec, o_shape = e_spec, (M, N)
    else:
        o_shape, o_block, o_map = out_blocks
        o_spec = pl.BlockSpec(o_block, lambda i, j, k: o_map(i, j))
    in_specs = [a_spec, b_spec] + ([e_spec] if has_extra else [])
    args = (a, b) + ((extra,) if has_extra else ())
    return _pcall(
        body, name=name, grid=(M // tm, N // tn, nk),
        in_specs=in_specs, out_specs=o_spec,
        out_shape=jax.ShapeDtypeStruct(o_shape, out_dtype),
        scratch_shapes=[pltpu.VMEM((tm, tn), F32)],
        compiler_params=_cp(("parallel", "parallel", "arbitrary")),
    )(*args)


def _outproj(ymix, wo, x, *, tm, name):
    S_, D = x.shape

    def body(y_ref, w_ref, x_ref, z_ref):
        z_ref[...] = DEEPNORM_ALPHA * x_ref[...] + _nn(y_ref[...], w_ref[...])

    return _pcall(
        body, name=name, grid=(S_ // tm,),
        in_specs=[pl.BlockSpec((tm, MIX_WIDTH), lambda i: (i, 0)),
                  pl.BlockSpec((MIX_WIDTH, D), lambda i: (0, 0)),
                  pl.BlockSpec((tm, D), lambda i: (i, 0))],
        out_specs=pl.BlockSpec((tm, D), lambda i: (i, 0)),
        out_shape=jax.ShapeDtypeStruct((S_, D), F32),
        compiler_params=_cp(("parallel",)),
    )(ymix, wo, x)


def _ln_stats(z):
    mu = jnp.mean(z, -1, keepdims=True)
    zc = z - mu
    var = jnp.mean(zc * zc, -1, keepdims=True)
    rstd = lax.rsqrt(var + LN_EPS)
    return zc * rstd, rstd


def _ln_fwd(z, g, b, *, tm, name):
    S_, D = z.shape

    def body(z_ref, g_ref, b_ref, y_ref, yb_ref):
        xh, _ = _ln_stats(z_ref[...])
        y = xh * g_ref[...] + b_ref[...]
        y_ref[...] = y
        yb_ref[...] = y.astype(BF16)

    row = pl.BlockSpec((tm, D), lambda i: (i, 0))
    vec = pl.BlockSpec((1, D), lambda i: (0, 0))
    return _pcall(
        body, name=name, grid=(S_ // tm,), in_specs=[row, vec, vec], out_specs=[row, row],
        out_shape=[jax.ShapeDtypeStruct((S_, D), F32), jax.ShapeDtypeStruct((S_, D), BF16)],
        compiler_params=_cp(("parallel",)),
    )(z, g.reshape(1, D), b.reshape(1, D))


def _ln_bwd(z, g, b, other, *, from_target, tm, name):
    S_, D = z.shape

    def body(z_ref, g_ref, b_ref, o_ref, dz_ref, dzb_ref, dg_ref, db_ref, loss_ref):
        i = pl.program_id(0)

        @pl.when(i == 0)
        def _():
            dg_ref[...] = jnp.zeros_like(dg_ref)
            db_ref[...] = jnp.zeros_like(db_ref)
            loss_ref[...] = jnp.zeros_like(loss_ref)

        xh, rstd = _ln_stats(z_ref[...])
        gam = g_ref[...]
        if from_target:
            err = xh * gam + b_ref[...] - o_ref[...]
            per_tok = jnp.mean(err * err, -1, keepdims=True)
            loss_ref[...] += 0.5 * jnp.sum(per_tok, 0, keepdims=True)
            dy = err * (1.0 / D)
        else:
            dy = o_ref[...]
        dxh = dy * gam
        m1 = jnp.mean(dxh, -1, keepdims=True)
        m2 = jnp.mean(dxh * xh, -1, keepdims=True)
        dz = rstd * (dxh - m1 - xh * m2)
        dz_ref[...] = dz
        dzb_ref[...] = dz.astype(BF16)
        dg_ref[...] += jnp.sum(dy * xh, 0, keepdims=True)
        db_ref[...] += jnp.sum(dy, 0, keepdims=True)

    row = pl.BlockSpec((tm, D), lambda i: (i, 0))
    vec = pl.BlockSpec((1, D), lambda i: (0, 0))
    one = pl.BlockSpec((1, 1), lambda i: (0, 0))
    return _pcall(
        body, name=name, grid=(S_ // tm,), in_specs=[row, vec, vec, row],
        out_specs=[row, row, vec, vec, one],
        out_shape=[jax.ShapeDtypeStruct((S_, D), F32), jax.ShapeDtypeStruct((S_, D), BF16),
                   jax.ShapeDtypeStruct((1, D), F32), jax.ShapeDtypeStruct((1, D), F32),
                   jax.ShapeDtypeStruct((1, 1), F32)],
        compiler_params=_cp(("arbitrary",)),
    )(z, g.reshape(1, D), b.reshape(1, D), other)


def _attn_common(i, T, b, h, qr, kd, vd, sk_ref):
    lane = lax.broadcasted_iota(jnp.int32, (1, 128), 1)
    lof = (lane < 64).astype(F32)
    hif = 1.0 - lof
    r0 = b * 128
    ri = lax.broadcasted_iota(jnp.int32, (512, 256), 0)
    cj = lax.broadcasted_iota(jnp.int32, (512, 256), 1)
    diff = (ri & 127) - cj + 128
    valid = (diff >= 0) & (diff < WINDOW) & ((i * T + r0 - 128 + cj) >= 0)
    grp = lax.broadcasted_iota(jnp.int32, (512, 1), 0) >> 7
    skv = jnp.zeros((512, 1), F32)
    for g in range(4):
        skv = jnp.where(grp == g, sk_ref[h * 4 + g], skv)
    pairs = [qr[r0:r0 + 128, h * 256 + p * 128:h * 256 + (p + 1) * 128] for p in (0, 1)]
    qs = _b(jnp.concatenate([pairs[0] * lof, pairs[0] * hif, pairs[1] * lof, pairs[1] * hif], 0))
    k2 = kd[h][r0:r0 + 256]
    v2 = vd[h][r0:r0 + 256]
    s = jnp.where(valid, _nt(qs, k2) * (A_HEAD_DIM ** -0.5), -jnp.inf)
    m = jnp.maximum(jnp.max(s, 1, keepdims=True), skv)
    p = jnp.exp(s - m)
    esk = jnp.exp(skv - m)
    rz = 1.0 / (jnp.sum(p, 1, keepdims=True) + esk)
    prob = p * rz
    o4 = _nn(_b(prob), v2)
    return lof, hif, qs, k2, v2, prob, esk * rz, o4


def _attn_prep(T, q_ref, k_ref, v_ref, c_ref, s_ref, kprev, vprev):
    C = c_ref[...]
    Sg = s_ref[...]
    C4 = jnp.concatenate([C] * 4, 1)
    S4 = jnp.concatenate([Sg] * 4, 1)
    q = q_ref[...]
    qr = q * C4 + _rot(q) * S4
    k = k_ref[...]
    kr = k * C + _rot(k) * Sg
    v = v_ref[...]
    kext = jnp.concatenate([kprev[...], kr], 0)
    vext = jnp.concatenate([vprev[...], v], 0)
    kprev[...] = kr[T - 128:]
    vprev[...] = v[T - 128:]
    lo = lax.broadcasted_iota(jnp.int32, (1, 128), 1) < 64
    kroll = pltpu.roll(kext, 64, 1)
    vroll = pltpu.roll(vext, 64, 1)
    kd = [_b(jnp.where(lo, kext, kroll)), _b(jnp.where(lo, kroll, kext))]
    vd = [_b(jnp.where(lo, vext, vroll)), _b(jnp.where(lo, vroll, vext))]
    return C, Sg, C4, S4, qr, kd, vd


def _attn_specs(T):
    return [pl.BlockSpec(memory_space=pltpu.SMEM),
            pl.BlockSpec((T, 512), lambda i: (i, OFF_AQ // 512)),
            pl.BlockSpec((T, 512), lambda i: (i, OFF_AZ // 512)),
            pl.BlockSpec((T, 128), lambda i: (i, OFF_AK // 128)),
            pl.BlockSpec((T, 128), lambda i: (i, OFF_AV // 128)),
            pl.BlockSpec((T, 128), lambda i: (i, 0)),
            pl.BlockSpec((T, 128), lambda i: (i, 0))]


def _attn_fwd(proj, rope_c, rope_s, sinks, *, T, name):
    S_ = proj.shape[0]
    nb = T // 128

    def body(sk_ref, q_ref, z_ref, k_ref, v_ref, c_ref, s_ref, y_ref, kprev, vprev):
        i = pl.program_id(0)

        @pl.when(i == 0)
        def _():
            kprev[...] = jnp.zeros_like(kprev)
            vprev[...] = jnp.zeros_like(vprev)

        _, _, _, _, qr, kd, vd = _attn_prep(T, q_ref, k_ref, v_ref, c_ref, s_ref, kprev, vprev)
        for b in range(nb):
            r0 = b * 128
            for h in range(2):
                lof, hif, _, _, _, _, _, o4 = _attn_common(i, T, b, h, qr, kd, vd, sk_ref)
                for p in range(2):
                    cs = slice(h * 256 + p * 128, h * 256 + (p + 1) * 128)
                    o = o4[2 * p * 128:(2 * p + 1) * 128] * lof + o4[(2 * p + 1) * 128:(2 * p + 2) * 128] * hif
                    y_ref[r0:r0 + 128, cs] = (o * _silu(z_ref[r0:r0 + 128, cs])).astype(BF16)

    return _pcall(
        body, name=name, grid=(S_ // T,), in_specs=_attn_specs(T),
        out_specs=pl.BlockSpec((T, 512), lambda i: (i, 0)),
        out_shape=jax.ShapeDtypeStruct((S_, 512), BF16),
        scratch_shapes=[pltpu.VMEM((128, 128), F32), pltpu.VMEM((128, 128), F32)],
        compiler_params=_cp(("arbitrary",)),
    )(sinks, proj, proj, proj, proj, rope_c, rope_s)


def _attn_bwd(proj, rope_c, rope_s, sinks, dymix, *, T, name):
    S_ = proj.shape[0]
    nb = T // 128
    nt = S_ // T

    def body(sk_ref, q_ref, z_ref, k_ref, v_ref, c_ref, s_ref, dy_ref,
             dq_ref, dz_ref, dk_ref, dv_ref, dkt_ref, dvt_ref, dsk_ref,
             kprev, vprev, cprev, sprev, dkacc, dvacc, dqacc):
        i = pl.program_id(0)

        @pl.when(i == 0)
        def _():
            kprev[...] = jnp.zeros_like(kprev)
            vprev[...] = jnp.zeros_like(vprev)
            cprev[...] = jnp.zeros_like(cprev)
            sprev[...] = jnp.zeros_like(sprev)
            dkacc[...] = jnp.zeros_like(dkacc)
            dvacc[...] = jnp.zeros_like(dvacc)
            dsk_ref[...] = jnp.zeros_like(dsk_ref)

        @pl.when(i > 0)
        def _():
            dkacc[0:128, :] = dkacc[T:T + 128, :]
            dvacc[0:128, :] = dvacc[T:T + 128, :]
            dkacc[128:, :] = jnp.zeros((T, 128), F32)
            dvacc[128:, :] = jnp.zeros((T, 128), F32)

        C, Sg, C4, S4, qr, kd, vd = _attn_prep(T, q_ref, k_ref, v_ref, c_ref, s_ref, kprev, vprev)
        lane = lax.broadcasted_iota(jnp.int32, (1, 128), 1)
        for b in range(nb):
            r0 = b * 128
            for h in range(2):
                lof, hif, qs, k2, v2, prob, psink, o4 = _attn_common(i, T, b, h, qr, kd, vd, sk_ref)
                dos = []
                for p in range(2):
                    cs = slice(h * 256 + p * 128, h * 256 + (p + 1) * 128)
                    o = o4[2 * p * 128:(2 * p + 1) * 128] * lof + o4[(2 * p + 1) * 128:(2 * p + 2) * 128] * hif
                    zc = z_ref[r0:r0 + 128, cs]
                    dyc = dy_ref[r0:r0 + 128, cs]
                    dz_ref[r0:r0 + 128, cs] = (dyc * o * _dsilu(zc)).astype(BF16)
                    do = dyc * _silu(zc)
                    dos += [do * lof, do * hif]
                dos = jnp.concatenate(dos, 0)
                os_ = jnp.concatenate([o4[0:128] * lof, o4[128:256] * hif, o4[256:384] * lof, o4[384:512] * hif], 0)
                delta = jnp.sum(dos * os_, 1, keepdims=True)
                dosb = _b(dos)
                dp = _nt(dosb, v2)
                ds = prob * (dp - delta)
                dsv = -psink * delta
                for g in range(4):
                    sg = jnp.sum(dsv[g * 128:(g + 1) * 128], 0, keepdims=True)
                    hd = h * 4 + g
                    dsk_ref[hd:hd + 1, :] += jnp.broadcast_to(sg, (1, 128))
                dsb = _b(ds * (A_HEAD_DIM ** -0.5))
                dqs = _nn(dsb, k2)
                for p in range(2):
                    cs = slice(h * 256 + p * 128, h * 256 + (p + 1) * 128)
                    dqacc[r0:r0 + 128, cs] = (dqs[2 * p * 128:(2 * p + 1) * 128] * lof
                                              + dqs[(2 * p + 1) * 128:(2 * p + 2) * 128] * hif)
                dkdup = _tn(dsb, qs)
                dvdup = _tn(_b(prob), dosb)
                half = (lane < 64) if h == 0 else (lane >= 64)
                dkacc[r0:r0 + 256, :] += jnp.where(half, dkdup + pltpu.roll(dkdup, 64, 1), 0.0)
                dvacc[r0:r0 + 256, :] += jnp.where(half, dvdup + pltpu.roll(dvdup, 64, 1), 0.0)
        dqr = dqacc[...]
        dq_ref[...] = (dqr * C4 + _rot(dqr * S4)).astype(BF16)
        cext = jnp.concatenate([cprev[...], C], 0)
        sext = jnp.concatenate([sprev[...], Sg], 0)
        dke = dkacc[...]
        dkp = dke * cext + _rot(dke * sext)
        dk_ref[...] = dkp[0:T].astype(BF16)
        dkt_ref[...] = dkp[T:T + 128].astype(BF16)
        dve = dvacc[...]
        dv_ref[...] = dve[0:T].astype(BF16)
        dvt_ref[...] = dve[T:T + 128].astype(BF16)
        cprev[...] = C[T - 128:]
        sprev[...] = Sg[T - 128:]

    wide = pl.BlockSpec((T, 512), lambda i: (i, 0))
    nar = pl.BlockSpec((T, 128), lambda i: (i, 0))
    tail = pl.BlockSpec((128, 128), lambda i: (0, 0))
    return _pcall(
        body, name=name, grid=(nt,),
        in_specs=_attn_specs(T) + [pl.BlockSpec((T, 512), lambda i: (i, MIX_A // 512))],
        out_specs=[wide, wide, nar, nar, tail, tail, pl.BlockSpec((8, 128), lambda i: (0, 0))],
        out_shape=[jax.ShapeDtypeStruct((S_, 512), BF16), jax.ShapeDtypeStruct((S_, 512), BF16),
                   jax.ShapeDtypeStruct((S_, 128), BF16), jax.ShapeDtypeStruct((S_, 128), BF16),
                   jax.ShapeDtypeStruct((128, 128), BF16), jax.ShapeDtypeStruct((128, 128), BF16),
                   jax.ShapeDtypeStruct((8, 128), F32)],
        scratch_shapes=[pltpu.VMEM((128, 128), F32)] * 4
        + [pltpu.VMEM((T + 128, 128), F32), pltpu.VMEM((T + 128, 128), F32), pltpu.VMEM((T, 512), F32)],
        compiler_params=_cp(("arbitrary",)),
    )(sinks, proj, proj, proj, proj, rope_c, rope_s, dymix)


def _rg_gates(xr, wa_ref, ba_ref, wx_ref, bx_ref, lam_ref):
    xb = _b(xr)
    pre_a = jnp.concatenate([_nn(xb[:, n * 128:(n + 1) * 128], wa_ref[n]) for n in range(R_BLOCKS)], 1) + ba_ref[...]
    pre_x = jnp.concatenate([_nn(xb[:, n * 128:(n + 1) * 128], wx_ref[n]) for n in range(R_BLOCKS)], 1) + bx_ref[...]
    r = _sigmoid(pre_a)
    ig = _sigmoid(pre_x)
    sp = _softplus(-lam_ref[...])
    log_a = -R_C * r * sp
    a = jnp.exp(log_a)
    mult = jnp.sqrt(-_expm1(2.0 * log_a))
    return xb, r, ig, sp, a, mult


def _rg_param_specs():
    C = R_WIDTH
    vec = pl.BlockSpec((1, C), lambda i: (0, 0))
    blk = pl.BlockSpec((R_BLOCKS, 128, 128), lambda i: (0, 0, 0))
    return [pl.BlockSpec((CONV_WIDTH, C), lambda i: (0, 0)), vec, blk, vec, blk, vec, vec]


def _rglru_fwd(proj, cw, cb, wa, ba, wx, bx, lam, *, T, name):
    S_ = proj.shape[0]
    C = R_WIDTH

    def body(rx_ref, rz_ref, cw_ref, cb_ref, wa_ref, ba_ref, wx_ref, bx_ref, lam_ref,
             h_ref, y_ref, halo, hcar):
        i = pl.program_id(0)

        @pl.when(i == 0)
        def _():
            halo[...] = jnp.zeros_like(halo)
            hcar[...] = jnp.zeros_like(hcar)

        rx = rx_ref[...]
        ext = jnp.concatenate([halo[...], rx], 0)
        halo[...] = rx[T - 8:]
        taps = _conv_taps(ext, T)
        xr = cb_ref[...] + sum(cw_ref[k:k + 1, :] * taps[k] for k in range(CONV_WIDTH))
        _, _, ig, _, a, mult = _rg_gates(xr, wa_ref, ba_ref, wx_ref, bx_ref, lam_ref)
        u = mult * (ig * xr)
        acum, hloc = _scan_lin(a, u, False)
        h = hloc + acum * hcar[0:1, :]
        hcar[...] = jnp.broadcast_to(h[T - 1:T, :], (8, C))
        h_ref[...] = h
        y_ref[...] = (h * _silu(rz_ref[...])).astype(BF16)

    row = pl.BlockSpec((T, C), lambda i: (i, 0))
    return _pcall(
        body, name=name, grid=(S_ // T,),
        in_specs=[pl.BlockSpec((T, C), lambda i: (i, OFF_RX // C)),
                  pl.BlockSpec((T, C), lambda i: (i, OFF_RZ // C))] + _rg_param_specs(),
        out_specs=[row, row],
        out_shape=[jax.ShapeDtypeStruct((S_, C), F32), jax.ShapeDtypeStruct((S_, C), BF16)],
        scratch_shapes=[pltpu.VMEM((8, C), F32), pltpu.VMEM((8, C), F32)],
        compiler_params=_cp(("arbitrary",)),
    )(proj, proj, cw, cb.reshape(1, C), _b(wa), ba.reshape(1, C), _b(wx), bx.reshape(1, C), lam.reshape(1, C))


def _rglru_bwd(proj, h, dymix, cw, cb, wa, ba, wx, bx, lam, *, T, name):
    S_ = proj.shape[0]
    C = R_WIDTH
    nt = S_ // T
    t8 = T // 8

    def body(rx_ref, rxp_ref, rz_ref, h_ref, hp_ref, dy_ref,
             cw_ref, cb_ref, wa_ref, ba_ref, wx_ref, bx_ref, lam_ref, wat_ref, wxt_ref,
             drx_ref, drz_ref, dcw_ref, dcb_ref, dwa_ref, dba_ref, dwx_ref, dbx_ref, dlam_ref,
             afirst, gfirst, dhalo):
        i = pl.program_id(0)
        first_tile = (i == nt - 1)

        @pl.when(i == 0)
        def _():
            afirst[...] = jnp.zeros_like(afirst)
            gfirst[...] = jnp.zeros_like(gfirst)
            dhalo[...] = jnp.zeros_like(dhalo)
            for r in (dcw_ref, dcb_ref, dwa_ref, dba_ref, dwx_ref, dbx_ref, dlam_ref):
                r[...] = jnp.zeros_like(r)

        keep = jnp.where(first_tile, 0.0, 1.0)
        rx = rx_ref[...]
        ext = jnp.concatenate([rxp_ref[...] * keep, rx], 0)
        taps = _conv_taps(ext, T)
        xr = cb_ref[...] + sum(cw_ref[k:k + 1, :] * taps[k] for k in range(CONV_WIDTH))
        xb, r, ig, sp, a, mult = _rg_gates(xr, wa_ref, ba_ref, wx_ref, bx_ref, lam_ref)
        hh = h_ref[...]
        rz = rz_ref[...]
        dy = dy_ref[...]
        drz_ref[...] = (dy * hh * _dsilu(rz)).astype(BF16)
        dh = dy * _silu(rz)
        row = lax.broadcasted_iota(jnp.int32, (T, 1), 0)
        c = jnp.where(row == T - 1, afirst[0:1, :], pltpu.roll(a, T - 1, 0))
        ccum, gloc = _scan_lin(c, dh, True)
        g = gloc + ccum * gfirst[0:1, :]
        afirst[...] = jnp.broadcast_to(a[0:1, :], (8, C))
        gfirst[...] = jnp.broadcast_to(g[0:1, :], (8, C))
        hprev = jnp.where(row == 0, hp_ref[7:8, :] * keep, pltpu.roll(hh, 1, 0))
        da = g * hprev
        gx = ig * xr
        dgx = g * mult
        dmult = g * gx
        dlog_a = da * a - dmult * (a * a) / mult
        dpre_a = dlog_a * (-R_C * sp) * r * (1.0 - r)
        dpre_x = dgx * xr * ig * (1.0 - ig)
        dlam_ref[...] += jnp.sum(dlog_a * (-R_C * r), 0, keepdims=True) * (-_sigmoid(-lam_ref[...]))
        dab = _b(dpre_a)
        dxb = _b(dpre_x)
        dxr = dgx * ig + jnp.concatenate(
            [_nn(dab[:, n * 128:(n + 1) * 128], wat_ref[n]) + _nn(dxb[:, n * 128:(n + 1) * 128], wxt_ref[n])
             for n in range(R_BLOCKS)], 1)
        for n in range(R_BLOCKS):
            cs = slice(n * 128, (n + 1) * 128)
            dwa_ref[n] += _tn(xb[:, cs], dab[:, cs])
            dwx_ref[n] += _tn(xb[:, cs], dxb[:, cs])
        dba_ref[...] += jnp.sum(dpre_a, 0, keepdims=True)
        dbx_ref[...] += jnp.sum(dpre_x, 0, keepdims=True)
        dcb_ref[...] += jnp.sum(dxr, 0, keepdims=True)
        for k in range(CONV_WIDTH):
            dcw_ref[k:k + 1, :] += jnp.sum(dxr * taps[k], 0, keepdims=True)
        ext2 = jnp.concatenate([dxr, dhalo[...]], 0)
        tt = _conv_taps_t(ext2, T)
        drx_ref[...] = sum(cw_ref[k:k + 1, :] * tt[k] for k in range(CONV_WIDTH)).astype(BF16)
        dhalo[...] = dxr[0:8]

    def rev(i):
        return nt - 1 - i

    def prev8(i):
        return jnp.maximum(rev(i) * t8 - 1, 0)

    vec = pl.BlockSpec((1, C), lambda i: (0, 0))
    blk = pl.BlockSpec((R_BLOCKS, 128, 128), lambda i: (0, 0, 0))
    row = pl.BlockSpec((T, C), lambda i: (rev(i), 0))
    wat = _b(jnp.swapaxes(wa, 1, 2))
    wxt = _b(jnp.swapaxes(wx, 1, 2))
    return _pcall(
        body, name=name, grid=(nt,),
        in_specs=[pl.BlockSpec((T, C), lambda i: (rev(i), OFF_RX // C)),
                  pl.BlockSpec((8, C), lambda i: (prev8(i), OFF_RX // C)),
                  pl.BlockSpec((T, C), lambda i: (rev(i), OFF_RZ // C)),
                  row,
                  pl.BlockSpec((8, C), lambda i: (prev8(i), 0)),
                  pl.BlockSpec((T, C), lambda i: (rev(i), MIX_R // C)),
                  ] + _rg_param_specs() + [blk, blk],
        out_specs=[row, row, pl.BlockSpec((CONV_WIDTH, C), lambda i: (0, 0)), vec, blk, vec, blk, vec, vec],
        out_shape=[jax.ShapeDtypeStruct((S_, C), BF16), jax.ShapeDtypeStruct((S_, C), BF16),
                   jax.ShapeDtypeStruct((CONV_WIDTH, C), F32), jax.ShapeDtypeStruct((1, C), F32),
                   jax.ShapeDtypeStruct((R_BLOCKS, 128, 128), F32), jax.ShapeDtypeStruct((1, C), F32),
                   jax.ShapeDtypeStruct((R_BLOCKS, 128, 128), F32), jax.ShapeDtypeStruct((1, C), F32),
                   jax.ShapeDtypeStruct((1, C), F32)],
        scratch_shapes=[pltpu.VMEM((8, C), F32)] * 3,
        compiler_params=_cp(("arbitrary",)),
    )(proj, proj, proj, h, h, dymix, cw, cb.reshape(1, C), _b(wa), ba.reshape(1, C), _b(wx), bx.reshape(1, C),
      lam.reshape(1, C), wat, wxt)


GW3 = 3 * G_WIDTH


def _lane_col(x, lane_idx):
    lane = lax.broadcasted_iota(jnp.int32, (1, x.shape[1]), 1)
    return jnp.sum(jnp.where(lane == lane_idx, x, 0.0), 1, keepdims=True)


def _gdn_pre(ext, T, cw_ref, gba, pv_ref):
    taps = _conv_taps(ext, T)
    c = sum(cw_ref[k:k + 1, :] * taps[k] for k in range(CONV_WIDTH))
    qkv = _silu(c)
    beta = _sigmoid(gba)
    sarg = gba + pv_ref[1:2, :]
    nea = -jnp.exp(pv_ref[0:1, :])
    gdec = nea * _softplus(sarg)
    ri = lax.broadcasted_iota(jnp.int32, (T, T), 0)
    cj = lax.broadcasted_iota(jnp.int32, (T, T), 1)
    same = (ri >> 6) == (cj >> 6)
    ltri = jnp.where((ri >= cj) & same, 1.0, 0.0).astype(BF16)
    gc = _dot_exact_lhs(_nn, ltri, gdec)
    return taps, c, qkv, beta, sarg, nea, gdec, gc


def _gdn_masks():
    ri = lax.broadcasted_iota(jnp.int32, (128, 128), 0)
    cj = lax.broadcasted_iota(jnp.int32, (128, 128), 1)
    same = (ri >> 6) == (cj >> 6)
    return (ri >= cj) & same, (ri > cj) & same, ri == cj


def _lockstep(gens):
    out = [None] * len(gens)
    live = list(range(len(gens)))
    while live:
        still = []
        for k in live:
            try:
                next(gens[k])
                still.append(k)
            except StopIteration as stop:
                out[k] = stop.value
        live = still
    return out


def _gdn_chunk(qkv, beta, gc, rs, h, tm=None):
    tril, strict, eye = _gdn_masks()
    rowi = lax.broadcasted_iota(jnp.int32, (128, 1), 0)
    lane = lax.broadcasted_iota(jnp.int32, (1, 128), 1)
    qh = qkv[rs, h * 128:(h + 1) * 128]
    kh = qkv[rs, 512 + h * 128:512 + (h + 1) * 128]
    vh = qkv[rs, 1024 + h * 128:1024 + (h + 1) * 128]
    rq = lax.rsqrt(jnp.sum(qh * qh, 1, keepdims=True) + RMS_EPS)
    rk = lax.rsqrt(jnp.sum(kh * kh, 1, keepdims=True) + RMS_EPS)
    qn = qh * (rq * (G_HEAD_DIM ** -0.5))
    kn = kh * rk
    gcb = gc[rs]
    gcol = _lane_col(gcb, 4 + h)
    bcol = _lane_col(beta[rs], h)
    grow = _dot_exact_lhs(_nt, jnp.ones((128, 128), BF16), jnp.where(lane == 4 + h, gcb, 0.0))
    D = jnp.where(tril, jnp.exp(jnp.minimum(gcol - grow, 0.0)), 0.0)
    kb = kn * bcol
    vb = vh * bcol
    knb = _b(kn)
    A = _nt(_b(kb), knb)
    Bm = _nt(_b(qn), knb)
    yield
    if tm is None:
        N = jnp.where(strict, -(A * D), 0.0)
        tm = jnp.where(eye, 1.0, 0.0) + N
        npow = N
        for _ in range(5):
            npow = _dot3(_nn, npow, npow)
            yield
            tm = tm + _dot3(_nn, tm, npow)
            yield
    eg = jnp.exp(gcol)
    u = _dot3(_nn, tm, vb)
    w = _dot3(_nn, tm, kb * eg)
    yield
    qk = jnp.where(tril, Bm * D, 0.0)
    qd = qn * eg
    gla = jnp.sum(jnp.where(rowi == 63, gcol, 0.0), 0, keepdims=True)
    glb = jnp.sum(jnp.where(rowi == 127, gcol, 0.0), 0, keepdims=True)
    ed = jnp.exp(jnp.where(rowi < 64, gla, glb) - gcol)
    kd = kn * ed
    return dict(qh=qh, kh=kh, vh=vh, rq=rq, rk=rk, qn=qn, kn=kn, gcol=gcol, bcol=bcol, D=D, A=A, Bm=Bm,
                tm=tm, eg=eg, ed=ed, u=u, w=w, qk=qk, qd=qd, kd=kd, kb=kb, vb=vb,
                gla=jnp.exp(gla), glb=jnp.exp(glb))


def _gdn_scan(q, sa):
    sab = _b(sa)
    wb = _b(q["w"])
    vna = q["u"] - _nn(wb, sab)
    yield
    sb = sa * q["gla"] + _tn(_b(q["kd"][0:64]), _b(vna[0:64]))
    yield
    sbb = _b(sb)
    vnb = q["u"] - _nn(wb, sbb)
    yield
    sn = sb * q["glb"] + _tn(_b(q["kd"][64:128]), _b(vnb[64:128]))
    yield
    vn = jnp.concatenate([vna[0:64], vnb[64:128]], 0)
    qdb = _b(q["qd"])
    o = jnp.concatenate([_nn(qdb[0:64], sab), _nn(qdb[64:128], sbb)], 0) + _nn(_b(q["qk"]), _b(vn))
    return sb, sn, vn, o


def _gdn_param_specs():
    return [pl.BlockSpec((CONV_WIDTH, GW3), lambda i: (0, 0)),
            pl.BlockSpec((8, 128), lambda i: (0, 0)),
            pl.BlockSpec((1, 128), lambda i: (0, 0))]


def _gdn_pvec(a_log, dt_bias):
    z = jnp.zeros((8, 128), F32)
    return z.at[0, 4:8].set(a_log).at[1, 4:8].set(dt_bias)


def _gdn_fwd(proj, cw, a_log, dt_bias, nw, *, T, name):
    S_ = proj.shape[0]
    nu = T // 128

    def body(x_ref, z_ref, g_ref, cw_ref, pv_ref, nw_ref, y_ref, st_ref, tm_ref, halo, state):
        i = pl.program_id(0)

        @pl.when(i == 0)
        def _():
            halo[...] = jnp.zeros_like(halo)
            state[...] = jnp.zeros_like(state)

        x = x_ref[...]
        ext = jnp.concatenate([halo[...], x], 0)
        halo[...] = x[T - 8:]
        _, _, qkv, beta, _, _, _, gc = _gdn_pre(ext, T, cw_ref, g_ref[...], pv_ref)
        items = [(dc, h) for dc in range(nu) for h in range(G_HEADS)]
        qs = _lockstep([_gdn_chunk(qkv, beta, gc, slice(dc * 128, (dc + 1) * 128), h) for dc, h in items])

        def head_chain(h):
            s = state[h]
            for dc in range(nu):
                rs = slice(dc * 128, (dc + 1) * 128)
                q = qs[dc * G_HEADS + h]
                sb, sn, _, o = yield from _gdn_scan(q, s)
                st_ref[2 * dc, h] = s
                st_ref[2 * dc + 1, h] = sb
                tm_ref[dc, h] = q["tm"]
                s = sn
                yield
                rn = lax.rsqrt(jnp.mean(o * o, 1, keepdims=True) + RMS_EPS)
                cs = slice(h * 128, (h + 1) * 128)
                y_ref[rs, cs] = (o * rn * nw_ref[...] * _silu(z_ref[rs, cs])).astype(BF16)
                yield
            state[h] = s

        _lockstep([head_chain(h) for h in range(G_HEADS)])

    return _pcall(
        body, name=name, grid=(S_ // T,),
        in_specs=[pl.BlockSpec((T, GW3), lambda i: (i, OFF_GQKV // GW3)),
                  pl.BlockSpec((T, 512), lambda i: (i, OFF_GZ // 512)),
                  pl.BlockSpec((T, 128), lambda i: (i, OFF_GBA // 128))] + _gdn_param_specs(),
        out_specs=[pl.BlockSpec((T, 512), lambda i: (i, 0)),
                   pl.BlockSpec((2 * nu, G_HEADS, 128, 128), lambda i: (i, 0, 0, 0)),
                   pl.BlockSpec((nu, G_HEADS, 128, 128), lambda i: (i, 0, 0, 0))],
        out_shape=[jax.ShapeDtypeStruct((S_, 512), BF16),
                   jax.ShapeDtypeStruct((S_ // 64, G_HEADS, 128, 128), F32),
                   jax.ShapeDtypeStruct((S_ // 128, G_HEADS, 128, 128), F32)],
        scratch_shapes=[pltpu.VMEM((8, GW3), F32), pltpu.VMEM((G_HEADS, 128, 128), F32)],
        compiler_params=_cp(("arbitrary",)),
    )(proj, proj, proj, cw, _gdn_pvec(a_log, dt_bias), nw.reshape(1, 128))


def _gdn_bwd(proj, states, tms, dymix, cw, a_log, dt_bias, nw, *, T, name):
    S_ = proj.shape[0]
    nt = S_ // T
    nu = T // 128
    t8 = T // 8

    def body(x_ref, xp_ref, z_ref, g_ref, st_ref, tm_ref, dy_ref, cw_ref, pv_ref, nw_ref,
             dx_ref, dz_ref, dg_ref, dcw_ref, dpv_ref, dnw_ref, dstate, dhalo, dqkv, dbg):
        i = pl.program_id(0)
        first_tile = (i == nt - 1)

        @pl.when(i == 0)
        def _():
            dstate[...] = jnp.zeros_like(dstate)
            dhalo[...] = jnp.zeros_like(dhalo)
            dcw_ref[...] = jnp.zeros_like(dcw_ref)
            dpv_ref[...] = jnp.zeros_like(dpv_ref)
            dnw_ref[...] = jnp.zeros_like(dnw_ref)

        keep = jnp.where(first_tile, 0.0, 1.0)
        ext = jnp.concatenate([xp_ref[...] * keep, x_ref[...]], 0)
        G = g_ref[...]
        taps, c, qkv, beta, sarg, nea, gdec, gc = _gdn_pre(ext, T, cw_ref, G, pv_ref)
        tril, strict, _ = _gdn_masks()
        rowi = lax.broadcasted_iota(jnp.int32, (128, 1), 0)
        lane = lax.broadcasted_iota(jnp.int32, (1, 128), 1)
        ones_b = jnp.ones((128, 128), BF16)
        nwv = nw_ref[...]
        items = [(dc, h) for dc in range(nu) for h in range(G_HEADS)]

        def recompute(dc, h):
            q = yield from _gdn_chunk(qkv, beta, gc, slice(dc * 128, (dc + 1) * 128), h, tm=tm_ref[dc, h])
            sa = st_ref[2 * dc, h]
            sb, _, vn, o = yield from _gdn_scan(q, sa)
            return q, sa, sb, vn, o

        fw = _lockstep([recompute(dc, h) for dc, h in items])
        chain_out = {}

        def head_chain(h):
            dS = dstate[h]
            for dc in reversed(range(nu)):
                rs = slice(dc * 128, (dc + 1) * 128)
                q, sa, sb, vn, o = fw[dc * G_HEADS + h]
                cs = slice(h * 128, (h + 1) * 128)
                zg = z_ref[rs, cs]
                dy = dy_ref[rs, cs]
                rn = lax.rsqrt(jnp.mean(o * o, 1, keepdims=True) + RMS_EPS)
                don = dy * _silu(zg)
                dz_ref[rs, cs] = (dy * (o * rn * nwv) * _dsilu(zg)).astype(BF16)
                dnw_ref[...] += jnp.sum(don * o * rn, 0, keepdims=True)
                tt = don * nwv
                do = rn * (tt - o * (rn * rn) * jnp.mean(tt * o, 1, keepdims=True))
                yield
                dob = _b(do)
                sab, sbb = _b(sa), _b(sb)
                vnb16 = _b(vn)
                dqk = jnp.where(tril, _nt(dob, vnb16), 0.0)
                dvn_o = _tn(_b(q["qk"]), dob)
                dSb16 = _b(dS)
                kdb = _b(q["kd"])
                wb = _b(q["w"])
                qdb = _b(q["qd"])
                yield
                dvn_b = dvn_o[64:128] + _nn(kdb[64:128], dSb16)
                dkd_b = _nt(vnb16[64:128], dSb16)
                dgl_b = jnp.sum(jnp.sum(dS * sb, 1, keepdims=True), 0, keepdims=True)
                yield
                dvn_b16 = _b(dvn_b)
                dw_b = -_nt(dvn_b16, sbb)
                dqd_b = _nt(dob[64:128], sbb)
                dSm = q["glb"] * dS + _tn(qdb[64:128], dob[64:128]) - _tn(wb[64:128], dvn_b16)
                yield
                dSm16 = _b(dSm)
                dvn_a = dvn_o[0:64] + _nn(kdb[0:64], dSm16)
                dkd_a = _nt(vnb16[0:64], dSm16)
                dgl_a = jnp.sum(jnp.sum(dSm * sa, 1, keepdims=True), 0, keepdims=True)
                yield
                dvn_a16 = _b(dvn_a)
                dw_a = -_nt(dvn_a16, sab)
                dqd_a = _nt(dob[0:64], sab)
                dS = q["gla"] * dSm + _tn(qdb[0:64], dob[0:64]) - _tn(wb[0:64], dvn_a16)
                chain_out[dc, h] = (dqk, jnp.concatenate([dvn_a, dvn_b], 0), jnp.concatenate([dw_a, dw_b], 0),
                                    jnp.concatenate([dkd_a, dkd_b], 0), jnp.concatenate([dqd_a, dqd_b], 0),
                                    dgl_a, dgl_b)
                yield
            dstate[h] = dS

        _lockstep([head_chain(h) for h in range(G_HEADS)])

        def local(dc, h):
            rs = slice(dc * 128, (dc + 1) * 128)
            q = fw[dc * G_HEADS + h][0]
            dqk, du, dw, dkd, dqd, dgl_a, dgl_b = chain_out[dc, h]
            if True:
                dvb = _dot3(_tn, q["tm"], du)
                dkbe = _dot3(_tn, q["tm"], dw)
                yield
                dM = jnp.where(strict, -(_nt(_b(dvb), _b(q["u"])) + _nt(_b(dkbe), _b(q["w"]))), 0.0)
                yield
                D = q["D"]
                dA = dM * D
                dB = dqk * D
                dDD = (dM * q["A"] + dqk * q["Bm"]) * D
                dh_, dm_, dl_ = _split3(dDD)
                colsum = _tn(dh_, ones_b) + (_tn(dm_, ones_b) + _tn(dl_, ones_b))
                dgc = jnp.sum(dDD, 1, keepdims=True) - _lane_col(colsum, 0)
                yield
                dA16, dB16 = _b(dA), _b(dB)
                knb, kbb, qnb = _b(q["kn"]), _b(q["kb"]), _b(q["qn"])
                eg, ed = q["eg"], q["ed"]
                dkb = _nn(dA16, knb) + dkbe * eg
                dkn = _tn(dA16, kbb) + _tn(dB16, qnb) + dkd * ed + dkb * q["bcol"]
                dqn = _nn(dB16, knb) + dqd * eg
                yield
                deg = jnp.sum(dkbe * q["kb"], 1, keepdims=True) + jnp.sum(dqd * q["qn"], 1, keepdims=True)
                ded = jnp.sum(dkd * q["kn"], 1, keepdims=True) * ed
                dgc = dgc + deg * eg - ded
                tail_a = jnp.sum(jnp.where(rowi < 64, ded, 0.0), 0, keepdims=True) + dgl_a * q["gla"]
                tail_b = jnp.sum(jnp.where(rowi >= 64, ded, 0.0), 0, keepdims=True) + dgl_b * q["glb"]
                dgc = dgc + jnp.where(rowi == 63, tail_a, 0.0) + jnp.where(rowi == 127, tail_b, 0.0)
                dbeta = jnp.sum(dkb * q["kn"], 1, keepdims=True) + jnp.sum(dvb * q["vh"], 1, keepdims=True)
                bcol = q["bcol"]
                blk = jnp.where(lane == h, dbeta * bcol * (1.0 - bcol), 0.0) + jnp.where(lane == 4 + h, dgc, 0.0)
                yield
                sc = G_HEAD_DIM ** -0.5
                rq, rk, qh, kh = q["rq"], q["rk"], q["qh"], q["kh"]
                dqh = sc * (dqn * rq - qh * (rq * rq * rq) * jnp.sum(dqn * qh, 1, keepdims=True))
                dkh = dkn * rk - kh * (rk * rk * rk) * jnp.sum(dkn * kh, 1, keepdims=True)
                dqkv[rs, h * 128:(h + 1) * 128] = dqh
                dqkv[rs, 512 + h * 128:512 + (h + 1) * 128] = dkh
                dqkv[rs, 1024 + h * 128:1024 + (h + 1) * 128] = dvb * bcol
            return blk

        blks = _lockstep([local(dc, h) for dc, h in items])
        for dc in range(nu):
            dbg[dc * 128:(dc + 1) * 128, :] = functools.reduce(
                lambda a, b: a + b, [blks[dc * G_HEADS + h] for h in range(G_HEADS)])
        ri = lax.broadcasted_iota(jnp.int32, (T, T), 0)
        cj = lax.broadcasted_iota(jnp.int32, (T, T), 1)
        utri = jnp.where((ri <= cj) & ((ri >> 6) == (cj >> 6)), 1.0, 0.0).astype(BF16)
        dbgv = dbg[...]
        dgd = _dot_exact_lhs(_nn, utri, dbgv)
        is_g = (lane >= 4) & (lane < 8)
        dga = jnp.where(is_g, dgd * nea * _sigmoid(sarg), 0.0)
        dg_ref[...] = jnp.where(lane < 4, dbgv, dga).astype(BF16)
        dpv_ref[0:1, :] += jnp.sum(jnp.where(is_g, dgd * gdec, 0.0), 0, keepdims=True)
        dpv_ref[1:2, :] += jnp.sum(dga, 0, keepdims=True)
        dc_ = dqkv[...] * _dsilu(c)
        for k in range(CONV_WIDTH):
            dcw_ref[k:k + 1, :] += jnp.sum(dc_ * taps[k], 0, keepdims=True)
        ext2 = jnp.concatenate([dc_, dhalo[...]], 0)
        tt2 = _conv_taps_t(ext2, T)
        dx_ref[...] = sum(cw_ref[k:k + 1, :] * tt2[k] for k in range(CONV_WIDTH)).astype(BF16)
        dhalo[...] = dc_[0:8]

    def rev(i):
        return nt - 1 - i

    def prev8(i):
        return jnp.maximum(rev(i) * t8 - 1, 0)

    return _pcall(
        body, name=name, grid=(nt,),
        in_specs=[pl.BlockSpec((T, GW3), lambda i: (rev(i), OFF_GQKV // GW3)),
                  pl.BlockSpec((8, GW3), lambda i: (prev8(i), OFF_GQKV // GW3)),
                  pl.BlockSpec((T, 512), lambda i: (rev(i), OFF_GZ // 512)),
                  pl.BlockSpec((T, 128), lambda i: (rev(i), OFF_GBA // 128)),
                  pl.BlockSpec((2 * nu, G_HEADS, 128, 128), lambda i: (rev(i), 0, 0, 0)),
                  pl.BlockSpec((nu, G_HEADS, 128, 128), lambda i: (rev(i), 0, 0, 0)),
                  pl.BlockSpec((T, 512), lambda i: (rev(i), MIX_G // 512))] + _gdn_param_specs(),
        out_specs=[pl.BlockSpec((T, GW3), lambda i: (rev(i), 0)),
                   pl.BlockSpec((T, 512), lambda i: (rev(i), 0)),
                   pl.BlockSpec((T, 128), lambda i: (rev(i), 0)),
                   pl.BlockSpec((CONV_WIDTH, GW3), lambda i: (0, 0)),
                   pl.BlockSpec((8, 128), lambda i: (0, 0)),
                   pl.BlockSpec((1, 128), lambda i: (0, 0))],
        out_shape=[jax.ShapeDtypeStruct((S_, GW3), BF16), jax.ShapeDtypeStruct((S_, 512), BF16),
                   jax.ShapeDtypeStruct((S_, 128), BF16), jax.ShapeDtypeStruct((CONV_WIDTH, GW3), F32),
                   jax.ShapeDtypeStruct((8, 128), F32), jax.ShapeDtypeStruct((1, 128), F32)],
        scratch_shapes=[pltpu.VMEM((G_HEADS, 128, 128), F32), pltpu.VMEM((8, GW3), F32),
                        pltpu.VMEM((T, GW3), F32), pltpu.VMEM((T, 128), F32)],
        compiler_params=_cp(("arbitrary",)),
    )(proj, proj, proj, proj, states, tms, dymix, cw, _gdn_pvec(a_log, dt_bias), nw.reshape(1, 128))


def _add2(a, b, *, out_dtype, tr, name):
    R_, C = a.shape

    def body(a_ref, b_ref, o_ref):
        o_ref[...] = (a_ref[...] + b_ref[...]).astype(o_ref.dtype)

    spec = pl.BlockSpec((tr, C), lambda i: (i, 0))
    return _pcall(body, name=name, grid=(R_ // tr,), in_specs=[spec, spec], out_specs=spec,
                  out_shape=jax.ShapeDtypeStruct((R_, C), out_dtype), compiler_params=_cp(("parallel",)))(a, b)


def _add_mine(a0, a1, b, *, out_dtype, tr, name):
    R_, C = b.shape

    def body(a0_ref, a1_ref, b_ref, o_ref):
        mine = jnp.where(lax.axis_index("c") == 0, a0_ref[...], a1_ref[...])
        o_ref[...] = (mine + b_ref[...]).astype(o_ref.dtype)

    spec = pl.BlockSpec((tr, C), lambda i: (i, 0))
    return _pcall(body, name=name, grid=(R_ // tr,), in_specs=[spec] * 3, out_specs=spec,
                  out_shape=jax.ShapeDtypeStruct((R_, C), out_dtype), compiler_params=_cp(("parallel",)))(a0, a1, b)


def _sum4(a, *, tr, name):
    _, R_, C = a.shape

    def body(a_ref, o_ref):
        o_ref[...] = ((a_ref[0].astype(F32) + a_ref[1].astype(F32)) + a_ref[2].astype(F32)) + a_ref[3].astype(F32)

    return _pcall(body, name=name, grid=(R_ // tr,),
                  in_specs=[pl.BlockSpec((4, tr, C), lambda i: (0, i, 0))],
                  out_specs=pl.BlockSpec((tr, C), lambda i: (i, 0)),
                  out_shape=jax.ShapeDtypeStruct((R_, C), F32), compiler_params=_cp(("parallel",)))(a)


def _adamw(w, g, m, v, *, tr, name):
    R_, C = w.shape
    c1 = 1.0 / (1.0 - ADAM_B1 ** ADAM_STEP)
    c2 = 1.0 / (1.0 - ADAM_B2 ** ADAM_STEP)

    def body(w_ref, g_ref, m_ref, v_ref, d_ref, mo_ref, vo_ref):
        gg = g_ref[...]
        mn = ADAM_B1 * m_ref[...] + (1.0 - ADAM_B1) * gg
        vn = ADAM_B2 * v_ref[...] + (1.0 - ADAM_B2) * (gg * gg)
        mo_ref[...] = mn
        vo_ref[...] = vn
        d_ref[...] = -ADAM_LR * ((mn * c1) / (jnp.sqrt(vn * c2) + ADAM_EPS) + ADAM_WD * w_ref[...])

    spec = pl.BlockSpec((tr, C), lambda i: (i, 0))
    shp = jax.ShapeDtypeStruct((R_, C), F32)
    return _pcall(body, name=name, grid=(R_ // tr,), in_specs=[spec] * 4, out_specs=[spec] * 3,
                  out_shape=[shp] * 3, compiler_params=_cp(("parallel",)))(w, g, m, v)


HBM_SPEC = pl.BlockSpec(memory_space=pltpu.HBM)


def _place():
    x, y, c = lax.axis_index("x"), lax.axis_index("y"), lax.axis_index("c")
    chips = [(1 - x, y), (x, 1 - y), (1 - x, 1 - y)]
    return x, y, c, 2 * x + y, chips, [2 * cx + cy for cx, cy in chips], (x, y, 1 - c)


def _remote(src, dst, ssem, rsem, dev):
    return pltpu.make_async_remote_copy(src_ref=src, dst_ref=dst, send_sem=ssem, recv_sem=rsem,
                                        device_id=dev, device_id_type=MESH)


def _gather_weights(win, wout, conv):
    def body(win_ref, wout_ref, cv_ref, gin_ref, gout_ref, gcv_ref, ssem, rsem, lsem):
        x, y, c, s, chips, sid, sib = _place()

        def in_half(slot, hc):
            return gin_ref.at[slot, hc]

        def out_half(slot, hc):
            return gout_ref.at[slot, hc]

        local = [pltpu.make_async_copy(win_ref, gin_ref.at[s], lsem.at[0]),
                 pltpu.make_async_copy(wout_ref, gout_ref.at[s], lsem.at[1]),
                 pltpu.make_async_copy(cv_ref, gcv_ref.at[s], lsem.at[2])]
        for cp in local:
            cp.start()
        sends = []
        for j, chip in enumerate(chips):
            dev = (*chip, c)
            sends.append(_remote(win_ref.at[c], in_half(s, c), ssem.at[j], rsem.at[j], dev))
            sends.append(_remote(wout_ref.at[c], out_half(s, c), ssem.at[3 + j], rsem.at[3 + j], dev))
            sends.append(_remote(cv_ref, gcv_ref.at[s], ssem.at[6 + j], rsem.at[6 + j], dev))
        for cp in sends:
            cp.start()
        for j in range(3):
            _remote(in_half(sid[j], c), in_half(sid[j], c), ssem.at[j], rsem.at[j], sib).wait_recv()
            f = _remote(in_half(sid[j], c), in_half(sid[j], c), ssem.at[9 + j], rsem.at[9 + j], sib)
            f.start()
            sends.append(f)
            _remote(out_half(sid[j], c), out_half(sid[j], c), ssem.at[3 + j], rsem.at[3 + j], sib).wait_recv()
            f = _remote(out_half(sid[j], c), out_half(sid[j], c), ssem.at[12 + j], rsem.at[12 + j], sib)
            f.start()
            sends.append(f)
        for j in range(3):
            _remote(in_half(sid[j], 1 - c), in_half(sid[j], 1 - c), ssem.at[9 + j], rsem.at[9 + j], sib).wait_recv()
            _remote(out_half(sid[j], 1 - c), out_half(sid[j], 1 - c), ssem.at[12 + j], rsem.at[12 + j], sib).wait_recv()
            _remote(gcv_ref.at[sid[j]], gcv_ref.at[sid[j]], ssem.at[6 + j], rsem.at[6 + j], sib).wait_recv()
        for cp in sends:
            cp.wait_send()
        for cp in local:
            cp.wait()

    return _pcall(
        body, name="gather_weights",
        in_specs=[HBM_SPEC] * 3, out_specs=[HBM_SPEC] * 3,
        out_shape=[jax.ShapeDtypeStruct((4,) + win.shape, win.dtype),
                   jax.ShapeDtypeStruct((4,) + wout.shape, wout.dtype),
                   jax.ShapeDtypeStruct((4,) + conv.shape, conv.dtype)],
        scratch_shapes=[pltpu.SemaphoreType.DMA((15,)), pltpu.SemaphoreType.DMA((15,)),
                        pltpu.SemaphoreType.DMA((3,))],
    )(win, wout, conv)


def _swap_sibling(pairs, stacked, name):
    n2, ns = len(pairs), len(stacked)
    n = n2 + ns

    def body(*refs):
        src2 = [(refs[2 * k], refs[2 * k + 1]) for k in range(n2)]
        srcs = refs[2 * n2:2 * n2 + ns]
        dst = refs[2 * n2 + ns:2 * n2 + ns + n]
        ssem, rsem = refs[-2], refs[-1]
        x, y, c, s, chips, sid, sib = _place()

        def exchange(give):
            cps = [_remote(src2[k][give], dst[k], ssem.at[k], rsem.at[k], sib) for k in range(n2)]
            for cp in cps:
                cp.start()
            for cp in cps:
                cp.wait()

        small = [_remote(srcs[k].at[1 - c], dst[n2 + k], ssem.at[n2 + k], rsem.at[n2 + k], sib) for k in range(ns)]
        for cp in small:
            cp.start()

        @pl.when(c == 0)
        def _():
            exchange(1)

        @pl.when(c == 1)
        def _():
            exchange(0)

        for cp in small:
            cp.wait()

    flat = [a for p in pairs for a in p] + list(stacked)
    return _pcall(
        body, name=name, in_specs=[HBM_SPEC] * len(flat), out_specs=[HBM_SPEC] * n,
        out_shape=[jax.ShapeDtypeStruct(p[0].shape, p[0].dtype) for p in pairs]
        + [jax.ShapeDtypeStruct(a.shape[1:], a.dtype) for a in stacked],
        scratch_shapes=[pltpu.SemaphoreType.DMA((n,)), pltpu.SemaphoreType.DMA((n,))],
    )(*flat)


def _scatter_chips(arrs, per_target, name):
    n = len(arrs)

    def body(*refs):
        src, dst = refs[:n], refs[n:2 * n]
        ssem, rsem, lsem = refs[2 * n], refs[2 * n + 1], refs[2 * n + 2]
        x, y, c, s, chips, sid, sib = _place()
        local = []
        sends = []
        for k in range(n):
            mine = src[k].at[s] if per_target[k] else src[k]
            local.append(pltpu.make_async_copy(mine, dst[k].at[s], lsem.at[k]))
            for j, chip in enumerate(chips):
                piece = src[k].at[sid[j]] if per_target[k] else src[k]
                sends.append(_remote(piece, dst[k].at[s], ssem.at[3 * k + j], rsem.at[3 * k + j], (*chip, c)))
        for cp in local + sends:
            cp.start()
        for k in range(n):
            for j in range(3):
                _remote(dst[k].at[sid[j]], dst[k].at[sid[j]], ssem.at[3 * k + j], rsem.at[3 * k + j], sib).wait_recv()
        for cp in sends:
            cp.wait_send()
        for cp in local:
            cp.wait()

    outs = [jax.ShapeDtypeStruct(a.shape if pt else (4,) + a.shape, a.dtype) for a, pt in zip(arrs, per_target)]
    return _pcall(
        body, name=name, in_specs=[HBM_SPEC] * n, out_specs=[HBM_SPEC] * n, out_shape=outs,
        scratch_shapes=[pltpu.SemaphoreType.DMA((3 * n,)), pltpu.SemaphoreType.DMA((3 * n,)),
                        pltpu.SemaphoreType.DMA((n,))],
    )(*arrs)


def _join_halves(arrs, name):
    n = len(arrs)

    def body(*refs):
        src, dst = refs[:n], refs[n:2 * n]
        ssem, rsem, lsem = refs[2 * n], refs[2 * n + 1], refs[2 * n + 2]
        x, y, c, s, chips, sid, sib = _place()

        def half(k, hc):
            return dst[k].at[hc]

        local = [pltpu.make_async_copy(src[k], half(k, c), lsem.at[k]) for k in range(n)]
        sends = [_remote(src[k], half(k, c), ssem.at[k], rsem.at[k], sib) for k in range(n)]
        for cp in local + sends:
            cp.start()
        for k in range(n):
            _remote(half(k, 1 - c), half(k, 1 - c), ssem.at[k], rsem.at[k], sib).wait_recv()
        for cp in sends:
            cp.wait_send()
        for cp in local:
            cp.wait()

    outs = [jax.ShapeDtypeStruct((2,) + a.shape, a.dtype) for a in arrs]
    return _pcall(
        body, name=name, in_specs=[HBM_SPEC] * n, out_specs=[HBM_SPEC] * n, out_shape=outs,
        scratch_shapes=[pltpu.SemaphoreType.DMA((n,)), pltpu.SemaphoreType.DMA((n,)),
                        pltpu.SemaphoreType.DMA((n,))],
    )(*arrs)


def _perm_cols(w):
    parts = [w[..., int(_ORIG_OFF[oi]):int(_ORIG_OFF[oi]) + IN_SIZES[oi]] for oi, _ in _PIECES]
    parts.append(jnp.zeros(w.shape[:-1] + (NP - N_IN,), w.dtype))
    return jnp.concatenate(parts, -1)


def _unperm_cols(g):
    parts = [None] * len(IN_SIZES)
    for oi, off in _PIECES:
        parts[oi] = g[..., off:off + IN_SIZES[oi]]
    return jnp.concatenate(parts, -1)


def _perm_rows(w):
    return jnp.concatenate([w[..., 512:1536, :], w[..., 0:512, :], w[..., 1536:2048, :]], -2)


def _unperm_rows(g):
    return jnp.concatenate([g[..., 1024:1536, :], g[..., 0:1024, :], g[..., 1536:2048, :]], -2)


_SMALL = ("sinks", "r_conv_b", "r_wa", "r_ba", "r_wx", "r_bx", "r_lam", "g_a_log", "g_dt_bias", "g_norm_w",
          "ln_g", "ln_b", "r_conv_w", "g_conv_w")
_PACK_ALIGN = 2048


def _pack(arrs):
    parts = []
    for a in arrs:
        flat = a.reshape(-1)
        pad = (-flat.shape[0]) % _PACK_ALIGN
        parts.append(jnp.concatenate([flat, jnp.zeros((pad,), flat.dtype)]) if pad else flat)
    return jnp.concatenate(parts).reshape(-1, 128)


def _unpack(packed, shapes):
    flat = packed.reshape(-1)
    out = []
    off = 0
    for shp in shapes:
        n = int(np.prod(shp))
        out.append(flat[off:off + n].reshape(shp))
        off += n + ((-n) % _PACK_ALIGN)
    return out


def _tile(n, t):
    return min(n, t)


def _layer_fwd(l, x, xb, wb, wob, rope_c, rope_s, p):
    S_ = x.shape[0]
    proj = _matmul(xb, wb, ta=False, tb=False, tm=_tile(S_, 1024), tn=512, tk=wb.shape[0], out_dtype=F32,
                   name=f"in_proj_{l}")
    ya = _attn_fwd(proj, rope_c, rope_s, p["sinks"], T=_tile(S_, 512), name=f"attn_fwd_{l}")
    h, yr = _rglru_fwd(proj, p["r_conv_w"], p["r_conv_b"], p["r_wa"], p["r_ba"], p["r_wx"], p["r_bx"], p["r_lam"],
                       T=_tile(S_, 256), name=f"rglru_fwd_{l}")
    yg, st, tms = _gdn_fwd(proj, p["g_conv_w"], p["g_a_log"], p["g_dt_bias"], p["g_norm_w"],
                           T=_tile(S_, 256), name=f"gdn_fwd_{l}")
    ymix = jnp.concatenate([yr, ya, yg], 1)
    z = _outproj(ymix, wob, x, tm=_tile(S_, 256), name=f"out_proj_{l}")
    return dict(proj=proj, h=h, st=st, tms=tms, ymix=ymix, z=z)


def _shard_layout(parts, nsh, nsp):
    rows, dt = parts[0].shape[0], parts[0].dtype
    segs, pos = [], 0
    for p in parts:
        st = 0
        while st < p.shape[1]:
            take = min(nsh - pos % nsh, p.shape[1] - st)
            segs.append(p[:, st:st + take])
            st += take
            pos += take
            if pos % nsh == 0:
                segs.append(jnp.zeros((rows, nsp - nsh), dt))
    return jnp.concatenate(segs, 1)


def _layer_bwd(l, sv, x_b, dz, dzb, ws, wob, rope_c, rope_s, p, nsh):
    S_, D = dz.shape
    nsp = ws.shape[1] // 4
    proj = sv["proj"]
    dymix = _matmul(dzb, wob, ta=False, tb=True, tm=_tile(S_, 1024), tn=512, tk=D, out_dtype=F32,
                    name=f"dmix_{l}")
    dwo = _matmul(sv["ymix"], dzb, ta=True, tb=False, tm=512, tn=_tile(D, 1024), tk=_tile(S_, 1024),
                  out_dtype=F32, name=f"dw_out_{l}",
                  out_blocks=((MIX_WIDTH, D), (512, _tile(D, 1024)),
                              lambda i, j: (jnp.where(i == 3, 3, (i + 1) % 3), j)))
    dq, daz, dk, dv, dkt, dvt, dsk = _attn_bwd(proj, rope_c, rope_s, p["sinks"], dymix, T=_tile(S_, 512),
                                               name=f"attn_bwd_{l}")
    (drx, drz, dcw_r, dcb_r, dwa, dba, dwx, dbx, dlam) = _rglru_bwd(
        proj, sv["h"], dymix, p["r_conv_w"], p["r_conv_b"], p["r_wa"], p["r_ba"], p["r_wx"], p["r_bx"], p["r_lam"],
        T=_tile(S_, 256), name=f"rglru_bwd_{l}")
    dqkv, dgz, dgba, dcw_g, dpv, dnw = _gdn_bwd(proj, sv["st"], sv["tms"], dymix, p["g_conv_w"], p["g_a_log"],
                                                p["g_dt_bias"], p["g_norm_w"], T=_tile(S_, 256), name=f"gdn_bwd_{l}")
    dproj = _shard_layout([dq, jnp.concatenate([dk[128:], dkt], 0), jnp.concatenate([dv[128:], dvt], 0), daz,
                           drx, drz, dqkv, dgz, dgba[:, :8]], nsh, nsp)
    dx = _matmul(dproj, ws, ta=False, tb=True, tm=_tile(S_, 512), tn=_tile(D, 1024), tk=2 * nsp, out_dtype=F32,
                 name=f"dx_{l}", extra=dz, alpha=DEEPNORM_ALPHA)
    tmw = _tile(D, 1024)
    dwin = _matmul(x_b, dproj, ta=True, tb=False, tm=tmw, tn=nsp, tk=_tile(S_, 1024), out_dtype=F32,
                   name=f"dw_in_{l}", out_blocks=((4, D, nsp), (None, tmw, nsp), lambda i, j: (j, i, 0)))
    small = dict(sinks=dsk[:, 0], r_conv_b=dcb_r[0], r_wa=dwa, r_ba=dba[0], r_wx=dwx, r_bx=dbx[0], r_lam=dlam[0],
                 g_a_log=dpv[0, 4:8], g_dt_bias=dpv[1, 4:8], g_norm_w=dnw[0], r_conv_w=dcw_r, g_conv_w=dcw_g)
    return dx, dwin, dwo, small


def kernel(x, w_in, sinks, r_conv_w, r_conv_b, r_wa, r_ba, r_wx, r_bx, r_lam, g_conv_w, g_a_log, g_dt_bias, g_norm_w, w_out, ln_g, ln_b, loss_target, m_w_in, m_sinks, m_r_conv_w, m_r_conv_b, m_r_wa, m_r_ba, m_r_wx, m_r_bx, m_r_lam, m_g_conv_w, m_g_a_log, m_g_dt_bias, m_g_norm_w, m_w_out, m_ln_g, m_ln_b, v_w_in, v_sinks, v_r_conv_w, v_r_conv_b, v_r_wa, v_r_ba, v_r_wx, v_r_bx, v_r_lam, v_g_conv_w, v_g_a_log, v_g_dt_bias, v_g_norm_w, v_w_out, v_ln_g, v_ln_b):
    S_, D = x.shape[1], x.shape[2]
    nsh = w_in.shape[2]
    rsh = w_out.shape[1]
    cx, cy, cc = lax.axis_index("x"), lax.axis_index("y"), lax.axis_index("c")
    chip = 2 * cx + cy
    rcw_n, gcw_n = r_conv_w.shape[2], g_conv_w.shape[2]

    conv_pack = jnp.concatenate([r_conv_w, g_conv_w], 2)
    g_in, g_out, g_conv = _gather_weights(w_in.astype(BF16), w_out.astype(BF16), conv_pack)
    w_full = jnp.concatenate([g_in[t] for t in range(4)], 2)
    wb = _perm_cols(w_full)
    nsp = -(-nsh // 128) * 128
    ws = jnp.concatenate([jnp.pad(g_in[t], ((0, 0), (0, 0), (0, nsp - nsh))) for t in range(4)], 2)
    wob = _perm_rows(jnp.concatenate([g_out[t] for t in range(4)], 1))
    rcw = jnp.concatenate([g_conv[t][:, :, :rcw_n] for t in range(4)], 2)
    gcw = jnp.concatenate([g_conv[t][:, :, rcw_n:] for t in range(4)], 2)

    pos = jnp.arange(S_, dtype=F32)[:, None]
    inv = 1.0 / (ROPE_THETA ** (jnp.arange(0, A_HEAD_DIM, 2, dtype=F32) / A_HEAD_DIM))
    ang = pos * inv[None, :]
    cos, sin = jnp.cos(ang), jnp.sin(ang)
    rope_c = jnp.concatenate([cos, cos, cos, cos], 1)
    rope_s = jnp.concatenate([-sin, sin, -sin, sin], 1)

    def params(l):
        return dict(sinks=sinks[l], r_conv_w=rcw[l], r_conv_b=r_conv_b[l], r_wa=r_wa[l], r_ba=r_ba[l],
                    r_wx=r_wx[l], r_bx=r_bx[l], r_lam=r_lam[l], g_conv_w=gcw[l], g_a_log=g_a_log[l],
                    g_dt_bias=g_dt_bias[l], g_norm_w=g_norm_w[l])

    xs, xbs, saved = [x[0]], [x[0].astype(BF16)], []
    for l in range(DEPTH):
        sv = _layer_fwd(l, xs[l], xbs[l], wb[l], wob[l], rope_c, rope_s, params(l))
        saved.append(sv)
        if l + 1 < DEPTH:
            xn, xnb = _ln_fwd(sv["z"], ln_g[l], ln_b[l], tm=_tile(S_, 256), name=f"ln_fwd_{l}")
            xs.append(xn)
            xbs.append(xnb)

    tm_ln = _tile(S_, 256)
    dz, dzb, dg_l, db_l, loss_part = _ln_bwd(saved[-1]["z"], ln_g[-1], ln_b[-1], loss_target[0], from_target=True,
                                             tm=tm_ln, name=f"ln_bwd_{DEPTH - 1}")
    dwin, dwo, small = [None] * DEPTH, [None] * DEPTH, [None] * DEPTH
    dlng, dlnb = [None] * DEPTH, [None] * DEPTH
    for l in reversed(range(DEPTH)):
        dlng[l], dlnb[l] = dg_l[0], db_l[0]
        dx, dwin[l], dwo[l], small[l] = _layer_bwd(l, saved[l], xbs[l], dz, dzb, ws[l], wob[l], rope_c, rope_s,
                                                   params(l), nsh)
        if l > 0:
            dz, dzb, dg_l, db_l, _ = _ln_bwd(saved[l - 1]["z"], ln_g[l - 1], ln_b[l - 1], dx, from_target=False,
                                             tm=tm_ln, name=f"ln_bwd_{l - 1}")
    grad_x = dx[None]
    loss = lax.psum(loss_part[0, 0], ("x", "y", "c"))

    sm = {k: jnp.stack([small[l][k] for l in range(DEPTH)]) for k in small[0]}
    sm["ln_g"], sm["ln_b"] = jnp.stack(dlng), jnp.stack(dlnb)
    names = list(_SMALL)
    gs = _pack([sm[n] for n in names])
    gs2 = gs.reshape(2, gs.shape[0] // 2, 128)
    in_got, out_got, s_got = _swap_sibling([(dwin[0], dwin[1]), (dwo[0], dwo[1])], [gs2], "reduce_pair")
    in_cp = _add_mine(dwin[0].reshape(-1, nsp), dwin[1].reshape(-1, nsp), in_got.reshape(-1, nsp), out_dtype=BF16,
                      tr=256, name="pair_sum_w_in").reshape(4, D, nsp)
    out_cp = _add_mine(dwo[0], dwo[1], out_got, out_dtype=BF16, tr=256, name="pair_sum_w_out").reshape(4, rsh, D)
    s_cp = _add_mine(gs2[0], gs2[1], s_got, out_dtype=F32, tr=gs2.shape[1], name="pair_sum_small")
    in_all, out_all, s_all = _scatter_chips([in_cp, out_cp, s_cp], [True, True, False], "reduce_chips")
    in_sum = _sum4(in_all, tr=256, name="chip_sum_w_in")
    out_sum = _sum4(out_all, tr=256, name="chip_sum_w_out")
    s_sum = _sum4(s_all, tr=s_cp.shape[0], name="chip_sum_small")
    g_w_in, g_w_out, g_small = _join_halves([in_sum, out_sum, s_sum], "reduce_join")
    g_w_in = g_w_in[:, :, :nsh]
    g_small = g_small.reshape(gs.shape)

    gsm = dict(zip(names, _unpack(g_small, [sm[n].shape for n in names])))
    gsm["r_conv_w"] = lax.dynamic_slice_in_dim(gsm["r_conv_w"], chip * rcw_n, rcw_n, 2)
    gsm["g_conv_w"] = lax.dynamic_slice_in_dim(gsm["g_conv_w"], chip * gcw_n, gcw_n, 2)
    wts = dict(sinks=sinks, r_conv_w=r_conv_w, r_conv_b=r_conv_b, r_wa=r_wa, r_ba=r_ba, r_wx=r_wx, r_bx=r_bx,
               r_lam=r_lam, g_conv_w=g_conv_w, g_a_log=g_a_log, g_dt_bias=g_dt_bias, g_norm_w=g_norm_w,
               ln_g=ln_g, ln_b=ln_b)
    mom = dict(sinks=m_sinks, r_conv_w=m_r_conv_w, r_conv_b=m_r_conv_b, r_wa=m_r_wa, r_ba=m_r_ba, r_wx=m_r_wx,
               r_bx=m_r_bx, r_lam=m_r_lam, g_conv_w=m_g_conv_w, g_a_log=m_g_a_log, g_dt_bias=m_g_dt_bias,
               g_norm_w=m_g_norm_w, ln_g=m_ln_g, ln_b=m_ln_b)
    vel = dict(sinks=v_sinks, r_conv_w=v_r_conv_w, r_conv_b=v_r_conv_b, r_wa=v_r_wa, r_ba=v_r_ba, r_wx=v_r_wx,
               r_bx=v_r_bx, r_lam=v_r_lam, g_conv_w=v_g_conv_w, g_a_log=v_g_a_log, g_dt_bias=v_g_dt_bias,
               g_norm_w=v_g_norm_w, ln_g=v_ln_g, ln_b=v_ln_b)
    pk = [_pack([d[n] for n in names]) for d in (wts, gsm, mom, vel)]
    sshapes = [wts[n].shape for n in names]
    d_s, m_s, v_s = _adamw(*pk, tr=pk[0].shape[0], name="adamw_small")
    d_sm, m_sm, v_sm = (dict(zip(names, _unpack(a, sshapes))) for a in (d_s, m_s, v_s))

    def big(w, g, m, v, name):
        C = w.shape[-1]
        outs = _adamw(w.reshape(-1, C), g.reshape(-1, C), m.reshape(-1, C), v.reshape(-1, C), tr=256, name=name)
        return [o.reshape(w.shape) for o in outs]

    d_in, m_in, v_in = big(w_in, g_w_in, m_w_in, v_w_in, "adamw_w_in")
    d_out, m_out, v_out = big(w_out, g_w_out, m_w_out, v_w_out, "adamw_w_out")

    order = ["w_in", "sinks", "r_conv_w", "r_conv_b", "r_wa", "r_ba", "r_wx", "r_bx", "r_lam", "g_conv_w",
             "g_a_log", "g_dt_bias", "g_norm_w", "w_out", "ln_g", "ln_b"]
    grads = dict(gsm, w_in=g_w_in, w_out=g_w_out)
    deltas = dict(d_sm, w_in=d_in, w_out=d_out)
    new_m = dict(m_sm, w_in=m_in, w_out=m_out)
    new_v = dict(v_sm, w_in=v_in, w_out=v_out)
    return (loss, grad_x, *[grads[n] for n in order], *[deltas[n] for n in order],
            *[new_m[n] for n in order], *[new_v[n] for n in order])
```

```python
import functools
import math

import jax
import jax.numpy as jnp
import numpy as np
from jax import lax
from jax.experimental import pallas as pl
from jax.experimental.pallas import tpu as pltpu

F32 = jnp.float32
BF16 = jnp.bfloat16
MESH = pl.DeviceIdType.MESH

DEPTH = 2
A_HEADS, A_KV_HEADS, A_HEAD_DIM = 8, 2, 64
A_WIDTH, A_KV_WIDTH = 512, 128
WINDOW = 128
ROPE_THETA = 10000.0
R_WIDTH, R_BLOCKS, R_BLOCK_DIM, R_C = 1024, 8, 128, 8.0
CONV_WIDTH = 4
G_HEADS, G_HEAD_DIM, G_WIDTH, G_CHUNK = 4, 128, 512, 64
MIX_WIDTH = 2048
IN_SIZES = (512, 128, 128, 512, 1024, 1024, 512, 512, 512, 512, 4, 4)
N_IN = 5384
DEEPNORM_ALPHA = (2 * DEPTH) ** 0.25
LN_EPS = 1e-5
RMS_EPS = 1e-6
ADAM_LR, ADAM_B1, ADAM_B2, ADAM_EPS, ADAM_WD, ADAM_STEP = 0.001, 0.9, 0.999, 1e-08, 0.01, 10

NP = 5632
OFF_RX, OFF_RZ, OFF_AQ, OFF_AZ, OFF_GQKV, OFF_GZ, OFF_AK, OFF_AV, OFF_GBA = (
    0, 1024, 2048, 2560, 3072, 4608, 5120, 5248, 5376)
_ORIG_OFF = np.concatenate([[0], np.cumsum(IN_SIZES)])[:-1]
_PIECES = ((4, OFF_RX), (5, OFF_RZ), (0, OFF_AQ), (3, OFF_AZ), (6, OFF_GQKV), (7, OFF_GQKV + 512),
           (8, OFF_GQKV + 1024), (9, OFF_GZ), (1, OFF_AK), (2, OFF_AV), (10, OFF_GBA), (11, OFF_GBA + 4))
MIX_R, MIX_A, MIX_G = 0, 1024, 1536
VMEM_LIMIT = 56 * 1024 * 1024


def _pcall(body, **kw):
    return pl.pallas_call(body, **kw)


def _cp(sem, limit=VMEM_LIMIT):
    return pltpu.CompilerParams(dimension_semantics=sem, vmem_limit_bytes=limit)


def _sigmoid(x):
    return 1.0 / (1.0 + jnp.exp(-x))


def _silu(x):
    return x * _sigmoid(x)


def _dsilu(x):
    s = _sigmoid(x)
    return s * (1.0 + x * (1.0 - s))


def _log1p(x):
    u = 1.0 + x
    d = jnp.where(u == 1.0, 1.0, u - 1.0)
    return jnp.where(u == 1.0, x, jnp.log(u) * (x / d))


def _softplus(x):
    return jnp.maximum(x, 0.0) + _log1p(jnp.exp(-jnp.abs(x)))


def _expm1(x):
    u = jnp.exp(x)
    lu = jnp.log(u)
    safe = jnp.where((u == 1.0) | (lu == 0.0), 1.0, lu)
    r = (u - 1.0) * (x / safe)
    r = jnp.where(u == 1.0, x, r)
    return jnp.where(u - 1.0 == -1.0, -1.0, r)


def _nn(a, b):
    return lax.dot_general(a, b, (((1,), (0,)), ((), ())), preferred_element_type=F32)


def _nt(a, b):
    return lax.dot_general(a, b, (((1,), (1,)), ((), ())), preferred_element_type=F32)


def _tn(a, b):
    return lax.dot_general(a, b, (((0,), (0,)), ((), ())), preferred_element_type=F32)


def _b(x):
    return x.astype(BF16)


def _split3(x):
    hi = x.astype(BF16)
    r1 = x - hi.astype(F32)
    mid = r1.astype(BF16)
    lo = (r1 - mid.astype(F32)).astype(BF16)
    return hi, mid, lo


def _dot3(f, a, b):
    ah, am, _ = _split3(a)
    bh, bm, _ = _split3(b)
    return f(ah, bh) + (f(ah, bm) + f(am, bh))


def _dot_exact_lhs(f, a_bf16, b):
    bh, bm, bl = _split3(b)
    return f(a_bf16, bh) + (f(a_bf16, bm) + f(a_bf16, bl))


def _rot(x):
    w = x.shape[-1]
    lane = lax.broadcasted_iota(jnp.int32, (1, w), 1)
    return jnp.where((lane & 63) < 32, pltpu.roll(x, w - 32, 1), pltpu.roll(x, 32, 1))


def _conv_taps(ext, n):
    return [pltpu.roll(ext, 3 - k, 0)[8:8 + n] if k < 3 else ext[8:8 + n] for k in range(CONV_WIDTH)]


def _conv_taps_t(ext, n):
    m = ext.shape[0]
    return [pltpu.roll(ext, m - (3 - k), 0)[0:n] if k < 3 else ext[0:n] for k in range(CONV_WIDTH)]


def _scan_lin(a, b, reverse):
    n = a.shape[0]
    row = lax.broadcasted_iota(jnp.int32, (n, 1), 0)
    s = 1
    while s < n:
        if reverse:
            a_sh = pltpu.roll(a, n - s, 0)
            b_sh = pltpu.roll(b, n - s, 0)
            ok = row < (n - s)
        else:
            a_sh = pltpu.roll(a, s, 0)
            b_sh = pltpu.roll(b, s, 0)
            ok = row >= s
        b = jnp.where(ok, a * b_sh + b, b)
        a = jnp.where(ok, a * a_sh, a)
        s *= 2
    return a, b


def _matmul(a, b, *, ta, tb, tm, tn, tk, out_dtype, name, extra=None, alpha=0.0, out_blocks=None):
    if ta:
        K, M = a.shape
    else:
        M, K = a.shape
    if tb:
        N, K2 = b.shape
    else:
        K2, N = b.shape
    assert K == K2 and M % tm == 0 and N % tn == 0 and K % tk == 0, (a.shape, b.shape, tm, tn, tk)
    nk = K // tk
    ca = 0 if ta else 1
    cb = 1 if tb else 0
    has_extra = extra is not None

    def body(*refs):
        if has_extra:
            a_ref, b_ref, e_ref, o_ref, acc = refs
        else:
            a_ref, b_ref, o_ref, acc = refs
            e_ref = None
        k = pl.program_id(2)

        @pl.when(k == 0)
        def _():
            acc[...] = jnp.zeros_like(acc)

        acc[...] += lax.dot_general(a_ref[...], b_ref[...], (((ca,), (cb,)), ((), ())),
                                    preferred_element_type=F32)

        @pl.when(k == nk - 1)
        def _():
            r = acc[...]
            if e_ref is not None:
                r = r + alpha * e_ref[...]
            o_ref[...] = r.astype(o_ref.dtype)

    a_spec = (pl.BlockSpec((tk, tm), lambda i, j, k: (k, i)) if ta
              else pl.BlockSpec((tm, tk), lambda i, j, k: (i, k)))
    b_spec = (pl.BlockSpec((tn, tk), lambda i, j, k: (j, k)) if tb
              else pl.BlockSpec((tk, tn), lambda i, j, k: (k, j)))
    e_spec = pl.BlockSpec((tm, tn), lambda i, j, k: (i, j))
    if out_blocks is None:
        o_spec, o_shape = e_spec, (M, N)
    else:
        o_shape, o_block, o_map = out_blocks
        o_spec = pl.BlockSpec(o_block, lambda i, j, k: o_map(i, j))
    in_specs = [a_spec, b_spec] + ([e_spec] if has_extra else [])
    args = (a, b) + ((extra,) if has_extra else ())
    return _pcall(
        body, name=name, grid=(M // tm, N // tn, nk),
        in_specs=in_specs, out_specs=o_spec,
        out_shape=jax.ShapeDtypeStruct(o_shape, out_dtype),
        scratch_shapes=[pltpu.VMEM((tm, tn), F32)],
        compiler_params=_cp(("parallel", "parallel", "arbitrary")),
    )(*args)


def _outproj(ymix, wo, x, *, tm, name):
    S_, D = x.shape

    def body(y_ref, w_ref, x_ref, z_ref):
        z_ref[...] = DEEPNORM_ALPHA * x_ref[...] + _nn(y_ref[...], w_ref[...])

    return _pcall(
        body, name=name, grid=(S_ // tm,),
        in_specs=[pl.BlockSpec((tm, MIX_WIDTH), lambda i: (i, 0)),
                  pl.BlockSpec((MIX_WIDTH, D), lambda i: (0, 0)),
                  pl.BlockSpec((tm, D), lambda i: (i, 0))],
        out_specs=pl.BlockSpec((tm, D), lambda i: (i, 0)),
        out_shape=jax.ShapeDtypeStruct((S_, D), F32),
        compiler_params=_cp(("parallel",)),
    )(ymix, wo, x)


def _ln_stats(z):
    mu = jnp.mean(z, -1, keepdims=True)
    zc = z - mu
    var = jnp.mean(zc * zc, -1, keepdims=True)
    rstd = lax.rsqrt(var + LN_EPS)
    return zc * rstd, rstd


def _ln_fwd(z, g, b, *, tm, name):
    S_, D = z.shape

    def body(z_ref, g_ref, b_ref, y_ref, yb_ref):
        xh, _ = _ln_stats(z_ref[...])
        y = xh * g_ref[...] + b_ref[...]
        y_ref[...] = y
        yb_ref[...] = y.astype(BF16)

    row = pl.BlockSpec((tm, D), lambda i: (i, 0))
    vec = pl.BlockSpec((1, D), lambda i: (0, 0))
    return _pcall(
        body, name=name, grid=(S_ // tm,), in_specs=[row, vec, vec], out_specs=[row, row],
        out_shape=[jax.ShapeDtypeStruct((S_, D), F32), jax.ShapeDtypeStruct((S_, D), BF16)],
        compiler_params=_cp(("parallel",)),
    )(z, g.reshape(1, D), b.reshape(1, D))


def _ln_bwd(z, g, b, other, *, from_target, tm, name):
    S_, D = z.shape

    def body(z_ref, g_ref, b_ref, o_ref, dz_ref, dzb_ref, dg_ref, db_ref, loss_ref):
        i = pl.program_id(0)

        @pl.when(i == 0)
        def _():
            dg_ref[...] = jnp.zeros_like(dg_ref)
            db_ref[...] = jnp.zeros_like(db_ref)
            loss_ref[...] = jnp.zeros_like(loss_ref)

        xh, rstd = _ln_stats(z_ref[...])
        gam = g_ref[...]
        if from_target:
            err = xh * gam + b_ref[...] - o_ref[...]
            per_tok = jnp.mean(err * err, -1, keepdims=True)
            loss_ref[...] += 0.5 * jnp.sum(per_tok, 0, keepdims=True)
            dy = err * (1.0 / D)
        else:
            dy = o_ref[...]
        dxh = dy * gam
        m1 = jnp.mean(dxh, -1, keepdims=True)
        m2 = jnp.mean(dxh * xh, -1, keepdims=True)
        dz = rstd * (dxh - m1 - xh * m2)
        dz_ref[...] = dz
        dzb_ref[...] = dz.astype(BF16)
        dg_ref[...] += jnp.sum(dy * xh, 0, keepdims=True)
        db_ref[...] += jnp.sum(dy, 0, keepdims=True)

    row = pl.BlockSpec((tm, D), lambda i: (i, 0))
    vec = pl.BlockSpec((1, D), lambda i: (0, 0))
    one = pl.BlockSpec((1, 1), lambda i: (0, 0))
    return _pcall(
        body, name=name, grid=(S_ // tm,), in_specs=[row, vec, vec, row],
        out_specs=[row, row, vec, vec, one],
        out_shape=[jax.ShapeDtypeStruct((S_, D), F32), jax.ShapeDtypeStruct((S_, D), BF16),
                   jax.ShapeDtypeStruct((1, D), F32), jax.ShapeDtypeStruct((1, D), F32),
                   jax.ShapeDtypeStruct((1, 1), F32)],
        compiler_params=_cp(("arbitrary",)),
    )(z, g.reshape(1, D), b.reshape(1, D), other)


def _attn_common(i, T, b, h, qr, kd, vd, sk_ref):
    lane = lax.broadcasted_iota(jnp.int32, (1, 128), 1)
    lof = (lane < 64).astype(F32)
    hif = 1.0 - lof
    r0 = b * 128
    ri = lax.broadcasted_iota(jnp.int32, (512, 256), 0)
    cj = lax.broadcasted_iota(jnp.int32, (512, 256), 1)
    diff = (ri & 127) - cj + 128
    valid = (diff >= 0) & (diff < WINDOW) & ((i * T + r0 - 128 + cj) >= 0)
    grp = lax.broadcasted_iota(jnp.int32, (512, 1), 0) >> 7
    skv = jnp.zeros((512, 1), F32)
    for g in range(4):
        skv = jnp.where(grp == g, sk_ref[h * 4 + g], skv)
    pairs = [qr[r0:r0 + 128, h * 256 + p * 128:h * 256 + (p + 1) * 128] for p in (0, 1)]
    qs = _b(jnp.concatenate([pairs[0] * lof, pairs[0] * hif, pairs[1] * lof, pairs[1] * hif], 0))
    k2 = kd[h][r0:r0 + 256]
    v2 = vd[h][r0:r0 + 256]
    s = jnp.where(valid, _nt(qs, k2) * (A_HEAD_DIM ** -0.5), -jnp.inf)
    m = jnp.maximum(jnp.max(s, 1, keepdims=True), skv)
    p = jnp.exp(s - m)
    esk = jnp.exp(skv - m)
    rz = 1.0 / (jnp.sum(p, 1, keepdims=True) + esk)
    prob = p * rz
    o4 = _nn(_b(prob), v2)
    return lof, hif, qs, k2, v2, prob, esk * rz, o4


def _attn_prep(T, q_ref, k_ref, v_ref, c_ref, s_ref, kprev, vprev):
    C = c_ref[...]
    Sg = s_ref[...]
    C4 = jnp.concatenate([C] * 4, 1)
    S4 = jnp.concatenate([Sg] * 4, 1)
    q = q_ref[...]
    qr = q * C4 + _rot(q) * S4
    k = k_ref[...]
    kr = k * C + _rot(k) * Sg
    v = v_ref[...]
    kext = jnp.concatenate([kprev[...], kr], 0)
    vext = jnp.concatenate([vprev[...], v], 0)
    kprev[...] = kr[T - 128:]
    vprev[...] = v[T - 128:]
    lo = lax.broadcasted_iota(jnp.int32, (1, 128), 1) < 64
    kroll = pltpu.roll(kext, 64, 1)
    vroll = pltpu.roll(vext, 64, 1)
    kd = [_b(jnp.where(lo, kext, kroll)), _b(jnp.where(lo, kroll, kext))]
    vd = [_b(jnp.where(lo, vext, vroll)), _b(jnp.where(lo, vroll, vext))]
    return C, Sg, C4, S4, qr, kd, vd


def _attn_specs(T):
    return [pl.BlockSpec(memory_space=pltpu.SMEM),
            pl.BlockSpec((T, 512), lambda i: (i, OFF_AQ // 512)),
            pl.BlockSpec((T, 512), lambda i: (i, OFF_AZ // 512)),
            pl.BlockSpec((T, 128), lambda i: (i, OFF_AK // 128)),
            pl.BlockSpec((T, 128), lambda i: (i, OFF_AV // 128)),
            pl.BlockSpec((T, 128), lambda i: (i, 0)),
            pl.BlockSpec((T, 128), lambda i: (i, 0))]


def _attn_fwd(proj, rope_c, rope_s, sinks, *, T, name):
    S_ = proj.shape[0]
    nb = T // 128

    def body(sk_ref, q_ref, z_ref, k_ref, v_ref, c_ref, s_ref, y_ref, kprev, vprev):
        i = pl.program_id(0)

        @pl.when(i == 0)
        def _():
            kprev[...] = jnp.zeros_like(kprev)
            vprev[...] = jnp.zeros_like(vprev)

        _, _, _, _, qr, kd, vd = _attn_prep(T, q_ref, k_ref, v_ref, c_ref, s_ref, kprev, vprev)
        for b in range(nb):
            r0 = b * 128
            for h in range(2):
                lof, hif, _, _, _, _, _, o4 = _attn_common(i, T, b, h, qr, kd, vd, sk_ref)
                for p in range(2):
                    cs = slice(h * 256 + p * 128, h * 256 + (p + 1) * 128)
                    o = o4[2 * p * 128:(2 * p + 1) * 128] * lof + o4[(2 * p + 1) * 128:(2 * p + 2) * 128] * hif
                    y_ref[r0:r0 + 128, cs] = (o * _silu(z_ref[r0:r0 + 128, cs])).astype(BF16)

    return _pcall(
        body, name=name, grid=(S_ // T,), in_specs=_attn_specs(T),
        out_specs=pl.BlockSpec((T, 512), lambda i: (i, 0)),
        out_shape=jax.ShapeDtypeStruct((S_, 512), BF16),
        scratch_shapes=[pltpu.VMEM((128, 128), F32), pltpu.VMEM((128, 128), F32)],
        compiler_params=_cp(("arbitrary",)),
    )(sinks, proj, proj, proj, proj, rope_c, rope_s)


def _attn_bwd(proj, rope_c, rope_s, sinks, dymix, *, T, name):
    S_ = proj.shape[0]
    nb = T // 128
    nt = S_ // T

    def body(sk_ref, q_ref, z_ref, k_ref, v_ref, c_ref, s_ref, dy_ref,
             dq_ref, dz_ref, dk_ref, dv_ref, dkt_ref, dvt_ref, dsk_ref,
             kprev, vprev, cprev, sprev, dkacc, dvacc, dqacc):
        i = pl.program_id(0)

        @pl.when(i == 0)
        def _():
            kprev[...] = jnp.zeros_like(kprev)
            vprev[...] = jnp.zeros_like(vprev)
            cprev[...] = jnp.zeros_like(cprev)
            sprev[...] = jnp.zeros_like(sprev)
            dkacc[...] = jnp.zeros_like(dkacc)
            dvacc[...] = jnp.zeros_like(dvacc)
            dsk_ref[...] = jnp.zeros_like(dsk_ref)

        @pl.when(i > 0)
        def _():
            dkacc[0:128, :] = dkacc[T:T + 128, :]
            dvacc[0:128, :] = dvacc[T:T + 128, :]
            dkacc[128:, :] = jnp.zeros((T, 128), F32)
            dvacc[128:, :] = jnp.zeros((T, 128), F32)

        C, Sg, C4, S4, qr, kd, vd = _attn_prep(T, q_ref, k_ref, v_ref, c_ref, s_ref, kprev, vprev)
        lane = lax.broadcasted_iota(jnp.int32, (1, 128), 1)
        for b in range(nb):
            r0 = b * 128
            for h in range(2):
                lof, hif, qs, k2, v2, prob, psink, o4 = _attn_common(i, T, b, h, qr, kd, vd, sk_ref)
                dos = []
                for p in range(2):
                    cs = slice(h * 256 + p * 128, h * 256 + (p + 1) * 128)
                    o = o4[2 * p * 128:(2 * p + 1) * 128] * lof + o4[(2 * p + 1) * 128:(2 * p + 2) * 128] * hif
                    zc = z_ref[r0:r0 + 128, cs]
                    dyc = dy_ref[r0:r0 + 128, cs]
                    dz_ref[r0:r0 + 128, cs] = (dyc * o * _dsilu(zc)).astype(BF16)
                    do = dyc * _silu(zc)
                    dos += [do * lof, do * hif]
                dos = jnp.concatenate(dos, 0)
                os_ = jnp.concatenate([o4[0:128] * lof, o4[128:256] * hif, o4[256:384] * lof, o4[384:512] * hif], 0)
                delta = jnp.sum(dos * os_, 1, keepdims=True)
                dosb = _b(dos)
                dp = _nt(dosb, v2)
                ds = prob * (dp - delta)
                dsv = -psink * delta
                for g in range(4):
                    sg = jnp.sum(dsv[g * 128:(g + 1) * 128], 0, keepdims=True)
                    hd = h * 4 + g
                    dsk_ref[hd:hd + 1, :] += jnp.broadcast_to(sg, (1, 128))
                dsb = _b(ds * (A_HEAD_DIM ** -0.5))
                dqs = _nn(dsb, k2)
                for p in range(2):
                    cs = slice(h * 256 + p * 128, h * 256 + (p + 1) * 128)
                    dqacc[r0:r0 + 128, cs] = (dqs[2 * p * 128:(2 * p + 1) * 128] * lof
                                              + dqs[(2 * p + 1) * 128:(2 * p + 2) * 128] * hif)
                dkdup = _tn(dsb, qs)
                dvdup = _tn(_b(prob), dosb)
                half = (lane < 64) if h == 0 else (lane >= 64)
                dkacc[r0:r0 + 256, :] += jnp.where(half, dkdup + pltpu.roll(dkdup, 64, 1), 0.0)
                dvacc[r0:r0 + 256, :] += jnp.where(half, dvdup + pltpu.roll(dvdup, 64, 1), 0.0)
        dqr = dqacc[...]
        dq_ref[...] = (dqr * C4 + _rot(dqr * S4)).astype(BF16)
        cext = jnp.concatenate([cprev[...], C], 0)
        sext = jnp.concatenate([sprev[...], Sg], 0)
        dke = dkacc[...]
        dkp = dke * cext + _rot(dke * sext)
        dk_ref[...] = dkp[0:T].astype(BF16)
        dkt_ref[...] = dkp[T:T + 128].astype(BF16)
        dve = dvacc[...]
        dv_ref[...] = dve[0:T].astype(BF16)
        dvt_ref[...] = dve[T:T + 128].astype(BF16)
        cprev[...] = C[T - 128:]
        sprev[...] = Sg[T - 128:]

    wide = pl.BlockSpec((T, 512), lambda i: (i, 0))
    nar = pl.BlockSpec((T, 128), lambda i: (i, 0))
    tail = pl.BlockSpec((128, 128), lambda i: (0, 0))
    return _pcall(
        body, name=name, grid=(nt,),
        in_specs=_attn_specs(T) + [pl.BlockSpec((T, 512), lambda i: (i, MIX_A // 512))],
        out_specs=[wide, wide, nar, nar, tail, tail, pl.BlockSpec((8, 128), lambda i: (0, 0))],
        out_shape=[jax.ShapeDtypeStruct((S_, 512), BF16), jax.ShapeDtypeStruct((S_, 512), BF16),
                   jax.ShapeDtypeStruct((S_, 128), BF16), jax.ShapeDtypeStruct((S_, 128), BF16),
                   jax.ShapeDtypeStruct((128, 128), BF16), jax.ShapeDtypeStruct((128, 128), BF16),
                   jax.ShapeDtypeStruct((8, 128), F32)],
        scratch_shapes=[pltpu.VMEM((128, 128), F32)] * 4
        + [pltpu.VMEM((T + 128, 128), F32), pltpu.VMEM((T + 128, 128), F32), pltpu.VMEM((T, 512), F32)],
        compiler_params=_cp(("arbitrary",)),
    )(sinks, proj, proj, proj, proj, rope_c, rope_s, dymix)


def _rg_gates(xr, wa_ref, ba_ref, wx_ref, bx_ref, lam_ref):
    xb = _b(xr)
    pre_a = jnp.concatenate([_nn(xb[:, n * 128:(n + 1) * 128], wa_ref[n]) for n in range(R_BLOCKS)], 1) + ba_ref[...]
    pre_x = jnp.concatenate([_nn(xb[:, n * 128:(n + 1) * 128], wx_ref[n]) for n in range(R_BLOCKS)], 1) + bx_ref[...]
    r = _sigmoid(pre_a)
    ig = _sigmoid(pre_x)
    sp = _softplus(-lam_ref[...])
    log_a = -R_C * r * sp
    a = jnp.exp(log_a)
    mult = jnp.sqrt(-_expm1(2.0 * log_a))
    return xb, r, ig, sp, a, mult


def _rg_param_specs():
    C = R_WIDTH
    vec = pl.BlockSpec((1, C), lambda i: (0, 0))
    blk = pl.BlockSpec((R_BLOCKS, 128, 128), lambda i: (0, 0, 0))
    return [pl.BlockSpec((CONV_WIDTH, C), lambda i: (0, 0)), vec, blk, vec, blk, vec, vec]


def _rglru_fwd(proj, cw, cb, wa, ba, wx, bx, lam, *, T, name):
    S_ = proj.shape[0]
    C = R_WIDTH

    def body(rx_ref, rz_ref, cw_ref, cb_ref, wa_ref, ba_ref, wx_ref, bx_ref, lam_ref,
             h_ref, y_ref, halo, hcar):
        i = pl.program_id(0)

        @pl.when(i == 0)
        def _():
            halo[...] = jnp.zeros_like(halo)
            hcar[...] = jnp.zeros_like(hcar)

        rx = rx_ref[...]
        ext = jnp.concatenate([halo[...], rx], 0)
        halo[...] = rx[T - 8:]
        taps = _conv_taps(ext, T)
        xr = cb_ref[...] + sum(cw_ref[k:k + 1, :] * taps[k] for k in range(CONV_WIDTH))
        _, _, ig, _, a, mult = _rg_gates(xr, wa_ref, ba_ref, wx_ref, bx_ref, lam_ref)
        u = mult * (ig * xr)
        acum, hloc = _scan_lin(a, u, False)
        h = hloc + acum * hcar[0:1, :]
        hcar[...] = jnp.broadcast_to(h[T - 1:T, :], (8, C))
        h_ref[...] = h
        y_ref[...] = (h * _silu(rz_ref[...])).astype(BF16)

    row = pl.BlockSpec((T, C), lambda i: (i, 0))
    return _pcall(
        body, name=name, grid=(S_ // T,),
        in_specs=[pl.BlockSpec((T, C), lambda i: (i, OFF_RX // C)),
                  pl.BlockSpec((T, C), lambda i: (i, OFF_RZ // C))] + _rg_param_specs(),
        out_specs=[row, row],
        out_shape=[jax.ShapeDtypeStruct((S_, C), F32), jax.ShapeDtypeStruct((S_, C), BF16)],
        scratch_shapes=[pltpu.VMEM((8, C), F32), pltpu.VMEM((8, C), F32)],
        compiler_params=_cp(("arbitrary",)),
    )(proj, proj, cw, cb.reshape(1, C), _b(wa), ba.reshape(1, C), _b(wx), bx.reshape(1, C), lam.reshape(1, C))


def _rglru_bwd(proj, h, dymix, cw, cb, wa, ba, wx, bx, lam, *, T, name):
    S_ = proj.shape[0]
    C = R_WIDTH
    nt = S_ // T
    t8 = T // 8

    def body(rx_ref, rxp_ref, rz_ref, h_ref, hp_ref, dy_ref,
             cw_ref, cb_ref, wa_ref, ba_ref, wx_ref, bx_ref, lam_ref, wat_ref, wxt_ref,
             drx_ref, drz_ref, dcw_ref, dcb_ref, dwa_ref, dba_ref, dwx_ref, dbx_ref, dlam_ref,
             afirst, gfirst, dhalo):
        i = pl.program_id(0)
        first_tile = (i == nt - 1)

        @pl.when(i == 0)
        def _():
            afirst[...] = jnp.zeros_like(afirst)
            gfirst[...] = jnp.zeros_like(gfirst)
            dhalo[...] = jnp.zeros_like(dhalo)
            for r in (dcw_ref, dcb_ref, dwa_ref, dba_ref, dwx_ref, dbx_ref, dlam_ref):
                r[...] = jnp.zeros_like(r)

        keep = jnp.where(first_tile, 0.0, 1.0)
        rx = rx_ref[...]
        ext = jnp.concatenate([rxp_ref[...] * keep, rx], 0)
        taps = _conv_taps(ext, T)
        xr = cb_ref[...] + sum(cw_ref[k:k + 1, :] * taps[k] for k in range(CONV_WIDTH))
        xb, r, ig, sp, a, mult = _rg_gates(xr, wa_ref, ba_ref, wx_ref, bx_ref, lam_ref)
        hh = h_ref[...]
        rz = rz_ref[...]
        dy = dy_ref[...]
        drz_ref[...] = (dy * hh * _dsilu(rz)).astype(BF16)
        dh = dy * _silu(rz)
        row = lax.broadcasted_iota(jnp.int32, (T, 1), 0)
        c = jnp.where(row == T - 1, afirst[0:1, :], pltpu.roll(a, T - 1, 0))
        ccum, gloc = _scan_lin(c, dh, True)
        g = gloc + ccum * gfirst[0:1, :]
        afirst[...] = jnp.broadcast_to(a[0:1, :], (8, C))
        gfirst[...] = jnp.broadcast_to(g[0:1, :], (8, C))
        hprev = jnp.where(row == 0, hp_ref[7:8, :] * keep, pltpu.roll(hh, 1, 0))
        da = g * hprev
        gx = ig * xr
        dgx = g * mult
        dmult = g * gx
        dlog_a = da * a - dmult * (a * a) / mult
        dpre_a = dlog_a * (-R_C * sp) * r * (1.0 - r)
        dpre_x = dgx * xr * ig * (1.0 - ig)
        dlam_ref[...] += jnp.sum(dlog_a * (-R_C * r), 0, keepdims=True) * (-_sigmoid(-lam_ref[...]))
        dab = _b(dpre_a)
        dxb = _b(dpre_x)
        dxr = dgx * ig + jnp.concatenate(
            [_nn(dab[:, n * 128:(n + 1) * 128], wat_ref[n]) + _nn(dxb[:, n * 128:(n + 1) * 128], wxt_ref[n])
             for n in range(R_BLOCKS)], 1)
        for n in range(R_BLOCKS):
            cs = slice(n * 128, (n + 1) * 128)
            dwa_ref[n] += _tn(xb[:, cs], dab[:, cs])
            dwx_ref[n] += _tn(xb[:, cs], dxb[:, cs])
        dba_ref[...] += jnp.sum(dpre_a, 0, keepdims=True)
        dbx_ref[...] += jnp.sum(dpre_x, 0, keepdims=True)
        dcb_ref[...] += jnp.sum(dxr, 0, keepdims=True)
        for k in range(CONV_WIDTH):
            dcw_ref[k:k + 1, :] += jnp.sum(dxr * taps[k], 0, keepdims=True)
        ext2 = jnp.concatenate([dxr, dhalo[...]], 0)
        tt = _conv_taps_t(ext2, T)
        drx_ref[...] = sum(cw_ref[k:k + 1, :] * tt[k] for k in range(CONV_WIDTH)).astype(BF16)
        dhalo[...] = dxr[0:8]

    def rev(i):
        return nt - 1 - i

    def prev8(i):
        return jnp.maximum(rev(i) * t8 - 1, 0)

    vec = pl.BlockSpec((1, C), lambda i: (0, 0))
    blk = pl.BlockSpec((R_BLOCKS, 128, 128), lambda i: (0, 0, 0))
    row = pl.BlockSpec((T, C), lambda i: (rev(i), 0))
    wat = _b(jnp.swapaxes(wa, 1, 2))
    wxt = _b(jnp.swapaxes(wx, 1, 2))
    return _pcall(
        body, name=name, grid=(nt,),
        in_specs=[pl.BlockSpec((T, C), lambda i: (rev(i), OFF_RX // C)),
                  pl.BlockSpec((8, C), lambda i: (prev8(i), OFF_RX // C)),
                  pl.BlockSpec((T, C), lambda i: (rev(i), OFF_RZ // C)),
                  row,
                  pl.BlockSpec((8, C), lambda i: (prev8(i), 0)),
                  pl.BlockSpec((T, C), lambda i: (rev(i), MIX_R // C)),
                  ] + _rg_param_specs() + [blk, blk],
        out_specs=[row, row, pl.BlockSpec((CONV_WIDTH, C), lambda i: (0, 0)), vec, blk, vec, blk, vec, vec],
        out_shape=[jax.ShapeDtypeStruct((S_, C), BF16), jax.ShapeDtypeStruct((S_, C), BF16),
                   jax.ShapeDtypeStruct((CONV_WIDTH, C), F32), jax.ShapeDtypeStruct((1, C), F32),
                   jax.ShapeDtypeStruct((R_BLOCKS, 128, 128), F32), jax.ShapeDtypeStruct((1, C), F32),
                   jax.ShapeDtypeStruct((R_BLOCKS, 128, 128), F32), jax.ShapeDtypeStruct((1, C), F32),
                   jax.ShapeDtypeStruct((1, C), F32)],
        scratch_shapes=[pltpu.VMEM((8, C), F32)] * 3,
        compiler_params=_cp(("arbitrary",)),
    )(proj, proj, proj, h, h, dymix, cw, cb.reshape(1, C), _b(wa), ba.reshape(1, C), _b(wx), bx.reshape(1, C),
      lam.reshape(1, C), wat, wxt)


GW3 = 3 * G_WIDTH


def _lane_col(x, lane_idx):
    lane = lax.broadcasted_iota(jnp.int32, (1, x.shape[1]), 1)
    return jnp.sum(jnp.where(lane == lane_idx, x, 0.0), 1, keepdims=True)


def _gdn_pre(ext, T, cw_ref, gba, pv_ref):
    taps = _conv_taps(ext, T)
    c = sum(cw_ref[k:k + 1, :] * taps[k] for k in range(CONV_WIDTH))
    qkv = _silu(c)
    beta = _sigmoid(gba)
    sarg = gba + pv_ref[1:2, :]
    nea = -jnp.exp(pv_ref[0:1, :])
    gdec = nea * _softplus(sarg)
    ri = lax.broadcasted_iota(jnp.int32, (T, T), 0)
    cj = lax.broadcasted_iota(jnp.int32, (T, T), 1)
    same = (ri >> 6) == (cj >> 6)
    ltri = jnp.where((ri >= cj) & same, 1.0, 0.0).astype(BF16)
    gc = _dot_exact_lhs(_nn, ltri, gdec)
    return taps, c, qkv, beta, sarg, nea, gdec, gc


def _gdn_masks():
    ri = lax.broadcasted_iota(jnp.int32, (128, 128), 0)
    cj = lax.broadcasted_iota(jnp.int32, (128, 128), 1)
    same = (ri >> 6) == (cj >> 6)
    return (ri >= cj) & same, (ri > cj) & same, ri == cj


def _lockstep(gens):
    out = [None] * len(gens)
    live = list(range(len(gens)))
    while live:
        still = []
        for k in live:
            try:
                next(gens[k])
                still.append(k)
            except StopIteration as stop:
                out[k] = stop.value
        live = still
    return out


def _gdn_chunk(qkv, beta, gc, rs, h, tm=None):
    tril, strict, eye = _gdn_masks()
    rowi = lax.broadcasted_iota(jnp.int32, (128, 1), 0)
    lane = lax.broadcasted_iota(jnp.int32, (1, 128), 1)
    qh = qkv[rs, h * 128:(h + 1) * 128]
    kh = qkv[rs, 512 + h * 128:512 + (h + 1) * 128]
    vh = qkv[rs, 1024 + h * 128:1024 + (h + 1) * 128]
    rq = lax.rsqrt(jnp.sum(qh * qh, 1, keepdims=True) + RMS_EPS)
    rk = lax.rsqrt(jnp.sum(kh * kh, 1, keepdims=True) + RMS_EPS)
    qn = qh * (rq * (G_HEAD_DIM ** -0.5))
    kn = kh * rk
    gcb = gc[rs]
    gcol = _lane_col(gcb, 4 + h)
    bcol = _lane_col(beta[rs], h)
    grow = _dot_exact_lhs(_nt, jnp.ones((128, 128), BF16), jnp.where(lane == 4 + h, gcb, 0.0))
    D = jnp.where(tril, jnp.exp(jnp.minimum(gcol - grow, 0.0)), 0.0)
    kb = kn * bcol
    vb = vh * bcol
    knb = _b(kn)
    A = _nt(_b(kb), knb)
    Bm = _nt(_b(qn), knb)
    yield
    if tm is None:
        N = jnp.where(strict, -(A * D), 0.0)
        tm = jnp.where(eye, 1.0, 0.0) + N
        npow = N
        for _ in range(5):
            npow = _dot3(_nn, npow, npow)
            yield
            tm = tm + _dot3(_nn, tm, npow)
            yield
    eg = jnp.exp(gcol)
    u = _dot3(_nn, tm, vb)
    w = _dot3(_nn, tm, kb * eg)
    yield
    qk = jnp.where(tril, Bm * D, 0.0)
    qd = qn * eg
    gla = jnp.sum(jnp.where(rowi == 63, gcol, 0.0), 0, keepdims=True)
    glb = jnp.sum(jnp.where(rowi == 127, gcol, 0.0), 0, keepdims=True)
    ed = jnp.exp(jnp.where(rowi < 64, gla, glb) - gcol)
    kd = kn * ed
    return dict(qh=qh, kh=kh, vh=vh, rq=rq, rk=rk, qn=qn, kn=kn, gcol=gcol, bcol=bcol, D=D, A=A, Bm=Bm,
                tm=tm, eg=eg, ed=ed, u=u, w=w, qk=qk, qd=qd, kd=kd, kb=kb, vb=vb,
                gla=jnp.exp(gla), glb=jnp.exp(glb))


def _gdn_scan(q, sa):
    sab = _b(sa)
    wb = _b(q["w"])
    vna = q["u"] - _nn(wb, sab)
    yield
    sb = sa * q["gla"] + _tn(_b(q["kd"][0:64]), _b(vna[0:64]))
    yield
    sbb = _b(sb)
    vnb = q["u"] - _nn(wb, sbb)
    yield
    sn = sb * q["glb"] + _tn(_b(q["kd"][64:128]), _b(vnb[64:128]))
    yield
    vn = jnp.concatenate([vna[0:64], vnb[64:128]], 0)
    qdb = _b(q["qd"])
    o = jnp.concatenate([_nn(qdb[0:64], sab), _nn(qdb[64:128], sbb)], 0) + _nn(_b(q["qk"]), _b(vn))
    return sb, sn, vn, o


def _gdn_param_specs():
    return [pl.BlockSpec((CONV_WIDTH, GW3), lambda i: (0, 0)),
            pl.BlockSpec((8, 128), lambda i: (0, 0)),
            pl.BlockSpec((1, 128), lambda i: (0, 0))]


def _gdn_pvec(a_log, dt_bias):
    z = jnp.zeros((8, 128), F32)
    return z.at[0, 4:8].set(a_log).at[1, 4:8].set(dt_bias)


def _gdn_fwd(proj, cw, a_log, dt_bias, nw, *, T, name):
    S_ = proj.shape[0]
    nu = T // 128

    def body(x_ref, z_ref, g_ref, cw_ref, pv_ref, nw_ref, y_ref, st_ref, tm_ref, halo, state):
        i = pl.program_id(0)

        @pl.when(i == 0)
        def _():
            halo[...] = jnp.zeros_like(halo)
            state[...] = jnp.zeros_like(state)

        x = x_ref[...]
        ext = jnp.concatenate([halo[...], x], 0)
        halo[...] = x[T - 8:]
        _, _, qkv, beta, _, _, _, gc = _gdn_pre(ext, T, cw_ref, g_ref[...], pv_ref)
        items = [(dc, h) for dc in range(nu) for h in range(G_HEADS)]
        qs = _lockstep([_gdn_chunk(qkv, beta, gc, slice(dc * 128, (dc + 1) * 128), h) for dc, h in items])

        def head_chain(h):
            s = state[h]
            for dc in range(nu):
                rs = slice(dc * 128, (dc + 1) * 128)
                q = qs[dc * G_HEADS + h]
                sb, sn, _, o = yield from _gdn_scan(q, s)
                st_ref[2 * dc, h] = s
                st_ref[2 * dc + 1, h] = sb
                tm_ref[dc, h] = q["tm"]
                s = sn
                yield
                rn = lax.rsqrt(jnp.mean(o * o, 1, keepdims=True) + RMS_EPS)
                cs = slice(h * 128, (h + 1) * 128)
                y_ref[rs, cs] = (o * rn * nw_ref[...] * _silu(z_ref[rs, cs])).astype(BF16)
                yield
            state[h] = s

        _lockstep([head_chain(h) for h in range(G_HEADS)])

    return _pcall(
        body, name=name, grid=(S_ // T,),
        in_specs=[pl.BlockSpec((T, GW3), lambda i: (i, OFF_GQKV // GW3)),
                  pl.BlockSpec((T, 512), lambda i: (i, OFF_GZ // 512)),
                  pl.BlockSpec((T, 128), lambda i: (i, OFF_GBA // 128))] + _gdn_param_specs(),
        out_specs=[pl.BlockSpec((T, 512), lambda i: (i, 0)),
                   pl.BlockSpec((2 * nu, G_HEADS, 128, 128), lambda i: (i, 0, 0, 0)),
                   pl.BlockSpec((nu, G_HEADS, 128, 128), lambda i: (i, 0, 0, 0))],
        out_shape=[jax.ShapeDtypeStruct((S_, 512), BF16),
                   jax.ShapeDtypeStruct((S_ // 64, G_HEADS, 128, 128), F32),
                   jax.ShapeDtypeStruct((S_ // 128, G_HEADS, 128, 128), F32)],
        scratch_shapes=[pltpu.VMEM((8, GW3), F32), pltpu.VMEM((G_HEADS, 128, 128), F32)],
        compiler_params=_cp(("arbitrary",)),
    )(proj, proj, proj, cw, _gdn_pvec(a_log, dt_bias), nw.reshape(1, 128))


def _gdn_bwd(proj, states, tms, dymix, cw, a_log, dt_bias, nw, *, T, name):
    S_ = proj.shape[0]
    nt = S_ // T
    nu = T // 128
    t8 = T // 8

    def body(x_ref, xp_ref, z_ref, g_ref, st_ref, tm_ref, dy_ref, cw_ref, pv_ref, nw_ref,
             dx_ref, dz_ref, dg_ref, dcw_ref, dpv_ref, dnw_ref, dstate, dhalo, dqkv, dbg):
        i = pl.program_id(0)
        first_tile = (i == nt - 1)

        @pl.when(i == 0)
        def _():
            dstate[...] = jnp.zeros_like(dstate)
            dhalo[...] = jnp.zeros_like(dhalo)
            dcw_ref[...] = jnp.zeros_like(dcw_ref)
            dpv_ref[...] = jnp.zeros_like(dpv_ref)
            dnw_ref[...] = jnp.zeros_like(dnw_ref)

        keep = jnp.where(first_tile, 0.0, 1.0)
        ext = jnp.concatenate([xp_ref[...] * keep, x_ref[...]], 0)
        G = g_ref[...]
        taps, c, qkv, beta, sarg, nea, gdec, gc = _gdn_pre(ext, T, cw_ref, G, pv_ref)
        tril, strict, _ = _gdn_masks()
        rowi = lax.broadcasted_iota(jnp.int32, (128, 1), 0)
        lane = lax.broadcasted_iota(jnp.int32, (1, 128), 1)
        ones_b = jnp.ones((128, 128), BF16)
        nwv = nw_ref[...]
        items = [(dc, h) for dc in range(nu) for h in range(G_HEADS)]

        def recompute(dc, h):
            q = yield from _gdn_chunk(qkv, beta, gc, slice(dc * 128, (dc + 1) * 128), h, tm=tm_ref[dc, h])
            sa = st_ref[2 * dc, h]
            sb, _, vn, o = yield from _gdn_scan(q, sa)
            return q, sa, sb, vn, o

        fw = _lockstep([recompute(dc, h) for dc, h in items])
        chain_out = {}

        def head_chain(h):
            dS = dstate[h]
            for dc in reversed(range(nu)):
                rs = slice(dc * 128, (dc + 1) * 128)
                q, sa, sb, vn, o = fw[dc * G_HEADS + h]
                cs = slice(h * 128, (h + 1) * 128)
                zg = z_ref[rs, cs]
                dy = dy_ref[rs, cs]
                rn = lax.rsqrt(jnp.mean(o * o, 1, keepdims=True) + RMS_EPS)
                don = dy * _silu(zg)
                dz_ref[rs, cs] = (dy * (o * rn * nwv) * _dsilu(zg)).astype(BF16)
                dnw_ref[...] += jnp.sum(don * o * rn, 0, keepdims=True)
                tt = don * nwv
                do = rn * (tt - o * (rn * rn) * jnp.mean(tt * o, 1, keepdims=True))
                yield
                dob = _b(do)
                sab, sbb = _b(sa), _b(sb)
                vnb16 = _b(vn)
                dqk = jnp.where(tril, _nt(dob, vnb16), 0.0)
                dvn_o = _tn(_b(q["qk"]), dob)
                dSb16 = _b(dS)
                kdb = _b(q["kd"])
                wb = _b(q["w"])
                qdb = _b(q["qd"])
                yield
                dvn_b = dvn_o[64:128] + _nn(kdb[64:128], dSb16)
                dkd_b = _nt(vnb16[64:128], dSb16)
                dgl_b = jnp.sum(jnp.sum(dS * sb, 1, keepdims=True), 0, keepdims=True)
                yield
                dvn_b16 = _b(dvn_b)
                dw_b = -_nt(dvn_b16, sbb)
                dqd_b = _nt(dob[64:128], sbb)
                dSm = q["glb"] * dS + _tn(qdb[64:128], dob[64:128]) - _tn(wb[64:128], dvn_b16)
                yield
                dSm16 = _b(dSm)
                dvn_a = dvn_o[0:64] + _nn(kdb[0:64], dSm16)
                dkd_a = _nt(vnb16[0:64], dSm16)
                dgl_a = jnp.sum(jnp.sum(dSm * sa, 1, keepdims=True), 0, keepdims=True)
                yield
                dvn_a16 = _b(dvn_a)
                dw_a = -_nt(dvn_a16, sab)
                dqd_a = _nt(dob[0:64], sab)
                dS = q["gla"] * dSm + _tn(qdb[0:64], dob[0:64]) - _tn(wb[0:64], dvn_a16)
                chain_out[dc, h] = (dqk, jnp.concatenate([dvn_a, dvn_b], 0), jnp.concatenate([dw_a, dw_b], 0),
                                    jnp.concatenate([dkd_a, dkd_b], 0), jnp.concatenate([dqd_a, dqd_b], 0),
                                    dgl_a, dgl_b)
                yield
            dstate[h] = dS

        _lockstep([head_chain(h) for h in range(G_HEADS)])

        def local(dc, h):
            rs = slice(dc * 128, (dc + 1) * 128)
            q = fw[dc * G_HEADS + h][0]
            dqk, du, dw, dkd, dqd, dgl_a, dgl_b = chain_out[dc, h]
            if True:
                dvb = _dot3(_tn, q["tm"], du)
                dkbe = _dot3(_tn, q["tm"], dw)
                yield
                dM = jnp.where(strict, -(_nt(_b(dvb), _b(q["u"])) + _nt(_b(dkbe), _b(q["w"]))), 0.0)
                yield
                D = q["D"]
                dA = dM * D
                dB = dqk * D
                dDD = (dM * q["A"] + dqk * q["Bm"]) * D
                dh_, dm_, dl_ = _split3(dDD)
                colsum = _tn(dh_, ones_b) + (_tn(dm_, ones_b) + _tn(dl_, ones_b))
                dgc = jnp.sum(dDD, 1, keepdims=True) - _lane_col(colsum, 0)
                yield
                dA16, dB16 = _b(dA), _b(dB)
                knb, kbb, qnb = _b(q["kn"]), _b(q["kb"]), _b(q["qn"])
                eg, ed = q["eg"], q["ed"]
                dkb = _nn(dA16, knb) + dkbe * eg
                dkn = _tn(dA16, kbb) + _tn(dB16, qnb) + dkd * ed + dkb * q["bcol"]
                dqn = _nn(dB16, knb) + dqd * eg
                yield
                deg = jnp.sum(dkbe * q["kb"], 1, keepdims=True) + jnp.sum(dqd * q["qn"], 1, keepdims=True)
                ded = jnp.sum(dkd * q["kn"], 1, keepdims=True) * ed
                dgc = dgc + deg * eg - ded
                tail_a = jnp.sum(jnp.where(rowi < 64, ded, 0.0), 0, keepdims=True) + dgl_a * q["gla"]
                tail_b = jnp.sum(jnp.where(rowi >= 64, ded, 0.0), 0, keepdims=True) + dgl_b * q["glb"]
                dgc = dgc + jnp.where(rowi == 63, tail_a, 0.0) + jnp.where(rowi == 127, tail_b, 0.0)
                dbeta = jnp.sum(dkb * q["kn"], 1, keepdims=True) + jnp.sum(dvb * q["vh"], 1, keepdims=True)
                bcol = q["bcol"]
                blk = jnp.where(lane == h, dbeta * bcol * (1.0 - bcol), 0.0) + jnp.where(lane == 4 + h, dgc, 0.0)
                yield
                sc = G_HEAD_DIM ** -0.5
                rq, rk, qh, kh = q["rq"], q["rk"], q["qh"], q["kh"]
                dqh = sc * (dqn * rq - qh * (rq * rq * rq) * jnp.sum(dqn * qh, 1, keepdims=True))
                dkh = dkn * rk - kh * (rk * rk * rk) * jnp.sum(dkn * kh, 1, keepdims=True)
                dqkv[rs, h * 128:(h + 1) * 128] = dqh
                dqkv[rs, 512 + h * 128:512 + (h + 1) * 128] = dkh
                dqkv[rs, 1024 + h * 128:1024 + (h + 1) * 128] = dvb * bcol
            return blk

        blks = _lockstep([local(dc, h) for dc, h in items])
        for dc in range(nu):
            dbg[dc * 128:(dc + 1) * 128, :] = functools.reduce(
                lambda a, b: a + b, [blks[dc * G_HEADS + h] for h in range(G_HEADS)])
        ri = lax.broadcasted_iota(jnp.int32, (T, T), 0)
        cj = lax.broadcasted_iota(jnp.int32, (T, T), 1)
        utri = jnp.where((ri <= cj) & ((ri >> 6) == (cj >> 6)), 1.0, 0.0).astype(BF16)
        dbgv = dbg[...]
        dgd = _dot_exact_lhs(_nn, utri, dbgv)
        is_g = (lane >= 4) & (lane < 8)
        dga = jnp.where(is_g, dgd * nea * _sigmoid(sarg), 0.0)
        dg_ref[...] = jnp.where(lane < 4, dbgv, dga).astype(BF16)
        dpv_ref[0:1, :] += jnp.sum(jnp.where(is_g, dgd * gdec, 0.0), 0, keepdims=True)
        dpv_ref[1:2, :] += jnp.sum(dga, 0, keepdims=True)
        dc_ = dqkv[...] * _dsilu(c)
        for k in range(CONV_WIDTH):
            dcw_ref[k:k + 1, :] += jnp.sum(dc_ * taps[k], 0, keepdims=True)
        ext2 = jnp.concatenate([dc_, dhalo[...]], 0)
        tt2 = _conv_taps_t(ext2, T)
        dx_ref[...] = sum(cw_ref[k:k + 1, :] * tt2[k] for k in range(CONV_WIDTH)).astype(BF16)
        dhalo[...] = dc_[0:8]

    def rev(i):
        return nt - 1 - i

    def prev8(i):
        return jnp.maximum(rev(i) * t8 - 1, 0)

    return _pcall(
        body, name=name, grid=(nt,),
        in_specs=[pl.BlockSpec((T, GW3), lambda i: (rev(i), OFF_GQKV // GW3)),
                  pl.BlockSpec((8, GW3), lambda i: (prev8(i), OFF_GQKV // GW3)),
                  pl.BlockSpec((T, 512), lambda i: (rev(i), OFF_GZ // 512)),
                  pl.BlockSpec((T, 128), lambda i: (rev(i), OFF_GBA // 128)),
                  pl.BlockSpec((2 * nu, G_HEADS, 128, 128), lambda i: (rev(i), 0, 0, 0)),
                  pl.BlockSpec((nu, G_HEADS, 128, 128), lambda i: (rev(i), 0, 0, 0)),
                  pl.BlockSpec((T, 512), lambda i: (rev(i), MIX_G // 512))] + _gdn_param_specs(),
        out_specs=[pl.BlockSpec((T, GW3), lambda i: (rev(i), 0)),
                   pl.BlockSpec((T, 512), lambda i: (rev(i), 0)),
                   pl.BlockSpec((T, 128), lambda i: (rev(i), 0)),
                   pl.BlockSpec((CONV_WIDTH, GW3), lambda i: (0, 0)),
                   pl.BlockSpec((8, 128), lambda i: (0, 0)),
                   pl.BlockSpec((1, 128), lambda i: (0, 0))],
        out_shape=[jax.ShapeDtypeStruct((S_, GW3), BF16), jax.ShapeDtypeStruct((S_, 512), BF16),
                   jax.ShapeDtypeStruct((S_, 128), BF16), jax.ShapeDtypeStruct((CONV_WIDTH, GW3), F32),
                   jax.ShapeDtypeStruct((8, 128), F32), jax.ShapeDtypeStruct((1, 128), F32)],
        scratch_shapes=[pltpu.VMEM((G_HEADS, 128, 128), F32), pltpu.VMEM((8, GW3), F32),
                        pltpu.VMEM((T, GW3), F32), pltpu.VMEM((T, 128), F32)],
        compiler_params=_cp(("arbitrary",)),
    )(proj, proj, proj, proj, states, tms, dymix, cw, _gdn_pvec(a_log, dt_bias), nw.reshape(1, 128))


def _add_mine_windows(a0, a1, b, nsh, width, *, out_dtype, name):
    R_, C = b.shape
    nb = width // 128
    assert (3 * nsh) // 128 + nb <= C // 128

    def body(a0_ref, a1_ref, b_ref, o_ref):
        mine = jnp.where(lax.axis_index("c") == 0, a0_ref[...], a1_ref[...])
        o_ref[...] = (mine + b_ref[...]).astype(o_ref.dtype)

    spec = pl.BlockSpec((R_, 128), lambda t, j: (0, (nsh * t) // 128 + j))
    return _pcall(body, name=name, grid=(4, nb), in_specs=[spec] * 3,
                  out_specs=pl.BlockSpec((None, R_, 128), lambda t, j: (t, 0, j)),
                  out_shape=jax.ShapeDtypeStruct((4, R_, width), out_dtype),
                  compiler_params=_cp(("parallel", "parallel")))(a0, a1, b)


def _add_mine(a0, a1, b, *, out_dtype, tr, name):
    R_, C = b.shape

    def body(a0_ref, a1_ref, b_ref, o_ref):
        mine = jnp.where(lax.axis_index("c") == 0, a0_ref[...], a1_ref[...])
        o_ref[...] = (mine + b_ref[...]).astype(o_ref.dtype)

    spec = pl.BlockSpec((tr, C), lambda i: (i, 0))
    return _pcall(body, name=name, grid=(R_ // tr,), in_specs=[spec] * 3, out_specs=spec,
                  out_shape=jax.ShapeDtypeStruct((R_, C), out_dtype), compiler_params=_cp(("parallel",)))(a0, a1, b)


def _sum4(a, mine, *, tr, name):
    _, R_, C = a.shape

    def body(a_ref, m_ref, o_ref):
        s = 2 * lax.axis_index("x") + lax.axis_index("y")
        mv = m_ref[...].astype(F32)
        p = [jnp.where(s == t, mv, a_ref[t].astype(F32)) for t in range(4)]
        o_ref[...] = ((p[0] + p[1]) + p[2]) + p[3]

    return _pcall(body, name=name, grid=(R_ // tr,),
                  in_specs=[pl.BlockSpec((4, tr, C), lambda i: (0, i, 0)), pl.BlockSpec((tr, C), lambda i: (i, 0))],
                  out_specs=pl.BlockSpec((tr, C), lambda i: (i, 0)),
                  out_shape=jax.ShapeDtypeStruct((R_, C), F32), compiler_params=_cp(("parallel",)))(a, mine)


def _adamw(w, g, m, v, *, tr, name):
    L, R_, C = w.shape
    c1 = 1.0 / (1.0 - ADAM_B1 ** ADAM_STEP)
    c2 = 1.0 / (1.0 - ADAM_B2 ** ADAM_STEP)

    def body(w_ref, g_ref, m_ref, v_ref, d_ref, mo_ref, vo_ref):
        gg = g_ref[...]
        mn = ADAM_B1 * m_ref[...] + (1.0 - ADAM_B1) * gg
        vn = ADAM_B2 * v_ref[...] + (1.0 - ADAM_B2) * (gg * gg)
        mo_ref[...] = mn
        vo_ref[...] = vn
        d_ref[...] = -ADAM_LR * ((mn * c1) / (jnp.sqrt(vn * c2) + ADAM_EPS) + ADAM_WD * w_ref[...])

    spec = pl.BlockSpec((None, tr, C), lambda l, i: (l, i, 0))
    shp = jax.ShapeDtypeStruct((L, R_, C), F32)
    return _pcall(body, name=name, grid=(L, R_ // tr), in_specs=[spec] * 4, out_specs=[spec] * 3,
                  out_shape=[shp] * 3, compiler_params=_cp(("parallel", "parallel")))(w, g, m, v)


HBM_SPEC = pl.BlockSpec(memory_space=pltpu.HBM)


def _place():
    x, y, c = lax.axis_index("x"), lax.axis_index("y"), lax.axis_index("c")
    chips = [(1 - x, y), (x, 1 - y), (1 - x, 1 - y)]
    return x, y, c, 2 * x + y, chips, [2 * cx + cy for cx, cy in chips], (x, y, 1 - c)


def _remote(src, dst, ssem, rsem, dev):
    return pltpu.make_async_remote_copy(src_ref=src, dst_ref=dst, send_sem=ssem, recv_sem=rsem,
                                        device_id=dev, device_id_type=MESH)


def _gather_weights(win, wout, conv):
    def body(win_ref, wout_ref, cv_ref, gin_ref, gout_ref, gcv_ref, ssem, rsem):
        x, y, c, s, chips, sid, sib = _place()

        def in_half(slot, hc):
            return gin_ref.at[slot, hc]

        def out_half(slot, hc):
            return gout_ref.at[slot, hc]

        sends = []
        for j, chip in enumerate(chips):
            dev = (*chip, c)
            sends.append(_remote(win_ref.at[c], in_half(s, c), ssem.at[j], rsem.at[j], dev))
            sends.append(_remote(wout_ref.at[c], out_half(s, c), ssem.at[3 + j], rsem.at[3 + j], dev))
            sends.append(_remote(cv_ref, gcv_ref.at[s], ssem.at[6 + j], rsem.at[6 + j], dev))
        for cp in sends:
            cp.start()
        for j in range(3):
            _remote(in_half(sid[j], c), in_half(sid[j], c), ssem.at[j], rsem.at[j], sib).wait_recv()
            f = _remote(in_half(sid[j], c), in_half(sid[j], c), ssem.at[9 + j], rsem.at[9 + j], sib)
            f.start()
            sends.append(f)
            _remote(out_half(sid[j], c), out_half(sid[j], c), ssem.at[3 + j], rsem.at[3 + j], sib).wait_recv()
            f = _remote(out_half(sid[j], c), out_half(sid[j], c), ssem.at[12 + j], rsem.at[12 + j], sib)
            f.start()
            sends.append(f)
        for j in range(3):
            _remote(in_half(sid[j], 1 - c), in_half(sid[j], 1 - c), ssem.at[9 + j], rsem.at[9 + j], sib).wait_recv()
            _remote(out_half(sid[j], 1 - c), out_half(sid[j], 1 - c), ssem.at[12 + j], rsem.at[12 + j], sib).wait_recv()
            _remote(gcv_ref.at[sid[j]], gcv_ref.at[sid[j]], ssem.at[6 + j], rsem.at[6 + j], sib).wait_recv()
        for cp in sends:
            cp.wait_send()

    return _pcall(
        body, name="gather_weights",
        in_specs=[HBM_SPEC] * 3, out_specs=[HBM_SPEC] * 3,
        out_shape=[jax.ShapeDtypeStruct((4,) + win.shape, win.dtype),
                   jax.ShapeDtypeStruct((4,) + wout.shape, wout.dtype),
                   jax.ShapeDtypeStruct((4,) + conv.shape, conv.dtype)],
        scratch_shapes=[pltpu.SemaphoreType.DMA((15,)), pltpu.SemaphoreType.DMA((15,))],
    )(win, wout, conv)


def _swap_sibling(pairs, stacked, name):
    n2, ns = len(pairs), len(stacked)
    n = n2 + ns

    def body(*refs):
        src2 = [(refs[2 * k], refs[2 * k + 1]) for k in range(n2)]
        srcs = refs[2 * n2:2 * n2 + ns]
        dst = refs[2 * n2 + ns:2 * n2 + ns + n]
        ssem, rsem = refs[-2], refs[-1]
        x, y, c, s, chips, sid, sib = _place()

        def exchange(give):
            cps = [_remote(src2[k][give], dst[k], ssem.at[k], rsem.at[k], sib) for k in range(n2)]
            for cp in cps:
                cp.start()
            for cp in cps:
                cp.wait()

        small = [_remote(srcs[k].at[1 - c], dst[n2 + k], ssem.at[n2 + k], rsem.at[n2 + k], sib) for k in range(ns)]
        for cp in small:
            cp.start()

        @pl.when(c == 0)
        def _():
            exchange(1)

        @pl.when(c == 1)
        def _():
            exchange(0)

        for cp in small:
            cp.wait()

    flat = [a for p in pairs for a in p] + list(stacked)
    return _pcall(
        body, name=name, in_specs=[HBM_SPEC] * len(flat), out_specs=[HBM_SPEC] * n,
        out_shape=[jax.ShapeDtypeStruct(p[0].shape, p[0].dtype) for p in pairs]
        + [jax.ShapeDtypeStruct(a.shape[1:], a.dtype) for a in stacked],
        scratch_shapes=[pltpu.SemaphoreType.DMA((n,)), pltpu.SemaphoreType.DMA((n,))],
    )(*flat)


def _scatter_chips(arrs, per_target, name):
    n = len(arrs)

    def body(*refs):
        src, dst = refs[:n], refs[n:2 * n]
        ssem, rsem = refs[2 * n], refs[2 * n + 1]
        x, y, c, s, chips, sid, sib = _place()
        sends = []
        for k in range(n):
            for j, chip in enumerate(chips):
                piece = src[k].at[sid[j]] if per_target[k] else src[k]
                sends.append(_remote(piece, dst[k].at[s], ssem.at[3 * k + j], rsem.at[3 * k + j], (*chip, c)))
        for cp in sends:
            cp.start()
        for k in range(n):
            for j in range(3):
                _remote(dst[k].at[sid[j]], dst[k].at[sid[j]], ssem.at[3 * k + j], rsem.at[3 * k + j], sib).wait_recv()
        for cp in sends:
            cp.wait_send()

    outs = [jax.ShapeDtypeStruct(a.shape if pt else (4,) + a.shape, a.dtype) for a, pt in zip(arrs, per_target)]
    return _pcall(
        body, name=name, in_specs=[HBM_SPEC] * n, out_specs=[HBM_SPEC] * n, out_shape=outs,
        scratch_shapes=[pltpu.SemaphoreType.DMA((3 * n,)), pltpu.SemaphoreType.DMA((3 * n,))],
    )(*arrs)


def _swap_whole(arrs, name):
    n = len(arrs)

    def body(*refs):
        src, dst, ssem, rsem = refs[:n], refs[n:2 * n], refs[2 * n], refs[2 * n + 1]
        *_, sib = _place()
        cps = [_remote(src[k], dst[k], ssem.at[k], rsem.at[k], sib) for k in range(n)]
        for cp in cps:
            cp.start()
        for cp in cps:
            cp.wait()

    return _pcall(
        body, name=name, in_specs=[HBM_SPEC] * n, out_specs=[HBM_SPEC] * n,
        out_shape=[jax.ShapeDtypeStruct(a.shape, a.dtype) for a in arrs],
        scratch_shapes=[pltpu.SemaphoreType.DMA((n,)), pltpu.SemaphoreType.DMA((n,))],
    )(*arrs)


def _perm_cols(w):
    parts = [w[..., int(_ORIG_OFF[oi]):int(_ORIG_OFF[oi]) + IN_SIZES[oi]] for oi, _ in _PIECES]
    parts.append(jnp.zeros(w.shape[:-1] + (NP - N_IN,), w.dtype))
    return jnp.concatenate(parts, -1)


def _perm_rows(w):
    return jnp.concatenate([w[..., 512:1536, :], w[..., 0:512, :], w[..., 1536:2048, :]], -2)


_SMALL = ("sinks", "r_conv_b", "r_wa", "r_ba", "r_wx", "r_bx", "r_lam", "g_a_log", "g_dt_bias", "g_norm_w",
          "ln_g", "ln_b", "r_conv_w", "g_conv_w")
_PACK_ROWS = 16


def _piece_rows(n):
    return -(-n // (128 * _PACK_ROWS)) * _PACK_ROWS


def _pack(arrs):
    parts = []
    for a in arrs:
        n = int(np.prod(a.shape))
        rows = _piece_rows(n)
        if n % 128 == 0:
            blk = a.reshape(n // 128, 128)
        else:
            blk = jnp.pad(a.reshape(1, n), ((0, 0), (0, (-n) % 128))).reshape(-1, 128)
        if blk.shape[0] < rows:
            blk = jnp.pad(blk, ((0, rows - blk.shape[0]), (0, 0)))
        parts.append(blk)
    return jnp.concatenate(parts, 0)


def _unpack(packed, shapes):
    out = []
    r = 0
    for shp in shapes:
        n = int(np.prod(shp))
        if n % 128 == 0:
            out.append(packed[r:r + n // 128].reshape(shp))
        else:
            nr = -(-n // 128)
            out.append(packed[r:r + nr].reshape(1, nr * 128)[:, :n].reshape(shp))
        r += _piece_rows(n)
    return out


def _tile(n, t):
    return min(n, t)


def _layer_fwd(l, x, xb, wb, wob, rope_c, rope_s, p):
    S_ = x.shape[0]
    proj = _matmul(xb, wb, ta=False, tb=False, tm=_tile(S_, 1024), tn=512, tk=wb.shape[0], out_dtype=F32,
                   name=f"in_proj_{l}")
    ya = _attn_fwd(proj, rope_c, rope_s, p["sinks"], T=_tile(S_, 512), name=f"attn_fwd_{l}")
    h, yr = _rglru_fwd(proj, p["r_conv_w"], p["r_conv_b"], p["r_wa"], p["r_ba"], p["r_wx"], p["r_bx"], p["r_lam"],
                       T=_tile(S_, 256), name=f"rglru_fwd_{l}")
    yg, st, tms = _gdn_fwd(proj, p["g_conv_w"], p["g_a_log"], p["g_dt_bias"], p["g_norm_w"],
                           T=_tile(S_, 256), name=f"gdn_fwd_{l}")
    ymix = jnp.concatenate([yr, ya, yg], 1)
    z = _outproj(ymix, wob, x, tm=_tile(S_, 256), name=f"out_proj_{l}")
    return dict(proj=proj, h=h, st=st, tms=tms, ymix=ymix, z=z)


def _layer_bwd(l, sv, x_b, dz, dzb, wo, wob, rope_c, rope_s, p):
    S_, D = dz.shape
    proj = sv["proj"]
    dymix = _matmul(dzb, wob, ta=False, tb=True, tm=_tile(S_, 1024), tn=512, tk=D, out_dtype=F32,
                    name=f"dmix_{l}")
    dwo = _matmul(sv["ymix"], dzb, ta=True, tb=False, tm=512, tn=_tile(D, 1024), tk=_tile(S_, 1024),
                  out_dtype=F32, name=f"dw_out_{l}",
                  out_blocks=((MIX_WIDTH, D), (512, _tile(D, 1024)),
                              lambda i, j: (jnp.where(i == 3, 3, (i + 1) % 3), j)))
    dq, daz, dk, dv, dkt, dvt, dsk = _attn_bwd(proj, rope_c, rope_s, p["sinks"], dymix, T=_tile(S_, 512),
                                               name=f"attn_bwd_{l}")
    (drx, drz, dcw_r, dcb_r, dwa, dba, dwx, dbx, dlam) = _rglru_bwd(
        proj, sv["h"], dymix, p["r_conv_w"], p["r_conv_b"], p["r_wa"], p["r_ba"], p["r_wx"], p["r_bx"], p["r_lam"],
        T=_tile(S_, 256), name=f"rglru_bwd_{l}")
    dqkv, dgz, dgba, dcw_g, dpv, dnw = _gdn_bwd(proj, sv["st"], sv["tms"], dymix, p["g_conv_w"], p["g_a_log"],
                                                p["g_dt_bias"], p["g_norm_w"], T=_tile(S_, 256), name=f"gdn_bwd_{l}")
    dproj = jnp.concatenate([dq, jnp.concatenate([dk[128:], dkt], 0), jnp.concatenate([dv[128:], dvt], 0), daz,
                             drx, drz, dqkv, dgz, dgba, jnp.zeros((S_, NP - N_IN - 120), BF16)], 1)
    dx = _matmul(dproj, wo, ta=False, tb=True, tm=_tile(S_, 512), tn=_tile(D, 1024), tk=NP // 2, out_dtype=F32,
                 name=f"dx_{l}", extra=dz, alpha=DEEPNORM_ALPHA)
    dwin = _matmul(x_b, dproj, ta=True, tb=False, tm=_tile(D, 1024), tn=512, tk=_tile(S_, 1024), out_dtype=F32,
                   name=f"dw_in_{l}")
    small = dict(sinks=dsk[:, 0], r_conv_b=dcb_r[0], r_wa=dwa, r_ba=dba[0], r_wx=dwx, r_bx=dbx[0], r_lam=dlam[0],
                 g_a_log=dpv[0, 4:8], g_dt_bias=dpv[1, 4:8], g_norm_w=dnw[0], r_conv_w=dcw_r, g_conv_w=dcw_g)
    return dx, dwin, dwo, small


def kernel(x, w_in, sinks, r_conv_w, r_conv_b, r_wa, r_ba, r_wx, r_bx, r_lam, g_conv_w, g_a_log, g_dt_bias, g_norm_w, w_out, ln_g, ln_b, loss_target, m_w_in, m_sinks, m_r_conv_w, m_r_conv_b, m_r_wa, m_r_ba, m_r_wx, m_r_bx, m_r_lam, m_g_conv_w, m_g_a_log, m_g_dt_bias, m_g_norm_w, m_w_out, m_ln_g, m_ln_b, v_w_in, v_sinks, v_r_conv_w, v_r_conv_b, v_r_wa, v_r_ba, v_r_wx, v_r_bx, v_r_lam, v_g_conv_w, v_g_a_log, v_g_dt_bias, v_g_norm_w, v_w_out, v_ln_g, v_ln_b):
    S_, D = x.shape[1], x.shape[2]
    nsh = w_in.shape[2]
    rsh = w_out.shape[1]
    cx, cy, cc = lax.axis_index("x"), lax.axis_index("y"), lax.axis_index("c")
    chip = 2 * cx + cy
    rcw_n, gcw_n = r_conv_w.shape[2], g_conv_w.shape[2]

    conv_pack = jnp.concatenate([r_conv_w, g_conv_w], 2)
    w_in_b, w_out_b = w_in.astype(BF16), w_out.astype(BF16)
    g_in, g_out, g_conv = _gather_weights(w_in_b, w_out_b, conv_pack)

    def shards(own, got, sel):
        return [jnp.where(chip == t, sel(own), sel(got[t])) for t in range(4)]

    wb, wo, wob = [], [], []
    for l in range(DEPTH):
        w_full = jnp.concatenate(shards(w_in_b, g_in, lambda a: a[l]), 1)
        wb.append(_perm_cols(w_full))
        wo.append(jnp.pad(w_full, ((0, 0), (0, NP - N_IN))))
        wob.append(_perm_rows(jnp.concatenate(shards(w_out_b, g_out, lambda a: a[l]), 0)))
    rcw = jnp.concatenate(shards(conv_pack, g_conv, lambda a: a[:, :, :rcw_n]), 2)
    gcw = jnp.concatenate(shards(conv_pack, g_conv, lambda a: a[:, :, rcw_n:]), 2)

    pos = jnp.arange(S_, dtype=F32)[:, None]
    inv = 1.0 / (ROPE_THETA ** (jnp.arange(0, A_HEAD_DIM, 2, dtype=F32) / A_HEAD_DIM))
    ang = pos * inv[None, :]
    cos, sin = jnp.cos(ang), jnp.sin(ang)
    rope_c = jnp.concatenate([cos, cos, cos, cos], 1)
    rope_s = jnp.concatenate([-sin, sin, -sin, sin], 1)

    def params(l):
        return dict(sinks=sinks[l], r_conv_w=rcw[l], r_conv_b=r_conv_b[l], r_wa=r_wa[l], r_ba=r_ba[l],
                    r_wx=r_wx[l], r_bx=r_bx[l], r_lam=r_lam[l], g_conv_w=gcw[l], g_a_log=g_a_log[l],
                    g_dt_bias=g_dt_bias[l], g_norm_w=g_norm_w[l])

    xs, xbs, saved = [x[0]], [x[0].astype(BF16)], []
    for l in range(DEPTH):
        sv = _layer_fwd(l, xs[l], xbs[l], wb[l], wob[l], rope_c, rope_s, params(l))
        saved.append(sv)
        if l + 1 < DEPTH:
            xn, xnb = _ln_fwd(sv["z"], ln_g[l], ln_b[l], tm=_tile(S_, 256), name=f"ln_fwd_{l}")
            xs.append(xn)
            xbs.append(xnb)

    tm_ln = _tile(S_, 256)
    dz, dzb, dg_l, db_l, loss_part = _ln_bwd(saved[-1]["z"], ln_g[-1], ln_b[-1], loss_target[0], from_target=True,
                                             tm=tm_ln, name=f"ln_bwd_{DEPTH - 1}")
    dwin, dwo, small = [None] * DEPTH, [None] * DEPTH, [None] * DEPTH
    dlng, dlnb = [None] * DEPTH, [None] * DEPTH
    for l in reversed(range(DEPTH)):
        dlng[l], dlnb[l] = dg_l[0], db_l[0]
        dx, dwin[l], dwo[l], small[l] = _layer_bwd(l, saved[l], xbs[l], dz, dzb, wo[l], wob[l], rope_c, rope_s,
                                                   params(l))
        if l > 0:
            dz, dzb, dg_l, db_l, _ = _ln_bwd(saved[l - 1]["z"], ln_g[l - 1], ln_b[l - 1], dx, from_target=False,
                                             tm=tm_ln, name=f"ln_bwd_{l - 1}")
    grad_x = dx[None]
    loss = lax.psum(loss_part[0, 0], ("x", "y", "c"))

    sm = {k: jnp.stack([small[l][k] for l in range(DEPTH)]) for k in small[0]}
    sm["ln_g"], sm["ln_b"] = jnp.stack(dlng), jnp.stack(dlnb)
    names = list(_SMALL)
    gs = _pack([sm[n] for n in names])
    gs2 = gs.reshape(2, gs.shape[0] // 2, 128)
    in_got, out_got, s_got = _swap_sibling([(dwin[0], dwin[1]), (dwo[0], dwo[1])], [gs2], "reduce_pair")
    wcov = (-(-nsh // 128) + 1) * 128
    in_cp = _add_mine_windows(dwin[0], dwin[1], in_got, nsh, wcov, out_dtype=BF16, name="pair_sum_w_in")
    out_cp = _add_mine(dwo[0], dwo[1], out_got, out_dtype=BF16, tr=256, name="pair_sum_w_out").reshape(4, rsh, D)
    s_cp = _add_mine(gs2[0], gs2[1], s_got, out_dtype=F32, tr=gs2.shape[1], name="pair_sum_small")
    in_all, out_all, s_all = _scatter_chips([in_cp, out_cp, s_cp], [True, True, False], "reduce_chips")

    def own(a):
        return lax.dynamic_index_in_dim(a, chip, 0, keepdims=False)

    in_sum = _sum4(in_all, own(in_cp), tr=256, name="chip_sum_w_in")
    out_sum = _sum4(out_all, own(out_cp), tr=256, name="chip_sum_w_out")
    s_sum = _sum4(s_all, s_cp, tr=s_cp.shape[0], name="chip_sum_small")
    in_oth, out_oth, s_oth = _swap_whole([in_sum, out_sum, s_sum], "reduce_join")

    def both(mine, other):
        return jnp.where(cc == 0, jnp.stack([mine, other]), jnp.stack([other, mine]))

    g_w_in = lax.dynamic_slice_in_dim(both(in_sum, in_oth), (nsh * chip) % 128, nsh, 2)
    g_w_out = both(out_sum, out_oth)
    g_small = both(s_sum, s_oth).reshape(gs.shape)

    gsm = dict(zip(names, _unpack(g_small, [sm[n].shape for n in names])))
    gsm["r_conv_w"] = lax.dynamic_slice_in_dim(gsm["r_conv_w"], chip * rcw_n, rcw_n, 2)
    gsm["g_conv_w"] = lax.dynamic_slice_in_dim(gsm["g_conv_w"], chip * gcw_n, gcw_n, 2)
    wts = dict(sinks=sinks, r_conv_w=r_conv_w, r_conv_b=r_conv_b, r_wa=r_wa, r_ba=r_ba, r_wx=r_wx, r_bx=r_bx,
               r_lam=r_lam, g_conv_w=g_conv_w, g_a_log=g_a_log, g_dt_bias=g_dt_bias, g_norm_w=g_norm_w,
               ln_g=ln_g, ln_b=ln_b)
    mom = dict(sinks=m_sinks, r_conv_w=m_r_conv_w, r_conv_b=m_r_conv_b, r_wa=m_r_wa, r_ba=m_r_ba, r_wx=m_r_wx,
               r_bx=m_r_bx, r_lam=m_r_lam, g_conv_w=m_g_conv_w, g_a_log=m_g_a_log, g_dt_bias=m_g_dt_bias,
               g_norm_w=m_g_norm_w, ln_g=m_ln_g, ln_b=m_ln_b)
    vel = dict(sinks=v_sinks, r_conv_w=v_r_conv_w, r_conv_b=v_r_conv_b, r_wa=v_r_wa, r_ba=v_r_ba, r_wx=v_r_wx,
               r_bx=v_r_bx, r_lam=v_r_lam, g_conv_w=v_g_conv_w, g_a_log=v_g_a_log, g_dt_bias=v_g_dt_bias,
               g_norm_w=v_g_norm_w, ln_g=v_ln_g, ln_b=v_ln_b)
    pk = [_pack([d[n] for n in names]) for d in (wts, gsm, mom, vel)]
    sshapes = [wts[n].shape for n in names]
    d_s, m_s, v_s = _adamw(*[a[None] for a in pk], tr=pk[0].shape[0], name="adamw_small")
    d_sm, m_sm, v_sm = (dict(zip(names, _unpack(a[0], sshapes))) for a in (d_s, m_s, v_s))
    d_in, m_in, v_in = _adamw(w_in, g_w_in, m_w_in, v_w_in, tr=256, name="adamw_w_in")
    d_out, m_out, v_out = _adamw(w_out, g_w_out, m_w_out, v_w_out, tr=256, name="adamw_w_out")

    order = ["w_in", "sinks", "r_conv_w", "r_conv_b", "r_wa", "r_ba", "r_wx", "r_bx", "r_lam", "g_conv_w",
             "g_a_log", "g_dt_bias", "g_norm_w", "w_out", "ln_g", "ln_b"]
    grads = dict(gsm, w_in=g_w_in, w_out=g_w_out)
    deltas = dict(d_sm, w_in=d_in, w_out=d_out)
    new_m = dict(m_sm, w_in=m_in, w_out=m_out)
    new_v = dict(v_sm, w_in=v_in, w_out=v_out)
    return (loss, grad_x, *[grads[n] for n in order], *[deltas[n] for n in order],
            *[new_m[n] for n in order], *[new_v[n] for n in order])
```

```python
import functools
import math

import jax
import jax.numpy as jnp
import numpy as np
from jax import lax
from jax.experimental import pallas as pl
from jax.experimental.pallas import tpu as pltpu

F32 = jnp.float32
BF16 = jnp.bfloat16
MESH = pl.DeviceIdType.MESH

DEPTH = 2
A_HEADS, A_KV_HEADS, A_HEAD_DIM = 8, 2, 64
A_WIDTH, A_KV_WIDTH = 512, 128
WINDOW = 128
ROPE_THETA = 10000.0
R_WIDTH, R_BLOCKS, R_BLOCK_DIM, R_C = 1024, 8, 128, 8.0
CONV_WIDTH = 4
G_HEADS, G_HEAD_DIM, G_WIDTH, G_CHUNK = 4, 128, 512, 64
MIX_WIDTH = 2048
IN_SIZES = (512, 128, 128, 512, 1024, 1024, 512, 512, 512, 512, 4, 4)
N_IN = 5384
DEEPNORM_ALPHA = (2 * DEPTH) ** 0.25
LN_EPS = 1e-5
RMS_EPS = 1e-6
ADAM_LR, ADAM_B1, ADAM_B2, ADAM_EPS, ADAM_WD, ADAM_STEP = 0.001, 0.9, 0.999, 1e-08, 0.01, 10

NP = 5632
OFF_RX, OFF_RZ, OFF_AQ, OFF_AZ, OFF_GQKV, OFF_GZ, OFF_AK, OFF_AV, OFF_GBA = (
    0, 1024, 2048, 2560, 3072, 4608, 5120, 5248, 5376)
_ORIG_OFF = np.concatenate([[0], np.cumsum(IN_SIZES)])[:-1]
_PIECES = ((4, OFF_RX), (5, OFF_RZ), (0, OFF_AQ), (3, OFF_AZ), (6, OFF_GQKV), (7, OFF_GQKV + 512),
           (8, OFF_GQKV + 1024), (9, OFF_GZ), (1, OFF_AK), (2, OFF_AV), (10, OFF_GBA), (11, OFF_GBA + 4))
MIX_R, MIX_A, MIX_G = 0, 1024, 1536
VMEM_LIMIT = 56 * 1024 * 1024


def _pcall(body, **kw):
    return pl.pallas_call(body, **kw)


def _cp(sem, limit=VMEM_LIMIT):
    return pltpu.CompilerParams(dimension_semantics=sem, vmem_limit_bytes=limit)


def _sigmoid(x):
    return 0.5 + 0.5 * jnp.tanh(0.5 * x)


def _silu(x):
    return x * _sigmoid(x)


def _dsilu(x):
    s = _sigmoid(x)
    return s * (1.0 + x * (1.0 - s))


def _log1p(x):
    u = 1.0 + x
    d = jnp.where(u == 1.0, 1.0, u - 1.0)
    return jnp.where(u == 1.0, x, jnp.log(u) * (x / d))


def _softplus(x):
    return jnp.maximum(x, 0.0) + _log1p(jnp.exp(-jnp.abs(x)))


def _one_minus_exp(x):
    series = -x * (1.0 + x * (0.5 + x * (1.0 / 6.0 + x * (1.0 / 24.0))))
    return jnp.where(x > -0.05, series, 1.0 - jnp.exp(x))


def _nn(a, b):
    return lax.dot_general(a, b, (((1,), (0,)), ((), ())), preferred_element_type=F32)


def _nt(a, b):
    return lax.dot_general(a, b, (((1,), (1,)), ((), ())), preferred_element_type=F32)


def _tn(a, b):
    return lax.dot_general(a, b, (((0,), (0,)), ((), ())), preferred_element_type=F32)


def _b(x):
    return x.astype(BF16)


def _split3(x):
    hi = x.astype(BF16)
    r1 = x - hi.astype(F32)
    mid = r1.astype(BF16)
    lo = (r1 - mid.astype(F32)).astype(BF16)
    return hi, mid, lo


def _dot3(f, a, b):
    ah, am, _ = _split3(a)
    bh, bm, _ = _split3(b)
    return f(ah, bh) + (f(ah, bm) + f(am, bh))


def _dot_exact_lhs(f, a_bf16, b):
    bh, bm, bl = _split3(b)
    return f(a_bf16, bh) + (f(a_bf16, bm) + f(a_bf16, bl))


def _rot(x):
    w = x.shape[-1]
    lane = lax.broadcasted_iota(jnp.int32, (1, w), 1)
    return jnp.where((lane & 63) < 32, pltpu.roll(x, w - 32, 1), pltpu.roll(x, 32, 1))


def _conv_taps(ext, n):
    return [pltpu.roll(ext, 3 - k, 0)[8:8 + n] if k < 3 else ext[8:8 + n] for k in range(CONV_WIDTH)]


def _conv_taps_t(ext, n):
    m = ext.shape[0]
    return [pltpu.roll(ext, m - (3 - k), 0)[0:n] if k < 3 else ext[0:n] for k in range(CONV_WIDTH)]


def _scan_lin(a, b, reverse):
    n = a.shape[0]
    row = lax.broadcasted_iota(jnp.int32, (n, 1), 0)
    s = 1
    while s < n:
        if reverse:
            a_sh = pltpu.roll(a, n - s, 0)
            b_sh = pltpu.roll(b, n - s, 0)
            ok = row < (n - s)
        else:
            a_sh = pltpu.roll(a, s, 0)
            b_sh = pltpu.roll(b, s, 0)
            ok = row >= s
        b = jnp.where(ok, a * b_sh + b, b)
        a = jnp.where(ok, a * a_sh, a)
        s *= 2
    return a, b


def _matmul(a, b, *, ta, tb, tm, tn, tk, out_dtype, name, extra=None, alpha=0.0, out_blocks=None):
    if ta:
        K, M = a.shape
    else:
        M, K = a.shape
    if tb:
        N, K2 = b.shape
    else:
        K2, N = b.shape
    assert K == K2 and M % tm == 0 and N % tn == 0 and K % tk == 0, (a.shape, b.shape, tm, tn, tk)
    nk = K // tk
    ca = 0 if ta else 1
    cb = 1 if tb else 0
    has_extra = extra is not None

    assert nk == 1 or out_dtype == F32

    def body(*refs):
        if has_extra:
            a_ref, b_ref, e_ref, o_ref = refs
        else:
            a_ref, b_ref, o_ref = refs
            e_ref = None
        k = pl.program_id(2)
        part = lax.dot_general(a_ref[...], b_ref[...], (((ca,), (cb,)), ((), ())), preferred_element_type=F32)
        if nk == 1:
            if e_ref is not None:
                part = part + alpha * e_ref[...]
            o_ref[...] = part.astype(o_ref.dtype)
            return

        @pl.when(k == 0)
        def _():
            o_ref[...] = part

        @pl.when((k > 0) & (k < nk - 1))
        def _():
            o_ref[...] += part

        @pl.when(k == nk - 1)
        def _():
            last = o_ref[...] + part
            if e_ref is not None:
                last = last + alpha * e_ref[...]
            o_ref[...] = last

    a_spec = (pl.BlockSpec((tk, tm), lambda i, j, k: (k, i)) if ta
              else pl.BlockSpec((tm, tk), lambda i, j, k: (i, k)))
    b_spec = (pl.BlockSpec((tn, tk), lambda i, j, k: (j, k)) if tb
              else pl.BlockSpec((tk, tn), lambda i, j, k: (k, j)))
    e_spec = pl.BlockSpec((tm, tn), lambda i, j, k: (i, j))
    if out_blocks is None:
        o_spec, o_shape = e_spec, (M, N)
    else:
        o_shape, o_block, o_map = out_blocks
        o_spec = pl.BlockSpec(o_block, lambda i, j, k: o_map(i, j))
    in_specs = [a_spec, b_spec] + ([e_spec] if has_extra else [])
    args = (a, b) + ((extra,) if has_extra else ())
    return _pcall(
        body, name=name, grid=(M // tm, N // tn, nk),
        in_specs=in_specs, out_specs=o_spec,
        out_shape=jax.ShapeDtypeStruct(o_shape, out_dtype),
        compiler_params=_cp(("parallel", "parallel", "arbitrary")),
    )(*args)


def _outproj(ymix, wo, x, *, tm, name):
    S_, D = x.shape

    def body(y_ref, w_ref, x_ref, z_ref):
        z_ref[...] = DEEPNORM_ALPHA * x_ref[...] + _nn(y_ref[...], w_ref[...])

    return _pcall(
        body, name=name, grid=(S_ // tm,),
        in_specs=[pl.BlockSpec((tm, MIX_WIDTH), lambda i: (i, 0)),
                  pl.BlockSpec((MIX_WIDTH, D), lambda i: (0, 0)),
                  pl.BlockSpec((tm, D), lambda i: (i, 0))],
        out_specs=pl.BlockSpec((tm, D), lambda i: (i, 0)),
        out_shape=jax.ShapeDtypeStruct((S_, D), F32),
        compiler_params=_cp(("parallel",)),
    )(ymix, wo, x)


def _ln_stats(z):
    mu = jnp.mean(z, -1, keepdims=True)
    zc = z - mu
    var = jnp.mean(zc * zc, -1, keepdims=True)
    rstd = lax.rsqrt(var + LN_EPS)
    return zc * rstd, rstd


def _ln_fwd(z, g, b, *, tm, name):
    S_, D = z.shape

    def body(z_ref, g_ref, b_ref, y_ref, yb_ref):
        xh, _ = _ln_stats(z_ref[...])
        y = xh * g_ref[...] + b_ref[...]
        y_ref[...] = y
        yb_ref[...] = y.astype(BF16)

    row = pl.BlockSpec((tm, D), lambda i: (i, 0))
    vec = pl.BlockSpec((1, D), lambda i: (0, 0))
    return _pcall(
        body, name=name, grid=(S_ // tm,), in_specs=[row, vec, vec], out_specs=[row, row],
        out_shape=[jax.ShapeDtypeStruct((S_, D), F32), jax.ShapeDtypeStruct((S_, D), BF16)],
        compiler_params=_cp(("parallel",)),
    )(z, g.reshape(1, D), b.reshape(1, D))


def _ln_bwd(z, g, b, other, *, from_target, tm, name):
    S_, D = z.shape

    def body(z_ref, g_ref, b_ref, o_ref, dz_ref, dzb_ref, dg_ref, db_ref, loss_ref):
        i = pl.program_id(0)

        @pl.when(i == 0)
        def _():
            dg_ref[...] = jnp.zeros_like(dg_ref)
            db_ref[...] = jnp.zeros_like(db_ref)
            loss_ref[...] = jnp.zeros_like(loss_ref)

        xh, rstd = _ln_stats(z_ref[...])
        gam = g_ref[...]
        if from_target:
            err = xh * gam + b_ref[...] - o_ref[...]
            per_tok = jnp.mean(err * err, -1, keepdims=True)
            loss_ref[...] += 0.5 * jnp.sum(per_tok, 0, keepdims=True)
            dy = err * (1.0 / D)
        else:
            dy = o_ref[...]
        dxh = dy * gam
        m1 = jnp.mean(dxh, -1, keepdims=True)
        m2 = jnp.mean(dxh * xh, -1, keepdims=True)
        dz = rstd * (dxh - m1 - xh * m2)
        dz_ref[...] = dz
        dzb_ref[...] = dz.astype(BF16)
        dg_ref[...] += jnp.sum(dy * xh, 0, keepdims=True)
        db_ref[...] += jnp.sum(dy, 0, keepdims=True)

    row = pl.BlockSpec((tm, D), lambda i: (i, 0))
    vec = pl.BlockSpec((1, D), lambda i: (0, 0))
    one = pl.BlockSpec((1, 1), lambda i: (0, 0))
    return _pcall(
        body, name=name, grid=(S_ // tm,), in_specs=[row, vec, vec, row],
        out_specs=[row, row, vec, vec, one],
        out_shape=[jax.ShapeDtypeStruct((S_, D), F32), jax.ShapeDtypeStruct((S_, D), BF16),
                   jax.ShapeDtypeStruct((1, D), F32), jax.ShapeDtypeStruct((1, D), F32),
                   jax.ShapeDtypeStruct((1, 1), F32)],
        compiler_params=_cp(("arbitrary",)),
    )(z, g.reshape(1, D), b.reshape(1, D), other)


def _attn_masks(i, sk_ref):
    ri = lax.broadcasted_iota(jnp.int32, (512, 256), 0)
    cj = lax.broadcasted_iota(jnp.int32, (512, 256), 1)
    diff = (ri & 127) - cj + 128
    band = (diff >= 0) & (diff < WINDOW)
    bias = jnp.where(band, 0.0, -jnp.inf)
    bias0 = jnp.where(band & ((i > 0) | (cj >= 128)), 0.0, -jnp.inf)
    grp = lax.broadcasted_iota(jnp.int32, (512, 1), 0) >> 7
    skvs = []
    for h in range(A_KV_HEADS):
        skv = jnp.zeros((512, 1), F32)
        for g in range(4):
            skv = jnp.where(grp == g, sk_ref[h * 4 + g], skv)
        skvs.append(skv)
    return bias0, bias, skvs


def _attn_common(masks, b, h, qr, kd, vd):
    lane = lax.broadcasted_iota(jnp.int32, (1, 128), 1)
    lof = (lane < 64).astype(F32)
    hif = 1.0 - lof
    r0 = b * 128
    skv = masks[2][h]
    pairs = [qr[r0:r0 + 128, h * 256 + p * 128:h * 256 + (p + 1) * 128] for p in (0, 1)]
    qs = _b(jnp.concatenate([pairs[0] * lof, pairs[0] * hif, pairs[1] * lof, pairs[1] * hif], 0))
    k2 = kd[h][r0:r0 + 256]
    v2 = vd[h][r0:r0 + 256]
    s = _nt(qs, k2) * (A_HEAD_DIM ** -0.5) + (masks[0] if b == 0 else masks[1])
    m = jnp.maximum(jnp.max(s, 1, keepdims=True), skv)
    p = jnp.exp(s - m)
    esk = jnp.exp(skv - m)
    rz = 1.0 / (jnp.sum(p, 1, keepdims=True) + esk)
    prob = p * rz
    o4 = _nn(_b(prob), v2)
    return lof, hif, qs, k2, v2, prob, esk * rz, o4


def _attn_prep(T, q_ref, k_ref, v_ref, c_ref, s_ref, kprev, vprev):
    C = c_ref[...]
    Sg = s_ref[...]
    C4 = jnp.concatenate([C] * 4, 1)
    S4 = jnp.concatenate([Sg] * 4, 1)
    q = q_ref[...]
    qr = q * C4 + _rot(q) * S4
    k = k_ref[...]
    kr = k * C + _rot(k) * Sg
    v = v_ref[...]
    kext = jnp.concatenate([kprev[...], kr], 0)
    vext = jnp.concatenate([vprev[...], v], 0)
    kprev[...] = kr[T - 128:]
    vprev[...] = v[T - 128:]
    lo = lax.broadcasted_iota(jnp.int32, (1, 128), 1) < 64
    kroll = pltpu.roll(kext, 64, 1)
    vroll = pltpu.roll(vext, 64, 1)
    kd = [_b(jnp.where(lo, kext, kroll)), _b(jnp.where(lo, kroll, kext))]
    vd = [_b(jnp.where(lo, vext, vroll)), _b(jnp.where(lo, vroll, vext))]
    return C, Sg, C4, S4, qr, kd, vd


def _attn_specs(T):
    return [pl.BlockSpec(memory_space=pltpu.SMEM),
            pl.BlockSpec((T, 512), lambda i: (i, OFF_AQ // 512)),
            pl.BlockSpec((T, 512), lambda i: (i, OFF_AZ // 512)),
            pl.BlockSpec((T, 128), lambda i: (i, OFF_AK // 128)),
            pl.BlockSpec((T, 128), lambda i: (i, OFF_AV // 128)),
            pl.BlockSpec((T, 128), lambda i: (i, 0)),
            pl.BlockSpec((T, 128), lambda i: (i, 0))]


def _attn_fwd(proj, rope_c, rope_s, sinks, *, T, name):
    S_ = proj.shape[0]
    nb = T // 128

    def body(sk_ref, q_ref, z_ref, k_ref, v_ref, c_ref, s_ref, y_ref, kprev, vprev):
        i = pl.program_id(0)

        @pl.when(i == 0)
        def _():
            kprev[...] = jnp.zeros_like(kprev)
            vprev[...] = jnp.zeros_like(vprev)

        _, _, _, _, qr, kd, vd = _attn_prep(T, q_ref, k_ref, v_ref, c_ref, s_ref, kprev, vprev)
        masks = _attn_masks(i, sk_ref)
        for b in range(nb):
            r0 = b * 128
            for h in range(2):
                lof, hif, _, _, _, _, _, o4 = _attn_common(masks, b, h, qr, kd, vd)
                for p in range(2):
                    cs = slice(h * 256 + p * 128, h * 256 + (p + 1) * 128)
                    o = o4[2 * p * 128:(2 * p + 1) * 128] * lof + o4[(2 * p + 1) * 128:(2 * p + 2) * 128] * hif
                    y_ref[r0:r0 + 128, cs] = (o * _silu(z_ref[r0:r0 + 128, cs])).astype(BF16)

    return _pcall(
        body, name=name, grid=(S_ // T,), in_specs=_attn_specs(T),
        out_specs=pl.BlockSpec((T, 512), lambda i: (i, 0)),
        out_shape=jax.ShapeDtypeStruct((S_, 512), BF16),
        scratch_shapes=[pltpu.VMEM((128, 128), F32), pltpu.VMEM((128, 128), F32)],
        compiler_params=_cp(("arbitrary",)),
    )(sinks, proj, proj, proj, proj, rope_c, rope_s)


def _attn_bwd(proj, rope_c, rope_s, sinks, dymix, *, T, name):
    S_ = proj.shape[0]
    nb = T // 128
    nt = S_ // T

    def body(sk_ref, q_ref, z_ref, k_ref, v_ref, c_ref, s_ref, dy_ref,
             dq_ref, dz_ref, dk_ref, dv_ref, dkt_ref, dvt_ref, dsk_ref,
             kprev, vprev, cprev, sprev, dkacc, dvacc, dqacc):
        i = pl.program_id(0)

        @pl.when(i == 0)
        def _():
            kprev[...] = jnp.zeros_like(kprev)
            vprev[...] = jnp.zeros_like(vprev)
            cprev[...] = jnp.zeros_like(cprev)
            sprev[...] = jnp.zeros_like(sprev)
            dkacc[...] = jnp.zeros_like(dkacc)
            dvacc[...] = jnp.zeros_like(dvacc)
            dsk_ref[...] = jnp.zeros_like(dsk_ref)

        @pl.when(i > 0)
        def _():
            dkacc[0:128, :] = dkacc[T:T + 128, :]
            dvacc[0:128, :] = dvacc[T:T + 128, :]
            dkacc[128:, :] = jnp.zeros((T, 128), F32)
            dvacc[128:, :] = jnp.zeros((T, 128), F32)

        C, Sg, C4, S4, qr, kd, vd = _attn_prep(T, q_ref, k_ref, v_ref, c_ref, s_ref, kprev, vprev)
        masks = _attn_masks(i, sk_ref)
        lane = lax.broadcasted_iota(jnp.int32, (1, 128), 1)
        for b in range(nb):
            r0 = b * 128
            for h in range(2):
                lof, hif, qs, k2, v2, prob, psink, o4 = _attn_common(masks, b, h, qr, kd, vd)
                dos = []
                for p in range(2):
                    cs = slice(h * 256 + p * 128, h * 256 + (p + 1) * 128)
                    o = o4[2 * p * 128:(2 * p + 1) * 128] * lof + o4[(2 * p + 1) * 128:(2 * p + 2) * 128] * hif
                    zc = z_ref[r0:r0 + 128, cs]
                    dyc = dy_ref[r0:r0 + 128, cs]
                    dz_ref[r0:r0 + 128, cs] = (dyc * o * _dsilu(zc)).astype(BF16)
                    do = dyc * _silu(zc)
                    dos += [do * lof, do * hif]
                dos = jnp.concatenate(dos, 0)
                os_ = jnp.concatenate([o4[0:128] * lof, o4[128:256] * hif, o4[256:384] * lof, o4[384:512] * hif], 0)
                delta = jnp.sum(dos * os_, 1, keepdims=True)
                dosb = _b(dos)
                dp = _nt(dosb, v2)
                ds = prob * (dp - delta)
                dsv = -psink * delta
                for g in range(4):
                    sg = jnp.sum(dsv[g * 128:(g + 1) * 128], 0, keepdims=True)
                    hd = h * 4 + g
                    dsk_ref[hd:hd + 1, :] += jnp.broadcast_to(sg, (1, 128))
                dsb = _b(ds * (A_HEAD_DIM ** -0.5))
                dqs = _nn(dsb, k2)
                for p in range(2):
                    cs = slice(h * 256 + p * 128, h * 256 + (p + 1) * 128)
                    dqacc[r0:r0 + 128, cs] = (dqs[2 * p * 128:(2 * p + 1) * 128] * lof
                                              + dqs[(2 * p + 1) * 128:(2 * p + 2) * 128] * hif)
                dkdup = _tn(dsb, qs)
                dvdup = _tn(_b(prob), dosb)
                half = (lane < 64) if h == 0 else (lane >= 64)
                dkacc[r0:r0 + 256, :] += jnp.where(half, dkdup + pltpu.roll(dkdup, 64, 1), 0.0)
                dvacc[r0:r0 + 256, :] += jnp.where(half, dvdup + pltpu.roll(dvdup, 64, 1), 0.0)
        dqr = dqacc[...]
        dq_ref[...] = (dqr * C4 + _rot(dqr * S4)).astype(BF16)
        cext = jnp.concatenate([cprev[...], C], 0)
        sext = jnp.concatenate([sprev[...], Sg], 0)
        dke = dkacc[...]
        dkp = dke * cext + _rot(dke * sext)
        dk_ref[...] = dkp[0:T].astype(BF16)
        dkt_ref[...] = dkp[T:T + 128].astype(BF16)
        dve = dvacc[...]
        dv_ref[...] = dve[0:T].astype(BF16)
        dvt_ref[...] = dve[T:T + 128].astype(BF16)
        cprev[...] = C[T - 128:]
        sprev[...] = Sg[T - 128:]

    wide = pl.BlockSpec((T, 512), lambda i: (i, 0))
    nar = pl.BlockSpec((T, 128), lambda i: (i, 0))
    tail = pl.BlockSpec((128, 128), lambda i: (0, 0))
    return _pcall(
        body, name=name, grid=(nt,),
        in_specs=_attn_specs(T) + [pl.BlockSpec((T, 512), lambda i: (i, MIX_A // 512))],
        out_specs=[wide, wide, nar, nar, tail, tail, pl.BlockSpec((8, 128), lambda i: (0, 0))],
        out_shape=[jax.ShapeDtypeStruct((S_, 512), BF16), jax.ShapeDtypeStruct((S_, 512), BF16),
                   jax.ShapeDtypeStruct((S_, 128), BF16), jax.ShapeDtypeStruct((S_, 128), BF16),
                   jax.ShapeDtypeStruct((128, 128), BF16), jax.ShapeDtypeStruct((128, 128), BF16),
                   jax.ShapeDtypeStruct((8, 128), F32)],
        scratch_shapes=[pltpu.VMEM((128, 128), F32)] * 4
        + [pltpu.VMEM((T + 128, 128), F32), pltpu.VMEM((T + 128, 128), F32), pltpu.VMEM((T, 512), F32)],
        compiler_params=_cp(("arbitrary",)),
    )(sinks, proj, proj, proj, proj, rope_c, rope_s, dymix)


def _rg_gates(xr, wa_ref, ba_ref, wx_ref, bx_ref, lam_ref):
    xb = _b(xr)
    pre_a = jnp.concatenate([_nn(xb[:, n * 128:(n + 1) * 128], wa_ref[n]) for n in range(R_BLOCKS)], 1) + ba_ref[...]
    pre_x = jnp.concatenate([_nn(xb[:, n * 128:(n + 1) * 128], wx_ref[n]) for n in range(R_BLOCKS)], 1) + bx_ref[...]
    r = _sigmoid(pre_a)
    ig = _sigmoid(pre_x)
    sp = _softplus(-lam_ref[...])
    log_a = -R_C * r * sp
    a = jnp.exp(log_a)
    mult = jnp.sqrt(_one_minus_exp(2.0 * log_a))
    return xb, r, ig, sp, a, mult


def _rg_param_specs():
    C = R_WIDTH
    vec = pl.BlockSpec((1, C), lambda i: (0, 0))
    blk = pl.BlockSpec((R_BLOCKS, 128, 128), lambda i: (0, 0, 0))
    return [pl.BlockSpec((CONV_WIDTH, C), lambda i: (0, 0)), vec, blk, vec, blk, vec, vec]


def _rglru_fwd(proj, cw, cb, wa, ba, wx, bx, lam, *, T, name):
    S_ = proj.shape[0]
    C = R_WIDTH

    def body(rx_ref, rz_ref, cw_ref, cb_ref, wa_ref, ba_ref, wx_ref, bx_ref, lam_ref,
             h_ref, y_ref, halo, hcar):
        i = pl.program_id(0)

        @pl.when(i == 0)
        def _():
            halo[...] = jnp.zeros_like(halo)
            hcar[...] = jnp.zeros_like(hcar)

        rx = rx_ref[...]
        ext = jnp.concatenate([halo[...], rx], 0)
        halo[...] = rx[T - 8:]
        taps = _conv_taps(ext, T)
        xr = cb_ref[...] + sum(cw_ref[k:k + 1, :] * taps[k] for k in range(CONV_WIDTH))
        _, _, ig, _, a, mult = _rg_gates(xr, wa_ref, ba_ref, wx_ref, bx_ref, lam_ref)
        u = mult * (ig * xr)
        acum, hloc = _scan_lin(a, u, False)
        h = hloc + acum * hcar[0:1, :]
        hcar[...] = jnp.broadcast_to(h[T - 1:T, :], (8, C))
        h_ref[...] = h
        y_ref[...] = (h * _silu(rz_ref[...])).astype(BF16)

    row = pl.BlockSpec((T, C), lambda i: (i, 0))
    return _pcall(
        body, name=name, grid=(S_ // T,),
        in_specs=[pl.BlockSpec((T, C), lambda i: (i, OFF_RX // C)),
                  pl.BlockSpec((T, C), lambda i: (i, OFF_RZ // C))] + _rg_param_specs(),
        out_specs=[row, row],
        out_shape=[jax.ShapeDtypeStruct((S_, C), F32), jax.ShapeDtypeStruct((S_, C), BF16)],
        scratch_shapes=[pltpu.VMEM((8, C), F32), pltpu.VMEM((8, C), F32)],
        compiler_params=_cp(("arbitrary",)),
    )(proj, proj, cw, cb.reshape(1, C), _b(wa), ba.reshape(1, C), _b(wx), bx.reshape(1, C), lam.reshape(1, C))


def _rglru_bwd(proj, h, dymix, cw, cb, wa, ba, wx, bx, lam, *, T, name):
    S_ = proj.shape[0]
    C = R_WIDTH
    nt = S_ // T
    t8 = T // 8

    def body(rx_ref, rxp_ref, rz_ref, h_ref, hp_ref, dy_ref,
             cw_ref, cb_ref, wa_ref, ba_ref, wx_ref, bx_ref, lam_ref, wat_ref, wxt_ref,
             drx_ref, drz_ref, dcw_ref, dcb_ref, dwa_ref, dba_ref, dwx_ref, dbx_ref, dlam_ref,
             afirst, gfirst, dhalo):
        i = pl.program_id(0)
        first_tile = (i == nt - 1)

        @pl.when(i == 0)
        def _():
            afirst[...] = jnp.zeros_like(afirst)
            gfirst[...] = jnp.zeros_like(gfirst)
            dhalo[...] = jnp.zeros_like(dhalo)
            for r in (dcw_ref, dcb_ref, dwa_ref, dba_ref, dwx_ref, dbx_ref, dlam_ref):
                r[...] = jnp.zeros_like(r)

        keep = jnp.where(first_tile, 0.0, 1.0)
        rx = rx_ref[...]
        ext = jnp.concatenate([rxp_ref[...] * keep, rx], 0)
        taps = _conv_taps(ext, T)
        xr = cb_ref[...] + sum(cw_ref[k:k + 1, :] * taps[k] for k in range(CONV_WIDTH))
        xb, r, ig, sp, a, mult = _rg_gates(xr, wa_ref, ba_ref, wx_ref, bx_ref, lam_ref)
        hh = h_ref[...]
        rz = rz_ref[...]
        dy = dy_ref[...]
        drz_ref[...] = (dy * hh * _dsilu(rz)).astype(BF16)
        dh = dy * _silu(rz)
        row = lax.broadcasted_iota(jnp.int32, (T, 1), 0)
        c = jnp.where(row == T - 1, afirst[0:1, :], pltpu.roll(a, T - 1, 0))
        ccum, gloc = _scan_lin(c, dh, True)
        g = gloc + ccum * gfirst[0:1, :]
        afirst[...] = jnp.broadcast_to(a[0:1, :], (8, C))
        gfirst[...] = jnp.broadcast_to(g[0:1, :], (8, C))
        hprev = jnp.where(row == 0, hp_ref[7:8, :] * keep, pltpu.roll(hh, 1, 0))
        da = g * hprev
        gx = ig * xr
        dgx = g * mult
        dmult = g * gx
        dlog_a = da * a - dmult * (a * a) * lax.rsqrt(mult * mult)
        dpre_a = dlog_a * (-R_C * sp) * r * (1.0 - r)
        dpre_x = dgx * xr * ig * (1.0 - ig)
        dlam_ref[...] += jnp.sum(dlog_a * (-R_C * r), 0, keepdims=True) * (-_sigmoid(-lam_ref[...]))
        dab = _b(dpre_a)
        dxb = _b(dpre_x)
        dxr = dgx * ig + jnp.concatenate(
            [_nn(dab[:, n * 128:(n + 1) * 128], wat_ref[n]) + _nn(dxb[:, n * 128:(n + 1) * 128], wxt_ref[n])
             for n in range(R_BLOCKS)], 1)
        for n in range(R_BLOCKS):
            cs = slice(n * 128, (n + 1) * 128)
            dwa_ref[n] += _tn(xb[:, cs], dab[:, cs])
            dwx_ref[n] += _tn(xb[:, cs], dxb[:, cs])
        dba_ref[...] += jnp.sum(dpre_a, 0, keepdims=True)
        dbx_ref[...] += jnp.sum(dpre_x, 0, keepdims=True)
        dcb_ref[...] += jnp.sum(dxr, 0, keepdims=True)
        for k in range(CONV_WIDTH):
            dcw_ref[k:k + 1, :] += jnp.sum(dxr * taps[k], 0, keepdims=True)
        ext2 = jnp.concatenate([dxr, dhalo[...]], 0)
        tt = _conv_taps_t(ext2, T)
        drx_ref[...] = sum(cw_ref[k:k + 1, :] * tt[k] for k in range(CONV_WIDTH)).astype(BF16)
        dhalo[...] = dxr[0:8]

    def rev(i):
        return nt - 1 - i

    def prev8(i):
        return jnp.maximum(rev(i) * t8 - 1, 0)

    vec = pl.BlockSpec((1, C), lambda i: (0, 0))
    blk = pl.BlockSpec((R_BLOCKS, 128, 128), lambda i: (0, 0, 0))
    row = pl.BlockSpec((T, C), lambda i: (rev(i), 0))
    wat = _b(jnp.swapaxes(wa, 1, 2))
    wxt = _b(jnp.swapaxes(wx, 1, 2))
    return _pcall(
        body, name=name, grid=(nt,),
        in_specs=[pl.BlockSpec((T, C), lambda i: (rev(i), OFF_RX // C)),
                  pl.BlockSpec((8, C), lambda i: (prev8(i), OFF_RX // C)),
                  pl.BlockSpec((T, C), lambda i: (rev(i), OFF_RZ // C)),
                  row,
                  pl.BlockSpec((8, C), lambda i: (prev8(i), 0)),
                  pl.BlockSpec((T, C), lambda i: (rev(i), MIX_R // C)),
                  ] + _rg_param_specs() + [blk, blk],
        out_specs=[row, row, pl.BlockSpec((CONV_WIDTH, C), lambda i: (0, 0)), vec, blk, vec, blk, vec, vec],
        out_shape=[jax.ShapeDtypeStruct((S_, C), BF16), jax.ShapeDtypeStruct((S_, C), BF16),
                   jax.ShapeDtypeStruct((CONV_WIDTH, C), F32), jax.ShapeDtypeStruct((1, C), F32),
                   jax.ShapeDtypeStruct((R_BLOCKS, 128, 128), F32), jax.ShapeDtypeStruct((1, C), F32),
                   jax.ShapeDtypeStruct((R_BLOCKS, 128, 128), F32), jax.ShapeDtypeStruct((1, C), F32),
                   jax.ShapeDtypeStruct((1, C), F32)],
        scratch_shapes=[pltpu.VMEM((8, C), F32)] * 3,
        compiler_params=_cp(("arbitrary",)),
    )(proj, proj, proj, h, h, dymix, cw, cb.reshape(1, C), _b(wa), ba.reshape(1, C), _b(wx), bx.reshape(1, C),
      lam.reshape(1, C), wat, wxt)


GW3 = 3 * G_WIDTH


def _lane_col(x, lane_idx):
    lane = lax.broadcasted_iota(jnp.int32, (1, x.shape[1]), 1)
    return jnp.sum(jnp.where(lane == lane_idx, x, 0.0), 1, keepdims=True)


def _gdn_pre(ext, T, cw_ref, gba, pv_ref):
    taps = _conv_taps(ext, T)
    c = sum(cw_ref[k:k + 1, :] * taps[k] for k in range(CONV_WIDTH))
    qkv = _silu(c)
    beta = _sigmoid(gba)
    sarg = gba + pv_ref[1:2, :]
    nea = -jnp.exp(pv_ref[0:1, :])
    gdec = nea * _softplus(sarg)
    ri = lax.broadcasted_iota(jnp.int32, (T, T), 0)
    cj = lax.broadcasted_iota(jnp.int32, (T, T), 1)
    same = (ri >> 6) == (cj >> 6)
    ltri = jnp.where((ri >= cj) & same, 1.0, 0.0).astype(BF16)
    gc = _dot_exact_lhs(_nn, ltri, gdec)
    return taps, c, qkv, beta, sarg, nea, gdec, gc


def _gdn_masks():
    ri = lax.broadcasted_iota(jnp.int32, (128, 128), 0)
    cj = lax.broadcasted_iota(jnp.int32, (128, 128), 1)
    same = (ri >> 6) == (cj >> 6)
    return (ri >= cj) & same, (ri > cj) & same, ri == cj


def _lockstep(gens):
    out = [None] * len(gens)
    live = list(range(len(gens)))
    while live:
        still = []
        for k in live:
            try:
                next(gens[k])
                still.append(k)
            except StopIteration as stop:
                out[k] = stop.value
        live = still
    return out


def _gdn_chunk(qkv, beta, gc, rs, h, tm=None):
    tril, strict, eye = _gdn_masks()
    rowi = lax.broadcasted_iota(jnp.int32, (128, 1), 0)
    lane = lax.broadcasted_iota(jnp.int32, (1, 128), 1)
    qh = qkv[rs, h * 128:(h + 1) * 128]
    kh = qkv[rs, 512 + h * 128:512 + (h + 1) * 128]
    vh = qkv[rs, 1024 + h * 128:1024 + (h + 1) * 128]
    rq = lax.rsqrt(jnp.sum(qh * qh, 1, keepdims=True) + RMS_EPS)
    rk = lax.rsqrt(jnp.sum(kh * kh, 1, keepdims=True) + RMS_EPS)
    qn = qh * (rq * (G_HEAD_DIM ** -0.5))
    kn = kh * rk
    gcb = gc[rs]
    gcol = _lane_col(gcb, 4 + h)
    bcol = _lane_col(beta[rs], h)
    grow = _dot_exact_lhs(_nt, jnp.ones((128, 128), BF16), jnp.where(lane == 4 + h, gcb, 0.0))
    D = jnp.where(tril, jnp.exp(jnp.minimum(gcol - grow, 0.0)), 0.0)
    kb = kn * bcol
    vb = vh * bcol
    knb = _b(kn)
    A = _nt(_b(kb), knb)
    Bm = _nt(_b(qn), knb)
    yield
    if tm is None:
        N = jnp.where(strict, -(A * D), 0.0)
        tm = jnp.where(eye, 1.0, 0.0) + N
        npow = N
        for _ in range(5):
            npow = _dot3(_nn, npow, npow)
            yield
            tm = tm + _dot3(_nn, tm, npow)
            yield
    eg = jnp.exp(gcol)
    u = _dot3(_nn, tm, vb)
    w = _dot3(_nn, tm, kb * eg)
    yield
    qk = jnp.where(tril, Bm * D, 0.0)
    qd = qn * eg
    gla = jnp.sum(jnp.where(rowi == 63, gcol, 0.0), 0, keepdims=True)
    glb = jnp.sum(jnp.where(rowi == 127, gcol, 0.0), 0, keepdims=True)
    ed = jnp.exp(jnp.where(rowi < 64, gla, glb) - gcol)
    kd = kn * ed
    return dict(qh=qh, kh=kh, vh=vh, rq=rq, rk=rk, qn=qn, kn=kn, gcol=gcol, bcol=bcol, D=D, A=A, Bm=Bm,
                tm=tm, eg=eg, ed=ed, u=u, w=w, qk=qk, qd=qd, kd=kd, kb=kb, vb=vb,
                gla=jnp.exp(gla), glb=jnp.exp(glb))


def _gdn_scan(q, sa):
    sab = _b(sa)
    wb = _b(q["w"])
    vna = q["u"] - _nn(wb, sab)
    yield
    sb = sa * q["gla"] + _tn(_b(q["kd"][0:64]), _b(vna[0:64]))
    yield
    sbb = _b(sb)
    vnb = q["u"] - _nn(wb, sbb)
    yield
    sn = sb * q["glb"] + _tn(_b(q["kd"][64:128]), _b(vnb[64:128]))
    yield
    vn = jnp.concatenate([vna[0:64], vnb[64:128]], 0)
    qdb = _b(q["qd"])
    o = jnp.concatenate([_nn(qdb[0:64], sab), _nn(qdb[64:128], sbb)], 0) + _nn(_b(q["qk"]), _b(vn))
    return sb, sn, vn, o


def _gdn_param_specs():
    return [pl.BlockSpec((CONV_WIDTH, GW3), lambda i: (0, 0)),
            pl.BlockSpec((8, 128), lambda i: (0, 0)),
            pl.BlockSpec((1, 128), lambda i: (0, 0))]


def _gdn_pvec(a_log, dt_bias):
    z = jnp.zeros((8, 128), F32)
    return z.at[0, 4:8].set(a_log).at[1, 4:8].set(dt_bias)


def _gdn_fwd(proj, cw, a_log, dt_bias, nw, *, T, name):
    S_ = proj.shape[0]
    nu = T // 128

    def body(x_ref, z_ref, g_ref, cw_ref, pv_ref, nw_ref, y_ref, st_ref, tm_ref, halo, state):
        i = pl.program_id(0)

        @pl.when(i == 0)
        def _():
            halo[...] = jnp.zeros_like(halo)
            state[...] = jnp.zeros_like(state)

        x = x_ref[...]
        ext = jnp.concatenate([halo[...], x], 0)
        halo[...] = x[T - 8:]
        _, _, qkv, beta, _, _, _, gc = _gdn_pre(ext, T, cw_ref, g_ref[...], pv_ref)
        items = [(dc, h) for dc in range(nu) for h in range(G_HEADS)]
        qs = _lockstep([_gdn_chunk(qkv, beta, gc, slice(dc * 128, (dc + 1) * 128), h) for dc, h in items])

        def head_chain(h):
            s = state[h]
            for dc in range(nu):
                rs = slice(dc * 128, (dc + 1) * 128)
                q = qs[dc * G_HEADS + h]
                sb, sn, _, o = yield from _gdn_scan(q, s)
                st_ref[2 * dc, h] = s
                st_ref[2 * dc + 1, h] = sb
                tm_ref[dc, h] = q["tm"]
                s = sn
                yield
                rn = lax.rsqrt(jnp.mean(o * o, 1, keepdims=True) + RMS_EPS)
                cs = slice(h * 128, (h + 1) * 128)
                y_ref[rs, cs] = (o * rn * nw_ref[...] * _silu(z_ref[rs, cs])).astype(BF16)
                yield
            state[h] = s

        _lockstep([head_chain(h) for h in range(G_HEADS)])

    return _pcall(
        body, name=name, grid=(S_ // T,),
        in_specs=[pl.BlockSpec((T, GW3), lambda i: (i, OFF_GQKV // GW3)),
                  pl.BlockSpec((T, 512), lambda i: (i, OFF_GZ // 512)),
                  pl.BlockSpec((T, 128), lambda i: (i, OFF_GBA // 128))] + _gdn_param_specs(),
        out_specs=[pl.BlockSpec((T, 512), lambda i: (i, 0)),
                   pl.BlockSpec((2 * nu, G_HEADS, 128, 128), lambda i: (i, 0, 0, 0)),
                   pl.BlockSpec((nu, G_HEADS, 128, 128), lambda i: (i, 0, 0, 0))],
        out_shape=[jax.ShapeDtypeStruct((S_, 512), BF16),
                   jax.ShapeDtypeStruct((S_ // 64, G_HEADS, 128, 128), F32),
                   jax.ShapeDtypeStruct((S_ // 128, G_HEADS, 128, 128), F32)],
        scratch_shapes=[pltpu.VMEM((8, GW3), F32), pltpu.VMEM((G_HEADS, 128, 128), F32)],
        compiler_params=_cp(("arbitrary",)),
    )(proj, proj, proj, cw, _gdn_pvec(a_log, dt_bias), nw.reshape(1, 128))


def _gdn_bwd(proj, states, tms, dymix, cw, a_log, dt_bias, nw, *, T, name):
    S_ = proj.shape[0]
    nt = S_ // T
    nu = T // 128
    t8 = T // 8

    def body(x_ref, xp_ref, z_ref, g_ref, st_ref, tm_ref, dy_ref, cw_ref, pv_ref, nw_ref,
             dx_ref, dz_ref, dg_ref, dcw_ref, dpv_ref, dnw_ref, dstate, dhalo, dqkv, dbg):
        i = pl.program_id(0)
        first_tile = (i == nt - 1)

        @pl.when(i == 0)
        def _():
            dstate[...] = jnp.zeros_like(dstate)
            dhalo[...] = jnp.zeros_like(dhalo)
            dcw_ref[...] = jnp.zeros_like(dcw_ref)
            dpv_ref[...] = jnp.zeros_like(dpv_ref)
            dnw_ref[...] = jnp.zeros_like(dnw_ref)

        keep = jnp.where(first_tile, 0.0, 1.0)
        ext = jnp.concatenate([xp_ref[...] * keep, x_ref[...]], 0)
        G = g_ref[...]
        taps, c, qkv, beta, sarg, nea, gdec, gc = _gdn_pre(ext, T, cw_ref, G, pv_ref)
        tril, strict, _ = _gdn_masks()
        rowi = lax.broadcasted_iota(jnp.int32, (128, 1), 0)
        lane = lax.broadcasted_iota(jnp.int32, (1, 128), 1)
        ones_b = jnp.ones((128, 128), BF16)
        nwv = nw_ref[...]
        items = [(dc, h) for dc in range(nu) for h in range(G_HEADS)]

        def recompute(dc, h):
            q = yield from _gdn_chunk(qkv, beta, gc, slice(dc * 128, (dc + 1) * 128), h, tm=tm_ref[dc, h])
            sa = st_ref[2 * dc, h]
            sb, _, vn, o = yield from _gdn_scan(q, sa)
            return q, sa, sb, vn, o

        fw = _lockstep([recompute(dc, h) for dc, h in items])
        chain_out = {}

        def head_chain(h):
            dS = dstate[h]
            for dc in reversed(range(nu)):
                rs = slice(dc * 128, (dc + 1) * 128)
                q, sa, sb, vn, o = fw[dc * G_HEADS + h]
                cs = slice(h * 128, (h + 1) * 128)
                zg = z_ref[rs, cs]
                dy = dy_ref[rs, cs]
                rn = lax.rsqrt(jnp.mean(o * o, 1, keepdims=True) + RMS_EPS)
                don = dy * _silu(zg)
                dz_ref[rs, cs] = (dy * (o * rn * nwv) * _dsilu(zg)).astype(BF16)
                dnw_ref[...] += jnp.sum(don * o * rn, 0, keepdims=True)
                tt = don * nwv
                do = rn * (tt - o * (rn * rn) * jnp.mean(tt * o, 1, keepdims=True))
                yield
                dob = _b(do)
                sab, sbb = _b(sa), _b(sb)
                vnb16 = _b(vn)
                dqk = jnp.where(tril, _nt(dob, vnb16), 0.0)
                dvn_o = _tn(_b(q["qk"]), dob)
                dSb16 = _b(dS)
                kdb = _b(q["kd"])
                wb = _b(q["w"])
                qdb = _b(q["qd"])
                yield
                dvn_b = dvn_o[64:128] + _nn(kdb[64:128], dSb16)
                dkd_b = _nt(vnb16[64:128], dSb16)
                dgl_b = jnp.sum(jnp.sum(dS * sb, 1, keepdims=True), 0, keepdims=True)
                yield
                dvn_b16 = _b(dvn_b)
                dw_b = -_nt(dvn_b16, sbb)
                dqd_b = _nt(dob[64:128], sbb)
                dSm = q["glb"] * dS + _tn(qdb[64:128], dob[64:128]) - _tn(wb[64:128], dvn_b16)
                yield
                dSm16 = _b(dSm)
                dvn_a = dvn_o[0:64] + _nn(kdb[0:64], dSm16)
                dkd_a = _nt(vnb16[0:64], dSm16)
                dgl_a = jnp.sum(jnp.sum(dSm * sa, 1, keepdims=True), 0, keepdims=True)
                yield
                dvn_a16 = _b(dvn_a)
                dw_a = -_nt(dvn_a16, sab)
                dqd_a = _nt(dob[0:64], sab)
                dS = q["gla"] * dSm + _tn(qdb[0:64], dob[0:64]) - _tn(wb[0:64], dvn_a16)
                chain_out[dc, h] = (dqk, jnp.concatenate([dvn_a, dvn_b], 0), jnp.concatenate([dw_a, dw_b], 0),
                                    jnp.concatenate([dkd_a, dkd_b], 0), jnp.concatenate([dqd_a, dqd_b], 0),
                                    dgl_a, dgl_b)
                yield
            dstate[h] = dS

        _lockstep([head_chain(h) for h in range(G_HEADS)])

        def local(dc, h):
            rs = slice(dc * 128, (dc + 1) * 128)
            q = fw[dc * G_HEADS + h][0]
            dqk, du, dw, dkd, dqd, dgl_a, dgl_b = chain_out[dc, h]
            if True:
                dvb = _dot3(_tn, q["tm"], du)
                dkbe = _dot3(_tn, q["tm"], dw)
                yield
                dM = jnp.where(strict, -(_nt(_b(dvb), _b(q["u"])) + _nt(_b(dkbe), _b(q["w"]))), 0.0)
                yield
                D = q["D"]
                dA = dM * D
                dB = dqk * D
                dDD = (dM * q["A"] + dqk * q["Bm"]) * D
                dh_, dm_, dl_ = _split3(dDD)
                colsum = _tn(dh_, ones_b) + (_tn(dm_, ones_b) + _tn(dl_, ones_b))
                dgc = jnp.sum(dDD, 1, keepdims=True) - _lane_col(colsum, 0)
                yield
                dA16, dB16 = _b(dA), _b(dB)
                knb, kbb, qnb = _b(q["kn"]), _b(q["kb"]), _b(q["qn"])
                eg, ed = q["eg"], q["ed"]
                dkb = _nn(dA16, knb) + dkbe * eg
                dkn = _tn(dA16, kbb) + _tn(dB16, qnb) + dkd * ed + dkb * q["bcol"]
                dqn = _nn(dB16, knb) + dqd * eg
                yield
                deg = jnp.sum(dkbe * q["kb"], 1, keepdims=True) + jnp.sum(dqd * q["qn"], 1, keepdims=True)
                ded = jnp.sum(dkd * q["kn"], 1, keepdims=True) * ed
                dgc = dgc + deg * eg - ded
                tail_a = jnp.sum(jnp.where(rowi < 64, ded, 0.0), 0, keepdims=True) + dgl_a * q["gla"]
                tail_b = jnp.sum(jnp.where(rowi >= 64, ded, 0.0), 0, keepdims=True) + dgl_b * q["glb"]
                dgc = dgc + jnp.where(rowi == 63, tail_a, 0.0) + jnp.where(rowi == 127, tail_b, 0.0)
                dbeta = jnp.sum(dkb * q["kn"], 1, keepdims=True) + jnp.sum(dvb * q["vh"], 1, keepdims=True)
                bcol = q["bcol"]
                blk = jnp.where(lane == h, dbeta * bcol * (1.0 - bcol), 0.0) + jnp.where(lane == 4 + h, dgc, 0.0)
                yield
                sc = G_HEAD_DIM ** -0.5
                rq, rk, qh, kh = q["rq"], q["rk"], q["qh"], q["kh"]
                dqh = sc * (dqn * rq - qh * (rq * rq * rq) * jnp.sum(dqn * qh, 1, keepdims=True))
                dkh = dkn * rk - kh * (rk * rk * rk) * jnp.sum(dkn * kh, 1, keepdims=True)
                dqkv[rs, h * 128:(h + 1) * 128] = dqh
                dqkv[rs, 512 + h * 128:512 + (h + 1) * 128] = dkh
                dqkv[rs, 1024 + h * 128:1024 + (h + 1) * 128] = dvb * bcol
            return blk

        blks = _lockstep([local(dc, h) for dc, h in items])
        for dc in range(nu):
            dbg[dc * 128:(dc + 1) * 128, :] = functools.reduce(
                lambda a, b: a + b, [blks[dc * G_HEADS + h] for h in range(G_HEADS)])
        ri = lax.broadcasted_iota(jnp.int32, (T, T), 0)
        cj = lax.broadcasted_iota(jnp.int32, (T, T), 1)
        utri = jnp.where((ri <= cj) & ((ri >> 6) == (cj >> 6)), 1.0, 0.0).astype(BF16)
        dbgv = dbg[...]
        dgd = _dot_exact_lhs(_nn, utri, dbgv)
        is_g = (lane >= 4) & (lane < 8)
        dga = jnp.where(is_g, dgd * nea * _sigmoid(sarg), 0.0)
        dg_ref[...] = jnp.where(lane < 4, dbgv, dga).astype(BF16)
        dpv_ref[0:1, :] += jnp.sum(jnp.where(is_g, dgd * gdec, 0.0), 0, keepdims=True)
        dpv_ref[1:2, :] += jnp.sum(dga, 0, keepdims=True)
        dc_ = dqkv[...] * _dsilu(c)
        for k in range(CONV_WIDTH):
            dcw_ref[k:k + 1, :] += jnp.sum(dc_ * taps[k], 0, keepdims=True)
        ext2 = jnp.concatenate([dc_, dhalo[...]], 0)
        tt2 = _conv_taps_t(ext2, T)
        dx_ref[...] = sum(cw_ref[k:k + 1, :] * tt2[k] for k in range(CONV_WIDTH)).astype(BF16)
        dhalo[...] = dc_[0:8]

    def rev(i):
        return nt - 1 - i

    def prev8(i):
        return jnp.maximum(rev(i) * t8 - 1, 0)

    return _pcall(
        body, name=name, grid=(nt,),
        in_specs=[pl.BlockSpec((T, GW3), lambda i: (rev(i), OFF_GQKV // GW3)),
                  pl.BlockSpec((8, GW3), lambda i: (prev8(i), OFF_GQKV // GW3)),
                  pl.BlockSpec((T, 512), lambda i: (rev(i), OFF_GZ // 512)),
                  pl.BlockSpec((T, 128), lambda i: (rev(i), OFF_GBA // 128)),
                  pl.BlockSpec((2 * nu, G_HEADS, 128, 128), lambda i: (rev(i), 0, 0, 0)),
                  pl.BlockSpec((nu, G_HEADS, 128, 128), lambda i: (rev(i), 0, 0, 0)),
                  pl.BlockSpec((T, 512), lambda i: (rev(i), MIX_G // 512))] + _gdn_param_specs(),
        out_specs=[pl.BlockSpec((T, GW3), lambda i: (rev(i), 0)),
                   pl.BlockSpec((T, 512), lambda i: (rev(i), 0)),
                   pl.BlockSpec((T, 128), lambda i: (rev(i), 0)),
                   pl.BlockSpec((CONV_WIDTH, GW3), lambda i: (0, 0)),
                   pl.BlockSpec((8, 128), lambda i: (0, 0)),
                   pl.BlockSpec((1, 128), lambda i: (0, 0))],
        out_shape=[jax.ShapeDtypeStruct((S_, GW3), BF16), jax.ShapeDtypeStruct((S_, 512), BF16),
                   jax.ShapeDtypeStruct((S_, 128), BF16), jax.ShapeDtypeStruct((CONV_WIDTH, GW3), F32),
                   jax.ShapeDtypeStruct((8, 128), F32), jax.ShapeDtypeStruct((1, 128), F32)],
        scratch_shapes=[pltpu.VMEM((G_HEADS, 128, 128), F32), pltpu.VMEM((8, GW3), F32),
                        pltpu.VMEM((T, GW3), F32), pltpu.VMEM((T, 128), F32)],
        compiler_params=_cp(("arbitrary",)),
    )(proj, proj, proj, proj, states, tms, dymix, cw, _gdn_pvec(a_log, dt_bias), nw.reshape(1, 128))


def _add_mine_windows(a0, a1, b, nsh, width, *, out_dtype, name):
    R_, C = b.shape
    nb = width // 128
    assert (3 * nsh) // 128 + nb <= C // 128

    def body(a0_ref, a1_ref, b_ref, o_ref):
        mine = jnp.where(lax.axis_index("c") == 0, a0_ref[...], a1_ref[...])
        o_ref[...] = (mine + b_ref[...]).astype(o_ref.dtype)

    spec = pl.BlockSpec((R_, 128), lambda t, j: (0, (nsh * t) // 128 + j))
    return _pcall(body, name=name, grid=(4, nb), in_specs=[spec] * 3,
                  out_specs=pl.BlockSpec((None, R_, 128), lambda t, j: (t, 0, j)),
                  out_shape=jax.ShapeDtypeStruct((4, R_, width), out_dtype),
                  compiler_params=_cp(("parallel", "parallel")))(a0, a1, b)


def _add_mine(a0, a1, b, *, out_dtype, tr, name):
    R_, C = b.shape

    def body(a0_ref, a1_ref, b_ref, o_ref):
        mine = jnp.where(lax.axis_index("c") == 0, a0_ref[...], a1_ref[...])
        o_ref[...] = (mine + b_ref[...]).astype(o_ref.dtype)

    spec = pl.BlockSpec((tr, C), lambda i: (i, 0))
    return _pcall(body, name=name, grid=(R_ // tr,), in_specs=[spec] * 3, out_specs=spec,
                  out_shape=jax.ShapeDtypeStruct((R_, C), out_dtype), compiler_params=_cp(("parallel",)))(a0, a1, b)


def _sum4(a, mine, *, tr, name):
    _, R_, C = a.shape

    def body(a_ref, m_ref, o_ref):
        s = 2 * lax.axis_index("x") + lax.axis_index("y")
        mv = m_ref[...].astype(F32)
        p = [jnp.where(s == t, mv, a_ref[t].astype(F32)) for t in range(4)]
        o_ref[...] = ((p[0] + p[1]) + p[2]) + p[3]

    return _pcall(body, name=name, grid=(R_ // tr,),
                  in_specs=[pl.BlockSpec((4, tr, C), lambda i: (0, i, 0)), pl.BlockSpec((tr, C), lambda i: (i, 0))],
                  out_specs=pl.BlockSpec((tr, C), lambda i: (i, 0)),
                  out_shape=jax.ShapeDtypeStruct((R_, C), F32), compiler_params=_cp(("parallel",)))(a, mine)


def _adamw(w, g, m, v, *, tr, name):
    L, R_, C = w.shape
    c1 = 1.0 / (1.0 - ADAM_B1 ** ADAM_STEP)
    c2 = 1.0 / (1.0 - ADAM_B2 ** ADAM_STEP)

    def body(w_ref, g_ref, m_ref, v_ref, d_ref, mo_ref, vo_ref):
        gg = g_ref[...]
        mn = ADAM_B1 * m_ref[...] + (1.0 - ADAM_B1) * gg
        vn = ADAM_B2 * v_ref[...] + (1.0 - ADAM_B2) * (gg * gg)
        mo_ref[...] = mn
        vo_ref[...] = vn
        d_ref[...] = -ADAM_LR * ((mn * c1) / (jnp.sqrt(vn * c2) + ADAM_EPS) + ADAM_WD * w_ref[...])

    spec = pl.BlockSpec((None, tr, C), lambda l, i: (l, i, 0))
    shp = jax.ShapeDtypeStruct((L, R_, C), F32)
    return _pcall(body, name=name, grid=(L, R_ // tr), in_specs=[spec] * 4, out_specs=[spec] * 3,
                  out_shape=[shp] * 3, compiler_params=_cp(("parallel", "parallel")))(w, g, m, v)


HBM_SPEC = pl.BlockSpec(memory_space=pltpu.HBM)


def _place():
    x, y, c = lax.axis_index("x"), lax.axis_index("y"), lax.axis_index("c")
    chips = [(1 - x, y), (x, 1 - y), (1 - x, 1 - y)]
    return x, y, c, 2 * x + y, chips, [2 * cx + cy for cx, cy in chips], (x, y, 1 - c)


def _remote(src, dst, ssem, rsem, dev):
    return pltpu.make_async_remote_copy(src_ref=src, dst_ref=dst, send_sem=ssem, recv_sem=rsem,
                                        device_id=dev, device_id_type=MESH)


def _gather_weights(win, wout, conv):
    def body(win_ref, wout_ref, cv_ref, gin_ref, gout_ref, gcv_ref, ssem, rsem):
        x, y, c, s, chips, sid, sib = _place()

        def in_half(slot, hc):
            return gin_ref.at[slot, hc]

        def out_half(slot, hc):
            return gout_ref.at[slot, hc]

        sends = []
        for j, chip in enumerate(chips):
            dev = (*chip, c)
            sends.append(_remote(win_ref.at[c], in_half(s, c), ssem.at[j], rsem.at[j], dev))
            sends.append(_remote(wout_ref.at[c], out_half(s, c), ssem.at[3 + j], rsem.at[3 + j], dev))
            sends.append(_remote(cv_ref, gcv_ref.at[s], ssem.at[6 + j], rsem.at[6 + j], dev))
        for cp in sends:
            cp.start()
        for j in range(3):
            _remote(in_half(sid[j], c), in_half(sid[j], c), ssem.at[j], rsem.at[j], sib).wait_recv()
            f = _remote(in_half(sid[j], c), in_half(sid[j], c), ssem.at[9 + j], rsem.at[9 + j], sib)
            f.start()
            sends.append(f)
            _remote(out_half(sid[j], c), out_half(sid[j], c), ssem.at[3 + j], rsem.at[3 + j], sib).wait_recv()
            f = _remote(out_half(sid[j], c), out_half(sid[j], c), ssem.at[12 + j], rsem.at[12 + j], sib)
            f.start()
            sends.append(f)
        for j in range(3):
            _remote(in_half(sid[j], 1 - c), in_half(sid[j], 1 - c), ssem.at[9 + j], rsem.at[9 + j], sib).wait_recv()
            _remote(out_half(sid[j], 1 - c), out_half(sid[j], 1 - c), ssem.at[12 + j], rsem.at[12 + j], sib).wait_recv()
            _remote(gcv_ref.at[sid[j]], gcv_ref.at[sid[j]], ssem.at[6 + j], rsem.at[6 + j], sib).wait_recv()
        for cp in sends:
            cp.wait_send()

    return _pcall(
        body, name="gather_weights",
        in_specs=[HBM_SPEC] * 3, out_specs=[HBM_SPEC] * 3,
        out_shape=[jax.ShapeDtypeStruct((4,) + win.shape, win.dtype),
                   jax.ShapeDtypeStruct((4,) + wout.shape, wout.dtype),
                   jax.ShapeDtypeStruct((4,) + conv.shape, conv.dtype)],
        scratch_shapes=[pltpu.SemaphoreType.DMA((15,)), pltpu.SemaphoreType.DMA((15,))],
    )(win, wout, conv)


def _swap_sibling(pairs, stacked, name):
    n2, ns = len(pairs), len(stacked)
    n = n2 + ns

    def body(*refs):
        src2 = [(refs[2 * k], refs[2 * k + 1]) for k in range(n2)]
        srcs = refs[2 * n2:2 * n2 + ns]
        dst = refs[2 * n2 + ns:2 * n2 + ns + n]
        ssem, rsem = refs[-2], refs[-1]
        x, y, c, s, chips, sid, sib = _place()

        def exchange(give):
            cps = [_remote(src2[k][give], dst[k], ssem.at[k], rsem.at[k], sib) for k in range(n2)]
            for cp in cps:
                cp.start()
            for cp in cps:
                cp.wait()

        small = [_remote(srcs[k].at[1 - c], dst[n2 + k], ssem.at[n2 + k], rsem.at[n2 + k], sib) for k in range(ns)]
        for cp in small:
            cp.start()

        @pl.when(c == 0)
        def _():
            exchange(1)

        @pl.when(c == 1)
        def _():
            exchange(0)

        for cp in small:
            cp.wait()

    flat = [a for p in pairs for a in p] + list(stacked)
    return _pcall(
        body, name=name, in_specs=[HBM_SPEC] * len(flat), out_specs=[HBM_SPEC] * n,
        out_shape=[jax.ShapeDtypeStruct(p[0].shape, p[0].dtype) for p in pairs]
        + [jax.ShapeDtypeStruct(a.shape[1:], a.dtype) for a in stacked],
        scratch_shapes=[pltpu.SemaphoreType.DMA((n,)), pltpu.SemaphoreType.DMA((n,))],
    )(*flat)


def _scatter_chips(arrs, per_target, name):
    n = len(arrs)

    def body(*refs):
        src, dst = refs[:n], refs[n:2 * n]
        ssem, rsem = refs[2 * n], refs[2 * n + 1]
        x, y, c, s, chips, sid, sib = _place()
        sends = []
        for k in range(n):
            for j, chip in enumerate(chips):
                piece = src[k].at[sid[j]] if per_target[k] else src[k]
                sends.append(_remote(piece, dst[k].at[s], ssem.at[3 * k + j], rsem.at[3 * k + j], (*chip, c)))
        for cp in sends:
            cp.start()
        for k in range(n):
            for j in range(3):
                _remote(dst[k].at[sid[j]], dst[k].at[sid[j]], ssem.at[3 * k + j], rsem.at[3 * k + j], sib).wait_recv()
        for cp in sends:
            cp.wait_send()

    outs = [jax.ShapeDtypeStruct(a.shape if pt else (4,) + a.shape, a.dtype) for a, pt in zip(arrs, per_target)]
    return _pcall(
        body, name=name, in_specs=[HBM_SPEC] * n, out_specs=[HBM_SPEC] * n, out_shape=outs,
        scratch_shapes=[pltpu.SemaphoreType.DMA((3 * n,)), pltpu.SemaphoreType.DMA((3 * n,))],
    )(*arrs)


def _swap_whole(arrs, name):
    n = len(arrs)

    def body(*refs):
        src, dst, ssem, rsem = refs[:n], refs[n:2 * n], refs[2 * n], refs[2 * n + 1]
        *_, sib = _place()
        cps = [_remote(src[k], dst[k], ssem.at[k], rsem.at[k], sib) for k in range(n)]
        for cp in cps:
            cp.start()
        for cp in cps:
            cp.wait()

    return _pcall(
        body, name=name, in_specs=[HBM_SPEC] * n, out_specs=[HBM_SPEC] * n,
        out_shape=[jax.ShapeDtypeStruct(a.shape, a.dtype) for a in arrs],
        scratch_shapes=[pltpu.SemaphoreType.DMA((n,)), pltpu.SemaphoreType.DMA((n,))],
    )(*arrs)


def _perm_cols(w):
    parts = [w[..., int(_ORIG_OFF[oi]):int(_ORIG_OFF[oi]) + IN_SIZES[oi]] for oi, _ in _PIECES]
    parts.append(jnp.zeros(w.shape[:-1] + (NP - N_IN,), w.dtype))
    return jnp.concatenate(parts, -1)


def _perm_rows(w):
    return jnp.concatenate([w[..., 512:1536, :], w[..., 0:512, :], w[..., 1536:2048, :]], -2)


_SMALL = ("sinks", "r_conv_b", "r_wa", "r_ba", "r_wx", "r_bx", "r_lam", "g_a_log", "g_dt_bias", "g_norm_w",
          "ln_g", "ln_b", "r_conv_w", "g_conv_w")
_PACK_ROWS = 16


def _piece_rows(n):
    return -(-n // (128 * _PACK_ROWS)) * _PACK_ROWS


def _pack(arrs):
    parts = []
    for a in arrs:
        n = int(np.prod(a.shape))
        rows = _piece_rows(n)
        if n % 128 == 0:
            blk = a.reshape(n // 128, 128)
        else:
            blk = jnp.pad(a.reshape(1, n), ((0, 0), (0, (-n) % 128))).reshape(-1, 128)
        if blk.shape[0] < rows:
            blk = jnp.pad(blk, ((0, rows - blk.shape[0]), (0, 0)))
        parts.append(blk)
    return jnp.concatenate(parts, 0)


def _unpack(packed, shapes):
    out = []
    r = 0
    for shp in shapes:
        n = int(np.prod(shp))
        if n % 128 == 0:
            out.append(packed[r:r + n // 128].reshape(shp))
        else:
            nr = -(-n // 128)
            out.append(packed[r:r + nr].reshape(1, nr * 128)[:, :n].reshape(shp))
        r += _piece_rows(n)
    return out


def _tile(n, t):
    return min(n, t)


def _layer_fwd(l, x, xb, wb, wob, rope_c, rope_s, p):
    S_ = x.shape[0]
    proj = _matmul(xb, wb, ta=False, tb=False, tm=_tile(S_, 1024), tn=512, tk=wb.shape[0], out_dtype=F32,
                   name=f"in_proj_{l}")
    ya = _attn_fwd(proj, rope_c, rope_s, p["sinks"], T=_tile(S_, 512), name=f"attn_fwd_{l}")
    h, yr = _rglru_fwd(proj, p["r_conv_w"], p["r_conv_b"], p["r_wa"], p["r_ba"], p["r_wx"], p["r_bx"], p["r_lam"],
                       T=_tile(S_, 256), name=f"rglru_fwd_{l}")
    yg, st, tms = _gdn_fwd(proj, p["g_conv_w"], p["g_a_log"], p["g_dt_bias"], p["g_norm_w"],
                           T=_tile(S_, 256), name=f"gdn_fwd_{l}")
    ymix = jnp.concatenate([yr, ya, yg], 1)
    z = _outproj(ymix, wob, x, tm=_tile(S_, 256), name=f"out_proj_{l}")
    return dict(proj=proj, h=h, st=st, tms=tms, ymix=ymix, z=z)


def _layer_bwd(l, sv, x_b, dz, dzb, wo, wob, rope_c, rope_s, p):
    S_, D = dz.shape
    proj = sv["proj"]
    dymix = _matmul(dzb, wob, ta=False, tb=True, tm=_tile(S_, 1024), tn=512, tk=D, out_dtype=F32,
                    name=f"dmix_{l}")
    dwo = _matmul(sv["ymix"], dzb, ta=True, tb=False, tm=512, tn=_tile(D, 2048), tk=_tile(S_, 1024),
                  out_dtype=F32, name=f"dw_out_{l}",
                  out_blocks=((MIX_WIDTH, D), (512, _tile(D, 2048)),
                              lambda i, j: (jnp.where(i == 3, 3, (i + 1) % 3), j)))
    dq, daz, dk, dv, dkt, dvt, dsk = _attn_bwd(proj, rope_c, rope_s, p["sinks"], dymix, T=_tile(S_, 512),
                                               name=f"attn_bwd_{l}")
    (drx, drz, dcw_r, dcb_r, dwa, dba, dwx, dbx, dlam) = _rglru_bwd(
        proj, sv["h"], dymix, p["r_conv_w"], p["r_conv_b"], p["r_wa"], p["r_ba"], p["r_wx"], p["r_bx"], p["r_lam"],
        T=_tile(S_, 256), name=f"rglru_bwd_{l}")
    dqkv, dgz, dgba, dcw_g, dpv, dnw = _gdn_bwd(proj, sv["st"], sv["tms"], dymix, p["g_conv_w"], p["g_a_log"],
                                                p["g_dt_bias"], p["g_norm_w"], T=_tile(S_, 256), name=f"gdn_bwd_{l}")
    dproj = jnp.concatenate([dq, jnp.concatenate([dk[128:], dkt], 0), jnp.concatenate([dv[128:], dvt], 0), daz,
                             drx, drz, dqkv, dgz, dgba, jnp.zeros((S_, NP - N_IN - 120), BF16)], 1)
    dx = _matmul(dproj, wo, ta=False, tb=True, tm=_tile(S_, 1024), tn=_tile(D, 1024), tk=NP // 2, out_dtype=F32,
                 name=f"dx_{l}", extra=dz, alpha=DEEPNORM_ALPHA)
    dwin = _matmul(x_b, dproj, ta=True, tb=False, tm=_tile(D, 1024), tn=NP // 4, tk=_tile(S_, 1024), out_dtype=F32,
                   name=f"dw_in_{l}")
    small = dict(sinks=dsk[:, 0], r_conv_b=dcb_r[0], r_wa=dwa, r_ba=dba[0], r_wx=dwx, r_bx=dbx[0], r_lam=dlam[0],
                 g_a_log=dpv[0, 4:8], g_dt_bias=dpv[1, 4:8], g_norm_w=dnw[0], r_conv_w=dcw_r, g_conv_w=dcw_g)
    return dx, dwin, dwo, small


def kernel(x, w_in, sinks, r_conv_w, r_conv_b, r_wa, r_ba, r_wx, r_bx, r_lam, g_conv_w, g_a_log, g_dt_bias, g_norm_w, w_out, ln_g, ln_b, loss_target, m_w_in, m_sinks, m_r_conv_w, m_r_conv_b, m_r_wa, m_r_ba, m_r_wx, m_r_bx, m_r_lam, m_g_conv_w, m_g_a_log, m_g_dt_bias, m_g_norm_w, m_w_out, m_ln_g, m_ln_b, v_w_in, v_sinks, v_r_conv_w, v_r_conv_b, v_r_wa, v_r_ba, v_r_wx, v_r_bx, v_r_lam, v_g_conv_w, v_g_a_log, v_g_dt_bias, v_g_norm_w, v_w_out, v_ln_g, v_ln_b):
    S_, D = x.shape[1], x.shape[2]
    nsh = w_in.shape[2]
    rsh = w_out.shape[1]
    cx, cy, cc = lax.axis_index("x"), lax.axis_index("y"), lax.axis_index("c")
    chip = 2 * cx + cy
    rcw_n, gcw_n = r_conv_w.shape[2], g_conv_w.shape[2]

    conv_pack = jnp.concatenate([r_conv_w, g_conv_w], 2)
    w_in_b, w_out_b = w_in.astype(BF16), w_out.astype(BF16)
    g_in, g_out, g_conv = _gather_weights(w_in_b, w_out_b, conv_pack)

    def shards(own, got, sel):
        return [jnp.where(chip == t, sel(own), sel(got[t])) for t in range(4)]

    wb, wo, wob = [], [], []
    for l in range(DEPTH):
        w_full = jnp.concatenate(shards(w_in_b, g_in, lambda a: a[l]), 1)
        wb.append(_perm_cols(w_full))
        wo.append(jnp.pad(w_full, ((0, 0), (0, NP - N_IN))))
        wob.append(_perm_rows(jnp.concatenate(shards(w_out_b, g_out, lambda a: a[l]), 0)))
    rcw = jnp.concatenate(shards(conv_pack, g_conv, lambda a: a[:, :, :rcw_n]), 2)
    gcw = jnp.concatenate(shards(conv_pack, g_conv, lambda a: a[:, :, rcw_n:]), 2)

    pos = jnp.arange(S_, dtype=F32)[:, None]
    inv = 1.0 / (ROPE_THETA ** (jnp.arange(0, A_HEAD_DIM, 2, dtype=F32) / A_HEAD_DIM))
    ang = pos * inv[None, :]
    cos, sin = jnp.cos(ang), jnp.sin(ang)
    rope_c = jnp.concatenate([cos, cos, cos, cos], 1)
    rope_s = jnp.concatenate([-sin, sin, -sin, sin], 1)

    def params(l):
        return dict(sinks=sinks[l], r_conv_w=rcw[l], r_conv_b=r_conv_b[l], r_wa=r_wa[l], r_ba=r_ba[l],
                    r_wx=r_wx[l], r_bx=r_bx[l], r_lam=r_lam[l], g_conv_w=gcw[l], g_a_log=g_a_log[l],
                    g_dt_bias=g_dt_bias[l], g_norm_w=g_norm_w[l])

    xs, xbs, saved = [x[0]], [x[0].astype(BF16)], []
    for l in range(DEPTH):
        sv = _layer_fwd(l, xs[l], xbs[l], wb[l], wob[l], rope_c, rope_s, params(l))
        saved.append(sv)
        if l + 1 < DEPTH:
            xn, xnb = _ln_fwd(sv["z"], ln_g[l], ln_b[l], tm=_tile(S_, 256), name=f"ln_fwd_{l}")
            xs.append(xn)
            xbs.append(xnb)

    tm_ln = _tile(S_, 256)
    dz, dzb, dg_l, db_l, loss_part = _ln_bwd(saved[-1]["z"], ln_g[-1], ln_b[-1], loss_target[0], from_target=True,
                                             tm=tm_ln, name=f"ln_bwd_{DEPTH - 1}")
    dwin, dwo, small = [None] * DEPTH, [None] * DEPTH, [None] * DEPTH
    dlng, dlnb = [None] * DEPTH, [None] * DEPTH
    for l in reversed(range(DEPTH)):
        dlng[l], dlnb[l] = dg_l[0], db_l[0]
        dx, dwin[l], dwo[l], small[l] = _layer_bwd(l, saved[l], xbs[l], dz, dzb, wo[l], wob[l], rope_c, rope_s,
                                                   params(l))
        if l > 0:
            dz, dzb, dg_l, db_l, _ = _ln_bwd(saved[l - 1]["z"], ln_g[l - 1], ln_b[l - 1], dx, from_target=False,
                                             tm=tm_ln, name=f"ln_bwd_{l - 1}")
    grad_x = dx[None]
    loss = lax.psum(loss_part[0, 0], ("x", "y", "c"))

    sm = {k: jnp.stack([small[l][k] for l in range(DEPTH)]) for k in small[0]}
    sm["ln_g"], sm["ln_b"] = jnp.stack(dlng), jnp.stack(dlnb)
    names = list(_SMALL)
    gs = _pack([sm[n] for n in names])
    gs2 = gs.reshape(2, gs.shape[0] // 2, 128)
    in_got, out_got, s_got = _swap_sibling([(dwin[0], dwin[1]), (dwo[0], dwo[1])], [gs2], "reduce_pair")
    wcov = (-(-nsh // 128) + 1) * 128
    in_cp = _add_mine_windows(dwin[0], dwin[1], in_got, nsh, wcov, out_dtype=BF16, name="pair_sum_w_in")
    out_cp = _add_mine(dwo[0], dwo[1], out_got, out_dtype=BF16, tr=256, name="pair_sum_w_out").reshape(4, rsh, D)
    s_cp = _add_mine(gs2[0], gs2[1], s_got, out_dtype=F32, tr=gs2.shape[1], name="pair_sum_small")
    in_all, out_all, s_all = _scatter_chips([in_cp, out_cp, s_cp], [True, True, False], "reduce_chips")

    def own(a):
        return lax.dynamic_index_in_dim(a, chip, 0, keepdims=False)

    in_sum = _sum4(in_all, own(in_cp), tr=256, name="chip_sum_w_in")
    out_sum = _sum4(out_all, own(out_cp), tr=256, name="chip_sum_w_out")
    s_sum = _sum4(s_all, s_cp, tr=s_cp.shape[0], name="chip_sum_small")
    in_oth, out_oth, s_oth = _swap_whole([in_sum, out_sum, s_sum], "reduce_join")

    def both(mine, other):
        return jnp.where(cc == 0, jnp.stack([mine, other]), jnp.stack([other, mine]))

    g_w_in = lax.dynamic_slice_in_dim(both(in_sum, in_oth), (nsh * chip) % 128, nsh, 2)
    g_w_out = both(out_sum, out_oth)
    g_small = both(s_sum, s_oth).reshape(gs.shape)

    gsm = dict(zip(names, _unpack(g_small, [sm[n].shape for n in names])))
    gsm["r_conv_w"] = lax.dynamic_slice_in_dim(gsm["r_conv_w"], chip * rcw_n, rcw_n, 2)
    gsm["g_conv_w"] = lax.dynamic_slice_in_dim(gsm["g_conv_w"], chip * gcw_n, gcw_n, 2)
    wts = dict(sinks=sinks, r_conv_w=r_conv_w, r_conv_b=r_conv_b, r_wa=r_wa, r_ba=r_ba, r_wx=r_wx, r_bx=r_bx,
               r_lam=r_lam, g_conv_w=g_conv_w, g_a_log=g_a_log, g_dt_bias=g_dt_bias, g_norm_w=g_norm_w,
               ln_g=ln_g, ln_b=ln_b)
    mom = dict(sinks=m_sinks, r_conv_w=m_r_conv_w, r_conv_b=m_r_conv_b, r_wa=m_r_wa, r_ba=m_r_ba, r_wx=m_r_wx,
               r_bx=m_r_bx, r_lam=m_r_lam, g_conv_w=m_g_conv_w, g_a_log=m_g_a_log, g_dt_bias=m_g_dt_bias,
               g_norm_w=m_g_norm_w, ln_g=m_ln_g, ln_b=m_ln_b)
    vel = dict(sinks=v_sinks, r_conv_w=v_r_conv_w, r_conv_b=v_r_conv_b, r_wa=v_r_wa, r_ba=v_r_ba, r_wx=v_r_wx,
               r_bx=v_r_bx, r_lam=v_r_lam, g_conv_w=v_g_conv_w, g_a_log=v_g_a_log, g_dt_bias=v_g_dt_bias,
               g_norm_w=v_g_norm_w, ln_g=v_ln_g, ln_b=v_ln_b)
    pk = [_pack([d[n] for n in names]) for d in (wts, gsm, mom, vel)]
    sshapes = [wts[n].shape for n in names]
    d_s, m_s, v_s = _adamw(*[a[None] for a in pk], tr=pk[0].shape[0], name="adamw_small")
    d_sm, m_sm, v_sm = (dict(zip(names, _unpack(a[0], sshapes))) for a in (d_s, m_s, v_s))
    d_in, m_in, v_in = _adamw(w_in, g_w_in, m_w_in, v_w_in, tr=256, name="adamw_w_in")
    d_out, m_out, v_out = _adamw(w_out, g_w_out, m_w_out, v_w_out, tr=256, name="adamw_w_out")

    order = ["w_in", "sinks", "r_conv_w", "r_conv_b", "r_wa", "r_ba", "r_wx", "r_bx", "r_lam", "g_conv_w",
             "g_a_log", "g_dt_bias", "g_norm_w", "w_out", "ln_g", "ln_b"]
    grads = dict(gsm, w_in=g_w_in, w_out=g_w_out)
    deltas = dict(d_sm, w_in=d_in, w_out=d_out)
    new_m = dict(m_sm, w_in=m_in, w_out=m_out)
    new_v = dict(v_sm, w_in=v_in, w_out=v_out)
    return (loss, grad_x, *[grads[n] for n in order], *[deltas[n] for n in order],
            *[new_m[n] for n in order], *[new_v[n] for n in order])
```

```python
import functools
import math

import jax
import jax.numpy as jnp
import numpy as np
from jax import lax
from jax.experimental import pallas as pl
from jax.experimental.pallas import tpu as pltpu

F32 = jnp.float32
BF16 = jnp.bfloat16
MESH = pl.DeviceIdType.MESH

DEPTH = 2
A_HEADS, A_KV_HEADS, A_HEAD_DIM = 8, 2, 64
A_WIDTH, A_KV_WIDTH = 512, 128
WINDOW = 128
ROPE_THETA = 10000.0
R_WIDTH, R_BLOCKS, R_BLOCK_DIM, R_C = 1024, 8, 128, 8.0
CONV_WIDTH = 4
G_HEADS, G_HEAD_DIM, G_WIDTH, G_CHUNK = 4, 128, 512, 64
MIX_WIDTH = 2048
IN_SIZES = (512, 128, 128, 512, 1024, 1024, 512, 512, 512, 512, 4, 4)
N_IN = 5384
DEEPNORM_ALPHA = (2 * DEPTH) ** 0.25
LN_EPS = 1e-5
RMS_EPS = 1e-6
ADAM_LR, ADAM_B1, ADAM_B2, ADAM_EPS, ADAM_WD, ADAM_STEP = 0.001, 0.9, 0.999, 1e-08, 0.01, 10

NP = 5632
OFF_RX, OFF_RZ, OFF_AQ, OFF_AZ, OFF_GQKV, OFF_GZ, OFF_AK, OFF_AV, OFF_GBA = (
    0, 1024, 2048, 2560, 3072, 4608, 5120, 5248, 5376)
_ORIG_OFF = np.concatenate([[0], np.cumsum(IN_SIZES)])[:-1]
_PIECES = ((4, OFF_RX), (5, OFF_RZ), (0, OFF_AQ), (3, OFF_AZ), (6, OFF_GQKV), (7, OFF_GQKV + 512),
           (8, OFF_GQKV + 1024), (9, OFF_GZ), (1, OFF_AK), (2, OFF_AV), (10, OFF_GBA), (11, OFF_GBA + 4))
MIX_R, MIX_A, MIX_G = 0, 1024, 1536
VMEM_LIMIT = 56 * 1024 * 1024


def _pcall(body, **kw):
    return pl.pallas_call(body, **kw)


def _cp(sem, limit=VMEM_LIMIT):
    return pltpu.CompilerParams(dimension_semantics=sem, vmem_limit_bytes=limit)


def _sigmoid(x):
    return 0.5 + 0.5 * jnp.tanh(0.5 * x)


def _silu(x):
    return x * _sigmoid(x)


def _dsilu(x):
    s = _sigmoid(x)
    return s * (1.0 + x * (1.0 - s))


def _log1p(x):
    u = 1.0 + x
    d = jnp.where(u == 1.0, 1.0, u - 1.0)
    return jnp.where(u == 1.0, x, jnp.log(u) * (x / d))


def _softplus(x):
    return jnp.maximum(x, 0.0) + _log1p(jnp.exp(-jnp.abs(x)))


def _one_minus_exp(x):
    series = -x * (1.0 + x * (0.5 + x * (1.0 / 6.0 + x * (1.0 / 24.0))))
    return jnp.where(x > -0.05, series, 1.0 - jnp.exp(x))


def _nn(a, b):
    return lax.dot_general(a, b, (((1,), (0,)), ((), ())), preferred_element_type=F32)


def _nt(a, b):
    return lax.dot_general(a, b, (((1,), (1,)), ((), ())), preferred_element_type=F32)


def _tn(a, b):
    return lax.dot_general(a, b, (((0,), (0,)), ((), ())), preferred_element_type=F32)


def _b(x):
    return x.astype(BF16)


def _split3(x):
    hi = x.astype(BF16)
    r1 = x - hi.astype(F32)
    mid = r1.astype(BF16)
    lo = (r1 - mid.astype(F32)).astype(BF16)
    return hi, mid, lo


def _dot3(f, a, b):
    ah, am, _ = _split3(a)
    bh, bm, _ = _split3(b)
    return f(ah, bh) + (f(ah, bm) + f(am, bh))


def _dot_exact_lhs(f, a_bf16, b):
    bh, bm, bl = _split3(b)
    return f(a_bf16, bh) + (f(a_bf16, bm) + f(a_bf16, bl))


def _rot(x):
    w = x.shape[-1]
    lane = lax.broadcasted_iota(jnp.int32, (1, w), 1)
    return jnp.where((lane & 63) < 32, pltpu.roll(x, w - 32, 1), pltpu.roll(x, 32, 1))


def _conv_taps(ext, n):
    return [pltpu.roll(ext, 3 - k, 0)[8:8 + n] if k < 3 else ext[8:8 + n] for k in range(CONV_WIDTH)]


def _conv_taps_t(ext, n):
    m = ext.shape[0]
    return [pltpu.roll(ext, m - (3 - k), 0)[0:n] if k < 3 else ext[0:n] for k in range(CONV_WIDTH)]


def _scan_lin(a, b, reverse):
    n = a.shape[0]
    row = lax.broadcasted_iota(jnp.int32, (n, 1), 0)
    s = 1
    while s < n:
        if reverse:
            a_sh = pltpu.roll(a, n - s, 0)
            b_sh = pltpu.roll(b, n - s, 0)
            ok = row < (n - s)
        else:
            a_sh = pltpu.roll(a, s, 0)
            b_sh = pltpu.roll(b, s, 0)
            ok = row >= s
        b = jnp.where(ok, a * b_sh + b, b)
        a = jnp.where(ok, a * a_sh, a)
        s *= 2
    return a, b


class _Side:
    def __init__(self, inputs, out_shapes, n_sems, start, finish):
        self.inputs, self.out_shapes, self.n_sems, self.start, self.finish = inputs, out_shapes, n_sems, start, finish


def _matmul(a, b, *, ta, tb, tm, tn, tk, out_dtype, name, extra=None, alpha=0.0, out_blocks=None, side=None):
    if ta:
        K, M = a.shape
    else:
        M, K = a.shape
    if tb:
        N, K2 = b.shape
    else:
        K2, N = b.shape
    assert K == K2 and M % tm == 0 and N % tn == 0 and K % tk == 0, (a.shape, b.shape, tm, tn, tk)
    nk = K // tk
    ca = 0 if ta else 1
    cb = 1 if tb else 0
    has_extra = extra is not None

    assert nk == 1 or out_dtype == F32
    n_in = 2 + int(has_extra)
    ns_in = len(side.inputs) if side else 0
    ns_out = len(side.out_shapes) if side else 0
    grid = (M // tm, N // tn, nk)

    def body(*refs):
        a_ref, b_ref = refs[0], refs[1]
        e_ref = refs[2] if has_extra else None
        o_ref = refs[n_in + ns_in]
        k = pl.program_id(2)
        if side:
            s_in = refs[n_in:n_in + ns_in]
            s_out = refs[n_in + ns_in + 1:n_in + ns_in + 1 + ns_out]
            ssem, rsem = refs[-2], refs[-1]
            i, j = pl.program_id(0), pl.program_id(1)

            @pl.when((i == 0) & (j == 0) & (k == 0))
            def _():
                side.start(s_in, s_out, ssem, rsem)

            @pl.when((i == grid[0] - 1) & (j == grid[1] - 1) & (k == grid[2] - 1))
            def _():
                side.finish(s_in, s_out, ssem, rsem)

        part = lax.dot_general(a_ref[...], b_ref[...], (((ca,), (cb,)), ((), ())), preferred_element_type=F32)
        if nk == 1:
            if e_ref is not None:
                part = part + alpha * e_ref[...]
            o_ref[...] = part.astype(o_ref.dtype)
            return

        @pl.when(k == 0)
        def _():
            o_ref[...] = part

        @pl.when((k > 0) & (k < nk - 1))
        def _():
            o_ref[...] += part

        @pl.when(k == nk - 1)
        def _():
            last = o_ref[...] + part
            if e_ref is not None:
                last = last + alpha * e_ref[...]
            o_ref[...] = last

    a_spec = (pl.BlockSpec((tk, tm), lambda i, j, k: (k, i)) if ta
              else pl.BlockSpec((tm, tk), lambda i, j, k: (i, k)))
    b_spec = (pl.BlockSpec((tn, tk), lambda i, j, k: (j, k)) if tb
              else pl.BlockSpec((tk, tn), lambda i, j, k: (k, j)))
    e_spec = pl.BlockSpec((tm, tn), lambda i, j, k: (i, j))
    if out_blocks is None:
        o_spec, o_shape = e_spec, (M, N)
    else:
        o_shape, o_block, o_map = out_blocks
        o_spec = pl.BlockSpec(o_block, lambda i, j, k: o_map(i, j))
    in_specs = [a_spec, b_spec] + ([e_spec] if has_extra else [])
    args = (a, b) + ((extra,) if has_extra else ())
    if not side:
        return _pcall(
            body, name=name, grid=grid, in_specs=in_specs, out_specs=o_spec,
            out_shape=jax.ShapeDtypeStruct(o_shape, out_dtype),
            compiler_params=_cp(("parallel", "parallel", "arbitrary")),
        )(*args)
    outs = _pcall(
        body, name=name, grid=grid, in_specs=in_specs + [HBM_SPEC] * ns_in,
        out_specs=[o_spec] + [HBM_SPEC] * ns_out,
        out_shape=[jax.ShapeDtypeStruct(o_shape, out_dtype)] + list(side.out_shapes),
        scratch_shapes=[pltpu.SemaphoreType.DMA((side.n_sems,)), pltpu.SemaphoreType.DMA((side.n_sems,))],
        compiler_params=_cp(("arbitrary", "arbitrary", "arbitrary")),
    )(*args, *side.inputs)
    return outs[0], outs[1:]


def _outproj(ymix, wo, x, *, tm, name):
    S_, D = x.shape

    def body(y_ref, w_ref, x_ref, z_ref):
        z_ref[...] = DEEPNORM_ALPHA * x_ref[...] + _nn(y_ref[...], w_ref[...])

    return _pcall(
        body, name=name, grid=(S_ // tm,),
        in_specs=[pl.BlockSpec((tm, MIX_WIDTH), lambda i: (i, 0)),
                  pl.BlockSpec((MIX_WIDTH, D), lambda i: (0, 0)),
                  pl.BlockSpec((tm, D), lambda i: (i, 0))],
        out_specs=pl.BlockSpec((tm, D), lambda i: (i, 0)),
        out_shape=jax.ShapeDtypeStruct((S_, D), F32),
        compiler_params=_cp(("parallel",)),
    )(ymix, wo, x)


def _ln_stats(z):
    mu = jnp.mean(z, -1, keepdims=True)
    zc = z - mu
    var = jnp.mean(zc * zc, -1, keepdims=True)
    rstd = lax.rsqrt(var + LN_EPS)
    return zc * rstd, rstd


def _ln_fwd(z, g, b, *, tm, name):
    S_, D = z.shape

    def body(z_ref, g_ref, b_ref, y_ref, yb_ref):
        xh, _ = _ln_stats(z_ref[...])
        y = xh * g_ref[...] + b_ref[...]
        y_ref[...] = y
        yb_ref[...] = y.astype(BF16)

    row = pl.BlockSpec((tm, D), lambda i: (i, 0))
    vec = pl.BlockSpec((1, D), lambda i: (0, 0))
    return _pcall(
        body, name=name, grid=(S_ // tm,), in_specs=[row, vec, vec], out_specs=[row, row],
        out_shape=[jax.ShapeDtypeStruct((S_, D), F32), jax.ShapeDtypeStruct((S_, D), BF16)],
        compiler_params=_cp(("parallel",)),
    )(z, g.reshape(1, D), b.reshape(1, D))


def _ln_bwd(z, g, b, other, *, from_target, tm, name):
    S_, D = z.shape

    def body(z_ref, g_ref, b_ref, o_ref, dz_ref, dzb_ref, dg_ref, db_ref, loss_ref):
        i = pl.program_id(0)

        @pl.when(i == 0)
        def _():
            dg_ref[...] = jnp.zeros_like(dg_ref)
            db_ref[...] = jnp.zeros_like(db_ref)
            loss_ref[...] = jnp.zeros_like(loss_ref)

        xh, rstd = _ln_stats(z_ref[...])
        gam = g_ref[...]
        if from_target:
            err = xh * gam + b_ref[...] - o_ref[...]
            per_tok = jnp.mean(err * err, -1, keepdims=True)
            loss_ref[...] += 0.5 * jnp.sum(per_tok, 0, keepdims=True)
            dy = err * (1.0 / D)
        else:
            dy = o_ref[...]
        dxh = dy * gam
        m1 = jnp.mean(dxh, -1, keepdims=True)
        m2 = jnp.mean(dxh * xh, -1, keepdims=True)
        dz = rstd * (dxh - m1 - xh * m2)
        dz_ref[...] = dz
        dzb_ref[...] = dz.astype(BF16)
        dg_ref[...] += jnp.sum(dy * xh, 0, keepdims=True)
        db_ref[...] += jnp.sum(dy, 0, keepdims=True)

    row = pl.BlockSpec((tm, D), lambda i: (i, 0))
    vec = pl.BlockSpec((1, D), lambda i: (0, 0))
    one = pl.BlockSpec((1, 1), lambda i: (0, 0))
    return _pcall(
        body, name=name, grid=(S_ // tm,), in_specs=[row, vec, vec, row],
        out_specs=[row, row, vec, vec, one],
        out_shape=[jax.ShapeDtypeStruct((S_, D), F32), jax.ShapeDtypeStruct((S_, D), BF16),
                   jax.ShapeDtypeStruct((1, D), F32), jax.ShapeDtypeStruct((1, D), F32),
                   jax.ShapeDtypeStruct((1, 1), F32)],
        compiler_params=_cp(("arbitrary",)),
    )(z, g.reshape(1, D), b.reshape(1, D), other)


def _attn_masks(i, sk_ref):
    ri = lax.broadcasted_iota(jnp.int32, (512, 256), 0)
    cj = lax.broadcasted_iota(jnp.int32, (512, 256), 1)
    diff = (ri & 127) - cj + 128
    band = (diff >= 0) & (diff < WINDOW)
    bias = jnp.where(band, 0.0, -jnp.inf)
    bias0 = jnp.where(band & ((i > 0) | (cj >= 128)), 0.0, -jnp.inf)
    grp = lax.broadcasted_iota(jnp.int32, (512, 1), 0) >> 7
    skvs = []
    for h in range(A_KV_HEADS):
        skv = jnp.zeros((512, 1), F32)
        for g in range(4):
            skv = jnp.where(grp == g, sk_ref[h * 4 + g], skv)
        skvs.append(skv)
    return bias0, bias, skvs


def _attn_common(masks, b, h, qr, kd, vd):
    lane = lax.broadcasted_iota(jnp.int32, (1, 128), 1)
    lof = (lane < 64).astype(F32)
    hif = 1.0 - lof
    r0 = b * 128
    skv = masks[2][h]
    pairs = [qr[r0:r0 + 128, h * 256 + p * 128:h * 256 + (p + 1) * 128] for p in (0, 1)]
    qs = _b(jnp.concatenate([pairs[0] * lof, pairs[0] * hif, pairs[1] * lof, pairs[1] * hif], 0))
    k2 = kd[h][r0:r0 + 256]
    v2 = vd[h][r0:r0 + 256]
    s = _nt(qs, k2) * (A_HEAD_DIM ** -0.5) + (masks[0] if b == 0 else masks[1])
    m = jnp.maximum(jnp.max(s, 1, keepdims=True), skv)
    p = jnp.exp(s - m)
    esk = jnp.exp(skv - m)
    rz = 1.0 / (jnp.sum(p, 1, keepdims=True) + esk)
    prob = p * rz
    o4 = _nn(_b(prob), v2)
    return lof, hif, qs, k2, v2, prob, esk * rz, o4


def _attn_prep(T, q_ref, k_ref, v_ref, c_ref, s_ref, kprev, vprev):
    C = c_ref[...]
    Sg = s_ref[...]
    C4 = jnp.concatenate([C] * 4, 1)
    S4 = jnp.concatenate([Sg] * 4, 1)
    q = q_ref[...]
    qr = q * C4 + _rot(q) * S4
    k = k_ref[...]
    kr = k * C + _rot(k) * Sg
    v = v_ref[...]
    kext = jnp.concatenate([kprev[...], kr], 0)
    vext = jnp.concatenate([vprev[...], v], 0)
    kprev[...] = kr[T - 128:]
    vprev[...] = v[T - 128:]
    lo = lax.broadcasted_iota(jnp.int32, (1, 128), 1) < 64
    kroll = pltpu.roll(kext, 64, 1)
    vroll = pltpu.roll(vext, 64, 1)
    kd = [_b(jnp.where(lo, kext, kroll)), _b(jnp.where(lo, kroll, kext))]
    vd = [_b(jnp.where(lo, vext, vroll)), _b(jnp.where(lo, vroll, vext))]
    return C, Sg, C4, S4, qr, kd, vd


def _attn_specs(T):
    return [pl.BlockSpec(memory_space=pltpu.SMEM),
            pl.BlockSpec((T, 512), lambda i: (i, OFF_AQ // 512)),
            pl.BlockSpec((T, 512), lambda i: (i, OFF_AZ // 512)),
            pl.BlockSpec((T, 128), lambda i: (i, OFF_AK // 128)),
            pl.BlockSpec((T, 128), lambda i: (i, OFF_AV // 128)),
            pl.BlockSpec((T, 128), lambda i: (i, 0)),
            pl.BlockSpec((T, 128), lambda i: (i, 0))]


def _attn_fwd(proj, rope_c, rope_s, sinks, *, T, name):
    S_ = proj.shape[0]
    nb = T // 128

    def body(sk_ref, q_ref, z_ref, k_ref, v_ref, c_ref, s_ref, y_ref, kprev, vprev):
        i = pl.program_id(0)

        @pl.when(i == 0)
        def _():
            kprev[...] = jnp.zeros_like(kprev)
            vprev[...] = jnp.zeros_like(vprev)

        _, _, _, _, qr, kd, vd = _attn_prep(T, q_ref, k_ref, v_ref, c_ref, s_ref, kprev, vprev)
        masks = _attn_masks(i, sk_ref)
        for b in range(nb):
            r0 = b * 128
            for h in range(2):
                lof, hif, _, _, _, _, _, o4 = _attn_common(masks, b, h, qr, kd, vd)
                for p in range(2):
                    cs = slice(h * 256 + p * 128, h * 256 + (p + 1) * 128)
                    o = o4[2 * p * 128:(2 * p + 1) * 128] * lof + o4[(2 * p + 1) * 128:(2 * p + 2) * 128] * hif
                    y_ref[r0:r0 + 128, cs] = (o * _silu(z_ref[r0:r0 + 128, cs])).astype(BF16)

    return _pcall(
        body, name=name, grid=(S_ // T,), in_specs=_attn_specs(T),
        out_specs=pl.BlockSpec((T, 512), lambda i: (i, 0)),
        out_shape=jax.ShapeDtypeStruct((S_, 512), BF16),
        scratch_shapes=[pltpu.VMEM((128, 128), F32), pltpu.VMEM((128, 128), F32)],
        compiler_params=_cp(("arbitrary",)),
    )(sinks, proj, proj, proj, proj, rope_c, rope_s)


def _attn_bwd(proj, rope_c, rope_s, sinks, dymix, *, T, name):
    S_ = proj.shape[0]
    nb = T // 128
    nt = S_ // T

    def body(sk_ref, q_ref, z_ref, k_ref, v_ref, c_ref, s_ref, dy_ref,
             dq_ref, dz_ref, dk_ref, dv_ref, dkt_ref, dvt_ref, dsk_ref,
             kprev, vprev, cprev, sprev, dkacc, dvacc, dqacc):
        i = pl.program_id(0)

        @pl.when(i == 0)
        def _():
            kprev[...] = jnp.zeros_like(kprev)
            vprev[...] = jnp.zeros_like(vprev)
            cprev[...] = jnp.zeros_like(cprev)
            sprev[...] = jnp.zeros_like(sprev)
            dkacc[...] = jnp.zeros_like(dkacc)
            dvacc[...] = jnp.zeros_like(dvacc)
            dsk_ref[...] = jnp.zeros_like(dsk_ref)

        @pl.when(i > 0)
        def _():
            dkacc[0:128, :] = dkacc[T:T + 128, :]
            dvacc[0:128, :] = dvacc[T:T + 128, :]
            dkacc[128:, :] = jnp.zeros((T, 128), F32)
            dvacc[128:, :] = jnp.zeros((T, 128), F32)

        C, Sg, C4, S4, qr, kd, vd = _attn_prep(T, q_ref, k_ref, v_ref, c_ref, s_ref, kprev, vprev)
        masks = _attn_masks(i, sk_ref)
        lane = lax.broadcasted_iota(jnp.int32, (1, 128), 1)
        for b in range(nb):
            r0 = b * 128
            for h in range(2):
                lof, hif, qs, k2, v2, prob, psink, o4 = _attn_common(masks, b, h, qr, kd, vd)
                dos = []
                for p in range(2):
                    cs = slice(h * 256 + p * 128, h * 256 + (p + 1) * 128)
                    o = o4[2 * p * 128:(2 * p + 1) * 128] * lof + o4[(2 * p + 1) * 128:(2 * p + 2) * 128] * hif
                    zc = z_ref[r0:r0 + 128, cs]
                    dyc = dy_ref[r0:r0 + 128, cs]
                    dz_ref[r0:r0 + 128, cs] = (dyc * o * _dsilu(zc)).astype(BF16)
                    do = dyc * _silu(zc)
                    dos += [do * lof, do * hif]
                dos = jnp.concatenate(dos, 0)
                os_ = jnp.concatenate([o4[0:128] * lof, o4[128:256] * hif, o4[256:384] * lof, o4[384:512] * hif], 0)
                delta = jnp.sum(dos * os_, 1, keepdims=True)
                dosb = _b(dos)
                dp = _nt(dosb, v2)
                ds = prob * (dp - delta)
                dsv = -psink * delta
                for g in range(4):
                    sg = jnp.sum(dsv[g * 128:(g + 1) * 128], 0, keepdims=True)
                    hd = h * 4 + g
                    dsk_ref[hd:hd + 1, :] += jnp.broadcast_to(sg, (1, 128))
                dsb = _b(ds * (A_HEAD_DIM ** -0.5))
                dqs = _nn(dsb, k2)
                for p in range(2):
                    cs = slice(h * 256 + p * 128, h * 256 + (p + 1) * 128)
                    dqacc[r0:r0 + 128, cs] = (dqs[2 * p * 128:(2 * p + 1) * 128] * lof
                                              + dqs[(2 * p + 1) * 128:(2 * p + 2) * 128] * hif)
                dkdup = _tn(dsb, qs)
                dvdup = _tn(_b(prob), dosb)
                half = (lane < 64) if h == 0 else (lane >= 64)
                dkacc[r0:r0 + 256, :] += jnp.where(half, dkdup + pltpu.roll(dkdup, 64, 1), 0.0)
                dvacc[r0:r0 + 256, :] += jnp.where(half, dvdup + pltpu.roll(dvdup, 64, 1), 0.0)
        dqr = dqacc[...]
        dq_ref[...] = (dqr * C4 + _rot(dqr * S4)).astype(BF16)
        cext = jnp.concatenate([cprev[...], C], 0)
        sext = jnp.concatenate([sprev[...], Sg], 0)
        dke = dkacc[...]
        dkp = dke * cext + _rot(dke * sext)
        dk_ref[...] = dkp[0:T].astype(BF16)
        dkt_ref[...] = dkp[T:T + 128].astype(BF16)
        dve = dvacc[...]
        dv_ref[...] = dve[0:T].astype(BF16)
        dvt_ref[...] = dve[T:T + 128].astype(BF16)
        cprev[...] = C[T - 128:]
        sprev[...] = Sg[T - 128:]

    wide = pl.BlockSpec((T, 512), lambda i: (i, 0))
    nar = pl.BlockSpec((T, 128), lambda i: (i, 0))
    tail = pl.BlockSpec((128, 128), lambda i: (0, 0))
    return _pcall(
        body, name=name, grid=(nt,),
        in_specs=_attn_specs(T) + [pl.BlockSpec((T, 512), lambda i: (i, MIX_A // 512))],
        out_specs=[wide, wide, nar, nar, tail, tail, pl.BlockSpec((8, 128), lambda i: (0, 0))],
        out_shape=[jax.ShapeDtypeStruct((S_, 512), BF16), jax.ShapeDtypeStruct((S_, 512), BF16),
                   jax.ShapeDtypeStruct((S_, 128), BF16), jax.ShapeDtypeStruct((S_, 128), BF16),
                   jax.ShapeDtypeStruct((128, 128), BF16), jax.ShapeDtypeStruct((128, 128), BF16),
                   jax.ShapeDtypeStruct((8, 128), F32)],
        scratch_shapes=[pltpu.VMEM((128, 128), F32)] * 4
        + [pltpu.VMEM((T + 128, 128), F32), pltpu.VMEM((T + 128, 128), F32), pltpu.VMEM((T, 512), F32)],
        compiler_params=_cp(("arbitrary",)),
    )(sinks, proj, proj, proj, proj, rope_c, rope_s, dymix)


def _rg_gates(xr, wa_ref, ba_ref, wx_ref, bx_ref, lam_ref):
    xb = _b(xr)
    pre_a = jnp.concatenate([_nn(xb[:, n * 128:(n + 1) * 128], wa_ref[n]) for n in range(R_BLOCKS)], 1) + ba_ref[...]
    pre_x = jnp.concatenate([_nn(xb[:, n * 128:(n + 1) * 128], wx_ref[n]) for n in range(R_BLOCKS)], 1) + bx_ref[...]
    r = _sigmoid(pre_a)
    ig = _sigmoid(pre_x)
    sp = _softplus(-lam_ref[...])
    log_a = -R_C * r * sp
    a = jnp.exp(log_a)
    mult = jnp.sqrt(_one_minus_exp(2.0 * log_a))
    return xb, r, ig, sp, a, mult


def _rg_param_specs():
    C = R_WIDTH
    vec = pl.BlockSpec((1, C), lambda i: (0, 0))
    blk = pl.BlockSpec((R_BLOCKS, 128, 128), lambda i: (0, 0, 0))
    return [pl.BlockSpec((CONV_WIDTH, C), lambda i: (0, 0)), vec, blk, vec, blk, vec, vec]


def _rglru_fwd(proj, cw, cb, wa, ba, wx, bx, lam, *, T, name):
    S_ = proj.shape[0]
    C = R_WIDTH

    def body(rx_ref, rz_ref, cw_ref, cb_ref, wa_ref, ba_ref, wx_ref, bx_ref, lam_ref,
             h_ref, y_ref, halo, hcar):
        i = pl.program_id(0)

        @pl.when(i == 0)
        def _():
            halo[...] = jnp.zeros_like(halo)
            hcar[...] = jnp.zeros_like(hcar)

        rx = rx_ref[...]
        ext = jnp.concatenate([halo[...], rx], 0)
        halo[...] = rx[T - 8:]
        taps = _conv_taps(ext, T)
        xr = cb_ref[...] + sum(cw_ref[k:k + 1, :] * taps[k] for k in range(CONV_WIDTH))
        _, _, ig, _, a, mult = _rg_gates(xr, wa_ref, ba_ref, wx_ref, bx_ref, lam_ref)
        u = mult * (ig * xr)
        acum, hloc = _scan_lin(a, u, False)
        h = hloc + acum * hcar[0:1, :]
        hcar[...] = jnp.broadcast_to(h[T - 1:T, :], (8, C))
        h_ref[...] = h
        y_ref[...] = (h * _silu(rz_ref[...])).astype(BF16)

    row = pl.BlockSpec((T, C), lambda i: (i, 0))
    return _pcall(
        body, name=name, grid=(S_ // T,),
        in_specs=[pl.BlockSpec((T, C), lambda i: (i, OFF_RX // C)),
                  pl.BlockSpec((T, C), lambda i: (i, OFF_RZ // C))] + _rg_param_specs(),
        out_specs=[row, row],
        out_shape=[jax.ShapeDtypeStruct((S_, C), F32), jax.ShapeDtypeStruct((S_, C), BF16)],
        scratch_shapes=[pltpu.VMEM((8, C), F32), pltpu.VMEM((8, C), F32)],
        compiler_params=_cp(("arbitrary",)),
    )(proj, proj, cw, cb.reshape(1, C), _b(wa), ba.reshape(1, C), _b(wx), bx.reshape(1, C), lam.reshape(1, C))


def _rglru_bwd(proj, h, dymix, cw, cb, wa, ba, wx, bx, lam, *, T, name):
    S_ = proj.shape[0]
    C = R_WIDTH
    nt = S_ // T
    t8 = T // 8

    def body(rx_ref, rxp_ref, rz_ref, h_ref, hp_ref, dy_ref,
             cw_ref, cb_ref, wa_ref, ba_ref, wx_ref, bx_ref, lam_ref, wat_ref, wxt_ref,
             drx_ref, drz_ref, dcw_ref, dcb_ref, dwa_ref, dba_ref, dwx_ref, dbx_ref, dlam_ref,
             afirst, gfirst, dhalo):
        i = pl.program_id(0)
        first_tile = (i == nt - 1)

        @pl.when(i == 0)
        def _():
            afirst[...] = jnp.zeros_like(afirst)
            gfirst[...] = jnp.zeros_like(gfirst)
            dhalo[...] = jnp.zeros_like(dhalo)
            for r in (dcw_ref, dcb_ref, dwa_ref, dba_ref, dwx_ref, dbx_ref, dlam_ref):
                r[...] = jnp.zeros_like(r)

        keep = jnp.where(first_tile, 0.0, 1.0)
        rx = rx_ref[...]
        ext = jnp.concatenate([rxp_ref[...] * keep, rx], 0)
        taps = _conv_taps(ext, T)
        xr = cb_ref[...] + sum(cw_ref[k:k + 1, :] * taps[k] for k in range(CONV_WIDTH))
        xb, r, ig, sp, a, mult = _rg_gates(xr, wa_ref, ba_ref, wx_ref, bx_ref, lam_ref)
        hh = h_ref[...]
        rz = rz_ref[...]
        dy = dy_ref[...]
        drz_ref[...] = (dy * hh * _dsilu(rz)).astype(BF16)
        dh = dy * _silu(rz)
        row = lax.broadcasted_iota(jnp.int32, (T, 1), 0)
        c = jnp.where(row == T - 1, afirst[0:1, :], pltpu.roll(a, T - 1, 0))
        ccum, gloc = _scan_lin(c, dh, True)
        g = gloc + ccum * gfirst[0:1, :]
        afirst[...] = jnp.broadcast_to(a[0:1, :], (8, C))
        gfirst[...] = jnp.broadcast_to(g[0:1, :], (8, C))
        hprev = jnp.where(row == 0, hp_ref[7:8, :] * keep, pltpu.roll(hh, 1, 0))
        da = g * hprev
        gx = ig * xr
        dgx = g * mult
        dmult = g * gx
        dlog_a = da * a - dmult * (a * a) * lax.rsqrt(mult * mult)
        dpre_a = dlog_a * (-R_C * sp) * r * (1.0 - r)
        dpre_x = dgx * xr * ig * (1.0 - ig)
        dlam_ref[...] += jnp.sum(dlog_a * (-R_C * r), 0, keepdims=True) * (-_sigmoid(-lam_ref[...]))
        dab = _b(dpre_a)
        dxb = _b(dpre_x)
        dxr = dgx * ig + jnp.concatenate(
            [_nn(dab[:, n * 128:(n + 1) * 128], wat_ref[n]) + _nn(dxb[:, n * 128:(n + 1) * 128], wxt_ref[n])
             for n in range(R_BLOCKS)], 1)
        for n in range(R_BLOCKS):
            cs = slice(n * 128, (n + 1) * 128)
            dwa_ref[n] += _tn(xb[:, cs], dab[:, cs])
            dwx_ref[n] += _tn(xb[:, cs], dxb[:, cs])
        dba_ref[...] += jnp.sum(dpre_a, 0, keepdims=True)
        dbx_ref[...] += jnp.sum(dpre_x, 0, keepdims=True)
        dcb_ref[...] += jnp.sum(dxr, 0, keepdims=True)
        for k in range(CONV_WIDTH):
            dcw_ref[k:k + 1, :] += jnp.sum(dxr * taps[k], 0, keepdims=True)
        ext2 = jnp.concatenate([dxr, dhalo[...]], 0)
        tt = _conv_taps_t(ext2, T)
        drx_ref[...] = sum(cw_ref[k:k + 1, :] * tt[k] for k in range(CONV_WIDTH)).astype(BF16)
        dhalo[...] = dxr[0:8]

    def rev(i):
        return nt - 1 - i

    def prev8(i):
        return jnp.maximum(rev(i) * t8 - 1, 0)

    vec = pl.BlockSpec((1, C), lambda i: (0, 0))
    blk = pl.BlockSpec((R_BLOCKS, 128, 128), lambda i: (0, 0, 0))
    row = pl.BlockSpec((T, C), lambda i: (rev(i), 0))
    wat = _b(jnp.swapaxes(wa, 1, 2))
    wxt = _b(jnp.swapaxes(wx, 1, 2))
    return _pcall(
        body, name=name, grid=(nt,),
        in_specs=[pl.BlockSpec((T, C), lambda i: (rev(i), OFF_RX // C)),
                  pl.BlockSpec((8, C), lambda i: (prev8(i), OFF_RX // C)),
                  pl.BlockSpec((T, C), lambda i: (rev(i), OFF_RZ // C)),
                  row,
                  pl.BlockSpec((8, C), lambda i: (prev8(i), 0)),
                  pl.BlockSpec((T, C), lambda i: (rev(i), MIX_R // C)),
                  ] + _rg_param_specs() + [blk, blk],
        out_specs=[row, row, pl.BlockSpec((CONV_WIDTH, C), lambda i: (0, 0)), vec, blk, vec, blk, vec, vec],
        out_shape=[jax.ShapeDtypeStruct((S_, C), BF16), jax.ShapeDtypeStruct((S_, C), BF16),
                   jax.ShapeDtypeStruct((CONV_WIDTH, C), F32), jax.ShapeDtypeStruct((1, C), F32),
                   jax.ShapeDtypeStruct((R_BLOCKS, 128, 128), F32), jax.ShapeDtypeStruct((1, C), F32),
                   jax.ShapeDtypeStruct((R_BLOCKS, 128, 128), F32), jax.ShapeDtypeStruct((1, C), F32),
                   jax.ShapeDtypeStruct((1, C), F32)],
        scratch_shapes=[pltpu.VMEM((8, C), F32)] * 3,
        compiler_params=_cp(("arbitrary",)),
    )(proj, proj, proj, h, h, dymix, cw, cb.reshape(1, C), _b(wa), ba.reshape(1, C), _b(wx), bx.reshape(1, C),
      lam.reshape(1, C), wat, wxt)


GW3 = 3 * G_WIDTH


def _lane_col(x, lane_idx):
    lane = lax.broadcasted_iota(jnp.int32, (1, x.shape[1]), 1)
    return jnp.sum(jnp.where(lane == lane_idx, x, 0.0), 1, keepdims=True)


def _gdn_pre(ext, T, cw_ref, gba, pv_ref):
    taps = _conv_taps(ext, T)
    c = sum(cw_ref[k:k + 1, :] * taps[k] for k in range(CONV_WIDTH))
    qkv = _silu(c)
    beta = _sigmoid(gba)
    sarg = gba + pv_ref[1:2, :]
    nea = -jnp.exp(pv_ref[0:1, :])
    gdec = nea * _softplus(sarg)
    ri = lax.broadcasted_iota(jnp.int32, (T, T), 0)
    cj = lax.broadcasted_iota(jnp.int32, (T, T), 1)
    same = (ri >> 6) == (cj >> 6)
    ltri = jnp.where((ri >= cj) & same, 1.0, 0.0).astype(BF16)
    gc = _dot_exact_lhs(_nn, ltri, gdec)
    return taps, c, qkv, beta, sarg, nea, gdec, gc


def _gdn_masks():
    ri = lax.broadcasted_iota(jnp.int32, (128, 128), 0)
    cj = lax.broadcasted_iota(jnp.int32, (128, 128), 1)
    same = (ri >> 6) == (cj >> 6)
    return (ri >= cj) & same, (ri > cj) & same, ri == cj


def _lockstep(gens):
    out = [None] * len(gens)
    live = list(range(len(gens)))
    while live:
        still = []
        for k in live:
            try:
                next(gens[k])
                still.append(k)
            except StopIteration as stop:
                out[k] = stop.value
        live = still
    return out


def _gdn_chunk(qkv, beta, gc, rs, h, tm=None):
    tril, strict, eye = _gdn_masks()
    rowi = lax.broadcasted_iota(jnp.int32, (128, 1), 0)
    lane = lax.broadcasted_iota(jnp.int32, (1, 128), 1)
    qh = qkv[rs, h * 128:(h + 1) * 128]
    kh = qkv[rs, 512 + h * 128:512 + (h + 1) * 128]
    vh = qkv[rs, 1024 + h * 128:1024 + (h + 1) * 128]
    rq = lax.rsqrt(jnp.sum(qh * qh, 1, keepdims=True) + RMS_EPS)
    rk = lax.rsqrt(jnp.sum(kh * kh, 1, keepdims=True) + RMS_EPS)
    qn = qh * (rq * (G_HEAD_DIM ** -0.5))
    kn = kh * rk
    gcb = gc[rs]
    gcol = _lane_col(gcb, 4 + h)
    bcol = _lane_col(beta[rs], h)
    grow = _dot_exact_lhs(_nt, jnp.ones((128, 128), BF16), jnp.where(lane == 4 + h, gcb, 0.0))
    D = jnp.where(tril, jnp.exp(jnp.minimum(gcol - grow, 0.0)), 0.0)
    kb = kn * bcol
    vb = vh * bcol
    knb = _b(kn)
    A = _nt(_b(kb), knb)
    Bm = _nt(_b(qn), knb)
    yield
    if tm is None:
        N = jnp.where(strict, -(A * D), 0.0)
        tm = jnp.where(eye, 1.0, 0.0) + N
        npow = N
        for _ in range(5):
            npow = _dot3(_nn, npow, npow)
            yield
            tm = tm + _dot3(_nn, tm, npow)
            yield
    eg = jnp.exp(gcol)
    u = _dot3(_nn, tm, vb)
    w = _dot3(_nn, tm, kb * eg)
    yield
    qk = jnp.where(tril, Bm * D, 0.0)
    qd = qn * eg
    gla = jnp.sum(jnp.where(rowi == 63, gcol, 0.0), 0, keepdims=True)
    glb = jnp.sum(jnp.where(rowi == 127, gcol, 0.0), 0, keepdims=True)
    ed = jnp.exp(jnp.where(rowi < 64, gla, glb) - gcol)
    kd = kn * ed
    return dict(qh=qh, kh=kh, vh=vh, rq=rq, rk=rk, qn=qn, kn=kn, gcol=gcol, bcol=bcol, D=D, A=A, Bm=Bm,
                tm=tm, eg=eg, ed=ed, u=u, w=w, qk=qk, qd=qd, kd=kd, kb=kb, vb=vb,
                gla=jnp.exp(gla), glb=jnp.exp(glb))


def _gdn_scan(q, sa):
    sab = _b(sa)
    wb = _b(q["w"])
    vna = q["u"] - _nn(wb, sab)
    yield
    sb = sa * q["gla"] + _tn(_b(q["kd"][0:64]), _b(vna[0:64]))
    yield
    sbb = _b(sb)
    vnb = q["u"] - _nn(wb, sbb)
    yield
    sn = sb * q["glb"] + _tn(_b(q["kd"][64:128]), _b(vnb[64:128]))
    yield
    vn = jnp.concatenate([vna[0:64], vnb[64:128]], 0)
    qdb = _b(q["qd"])
    o = jnp.concatenate([_nn(qdb[0:64], sab), _nn(qdb[64:128], sbb)], 0) + _nn(_b(q["qk"]), _b(vn))
    return sb, sn, vn, o


def _gdn_param_specs():
    return [pl.BlockSpec((CONV_WIDTH, GW3), lambda i: (0, 0)),
            pl.BlockSpec((8, 128), lambda i: (0, 0)),
            pl.BlockSpec((1, 128), lambda i: (0, 0))]


def _gdn_pvec(a_log, dt_bias):
    z = jnp.zeros((8, 128), F32)
    return z.at[0, 4:8].set(a_log).at[1, 4:8].set(dt_bias)


def _gdn_fwd(proj, cw, a_log, dt_bias, nw, *, T, name):
    S_ = proj.shape[0]
    nu = T // 128

    def body(x_ref, z_ref, g_ref, cw_ref, pv_ref, nw_ref, y_ref, st_ref, tm_ref, halo, state):
        i = pl.program_id(0)

        @pl.when(i == 0)
        def _():
            halo[...] = jnp.zeros_like(halo)
            state[...] = jnp.zeros_like(state)

        x = x_ref[...]
        ext = jnp.concatenate([halo[...], x], 0)
        halo[...] = x[T - 8:]
        _, _, qkv, beta, _, _, _, gc = _gdn_pre(ext, T, cw_ref, g_ref[...], pv_ref)
        items = [(dc, h) for dc in range(nu) for h in range(G_HEADS)]
        qs = _lockstep([_gdn_chunk(qkv, beta, gc, slice(dc * 128, (dc + 1) * 128), h) for dc, h in items])

        def head_chain(h):
            s = state[h]
            for dc in range(nu):
                rs = slice(dc * 128, (dc + 1) * 128)
                q = qs[dc * G_HEADS + h]
                sb, sn, _, o = yield from _gdn_scan(q, s)
                st_ref[2 * dc, h] = s
                st_ref[2 * dc + 1, h] = sb
                tm_ref[dc, h] = q["tm"]
                s = sn
                yield
                rn = lax.rsqrt(jnp.mean(o * o, 1, keepdims=True) + RMS_EPS)
                cs = slice(h * 128, (h + 1) * 128)
                y_ref[rs, cs] = (o * rn * nw_ref[...] * _silu(z_ref[rs, cs])).astype(BF16)
                yield
            state[h] = s

        _lockstep([head_chain(h) for h in range(G_HEADS)])

    return _pcall(
        body, name=name, grid=(S_ // T,),
        in_specs=[pl.BlockSpec((T, GW3), lambda i: (i, OFF_GQKV // GW3)),
                  pl.BlockSpec((T, 512), lambda i: (i, OFF_GZ // 512)),
                  pl.BlockSpec((T, 128), lambda i: (i, OFF_GBA // 128))] + _gdn_param_specs(),
        out_specs=[pl.BlockSpec((T, 512), lambda i: (i, 0)),
                   pl.BlockSpec((2 * nu, G_HEADS, 128, 128), lambda i: (i, 0, 0, 0)),
                   pl.BlockSpec((nu, G_HEADS, 128, 128), lambda i: (i, 0, 0, 0))],
        out_shape=[jax.ShapeDtypeStruct((S_, 512), BF16),
                   jax.ShapeDtypeStruct((S_ // 64, G_HEADS, 128, 128), F32),
                   jax.ShapeDtypeStruct((S_ // 128, G_HEADS, 128, 128), F32)],
        scratch_shapes=[pltpu.VMEM((8, GW3), F32), pltpu.VMEM((G_HEADS, 128, 128), F32)],
        compiler_params=_cp(("arbitrary",)),
    )(proj, proj, proj, cw, _gdn_pvec(a_log, dt_bias), nw.reshape(1, 128))


def _gdn_bwd(proj, states, tms, dymix, cw, a_log, dt_bias, nw, *, T, name):
    S_ = proj.shape[0]
    nt = S_ // T
    nu = T // 128
    t8 = T // 8

    def body(x_ref, xp_ref, z_ref, g_ref, st_ref, tm_ref, dy_ref, cw_ref, pv_ref, nw_ref,
             dx_ref, dz_ref, dg_ref, dcw_ref, dpv_ref, dnw_ref, dstate, dhalo, dqkv, dbg):
        i = pl.program_id(0)
        first_tile = (i == nt - 1)

        @pl.when(i == 0)
        def _():
            dstate[...] = jnp.zeros_like(dstate)
            dhalo[...] = jnp.zeros_like(dhalo)
            dcw_ref[...] = jnp.zeros_like(dcw_ref)
            dpv_ref[...] = jnp.zeros_like(dpv_ref)
            dnw_ref[...] = jnp.zeros_like(dnw_ref)

        keep = jnp.where(first_tile, 0.0, 1.0)
        ext = jnp.concatenate([xp_ref[...] * keep, x_ref[...]], 0)
        G = g_ref[...]
        taps, c, qkv, beta, sarg, nea, gdec, gc = _gdn_pre(ext, T, cw_ref, G, pv_ref)
        tril, strict, _ = _gdn_masks()
        rowi = lax.broadcasted_iota(jnp.int32, (128, 1), 0)
        lane = lax.broadcasted_iota(jnp.int32, (1, 128), 1)
        ones_b = jnp.ones((128, 128), BF16)
        nwv = nw_ref[...]
        items = [(dc, h) for dc in range(nu) for h in range(G_HEADS)]

        def recompute(dc, h):
            q = yield from _gdn_chunk(qkv, beta, gc, slice(dc * 128, (dc + 1) * 128), h, tm=tm_ref[dc, h])
            sa = st_ref[2 * dc, h]
            sb, _, vn, o = yield from _gdn_scan(q, sa)
            return q, sa, sb, vn, o

        fw = _lockstep([recompute(dc, h) for dc, h in items])
        chain_out = {}

        def head_chain(h):
            dS = dstate[h]
            for dc in reversed(range(nu)):
                rs = slice(dc * 128, (dc + 1) * 128)
                q, sa, sb, vn, o = fw[dc * G_HEADS + h]
                cs = slice(h * 128, (h + 1) * 128)
                zg = z_ref[rs, cs]
                dy = dy_ref[rs, cs]
                rn = lax.rsqrt(jnp.mean(o * o, 1, keepdims=True) + RMS_EPS)
                don = dy * _silu(zg)
                dz_ref[rs, cs] = (dy * (o * rn * nwv) * _dsilu(zg)).astype(BF16)
                dnw_ref[...] += jnp.sum(don * o * rn, 0, keepdims=True)
                tt = don * nwv
                do = rn * (tt - o * (rn * rn) * jnp.mean(tt * o, 1, keepdims=True))
                yield
                dob = _b(do)
                sab, sbb = _b(sa), _b(sb)
                vnb16 = _b(vn)
                dqk = jnp.where(tril, _nt(dob, vnb16), 0.0)
                dvn_o = _tn(_b(q["qk"]), dob)
                dSb16 = _b(dS)
                kdb = _b(q["kd"])
                wb = _b(q["w"])
                qdb = _b(q["qd"])
                yield
                dvn_b = dvn_o[64:128] + _nn(kdb[64:128], dSb16)
                dkd_b = _nt(vnb16[64:128], dSb16)
                dgl_b = jnp.sum(jnp.sum(dS * sb, 1, keepdims=True), 0, keepdims=True)
                yield
                dvn_b16 = _b(dvn_b)
                dw_b = -_nt(dvn_b16, sbb)
                dqd_b = _nt(dob[64:128], sbb)
                dSm = q["glb"] * dS + _tn(qdb[64:128], dob[64:128]) - _tn(wb[64:128], dvn_b16)
                yield
                dSm16 = _b(dSm)
                dvn_a = dvn_o[0:64] + _nn(kdb[0:64], dSm16)
                dkd_a = _nt(vnb16[0:64], dSm16)
                dgl_a = jnp.sum(jnp.sum(dSm * sa, 1, keepdims=True), 0, keepdims=True)
                yield
                dvn_a16 = _b(dvn_a)
                dw_a = -_nt(dvn_a16, sab)
                dqd_a = _nt(dob[0:64], sab)
                dS = q["gla"] * dSm + _tn(qdb[0:64], dob[0:64]) - _tn(wb[0:64], dvn_a16)
                chain_out[dc, h] = (dqk, jnp.concatenate([dvn_a, dvn_b], 0), jnp.concatenate([dw_a, dw_b], 0),
                                    jnp.concatenate([dkd_a, dkd_b], 0), jnp.concatenate([dqd_a, dqd_b], 0),
                                    dgl_a, dgl_b)
                yield
            dstate[h] = dS

        _lockstep([head_chain(h) for h in range(G_HEADS)])

        def local(dc, h):
            rs = slice(dc * 128, (dc + 1) * 128)
            q = fw[dc * G_HEADS + h][0]
            dqk, du, dw, dkd, dqd, dgl_a, dgl_b = chain_out[dc, h]
            if True:
                dvb = _dot3(_tn, q["tm"], du)
                dkbe = _dot3(_tn, q["tm"], dw)
                yield
                dM = jnp.where(strict, -(_nt(_b(dvb), _b(q["u"])) + _nt(_b(dkbe), _b(q["w"]))), 0.0)
                yield
                D = q["D"]
                dA = dM * D
                dB = dqk * D
                dDD = (dM * q["A"] + dqk * q["Bm"]) * D
                dh_, dm_, dl_ = _split3(dDD)
                colsum = _tn(dh_, ones_b) + (_tn(dm_, ones_b) + _tn(dl_, ones_b))
                dgc = jnp.sum(dDD, 1, keepdims=True) - _lane_col(colsum, 0)
                yield
                dA16, dB16 = _b(dA), _b(dB)
                knb, kbb, qnb = _b(q["kn"]), _b(q["kb"]), _b(q["qn"])
                eg, ed = q["eg"], q["ed"]
                dkb = _nn(dA16, knb) + dkbe * eg
                dkn = _tn(dA16, kbb) + _tn(dB16, qnb) + dkd * ed + dkb * q["bcol"]
                dqn = _nn(dB16, knb) + dqd * eg
                yield
                deg = jnp.sum(dkbe * q["kb"], 1, keepdims=True) + jnp.sum(dqd * q["qn"], 1, keepdims=True)
                ded = jnp.sum(dkd * q["kn"], 1, keepdims=True) * ed
                dgc = dgc + deg * eg - ded
                tail_a = jnp.sum(jnp.where(rowi < 64, ded, 0.0), 0, keepdims=True) + dgl_a * q["gla"]
                tail_b = jnp.sum(jnp.where(rowi >= 64, ded, 0.0), 0, keepdims=True) + dgl_b * q["glb"]
                dgc = dgc + jnp.where(rowi == 63, tail_a, 0.0) + jnp.where(rowi == 127, tail_b, 0.0)
                dbeta = jnp.sum(dkb * q["kn"], 1, keepdims=True) + jnp.sum(dvb * q["vh"], 1, keepdims=True)
                bcol = q["bcol"]
                blk = jnp.where(lane == h, dbeta * bcol * (1.0 - bcol), 0.0) + jnp.where(lane == 4 + h, dgc, 0.0)
                yield
                sc = G_HEAD_DIM ** -0.5
                rq, rk, qh, kh = q["rq"], q["rk"], q["qh"], q["kh"]
                dqh = sc * (dqn * rq - qh * (rq * rq * rq) * jnp.sum(dqn * qh, 1, keepdims=True))
                dkh = dkn * rk - kh * (rk * rk * rk) * jnp.sum(dkn * kh, 1, keepdims=True)
                dqkv[rs, h * 128:(h + 1) * 128] = dqh
                dqkv[rs, 512 + h * 128:512 + (h + 1) * 128] = dkh
                dqkv[rs, 1024 + h * 128:1024 + (h + 1) * 128] = dvb * bcol
            return blk

        blks = _lockstep([local(dc, h) for dc, h in items])
        for dc in range(nu):
            dbg[dc * 128:(dc + 1) * 128, :] = functools.reduce(
                lambda a, b: a + b, [blks[dc * G_HEADS + h] for h in range(G_HEADS)])
        ri = lax.broadcasted_iota(jnp.int32, (T, T), 0)
        cj = lax.broadcasted_iota(jnp.int32, (T, T), 1)
        utri = jnp.where((ri <= cj) & ((ri >> 6) == (cj >> 6)), 1.0, 0.0).astype(BF16)
        dbgv = dbg[...]
        dgd = _dot_exact_lhs(_nn, utri, dbgv)
        is_g = (lane >= 4) & (lane < 8)
        dga = jnp.where(is_g, dgd * nea * _sigmoid(sarg), 0.0)
        dg_ref[...] = jnp.where(lane < 4, dbgv, dga).astype(BF16)
        dpv_ref[0:1, :] += jnp.sum(jnp.where(is_g, dgd * gdec, 0.0), 0, keepdims=True)
        dpv_ref[1:2, :] += jnp.sum(dga, 0, keepdims=True)
        dc_ = dqkv[...] * _dsilu(c)
        for k in range(CONV_WIDTH):
            dcw_ref[k:k + 1, :] += jnp.sum(dc_ * taps[k], 0, keepdims=True)
        ext2 = jnp.concatenate([dc_, dhalo[...]], 0)
        tt2 = _conv_taps_t(ext2, T)
        dx_ref[...] = sum(cw_ref[k:k + 1, :] * tt2[k] for k in range(CONV_WIDTH)).astype(BF16)
        dhalo[...] = dc_[0:8]

    def rev(i):
        return nt - 1 - i

    def prev8(i):
        return jnp.maximum(rev(i) * t8 - 1, 0)

    return _pcall(
        body, name=name, grid=(nt,),
        in_specs=[pl.BlockSpec((T, GW3), lambda i: (rev(i), OFF_GQKV // GW3)),
                  pl.BlockSpec((8, GW3), lambda i: (prev8(i), OFF_GQKV // GW3)),
                  pl.BlockSpec((T, 512), lambda i: (rev(i), OFF_GZ // 512)),
                  pl.BlockSpec((T, 128), lambda i: (rev(i), OFF_GBA // 128)),
                  pl.BlockSpec((2 * nu, G_HEADS, 128, 128), lambda i: (rev(i), 0, 0, 0)),
                  pl.BlockSpec((nu, G_HEADS, 128, 128), lambda i: (rev(i), 0, 0, 0)),
                  pl.BlockSpec((T, 512), lambda i: (rev(i), MIX_G // 512))] + _gdn_param_specs(),
        out_specs=[pl.BlockSpec((T, GW3), lambda i: (rev(i), 0)),
                   pl.BlockSpec((T, 512), lambda i: (rev(i), 0)),
                   pl.BlockSpec((T, 128), lambda i: (rev(i), 0)),
                   pl.BlockSpec((CONV_WIDTH, GW3), lambda i: (0, 0)),
                   pl.BlockSpec((8, 128), lambda i: (0, 0)),
                   pl.BlockSpec((1, 128), lambda i: (0, 0))],
        out_shape=[jax.ShapeDtypeStruct((S_, GW3), BF16), jax.ShapeDtypeStruct((S_, 512), BF16),
                   jax.ShapeDtypeStruct((S_, 128), BF16), jax.ShapeDtypeStruct((CONV_WIDTH, GW3), F32),
                   jax.ShapeDtypeStruct((8, 128), F32), jax.ShapeDtypeStruct((1, 128), F32)],
        scratch_shapes=[pltpu.VMEM((G_HEADS, 128, 128), F32), pltpu.VMEM((8, GW3), F32),
                        pltpu.VMEM((T, GW3), F32), pltpu.VMEM((T, 128), F32)],
        compiler_params=_cp(("arbitrary",)),
    )(proj, proj, proj, proj, states, tms, dymix, cw, _gdn_pvec(a_log, dt_bias), nw.reshape(1, 128))


def _add_mine_windows(a0, a1, b, nsh, width, *, out_dtype, name):
    R_, C = b.shape
    nb = width // 128
    assert (3 * nsh) // 128 + nb <= C // 128

    def body(a0_ref, a1_ref, b_ref, o_ref):
        mine = jnp.where(lax.axis_index("c") == 0, a0_ref[...], a1_ref[...])
        o_ref[...] = (mine + b_ref[...]).astype(o_ref.dtype)

    spec = pl.BlockSpec((R_, 128), lambda t, j: (0, (nsh * t) // 128 + j))
    return _pcall(body, name=name, grid=(4, nb), in_specs=[spec] * 3,
                  out_specs=pl.BlockSpec((None, R_, 128), lambda t, j: (t, 0, j)),
                  out_shape=jax.ShapeDtypeStruct((4, R_, width), out_dtype),
                  compiler_params=_cp(("parallel", "parallel")))(a0, a1, b)


def _add_mine(a0, a1, b, *, out_dtype, tr, name):
    R_, C = b.shape

    def body(a0_ref, a1_ref, b_ref, o_ref):
        mine = jnp.where(lax.axis_index("c") == 0, a0_ref[...], a1_ref[...])
        o_ref[...] = (mine + b_ref[...]).astype(o_ref.dtype)

    spec = pl.BlockSpec((tr, C), lambda i: (i, 0))
    return _pcall(body, name=name, grid=(R_ // tr,), in_specs=[spec] * 3, out_specs=spec,
                  out_shape=jax.ShapeDtypeStruct((R_, C), out_dtype), compiler_params=_cp(("parallel",)))(a0, a1, b)


def _sum4(a, mine, *, tr, name):
    _, R_, C = a.shape

    def body(a_ref, m_ref, o_ref):
        s = 2 * lax.axis_index("x") + lax.axis_index("y")
        mv = m_ref[...].astype(F32)
        p = [jnp.where(s == t, mv, a_ref[t].astype(F32)) for t in range(4)]
        o_ref[...] = ((p[0] + p[1]) + p[2]) + p[3]

    return _pcall(body, name=name, grid=(R_ // tr,),
                  in_specs=[pl.BlockSpec((4, tr, C), lambda i: (0, i, 0)), pl.BlockSpec((tr, C), lambda i: (i, 0))],
                  out_specs=pl.BlockSpec((tr, C), lambda i: (i, 0)),
                  out_shape=jax.ShapeDtypeStruct((R_, C), F32), compiler_params=_cp(("parallel",)))(a, mine)


def _adamw_refs(w_ref, g_ref, m_ref, v_ref, d_ref, mo_ref, vo_ref):
    c1 = 1.0 / (1.0 - ADAM_B1 ** ADAM_STEP)
    c2 = 1.0 / (1.0 - ADAM_B2 ** ADAM_STEP)
    gg = g_ref[...]
    mn = ADAM_B1 * m_ref[...] + (1.0 - ADAM_B1) * gg
    vn = ADAM_B2 * v_ref[...] + (1.0 - ADAM_B2) * (gg * gg)
    mo_ref[...] = mn
    vo_ref[...] = vn
    d_ref[...] = -ADAM_LR * ((mn * c1) / (jnp.sqrt(vn * c2) + ADAM_EPS) + ADAM_WD * w_ref[...])


def _adamw_many(ws, gs, ms, vs, *, name):
    n = len(ws)

    def body(*refs):
        for k in range(n):
            _adamw_refs(*[refs[q * n + k] for q in range(7)])

    vm = pl.BlockSpec(memory_space=pltpu.VMEM)
    shp = [jax.ShapeDtypeStruct(w.shape, F32) for w in ws]
    outs = _pcall(body, name=name, in_specs=[vm] * (4 * n), out_specs=[vm] * (3 * n), out_shape=shp * 3,
                  compiler_params=pltpu.CompilerParams(vmem_limit_bytes=VMEM_LIMIT))(*ws, *gs, *ms, *vs)
    return outs[:n], outs[n:2 * n], outs[2 * n:]


def _adamw(w, g, m, v, *, tr, name):
    L, R_, C = w.shape
    body = functools.partial(_adamw_refs)

    spec = pl.BlockSpec((None, tr, C), lambda l, i: (l, i, 0))
    shp = jax.ShapeDtypeStruct((L, R_, C), F32)
    return _pcall(body, name=name, grid=(L, R_ // tr), in_specs=[spec] * 4, out_specs=[spec] * 3,
                  out_shape=[shp] * 3, compiler_params=_cp(("parallel", "parallel")))(w, g, m, v)


HBM_SPEC = pl.BlockSpec(memory_space=pltpu.HBM)


def _place():
    x, y, c = lax.axis_index("x"), lax.axis_index("y"), lax.axis_index("c")
    chips = [(1 - x, y), (x, 1 - y), (1 - x, 1 - y)]
    return x, y, c, 2 * x + y, chips, [2 * cx + cy for cx, cy in chips], (x, y, 1 - c)


def _remote(src, dst, ssem, rsem, dev):
    return pltpu.make_async_remote_copy(src_ref=src, dst_ref=dst, send_sem=ssem, recv_sem=rsem,
                                        device_id=dev, device_id_type=MESH)


def _row_half(ref, lead, hc):
    hl = ref.shape[-2] // 2
    return ref.at[lead, pl.ds(hc * hl, hl), :]


def _gather_side(win, wout, layer):
    def copies(ins, outs, ssem, rsem):
        x, y, c, s, chips, sid, sib = _place()
        cps = []
        for j, chip in enumerate(chips):
            dev = (*chip, c)
            cps.append(_remote(_row_half(ins[0], layer, c), _row_half(outs[0], s, c), ssem.at[j], rsem.at[j], dev))
            cps.append(_remote(_row_half(ins[1], layer, c), _row_half(outs[1], s, c), ssem.at[3 + j], rsem.at[3 + j],
                               dev))
        return cps, c, sid, sib

    def start(ins, outs, ssem, rsem):
        for cp in copies(ins, outs, ssem, rsem)[0]:
            cp.start()

    def finish(ins, outs, ssem, rsem):
        cps, c, sid, sib = copies(ins, outs, ssem, rsem)
        for j in range(3):
            for k in range(2):
                got = _row_half(outs[k], sid[j], c)
                _remote(got, got, ssem.at[3 * k + j], rsem.at[3 * k + j], sib).wait_recv()
        for cp in cps:
            cp.wait_send()

    shapes = [jax.ShapeDtypeStruct((4,) + win.shape[1:], win.dtype), jax.ShapeDtypeStruct((4,) + wout.shape[1:], wout.dtype)]
    return _Side([win, wout], shapes, 6, start, finish)


def _gather_join(gin, gout, name):
    def body(gin_in, gout_in, gin_ref, gout_ref, ssem, rsem):
        x, y, c, s, chips, sid, sib = _place()
        cps = []
        for j in range(3):
            for k, ref in enumerate((gin_ref, gout_ref)):
                mine = _row_half(ref, sid[j], c)
                cps.append(_remote(mine, mine, ssem.at[3 * k + j], rsem.at[3 * k + j], sib))
        for cp in cps:
            cp.start()
        for j in range(3):
            for k, ref in enumerate((gin_ref, gout_ref)):
                other = _row_half(ref, sid[j], 1 - c)
                _remote(other, other, ssem.at[3 * k + j], rsem.at[3 * k + j], sib).wait_recv()
        for cp in cps:
            cp.wait_send()

    return _pcall(
        body, name=name, in_specs=[HBM_SPEC] * 2, out_specs=[HBM_SPEC] * 2,
        out_shape=[jax.ShapeDtypeStruct(gin.shape, gin.dtype), jax.ShapeDtypeStruct(gout.shape, gout.dtype)],
        input_output_aliases={0: 0, 1: 1},
        scratch_shapes=[pltpu.SemaphoreType.DMA((6,)), pltpu.SemaphoreType.DMA((6,))],
    )(gin, gout)


def _gather_layer0(win, wout, conv):
    def body(win_ref, wout_ref, cv_ref, gin_ref, gout_ref, gcv_ref, ssem, rsem):
        x, y, c, s, chips, sid, sib = _place()

        def in_half(slot, hc):
            return _row_half(gin_ref, slot, hc)

        def out_half(slot, hc):
            return _row_half(gout_ref, slot, hc)

        sends = []
        for j, chip in enumerate(chips):
            dev = (*chip, c)
            sends.append(_remote(_row_half(win_ref, 0, c), in_half(s, c), ssem.at[j], rsem.at[j], dev))
            sends.append(_remote(_row_half(wout_ref, 0, c), out_half(s, c), ssem.at[3 + j], rsem.at[3 + j], dev))
            sends.append(_remote(cv_ref, gcv_ref.at[s], ssem.at[6 + j], rsem.at[6 + j], dev))
        for cp in sends:
            cp.start()
        for j in range(3):
            _remote(in_half(sid[j], c), in_half(sid[j], c), ssem.at[j], rsem.at[j], sib).wait_recv()
            f = _remote(in_half(sid[j], c), in_half(sid[j], c), ssem.at[9 + j], rsem.at[9 + j], sib)
            f.start()
            sends.append(f)
            _remote(out_half(sid[j], c), out_half(sid[j], c), ssem.at[3 + j], rsem.at[3 + j], sib).wait_recv()
            f = _remote(out_half(sid[j], c), out_half(sid[j], c), ssem.at[12 + j], rsem.at[12 + j], sib)
            f.start()
            sends.append(f)
        for j in range(3):
            _remote(in_half(sid[j], 1 - c), in_half(sid[j], 1 - c), ssem.at[9 + j], rsem.at[9 + j], sib).wait_recv()
            _remote(out_half(sid[j], 1 - c), out_half(sid[j], 1 - c), ssem.at[12 + j], rsem.at[12 + j], sib).wait_recv()
            _remote(gcv_ref.at[sid[j]], gcv_ref.at[sid[j]], ssem.at[6 + j], rsem.at[6 + j], sib).wait_recv()
        for cp in sends:
            cp.wait_send()

    return _pcall(
        body, name="gather_layer0",
        in_specs=[HBM_SPEC] * 3, out_specs=[HBM_SPEC] * 3,
        out_shape=[jax.ShapeDtypeStruct((4,) + win.shape[1:], win.dtype),
                   jax.ShapeDtypeStruct((4,) + wout.shape[1:], wout.dtype),
                   jax.ShapeDtypeStruct((4,) + conv.shape, conv.dtype)],
        scratch_shapes=[pltpu.SemaphoreType.DMA((15,)), pltpu.SemaphoreType.DMA((15,))],
    )(win, wout, conv)


def _swap_sibling(pairs, stacked, name):
    n2, ns = len(pairs), len(stacked)
    n = n2 + ns

    def body(*refs):
        src2 = [(refs[2 * k], refs[2 * k + 1]) for k in range(n2)]
        srcs = refs[2 * n2:2 * n2 + ns]
        dst = refs[2 * n2 + ns:2 * n2 + ns + n]
        ssem, rsem = refs[-2], refs[-1]
        x, y, c, s, chips, sid, sib = _place()

        def exchange(give):
            cps = [_remote(src2[k][give], dst[k], ssem.at[k], rsem.at[k], sib) for k in range(n2)]
            for cp in cps:
                cp.start()
            for cp in cps:
                cp.wait()

        small = [_remote(srcs[k].at[1 - c], dst[n2 + k], ssem.at[n2 + k], rsem.at[n2 + k], sib) for k in range(ns)]
        for cp in small:
            cp.start()

        @pl.when(c == 0)
        def _():
            exchange(1)

        @pl.when(c == 1)
        def _():
            exchange(0)

        for cp in small:
            cp.wait()

    flat = [a for p in pairs for a in p] + list(stacked)
    return _pcall(
        body, name=name, in_specs=[HBM_SPEC] * len(flat), out_specs=[HBM_SPEC] * n,
        out_shape=[jax.ShapeDtypeStruct(p[0].shape, p[0].dtype) for p in pairs]
        + [jax.ShapeDtypeStruct(a.shape[1:], a.dtype) for a in stacked],
        scratch_shapes=[pltpu.SemaphoreType.DMA((n,)), pltpu.SemaphoreType.DMA((n,))],
    )(*flat)


def _scatter_chips(arrs, per_target, name):
    n = len(arrs)

    def body(*refs):
        src, dst = refs[:n], refs[n:2 * n]
        ssem, rsem = refs[2 * n], refs[2 * n + 1]
        x, y, c, s, chips, sid, sib = _place()
        sends = []
        for k in range(n):
            for j, chip in enumerate(chips):
                piece = src[k].at[sid[j]] if per_target[k] else src[k]
                sends.append(_remote(piece, dst[k].at[s], ssem.at[3 * k + j], rsem.at[3 * k + j], (*chip, c)))
        for cp in sends:
            cp.start()
        for k in range(n):
            for j in range(3):
                _remote(dst[k].at[sid[j]], dst[k].at[sid[j]], ssem.at[3 * k + j], rsem.at[3 * k + j], sib).wait_recv()
        for cp in sends:
            cp.wait_send()

    outs = [jax.ShapeDtypeStruct(a.shape if pt else (4,) + a.shape, a.dtype) for a, pt in zip(arrs, per_target)]
    return _pcall(
        body, name=name, in_specs=[HBM_SPEC] * n, out_specs=[HBM_SPEC] * n, out_shape=outs,
        scratch_shapes=[pltpu.SemaphoreType.DMA((3 * n,)), pltpu.SemaphoreType.DMA((3 * n,))],
    )(*arrs)


def _swap_whole(arrs, name):
    n = len(arrs)

    def body(*refs):
        src, dst, ssem, rsem = refs[:n], refs[n:2 * n], refs[2 * n], refs[2 * n + 1]
        *_, sib = _place()
        cps = [_remote(src[k], dst[k], ssem.at[k], rsem.at[k], sib) for k in range(n)]
        for cp in cps:
            cp.start()
        for cp in cps:
            cp.wait()

    return _pcall(
        body, name=name, in_specs=[HBM_SPEC] * n, out_specs=[HBM_SPEC] * n,
        out_shape=[jax.ShapeDtypeStruct(a.shape, a.dtype) for a in arrs],
        scratch_shapes=[pltpu.SemaphoreType.DMA((n,)), pltpu.SemaphoreType.DMA((n,))],
    )(*arrs)


def _perm_cols(w):
    parts = [w[..., int(_ORIG_OFF[oi]):int(_ORIG_OFF[oi]) + IN_SIZES[oi]] for oi, _ in _PIECES]
    parts.append(jnp.zeros(w.shape[:-1] + (NP - N_IN,), w.dtype))
    return jnp.concatenate(parts, -1)


def _perm_rows(w):
    return jnp.concatenate([w[..., 512:1536, :], w[..., 0:512, :], w[..., 1536:2048, :]], -2)


_SMALL = ("sinks", "r_conv_b", "r_wa", "r_ba", "r_wx", "r_bx", "r_lam", "g_a_log", "g_dt_bias", "g_norm_w",
          "ln_g", "ln_b", "r_conv_w", "g_conv_w")
_PACK_ROWS = 16


def _piece_rows(n):
    return -(-n // (128 * _PACK_ROWS)) * _PACK_ROWS


def _pack(arrs):
    parts = []
    for a in arrs:
        n = int(np.prod(a.shape))
        rows = _piece_rows(n)
        if n % 128 == 0:
            blk = a.reshape(n // 128, 128)
        else:
            blk = jnp.pad(a.reshape(1, n), ((0, 0), (0, (-n) % 128))).reshape(-1, 128)
        if blk.shape[0] < rows:
            blk = jnp.pad(blk, ((0, rows - blk.shape[0]), (0, 0)))
        parts.append(blk)
    return jnp.concatenate(parts, 0)


def _unpack(packed, shapes):
    out = []
    r = 0
    for shp in shapes:
        n = int(np.prod(shp))
        if n % 128 == 0:
            out.append(packed[r:r + n // 128].reshape(shp))
        else:
            nr = -(-n // 128)
            out.append(packed[r:r + nr].reshape(1, nr * 128)[:, :n].reshape(shp))
        r += _piece_rows(n)
    return out


def _tile(n, t):
    return min(n, t)


def _layer_fwd(l, x, xb, wb, wob, rope_c, rope_s, p, side=None):
    S_ = x.shape[0]
    proj = _matmul(xb, wb, ta=False, tb=False, tm=_tile(S_, 1024), tn=NP // 4, tk=wb.shape[0], out_dtype=F32,
                   name=f"in_proj_{l}", side=side)
    side_out = None
    if side:
        proj, side_out = proj
    ya = _attn_fwd(proj, rope_c, rope_s, p["sinks"], T=_tile(S_, 512), name=f"attn_fwd_{l}")
    h, yr = _rglru_fwd(proj, p["r_conv_w"], p["r_conv_b"], p["r_wa"], p["r_ba"], p["r_wx"], p["r_bx"], p["r_lam"],
                       T=_tile(S_, 256), name=f"rglru_fwd_{l}")
    yg, st, tms = _gdn_fwd(proj, p["g_conv_w"], p["g_a_log"], p["g_dt_bias"], p["g_norm_w"],
                           T=_tile(S_, 256), name=f"gdn_fwd_{l}")
    ymix = jnp.concatenate([yr, ya, yg], 1)
    z = _outproj(ymix, wob, x, tm=_tile(S_, 256), name=f"out_proj_{l}")
    return dict(proj=proj, h=h, st=st, tms=tms, ymix=ymix, z=z, side=side_out)


def _layer_bwd(l, sv, x_b, dz, dzb, wo, wob, rope_c, rope_s, p):
    S_, D = dz.shape
    proj = sv["proj"]
    dymix = _matmul(dzb, wob, ta=False, tb=True, tm=_tile(S_, 1024), tn=512, tk=D, out_dtype=F32,
                    name=f"dmix_{l}")
    dwo = _matmul(sv["ymix"], dzb, ta=True, tb=False, tm=512, tn=_tile(D, 2048), tk=_tile(S_, 1024),
                  out_dtype=F32, name=f"dw_out_{l}",
                  out_blocks=((MIX_WIDTH, D), (512, _tile(D, 2048)),
                              lambda i, j: (jnp.where(i == 3, 3, (i + 1) % 3), j)))
    dq, daz, dk, dv, dkt, dvt, dsk = _attn_bwd(proj, rope_c, rope_s, p["sinks"], dymix, T=_tile(S_, 512),
                                               name=f"attn_bwd_{l}")
    (drx, drz, dcw_r, dcb_r, dwa, dba, dwx, dbx, dlam) = _rglru_bwd(
        proj, sv["h"], dymix, p["r_conv_w"], p["r_conv_b"], p["r_wa"], p["r_ba"], p["r_wx"], p["r_bx"], p["r_lam"],
        T=_tile(S_, 256), name=f"rglru_bwd_{l}")
    dqkv, dgz, dgba, dcw_g, dpv, dnw = _gdn_bwd(proj, sv["st"], sv["tms"], dymix, p["g_conv_w"], p["g_a_log"],
                                                p["g_dt_bias"], p["g_norm_w"], T=_tile(S_, 256), name=f"gdn_bwd_{l}")
    dproj = jnp.concatenate([dq, jnp.concatenate([dk[128:], dkt], 0), jnp.concatenate([dv[128:], dvt], 0), daz,
                             drx, drz, dqkv, dgz, dgba, jnp.zeros((S_, NP - N_IN - 120), BF16)], 1)
    dx = _matmul(dproj, wo, ta=False, tb=True, tm=_tile(S_, 1024), tn=_tile(D, 1024), tk=NP // 2, out_dtype=F32,
                 name=f"dx_{l}", extra=dz, alpha=DEEPNORM_ALPHA)
    dwin = _matmul(x_b, dproj, ta=True, tb=False, tm=_tile(D, 1024), tn=NP // 4, tk=_tile(S_, 1024), out_dtype=F32,
                   name=f"dw_in_{l}")
    small = dict(sinks=dsk[:, 0], r_conv_b=dcb_r[0], r_wa=dwa, r_ba=dba[0], r_wx=dwx, r_bx=dbx[0], r_lam=dlam[0],
                 g_a_log=dpv[0, 4:8], g_dt_bias=dpv[1, 4:8], g_norm_w=dnw[0], r_conv_w=dcw_r, g_conv_w=dcw_g)
    return dx, dwin, dwo, small


def kernel(x, w_in, sinks, r_conv_w, r_conv_b, r_wa, r_ba, r_wx, r_bx, r_lam, g_conv_w, g_a_log, g_dt_bias, g_norm_w, w_out, ln_g, ln_b, loss_target, m_w_in, m_sinks, m_r_conv_w, m_r_conv_b, m_r_wa, m_r_ba, m_r_wx, m_r_bx, m_r_lam, m_g_conv_w, m_g_a_log, m_g_dt_bias, m_g_norm_w, m_w_out, m_ln_g, m_ln_b, v_w_in, v_sinks, v_r_conv_w, v_r_conv_b, v_r_wa, v_r_ba, v_r_wx, v_r_bx, v_r_lam, v_g_conv_w, v_g_a_log, v_g_dt_bias, v_g_norm_w, v_w_out, v_ln_g, v_ln_b):
    S_, D = x.shape[1], x.shape[2]
    nsh = w_in.shape[2]
    rsh = w_out.shape[1]
    cx, cy, cc = lax.axis_index("x"), lax.axis_index("y"), lax.axis_index("c")
    chip = 2 * cx + cy
    rcw_n, gcw_n = r_conv_w.shape[2], g_conv_w.shape[2]

    conv_pack = jnp.concatenate([r_conv_w, g_conv_w], 2)
    w_in_b, w_out_b = w_in.astype(BF16), w_out.astype(BF16)
    g_in0, g_out0, g_conv = _gather_layer0(w_in_b, w_out_b, conv_pack)

    def shards(own, got):
        return [jnp.where(chip == t, own, got[t]) for t in range(4)]

    def layer_weights(l, g_in, g_out):
        w_full = jnp.concatenate(shards(w_in_b[l], g_in), 1)
        return (_perm_cols(w_full),
                jnp.concatenate([w_full, jnp.zeros((D, NP - N_IN), BF16)], 1),
                _perm_rows(jnp.concatenate(shards(w_out_b[l], g_out), 0)))

    rcw = jnp.concatenate(shards(r_conv_w, g_conv[:, :, :, :rcw_n]), 2)
    gcw = jnp.concatenate(shards(g_conv_w, g_conv[:, :, :, rcw_n:]), 2)

    pos = jnp.arange(S_, dtype=F32)[:, None]
    inv = 1.0 / (ROPE_THETA ** (jnp.arange(0, A_HEAD_DIM, 2, dtype=F32) / A_HEAD_DIM))
    ang = pos * inv[None, :]
    cos, sin = jnp.cos(ang), jnp.sin(ang)
    rope_c = jnp.concatenate([cos, cos, cos, cos], 1)
    rope_s = jnp.concatenate([-sin, sin, -sin, sin], 1)

    def params(l):
        return dict(sinks=sinks[l], r_conv_w=rcw[l], r_conv_b=r_conv_b[l], r_wa=r_wa[l], r_ba=r_ba[l],
                    r_wx=r_wx[l], r_bx=r_bx[l], r_lam=r_lam[l], g_conv_w=gcw[l], g_a_log=g_a_log[l],
                    g_dt_bias=g_dt_bias[l], g_norm_w=g_norm_w[l])

    xs, xbs, saved = [x[0]], [x[0].astype(BF16)], []
    wb, wo, wob = [None] * DEPTH, [None] * DEPTH, [None] * DEPTH
    wb[0], wo[0], wob[0] = layer_weights(0, g_in0, g_out0)
    for l in range(DEPTH):
        nxt = _gather_side(w_in_b, w_out_b, l + 1) if l + 1 < DEPTH else None
        sv = _layer_fwd(l, xs[l], xbs[l], wb[l], wob[l], rope_c, rope_s, params(l), side=nxt)
        if nxt:
            wb[l + 1], wo[l + 1], wob[l + 1] = layer_weights(l + 1, *_gather_join(*sv["side"], f"gather_join_{l + 1}"))
        saved.append(sv)
        if l + 1 < DEPTH:
            xn, xnb = _ln_fwd(sv["z"], ln_g[l], ln_b[l], tm=_tile(S_, 256), name=f"ln_fwd_{l}")
            xs.append(xn)
            xbs.append(xnb)

    tm_ln = _tile(S_, 256)
    dz, dzb, dg_l, db_l, loss_part = _ln_bwd(saved[-1]["z"], ln_g[-1], ln_b[-1], loss_target[0], from_target=True,
                                             tm=tm_ln, name=f"ln_bwd_{DEPTH - 1}")
    dwin, dwo, small = [None] * DEPTH, [None] * DEPTH, [None] * DEPTH
    dlng, dlnb = [None] * DEPTH, [None] * DEPTH
    for l in reversed(range(DEPTH)):
        dlng[l], dlnb[l] = dg_l[0], db_l[0]
        dx, dwin[l], dwo[l], small[l] = _layer_bwd(l, saved[l], xbs[l], dz, dzb, wo[l], wob[l], rope_c, rope_s,
                                                   params(l))
        if l > 0:
            dz, dzb, dg_l, db_l, _ = _ln_bwd(saved[l - 1]["z"], ln_g[l - 1], ln_b[l - 1], dx, from_target=False,
                                             tm=tm_ln, name=f"ln_bwd_{l - 1}")
    grad_x = dx[None]
    loss = lax.psum(loss_part[0, 0], ("x", "y", "c"))

    sm = {k: jnp.stack([small[l][k] for l in range(DEPTH)]) for k in small[0]}
    sm["ln_g"], sm["ln_b"] = jnp.stack(dlng), jnp.stack(dlnb)
    names = list(_SMALL)
    gs = _pack([sm[n] for n in names])
    gs2 = gs.reshape(2, gs.shape[0] // 2, 128)
    in_got, out_got, s_got = _swap_sibling([(dwin[0], dwin[1]), (dwo[0], dwo[1])], [gs2], "reduce_pair")
    wcov = (-(-nsh // 128) + 1) * 128
    in_cp = _add_mine_windows(dwin[0], dwin[1], in_got, nsh, wcov, out_dtype=BF16, name="pair_sum_w_in")
    out_cp = _add_mine(dwo[0], dwo[1], out_got, out_dtype=BF16, tr=256, name="pair_sum_w_out").reshape(4, rsh, D)
    s_cp = _add_mine(gs2[0], gs2[1], s_got, out_dtype=F32, tr=gs2.shape[1], name="pair_sum_small")
    in_all, out_all, s_all = _scatter_chips([in_cp, out_cp, s_cp], [True, True, False], "reduce_chips")

    def own(a):
        return lax.dynamic_index_in_dim(a, chip, 0, keepdims=False)

    in_sum = _sum4(in_all, own(in_cp), tr=256, name="chip_sum_w_in")
    out_sum = _sum4(out_all, own(out_cp), tr=256, name="chip_sum_w_out")
    s_sum = _sum4(s_all, s_cp, tr=s_cp.shape[0], name="chip_sum_small")
    in_oth, out_oth, s_oth = _swap_whole([in_sum, out_sum, s_sum], "reduce_join")

    def both(mine, other):
        return jnp.where(cc == 0, jnp.stack([mine, other]), jnp.stack([other, mine]))

    g_w_in = lax.dynamic_slice_in_dim(both(in_sum, in_oth), (nsh * chip) % 128, nsh, 2)
    g_w_out = both(out_sum, out_oth)
    g_small = both(s_sum, s_oth).reshape(gs.shape)

    gsm = dict(zip(names, _unpack(g_small, [sm[n].shape for n in names])))
    gsm["r_conv_w"] = lax.dynamic_slice_in_dim(gsm["r_conv_w"], chip * rcw_n, rcw_n, 2)
    gsm["g_conv_w"] = lax.dynamic_slice_in_dim(gsm["g_conv_w"], chip * gcw_n, gcw_n, 2)
    wts = dict(sinks=sinks, r_conv_w=r_conv_w, r_conv_b=r_conv_b, r_wa=r_wa, r_ba=r_ba, r_wx=r_wx, r_bx=r_bx,
               r_lam=r_lam, g_conv_w=g_conv_w, g_a_log=g_a_log, g_dt_bias=g_dt_bias, g_norm_w=g_norm_w,
               ln_g=ln_g, ln_b=ln_b)
    mom = dict(sinks=m_sinks, r_conv_w=m_r_conv_w, r_conv_b=m_r_conv_b, r_wa=m_r_wa, r_ba=m_r_ba, r_wx=m_r_wx,
               r_bx=m_r_bx, r_lam=m_r_lam, g_conv_w=m_g_conv_w, g_a_log=m_g_a_log, g_dt_bias=m_g_dt_bias,
               g_norm_w=m_g_norm_w, ln_g=m_ln_g, ln_b=m_ln_b)
    vel = dict(sinks=v_sinks, r_conv_w=v_r_conv_w, r_conv_b=v_r_conv_b, r_wa=v_r_wa, r_ba=v_r_ba, r_wx=v_r_wx,
               r_bx=v_r_bx, r_lam=v_r_lam, g_conv_w=v_g_conv_w, g_a_log=v_g_a_log, g_dt_bias=v_g_dt_bias,
               g_norm_w=v_g_norm_w, ln_g=v_ln_g, ln_b=v_ln_b)
    d_s, m_s, v_s = _adamw_many(*[[d[n] for n in names] for d in (wts, gsm, mom, vel)], name="adamw_small")
    d_sm, m_sm, v_sm = (dict(zip(names, a)) for a in (d_s, m_s, v_s))
    d_in, m_in, v_in = _adamw(w_in, g_w_in, m_w_in, v_w_in, tr=256, name="adamw_w_in")
    d_out, m_out, v_out = _adamw(w_out, g_w_out, m_w_out, v_w_out, tr=256, name="adamw_w_out")

    order = ["w_in", "sinks", "r_conv_w", "r_conv_b", "r_wa", "r_ba", "r_wx", "r_bx", "r_lam", "g_conv_w",
             "g_a_log", "g_dt_bias", "g_norm_w", "w_out", "ln_g", "ln_b"]
    grads = dict(gsm, w_in=g_w_in, w_out=g_w_out)
    deltas = dict(d_sm, w_in=d_in, w_out=d_out)
    new_m = dict(m_sm, w_in=m_in, w_out=m_out)
    new_v = dict(v_sm, w_in=v_in, w_out=v_out)
    return (loss, grad_x, *[grads[n] for n in order], *[deltas[n] for n in order],
            *[new_m[n] for n in order], *[new_v[n] for n in order])
```

```python
import functools
import math

import jax
import jax.numpy as jnp
import numpy as np
from jax import lax
from jax.experimental import pallas as pl
from jax.experimental.pallas import tpu as pltpu

F32 = jnp.float32
BF16 = jnp.bfloat16
MESH = pl.DeviceIdType.MESH

DEPTH = 2
A_HEADS, A_KV_HEADS, A_HEAD_DIM = 8, 2, 64
A_WIDTH, A_KV_WIDTH = 512, 128
WINDOW = 128
ROPE_THETA = 10000.0
R_WIDTH, R_BLOCKS, R_BLOCK_DIM, R_C = 1024, 8, 128, 8.0
CONV_WIDTH = 4
G_HEADS, G_HEAD_DIM, G_WIDTH, G_CHUNK = 4, 128, 512, 64
MIX_WIDTH = 2048
IN_SIZES = (512, 128, 128, 512, 1024, 1024, 512, 512, 512, 512, 4, 4)
N_IN = 5384
DEEPNORM_ALPHA = (2 * DEPTH) ** 0.25
LN_EPS = 1e-5
RMS_EPS = 1e-6
ADAM_LR, ADAM_B1, ADAM_B2, ADAM_EPS, ADAM_WD, ADAM_STEP = 0.001, 0.9, 0.999, 1e-08, 0.01, 10

NP = 5632
OFF_RX, OFF_RZ, OFF_AQ, OFF_AZ, OFF_GQKV, OFF_GZ, OFF_AK, OFF_AV, OFF_GBA = (
    0, 1024, 2048, 2560, 3072, 4608, 5120, 5248, 5376)
_ORIG_OFF = np.concatenate([[0], np.cumsum(IN_SIZES)])[:-1]
_PIECES = ((4, OFF_RX), (5, OFF_RZ), (0, OFF_AQ), (3, OFF_AZ), (6, OFF_GQKV), (7, OFF_GQKV + 512),
           (8, OFF_GQKV + 1024), (9, OFF_GZ), (1, OFF_AK), (2, OFF_AV), (10, OFF_GBA), (11, OFF_GBA + 4))
MIX_R, MIX_A, MIX_G = 0, 1024, 1536
VMEM_LIMIT = 56 * 1024 * 1024


def _pcall(body, **kw):
    return pl.pallas_call(body, **kw)


def _cp(sem, limit=VMEM_LIMIT):
    return pltpu.CompilerParams(dimension_semantics=sem, vmem_limit_bytes=limit)


def _sigmoid(x):
    return 0.5 + 0.5 * jnp.tanh(0.5 * x)


def _silu(x):
    return x * _sigmoid(x)


def _dsilu(x):
    s = _sigmoid(x)
    return s * (1.0 + x * (1.0 - s))


def _log1p(x):
    u = 1.0 + x
    d = jnp.where(u == 1.0, 1.0, u - 1.0)
    return jnp.where(u == 1.0, x, jnp.log(u) * (x / d))


def _softplus(x):
    return jnp.maximum(x, 0.0) + _log1p(jnp.exp(-jnp.abs(x)))


def _one_minus_exp(x):
    series = -x * (1.0 + x * (0.5 + x * (1.0 / 6.0 + x * (1.0 / 24.0))))
    return jnp.where(x > -0.05, series, 1.0 - jnp.exp(x))


def _nn(a, b):
    return lax.dot_general(a, b, (((1,), (0,)), ((), ())), preferred_element_type=F32)


def _nt(a, b):
    return lax.dot_general(a, b, (((1,), (1,)), ((), ())), preferred_element_type=F32)


def _tn(a, b):
    return lax.dot_general(a, b, (((0,), (0,)), ((), ())), preferred_element_type=F32)


def _b(x):
    return x.astype(BF16)


def _split3(x):
    hi = x.astype(BF16)
    r1 = x - hi.astype(F32)
    mid = r1.astype(BF16)
    lo = (r1 - mid.astype(F32)).astype(BF16)
    return hi, mid, lo


def _dot3(f, a, b):
    ah, am, _ = _split3(a)
    bh, bm, _ = _split3(b)
    return f(ah, bh) + (f(ah, bm) + f(am, bh))


def _dot_exact_lhs(f, a_bf16, b):
    bh, bm, bl = _split3(b)
    return f(a_bf16, bh) + (f(a_bf16, bm) + f(a_bf16, bl))


def _rot(x):
    w = x.shape[-1]
    lane = lax.broadcasted_iota(jnp.int32, (1, w), 1)
    return jnp.where((lane & 63) < 32, pltpu.roll(x, w - 32, 1), pltpu.roll(x, 32, 1))


def _conv_taps(ext, n):
    return [pltpu.roll(ext, 3 - k, 0)[8:8 + n] if k < 3 else ext[8:8 + n] for k in range(CONV_WIDTH)]


def _conv_taps_t(ext, n):
    m = ext.shape[0]
    return [pltpu.roll(ext, m - (3 - k), 0)[0:n] if k < 3 else ext[0:n] for k in range(CONV_WIDTH)]


def _scan_lin(a, b, reverse):
    n = a.shape[0]
    row = lax.broadcasted_iota(jnp.int32, (n, 1), 0)
    s = 1
    while s < n:
        if reverse:
            a_sh = pltpu.roll(a, n - s, 0)
            b_sh = pltpu.roll(b, n - s, 0)
            ok = row < (n - s)
        else:
            a_sh = pltpu.roll(a, s, 0)
            b_sh = pltpu.roll(b, s, 0)
            ok = row >= s
        b = jnp.where(ok, a * b_sh + b, b)
        a = jnp.where(ok, a * a_sh, a)
        s *= 2
    return a, b


class _Side:
    def __init__(self, inputs, out_shapes, n_sems, start, finish):
        self.inputs, self.out_shapes, self.n_sems, self.start, self.finish = inputs, out_shapes, n_sems, start, finish


def _matmul(a, b, *, ta, tb, tm, tn, tk, out_dtype, name, extra=None, alpha=0.0, out_blocks=None, side=None):
    if ta:
        K, M = a.shape
    else:
        M, K = a.shape
    if tb:
        N, K2 = b.shape
    else:
        K2, N = b.shape
    assert K == K2 and M % tm == 0 and N % tn == 0 and K % tk == 0, (a.shape, b.shape, tm, tn, tk)
    nk = K // tk
    ca = 0 if ta else 1
    cb = 1 if tb else 0
    has_extra = extra is not None

    assert nk == 1 or out_dtype == F32
    n_in = 2 + int(has_extra)
    ns_in = len(side.inputs) if side else 0
    ns_out = len(side.out_shapes) if side else 0
    grid = (M // tm, N // tn, nk)

    def body(*refs):
        a_ref, b_ref = refs[0], refs[1]
        e_ref = refs[2] if has_extra else None
        o_ref = refs[n_in + ns_in]
        k = pl.program_id(2)
        if side:
            s_in = refs[n_in:n_in + ns_in]
            s_out = refs[n_in + ns_in + 1:n_in + ns_in + 1 + ns_out]
            ssem, rsem = refs[-2], refs[-1]
            i, j = pl.program_id(0), pl.program_id(1)

            @pl.when((i == 0) & (j == 0) & (k == 0))
            def _():
                side.start(s_in, s_out, ssem, rsem)

            @pl.when((i == grid[0] - 1) & (j == grid[1] - 1) & (k == grid[2] - 1))
            def _():
                side.finish(s_in, s_out, ssem, rsem)

        part = lax.dot_general(a_ref[...], b_ref[...], (((ca,), (cb,)), ((), ())), preferred_element_type=F32)
        if nk == 1:
            if e_ref is not None:
                part = part + alpha * e_ref[...]
            o_ref[...] = part.astype(o_ref.dtype)
            return

        @pl.when(k == 0)
        def _():
            o_ref[...] = part

        @pl.when((k > 0) & (k < nk - 1))
        def _():
            o_ref[...] += part

        @pl.when(k == nk - 1)
        def _():
            last = o_ref[...] + part
            if e_ref is not None:
                last = last + alpha * e_ref[...]
            o_ref[...] = last

    a_spec = (pl.BlockSpec((tk, tm), lambda i, j, k: (k, i)) if ta
              else pl.BlockSpec((tm, tk), lambda i, j, k: (i, k)))
    b_spec = (pl.BlockSpec((tn, tk), lambda i, j, k: (j, k)) if tb
              else pl.BlockSpec((tk, tn), lambda i, j, k: (k, j)))
    e_spec = pl.BlockSpec((tm, tn), lambda i, j, k: (i, j))
    if out_blocks is None:
        o_spec, o_shape = e_spec, (M, N)
    else:
        o_shape, o_block, o_map = out_blocks
        o_spec = pl.BlockSpec(o_block, lambda i, j, k: o_map(i, j))
    in_specs = [a_spec, b_spec] + ([e_spec] if has_extra else [])
    args = (a, b) + ((extra,) if has_extra else ())
    if not side:
        return _pcall(
            body, name=name, grid=grid, in_specs=in_specs, out_specs=o_spec,
            out_shape=jax.ShapeDtypeStruct(o_shape, out_dtype),
            compiler_params=_cp(("parallel", "parallel", "arbitrary")),
        )(*args)
    outs = _pcall(
        body, name=name, grid=grid, in_specs=in_specs + [HBM_SPEC] * ns_in,
        out_specs=[o_spec] + [HBM_SPEC] * ns_out,
        out_shape=[jax.ShapeDtypeStruct(o_shape, out_dtype)] + list(side.out_shapes),
        scratch_shapes=[pltpu.SemaphoreType.DMA((side.n_sems,)), pltpu.SemaphoreType.DMA((side.n_sems,))],
        compiler_params=_cp(("arbitrary", "arbitrary", "arbitrary")),
    )(*args, *side.inputs)
    return outs[0], outs[1:]


def _outproj(ymix, wo, x, *, tm, name):
    S_, D = x.shape

    def body(y_ref, w_ref, x_ref, z_ref):
        z_ref[...] = DEEPNORM_ALPHA * x_ref[...] + _nn(y_ref[...], w_ref[...])

    return _pcall(
        body, name=name, grid=(S_ // tm,),
        in_specs=[pl.BlockSpec((tm, MIX_WIDTH), lambda i: (i, 0)),
                  pl.BlockSpec((MIX_WIDTH, D), lambda i: (0, 0)),
                  pl.BlockSpec((tm, D), lambda i: (i, 0))],
        out_specs=pl.BlockSpec((tm, D), lambda i: (i, 0)),
        out_shape=jax.ShapeDtypeStruct((S_, D), F32),
        compiler_params=_cp(("parallel",)),
    )(ymix, wo, x)


def _ln_stats(z):
    mu = jnp.mean(z, -1, keepdims=True)
    zc = z - mu
    var = jnp.mean(zc * zc, -1, keepdims=True)
    rstd = lax.rsqrt(var + LN_EPS)
    return zc * rstd, rstd


def _ln_fwd(z, g, b, *, tm, name):
    S_, D = z.shape

    def body(z_ref, g_ref, b_ref, y_ref, yb_ref):
        xh, _ = _ln_stats(z_ref[...])
        y = xh * g_ref[...] + b_ref[...]
        y_ref[...] = y
        yb_ref[...] = y.astype(BF16)

    row = pl.BlockSpec((tm, D), lambda i: (i, 0))
    vec = pl.BlockSpec((1, D), lambda i: (0, 0))
    return _pcall(
        body, name=name, grid=(S_ // tm,), in_specs=[row, vec, vec], out_specs=[row, row],
        out_shape=[jax.ShapeDtypeStruct((S_, D), F32), jax.ShapeDtypeStruct((S_, D), BF16)],
        compiler_params=_cp(("parallel",)),
    )(z, g.reshape(1, D), b.reshape(1, D))


def _ln_bwd(z, g, b, other, *, from_target, tm, name):
    S_, D = z.shape

    def body(z_ref, g_ref, b_ref, o_ref, dz_ref, dzb_ref, dg_ref, db_ref, loss_ref):
        i = pl.program_id(0)

        @pl.when(i == 0)
        def _():
            dg_ref[...] = jnp.zeros_like(dg_ref)
            db_ref[...] = jnp.zeros_like(db_ref)
            loss_ref[...] = jnp.zeros_like(loss_ref)

        xh, rstd = _ln_stats(z_ref[...])
        gam = g_ref[...]
        if from_target:
            err = xh * gam + b_ref[...] - o_ref[...]
            per_tok = jnp.mean(err * err, -1, keepdims=True)
            loss_ref[...] += 0.5 * jnp.sum(per_tok, 0, keepdims=True)
            dy = err * (1.0 / D)
        else:
            dy = o_ref[...]
        dxh = dy * gam
        m1 = jnp.mean(dxh, -1, keepdims=True)
        m2 = jnp.mean(dxh * xh, -1, keepdims=True)
        dz = rstd * (dxh - m1 - xh * m2)
        dz_ref[...] = dz
        dzb_ref[...] = dz.astype(BF16)
        dg_ref[...] += jnp.sum(dy * xh, 0, keepdims=True)
        db_ref[...] += jnp.sum(dy, 0, keepdims=True)

    row = pl.BlockSpec((tm, D), lambda i: (i, 0))
    vec = pl.BlockSpec((1, D), lambda i: (0, 0))
    one = pl.BlockSpec((1, 1), lambda i: (0, 0))
    return _pcall(
        body, name=name, grid=(S_ // tm,), in_specs=[row, vec, vec, row],
        out_specs=[row, row, vec, vec, one],
        out_shape=[jax.ShapeDtypeStruct((S_, D), F32), jax.ShapeDtypeStruct((S_, D), BF16),
                   jax.ShapeDtypeStruct((1, D), F32), jax.ShapeDtypeStruct((1, D), F32),
                   jax.ShapeDtypeStruct((1, 1), F32)],
        compiler_params=_cp(("arbitrary",)),
    )(z, g.reshape(1, D), b.reshape(1, D), other)


def _attn_masks(i, sk_ref):
    ri = lax.broadcasted_iota(jnp.int32, (512, 256), 0)
    cj = lax.broadcasted_iota(jnp.int32, (512, 256), 1)
    diff = (ri & 127) - cj + 128
    band = (diff >= 0) & (diff < WINDOW)
    bias = jnp.where(band, 0.0, -jnp.inf)
    bias0 = jnp.where(band & ((i > 0) | (cj >= 128)), 0.0, -jnp.inf)
    grp = lax.broadcasted_iota(jnp.int32, (512, 1), 0) >> 7
    skvs = []
    for h in range(A_KV_HEADS):
        skv = jnp.zeros((512, 1), F32)
        for g in range(4):
            skv = jnp.where(grp == g, sk_ref[h * 4 + g], skv)
        skvs.append(skv)
    return bias0, bias, skvs


def _attn_common(masks, b, h, qr, kd, vd):
    lane = lax.broadcasted_iota(jnp.int32, (1, 128), 1)
    lof = (lane < 64).astype(F32)
    hif = 1.0 - lof
    r0 = b * 128
    skv = masks[2][h]
    pairs = [qr[r0:r0 + 128, h * 256 + p * 128:h * 256 + (p + 1) * 128] for p in (0, 1)]
    qs = _b(jnp.concatenate([pairs[0] * lof, pairs[0] * hif, pairs[1] * lof, pairs[1] * hif], 0))
    k2 = kd[h][r0:r0 + 256]
    v2 = vd[h][r0:r0 + 256]
    s = _nt(qs, k2) * (A_HEAD_DIM ** -0.5) + (masks[0] if b == 0 else masks[1])
    m = jnp.maximum(jnp.max(s, 1, keepdims=True), skv)
    p = jnp.exp(s - m)
    esk = jnp.exp(skv - m)
    rz = 1.0 / (jnp.sum(p, 1, keepdims=True) + esk)
    prob = p * rz
    o4 = _nn(_b(prob), v2)
    return lof, hif, qs, k2, v2, prob, esk * rz, o4


def _attn_prep(T, q_ref, k_ref, v_ref, c_ref, s_ref, kprev, vprev):
    C = c_ref[...]
    Sg = s_ref[...]
    C4 = jnp.concatenate([C] * 4, 1)
    S4 = jnp.concatenate([Sg] * 4, 1)
    q = q_ref[...]
    qr = q * C4 + _rot(q) * S4
    k = k_ref[...]
    kr = k * C + _rot(k) * Sg
    v = v_ref[...]
    kext = jnp.concatenate([kprev[...], kr], 0)
    vext = jnp.concatenate([vprev[...], v], 0)
    kprev[...] = kr[T - 128:]
    vprev[...] = v[T - 128:]
    lo = lax.broadcasted_iota(jnp.int32, (1, 128), 1) < 64
    kroll = pltpu.roll(kext, 64, 1)
    vroll = pltpu.roll(vext, 64, 1)
    kd = [_b(jnp.where(lo, kext, kroll)), _b(jnp.where(lo, kroll, kext))]
    vd = [_b(jnp.where(lo, vext, vroll)), _b(jnp.where(lo, vroll, vext))]
    return C, Sg, C4, S4, qr, kd, vd


def _attn_specs(T):
    return [pl.BlockSpec(memory_space=pltpu.SMEM),
            pl.BlockSpec((T, 512), lambda i: (i, OFF_AQ // 512)),
            pl.BlockSpec((T, 512), lambda i: (i, OFF_AZ // 512)),
            pl.BlockSpec((T, 128), lambda i: (i, OFF_AK // 128)),
            pl.BlockSpec((T, 128), lambda i: (i, OFF_AV // 128)),
            pl.BlockSpec((T, 128), lambda i: (i, 0)),
            pl.BlockSpec((T, 128), lambda i: (i, 0))]


def _attn_fwd(proj, rope_c, rope_s, sinks, *, T, name):
    S_ = proj.shape[0]
    nb = T // 128

    def body(sk_ref, q_ref, z_ref, k_ref, v_ref, c_ref, s_ref, y_ref, kprev, vprev):
        i = pl.program_id(0)

        @pl.when(i == 0)
        def _():
            kprev[...] = jnp.zeros_like(kprev)
            vprev[...] = jnp.zeros_like(vprev)

        _, _, _, _, qr, kd, vd = _attn_prep(T, q_ref, k_ref, v_ref, c_ref, s_ref, kprev, vprev)
        masks = _attn_masks(i, sk_ref)
        for b in range(nb):
            r0 = b * 128
            for h in range(2):
                lof, hif, _, _, _, _, _, o4 = _attn_common(masks, b, h, qr, kd, vd)
                for p in range(2):
                    cs = slice(h * 256 + p * 128, h * 256 + (p + 1) * 128)
                    o = o4[2 * p * 128:(2 * p + 1) * 128] * lof + o4[(2 * p + 1) * 128:(2 * p + 2) * 128] * hif
                    y_ref[r0:r0 + 128, cs] = (o * _silu(z_ref[r0:r0 + 128, cs])).astype(BF16)

    return _pcall(
        body, name=name, grid=(S_ // T,), in_specs=_attn_specs(T),
        out_specs=pl.BlockSpec((T, 512), lambda i: (i, 0)),
        out_shape=jax.ShapeDtypeStruct((S_, 512), BF16),
        scratch_shapes=[pltpu.VMEM((128, 128), F32), pltpu.VMEM((128, 128), F32)],
        compiler_params=_cp(("arbitrary",)),
    )(sinks, proj, proj, proj, proj, rope_c, rope_s)


def _attn_bwd(proj, rope_c, rope_s, sinks, dymix, *, T, name):
    S_ = proj.shape[0]
    nb = T // 128
    nt = S_ // T

    def body(sk_ref, q_ref, z_ref, k_ref, v_ref, c_ref, s_ref, dy_ref,
             dq_ref, dz_ref, dk_ref, dv_ref, dkt_ref, dvt_ref, dsk_ref,
             kprev, vprev, cprev, sprev, dkacc, dvacc, dqacc):
        i = pl.program_id(0)

        @pl.when(i == 0)
        def _():
            kprev[...] = jnp.zeros_like(kprev)
            vprev[...] = jnp.zeros_like(vprev)
            cprev[...] = jnp.zeros_like(cprev)
            sprev[...] = jnp.zeros_like(sprev)
            dkacc[...] = jnp.zeros_like(dkacc)
            dvacc[...] = jnp.zeros_like(dvacc)
            dsk_ref[...] = jnp.zeros_like(dsk_ref)

        @pl.when(i > 0)
        def _():
            dkacc[0:128, :] = dkacc[T:T + 128, :]
            dvacc[0:128, :] = dvacc[T:T + 128, :]
            dkacc[128:, :] = jnp.zeros((T, 128), F32)
            dvacc[128:, :] = jnp.zeros((T, 128), F32)

        C, Sg, C4, S4, qr, kd, vd = _attn_prep(T, q_ref, k_ref, v_ref, c_ref, s_ref, kprev, vprev)
        masks = _attn_masks(i, sk_ref)
        lane = lax.broadcasted_iota(jnp.int32, (1, 128), 1)
        for b in range(nb):
            r0 = b * 128
            for h in range(2):
                lof, hif, qs, k2, v2, prob, psink, o4 = _attn_common(masks, b, h, qr, kd, vd)
                dos = []
                for p in range(2):
                    cs = slice(h * 256 + p * 128, h * 256 + (p + 1) * 128)
                    o = o4[2 * p * 128:(2 * p + 1) * 128] * lof + o4[(2 * p + 1) * 128:(2 * p + 2) * 128] * hif
                    zc = z_ref[r0:r0 + 128, cs]
                    dyc = dy_ref[r0:r0 + 128, cs]
                    dz_ref[r0:r0 + 128, cs] = (dyc * o * _dsilu(zc)).astype(BF16)
                    do = dyc * _silu(zc)
                    dos += [do * lof, do * hif]
                dos = jnp.concatenate(dos, 0)
                os_ = jnp.concatenate([o4[0:128] * lof, o4[128:256] * hif, o4[256:384] * lof, o4[384:512] * hif], 0)
                delta = jnp.sum(dos * os_, 1, keepdims=True)
                dosb = _b(dos)
                dp = _nt(dosb, v2)
                ds = prob * (dp - delta)
                dsv = -psink * delta
                for g in range(4):
                    sg = jnp.sum(dsv[g * 128:(g + 1) * 128], 0, keepdims=True)
                    hd = h * 4 + g
                    dsk_ref[hd:hd + 1, :] += jnp.broadcast_to(sg, (1, 128))
                dsb = _b(ds * (A_HEAD_DIM ** -0.5))
                dqs = _nn(dsb, k2)
                for p in range(2):
                    cs = slice(h * 256 + p * 128, h * 256 + (p + 1) * 128)
                    dqacc[r0:r0 + 128, cs] = (dqs[2 * p * 128:(2 * p + 1) * 128] * lof
                                              + dqs[(2 * p + 1) * 128:(2 * p + 2) * 128] * hif)
                dkdup = _tn(dsb, qs)
                dvdup = _tn(_b(prob), dosb)
                half = (lane < 64) if h == 0 else (lane >= 64)
                dkacc[r0:r0 + 256, :] += jnp.where(half, dkdup + pltpu.roll(dkdup, 64, 1), 0.0)
                dvacc[r0:r0 + 256, :] += jnp.where(half, dvdup + pltpu.roll(dvdup, 64, 1), 0.0)
        dqr = dqacc[...]
        dq_ref[...] = (dqr * C4 + _rot(dqr * S4)).astype(BF16)
        cext = jnp.concatenate([cprev[...], C], 0)
        sext = jnp.concatenate([sprev[...], Sg], 0)
        dke = dkacc[...]
        dkp = dke * cext + _rot(dke * sext)
        dk_ref[...] = dkp[0:T].astype(BF16)
        dkt_ref[...] = dkp[T:T + 128].astype(BF16)
        dve = dvacc[...]
        dv_ref[...] = dve[0:T].astype(BF16)
        dvt_ref[...] = dve[T:T + 128].astype(BF16)
        cprev[...] = C[T - 128:]
        sprev[...] = Sg[T - 128:]

    wide = pl.BlockSpec((T, 512), lambda i: (i, 0))
    nar = pl.BlockSpec((T, 128), lambda i: (i, 0))
    tail = pl.BlockSpec((128, 128), lambda i: (0, 0))
    return _pcall(
        body, name=name, grid=(nt,),
        in_specs=_attn_specs(T) + [pl.BlockSpec((T, 512), lambda i: (i, MIX_A // 512))],
        out_specs=[wide, wide, nar, nar, tail, tail, pl.BlockSpec((8, 128), lambda i: (0, 0))],
        out_shape=[jax.ShapeDtypeStruct((S_, 512), BF16), jax.ShapeDtypeStruct((S_, 512), BF16),
                   jax.ShapeDtypeStruct((S_, 128), BF16), jax.ShapeDtypeStruct((S_, 128), BF16),
                   jax.ShapeDtypeStruct((128, 128), BF16), jax.ShapeDtypeStruct((128, 128), BF16),
                   jax.ShapeDtypeStruct((8, 128), F32)],
        scratch_shapes=[pltpu.VMEM((128, 128), F32)] * 4
        + [pltpu.VMEM((T + 128, 128), F32), pltpu.VMEM((T + 128, 128), F32), pltpu.VMEM((T, 512), F32)],
        compiler_params=_cp(("arbitrary",)),
    )(sinks, proj, proj, proj, proj, rope_c, rope_s, dymix)


def _rg_gates(xr, wa_ref, ba_ref, wx_ref, bx_ref, lam_ref):
    xb = _b(xr)
    pre_a = jnp.concatenate([_nn(xb[:, n * 128:(n + 1) * 128], wa_ref[n]) for n in range(R_BLOCKS)], 1) + ba_ref[...]
    pre_x = jnp.concatenate([_nn(xb[:, n * 128:(n + 1) * 128], wx_ref[n]) for n in range(R_BLOCKS)], 1) + bx_ref[...]
    r = _sigmoid(pre_a)
    ig = _sigmoid(pre_x)
    sp = _softplus(-lam_ref[...])
    log_a = -R_C * r * sp
    a = jnp.exp(log_a)
    mult = jnp.sqrt(_one_minus_exp(2.0 * log_a))
    return xb, r, ig, sp, a, mult


def _rg_param_specs():
    C = R_WIDTH
    vec = pl.BlockSpec((1, C), lambda i: (0, 0))
    blk = pl.BlockSpec((R_BLOCKS, 128, 128), lambda i: (0, 0, 0))
    return [pl.BlockSpec((CONV_WIDTH, C), lambda i: (0, 0)), vec, blk, vec, blk, vec, vec]


def _rglru_fwd(proj, cw, cb, wa, ba, wx, bx, lam, *, T, name):
    S_ = proj.shape[0]
    C = R_WIDTH

    def body(rx_ref, rz_ref, cw_ref, cb_ref, wa_ref, ba_ref, wx_ref, bx_ref, lam_ref,
             h_ref, y_ref, halo, hcar):
        i = pl.program_id(0)

        @pl.when(i == 0)
        def _():
            halo[...] = jnp.zeros_like(halo)
            hcar[...] = jnp.zeros_like(hcar)

        rx = rx_ref[...]
        ext = jnp.concatenate([halo[...], rx], 0)
        halo[...] = rx[T - 8:]
        taps = _conv_taps(ext, T)
        xr = cb_ref[...] + sum(cw_ref[k:k + 1, :] * taps[k] for k in range(CONV_WIDTH))
        _, _, ig, _, a, mult = _rg_gates(xr, wa_ref, ba_ref, wx_ref, bx_ref, lam_ref)
        u = mult * (ig * xr)
        acum, hloc = _scan_lin(a, u, False)
        h = hloc + acum * hcar[0:1, :]
        hcar[...] = jnp.broadcast_to(h[T - 1:T, :], (8, C))
        h_ref[...] = h
        y_ref[...] = (h * _silu(rz_ref[...])).astype(BF16)

    row = pl.BlockSpec((T, C), lambda i: (i, 0))
    return _pcall(
        body, name=name, grid=(S_ // T,),
        in_specs=[pl.BlockSpec((T, C), lambda i: (i, OFF_RX // C)),
                  pl.BlockSpec((T, C), lambda i: (i, OFF_RZ // C))] + _rg_param_specs(),
        out_specs=[row, row],
        out_shape=[jax.ShapeDtypeStruct((S_, C), F32), jax.ShapeDtypeStruct((S_, C), BF16)],
        scratch_shapes=[pltpu.VMEM((8, C), F32), pltpu.VMEM((8, C), F32)],
        compiler_params=_cp(("arbitrary",)),
    )(proj, proj, cw, cb.reshape(1, C), _b(wa), ba.reshape(1, C), _b(wx), bx.reshape(1, C), lam.reshape(1, C))


def _rglru_bwd(proj, h, dymix, cw, cb, wa, ba, wx, bx, lam, *, T, name):
    S_ = proj.shape[0]
    C = R_WIDTH
    nt = S_ // T
    t8 = T // 8

    def body(rx_ref, rxp_ref, rz_ref, h_ref, hp_ref, dy_ref,
             cw_ref, cb_ref, wa_ref, ba_ref, wx_ref, bx_ref, lam_ref, wat_ref, wxt_ref,
             drx_ref, drz_ref, dcw_ref, dcb_ref, dwa_ref, dba_ref, dwx_ref, dbx_ref, dlam_ref,
             afirst, gfirst, dhalo):
        i = pl.program_id(0)
        first_tile = (i == nt - 1)

        @pl.when(i == 0)
        def _():
            afirst[...] = jnp.zeros_like(afirst)
            gfirst[...] = jnp.zeros_like(gfirst)
            dhalo[...] = jnp.zeros_like(dhalo)
            for r in (dcw_ref, dcb_ref, dwa_ref, dba_ref, dwx_ref, dbx_ref, dlam_ref):
                r[...] = jnp.zeros_like(r)

        keep = jnp.where(first_tile, 0.0, 1.0)
        rx = rx_ref[...]
        ext = jnp.concatenate([rxp_ref[...] * keep, rx], 0)
        taps = _conv_taps(ext, T)
        xr = cb_ref[...] + sum(cw_ref[k:k + 1, :] * taps[k] for k in range(CONV_WIDTH))
        xb, r, ig, sp, a, mult = _rg_gates(xr, wa_ref, ba_ref, wx_ref, bx_ref, lam_ref)
        hh = h_ref[...]
        rz = rz_ref[...]
        dy = dy_ref[...]
        drz_ref[...] = (dy * hh * _dsilu(rz)).astype(BF16)
        dh = dy * _silu(rz)
        row = lax.broadcasted_iota(jnp.int32, (T, 1), 0)
        c = jnp.where(row == T - 1, afirst[0:1, :], pltpu.roll(a, T - 1, 0))
        ccum, gloc = _scan_lin(c, dh, True)
        g = gloc + ccum * gfirst[0:1, :]
        afirst[...] = jnp.broadcast_to(a[0:1, :], (8, C))
        gfirst[...] = jnp.broadcast_to(g[0:1, :], (8, C))
        hprev = jnp.where(row == 0, hp_ref[7:8, :] * keep, pltpu.roll(hh, 1, 0))
        da = g * hprev
        gx = ig * xr
        dgx = g * mult
        dmult = g * gx
        dlog_a = da * a - dmult * (a * a) * lax.rsqrt(mult * mult)
        dpre_a = dlog_a * (-R_C * sp) * r * (1.0 - r)
        dpre_x = dgx * xr * ig * (1.0 - ig)
        dlam_ref[...] += jnp.sum(dlog_a * (-R_C * r), 0, keepdims=True) * (-_sigmoid(-lam_ref[...]))
        dab = _b(dpre_a)
        dxb = _b(dpre_x)
        dxr = dgx * ig + jnp.concatenate(
            [_nn(dab[:, n * 128:(n + 1) * 128], wat_ref[n]) + _nn(dxb[:, n * 128:(n + 1) * 128], wxt_ref[n])
             for n in range(R_BLOCKS)], 1)
        for n in range(R_BLOCKS):
            cs = slice(n * 128, (n + 1) * 128)
            dwa_ref[n] += _tn(xb[:, cs], dab[:, cs])
            dwx_ref[n] += _tn(xb[:, cs], dxb[:, cs])
        dba_ref[...] += jnp.sum(dpre_a, 0, keepdims=True)
        dbx_ref[...] += jnp.sum(dpre_x, 0, keepdims=True)
        dcb_ref[...] += jnp.sum(dxr, 0, keepdims=True)
        for k in range(CONV_WIDTH):
            dcw_ref[k:k + 1, :] += jnp.sum(dxr * taps[k], 0, keepdims=True)
        ext2 = jnp.concatenate([dxr, dhalo[...]], 0)
        tt = _conv_taps_t(ext2, T)
        drx_ref[...] = sum(cw_ref[k:k + 1, :] * tt[k] for k in range(CONV_WIDTH)).astype(BF16)
        dhalo[...] = dxr[0:8]

    def rev(i):
        return nt - 1 - i

    def prev8(i):
        return jnp.maximum(rev(i) * t8 - 1, 0)

    vec = pl.BlockSpec((1, C), lambda i: (0, 0))
    blk = pl.BlockSpec((R_BLOCKS, 128, 128), lambda i: (0, 0, 0))
    row = pl.BlockSpec((T, C), lambda i: (rev(i), 0))
    wat = _b(jnp.swapaxes(wa, 1, 2))
    wxt = _b(jnp.swapaxes(wx, 1, 2))
    return _pcall(
        body, name=name, grid=(nt,),
        in_specs=[pl.BlockSpec((T, C), lambda i: (rev(i), OFF_RX // C)),
                  pl.BlockSpec((8, C), lambda i: (prev8(i), OFF_RX // C)),
                  pl.BlockSpec((T, C), lambda i: (rev(i), OFF_RZ // C)),
                  row,
                  pl.BlockSpec((8, C), lambda i: (prev8(i), 0)),
                  pl.BlockSpec((T, C), lambda i: (rev(i), MIX_R // C)),
                  ] + _rg_param_specs() + [blk, blk],
        out_specs=[row, row, pl.BlockSpec((CONV_WIDTH, C), lambda i: (0, 0)), vec, blk, vec, blk, vec, vec],
        out_shape=[jax.ShapeDtypeStruct((S_, C), BF16), jax.ShapeDtypeStruct((S_, C), BF16),
                   jax.ShapeDtypeStruct((CONV_WIDTH, C), F32), jax.ShapeDtypeStruct((1, C), F32),
                   jax.ShapeDtypeStruct((R_BLOCKS, 128, 128), F32), jax.ShapeDtypeStruct((1, C), F32),
                   jax.ShapeDtypeStruct((R_BLOCKS, 128, 128), F32), jax.ShapeDtypeStruct((1, C), F32),
                   jax.ShapeDtypeStruct((1, C), F32)],
        scratch_shapes=[pltpu.VMEM((8, C), F32)] * 3,
        compiler_params=_cp(("arbitrary",)),
    )(proj, proj, proj, h, h, dymix, cw, cb.reshape(1, C), _b(wa), ba.reshape(1, C), _b(wx), bx.reshape(1, C),
      lam.reshape(1, C), wat, wxt)


GW3 = 3 * G_WIDTH


def _lane_col(x, lane_idx):
    lane = lax.broadcasted_iota(jnp.int32, (1, x.shape[1]), 1)
    return jnp.sum(jnp.where(lane == lane_idx, x, 0.0), 1, keepdims=True)


def _gdn_pre(ext, T, cw_ref, gba, pv_ref):
    taps = _conv_taps(ext, T)
    c = sum(cw_ref[k:k + 1, :] * taps[k] for k in range(CONV_WIDTH))
    qkv = _silu(c)
    beta = _sigmoid(gba)
    sarg = gba + pv_ref[1:2, :]
    nea = -jnp.exp(pv_ref[0:1, :])
    gdec = nea * _softplus(sarg)
    ri = lax.broadcasted_iota(jnp.int32, (T, T), 0)
    cj = lax.broadcasted_iota(jnp.int32, (T, T), 1)
    same = (ri >> 6) == (cj >> 6)
    ltri = jnp.where((ri >= cj) & same, 1.0, 0.0).astype(BF16)
    gc = _dot_exact_lhs(_nn, ltri, gdec)
    return taps, c, qkv, beta, sarg, nea, gdec, gc


def _gdn_masks():
    ri = lax.broadcasted_iota(jnp.int32, (128, 128), 0)
    cj = lax.broadcasted_iota(jnp.int32, (128, 128), 1)
    same = (ri >> 6) == (cj >> 6)
    return (ri >= cj) & same, (ri > cj) & same, ri == cj


def _lockstep(gens):
    out = [None] * len(gens)
    live = list(range(len(gens)))
    while live:
        still = []
        for k in live:
            try:
                next(gens[k])
                still.append(k)
            except StopIteration as stop:
                out[k] = stop.value
        live = still
    return out


def _gdn_chunk(qkv, beta, gc, rs, h, tm=None):
    tril, strict, eye = _gdn_masks()
    rowi = lax.broadcasted_iota(jnp.int32, (128, 1), 0)
    lane = lax.broadcasted_iota(jnp.int32, (1, 128), 1)
    qh = qkv[rs, h * 128:(h + 1) * 128]
    kh = qkv[rs, 512 + h * 128:512 + (h + 1) * 128]
    vh = qkv[rs, 1024 + h * 128:1024 + (h + 1) * 128]
    rq = lax.rsqrt(jnp.sum(qh * qh, 1, keepdims=True) + RMS_EPS)
    rk = lax.rsqrt(jnp.sum(kh * kh, 1, keepdims=True) + RMS_EPS)
    qn = qh * (rq * (G_HEAD_DIM ** -0.5))
    kn = kh * rk
    gcb = gc[rs]
    gcol = _lane_col(gcb, 4 + h)
    bcol = _lane_col(beta[rs], h)
    grow = _dot_exact_lhs(_nt, jnp.ones((128, 128), BF16), jnp.where(lane == 4 + h, gcb, 0.0))
    D = jnp.where(tril, jnp.exp(jnp.minimum(gcol - grow, 0.0)), 0.0)
    kb = kn * bcol
    vb = vh * bcol
    knb = _b(kn)
    A = _nt(_b(kb), knb)
    Bm = _nt(_b(qn), knb)
    yield
    if tm is None:
        N = jnp.where(strict, -(A * D), 0.0)
        tm = jnp.where(eye, 1.0, 0.0) + N
        npow = N
        for _ in range(5):
            npow = _dot3(_nn, npow, npow)
            yield
            tm = tm + _dot3(_nn, tm, npow)
            yield
    eg = jnp.exp(gcol)
    u = _dot3(_nn, tm, vb)
    w = _dot3(_nn, tm, kb * eg)
    yield
    qk = jnp.where(tril, Bm * D, 0.0)
    qd = qn * eg
    gla = jnp.sum(jnp.where(rowi == 63, gcol, 0.0), 0, keepdims=True)
    glb = jnp.sum(jnp.where(rowi == 127, gcol, 0.0), 0, keepdims=True)
    ed = jnp.exp(jnp.where(rowi < 64, gla, glb) - gcol)
    kd = kn * ed
    return dict(qh=qh, kh=kh, vh=vh, rq=rq, rk=rk, qn=qn, kn=kn, gcol=gcol, bcol=bcol, D=D, A=A, Bm=Bm,
                tm=tm, eg=eg, ed=ed, u=u, w=w, qk=qk, qd=qd, kd=kd, kb=kb, vb=vb,
                gla=jnp.exp(gla), glb=jnp.exp(glb))


def _gdn_scan(q, sa):
    sab = _b(sa)
    wb = _b(q["w"])
    vna = q["u"] - _nn(wb, sab)
    yield
    sb = sa * q["gla"] + _tn(_b(q["kd"][0:64]), _b(vna[0:64]))
    yield
    sbb = _b(sb)
    vnb = q["u"] - _nn(wb, sbb)
    yield
    sn = sb * q["glb"] + _tn(_b(q["kd"][64:128]), _b(vnb[64:128]))
    yield
    vn = jnp.concatenate([vna[0:64], vnb[64:128]], 0)
    qdb = _b(q["qd"])
    o = jnp.concatenate([_nn(qdb[0:64], sab), _nn(qdb[64:128], sbb)], 0) + _nn(_b(q["qk"]), _b(vn))
    return sb, sn, vn, o


def _gdn_param_specs():
    return [pl.BlockSpec((CONV_WIDTH, GW3), lambda i: (0, 0)),
            pl.BlockSpec((8, 128), lambda i: (0, 0)),
            pl.BlockSpec((1, 128), lambda i: (0, 0))]


def _gdn_pvec(a_log, dt_bias):
    z = jnp.zeros((8, 128), F32)
    return z.at[0, 4:8].set(a_log).at[1, 4:8].set(dt_bias)


def _gdn_fwd(proj, cw, a_log, dt_bias, nw, *, T, name):
    S_ = proj.shape[0]
    nu = T // 128

    def body(x_ref, z_ref, g_ref, cw_ref, pv_ref, nw_ref, y_ref, st_ref, tm_ref, halo, state):
        i = pl.program_id(0)

        @pl.when(i == 0)
        def _():
            halo[...] = jnp.zeros_like(halo)
            state[...] = jnp.zeros_like(state)

        x = x_ref[...]
        ext = jnp.concatenate([halo[...], x], 0)
        halo[...] = x[T - 8:]
        _, _, qkv, beta, _, _, _, gc = _gdn_pre(ext, T, cw_ref, g_ref[...], pv_ref)
        items = [(dc, h) for dc in range(nu) for h in range(G_HEADS)]
        qs = _lockstep([_gdn_chunk(qkv, beta, gc, slice(dc * 128, (dc + 1) * 128), h) for dc, h in items])

        def head_chain(h):
            s = state[h]
            for dc in range(nu):
                rs = slice(dc * 128, (dc + 1) * 128)
                q = qs[dc * G_HEADS + h]
                sb, sn, _, o = yield from _gdn_scan(q, s)
                st_ref[2 * dc, h] = s
                st_ref[2 * dc + 1, h] = sb
                tm_ref[dc, h] = q["tm"]
                s = sn
                yield
                rn = lax.rsqrt(jnp.mean(o * o, 1, keepdims=True) + RMS_EPS)
                cs = slice(h * 128, (h + 1) * 128)
                y_ref[rs, cs] = (o * rn * nw_ref[...] * _silu(z_ref[rs, cs])).astype(BF16)
                yield
            state[h] = s

        _lockstep([head_chain(h) for h in range(G_HEADS)])

    return _pcall(
        body, name=name, grid=(S_ // T,),
        in_specs=[pl.BlockSpec((T, GW3), lambda i: (i, OFF_GQKV // GW3)),
                  pl.BlockSpec((T, 512), lambda i: (i, OFF_GZ // 512)),
                  pl.BlockSpec((T, 128), lambda i: (i, OFF_GBA // 128))] + _gdn_param_specs(),
        out_specs=[pl.BlockSpec((T, 512), lambda i: (i, 0)),
                   pl.BlockSpec((2 * nu, G_HEADS, 128, 128), lambda i: (i, 0, 0, 0)),
                   pl.BlockSpec((nu, G_HEADS, 128, 128), lambda i: (i, 0, 0, 0))],
        out_shape=[jax.ShapeDtypeStruct((S_, 512), BF16),
                   jax.ShapeDtypeStruct((S_ // 64, G_HEADS, 128, 128), F32),
                   jax.ShapeDtypeStruct((S_ // 128, G_HEADS, 128, 128), F32)],
        scratch_shapes=[pltpu.VMEM((8, GW3), F32), pltpu.VMEM((G_HEADS, 128, 128), F32)],
        compiler_params=_cp(("arbitrary",)),
    )(proj, proj, proj, cw, _gdn_pvec(a_log, dt_bias), nw.reshape(1, 128))


def _gdn_bwd(proj, states, tms, dymix, cw, a_log, dt_bias, nw, *, T, name):
    S_ = proj.shape[0]
    nt = S_ // T
    nu = T // 128
    t8 = T // 8

    def body(x_ref, xp_ref, z_ref, g_ref, st_ref, tm_ref, dy_ref, cw_ref, pv_ref, nw_ref,
             dx_ref, dz_ref, dg_ref, dcw_ref, dpv_ref, dnw_ref, dstate, dhalo, dqkv, dbg):
        i = pl.program_id(0)
        first_tile = (i == nt - 1)

        @pl.when(i == 0)
        def _():
            dstate[...] = jnp.zeros_like(dstate)
            dhalo[...] = jnp.zeros_like(dhalo)
            dcw_ref[...] = jnp.zeros_like(dcw_ref)
            dpv_ref[...] = jnp.zeros_like(dpv_ref)
            dnw_ref[...] = jnp.zeros_like(dnw_ref)

        keep = jnp.where(first_tile, 0.0, 1.0)
        ext = jnp.concatenate([xp_ref[...] * keep, x_ref[...]], 0)
        G = g_ref[...]
        taps, c, qkv, beta, sarg, nea, gdec, gc = _gdn_pre(ext, T, cw_ref, G, pv_ref)
        tril, strict, _ = _gdn_masks()
        rowi = lax.broadcasted_iota(jnp.int32, (128, 1), 0)
        lane = lax.broadcasted_iota(jnp.int32, (1, 128), 1)
        ones_b = jnp.ones((128, 128), BF16)
        nwv = nw_ref[...]
        items = [(dc, h) for dc in range(nu) for h in range(G_HEADS)]

        def recompute(dc, h):
            q = yield from _gdn_chunk(qkv, beta, gc, slice(dc * 128, (dc + 1) * 128), h, tm=tm_ref[dc, h])
            sa = st_ref[2 * dc, h]
            sb, _, vn, o = yield from _gdn_scan(q, sa)
            return q, sa, sb, vn, o

        fw = _lockstep([recompute(dc, h) for dc, h in items])
        chain_out = {}

        def head_chain(h):
            dS = dstate[h]
            for dc in reversed(range(nu)):
                rs = slice(dc * 128, (dc + 1) * 128)
                q, sa, sb, vn, o = fw[dc * G_HEADS + h]
                cs = slice(h * 128, (h + 1) * 128)
                zg = z_ref[rs, cs]
                dy = dy_ref[rs, cs]
                rn = lax.rsqrt(jnp.mean(o * o, 1, keepdims=True) + RMS_EPS)
                don = dy * _silu(zg)
                dz_ref[rs, cs] = (dy * (o * rn * nwv) * _dsilu(zg)).astype(BF16)
                dnw_ref[...] += jnp.sum(don * o * rn, 0, keepdims=True)
                tt = don * nwv
                do = rn * (tt - o * (rn * rn) * jnp.mean(tt * o, 1, keepdims=True))
                yield
                dob = _b(do)
                sab, sbb = _b(sa), _b(sb)
                vnb16 = _b(vn)
                dqk = jnp.where(tril, _nt(dob, vnb16), 0.0)
                dvn_o = _tn(_b(q["qk"]), dob)
                dSb16 = _b(dS)
                kdb = _b(q["kd"])
                wb = _b(q["w"])
                qdb = _b(q["qd"])
                yield
                dvn_b = dvn_o[64:128] + _nn(kdb[64:128], dSb16)
                dkd_b = _nt(vnb16[64:128], dSb16)
                dgl_b = jnp.sum(jnp.sum(dS * sb, 1, keepdims=True), 0, keepdims=True)
                yield
                dvn_b16 = _b(dvn_b)
                dw_b = -_nt(dvn_b16, sbb)
                dqd_b = _nt(dob[64:128], sbb)
                dSm = q["glb"] * dS + _tn(qdb[64:128], dob[64:128]) - _tn(wb[64:128], dvn_b16)
                yield
                dSm16 = _b(dSm)
                dvn_a = dvn_o[0:64] + _nn(kdb[0:64], dSm16)
                dkd_a = _nt(vnb16[0:64], dSm16)
                dgl_a = jnp.sum(jnp.sum(dSm * sa, 1, keepdims=True), 0, keepdims=True)
                yield
                dvn_a16 = _b(dvn_a)
                dw_a = -_nt(dvn_a16, sab)
                dqd_a = _nt(dob[0:64], sab)
                dS = q["gla"] * dSm + _tn(qdb[0:64], dob[0:64]) - _tn(wb[0:64], dvn_a16)
                chain_out[dc, h] = (dqk, jnp.concatenate([dvn_a, dvn_b], 0), jnp.concatenate([dw_a, dw_b], 0),
                                    jnp.concatenate([dkd_a, dkd_b], 0), jnp.concatenate([dqd_a, dqd_b], 0),
                                    dgl_a, dgl_b)
                yield
            dstate[h] = dS

        _lockstep([head_chain(h) for h in range(G_HEADS)])

        def local(dc, h):
            rs = slice(dc * 128, (dc + 1) * 128)
            q = fw[dc * G_HEADS + h][0]
            dqk, du, dw, dkd, dqd, dgl_a, dgl_b = chain_out[dc, h]
            if True:
                dvb = _dot3(_tn, q["tm"], du)
                dkbe = _dot3(_tn, q["tm"], dw)
                yield
                dM = jnp.where(strict, -(_nt(_b(dvb), _b(q["u"])) + _nt(_b(dkbe), _b(q["w"]))), 0.0)
                yield
                D = q["D"]
                dA = dM * D
                dB = dqk * D
                dDD = (dM * q["A"] + dqk * q["Bm"]) * D
                dh_, dm_, dl_ = _split3(dDD)
                colsum = _tn(dh_, ones_b) + (_tn(dm_, ones_b) + _tn(dl_, ones_b))
                dgc = jnp.sum(dDD, 1, keepdims=True) - _lane_col(colsum, 0)
                yield
                dA16, dB16 = _b(dA), _b(dB)
                knb, kbb, qnb = _b(q["kn"]), _b(q["kb"]), _b(q["qn"])
                eg, ed = q["eg"], q["ed"]
                dkb = _nn(dA16, knb) + dkbe * eg
                dkn = _tn(dA16, kbb) + _tn(dB16, qnb) + dkd * ed + dkb * q["bcol"]
                dqn = _nn(dB16, knb) + dqd * eg
                yield
                deg = jnp.sum(dkbe * q["kb"], 1, keepdims=True) + jnp.sum(dqd * q["qn"], 1, keepdims=True)
                ded = jnp.sum(dkd * q["kn"], 1, keepdims=True) * ed
                dgc = dgc + deg * eg - ded
                tail_a = jnp.sum(jnp.where(rowi < 64, ded, 0.0), 0, keepdims=True) + dgl_a * q["gla"]
                tail_b = jnp.sum(jnp.where(rowi >= 64, ded, 0.0), 0, keepdims=True) + dgl_b * q["glb"]
                dgc = dgc + jnp.where(rowi == 63, tail_a, 0.0) + jnp.where(rowi == 127, tail_b, 0.0)
                dbeta = jnp.sum(dkb * q["kn"], 1, keepdims=True) + jnp.sum(dvb * q["vh"], 1, keepdims=True)
                bcol = q["bcol"]
                blk = jnp.where(lane == h, dbeta * bcol * (1.0 - bcol), 0.0) + jnp.where(lane == 4 + h, dgc, 0.0)
                yield
                sc = G_HEAD_DIM ** -0.5
                rq, rk, qh, kh = q["rq"], q["rk"], q["qh"], q["kh"]
                dqh = sc * (dqn * rq - qh * (rq * rq * rq) * jnp.sum(dqn * qh, 1, keepdims=True))
                dkh = dkn * rk - kh * (rk * rk * rk) * jnp.sum(dkn * kh, 1, keepdims=True)
                dqkv[rs, h * 128:(h + 1) * 128] = dqh
                dqkv[rs, 512 + h * 128:512 + (h + 1) * 128] = dkh
                dqkv[rs, 1024 + h * 128:1024 + (h + 1) * 128] = dvb * bcol
            return blk

        blks = _lockstep([local(dc, h) for dc, h in items])
        for dc in range(nu):
            dbg[dc * 128:(dc + 1) * 128, :] = functools.reduce(
                lambda a, b: a + b, [blks[dc * G_HEADS + h] for h in range(G_HEADS)])
        ri = lax.broadcasted_iota(jnp.int32, (T, T), 0)
        cj = lax.broadcasted_iota(jnp.int32, (T, T), 1)
        utri = jnp.where((ri <= cj) & ((ri >> 6) == (cj >> 6)), 1.0, 0.0).astype(BF16)
        dbgv = dbg[...]
        dgd = _dot_exact_lhs(_nn, utri, dbgv)
        is_g = (lane >= 4) & (lane < 8)
        dga = jnp.where(is_g, dgd * nea * _sigmoid(sarg), 0.0)
        dg_ref[...] = jnp.where(lane < 4, dbgv, dga).astype(BF16)
        dpv_ref[0:1, :] += jnp.sum(jnp.where(is_g, dgd * gdec, 0.0), 0, keepdims=True)
        dpv_ref[1:2, :] += jnp.sum(dga, 0, keepdims=True)
        dc_ = dqkv[...] * _dsilu(c)
        for k in range(CONV_WIDTH):
            dcw_ref[k:k + 1, :] += jnp.sum(dc_ * taps[k], 0, keepdims=True)
        ext2 = jnp.concatenate([dc_, dhalo[...]], 0)
        tt2 = _conv_taps_t(ext2, T)
        dx_ref[...] = sum(cw_ref[k:k + 1, :] * tt2[k] for k in range(CONV_WIDTH)).astype(BF16)
        dhalo[...] = dc_[0:8]

    def rev(i):
        return nt - 1 - i

    def prev8(i):
        return jnp.maximum(rev(i) * t8 - 1, 0)

    return _pcall(
        body, name=name, grid=(nt,),
        in_specs=[pl.BlockSpec((T, GW3), lambda i: (rev(i), OFF_GQKV // GW3)),
                  pl.BlockSpec((8, GW3), lambda i: (prev8(i), OFF_GQKV // GW3)),
                  pl.BlockSpec((T, 512), lambda i: (rev(i), OFF_GZ // 512)),
                  pl.BlockSpec((T, 128), lambda i: (rev(i), OFF_GBA // 128)),
                  pl.BlockSpec((2 * nu, G_HEADS, 128, 128), lambda i: (rev(i), 0, 0, 0)),
                  pl.BlockSpec((nu, G_HEADS, 128, 128), lambda i: (rev(i), 0, 0, 0)),
                  pl.BlockSpec((T, 512), lambda i: (rev(i), MIX_G // 512))] + _gdn_param_specs(),
        out_specs=[pl.BlockSpec((T, GW3), lambda i: (rev(i), 0)),
                   pl.BlockSpec((T, 512), lambda i: (rev(i), 0)),
                   pl.BlockSpec((T, 128), lambda i: (rev(i), 0)),
                   pl.BlockSpec((CONV_WIDTH, GW3), lambda i: (0, 0)),
                   pl.BlockSpec((8, 128), lambda i: (0, 0)),
                   pl.BlockSpec((1, 128), lambda i: (0, 0))],
        out_shape=[jax.ShapeDtypeStruct((S_, GW3), BF16), jax.ShapeDtypeStruct((S_, 512), BF16),
                   jax.ShapeDtypeStruct((S_, 128), BF16), jax.ShapeDtypeStruct((CONV_WIDTH, GW3), F32),
                   jax.ShapeDtypeStruct((8, 128), F32), jax.ShapeDtypeStruct((1, 128), F32)],
        scratch_shapes=[pltpu.VMEM((G_HEADS, 128, 128), F32), pltpu.VMEM((8, GW3), F32),
                        pltpu.VMEM((T, GW3), F32), pltpu.VMEM((T, 128), F32)],
        compiler_params=_cp(("arbitrary",)),
    )(proj, proj, proj, proj, states, tms, dymix, cw, _gdn_pvec(a_log, dt_bias), nw.reshape(1, 128))


def _pair_sum_windows(a, b, nsh, width, *, out_dtype, name):
    R_, C = a.shape
    hr = R_ // 2
    nb = width // 128
    assert (3 * nsh) // 128 + nb <= C // 128

    def body(a0_ref, a1_ref, b_ref, o_ref):
        mine = jnp.where(lax.axis_index("c") == 0, a0_ref[...], a1_ref[...])
        o_ref[...] = (mine + b_ref[...]).astype(o_ref.dtype)

    def spec(half):
        return pl.BlockSpec((hr, 128), lambda t, j: (half, (nsh * t) // 128 + j))

    return _pcall(body, name=name, grid=(4, nb), in_specs=[spec(0), spec(1), spec(0)],
                  out_specs=pl.BlockSpec((None, hr, 128), lambda t, j: (t, 0, j)),
                  out_shape=jax.ShapeDtypeStruct((4, hr, width), out_dtype),
                  compiler_params=_cp(("parallel", "parallel")))(a, a, b)


def _pair_sum_blocks(a, b, *, out_dtype, name):
    L, R_, C = a.shape
    hr = R_ // 2

    def body(a0_ref, a1_ref, b_ref, o_ref):
        mine = jnp.where(lax.axis_index("c") == 0, a0_ref[...], a1_ref[...])
        o_ref[...] = (mine + b_ref[...]).astype(o_ref.dtype)

    def spec(half):
        return pl.BlockSpec((None, hr, C), lambda t: (t, half, 0))

    return _pcall(body, name=name, grid=(L,), in_specs=[spec(0), spec(1), spec(0)], out_specs=spec(0),
                  out_shape=jax.ShapeDtypeStruct((L, hr, C), out_dtype),
                  compiler_params=_cp(("parallel",)))(a, a, b)


def _add_mine(a0, a1, b, *, out_dtype, tr, name):
    R_, C = b.shape

    def body(a0_ref, a1_ref, b_ref, o_ref):
        mine = jnp.where(lax.axis_index("c") == 0, a0_ref[...], a1_ref[...])
        o_ref[...] = (mine + b_ref[...]).astype(o_ref.dtype)

    spec = pl.BlockSpec((tr, C), lambda i: (i, 0))
    return _pcall(body, name=name, grid=(R_ // tr,), in_specs=[spec] * 3, out_specs=spec,
                  out_shape=jax.ShapeDtypeStruct((R_, C), out_dtype), compiler_params=_cp(("parallel",)))(a0, a1, b)


def _sum4(a, mine, *, tr, name):
    _, R_, C = a.shape

    def body(a_ref, m_ref, o_ref):
        s = 2 * lax.axis_index("x") + lax.axis_index("y")
        mv = m_ref[...].astype(F32)
        p = [jnp.where(s == t, mv, a_ref[t].astype(F32)) for t in range(4)]
        o_ref[...] = ((p[0] + p[1]) + p[2]) + p[3]

    return _pcall(body, name=name, grid=(R_ // tr,),
                  in_specs=[pl.BlockSpec((4, tr, C), lambda i: (0, i, 0)), pl.BlockSpec((tr, C), lambda i: (i, 0))],
                  out_specs=pl.BlockSpec((tr, C), lambda i: (i, 0)),
                  out_shape=jax.ShapeDtypeStruct((R_, C), F32), compiler_params=_cp(("parallel",)))(a, mine)


def _adamw_refs(w_ref, g_ref, m_ref, v_ref, d_ref, mo_ref, vo_ref):
    c1 = 1.0 / (1.0 - ADAM_B1 ** ADAM_STEP)
    c2 = 1.0 / (1.0 - ADAM_B2 ** ADAM_STEP)
    gg = g_ref[...]
    mn = ADAM_B1 * m_ref[...] + (1.0 - ADAM_B1) * gg
    vn = ADAM_B2 * v_ref[...] + (1.0 - ADAM_B2) * (gg * gg)
    mo_ref[...] = mn
    vo_ref[...] = vn
    d_ref[...] = -ADAM_LR * ((mn * c1) / (jnp.sqrt(vn * c2) + ADAM_EPS) + ADAM_WD * w_ref[...])


def _adamw_many(ws, gs, ms, vs, *, name):
    n = len(ws)

    def body(*refs):
        for k in range(n):
            _adamw_refs(*[refs[q * n + k] for q in range(7)])

    vm = pl.BlockSpec(memory_space=pltpu.VMEM)
    shp = [jax.ShapeDtypeStruct(w.shape, F32) for w in ws]
    outs = _pcall(body, name=name, in_specs=[vm] * (4 * n), out_specs=[vm] * (3 * n), out_shape=shp * 3,
                  compiler_params=pltpu.CompilerParams(vmem_limit_bytes=VMEM_LIMIT))(*ws, *gs, *ms, *vs)
    return outs[:n], outs[n:2 * n], outs[2 * n:]


def _adamw(w, g, m, v, *, tr, name):
    L, R_, C = w.shape
    body = functools.partial(_adamw_refs)

    spec = pl.BlockSpec((None, tr, C), lambda l, i: (l, i, 0))
    shp = jax.ShapeDtypeStruct((L, R_, C), F32)
    return _pcall(body, name=name, grid=(L, R_ // tr), in_specs=[spec] * 4, out_specs=[spec] * 3,
                  out_shape=[shp] * 3, compiler_params=_cp(("parallel", "parallel")))(w, g, m, v)


HBM_SPEC = pl.BlockSpec(memory_space=pltpu.HBM)


def _place():
    x, y, c = lax.axis_index("x"), lax.axis_index("y"), lax.axis_index("c")
    chips = [(1 - x, y), (x, 1 - y), (1 - x, 1 - y)]
    return x, y, c, 2 * x + y, chips, [2 * cx + cy for cx, cy in chips], (x, y, 1 - c)


def _remote(src, dst, ssem, rsem, dev):
    return pltpu.make_async_remote_copy(src_ref=src, dst_ref=dst, send_sem=ssem, recv_sem=rsem,
                                        device_id=dev, device_id_type=MESH)


def _row_half(ref, lead, hc):
    hl = ref.shape[-2] // 2
    return ref.at[lead, pl.ds(hc * hl, hl), :]


def _gather_side(win, wout, layer):
    def copies(ins, outs, ssem, rsem):
        x, y, c, s, chips, sid, sib = _place()
        cps = []
        for j, chip in enumerate(chips):
            dev = (*chip, c)
            cps.append(_remote(_row_half(ins[0], layer, c), _row_half(outs[0], s, c), ssem.at[j], rsem.at[j], dev))
            cps.append(_remote(_row_half(ins[1], layer, c), _row_half(outs[1], s, c), ssem.at[3 + j], rsem.at[3 + j],
                               dev))
        return cps, c, sid, sib

    def start(ins, outs, ssem, rsem):
        for cp in copies(ins, outs, ssem, rsem)[0]:
            cp.start()

    def finish(ins, outs, ssem, rsem):
        cps, c, sid, sib = copies(ins, outs, ssem, rsem)
        for j in range(3):
            for k in range(2):
                got = _row_half(outs[k], sid[j], c)
                _remote(got, got, ssem.at[3 * k + j], rsem.at[3 * k + j], sib).wait_recv()
        for cp in cps:
            cp.wait_send()

    shapes = [jax.ShapeDtypeStruct((4,) + win.shape[1:], win.dtype), jax.ShapeDtypeStruct((4,) + wout.shape[1:], wout.dtype)]
    return _Side([win, wout], shapes, 6, start, finish)


def _gather_join(gin, gout, name):
    def body(gin_in, gout_in, gin_ref, gout_ref, ssem, rsem):
        x, y, c, s, chips, sid, sib = _place()
        cps = []
        for j in range(3):
            for k, ref in enumerate((gin_ref, gout_ref)):
                mine = _row_half(ref, sid[j], c)
                cps.append(_remote(mine, mine, ssem.at[3 * k + j], rsem.at[3 * k + j], sib))
        for cp in cps:
            cp.start()
        for j in range(3):
            for k, ref in enumerate((gin_ref, gout_ref)):
                other = _row_half(ref, sid[j], 1 - c)
                _remote(other, other, ssem.at[3 * k + j], rsem.at[3 * k + j], sib).wait_recv()
        for cp in cps:
            cp.wait_send()

    return _pcall(
        body, name=name, in_specs=[HBM_SPEC] * 2, out_specs=[HBM_SPEC] * 2,
        out_shape=[jax.ShapeDtypeStruct(gin.shape, gin.dtype), jax.ShapeDtypeStruct(gout.shape, gout.dtype)],
        input_output_aliases={0: 0, 1: 1},
        scratch_shapes=[pltpu.SemaphoreType.DMA((6,)), pltpu.SemaphoreType.DMA((6,))],
    )(gin, gout)


def _gather_layer0(win, wout, conv):
    def body(win_ref, wout_ref, cv_ref, gin_ref, gout_ref, gcv_ref, ssem, rsem):
        x, y, c, s, chips, sid, sib = _place()

        def in_half(slot, hc):
            return _row_half(gin_ref, slot, hc)

        def out_half(slot, hc):
            return _row_half(gout_ref, slot, hc)

        sends = []
        for j, chip in enumerate(chips):
            dev = (*chip, c)
            sends.append(_remote(_row_half(win_ref, 0, c), in_half(s, c), ssem.at[j], rsem.at[j], dev))
            sends.append(_remote(_row_half(wout_ref, 0, c), out_half(s, c), ssem.at[3 + j], rsem.at[3 + j], dev))
            sends.append(_remote(cv_ref, gcv_ref.at[s], ssem.at[6 + j], rsem.at[6 + j], dev))
        for cp in sends:
            cp.start()
        for j in range(3):
            _remote(in_half(sid[j], c), in_half(sid[j], c), ssem.at[j], rsem.at[j], sib).wait_recv()
            f = _remote(in_half(sid[j], c), in_half(sid[j], c), ssem.at[9 + j], rsem.at[9 + j], sib)
            f.start()
            sends.append(f)
            _remote(out_half(sid[j], c), out_half(sid[j], c), ssem.at[3 + j], rsem.at[3 + j], sib).wait_recv()
            f = _remote(out_half(sid[j], c), out_half(sid[j], c), ssem.at[12 + j], rsem.at[12 + j], sib)
            f.start()
            sends.append(f)
        for j in range(3):
            _remote(in_half(sid[j], 1 - c), in_half(sid[j], 1 - c), ssem.at[9 + j], rsem.at[9 + j], sib).wait_recv()
            _remote(out_half(sid[j], 1 - c), out_half(sid[j], 1 - c), ssem.at[12 + j], rsem.at[12 + j], sib).wait_recv()
            _remote(gcv_ref.at[sid[j]], gcv_ref.at[sid[j]], ssem.at[6 + j], rsem.at[6 + j], sib).wait_recv()
        for cp in sends:
            cp.wait_send()

    return _pcall(
        body, name="gather_layer0",
        in_specs=[HBM_SPEC] * 3, out_specs=[HBM_SPEC] * 3,
        out_shape=[jax.ShapeDtypeStruct((4,) + win.shape[1:], win.dtype),
                   jax.ShapeDtypeStruct((4,) + wout.shape[1:], wout.dtype),
                   jax.ShapeDtypeStruct((4,) + conv.shape, conv.dtype)],
        scratch_shapes=[pltpu.SemaphoreType.DMA((15,)), pltpu.SemaphoreType.DMA((15,))],
    )(win, wout, conv)


def _swap_halves(arrs, axes, name):
    n = len(arrs)

    def half_shape(a, ax):
        return a.shape[:ax] + (a.shape[ax] // 2,) + a.shape[ax + 1:]

    def body(*refs):
        src, dst, ssem, rsem = refs[:n], refs[n:2 * n], refs[2 * n], refs[2 * n + 1]
        x, y, c, s, chips, sid, sib = _place()
        cps = []
        for k in range(n):
            hl = src[k].shape[axes[k]] // 2
            idx = [slice(None)] * len(src[k].shape)
            idx[axes[k]] = pl.ds((1 - c) * hl, hl)
            cps.append(_remote(src[k].at[tuple(idx)], dst[k], ssem.at[k], rsem.at[k], sib))
        for cp in cps:
            cp.start()
        for cp in cps:
            cp.wait()

    return _pcall(
        body, name=name, in_specs=[HBM_SPEC] * n, out_specs=[HBM_SPEC] * n,
        out_shape=[jax.ShapeDtypeStruct(half_shape(a, ax), a.dtype) for a, ax in zip(arrs, axes)],
        scratch_shapes=[pltpu.SemaphoreType.DMA((n,)), pltpu.SemaphoreType.DMA((n,))],
    )(*arrs)


def _chips_side(arrs):
    n = len(arrs)

    def copies(ins, outs, ssem, rsem):
        x, y, c, s, chips, sid, sib = _place()
        cps = [_remote(ins[k].at[sid[j]], outs[k].at[s], ssem.at[3 * k + j], rsem.at[3 * k + j], (*chip, c))
               for k in range(n) for j, chip in enumerate(chips)]
        return cps, sid, sib

    def start(ins, outs, ssem, rsem):
        for cp in copies(ins, outs, ssem, rsem)[0]:
            cp.start()

    def finish(ins, outs, ssem, rsem):
        cps, sid, sib = copies(ins, outs, ssem, rsem)
        for k in range(n):
            for j in range(3):
                got = outs[k].at[sid[j]]
                _remote(got, got, ssem.at[3 * k + j], rsem.at[3 * k + j], sib).wait_recv()
        for cp in cps:
            cp.wait_send()

    return _Side(list(arrs), [jax.ShapeDtypeStruct(a.shape, a.dtype) for a in arrs], 3 * n, start, finish)


def _scatter_chips(arrs, per_target, name):
    n = len(arrs)

    def body(*refs):
        src, dst = refs[:n], refs[n:2 * n]
        ssem, rsem = refs[2 * n], refs[2 * n + 1]
        x, y, c, s, chips, sid, sib = _place()
        sends = []
        for k in range(n):
            for j, chip in enumerate(chips):
                piece = src[k].at[sid[j]] if per_target[k] else src[k]
                sends.append(_remote(piece, dst[k].at[s], ssem.at[3 * k + j], rsem.at[3 * k + j], (*chip, c)))
        for cp in sends:
            cp.start()
        for k in range(n):
            for j in range(3):
                _remote(dst[k].at[sid[j]], dst[k].at[sid[j]], ssem.at[3 * k + j], rsem.at[3 * k + j], sib).wait_recv()
        for cp in sends:
            cp.wait_send()

    outs = [jax.ShapeDtypeStruct(a.shape if pt else (4,) + a.shape, a.dtype) for a, pt in zip(arrs, per_target)]
    return _pcall(
        body, name=name, in_specs=[HBM_SPEC] * n, out_specs=[HBM_SPEC] * n, out_shape=outs,
        scratch_shapes=[pltpu.SemaphoreType.DMA((3 * n,)), pltpu.SemaphoreType.DMA((3 * n,))],
    )(*arrs)


def _swap_whole(arrs, name):
    n = len(arrs)

    def body(*refs):
        src, dst, ssem, rsem = refs[:n], refs[n:2 * n], refs[2 * n], refs[2 * n + 1]
        *_, sib = _place()
        cps = [_remote(src[k], dst[k], ssem.at[k], rsem.at[k], sib) for k in range(n)]
        for cp in cps:
            cp.start()
        for cp in cps:
            cp.wait()

    return _pcall(
        body, name=name, in_specs=[HBM_SPEC] * n, out_specs=[HBM_SPEC] * n,
        out_shape=[jax.ShapeDtypeStruct(a.shape, a.dtype) for a in arrs],
        scratch_shapes=[pltpu.SemaphoreType.DMA((n,)), pltpu.SemaphoreType.DMA((n,))],
    )(*arrs)


def _perm_cols(w):
    parts = [w[..., int(_ORIG_OFF[oi]):int(_ORIG_OFF[oi]) + IN_SIZES[oi]] for oi, _ in _PIECES]
    parts.append(jnp.zeros(w.shape[:-1] + (NP - N_IN,), w.dtype))
    return jnp.concatenate(parts, -1)


def _perm_rows(w):
    return jnp.concatenate([w[..., 512:1536, :], w[..., 0:512, :], w[..., 1536:2048, :]], -2)


_SMALL = ("sinks", "r_conv_b", "r_wa", "r_ba", "r_wx", "r_bx", "r_lam", "g_a_log", "g_dt_bias", "g_norm_w",
          "ln_g", "ln_b", "r_conv_w", "g_conv_w")
_PACK_ROWS = 16


def _piece_rows(n):
    return -(-n // (128 * _PACK_ROWS)) * _PACK_ROWS


def _pack(arrs):
    parts = []
    for a in arrs:
        n = int(np.prod(a.shape))
        rows = _piece_rows(n)
        if n % 128 == 0:
            blk = a.reshape(n // 128, 128)
        else:
            blk = jnp.pad(a.reshape(1, n), ((0, 0), (0, (-n) % 128))).reshape(-1, 128)
        if blk.shape[0] < rows:
            blk = jnp.pad(blk, ((0, rows - blk.shape[0]), (0, 0)))
        parts.append(blk)
    return jnp.concatenate(parts, 0)


def _unpack(packed, shapes):
    out = []
    r = 0
    for shp in shapes:
        n = int(np.prod(shp))
        if n % 128 == 0:
            out.append(packed[r:r + n // 128].reshape(shp))
        else:
            nr = -(-n // 128)
            out.append(packed[r:r + nr].reshape(1, nr * 128)[:, :n].reshape(shp))
        r += _piece_rows(n)
    return out


def _tile(n, t):
    return min(n, t)


def _layer_fwd(l, x, xb, wb, wob, rope_c, rope_s, p, side=None):
    S_ = x.shape[0]
    proj = _matmul(xb, wb, ta=False, tb=False, tm=_tile(S_, 1024), tn=NP // 4, tk=wb.shape[0], out_dtype=F32,
                   name=f"in_proj_{l}", side=side)
    side_out = None
    if side:
        proj, side_out = proj
    ya = _attn_fwd(proj, rope_c, rope_s, p["sinks"], T=_tile(S_, 512), name=f"attn_fwd_{l}")
    h, yr = _rglru_fwd(proj, p["r_conv_w"], p["r_conv_b"], p["r_wa"], p["r_ba"], p["r_wx"], p["r_bx"], p["r_lam"],
                       T=_tile(S_, 256), name=f"rglru_fwd_{l}")
    yg, st, tms = _gdn_fwd(proj, p["g_conv_w"], p["g_a_log"], p["g_dt_bias"], p["g_norm_w"],
                           T=_tile(S_, 256), name=f"gdn_fwd_{l}")
    ymix = jnp.concatenate([yr, ya, yg], 1)
    z = _outproj(ymix, wob, x, tm=_tile(S_, 256), name=f"out_proj_{l}")
    return dict(proj=proj, h=h, st=st, tms=tms, ymix=ymix, z=z, side=side_out)


def _layer_bwd(l, sv, x_b, dz, dzb, wo, wob, rope_c, rope_s, p, side=None):
    S_, D = dz.shape
    proj = sv["proj"]
    dymix = _matmul(dzb, wob, ta=False, tb=True, tm=_tile(S_, 1024), tn=512, tk=D, out_dtype=F32,
                    name=f"dmix_{l}")
    dwo = _matmul(sv["ymix"], dzb, ta=True, tb=False, tm=512, tn=_tile(D, 2048), tk=_tile(S_, 1024),
                  out_dtype=F32, name=f"dw_out_{l}",
                  out_blocks=((MIX_WIDTH, D), (512, _tile(D, 2048)),
                              lambda i, j: (jnp.where(i == 3, 3, (i + 1) % 3), j)))
    dq, daz, dk, dv, dkt, dvt, dsk = _attn_bwd(proj, rope_c, rope_s, p["sinks"], dymix, T=_tile(S_, 512),
                                               name=f"attn_bwd_{l}")
    (drx, drz, dcw_r, dcb_r, dwa, dba, dwx, dbx, dlam) = _rglru_bwd(
        proj, sv["h"], dymix, p["r_conv_w"], p["r_conv_b"], p["r_wa"], p["r_ba"], p["r_wx"], p["r_bx"], p["r_lam"],
        T=_tile(S_, 256), name=f"rglru_bwd_{l}")
    dqkv, dgz, dgba, dcw_g, dpv, dnw = _gdn_bwd(proj, sv["st"], sv["tms"], dymix, p["g_conv_w"], p["g_a_log"],
                                                p["g_dt_bias"], p["g_norm_w"], T=_tile(S_, 256), name=f"gdn_bwd_{l}")
    dproj = jnp.concatenate([dq, jnp.concatenate([dk[128:], dkt], 0), jnp.concatenate([dv[128:], dvt], 0), daz,
                             drx, drz, dqkv, dgz, dgba, jnp.zeros((S_, NP - N_IN - 120), BF16)], 1)
    dx = _matmul(dproj, wo, ta=False, tb=True, tm=_tile(S_, 1024), tn=_tile(D, 1024), tk=NP // 2, out_dtype=F32,
                 name=f"dx_{l}", extra=dz, alpha=DEEPNORM_ALPHA, side=side)
    side_out = None
    if side:
        dx, side_out = dx
    dwin = _matmul(x_b, dproj, ta=True, tb=False, tm=_tile(D, 1024), tn=NP // 4, tk=_tile(S_, 1024), out_dtype=F32,
                   name=f"dw_in_{l}")
    small = dict(sinks=dsk[:, 0], r_conv_b=dcb_r[0], r_wa=dwa, r_ba=dba[0], r_wx=dwx, r_bx=dbx[0], r_lam=dlam[0],
                 g_a_log=dpv[0, 4:8], g_dt_bias=dpv[1, 4:8], g_norm_w=dnw[0], r_conv_w=dcw_r, g_conv_w=dcw_g)
    return dx, dwin, dwo, small, side_out


def kernel(x, w_in, sinks, r_conv_w, r_conv_b, r_wa, r_ba, r_wx, r_bx, r_lam, g_conv_w, g_a_log, g_dt_bias, g_norm_w, w_out, ln_g, ln_b, loss_target, m_w_in, m_sinks, m_r_conv_w, m_r_conv_b, m_r_wa, m_r_ba, m_r_wx, m_r_bx, m_r_lam, m_g_conv_w, m_g_a_log, m_g_dt_bias, m_g_norm_w, m_w_out, m_ln_g, m_ln_b, v_w_in, v_sinks, v_r_conv_w, v_r_conv_b, v_r_wa, v_r_ba, v_r_wx, v_r_bx, v_r_lam, v_g_conv_w, v_g_a_log, v_g_dt_bias, v_g_norm_w, v_w_out, v_ln_g, v_ln_b):
    S_, D = x.shape[1], x.shape[2]
    nsh = w_in.shape[2]
    rsh = w_out.shape[1]
    cx, cy, cc = lax.axis_index("x"), lax.axis_index("y"), lax.axis_index("c")
    chip = 2 * cx + cy
    rcw_n, gcw_n = r_conv_w.shape[2], g_conv_w.shape[2]

    conv_pack = jnp.concatenate([r_conv_w, g_conv_w], 2)
    w_in_b, w_out_b = w_in.astype(BF16), w_out.astype(BF16)
    g_in0, g_out0, g_conv = _gather_layer0(w_in_b, w_out_b, conv_pack)

    def shards(own, got):
        return [jnp.where(chip == t, own, got[t]) for t in range(4)]

    def layer_weights(l, g_in, g_out):
        w_full = jnp.concatenate(shards(w_in_b[l], g_in), 1)
        return (_perm_cols(w_full),
                jnp.concatenate([w_full, jnp.zeros((D, NP - N_IN), BF16)], 1),
                _perm_rows(jnp.concatenate(shards(w_out_b[l], g_out), 0)))

    rcw = jnp.concatenate(shards(r_conv_w, g_conv[:, :, :, :rcw_n]), 2)
    gcw = jnp.concatenate(shards(g_conv_w, g_conv[:, :, :, rcw_n:]), 2)

    pos = jnp.arange(S_, dtype=F32)[:, None]
    inv = 1.0 / (ROPE_THETA ** (jnp.arange(0, A_HEAD_DIM, 2, dtype=F32) / A_HEAD_DIM))
    ang = pos * inv[None, :]
    cos, sin = jnp.cos(ang), jnp.sin(ang)
    rope_c = jnp.concatenate([cos, cos, cos, cos], 1)
    rope_s = jnp.concatenate([-sin, sin, -sin, sin], 1)

    def params(l):
        return dict(sinks=sinks[l], r_conv_w=rcw[l], r_conv_b=r_conv_b[l], r_wa=r_wa[l], r_ba=r_ba[l],
                    r_wx=r_wx[l], r_bx=r_bx[l], r_lam=r_lam[l], g_conv_w=gcw[l], g_a_log=g_a_log[l],
                    g_dt_bias=g_dt_bias[l], g_norm_w=g_norm_w[l])

    xs, xbs, saved = [x[0]], [x[0].astype(BF16)], []
    wb, wo, wob = [None] * DEPTH, [None] * DEPTH, [None] * DEPTH
    wb[0], wo[0], wob[0] = layer_weights(0, g_in0, g_out0)
    for l in range(DEPTH):
        nxt = _gather_side(w_in_b, w_out_b, l + 1) if l + 1 < DEPTH else None
        sv = _layer_fwd(l, xs[l], xbs[l], wb[l], wob[l], rope_c, rope_s, params(l), side=nxt)
        if nxt:
            wb[l + 1], wo[l + 1], wob[l + 1] = layer_weights(l + 1, *_gather_join(*sv["side"], f"gather_join_{l + 1}"))
        saved.append(sv)
        if l + 1 < DEPTH:
            xn, xnb = _ln_fwd(sv["z"], ln_g[l], ln_b[l], tm=_tile(S_, 256), name=f"ln_fwd_{l}")
            xs.append(xn)
            xbs.append(xnb)

    tm_ln = _tile(S_, 256)
    dz, dzb, dg_l, db_l, loss_part = _ln_bwd(saved[-1]["z"], ln_g[-1], ln_b[-1], loss_target[0], from_target=True,
                                             tm=tm_ln, name=f"ln_bwd_{DEPTH - 1}")
    wcov = (-(-nsh // 128) + 1) * 128

    def own(a):
        return lax.dynamic_index_in_dim(a, chip, 0, keepdims=False)

    def pair_reduce(l, dwin_l, dwo_l, extra):
        dwo4 = dwo_l.reshape(4, rsh, D)
        got = _swap_halves([dwin_l, dwo4] + [e for e in extra], [0, 1] + [0] * len(extra), f"reduce_pair_{l}")
        in_cp = _pair_sum_windows(dwin_l, got[0], nsh, wcov, out_dtype=BF16, name=f"pair_sum_w_in_{l}")
        out_cp = _pair_sum_blocks(dwo4, got[1], out_dtype=BF16, name=f"pair_sum_w_out_{l}")
        ex_cp = [_pair_sum_blocks(e[None], g[None], out_dtype=F32, name=f"pair_sum_small_{l}")[0]
                 for e, g in zip(extra, got[2:])]
        return in_cp, out_cp, ex_cp

    def chip_sums(l, in_cp, out_cp, in_all, out_all):
        return (_sum4(in_all, own(in_cp), tr=_tile(D // 2, 256), name=f"chip_sum_w_in_{l}"),
                _sum4(out_all, own(out_cp), tr=rsh // 2, name=f"chip_sum_w_out_{l}"))

    small = [None] * DEPTH
    dlng, dlnb = [None] * DEPTH, [None] * DEPTH
    sums = [None] * DEPTH
    riding = None
    for l in reversed(range(DEPTH)):
        dlng[l], dlnb[l] = dg_l[0], db_l[0]
        side = _chips_side(list(riding[1:])) if riding else None
        dx, dwin_l, dwo_l, small[l], arrived = _layer_bwd(l, saved[l], xbs[l], dz, dzb, wo[l], wob[l], rope_c, rope_s,
                                                          params(l), side=side)
        if riding:
            sums[riding[0]] = chip_sums(riding[0], riding[1], riding[2], *arrived)
        if l > 0:
            riding = (l,) + pair_reduce(l, dwin_l, dwo_l, [])[:2]
            dz, dzb, dg_l, db_l, _ = _ln_bwd(saved[l - 1]["z"], ln_g[l - 1], ln_b[l - 1], dx, from_target=False,
                                             tm=tm_ln, name=f"ln_bwd_{l - 1}")
    grad_x = dx[None]
    loss = lax.psum(loss_part[0, 0], ("x", "y", "c"))

    sm = {k: jnp.stack([small[l][k] for l in range(DEPTH)]) for k in small[0]}
    sm["ln_g"], sm["ln_b"] = jnp.stack(dlng), jnp.stack(dlnb)
    names = list(_SMALL)
    gs = _pack([sm[n] for n in names])
    in_cp, out_cp, (s_cp,) = pair_reduce(0, dwin_l, dwo_l, [gs])
    in_all, out_all, s_all = _scatter_chips([in_cp, out_cp, s_cp], [True, True, False], "reduce_chips_0")
    sums[0] = chip_sums(0, in_cp, out_cp, in_all, out_all)
    s_sum = _sum4(s_all, s_cp, tr=s_cp.shape[0], name="chip_sum_small")
    mine = [a for l in range(DEPTH) for a in sums[l]] + [s_sum]
    other = _swap_whole(mine, "reduce_join")

    def both(k, axis):
        return jnp.where(cc == 0, jnp.concatenate([mine[k], other[k]], axis),
                         jnp.concatenate([other[k], mine[k]], axis))

    g_w_in = lax.dynamic_slice_in_dim(jnp.stack([both(2 * l, 0) for l in range(DEPTH)]), (nsh * chip) % 128, nsh, 2)
    g_w_out = jnp.stack([both(2 * l + 1, 0) for l in range(DEPTH)])
    g_small = both(2 * DEPTH, 0)

    gsm = dict(zip(names, _unpack(g_small, [sm[n].shape for n in names])))
    gsm["r_conv_w"] = lax.dynamic_slice_in_dim(gsm["r_conv_w"], chip * rcw_n, rcw_n, 2)
    gsm["g_conv_w"] = lax.dynamic_slice_in_dim(gsm["g_conv_w"], chip * gcw_n, gcw_n, 2)
    wts = dict(sinks=sinks, r_conv_w=r_conv_w, r_conv_b=r_conv_b, r_wa=r_wa, r_ba=r_ba, r_wx=r_wx, r_bx=r_bx,
               r_lam=r_lam, g_conv_w=g_conv_w, g_a_log=g_a_log, g_dt_bias=g_dt_bias, g_norm_w=g_norm_w,
               ln_g=ln_g, ln_b=ln_b)
    mom = dict(sinks=m_sinks, r_conv_w=m_r_conv_w, r_conv_b=m_r_conv_b, r_wa=m_r_wa, r_ba=m_r_ba, r_wx=m_r_wx,
               r_bx=m_r_bx, r_lam=m_r_lam, g_conv_w=m_g_conv_w, g_a_log=m_g_a_log, g_dt_bias=m_g_dt_bias,
               g_norm_w=m_g_norm_w, ln_g=m_ln_g, ln_b=m_ln_b)
    vel = dict(sinks=v_sinks, r_conv_w=v_r_conv_w, r_conv_b=v_r_conv_b, r_wa=v_r_wa, r_ba=v_r_ba, r_wx=v_r_wx,
               r_bx=v_r_bx, r_lam=v_r_lam, g_conv_w=v_g_conv_w, g_a_log=v_g_a_log, g_dt_bias=v_g_dt_bias,
               g_norm_w=v_g_norm_w, ln_g=v_ln_g, ln_b=v_ln_b)
    d_s, m_s, v_s = _adamw_many(*[[d[n] for n in names] for d in (wts, gsm, mom, vel)], name="adamw_small")
    d_sm, m_sm, v_sm = (dict(zip(names, a)) for a in (d_s, m_s, v_s))
    d_in, m_in, v_in = _adamw(w_in, g_w_in, m_w_in, v_w_in, tr=256, name="adamw_w_in")
    d_out, m_out, v_out = _adamw(w_out, g_w_out, m_w_out, v_w_out, tr=256, name="adamw_w_out")

    order = ["w_in", "sinks", "r_conv_w", "r_conv_b", "r_wa", "r_ba", "r_wx", "r_bx", "r_lam", "g_conv_w",
             "g_a_log", "g_dt_bias", "g_norm_w", "w_out", "ln_g", "ln_b"]
    grads = dict(gsm, w_in=g_w_in, w_out=g_w_out)
    deltas = dict(d_sm, w_in=d_in, w_out=d_out)
    new_m = dict(m_sm, w_in=m_in, w_out=m_out)
    new_v = dict(v_sm, w_in=v_in, w_out=v_out)
    return (loss, grad_x, *[grads[n] for n in order], *[deltas[n] for n in order],
            *[new_m[n] for n in order], *[new_v[n] for n in order])
```

```python
import functools
import math

import jax
import jax.numpy as jnp
import numpy as np
from jax import lax
from jax.experimental import pallas as pl
from jax.experimental.pallas import tpu as pltpu

F32 = jnp.float32
BF16 = jnp.bfloat16
MESH = pl.DeviceIdType.MESH

DEPTH = 2
A_HEADS, A_KV_HEADS, A_HEAD_DIM = 8, 2, 64
A_WIDTH, A_KV_WIDTH = 512, 128
WINDOW = 128
ROPE_THETA = 10000.0
R_WIDTH, R_BLOCKS, R_BLOCK_DIM, R_C = 1024, 8, 128, 8.0
CONV_WIDTH = 4
G_HEADS, G_HEAD_DIM, G_WIDTH, G_CHUNK = 4, 128, 512, 64
MIX_WIDTH = 2048
IN_SIZES = (512, 128, 128, 512, 1024, 1024, 512, 512, 512, 512, 4, 4)
N_IN = 5384
DEEPNORM_ALPHA = (2 * DEPTH) ** 0.25
LN_EPS = 1e-5
RMS_EPS = 1e-6
ADAM_LR, ADAM_B1, ADAM_B2, ADAM_EPS, ADAM_WD, ADAM_STEP = 0.001, 0.9, 0.999, 1e-08, 0.01, 10

NP = 5632
OFF_GQKV, OFF_GZ, OFF_RX, OFF_RZ, OFF_AQ, OFF_AZ, OFF_AK, OFF_AV, OFF_GBA = (
    0, 1536, 2048, 3072, 4096, 4608, 5120, 5248, 5376)
_ORIG_OFF = np.concatenate([[0], np.cumsum(IN_SIZES)])[:-1]
_PIECES = ((6, OFF_GQKV), (7, OFF_GQKV + 512), (8, OFF_GQKV + 1024), (9, OFF_GZ), (4, OFF_RX), (5, OFF_RZ),
           (0, OFF_AQ), (3, OFF_AZ), (1, OFF_AK), (2, OFF_AV), (10, OFF_GBA), (11, OFF_GBA + 4))


def _orig_block_to_perm():
    table = list(range(NP // 128))
    for oi, off in _PIECES:
        if IN_SIZES[oi] % 128 == 0:
            for k in range(IN_SIZES[oi] // 128):
                table[int(_ORIG_OFF[oi]) // 128 + k] = off // 128 + k
    return table
MIX_R, MIX_A, MIX_G = 0, 1024, 1536
VMEM_LIMIT = 56 * 1024 * 1024
ANY_SPEC = pl.BlockSpec(memory_space=pl.ANY)


def _pcall(body, **kw):
    return pl.pallas_call(body, **kw)


def _cp(sem, limit=VMEM_LIMIT):
    return pltpu.CompilerParams(dimension_semantics=sem, vmem_limit_bytes=limit)


def _sigmoid(x):
    return 0.5 + 0.5 * jnp.tanh(0.5 * x)


def _silu(x):
    return x * _sigmoid(x)


def _dsilu(x):
    s = _sigmoid(x)
    return s * (1.0 + x * (1.0 - s))


def _log1p(x):
    u = 1.0 + x
    d = jnp.where(u == 1.0, 1.0, u - 1.0)
    return jnp.where(u == 1.0, x, jnp.log(u) * (x / d))


def _softplus(x):
    return jnp.maximum(x, 0.0) + _log1p(jnp.exp(-jnp.abs(x)))


def _one_minus_exp(x):
    series = -x * (1.0 + x * (0.5 + x * (1.0 / 6.0 + x * (1.0 / 24.0))))
    return jnp.where(x > -0.05, series, 1.0 - jnp.exp(x))


def _nn(a, b):
    return lax.dot_general(a, b, (((1,), (0,)), ((), ())), preferred_element_type=F32)


def _nt(a, b):
    return lax.dot_general(a, b, (((1,), (1,)), ((), ())), preferred_element_type=F32)


def _tn(a, b):
    return lax.dot_general(a, b, (((0,), (0,)), ((), ())), preferred_element_type=F32)


def _b(x):
    return x.astype(BF16)


def _split3(x):
    hi = x.astype(BF16)
    r1 = x - hi.astype(F32)
    mid = r1.astype(BF16)
    lo = (r1 - mid.astype(F32)).astype(BF16)
    return hi, mid, lo


def _dot3(f, a, b):
    ah, am, _ = _split3(a)
    bh, bm, _ = _split3(b)
    return f(ah, bh) + (f(ah, bm) + f(am, bh))


def _dot_exact_lhs(f, a_bf16, b):
    bh, bm, bl = _split3(b)
    return f(a_bf16, bh) + (f(a_bf16, bm) + f(a_bf16, bl))


def _rot(x):
    w = x.shape[-1]
    lane = lax.broadcasted_iota(jnp.int32, (1, w), 1)
    return jnp.where((lane & 63) < 32, pltpu.roll(x, w - 32, 1), pltpu.roll(x, 32, 1))


def _conv_taps(ext, n):
    return [pltpu.roll(ext, 3 - k, 0)[8:8 + n] if k < 3 else ext[8:8 + n] for k in range(CONV_WIDTH)]


def _conv_taps_t(ext, n):
    m = ext.shape[0]
    return [pltpu.roll(ext, m - (3 - k), 0)[0:n] if k < 3 else ext[0:n] for k in range(CONV_WIDTH)]


def _scan_steps(a, b, pos, span, shifts, reverse):
    n = a.shape[0]
    for s in shifts:
        if reverse:
            a_sh = pltpu.roll(a, n - s, 0)
            b_sh = pltpu.roll(b, n - s, 0)
            ok = pos < (span - s)
        else:
            a_sh = pltpu.roll(a, s, 0)
            b_sh = pltpu.roll(b, s, 0)
            ok = pos >= s
        b = jnp.where(ok, a * b_sh + b, b)
        a = jnp.where(ok, a * a_sh, a)
    return a, b


def _scan_lin(a, b, reverse):
    n = a.shape[0]
    shifts = []
    s = 1
    while s < n:
        shifts.append(s)
        s *= 2
    return _scan_steps(a, b, lax.broadcasted_iota(jnp.int32, (n, 1), 0), n, shifts, reverse)


class _Side:
    def __init__(self, inputs, out_shapes, n_sems, start, finish):
        self.inputs, self.out_shapes, self.n_sems, self.start, self.finish = inputs, out_shapes, n_sems, start, finish


def _matmul(a, b, *, ta, tb, tm, tn, tk, out_dtype, name, extra=None, alpha=0.0, out_blocks=None, side=None):
    if ta:
        K, M = a.shape
    else:
        M, K = a.shape
    if tb:
        N, K2 = b.shape
    else:
        K2, N = b.shape
    assert K == K2 and M % tm == 0 and N % tn == 0 and K % tk == 0, (a.shape, b.shape, tm, tn, tk)
    nk = K // tk
    ca = 0 if ta else 1
    cb = 1 if tb else 0
    has_extra = extra is not None

    assert nk == 1 or out_dtype == F32
    n_in = 2 + int(has_extra)
    ns_in = len(side.inputs) if side else 0
    ns_out = len(side.out_shapes) if side else 0
    grid = (M // tm, N // tn, nk)

    def body(*refs):
        a_ref, b_ref = refs[0], refs[1]
        e_ref = refs[2] if has_extra else None
        o_ref = refs[n_in + ns_in]
        k = pl.program_id(2)
        if side:
            s_in = refs[n_in:n_in + ns_in]
            s_out = refs[n_in + ns_in + 1:n_in + ns_in + 1 + ns_out]
            ssem, rsem = refs[-2], refs[-1]
            i, j = pl.program_id(0), pl.program_id(1)

            @pl.when((i == 0) & (j == 0) & (k == 0))
            def _():
                side.start(s_in, s_out, ssem, rsem)

            @pl.when((i == grid[0] - 1) & (j == grid[1] - 1) & (k == grid[2] - 1))
            def _():
                side.finish(s_in, s_out, ssem, rsem)

        part = lax.dot_general(a_ref[...], b_ref[...], (((ca,), (cb,)), ((), ())), preferred_element_type=F32)
        if nk == 1:
            if e_ref is not None:
                part = part + alpha * e_ref[...]
            o_ref[...] = part.astype(o_ref.dtype)
            return

        @pl.when(k == 0)
        def _():
            o_ref[...] = part

        @pl.when((k > 0) & (k < nk - 1))
        def _():
            o_ref[...] += part

        @pl.when(k == nk - 1)
        def _():
            last = o_ref[...] + part
            if e_ref is not None:
                last = last + alpha * e_ref[...]
            o_ref[...] = last

    a_spec = (pl.BlockSpec((tk, tm), lambda i, j, k: (k, i)) if ta
              else pl.BlockSpec((tm, tk), lambda i, j, k: (i, k)))
    b_spec = (pl.BlockSpec((tn, tk), lambda i, j, k: (j, k)) if tb
              else pl.BlockSpec((tk, tn), lambda i, j, k: (k, j)))
    e_spec = pl.BlockSpec((tm, tn), lambda i, j, k: (i, j))
    if out_blocks is None:
        o_spec, o_shape = e_spec, (M, N)
    else:
        o_shape, o_block, o_map = out_blocks
        o_spec = pl.BlockSpec(o_block, lambda i, j, k: o_map(i, j))
    in_specs = [a_spec, b_spec] + ([e_spec] if has_extra else [])
    args = (a, b) + ((extra,) if has_extra else ())
    if not side:
        return _pcall(
            body, name=name, grid=grid, in_specs=in_specs, out_specs=o_spec,
            out_shape=jax.ShapeDtypeStruct(o_shape, out_dtype),
            compiler_params=_cp(("parallel", "parallel", "arbitrary")),
        )(*args)
    outs = _pcall(
        body, name=name, grid=grid, in_specs=in_specs + [HBM_SPEC] * ns_in,
        out_specs=[o_spec] + [HBM_SPEC] * ns_out,
        out_shape=[jax.ShapeDtypeStruct(o_shape, out_dtype)] + list(side.out_shapes),
        scratch_shapes=[pltpu.SemaphoreType.DMA((side.n_sems,)), pltpu.SemaphoreType.DMA((side.n_sems,))],
        compiler_params=_cp(("arbitrary", "arbitrary", "arbitrary")),
    )(*args, *side.inputs)
    return outs[0], outs[1:]


def _outproj(ymix, wo, x, *, tm, name):
    S_, D = x.shape

    def body(y_ref, w_ref, x_ref, z_ref):
        z_ref[...] = DEEPNORM_ALPHA * x_ref[...] + _nn(y_ref[...], w_ref[...])

    return _pcall(
        body, name=name, grid=(S_ // tm,),
        in_specs=[pl.BlockSpec((tm, MIX_WIDTH), lambda i: (i, 0)),
                  pl.BlockSpec((MIX_WIDTH, D), lambda i: (0, 0)),
                  pl.BlockSpec((tm, D), lambda i: (i, 0))],
        out_specs=pl.BlockSpec((tm, D), lambda i: (i, 0)),
        out_shape=jax.ShapeDtypeStruct((S_, D), F32),
        compiler_params=_cp(("parallel",)),
    )(ymix, wo, x)


def _ln_stats(z):
    mu = jnp.mean(z, -1, keepdims=True)
    zc = z - mu
    var = jnp.mean(zc * zc, -1, keepdims=True)
    rstd = lax.rsqrt(var + LN_EPS)
    return zc * rstd, rstd


def _ln_fwd(z, g, b, *, tm, name):
    S_, D = z.shape

    def body(z_ref, g_ref, b_ref, y_ref, yb_ref):
        xh, _ = _ln_stats(z_ref[...])
        y = xh * g_ref[...] + b_ref[...]
        y_ref[...] = y
        yb_ref[...] = y.astype(BF16)

    row = pl.BlockSpec((tm, D), lambda i: (i, 0))
    vec = pl.BlockSpec((1, D), lambda i: (0, 0))
    return _pcall(
        body, name=name, grid=(S_ // tm,), in_specs=[row, vec, vec], out_specs=[row, row],
        out_shape=[jax.ShapeDtypeStruct((S_, D), F32), jax.ShapeDtypeStruct((S_, D), BF16)],
        compiler_params=_cp(("parallel",)),
    )(z, g.reshape(1, D), b.reshape(1, D))


def _ln_bwd(z, g, b, other, *, from_target, tm, name):
    S_, D = z.shape

    def body(z_ref, g_ref, b_ref, o_ref, dz_ref, dzb_ref, dg_ref, db_ref, loss_ref):
        i = pl.program_id(0)

        @pl.when(i == 0)
        def _():
            dg_ref[...] = jnp.zeros_like(dg_ref)
            db_ref[...] = jnp.zeros_like(db_ref)
            loss_ref[...] = jnp.zeros_like(loss_ref)

        xh, rstd = _ln_stats(z_ref[...])
        gam = g_ref[...]
        if from_target:
            err = xh * gam + b_ref[...] - o_ref[...]
            per_tok = jnp.mean(err * err, -1, keepdims=True)
            loss_ref[...] += 0.5 * jnp.sum(per_tok, 0, keepdims=True)
            dy = err * (1.0 / D)
        else:
            dy = o_ref[...]
        dxh = dy * gam
        m1 = jnp.mean(dxh, -1, keepdims=True)
        m2 = jnp.mean(dxh * xh, -1, keepdims=True)
        dz = rstd * (dxh - m1 - xh * m2)
        dz_ref[...] = dz
        dzb_ref[...] = dz.astype(BF16)
        dg_ref[...] += jnp.sum(dy * xh, 0, keepdims=True)
        db_ref[...] += jnp.sum(dy, 0, keepdims=True)

    row = pl.BlockSpec((tm, D), lambda i: (i, 0))
    vec = pl.BlockSpec((1, D), lambda i: (0, 0))
    one = pl.BlockSpec((1, 1), lambda i: (0, 0))
    return _pcall(
        body, name=name, grid=(S_ // tm,), in_specs=[row, vec, vec, row],
        out_specs=[row, row, vec, vec, one],
        out_shape=[jax.ShapeDtypeStruct((S_, D), F32), jax.ShapeDtypeStruct((S_, D), BF16),
                   jax.ShapeDtypeStruct((1, D), F32), jax.ShapeDtypeStruct((1, D), F32),
                   jax.ShapeDtypeStruct((1, 1), F32)],
        compiler_params=_cp(("arbitrary",)),
    )(z, g.reshape(1, D), b.reshape(1, D), other)


def _attn_masks(i, sk_ref):
    ri = lax.broadcasted_iota(jnp.int32, (512, 256), 0)
    cj = lax.broadcasted_iota(jnp.int32, (512, 256), 1)
    diff = (ri & 127) - cj + 128
    band = (diff >= 0) & (diff < WINDOW)
    bias = jnp.where(band, 0.0, -jnp.inf)
    bias0 = jnp.where(band & ((i > 0) | (cj >= 128)), 0.0, -jnp.inf)
    grp = lax.broadcasted_iota(jnp.int32, (512, 1), 0) >> 7
    skvs = []
    for h in range(A_KV_HEADS):
        skv = jnp.zeros((512, 1), F32)
        for g in range(4):
            skv = jnp.where(grp == g, sk_ref[h * 4 + g], skv)
        skvs.append(skv)
    return bias0, bias, skvs


def _attn_common(masks, b, h, qr, kd, vd):
    lane = lax.broadcasted_iota(jnp.int32, (1, 128), 1)
    lof = (lane < 64).astype(F32)
    hif = 1.0 - lof
    r0 = b * 128
    skv = masks[2][h]
    pairs = [qr[r0:r0 + 128, h * 256 + p * 128:h * 256 + (p + 1) * 128] for p in (0, 1)]
    qs = _b(jnp.concatenate([pairs[0] * lof, pairs[0] * hif, pairs[1] * lof, pairs[1] * hif], 0))
    k2 = kd[h][r0:r0 + 256]
    v2 = vd[h][r0:r0 + 256]
    s = _nt(qs, k2) * (A_HEAD_DIM ** -0.5) + (masks[0] if b == 0 else masks[1])
    m = jnp.maximum(jnp.max(s, 1, keepdims=True), skv)
    p = jnp.exp(s - m)
    esk = jnp.exp(skv - m)
    rz = 1.0 / (jnp.sum(p, 1, keepdims=True) + esk)
    prob = p * rz
    o4 = _nn(_b(prob), v2)
    return lof, hif, qs, k2, v2, prob, esk * rz, o4


def _attn_prep(T, q_ref, k_ref, v_ref, c_ref, s_ref, kprev, vprev):
    C = c_ref[...]
    Sg = s_ref[...]
    C4 = jnp.concatenate([C] * 4, 1)
    S4 = jnp.concatenate([Sg] * 4, 1)
    q = q_ref[...]
    qr = q * C4 + _rot(q) * S4
    k = k_ref[...]
    kr = k * C + _rot(k) * Sg
    v = v_ref[...]
    kext = jnp.concatenate([kprev[...], kr], 0)
    vext = jnp.concatenate([vprev[...], v], 0)
    kprev[...] = kr[T - 128:]
    vprev[...] = v[T - 128:]
    lo = lax.broadcasted_iota(jnp.int32, (1, 128), 1) < 64
    kroll = pltpu.roll(kext, 64, 1)
    vroll = pltpu.roll(vext, 64, 1)
    kd = [_b(jnp.where(lo, kext, kroll)), _b(jnp.where(lo, kroll, kext))]
    vd = [_b(jnp.where(lo, vext, vroll)), _b(jnp.where(lo, vroll, vext))]
    return C, Sg, C4, S4, qr, kd, vd


def _attn_specs(T):
    return [pl.BlockSpec(memory_space=pltpu.SMEM),
            pl.BlockSpec((T, 512), lambda i: (i, OFF_AQ // 512)),
            pl.BlockSpec((T, 512), lambda i: (i, OFF_AZ // 512)),
            pl.BlockSpec((T, 128), lambda i: (i, OFF_AK // 128)),
            pl.BlockSpec((T, 128), lambda i: (i, OFF_AV // 128)),
            pl.BlockSpec((T, 128), lambda i: (i, 0)),
            pl.BlockSpec((T, 128), lambda i: (i, 0))]


def _attn_fwd(proj, rope_c, rope_s, sinks, ymix, *, T, name):
    S_ = proj.shape[0]
    nb = T // 128

    def body(sk_ref, q_ref, z_ref, k_ref, v_ref, c_ref, s_ref, _, y_ref, kprev, vprev):
        i = pl.program_id(0)

        @pl.when(i == 0)
        def _():
            kprev[...] = jnp.zeros_like(kprev)
            vprev[...] = jnp.zeros_like(vprev)

        _, _, _, _, qr, kd, vd = _attn_prep(T, q_ref, k_ref, v_ref, c_ref, s_ref, kprev, vprev)
        masks = _attn_masks(i, sk_ref)
        for b in range(nb):
            r0 = b * 128
            for h in range(2):
                lof, hif, _, _, _, _, _, o4 = _attn_common(masks, b, h, qr, kd, vd)
                for p in range(2):
                    cs = slice(h * 256 + p * 128, h * 256 + (p + 1) * 128)
                    o = o4[2 * p * 128:(2 * p + 1) * 128] * lof + o4[(2 * p + 1) * 128:(2 * p + 2) * 128] * hif
                    y_ref[r0:r0 + 128, cs] = (o * _silu(z_ref[r0:r0 + 128, cs])).astype(BF16)

    return _pcall(
        body, name=name, grid=(S_ // T,), in_specs=_attn_specs(T) + [ANY_SPEC],
        out_specs=pl.BlockSpec((T, 512), lambda i: (i, MIX_A // 512)),
        out_shape=jax.ShapeDtypeStruct(ymix.shape, BF16),
        input_output_aliases={7: 0},
        scratch_shapes=[pltpu.VMEM((128, 128), F32), pltpu.VMEM((128, 128), F32)],
        compiler_params=_cp(("arbitrary",)),
    )(sinks, proj, proj, proj, proj, rope_c, rope_s, ymix)


def _attn_bwd(proj, rope_c, rope_s, sinks, dymix, *, T, name):
    S_ = proj.shape[0]
    nb = T // 128
    nt = S_ // T

    def body(sk_ref, q_ref, z_ref, k_ref, v_ref, c_ref, s_ref, dy_ref,
             dp_ref, dk_ref, dv_ref, dkt_ref, dvt_ref, dsk_ref,
             kprev, vprev, cprev, sprev, dkacc, dvacc, dqacc):
        i = pl.program_id(0)

        @pl.when(i == 0)
        def _():
            kprev[...] = jnp.zeros_like(kprev)
            vprev[...] = jnp.zeros_like(vprev)
            cprev[...] = jnp.zeros_like(cprev)
            sprev[...] = jnp.zeros_like(sprev)
            dkacc[...] = jnp.zeros_like(dkacc)
            dvacc[...] = jnp.zeros_like(dvacc)
            dsk_ref[...] = jnp.zeros_like(dsk_ref)

        @pl.when(i > 0)
        def _():
            dkacc[0:128, :] = dkacc[T:T + 128, :]
            dvacc[0:128, :] = dvacc[T:T + 128, :]
            dkacc[128:, :] = jnp.zeros((T, 128), F32)
            dvacc[128:, :] = jnp.zeros((T, 128), F32)

        C, Sg, C4, S4, qr, kd, vd = _attn_prep(T, q_ref, k_ref, v_ref, c_ref, s_ref, kprev, vprev)
        masks = _attn_masks(i, sk_ref)
        lane = lax.broadcasted_iota(jnp.int32, (1, 128), 1)
        for b in range(nb):
            r0 = b * 128
            for h in range(2):
                lof, hif, qs, k2, v2, prob, psink, o4 = _attn_common(masks, b, h, qr, kd, vd)
                dos = []
                for p in range(2):
                    cs = slice(h * 256 + p * 128, h * 256 + (p + 1) * 128)
                    o = o4[2 * p * 128:(2 * p + 1) * 128] * lof + o4[(2 * p + 1) * 128:(2 * p + 2) * 128] * hif
                    zc = z_ref[r0:r0 + 128, cs]
                    dyc = dy_ref[r0:r0 + 128, cs]
                    dp_ref[r0:r0 + 128, 512 + cs.start:512 + cs.stop] = (dyc * o * _dsilu(zc)).astype(BF16)
                    do = dyc * _silu(zc)
                    dos += [do * lof, do * hif]
                dos = jnp.concatenate(dos, 0)
                os_ = jnp.concatenate([o4[0:128] * lof, o4[128:256] * hif, o4[256:384] * lof, o4[384:512] * hif], 0)
                delta = jnp.sum(dos * os_, 1, keepdims=True)
                dosb = _b(dos)
                dp = _nt(dosb, v2)
                ds = prob * (dp - delta)
                dsv = -psink * delta
                for g in range(4):
                    sg = jnp.sum(dsv[g * 128:(g + 1) * 128], 0, keepdims=True)
                    hd = h * 4 + g
                    dsk_ref[hd:hd + 1, :] += jnp.broadcast_to(sg, (1, 128))
                dsb = _b(ds * (A_HEAD_DIM ** -0.5))
                dqs = _nn(dsb, k2)
                for p in range(2):
                    cs = slice(h * 256 + p * 128, h * 256 + (p + 1) * 128)
                    dqacc[r0:r0 + 128, cs] = (dqs[2 * p * 128:(2 * p + 1) * 128] * lof
                                              + dqs[(2 * p + 1) * 128:(2 * p + 2) * 128] * hif)
                dkdup = _tn(dsb, qs)
                dvdup = _tn(_b(prob), dosb)
                half = (lane < 64) if h == 0 else (lane >= 64)
                dkacc[r0:r0 + 256, :] += jnp.where(half, dkdup + pltpu.roll(dkdup, 64, 1), 0.0)
                dvacc[r0:r0 + 256, :] += jnp.where(half, dvdup + pltpu.roll(dvdup, 64, 1), 0.0)
        dqr = dqacc[...]
        dp_ref[:, 0:512] = (dqr * C4 + _rot(dqr * S4)).astype(BF16)
        cext = jnp.concatenate([cprev[...], C], 0)
        sext = jnp.concatenate([sprev[...], Sg], 0)
        dke = dkacc[...]
        dkp = dke * cext + _rot(dke * sext)
        dk_ref[...] = dkp[0:T].astype(BF16)
        dkt_ref[...] = dkp[T:T + 128].astype(BF16)
        dve = dvacc[...]
        dv_ref[...] = dve[0:T].astype(BF16)
        dvt_ref[...] = dve[T:T + 128].astype(BF16)
        cprev[...] = C[T - 128:]
        sprev[...] = Sg[T - 128:]

    nar = pl.BlockSpec((T, 128), lambda i: (i, 0))
    tail = pl.BlockSpec((128, 128), lambda i: (0, 0))
    return _pcall(
        body, name=name, grid=(nt,),
        in_specs=_attn_specs(T) + [pl.BlockSpec((T, 512), lambda i: (i, MIX_A // 512))],
        out_specs=[pl.BlockSpec((T, 1024), lambda i: (i, OFF_AQ // 1024)), nar, nar, tail, tail,
                   pl.BlockSpec((8, 128), lambda i: (0, 0))],
        out_shape=[jax.ShapeDtypeStruct((S_, NP), BF16),
                   jax.ShapeDtypeStruct((S_, 128), BF16), jax.ShapeDtypeStruct((S_, 128), BF16),
                   jax.ShapeDtypeStruct((128, 128), BF16), jax.ShapeDtypeStruct((128, 128), BF16),
                   jax.ShapeDtypeStruct((8, 128), F32)],
        scratch_shapes=[pltpu.VMEM((128, 128), F32)] * 4
        + [pltpu.VMEM((T + 128, 128), F32), pltpu.VMEM((T + 128, 128), F32), pltpu.VMEM((T, 512), F32)],
        compiler_params=_cp(("arbitrary",)),
    )(sinks, proj, proj, proj, proj, rope_c, rope_s, dymix)


def _rg_gates(xr, wa_ref, ba_ref, wx_ref, bx_ref, lam_ref):
    xb = _b(xr)
    pre_a = jnp.concatenate([_nn(xb[:, n * 128:(n + 1) * 128], wa_ref[n]) for n in range(R_BLOCKS)], 1) + ba_ref[...]
    pre_x = jnp.concatenate([_nn(xb[:, n * 128:(n + 1) * 128], wx_ref[n]) for n in range(R_BLOCKS)], 1) + bx_ref[...]
    r = _sigmoid(pre_a)
    ig = _sigmoid(pre_x)
    sp = _softplus(-lam_ref[...])
    log_a = -R_C * r * sp
    a = jnp.exp(log_a)
    mult = jnp.sqrt(_one_minus_exp(2.0 * log_a))
    return xb, r, ig, sp, a, mult


def _rg_param_specs():
    C = R_WIDTH
    vec = pl.BlockSpec((1, C), lambda i: (0, 0))
    blk = pl.BlockSpec((R_BLOCKS, 128, 128), lambda i: (0, 0, 0))
    return [pl.BlockSpec((CONV_WIDTH, C), lambda i: (0, 0)), vec, blk, vec, blk, vec, vec]


def _rglru_fwd(proj, cw, cb, wa, ba, wx, bx, lam, *, T, name):
    S_ = proj.shape[0]
    C = R_WIDTH

    def body(rx_ref, rz_ref, cw_ref, cb_ref, wa_ref, ba_ref, wx_ref, bx_ref, lam_ref,
             h_ref, y_ref, halo, hcar):
        i = pl.program_id(0)

        @pl.when(i == 0)
        def _():
            halo[...] = jnp.zeros_like(halo)
            hcar[...] = jnp.zeros_like(hcar)

        rx = rx_ref[...]
        ext = jnp.concatenate([halo[...], rx], 0)
        halo[...] = rx[T - 8:]
        taps = _conv_taps(ext, T)
        xr = cb_ref[...] + sum(cw_ref[k:k + 1, :] * taps[k] for k in range(CONV_WIDTH))
        _, _, ig, _, a, mult = _rg_gates(xr, wa_ref, ba_ref, wx_ref, bx_ref, lam_ref)
        u = mult * (ig * xr)
        acum, hloc = _scan_lin(a, u, False)
        h = hloc + acum * hcar[0:1, :]
        hcar[...] = jnp.broadcast_to(h[T - 1:T, :], (8, C))
        h_ref[...] = h
        y_ref[...] = (h * _silu(rz_ref[...])).astype(BF16)

    row = pl.BlockSpec((T, C), lambda i: (i, 0))
    return _pcall(
        body, name=name, grid=(S_ // T,),
        in_specs=[pl.BlockSpec((T, C), lambda i: (i, OFF_RX // C)),
                  pl.BlockSpec((T, C), lambda i: (i, OFF_RZ // C))] + _rg_param_specs(),
        out_specs=[row, pl.BlockSpec((T, C), lambda i: (i, MIX_R // C))],
        out_shape=[jax.ShapeDtypeStruct((S_, C), F32), jax.ShapeDtypeStruct((S_, MIX_WIDTH), BF16)],
        scratch_shapes=[pltpu.VMEM((8, C), F32), pltpu.VMEM((8, C), F32)],
        compiler_params=_cp(("arbitrary",)),
    )(proj, proj, cw, cb.reshape(1, C), _b(wa), ba.reshape(1, C), _b(wx), bx.reshape(1, C), lam.reshape(1, C))


def _rglru_bwd(proj, h, dymix, dproj, cw, cb, wa, ba, wx, bx, lam, *, T, name):
    S_ = proj.shape[0]
    C = R_WIDTH
    nt = S_ // T
    t8 = T // 8

    def body(rx_ref, rxp_ref, rz_ref, h_ref, hp_ref, dy_ref,
             cw_ref, cb_ref, wa_ref, ba_ref, wx_ref, bx_ref, lam_ref, wat_ref, wxt_ref,
             _, dp_ref, dcw_ref, dcb_ref, dwa_ref, dba_ref, dwx_ref, dbx_ref, dlam_ref,
             afirst, gfirst, dhalo):
        i = pl.program_id(0)
        first_tile = (i == nt - 1)

        @pl.when(i == 0)
        def _():
            afirst[...] = jnp.zeros_like(afirst)
            gfirst[...] = jnp.zeros_like(gfirst)
            dhalo[...] = jnp.zeros_like(dhalo)
            for r in (dcw_ref, dcb_ref, dwa_ref, dba_ref, dwx_ref, dbx_ref, dlam_ref):
                r[...] = jnp.zeros_like(r)

        keep = jnp.where(first_tile, 0.0, 1.0)
        rx = rx_ref[...]
        ext = jnp.concatenate([rxp_ref[...] * keep, rx], 0)
        taps = _conv_taps(ext, T)
        xr = cb_ref[...] + sum(cw_ref[k:k + 1, :] * taps[k] for k in range(CONV_WIDTH))
        xb, r, ig, sp, a, mult = _rg_gates(xr, wa_ref, ba_ref, wx_ref, bx_ref, lam_ref)
        hh = h_ref[...]
        rz = rz_ref[...]
        dy = dy_ref[...]
        dp_ref[:, C:2 * C] = (dy * hh * _dsilu(rz)).astype(BF16)
        dh = dy * _silu(rz)
        row = lax.broadcasted_iota(jnp.int32, (T, 1), 0)
        c = jnp.where(row == T - 1, afirst[0:1, :], pltpu.roll(a, T - 1, 0))
        ccum, gloc = _scan_lin(c, dh, True)
        g = gloc + ccum * gfirst[0:1, :]
        afirst[...] = jnp.broadcast_to(a[0:1, :], (8, C))
        gfirst[...] = jnp.broadcast_to(g[0:1, :], (8, C))
        hprev = jnp.where(row == 0, hp_ref[7:8, :] * keep, pltpu.roll(hh, 1, 0))
        da = g * hprev
        gx = ig * xr
        dgx = g * mult
        dmult = g * gx
        dlog_a = da * a - dmult * (a * a) * lax.rsqrt(mult * mult)
        dpre_a = dlog_a * (-R_C * sp) * r * (1.0 - r)
        dpre_x = dgx * xr * ig * (1.0 - ig)
        dlam_ref[...] += jnp.sum(dlog_a * (-R_C * r), 0, keepdims=True) * (-_sigmoid(-lam_ref[...]))
        dab = _b(dpre_a)
        dxb = _b(dpre_x)
        dxr = dgx * ig + jnp.concatenate(
            [_nn(dab[:, n * 128:(n + 1) * 128], wat_ref[n]) + _nn(dxb[:, n * 128:(n + 1) * 128], wxt_ref[n])
             for n in range(R_BLOCKS)], 1)
        for n in range(R_BLOCKS):
            cs = slice(n * 128, (n + 1) * 128)
            dwa_ref[n] += _tn(xb[:, cs], dab[:, cs])
            dwx_ref[n] += _tn(xb[:, cs], dxb[:, cs])
        dba_ref[...] += jnp.sum(dpre_a, 0, keepdims=True)
        dbx_ref[...] += jnp.sum(dpre_x, 0, keepdims=True)
        dcb_ref[...] += jnp.sum(dxr, 0, keepdims=True)
        for k in range(CONV_WIDTH):
            dcw_ref[k:k + 1, :] += jnp.sum(dxr * taps[k], 0, keepdims=True)
        ext2 = jnp.concatenate([dxr, dhalo[...]], 0)
        tt = _conv_taps_t(ext2, T)
        dp_ref[:, 0:C] = sum(cw_ref[k:k + 1, :] * tt[k] for k in range(CONV_WIDTH)).astype(BF16)
        dhalo[...] = dxr[0:8]

    def rev(i):
        return nt - 1 - i

    def prev8(i):
        return jnp.maximum(rev(i) * t8 - 1, 0)

    vec = pl.BlockSpec((1, C), lambda i: (0, 0))
    blk = pl.BlockSpec((R_BLOCKS, 128, 128), lambda i: (0, 0, 0))
    row = pl.BlockSpec((T, C), lambda i: (rev(i), 0))
    wat = _b(jnp.swapaxes(wa, 1, 2))
    wxt = _b(jnp.swapaxes(wx, 1, 2))
    return _pcall(
        body, name=name, grid=(nt,),
        in_specs=[pl.BlockSpec((T, C), lambda i: (rev(i), OFF_RX // C)),
                  pl.BlockSpec((8, C), lambda i: (prev8(i), OFF_RX // C)),
                  pl.BlockSpec((T, C), lambda i: (rev(i), OFF_RZ // C)),
                  row,
                  pl.BlockSpec((8, C), lambda i: (prev8(i), 0)),
                  pl.BlockSpec((T, C), lambda i: (rev(i), MIX_R // C)),
                  ] + _rg_param_specs() + [blk, blk, ANY_SPEC],
        out_specs=[pl.BlockSpec((T, 2 * C), lambda i: (rev(i), OFF_RX // (2 * C))),
                   pl.BlockSpec((CONV_WIDTH, C), lambda i: (0, 0)), vec, blk, vec, blk, vec, vec],
        out_shape=[jax.ShapeDtypeStruct(dproj.shape, BF16),
                   jax.ShapeDtypeStruct((CONV_WIDTH, C), F32), jax.ShapeDtypeStruct((1, C), F32),
                   jax.ShapeDtypeStruct((R_BLOCKS, 128, 128), F32), jax.ShapeDtypeStruct((1, C), F32),
                   jax.ShapeDtypeStruct((R_BLOCKS, 128, 128), F32), jax.ShapeDtypeStruct((1, C), F32),
                   jax.ShapeDtypeStruct((1, C), F32)],
        input_output_aliases={15: 0},
        scratch_shapes=[pltpu.VMEM((8, C), F32)] * 3,
        compiler_params=_cp(("arbitrary",)),
    )(proj, proj, proj, h, h, dymix, cw, cb.reshape(1, C), _b(wa), ba.reshape(1, C), _b(wx), bx.reshape(1, C),
      lam.reshape(1, C), wat, wxt, dproj)


GW3 = 3 * G_WIDTH


def _lane_col(x, lane_idx):
    lane = lax.broadcasted_iota(jnp.int32, (1, x.shape[1]), 1)
    return jnp.sum(jnp.where(lane == lane_idx, x, 0.0), 1, keepdims=True)


def _gdn_pre(ext, T, cw_ref, gba, pv_ref):
    taps = _conv_taps(ext, T)
    c = sum(cw_ref[k:k + 1, :] * taps[k] for k in range(CONV_WIDTH))
    qkv = _silu(c)
    beta = _sigmoid(gba)
    sarg = gba + pv_ref[1:2, :]
    nea = -jnp.exp(pv_ref[0:1, :])
    gdec = nea * _softplus(sarg)
    ri = lax.broadcasted_iota(jnp.int32, (T, T), 0)
    cj = lax.broadcasted_iota(jnp.int32, (T, T), 1)
    same = (ri >> 6) == (cj >> 6)
    ltri = jnp.where((ri >= cj) & same, 1.0, 0.0).astype(BF16)
    gc = _dot_exact_lhs(_nn, ltri, gdec)
    return taps, c, qkv, beta, sarg, nea, gdec, gc


def _gdn_masks():
    ri = lax.broadcasted_iota(jnp.int32, (128, 128), 0)
    cj = lax.broadcasted_iota(jnp.int32, (128, 128), 1)
    same = (ri >> 6) == (cj >> 6)
    return (ri >= cj) & same, (ri > cj) & same, ri == cj


def _lockstep(gens):
    out = [None] * len(gens)
    live = list(range(len(gens)))
    while live:
        still = []
        for k in live:
            try:
                next(gens[k])
                still.append(k)
            except StopIteration as stop:
                out[k] = stop.value
        live = still
    return out


def _gdn_chunk(qkv, beta, gc, rs, h, tm=None):
    tril, strict, eye = _gdn_masks()
    rowi = lax.broadcasted_iota(jnp.int32, (128, 1), 0)
    lane = lax.broadcasted_iota(jnp.int32, (1, 128), 1)
    qh = qkv[rs, h * 128:(h + 1) * 128]
    kh = qkv[rs, 512 + h * 128:512 + (h + 1) * 128]
    vh = qkv[rs, 1024 + h * 128:1024 + (h + 1) * 128]
    rq = lax.rsqrt(jnp.sum(qh * qh, 1, keepdims=True) + RMS_EPS)
    rk = lax.rsqrt(jnp.sum(kh * kh, 1, keepdims=True) + RMS_EPS)
    qn = qh * (rq * (G_HEAD_DIM ** -0.5))
    kn = kh * rk
    gcb = gc[rs]
    gcol = _lane_col(gcb, 4 + h)
    bcol = _lane_col(beta[rs], h)
    grow = _dot_exact_lhs(_nt, jnp.ones((128, 128), BF16), jnp.where(lane == 4 + h, gcb, 0.0))
    D = jnp.where(tril, jnp.exp(jnp.minimum(gcol - grow, 0.0)), 0.0)
    kb = kn * bcol
    vb = vh * bcol
    knb = _b(kn)
    A = _nt(_b(kb), knb)
    Bm = _nt(_b(qn), knb)
    yield
    if tm is None:
        N = jnp.where(strict, -(A * D), 0.0)
        tm = jnp.where(eye, 1.0, 0.0) + N
        npow = N
        for _ in range(5):
            npow = _dot3(_nn, npow, npow)
            yield
            tm = tm + _dot3(_nn, tm, npow)
            yield
    eg = jnp.exp(gcol)
    u = _dot3(_nn, tm, vb)
    w = _dot3(_nn, tm, kb * eg)
    yield
    qk = jnp.where(tril, Bm * D, 0.0)
    qd = qn * eg
    gla = jnp.sum(jnp.where(rowi == 63, gcol, 0.0), 0, keepdims=True)
    glb = jnp.sum(jnp.where(rowi == 127, gcol, 0.0), 0, keepdims=True)
    ed = jnp.exp(jnp.where(rowi < 64, gla, glb) - gcol)
    kd = kn * ed
    return dict(qh=qh, kh=kh, vh=vh, rq=rq, rk=rk, qn=qn, kn=kn, gcol=gcol, bcol=bcol, D=D, A=A, Bm=Bm,
                tm=tm, eg=eg, ed=ed, u=u, w=w, qk=qk, qd=qd, kd=kd, kb=kb, vb=vb,
                gla=jnp.exp(gla), glb=jnp.exp(glb))


def _gdn_scan(q, sa):
    sab = _b(sa)
    wb = _b(q["w"])
    vna = q["u"] - _nn(wb, sab)
    yield
    sb = sa * q["gla"] + _tn(_b(q["kd"][0:64]), _b(vna[0:64]))
    yield
    sbb = _b(sb)
    vnb = q["u"] - _nn(wb, sbb)
    yield
    sn = sb * q["glb"] + _tn(_b(q["kd"][64:128]), _b(vnb[64:128]))
    yield
    vn = jnp.concatenate([vna[0:64], vnb[64:128]], 0)
    qdb = _b(q["qd"])
    o = jnp.concatenate([_nn(qdb[0:64], sab), _nn(qdb[64:128], sbb)], 0) + _nn(_b(q["qk"]), _b(vn))
    return sb, sn, vn, o


def _gdn_param_specs():
    return [pl.BlockSpec((CONV_WIDTH, GW3), lambda i: (0, 0)),
            pl.BlockSpec((8, 128), lambda i: (0, 0)),
            pl.BlockSpec((1, 128), lambda i: (0, 0))]


def _gdn_pvec(a_log, dt_bias):
    z = jnp.zeros((8, 128), F32)
    return z.at[0, 4:8].set(a_log).at[1, 4:8].set(dt_bias)


def _gdn_fwd(proj, cw, a_log, dt_bias, nw, ymix, *, T, name):
    S_ = proj.shape[0]
    nu = T // 128

    def body(x_ref, z_ref, g_ref, cw_ref, pv_ref, nw_ref, _, y_ref, st_ref, tm_ref, halo, state):
        i = pl.program_id(0)

        @pl.when(i == 0)
        def _():
            halo[...] = jnp.zeros_like(halo)
            state[...] = jnp.zeros_like(state)

        x = x_ref[...]
        ext = jnp.concatenate([halo[...], x], 0)
        halo[...] = x[T - 8:]
        _, _, qkv, beta, _, _, _, gc = _gdn_pre(ext, T, cw_ref, g_ref[...], pv_ref)
        items = [(dc, h) for dc in range(nu) for h in range(G_HEADS)]
        qs = _lockstep([_gdn_chunk(qkv, beta, gc, slice(dc * 128, (dc + 1) * 128), h) for dc, h in items])

        def head_chain(h):
            s = state[h]
            for dc in range(nu):
                rs = slice(dc * 128, (dc + 1) * 128)
                q = qs[dc * G_HEADS + h]
                sb, sn, _, o = yield from _gdn_scan(q, s)
                st_ref[2 * dc, h] = s
                st_ref[2 * dc + 1, h] = sb
                tm_ref[dc, h] = q["tm"]
                s = sn
                yield
                rn = lax.rsqrt(jnp.mean(o * o, 1, keepdims=True) + RMS_EPS)
                cs = slice(h * 128, (h + 1) * 128)
                y_ref[rs, cs] = (o * rn * nw_ref[...] * _silu(z_ref[rs, cs])).astype(BF16)
                yield
            state[h] = s

        _lockstep([head_chain(h) for h in range(G_HEADS)])

    return _pcall(
        body, name=name, grid=(S_ // T,),
        in_specs=[pl.BlockSpec((T, GW3), lambda i: (i, OFF_GQKV // GW3)),
                  pl.BlockSpec((T, 512), lambda i: (i, OFF_GZ // 512)),
                  pl.BlockSpec((T, 128), lambda i: (i, OFF_GBA // 128))] + _gdn_param_specs() + [ANY_SPEC],
        out_specs=[pl.BlockSpec((T, 512), lambda i: (i, MIX_G // 512)),
                   pl.BlockSpec((2 * nu, G_HEADS, 128, 128), lambda i: (i, 0, 0, 0)),
                   pl.BlockSpec((nu, G_HEADS, 128, 128), lambda i: (i, 0, 0, 0))],
        out_shape=[jax.ShapeDtypeStruct(ymix.shape, BF16),
                   jax.ShapeDtypeStruct((S_ // 64, G_HEADS, 128, 128), F32),
                   jax.ShapeDtypeStruct((S_ // 128, G_HEADS, 128, 128), F32)],
        input_output_aliases={6: 0},
        scratch_shapes=[pltpu.VMEM((8, GW3), F32), pltpu.VMEM((G_HEADS, 128, 128), F32)],
        compiler_params=_cp(("arbitrary",)),
    )(proj, proj, proj, cw, _gdn_pvec(a_log, dt_bias), nw.reshape(1, 128), ymix)


def _gdn_bwd(proj, states, tms, dymix, dproj, cw, a_log, dt_bias, nw, *, T, name):
    S_ = proj.shape[0]
    nt = S_ // T
    nu = T // 128
    t8 = T // 8

    def body(x_ref, xp_ref, z_ref, g_ref, st_ref, tm_ref, dy_ref, cw_ref, pv_ref, nw_ref, _,
             dp_ref, dg_ref, dcw_ref, dpv_ref, dnw_ref, dstate, dhalo, dqkv, dbg):
        i = pl.program_id(0)
        first_tile = (i == nt - 1)

        @pl.when(i == 0)
        def _():
            dstate[...] = jnp.zeros_like(dstate)
            dhalo[...] = jnp.zeros_like(dhalo)
            dcw_ref[...] = jnp.zeros_like(dcw_ref)
            dpv_ref[...] = jnp.zeros_like(dpv_ref)
            dnw_ref[...] = jnp.zeros_like(dnw_ref)

        keep = jnp.where(first_tile, 0.0, 1.0)
        ext = jnp.concatenate([xp_ref[...] * keep, x_ref[...]], 0)
        G = g_ref[...]
        taps, c, qkv, beta, sarg, nea, gdec, gc = _gdn_pre(ext, T, cw_ref, G, pv_ref)
        tril, strict, _ = _gdn_masks()
        rowi = lax.broadcasted_iota(jnp.int32, (128, 1), 0)
        lane = lax.broadcasted_iota(jnp.int32, (1, 128), 1)
        ones_b = jnp.ones((128, 128), BF16)
        nwv = nw_ref[...]
        items = [(dc, h) for dc in range(nu) for h in range(G_HEADS)]

        def recompute(dc, h):
            q = yield from _gdn_chunk(qkv, beta, gc, slice(dc * 128, (dc + 1) * 128), h, tm=tm_ref[dc, h])
            sa = st_ref[2 * dc, h]
            sb, _, vn, o = yield from _gdn_scan(q, sa)
            return q, sa, sb, vn, o

        fw = _lockstep([recompute(dc, h) for dc, h in items])
        chain_out = {}

        def head_chain(h):
            dS = dstate[h]
            for dc in reversed(range(nu)):
                rs = slice(dc * 128, (dc + 1) * 128)
                q, sa, sb, vn, o = fw[dc * G_HEADS + h]
                cs = slice(h * 128, (h + 1) * 128)
                zg = z_ref[rs, cs]
                dy = dy_ref[rs, cs]
                rn = lax.rsqrt(jnp.mean(o * o, 1, keepdims=True) + RMS_EPS)
                don = dy * _silu(zg)
                dp_ref[rs, GW3 + cs.start:GW3 + cs.stop] = (dy * (o * rn * nwv) * _dsilu(zg)).astype(BF16)
                dnw_ref[...] += jnp.sum(don * o * rn, 0, keepdims=True)
                tt = don * nwv
                do = rn * (tt - o * (rn * rn) * jnp.mean(tt * o, 1, keepdims=True))
                yield
                dob = _b(do)
                sab, sbb = _b(sa), _b(sb)
                vnb16 = _b(vn)
                dqk = jnp.where(tril, _nt(dob, vnb16), 0.0)
                dvn_o = _tn(_b(q["qk"]), dob)
                dSb16 = _b(dS)
                kdb = _b(q["kd"])
                wb = _b(q["w"])
                qdb = _b(q["qd"])
                yield
                dvn_b = dvn_o[64:128] + _nn(kdb[64:128], dSb16)
                dkd_b = _nt(vnb16[64:128], dSb16)
                dgl_b = jnp.sum(jnp.sum(dS * sb, 1, keepdims=True), 0, keepdims=True)
                yield
                dvn_b16 = _b(dvn_b)
                dw_b = -_nt(dvn_b16, sbb)
                dqd_b = _nt(dob[64:128], sbb)
                dSm = q["glb"] * dS + _tn(qdb[64:128], dob[64:128]) - _tn(wb[64:128], dvn_b16)
                yield
                dSm16 = _b(dSm)
                dvn_a = dvn_o[0:64] + _nn(kdb[0:64], dSm16)
                dkd_a = _nt(vnb16[0:64], dSm16)
                dgl_a = jnp.sum(jnp.sum(dSm * sa, 1, keepdims=True), 0, keepdims=True)
                yield
                dvn_a16 = _b(dvn_a)
                dw_a = -_nt(dvn_a16, sab)
                dqd_a = _nt(dob[0:64], sab)
                dS = q["gla"] * dSm + _tn(qdb[0:64], dob[0:64]) - _tn(wb[0:64], dvn_a16)
                chain_out[dc, h] = (dqk, jnp.concatenate([dvn_a, dvn_b], 0), jnp.concatenate([dw_a, dw_b], 0),
                                    jnp.concatenate([dkd_a, dkd_b], 0), jnp.concatenate([dqd_a, dqd_b], 0),
                                    dgl_a, dgl_b)
                yield
            dstate[h] = dS

        _lockstep([head_chain(h) for h in range(G_HEADS)])

        def local(dc, h):
            rs = slice(dc * 128, (dc + 1) * 128)
            q = fw[dc * G_HEADS + h][0]
            dqk, du, dw, dkd, dqd, dgl_a, dgl_b = chain_out[dc, h]
            if True:
                dvb = _dot3(_tn, q["tm"], du)
                dkbe = _dot3(_tn, q["tm"], dw)
                yield
                dM = jnp.where(strict, -(_nt(_b(dvb), _b(q["u"])) + _nt(_b(dkbe), _b(q["w"]))), 0.0)
                yield
                D = q["D"]
                dA = dM * D
                dB = dqk * D
                dDD = (dM * q["A"] + dqk * q["Bm"]) * D
                dh_, dm_, dl_ = _split3(dDD)
                colsum = _tn(dh_, ones_b) + (_tn(dm_, ones_b) + _tn(dl_, ones_b))
                dgc = jnp.sum(dDD, 1, keepdims=True) - _lane_col(colsum, 0)
                yield
                dA16, dB16 = _b(dA), _b(dB)
                knb, kbb, qnb = _b(q["kn"]), _b(q["kb"]), _b(q["qn"])
                eg, ed = q["eg"], q["ed"]
                dkb = _nn(dA16, knb) + dkbe * eg
                dkn = _tn(dA16, kbb) + _tn(dB16, qnb) + dkd * ed + dkb * q["bcol"]
                dqn = _nn(dB16, knb) + dqd * eg
                yield
                deg = jnp.sum(dkbe * q["kb"], 1, keepdims=True) + jnp.sum(dqd * q["qn"], 1, keepdims=True)
                ded = jnp.sum(dkd * q["kn"], 1, keepdims=True) * ed
                dgc = dgc + deg * eg - ded
                tail_a = jnp.sum(jnp.where(rowi < 64, ded, 0.0), 0, keepdims=True) + dgl_a * q["gla"]
                tail_b = jnp.sum(jnp.where(rowi >= 64, ded, 0.0), 0, keepdims=True) + dgl_b * q["glb"]
                dgc = dgc + jnp.where(rowi == 63, tail_a, 0.0) + jnp.where(rowi == 127, tail_b, 0.0)
                dbeta = jnp.sum(dkb * q["kn"], 1, keepdims=True) + jnp.sum(dvb * q["vh"], 1, keepdims=True)
                bcol = q["bcol"]
                blk = jnp.where(lane == h, dbeta * bcol * (1.0 - bcol), 0.0) + jnp.where(lane == 4 + h, dgc, 0.0)
                yield
                sc = G_HEAD_DIM ** -0.5
                rq, rk, qh, kh = q["rq"], q["rk"], q["qh"], q["kh"]
                dqh = sc * (dqn * rq - qh * (rq * rq * rq) * jnp.sum(dqn * qh, 1, keepdims=True))
                dkh = dkn * rk - kh * (rk * rk * rk) * jnp.sum(dkn * kh, 1, keepdims=True)
                dqkv[rs, h * 128:(h + 1) * 128] = dqh
                dqkv[rs, 512 + h * 128:512 + (h + 1) * 128] = dkh
                dqkv[rs, 1024 + h * 128:1024 + (h + 1) * 128] = dvb * bcol
            return blk

        blks = _lockstep([local(dc, h) for dc, h in items])
        for dc in range(nu):
            dbg[dc * 128:(dc + 1) * 128, :] = functools.reduce(
                lambda a, b: a + b, [blks[dc * G_HEADS + h] for h in range(G_HEADS)])
        ri = lax.broadcasted_iota(jnp.int32, (T, T), 0)
        cj = lax.broadcasted_iota(jnp.int32, (T, T), 1)
        utri = jnp.where((ri <= cj) & ((ri >> 6) == (cj >> 6)), 1.0, 0.0).astype(BF16)
        dbgv = dbg[...]
        dgd = _dot_exact_lhs(_nn, utri, dbgv)
        is_g = (lane >= 4) & (lane < 8)
        dga = jnp.where(is_g, dgd * nea * _sigmoid(sarg), 0.0)
        dg_ref[...] = jnp.where(lane < 4, dbgv, dga).astype(BF16)
        dpv_ref[0:1, :] += jnp.sum(jnp.where(is_g, dgd * gdec, 0.0), 0, keepdims=True)
        dpv_ref[1:2, :] += jnp.sum(dga, 0, keepdims=True)
        dc_ = dqkv[...] * _dsilu(c)
        for k in range(CONV_WIDTH):
            dcw_ref[k:k + 1, :] += jnp.sum(dc_ * taps[k], 0, keepdims=True)
        ext2 = jnp.concatenate([dc_, dhalo[...]], 0)
        tt2 = _conv_taps_t(ext2, T)
        dp_ref[:, 0:GW3] = sum(cw_ref[k:k + 1, :] * tt2[k] for k in range(CONV_WIDTH)).astype(BF16)
        dhalo[...] = dc_[0:8]

    def rev(i):
        return nt - 1 - i

    def prev8(i):
        return jnp.maximum(rev(i) * t8 - 1, 0)

    return _pcall(
        body, name=name, grid=(nt,),
        in_specs=[pl.BlockSpec((T, GW3), lambda i: (rev(i), OFF_GQKV // GW3)),
                  pl.BlockSpec((8, GW3), lambda i: (prev8(i), OFF_GQKV // GW3)),
                  pl.BlockSpec((T, 512), lambda i: (rev(i), OFF_GZ // 512)),
                  pl.BlockSpec((T, 128), lambda i: (rev(i), OFF_GBA // 128)),
                  pl.BlockSpec((2 * nu, G_HEADS, 128, 128), lambda i: (rev(i), 0, 0, 0)),
                  pl.BlockSpec((nu, G_HEADS, 128, 128), lambda i: (rev(i), 0, 0, 0)),
                  pl.BlockSpec((T, 512), lambda i: (rev(i), MIX_G // 512))] + _gdn_param_specs() + [ANY_SPEC],
        out_specs=[pl.BlockSpec((T, GW3 + 512), lambda i: (rev(i), OFF_GQKV // (GW3 + 512))),
                   pl.BlockSpec((T, 128), lambda i: (rev(i), 0)),
                   pl.BlockSpec((CONV_WIDTH, GW3), lambda i: (0, 0)),
                   pl.BlockSpec((8, 128), lambda i: (0, 0)),
                   pl.BlockSpec((1, 128), lambda i: (0, 0))],
        out_shape=[jax.ShapeDtypeStruct(dproj.shape, BF16),
                   jax.ShapeDtypeStruct((S_, 128), BF16), jax.ShapeDtypeStruct((CONV_WIDTH, GW3), F32),
                   jax.ShapeDtypeStruct((8, 128), F32), jax.ShapeDtypeStruct((1, 128), F32)],
        input_output_aliases={10: 0},
        scratch_shapes=[pltpu.VMEM((G_HEADS, 128, 128), F32), pltpu.VMEM((8, GW3), F32),
                        pltpu.VMEM((T, GW3), F32), pltpu.VMEM((T, 128), F32)],
        compiler_params=_cp(("arbitrary",)),
    )(proj, proj, proj, proj, states, tms, dymix, cw, _gdn_pvec(a_log, dt_bias), nw.reshape(1, 128), dproj)


def _pair_sum_windows(a, b, nsh, width, *, out_dtype, name):
    R_, C = a.shape
    hr = R_ // 2
    nb = width // 128
    assert (3 * nsh) // 128 + nb <= C // 128
    to_perm = _orig_block_to_perm()
    table = jnp.asarray([to_perm[(nsh * t) // 128 + j] for t in range(4) for j in range(nb)], jnp.int32)

    def body(tab_ref, a0_ref, a1_ref, b_ref, o_ref):
        mine = jnp.where(lax.axis_index("c") == 0, a0_ref[...], a1_ref[...])
        o_ref[...] = (mine + b_ref[...]).astype(o_ref.dtype)

    def spec(half):
        return pl.BlockSpec((hr, 128), lambda t, j, tab: (half, tab[t * nb + j]))

    return _pcall(
        body, name=name,
        grid_spec=pltpu.PrefetchScalarGridSpec(
            num_scalar_prefetch=1, grid=(4, nb), in_specs=[spec(0), spec(1), spec(0)],
            out_specs=pl.BlockSpec((None, hr, 128), lambda t, j, tab: (t, 0, j))),
        out_shape=jax.ShapeDtypeStruct((4, hr, width), out_dtype),
        compiler_params=_cp(("parallel", "parallel")))(table, a, a, b)


def _pair_sum_blocks(a, b, *, out_dtype, name):
    L, R_, C = a.shape
    hr = R_ // 2

    def body(a0_ref, a1_ref, b_ref, o_ref):
        mine = jnp.where(lax.axis_index("c") == 0, a0_ref[...], a1_ref[...])
        o_ref[...] = (mine + b_ref[...]).astype(o_ref.dtype)

    def spec(half):
        return pl.BlockSpec((None, hr, C), lambda t: (t, half, 0))

    return _pcall(body, name=name, grid=(L,), in_specs=[spec(0), spec(1), spec(0)], out_specs=spec(0),
                  out_shape=jax.ShapeDtypeStruct((L, hr, C), out_dtype),
                  compiler_params=_cp(("parallel",)))(a, a, b)


def _add_mine(a0, a1, b, *, out_dtype, tr, name):
    R_, C = b.shape

    def body(a0_ref, a1_ref, b_ref, o_ref):
        mine = jnp.where(lax.axis_index("c") == 0, a0_ref[...], a1_ref[...])
        o_ref[...] = (mine + b_ref[...]).astype(o_ref.dtype)

    spec = pl.BlockSpec((tr, C), lambda i: (i, 0))
    return _pcall(body, name=name, grid=(R_ // tr,), in_specs=[spec] * 3, out_specs=spec,
                  out_shape=jax.ShapeDtypeStruct((R_, C), out_dtype), compiler_params=_cp(("parallel",)))(a0, a1, b)


def _sum4(a, mine, *, tr, name):
    _, R_, C = a.shape

    def body(a_ref, m_ref, o_ref):
        s = 2 * lax.axis_index("x") + lax.axis_index("y")
        mv = m_ref[...].astype(F32)
        p = [jnp.where(s == t, mv, a_ref[t].astype(F32)) for t in range(4)]
        o_ref[...] = ((p[0] + p[1]) + p[2]) + p[3]

    return _pcall(body, name=name, grid=(R_ // tr,),
                  in_specs=[pl.BlockSpec((4, tr, C), lambda i: (0, i, 0)), pl.BlockSpec((tr, C), lambda i: (i, 0))],
                  out_specs=pl.BlockSpec((tr, C), lambda i: (i, 0)),
                  out_shape=jax.ShapeDtypeStruct((R_, C), F32), compiler_params=_cp(("parallel",)))(a, mine)


def _adamw_refs(w_ref, g_ref, m_ref, v_ref, d_ref, mo_ref, vo_ref):
    c1 = 1.0 / (1.0 - ADAM_B1 ** ADAM_STEP)
    c2 = 1.0 / (1.0 - ADAM_B2 ** ADAM_STEP)
    gg = g_ref[...]
    mn = ADAM_B1 * m_ref[...] + (1.0 - ADAM_B1) * gg
    vn = ADAM_B2 * v_ref[...] + (1.0 - ADAM_B2) * (gg * gg)
    mo_ref[...] = mn
    vo_ref[...] = vn
    d_ref[...] = -ADAM_LR * ((mn * c1) / (jnp.sqrt(vn * c2) + ADAM_EPS) + ADAM_WD * w_ref[...])


def _adamw_many(ws, gs, ms, vs, *, name):
    n = len(ws)

    def body(*refs):
        for k in range(n):
            _adamw_refs(*[refs[q * n + k] for q in range(7)])

    vm = pl.BlockSpec(memory_space=pltpu.VMEM)
    shp = [jax.ShapeDtypeStruct(w.shape, F32) for w in ws]
    outs = _pcall(body, name=name, in_specs=[vm] * (4 * n), out_specs=[vm] * (3 * n), out_shape=shp * 3,
                  compiler_params=pltpu.CompilerParams(vmem_limit_bytes=VMEM_LIMIT))(*ws, *gs, *ms, *vs)
    return outs[:n], outs[n:2 * n], outs[2 * n:]


def _adamw(w, g, m, v, *, tr, name):
    L, R_, C = w.shape
    body = functools.partial(_adamw_refs)

    spec = pl.BlockSpec((None, tr, C), lambda l, i: (l, i, 0))
    shp = jax.ShapeDtypeStruct((L, R_, C), F32)
    return _pcall(body, name=name, grid=(L, R_ // tr), in_specs=[spec] * 4, out_specs=[spec] * 3,
                  out_shape=[shp] * 3, compiler_params=_cp(("parallel", "parallel")))(w, g, m, v)


HBM_SPEC = pl.BlockSpec(memory_space=pltpu.HBM)


def _place():
    x, y, c = lax.axis_index("x"), lax.axis_index("y"), lax.axis_index("c")
    chips = [(1 - x, y), (x, 1 - y), (1 - x, 1 - y)]
    return x, y, c, 2 * x + y, chips, [2 * cx + cy for cx, cy in chips], (x, y, 1 - c)


def _remote(src, dst, ssem, rsem, dev):
    return pltpu.make_async_remote_copy(src_ref=src, dst_ref=dst, send_sem=ssem, recv_sem=rsem,
                                        device_id=dev, device_id_type=MESH)


def _row_half(ref, lead, hc):
    hl = ref.shape[-2] // 2
    return ref.at[lead, pl.ds(hc * hl, hl), :]


def _gather_side(win, wout, layer):
    def copies(ins, outs, ssem, rsem):
        x, y, c, s, chips, sid, sib = _place()
        cps = []
        for j, chip in enumerate(chips):
            dev = (*chip, c)
            cps.append(_remote(_row_half(ins[0], layer, c), _row_half(outs[0], s, c), ssem.at[j], rsem.at[j], dev))
            cps.append(_remote(_row_half(ins[1], layer, c), _row_half(outs[1], s, c), ssem.at[3 + j], rsem.at[3 + j],
                               dev))
        return cps, c, sid, sib

    def start(ins, outs, ssem, rsem):
        for cp in copies(ins, outs, ssem, rsem)[0]:
            cp.start()

    def finish(ins, outs, ssem, rsem):
        cps, c, sid, sib = copies(ins, outs, ssem, rsem)
        for j in range(3):
            for k in range(2):
                got = _row_half(outs[k], sid[j], c)
                _remote(got, got, ssem.at[3 * k + j], rsem.at[3 * k + j], sib).wait_recv()
        for cp in cps:
            cp.wait_send()

    shapes = [jax.ShapeDtypeStruct((4,) + win.shape[1:], win.dtype), jax.ShapeDtypeStruct((4,) + wout.shape[1:], wout.dtype)]
    return _Side([win, wout], shapes, 6, start, finish)


def _gather_join(gin, gout, name):
    def body(gin_in, gout_in, gin_ref, gout_ref, ssem, rsem):
        x, y, c, s, chips, sid, sib = _place()
        cps = []
        for j in range(3):
            for k, ref in enumerate((gin_ref, gout_ref)):
                mine = _row_half(ref, sid[j], c)
                cps.append(_remote(mine, mine, ssem.at[3 * k + j], rsem.at[3 * k + j], sib))
        for cp in cps:
            cp.start()
        for j in range(3):
            for k, ref in enumerate((gin_ref, gout_ref)):
                other = _row_half(ref, sid[j], 1 - c)
                _remote(other, other, ssem.at[3 * k + j], rsem.at[3 * k + j], sib).wait_recv()
        for cp in cps:
            cp.wait_send()

    return _pcall(
        body, name=name, in_specs=[HBM_SPEC] * 2, out_specs=[HBM_SPEC] * 2,
        out_shape=[jax.ShapeDtypeStruct(gin.shape, gin.dtype), jax.ShapeDtypeStruct(gout.shape, gout.dtype)],
        input_output_aliases={0: 0, 1: 1},
        scratch_shapes=[pltpu.SemaphoreType.DMA((6,)), pltpu.SemaphoreType.DMA((6,))],
    )(gin, gout)


def _gather_layer0(win, wout, conv):
    def body(win_ref, wout_ref, cv_ref, gin_ref, gout_ref, gcv_ref, ssem, rsem):
        x, y, c, s, chips, sid, sib = _place()

        def in_half(slot, hc):
            return _row_half(gin_ref, slot, hc)

        def out_half(slot, hc):
            return _row_half(gout_ref, slot, hc)

        sends = []
        for j, chip in enumerate(chips):
            dev = (*chip, c)
            sends.append(_remote(_row_half(win_ref, 0, c), in_half(s, c), ssem.at[j], rsem.at[j], dev))
            sends.append(_remote(_row_half(wout_ref, 0, c), out_half(s, c), ssem.at[3 + j], rsem.at[3 + j], dev))
            sends.append(_remote(cv_ref, gcv_ref.at[s], ssem.at[6 + j], rsem.at[6 + j], dev))
        for cp in sends:
            cp.start()
        for j in range(3):
            _remote(in_half(sid[j], c), in_half(sid[j], c), ssem.at[j], rsem.at[j], sib).wait_recv()
            f = _remote(in_half(sid[j], c), in_half(sid[j], c), ssem.at[9 + j], rsem.at[9 + j], sib)
            f.start()
            sends.append(f)
            _remote(out_half(sid[j], c), out_half(sid[j], c), ssem.at[3 + j], rsem.at[3 + j], sib).wait_recv()
            f = _remote(out_half(sid[j], c), out_half(sid[j], c), ssem.at[12 + j], rsem.at[12 + j], sib)
            f.start()
            sends.append(f)
        for j in range(3):
            _remote(in_half(sid[j], 1 - c), in_half(sid[j], 1 - c), ssem.at[9 + j], rsem.at[9 + j], sib).wait_recv()
            _remote(out_half(sid[j], 1 - c), out_half(sid[j], 1 - c), ssem.at[12 + j], rsem.at[12 + j], sib).wait_recv()
            _remote(gcv_ref.at[sid[j]], gcv_ref.at[sid[j]], ssem.at[6 + j], rsem.at[6 + j], sib).wait_recv()
        for cp in sends:
            cp.wait_send()

    return _pcall(
        body, name="gather_layer0",
        in_specs=[HBM_SPEC] * 3, out_specs=[HBM_SPEC] * 3,
        out_shape=[jax.ShapeDtypeStruct((4,) + win.shape[1:], win.dtype),
                   jax.ShapeDtypeStruct((4,) + wout.shape[1:], wout.dtype),
                   jax.ShapeDtypeStruct((4,) + conv.shape, conv.dtype)],
        scratch_shapes=[pltpu.SemaphoreType.DMA((15,)), pltpu.SemaphoreType.DMA((15,))],
    )(win, wout, conv)


def _swap_halves(arrs, axes, name):
    n = len(arrs)

    def half_shape(a, ax):
        return a.shape[:ax] + (a.shape[ax] // 2,) + a.shape[ax + 1:]

    def body(*refs):
        src, dst, ssem, rsem = refs[:n], refs[n:2 * n], refs[2 * n], refs[2 * n + 1]
        x, y, c, s, chips, sid, sib = _place()
        cps = []
        for k in range(n):
            hl = src[k].shape[axes[k]] // 2
            idx = [slice(None)] * len(src[k].shape)
            idx[axes[k]] = pl.ds((1 - c) * hl, hl)
            cps.append(_remote(src[k].at[tuple(idx)], dst[k], ssem.at[k], rsem.at[k], sib))
        for cp in cps:
            cp.start()
        for cp in cps:
            cp.wait()

    return _pcall(
        body, name=name, in_specs=[HBM_SPEC] * n, out_specs=[HBM_SPEC] * n,
        out_shape=[jax.ShapeDtypeStruct(half_shape(a, ax), a.dtype) for a, ax in zip(arrs, axes)],
        scratch_shapes=[pltpu.SemaphoreType.DMA((n,)), pltpu.SemaphoreType.DMA((n,))],
    )(*arrs)


def _chips_side(arrs):
    n = len(arrs)

    def copies(ins, outs, ssem, rsem):
        x, y, c, s, chips, sid, sib = _place()
        cps = [_remote(ins[k].at[sid[j]], outs[k].at[s], ssem.at[3 * k + j], rsem.at[3 * k + j], (*chip, c))
               for k in range(n) for j, chip in enumerate(chips)]
        return cps, sid, sib

    def start(ins, outs, ssem, rsem):
        for cp in copies(ins, outs, ssem, rsem)[0]:
            cp.start()

    def finish(ins, outs, ssem, rsem):
        cps, sid, sib = copies(ins, outs, ssem, rsem)
        for k in range(n):
            for j in range(3):
                got = outs[k].at[sid[j]]
                _remote(got, got, ssem.at[3 * k + j], rsem.at[3 * k + j], sib).wait_recv()
        for cp in cps:
            cp.wait_send()

    return _Side(list(arrs), [jax.ShapeDtypeStruct(a.shape, a.dtype) for a in arrs], 3 * n, start, finish)


def _scatter_chips(arrs, per_target, name):
    n = len(arrs)

    def body(*refs):
        src, dst = refs[:n], refs[n:2 * n]
        ssem, rsem = refs[2 * n], refs[2 * n + 1]
        x, y, c, s, chips, sid, sib = _place()
        sends = []
        for k in range(n):
            for j, chip in enumerate(chips):
                piece = src[k].at[sid[j]] if per_target[k] else src[k]
                sends.append(_remote(piece, dst[k].at[s], ssem.at[3 * k + j], rsem.at[3 * k + j], (*chip, c)))
        for cp in sends:
            cp.start()
        for k in range(n):
            for j in range(3):
                _remote(dst[k].at[sid[j]], dst[k].at[sid[j]], ssem.at[3 * k + j], rsem.at[3 * k + j], sib).wait_recv()
        for cp in sends:
            cp.wait_send()

    outs = [jax.ShapeDtypeStruct(a.shape if pt else (4,) + a.shape, a.dtype) for a, pt in zip(arrs, per_target)]
    return _pcall(
        body, name=name, in_specs=[HBM_SPEC] * n, out_specs=[HBM_SPEC] * n, out_shape=outs,
        scratch_shapes=[pltpu.SemaphoreType.DMA((3 * n,)), pltpu.SemaphoreType.DMA((3 * n,))],
    )(*arrs)


def _swap_whole(arrs, name):
    n = len(arrs)

    def body(*refs):
        src, dst, ssem, rsem = refs[:n], refs[n:2 * n], refs[2 * n], refs[2 * n + 1]
        *_, sib = _place()
        cps = [_remote(src[k], dst[k], ssem.at[k], rsem.at[k], sib) for k in range(n)]
        for cp in cps:
            cp.start()
        for cp in cps:
            cp.wait()

    return _pcall(
        body, name=name, in_specs=[HBM_SPEC] * n, out_specs=[HBM_SPEC] * n,
        out_shape=[jax.ShapeDtypeStruct(a.shape, a.dtype) for a in arrs],
        scratch_shapes=[pltpu.SemaphoreType.DMA((n,)), pltpu.SemaphoreType.DMA((n,))],
    )(*arrs)


def _perm_cols(w):
    parts = [w[..., int(_ORIG_OFF[oi]):int(_ORIG_OFF[oi]) + IN_SIZES[oi]] for oi, _ in _PIECES]
    parts.append(jnp.zeros(w.shape[:-1] + (NP - N_IN,), w.dtype))
    return jnp.concatenate(parts, -1)


def _perm_rows(w):
    return jnp.concatenate([w[..., 512:1536, :], w[..., 0:512, :], w[..., 1536:2048, :]], -2)


_SMALL = ("sinks", "r_conv_b", "r_wa", "r_ba", "r_wx", "r_bx", "r_lam", "g_a_log", "g_dt_bias", "g_norm_w",
          "ln_g", "ln_b", "r_conv_w", "g_conv_w")
_PACK_ROWS = 16


def _piece_rows(n):
    return -(-n // (128 * _PACK_ROWS)) * _PACK_ROWS


def _pack(arrs):
    parts = []
    for a in arrs:
        n = int(np.prod(a.shape))
        rows = _piece_rows(n)
        if n % 128 == 0:
            blk = a.reshape(n // 128, 128)
        else:
            blk = jnp.pad(a.reshape(1, n), ((0, 0), (0, (-n) % 128))).reshape(-1, 128)
        if blk.shape[0] < rows:
            blk = jnp.pad(blk, ((0, rows - blk.shape[0]), (0, 0)))
        parts.append(blk)
    return jnp.concatenate(parts, 0)


def _unpack(packed, shapes):
    out = []
    r = 0
    for shp in shapes:
        n = int(np.prod(shp))
        if n % 128 == 0:
            out.append(packed[r:r + n // 128].reshape(shp))
        else:
            nr = -(-n // 128)
            out.append(packed[r:r + nr].reshape(1, nr * 128)[:, :n].reshape(shp))
        r += _piece_rows(n)
    return out


def _tile(n, t):
    return min(n, t)


def _layer_fwd(l, x, xb, wb, wob, rope_c, rope_s, p, side=None):
    S_ = x.shape[0]
    proj = _matmul(xb, wb, ta=False, tb=False, tm=_tile(S_, 1024), tn=NP // 4, tk=wb.shape[0], out_dtype=F32,
                   name=f"in_proj_{l}", side=side)
    side_out = None
    if side:
        proj, side_out = proj
    h, ymix = _rglru_fwd(proj, p["r_conv_w"], p["r_conv_b"], p["r_wa"], p["r_ba"], p["r_wx"], p["r_bx"], p["r_lam"],
                         T=_tile(S_, 256), name=f"rglru_fwd_{l}")
    ymix = _attn_fwd(proj, rope_c, rope_s, p["sinks"], ymix, T=_tile(S_, 512), name=f"attn_fwd_{l}")
    ymix, st, tms = _gdn_fwd(proj, p["g_conv_w"], p["g_a_log"], p["g_dt_bias"], p["g_norm_w"], ymix,
                             T=_tile(S_, 256), name=f"gdn_fwd_{l}")
    z = _outproj(ymix, wob, x, tm=_tile(S_, 256), name=f"out_proj_{l}")
    return dict(proj=proj, h=h, st=st, tms=tms, ymix=ymix, z=z, side=side_out)


def _layer_bwd(l, sv, x_b, dz, dzb, wb, wob, rope_c, rope_s, p, side=None):
    S_, D = dz.shape
    proj = sv["proj"]
    dymix = _matmul(dzb, wob, ta=False, tb=True, tm=_tile(S_, 1024), tn=512, tk=D, out_dtype=F32,
                    name=f"dmix_{l}")
    dwo = _matmul(sv["ymix"], dzb, ta=True, tb=False, tm=512, tn=_tile(D, 2048), tk=_tile(S_, 1024),
                  out_dtype=F32, name=f"dw_out_{l}",
                  out_blocks=((MIX_WIDTH, D), (512, _tile(D, 2048)),
                              lambda i, j: (jnp.where(i == 3, 3, (i + 1) % 3), j)))
    dproj, dk, dv, dkt, dvt, dsk = _attn_bwd(proj, rope_c, rope_s, p["sinks"], dymix, T=_tile(S_, 512),
                                             name=f"attn_bwd_{l}")
    (dproj, dcw_r, dcb_r, dwa, dba, dwx, dbx, dlam) = _rglru_bwd(
        proj, sv["h"], dymix, dproj, p["r_conv_w"], p["r_conv_b"], p["r_wa"], p["r_ba"], p["r_wx"], p["r_bx"],
        p["r_lam"], T=_tile(S_, 256), name=f"rglru_bwd_{l}")
    dproj, dgba, dcw_g, dpv, dnw = _gdn_bwd(proj, sv["st"], sv["tms"], dymix, dproj, p["g_conv_w"], p["g_a_log"],
                                            p["g_dt_bias"], p["g_norm_w"], T=_tile(S_, 256), name=f"gdn_bwd_{l}")
    tail = jnp.concatenate([dk[128:], dkt, dv[128:], dvt], 0).reshape(2, S_, 128)
    tail = jnp.concatenate([tail[0], tail[1], dgba, jnp.zeros((S_, NP - OFF_GBA - 128), BF16)], 1)
    dproj = lax.dynamic_update_slice(dproj, tail, (0, OFF_AK))
    dx = _matmul(dproj, wb, ta=False, tb=True, tm=_tile(S_, 1024), tn=_tile(D, 1024), tk=NP // 2, out_dtype=F32,
                 name=f"dx_{l}", extra=dz, alpha=DEEPNORM_ALPHA)
    dwin = _matmul(x_b, dproj, ta=True, tb=False, tm=_tile(D, 1024), tn=NP // 4, tk=_tile(S_, 1024), out_dtype=F32,
                   name=f"dw_in_{l}", side=side)
    side_out = None
    if side:
        dwin, side_out = dwin
    small = dict(sinks=dsk[:, 0], r_conv_b=dcb_r[0], r_wa=dwa, r_ba=dba[0], r_wx=dwx, r_bx=dbx[0], r_lam=dlam[0],
                 g_a_log=dpv[0, 4:8], g_dt_bias=dpv[1, 4:8], g_norm_w=dnw[0], r_conv_w=dcw_r, g_conv_w=dcw_g)
    return dx, dwin, dwo, small, side_out


def kernel(x, w_in, sinks, r_conv_w, r_conv_b, r_wa, r_ba, r_wx, r_bx, r_lam, g_conv_w, g_a_log, g_dt_bias, g_norm_w, w_out, ln_g, ln_b, loss_target, m_w_in, m_sinks, m_r_conv_w, m_r_conv_b, m_r_wa, m_r_ba, m_r_wx, m_r_bx, m_r_lam, m_g_conv_w, m_g_a_log, m_g_dt_bias, m_g_norm_w, m_w_out, m_ln_g, m_ln_b, v_w_in, v_sinks, v_r_conv_w, v_r_conv_b, v_r_wa, v_r_ba, v_r_wx, v_r_bx, v_r_lam, v_g_conv_w, v_g_a_log, v_g_dt_bias, v_g_norm_w, v_w_out, v_ln_g, v_ln_b):
    S_, D = x.shape[1], x.shape[2]
    nsh = w_in.shape[2]
    rsh = w_out.shape[1]
    cx, cy, cc = lax.axis_index("x"), lax.axis_index("y"), lax.axis_index("c")
    chip = 2 * cx + cy
    rcw_n, gcw_n = r_conv_w.shape[2], g_conv_w.shape[2]

    conv_pack = jnp.concatenate([r_conv_w, g_conv_w], 2)
    w_in_b, w_out_b = w_in.astype(BF16), w_out.astype(BF16)
    g_in0, g_out0, g_conv = _gather_layer0(w_in_b, w_out_b, conv_pack)

    def shards(own, got):
        return [jnp.where(chip == t, own, got[t]) for t in range(4)]

    def layer_weights(l, g_in, g_out):
        w_full = jnp.concatenate(shards(w_in_b[l], g_in), 1)
        return (_perm_cols(w_full),
                _perm_rows(jnp.concatenate(shards(w_out_b[l], g_out), 0)))

    rcw = jnp.concatenate(shards(r_conv_w, g_conv[:, :, :, :rcw_n]), 2)
    gcw = jnp.concatenate(shards(g_conv_w, g_conv[:, :, :, rcw_n:]), 2)

    pos = jnp.arange(S_, dtype=F32)[:, None]
    inv = 1.0 / (ROPE_THETA ** (jnp.arange(0, A_HEAD_DIM, 2, dtype=F32) / A_HEAD_DIM))
    ang = pos * inv[None, :]
    cos, sin = jnp.cos(ang), jnp.sin(ang)
    rope_c = jnp.concatenate([cos, cos, cos, cos], 1)
    rope_s = jnp.concatenate([-sin, sin, -sin, sin], 1)

    def params(l):
        return dict(sinks=sinks[l], r_conv_w=rcw[l], r_conv_b=r_conv_b[l], r_wa=r_wa[l], r_ba=r_ba[l],
                    r_wx=r_wx[l], r_bx=r_bx[l], r_lam=r_lam[l], g_conv_w=gcw[l], g_a_log=g_a_log[l],
                    g_dt_bias=g_dt_bias[l], g_norm_w=g_norm_w[l])

    xs, xbs, saved = [x[0]], [x[0].astype(BF16)], []
    wb, wob = [None] * DEPTH, [None] * DEPTH
    wb[0], wob[0] = layer_weights(0, g_in0, g_out0)
    for l in range(DEPTH):
        nxt = _gather_side(w_in_b, w_out_b, l + 1) if l + 1 < DEPTH else None
        sv = _layer_fwd(l, xs[l], xbs[l], wb[l], wob[l], rope_c, rope_s, params(l), side=nxt)
        if nxt:
            wb[l + 1], wob[l + 1] = layer_weights(l + 1, *_gather_join(*sv["side"], f"gather_join_{l + 1}"))
        saved.append(sv)
        if l + 1 < DEPTH:
            xn, xnb = _ln_fwd(sv["z"], ln_g[l], ln_b[l], tm=_tile(S_, 256), name=f"ln_fwd_{l}")
            xs.append(xn)
            xbs.append(xnb)

    tm_ln = _tile(S_, 256)
    dz, dzb, dg_l, db_l, loss_part = _ln_bwd(saved[-1]["z"], ln_g[-1], ln_b[-1], loss_target[0], from_target=True,
                                             tm=tm_ln, name=f"ln_bwd_{DEPTH - 1}")
    wcov = (-(-nsh // 128) + 1) * 128

    def own(a):
        return lax.dynamic_index_in_dim(a, chip, 0, keepdims=False)

    def pair_reduce(l, dwin_l, dwo_l, extra):
        dwo4 = dwo_l.reshape(4, rsh, D)
        got = _swap_halves([dwin_l, dwo4] + [e for e in extra], [0, 1] + [0] * len(extra), f"reduce_pair_{l}")
        in_cp = _pair_sum_windows(dwin_l, got[0], nsh, wcov, out_dtype=BF16, name=f"pair_sum_w_in_{l}")
        out_cp = _pair_sum_blocks(dwo4, got[1], out_dtype=BF16, name=f"pair_sum_w_out_{l}")
        ex_cp = [_pair_sum_blocks(e[None], g[None], out_dtype=F32, name=f"pair_sum_small_{l}")[0]
                 for e, g in zip(extra, got[2:])]
        return in_cp, out_cp, ex_cp

    def chip_sums(l, in_cp, out_cp, in_all, out_all):
        return (_sum4(in_all, own(in_cp), tr=_tile(D // 2, 256), name=f"chip_sum_w_in_{l}"),
                _sum4(out_all, own(out_cp), tr=rsh // 2, name=f"chip_sum_w_out_{l}"))

    small = [None] * DEPTH
    dlng, dlnb = [None] * DEPTH, [None] * DEPTH
    sums = [None] * DEPTH
    riding = None
    for l in reversed(range(DEPTH)):
        dlng[l], dlnb[l] = dg_l[0], db_l[0]
        side = _chips_side(list(riding[1:])) if riding else None
        dx, dwin_l, dwo_l, small[l], arrived = _layer_bwd(l, saved[l], xbs[l], dz, dzb, wb[l], wob[l], rope_c, rope_s,
                                                          params(l), side=side)
        if riding:
            sums[riding[0]] = chip_sums(riding[0], riding[1], riding[2], *arrived)
        if l > 0:
            riding = (l,) + pair_reduce(l, dwin_l, dwo_l, [])[:2]
            dz, dzb, dg_l, db_l, _ = _ln_bwd(saved[l - 1]["z"], ln_g[l - 1], ln_b[l - 1], dx, from_target=False,
                                             tm=tm_ln, name=f"ln_bwd_{l - 1}")
    grad_x = dx[None]
    loss = lax.psum(loss_part[0, 0], ("x", "y", "c"))

    sm = {k: jnp.stack([small[l][k] for l in range(DEPTH)]) for k in small[0]}
    sm["ln_g"], sm["ln_b"] = jnp.stack(dlng), jnp.stack(dlnb)
    names = list(_SMALL)
    gs = _pack([sm[n] for n in names])
    in_cp, out_cp, (s_cp,) = pair_reduce(0, dwin_l, dwo_l, [gs])
    in_all, out_all, s_all = _scatter_chips([in_cp, out_cp, s_cp], [True, True, False], "reduce_chips_0")
    sums[0] = chip_sums(0, in_cp, out_cp, in_all, out_all)
    s_sum = _sum4(s_all, s_cp, tr=s_cp.shape[0], name="chip_sum_small")
    mine = [a for l in range(DEPTH) for a in sums[l]] + [s_sum]
    other = _swap_whole(mine, "reduce_join")

    def both(k, axis):
        return jnp.where(cc == 0, jnp.concatenate([mine[k], other[k]], axis),
                         jnp.concatenate([other[k], mine[k]], axis))

    g_w_in = lax.dynamic_slice_in_dim(jnp.stack([both(2 * l, 0) for l in range(DEPTH)]), (nsh * chip) % 128, nsh, 2)
    g_w_out = jnp.stack([both(2 * l + 1, 0) for l in range(DEPTH)])
    g_small = both(2 * DEPTH, 0)

    gsm = dict(zip(names, _unpack(g_small, [sm[n].shape for n in names])))
    gsm["r_conv_w"] = lax.dynamic_slice_in_dim(gsm["r_conv_w"], chip * rcw_n, rcw_n, 2)
    gsm["g_conv_w"] = lax.dynamic_slice_in_dim(gsm["g_conv_w"], chip * gcw_n, gcw_n, 2)
    wts = dict(sinks=sinks, r_conv_w=r_conv_w, r_conv_b=r_conv_b, r_wa=r_wa, r_ba=r_ba, r_wx=r_wx, r_bx=r_bx,
               r_lam=r_lam, g_conv_w=g_conv_w, g_a_log=g_a_log, g_dt_bias=g_dt_bias, g_norm_w=g_norm_w,
               ln_g=ln_g, ln_b=ln_b)
    mom = dict(sinks=m_sinks, r_conv_w=m_r_conv_w, r_conv_b=m_r_conv_b, r_wa=m_r_wa, r_ba=m_r_ba, r_wx=m_r_wx,
               r_bx=m_r_bx, r_lam=m_r_lam, g_conv_w=m_g_conv_w, g_a_log=m_g_a_log, g_dt_bias=m_g_dt_bias,
               g_norm_w=m_g_norm_w, ln_g=m_ln_g, ln_b=m_ln_b)
    vel = dict(sinks=v_sinks, r_conv_w=v_r_conv_w, r_conv_b=v_r_conv_b, r_wa=v_r_wa, r_ba=v_r_ba, r_wx=v_r_wx,
               r_bx=v_r_bx, r_lam=v_r_lam, g_conv_w=v_g_conv_w, g_a_log=v_g_a_log, g_dt_bias=v_g_dt_bias,
               g_norm_w=v_g_norm_w, ln_g=v_ln_g, ln_b=v_ln_b)
    d_s, m_s, v_s = _adamw_many(*[[d[n] for n in names] for d in (wts, gsm, mom, vel)], name="adamw_small")
    d_sm, m_sm, v_sm = (dict(zip(names, a)) for a in (d_s, m_s, v_s))
    d_in, m_in, v_in = _adamw(w_in, g_w_in, m_w_in, v_w_in, tr=256, name="adamw_w_in")
    d_out, m_out, v_out = _adamw(w_out, g_w_out, m_w_out, v_w_out, tr=256, name="adamw_w_out")

    order = ["w_in", "sinks", "r_conv_w", "r_conv_b", "r_wa", "r_ba", "r_wx", "r_bx", "r_lam", "g_conv_w",
             "g_a_log", "g_dt_bias", "g_norm_w", "w_out", "ln_g", "ln_b"]
    grads = dict(gsm, w_in=g_w_in, w_out=g_w_out)
    deltas = dict(d_sm, w_in=d_in, w_out=d_out)
    new_m = dict(m_sm, w_in=m_in, w_out=m_out)
    new_v = dict(v_sm, w_in=v_in, w_out=v_out)
    return (loss, grad_x, *[grads[n] for n in order], *[deltas[n] for n in order],
            *[new_m[n] for n in order], *[new_v[n] for n in order])
```

```python
import functools
import math

import jax
import jax.numpy as jnp
import numpy as np
from jax import lax
from jax.experimental import pallas as pl
from jax.experimental.pallas import tpu as pltpu

F32 = jnp.float32
BF16 = jnp.bfloat16
MESH = pl.DeviceIdType.MESH

DEPTH = 2
A_HEADS, A_KV_HEADS, A_HEAD_DIM = 8, 2, 64
A_WIDTH, A_KV_WIDTH = 512, 128
WINDOW = 128
ROPE_THETA = 10000.0
R_WIDTH, R_BLOCKS, R_BLOCK_DIM, R_C = 1024, 8, 128, 8.0
CONV_WIDTH = 4
G_HEADS, G_HEAD_DIM, G_WIDTH, G_CHUNK = 4, 128, 512, 64
MIX_WIDTH = 2048
IN_SIZES = (512, 128, 128, 512, 1024, 1024, 512, 512, 512, 512, 4, 4)
N_IN = 5384
DEEPNORM_ALPHA = (2 * DEPTH) ** 0.25
LN_EPS = 1e-5
RMS_EPS = 1e-6
ADAM_LR, ADAM_B1, ADAM_B2, ADAM_EPS, ADAM_WD, ADAM_STEP = 0.001, 0.9, 0.999, 1e-08, 0.01, 10

NP = 5632
OFF_GQKV, OFF_GZ, OFF_RX, OFF_RZ, OFF_AQ, OFF_AZ, OFF_AK, OFF_AV, OFF_GBA = (
    0, 1536, 2048, 3072, 4096, 4608, 5120, 5248, 5376)
_ORIG_OFF = np.concatenate([[0], np.cumsum(IN_SIZES)])[:-1]
_PIECES = ((6, OFF_GQKV), (7, OFF_GQKV + 512), (8, OFF_GQKV + 1024), (9, OFF_GZ), (4, OFF_RX), (5, OFF_RZ),
           (0, OFF_AQ), (3, OFF_AZ), (1, OFF_AK), (2, OFF_AV), (10, OFF_GBA), (11, OFF_GBA + 4))


def _orig_block_to_perm():
    table = list(range(NP // 128))
    for oi, off in _PIECES:
        if IN_SIZES[oi] % 128 == 0:
            for k in range(IN_SIZES[oi] // 128):
                table[int(_ORIG_OFF[oi]) // 128 + k] = off // 128 + k
    return table
MIX_R, MIX_A, MIX_G = 0, 1024, 1536
VMEM_LIMIT = 56 * 1024 * 1024
ANY_SPEC = pl.BlockSpec(memory_space=pl.ANY)


def _pcall(body, **kw):
    return pl.pallas_call(body, **kw)


def _cp(sem, limit=VMEM_LIMIT):
    return pltpu.CompilerParams(dimension_semantics=sem, vmem_limit_bytes=limit)


def _sigmoid(x):
    return 0.5 + 0.5 * jnp.tanh(0.5 * x)


def _silu(x):
    return x * _sigmoid(x)


def _dsilu(x):
    s = _sigmoid(x)
    return s * (1.0 + x * (1.0 - s))


def _log1p(x):
    u = 1.0 + x
    d = jnp.where(u == 1.0, 1.0, u - 1.0)
    return jnp.where(u == 1.0, x, jnp.log(u) * (x / d))


def _softplus(x):
    return jnp.maximum(x, 0.0) + _log1p(jnp.exp(-jnp.abs(x)))


def _one_minus_exp(x):
    series = -x * (1.0 + x * (0.5 + x * (1.0 / 6.0 + x * (1.0 / 24.0))))
    return jnp.where(x > -0.05, series, 1.0 - jnp.exp(x))


def _nn(a, b):
    return lax.dot_general(a, b, (((1,), (0,)), ((), ())), preferred_element_type=F32)


def _nt(a, b):
    return lax.dot_general(a, b, (((1,), (1,)), ((), ())), preferred_element_type=F32)


def _tn(a, b):
    return lax.dot_general(a, b, (((0,), (0,)), ((), ())), preferred_element_type=F32)


def _b(x):
    return x.astype(BF16)


def _split3(x):
    hi = x.astype(BF16)
    r1 = x - hi.astype(F32)
    mid = r1.astype(BF16)
    lo = (r1 - mid.astype(F32)).astype(BF16)
    return hi, mid, lo


def _dot3(f, a, b):
    ah, am, _ = _split3(a)
    bh, bm, _ = _split3(b)
    return f(ah, bh) + (f(ah, bm) + f(am, bh))


def _dot_exact_lhs(f, a_bf16, b):
    bh, bm, bl = _split3(b)
    return f(a_bf16, bh) + (f(a_bf16, bm) + f(a_bf16, bl))


def _rot(x):
    w = x.shape[-1]
    lane = lax.broadcasted_iota(jnp.int32, (1, w), 1)
    return jnp.where((lane & 63) < 32, pltpu.roll(x, w - 32, 1), pltpu.roll(x, 32, 1))


def _conv_taps(ext, n):
    return [pltpu.roll(ext, 3 - k, 0)[8:8 + n] if k < 3 else ext[8:8 + n] for k in range(CONV_WIDTH)]


def _conv_taps_t(ext, n):
    m = ext.shape[0]
    return [pltpu.roll(ext, m - (3 - k), 0)[0:n] if k < 3 else ext[0:n] for k in range(CONV_WIDTH)]


def _scan_steps(a, b, pos, span, shifts, reverse):
    n = a.shape[0]
    for s in shifts:
        if reverse:
            a_sh = pltpu.roll(a, n - s, 0)
            b_sh = pltpu.roll(b, n - s, 0)
            ok = pos < (span - s)
        else:
            a_sh = pltpu.roll(a, s, 0)
            b_sh = pltpu.roll(b, s, 0)
            ok = pos >= s
        b = jnp.where(ok, a * b_sh + b, b)
        a = jnp.where(ok, a * a_sh, a)
    return a, b


def _scan_lin(a, b, reverse):
    n = a.shape[0]
    shifts = []
    s = 1
    while s < n:
        shifts.append(s)
        s *= 2
    return _scan_steps(a, b, lax.broadcasted_iota(jnp.int32, (n, 1), 0), n, shifts, reverse)


class _Side:
    def __init__(self, inputs, out_shapes, n_sems, start, finish):
        self.inputs, self.out_shapes, self.n_sems, self.start, self.finish = inputs, out_shapes, n_sems, start, finish


def _matmul(a, b, *, ta, tb, tm, tn, tk, out_dtype, name, extra=None, alpha=0.0, out_blocks=None, side=None):
    if ta:
        K, M = a.shape
    else:
        M, K = a.shape
    if tb:
        N, K2 = b.shape
    else:
        K2, N = b.shape
    assert K == K2 and M % tm == 0 and N % tn == 0 and K % tk == 0, (a.shape, b.shape, tm, tn, tk)
    nk = K // tk
    ca = 0 if ta else 1
    cb = 1 if tb else 0
    has_extra = extra is not None

    assert nk == 1 or out_dtype == F32
    n_in = 2 + int(has_extra)
    ns_in = len(side.inputs) if side else 0
    ns_out = len(side.out_shapes) if side else 0
    grid = (M // tm, N // tn, nk)

    def body(*refs):
        a_ref, b_ref = refs[0], refs[1]
        e_ref = refs[2] if has_extra else None
        o_ref = refs[n_in + ns_in]
        k = pl.program_id(2)
        if side:
            s_in = refs[n_in:n_in + ns_in]
            s_out = refs[n_in + ns_in + 1:n_in + ns_in + 1 + ns_out]
            ssem, rsem = refs[-2], refs[-1]
            i, j = pl.program_id(0), pl.program_id(1)

            @pl.when((i == 0) & (j == 0) & (k == 0))
            def _():
                side.start(s_in, s_out, ssem, rsem)

            @pl.when((i == grid[0] - 1) & (j == grid[1] - 1) & (k == grid[2] - 1))
            def _():
                side.finish(s_in, s_out, ssem, rsem)

        part = lax.dot_general(a_ref[...], b_ref[...], (((ca,), (cb,)), ((), ())), preferred_element_type=F32)
        if nk == 1:
            if e_ref is not None:
                part = part + alpha * e_ref[...]
            o_ref[...] = part.astype(o_ref.dtype)
            return

        @pl.when(k == 0)
        def _():
            o_ref[...] = part

        @pl.when((k > 0) & (k < nk - 1))
        def _():
            o_ref[...] += part

        @pl.when(k == nk - 1)
        def _():
            last = o_ref[...] + part
            if e_ref is not None:
                last = last + alpha * e_ref[...]
            o_ref[...] = last

    a_spec = (pl.BlockSpec((tk, tm), lambda i, j, k: (k, i)) if ta
              else pl.BlockSpec((tm, tk), lambda i, j, k: (i, k)))
    b_spec = (pl.BlockSpec((tn, tk), lambda i, j, k: (j, k)) if tb
              else pl.BlockSpec((tk, tn), lambda i, j, k: (k, j)))
    e_spec = pl.BlockSpec((tm, tn), lambda i, j, k: (i, j))
    if out_blocks is None:
        o_spec, o_shape = e_spec, (M, N)
    else:
        o_shape, o_block, o_map = out_blocks
        o_spec = pl.BlockSpec(o_block, lambda i, j, k: o_map(i, j))
    in_specs = [a_spec, b_spec] + ([e_spec] if has_extra else [])
    args = (a, b) + ((extra,) if has_extra else ())
    if not side:
        return _pcall(
            body, name=name, grid=grid, in_specs=in_specs, out_specs=o_spec,
            out_shape=jax.ShapeDtypeStruct(o_shape, out_dtype),
            compiler_params=_cp(("parallel", "parallel", "arbitrary")),
        )(*args)
    outs = _pcall(
        body, name=name, grid=grid, in_specs=in_specs + [HBM_SPEC] * ns_in,
        out_specs=[o_spec] + [HBM_SPEC] * ns_out,
        out_shape=[jax.ShapeDtypeStruct(o_shape, out_dtype)] + list(side.out_shapes),
        scratch_shapes=[pltpu.SemaphoreType.DMA((side.n_sems,)), pltpu.SemaphoreType.DMA((side.n_sems,))],
        compiler_params=_cp(("arbitrary", "arbitrary", "arbitrary")),
    )(*args, *side.inputs)
    return outs[0], outs[1:]


def _outproj(ymix, wo, x, *, tm, name):
    S_, D = x.shape

    def body(y_ref, w_ref, x_ref, z_ref):
        z_ref[...] = DEEPNORM_ALPHA * x_ref[...] + _nn(y_ref[...], w_ref[...])

    return _pcall(
        body, name=name, grid=(S_ // tm,),
        in_specs=[pl.BlockSpec((tm, MIX_WIDTH), lambda i: (i, 0)),
                  pl.BlockSpec((MIX_WIDTH, D), lambda i: (0, 0)),
                  pl.BlockSpec((tm, D), lambda i: (i, 0))],
        out_specs=pl.BlockSpec((tm, D), lambda i: (i, 0)),
        out_shape=jax.ShapeDtypeStruct((S_, D), F32),
        compiler_params=_cp(("parallel",)),
    )(ymix, wo, x)


def _ln_stats(z):
    mu = jnp.mean(z, -1, keepdims=True)
    zc = z - mu
    var = jnp.mean(zc * zc, -1, keepdims=True)
    rstd = lax.rsqrt(var + LN_EPS)
    return zc * rstd, rstd


def _ln_fwd(z, g, b, *, tm, name):
    S_, D = z.shape

    def body(z_ref, g_ref, b_ref, y_ref, yb_ref):
        xh, _ = _ln_stats(z_ref[...])
        y = xh * g_ref[...] + b_ref[...]
        y_ref[...] = y
        yb_ref[...] = y.astype(BF16)

    row = pl.BlockSpec((tm, D), lambda i: (i, 0))
    vec = pl.BlockSpec((1, D), lambda i: (0, 0))
    return _pcall(
        body, name=name, grid=(S_ // tm,), in_specs=[row, vec, vec], out_specs=[row, row],
        out_shape=[jax.ShapeDtypeStruct((S_, D), F32), jax.ShapeDtypeStruct((S_, D), BF16)],
        compiler_params=_cp(("parallel",)),
    )(z, g.reshape(1, D), b.reshape(1, D))


def _ln_bwd(z, g, b, other, *, from_target, tm, name):
    S_, D = z.shape

    def body(z_ref, g_ref, b_ref, o_ref, dz_ref, dzb_ref, dg_ref, db_ref, loss_ref):
        i = pl.program_id(0)

        @pl.when(i == 0)
        def _():
            dg_ref[...] = jnp.zeros_like(dg_ref)
            db_ref[...] = jnp.zeros_like(db_ref)
            loss_ref[...] = jnp.zeros_like(loss_ref)

        xh, rstd = _ln_stats(z_ref[...])
        gam = g_ref[...]
        if from_target:
            err = xh * gam + b_ref[...] - o_ref[...]
            per_tok = jnp.mean(err * err, -1, keepdims=True)
            loss_ref[...] += 0.5 * jnp.sum(per_tok, 0, keepdims=True)
            dy = err * (1.0 / D)
        else:
            dy = o_ref[...]
        dxh = dy * gam
        m1 = jnp.mean(dxh, -1, keepdims=True)
        m2 = jnp.mean(dxh * xh, -1, keepdims=True)
        dz = rstd * (dxh - m1 - xh * m2)
        dz_ref[...] = dz
        dzb_ref[...] = dz.astype(BF16)
        dg_ref[...] += jnp.sum(dy * xh, 0, keepdims=True)
        db_ref[...] += jnp.sum(dy, 0, keepdims=True)

    row = pl.BlockSpec((tm, D), lambda i: (i, 0))
    vec = pl.BlockSpec((1, D), lambda i: (0, 0))
    one = pl.BlockSpec((1, 1), lambda i: (0, 0))
    return _pcall(
        body, name=name, grid=(S_ // tm,), in_specs=[row, vec, vec, row],
        out_specs=[row, row, vec, vec, one],
        out_shape=[jax.ShapeDtypeStruct((S_, D), F32), jax.ShapeDtypeStruct((S_, D), BF16),
                   jax.ShapeDtypeStruct((1, D), F32), jax.ShapeDtypeStruct((1, D), F32),
                   jax.ShapeDtypeStruct((1, 1), F32)],
        compiler_params=_cp(("arbitrary",)),
    )(z, g.reshape(1, D), b.reshape(1, D), other)


def _attn_masks(i, sk_ref):
    ri = lax.broadcasted_iota(jnp.int32, (512, 256), 0)
    cj = lax.broadcasted_iota(jnp.int32, (512, 256), 1)
    diff = (ri & 127) - cj + 128
    band = (diff >= 0) & (diff < WINDOW)
    bias = jnp.where(band, 0.0, -jnp.inf)
    bias0 = jnp.where(band & ((i > 0) | (cj >= 128)), 0.0, -jnp.inf)
    grp = lax.broadcasted_iota(jnp.int32, (512, 1), 0) >> 7
    skvs = []
    for h in range(A_KV_HEADS):
        skv = jnp.zeros((512, 1), F32)
        for g in range(4):
            skv = jnp.where(grp == g, sk_ref[h * 4 + g], skv)
        skvs.append(skv)
    return bias0, bias, skvs


def _attn_common(masks, b, h, qr, kd, vd):
    lane = lax.broadcasted_iota(jnp.int32, (1, 128), 1)
    lof = (lane < 64).astype(F32)
    hif = 1.0 - lof
    r0 = b * 128
    skv = masks[2][h]
    pairs = [qr[r0:r0 + 128, h * 256 + p * 128:h * 256 + (p + 1) * 128] for p in (0, 1)]
    qs = _b(jnp.concatenate([pairs[0] * lof, pairs[0] * hif, pairs[1] * lof, pairs[1] * hif], 0))
    k2 = kd[h][r0:r0 + 256]
    v2 = vd[h][r0:r0 + 256]
    s = _nt(qs, k2) * (A_HEAD_DIM ** -0.5) + (masks[0] if b == 0 else masks[1])
    m = jnp.maximum(jnp.max(s, 1, keepdims=True), skv)
    p = jnp.exp(s - m)
    esk = jnp.exp(skv - m)
    rz = 1.0 / (jnp.sum(p, 1, keepdims=True) + esk)
    prob = p * rz
    o4 = _nn(_b(prob), v2)
    return lof, hif, qs, k2, v2, prob, esk * rz, o4


def _attn_prep(T, q_ref, k_ref, v_ref, c_ref, s_ref, kprev, vprev):
    C = c_ref[...]
    Sg = s_ref[...]
    C4 = jnp.concatenate([C] * 4, 1)
    S4 = jnp.concatenate([Sg] * 4, 1)
    q = q_ref[...]
    qr = q * C4 + _rot(q) * S4
    k = k_ref[...]
    kr = k * C + _rot(k) * Sg
    v = v_ref[...]
    kext = jnp.concatenate([kprev[...], kr], 0)
    vext = jnp.concatenate([vprev[...], v], 0)
    kprev[...] = kr[T - 128:]
    vprev[...] = v[T - 128:]
    lo = lax.broadcasted_iota(jnp.int32, (1, 128), 1) < 64
    kroll = pltpu.roll(kext, 64, 1)
    vroll = pltpu.roll(vext, 64, 1)
    kd = [_b(jnp.where(lo, kext, kroll)), _b(jnp.where(lo, kroll, kext))]
    vd = [_b(jnp.where(lo, vext, vroll)), _b(jnp.where(lo, vroll, vext))]
    return C, Sg, C4, S4, qr, kd, vd


def _attn_specs(T):
    return [pl.BlockSpec(memory_space=pltpu.SMEM),
            pl.BlockSpec((T, 512), lambda i: (i, OFF_AQ // 512)),
            pl.BlockSpec((T, 512), lambda i: (i, OFF_AZ // 512)),
            pl.BlockSpec((T, 128), lambda i: (i, OFF_AK // 128)),
            pl.BlockSpec((T, 128), lambda i: (i, OFF_AV // 128)),
            pl.BlockSpec((T, 128), lambda i: (i, 0)),
            pl.BlockSpec((T, 128), lambda i: (i, 0))]


def _attn_fwd(proj, rope_c, rope_s, sinks, ymix, *, T, name):
    S_ = proj.shape[0]
    nb = T // 128

    def body(sk_ref, q_ref, z_ref, k_ref, v_ref, c_ref, s_ref, _, y_ref, kprev, vprev):
        i = pl.program_id(0)

        @pl.when(i == 0)
        def _():
            kprev[...] = jnp.zeros_like(kprev)
            vprev[...] = jnp.zeros_like(vprev)

        _, _, _, _, qr, kd, vd = _attn_prep(T, q_ref, k_ref, v_ref, c_ref, s_ref, kprev, vprev)
        masks = _attn_masks(i, sk_ref)
        for b in range(nb):
            r0 = b * 128
            for h in range(2):
                lof, hif, _, _, _, _, _, o4 = _attn_common(masks, b, h, qr, kd, vd)
                for p in range(2):
                    cs = slice(h * 256 + p * 128, h * 256 + (p + 1) * 128)
                    o = o4[2 * p * 128:(2 * p + 1) * 128] * lof + o4[(2 * p + 1) * 128:(2 * p + 2) * 128] * hif
                    y_ref[r0:r0 + 128, cs] = (o * _silu(z_ref[r0:r0 + 128, cs])).astype(BF16)

    return _pcall(
        body, name=name, grid=(S_ // T,), in_specs=_attn_specs(T) + [ANY_SPEC],
        out_specs=pl.BlockSpec((T, 512), lambda i: (i, MIX_A // 512)),
        out_shape=jax.ShapeDtypeStruct(ymix.shape, BF16),
        input_output_aliases={7: 0},
        scratch_shapes=[pltpu.VMEM((128, 128), F32), pltpu.VMEM((128, 128), F32)],
        compiler_params=_cp(("arbitrary",)),
    )(sinks, proj, proj, proj, proj, rope_c, rope_s, ymix)


def _attn_bwd(proj, rope_c, rope_s, sinks, dymix, *, T, name):
    S_ = proj.shape[0]
    nb = T // 128
    nt = S_ // T

    def body(sk_ref, q_ref, z_ref, k_ref, v_ref, c_ref, s_ref, dy_ref,
             dp_ref, dk_ref, dv_ref, dkt_ref, dvt_ref, dsk_ref,
             kprev, vprev, cprev, sprev, dkacc, dvacc, dqacc):
        i = pl.program_id(0)

        @pl.when(i == 0)
        def _():
            kprev[...] = jnp.zeros_like(kprev)
            vprev[...] = jnp.zeros_like(vprev)
            cprev[...] = jnp.zeros_like(cprev)
            sprev[...] = jnp.zeros_like(sprev)
            dkacc[...] = jnp.zeros_like(dkacc)
            dvacc[...] = jnp.zeros_like(dvacc)
            dsk_ref[...] = jnp.zeros_like(dsk_ref)

        @pl.when(i > 0)
        def _():
            dkacc[0:128, :] = dkacc[T:T + 128, :]
            dvacc[0:128, :] = dvacc[T:T + 128, :]
            dkacc[128:, :] = jnp.zeros((T, 128), F32)
            dvacc[128:, :] = jnp.zeros((T, 128), F32)

        C, Sg, C4, S4, qr, kd, vd = _attn_prep(T, q_ref, k_ref, v_ref, c_ref, s_ref, kprev, vprev)
        masks = _attn_masks(i, sk_ref)
        lane = lax.broadcasted_iota(jnp.int32, (1, 128), 1)
        for b in range(nb):
            r0 = b * 128
            for h in range(2):
                lof, hif, qs, k2, v2, prob, psink, o4 = _attn_common(masks, b, h, qr, kd, vd)
                dos = []
                for p in range(2):
                    cs = slice(h * 256 + p * 128, h * 256 + (p + 1) * 128)
                    o = o4[2 * p * 128:(2 * p + 1) * 128] * lof + o4[(2 * p + 1) * 128:(2 * p + 2) * 128] * hif
                    zc = z_ref[r0:r0 + 128, cs]
                    dyc = dy_ref[r0:r0 + 128, cs]
                    dp_ref[r0:r0 + 128, 512 + cs.start:512 + cs.stop] = (dyc * o * _dsilu(zc)).astype(BF16)
                    do = dyc * _silu(zc)
                    dos += [do * lof, do * hif]
                dos = jnp.concatenate(dos, 0)
                os_ = jnp.concatenate([o4[0:128] * lof, o4[128:256] * hif, o4[256:384] * lof, o4[384:512] * hif], 0)
                delta = jnp.sum(dos * os_, 1, keepdims=True)
                dosb = _b(dos)
                dp = _nt(dosb, v2)
                ds = prob * (dp - delta)
                dsv = -psink * delta
                for g in range(4):
                    sg = jnp.sum(dsv[g * 128:(g + 1) * 128], 0, keepdims=True)
                    hd = h * 4 + g
                    dsk_ref[hd:hd + 1, :] += jnp.broadcast_to(sg, (1, 128))
                dsb = _b(ds * (A_HEAD_DIM ** -0.5))
                dqs = _nn(dsb, k2)
                for p in range(2):
                    cs = slice(h * 256 + p * 128, h * 256 + (p + 1) * 128)
                    dqacc[r0:r0 + 128, cs] = (dqs[2 * p * 128:(2 * p + 1) * 128] * lof
                                              + dqs[(2 * p + 1) * 128:(2 * p + 2) * 128] * hif)
                dkdup = _tn(dsb, qs)
                dvdup = _tn(_b(prob), dosb)
                half = (lane < 64) if h == 0 else (lane >= 64)
                dkacc[r0:r0 + 256, :] += jnp.where(half, dkdup + pltpu.roll(dkdup, 64, 1), 0.0)
                dvacc[r0:r0 + 256, :] += jnp.where(half, dvdup + pltpu.roll(dvdup, 64, 1), 0.0)
        dqr = dqacc[...]
        dp_ref[:, 0:512] = (dqr * C4 + _rot(dqr * S4)).astype(BF16)
        cext = jnp.concatenate([cprev[...], C], 0)
        sext = jnp.concatenate([sprev[...], Sg], 0)
        dke = dkacc[...]
        dkp = dke * cext + _rot(dke * sext)
        dk_ref[...] = dkp[0:T].astype(BF16)
        dkt_ref[...] = dkp[T:T + 128].astype(BF16)
        dve = dvacc[...]
        dv_ref[...] = dve[0:T].astype(BF16)
        dvt_ref[...] = dve[T:T + 128].astype(BF16)
        cprev[...] = C[T - 128:]
        sprev[...] = Sg[T - 128:]

    nar = pl.BlockSpec((T, 128), lambda i: (i, 0))
    tail = pl.BlockSpec((128, 128), lambda i: (0, 0))
    return _pcall(
        body, name=name, grid=(nt,),
        in_specs=_attn_specs(T) + [pl.BlockSpec((T, 512), lambda i: (i, MIX_A // 512))],
        out_specs=[pl.BlockSpec((T, 1024), lambda i: (i, OFF_AQ // 1024)), nar, nar, tail, tail,
                   pl.BlockSpec((8, 128), lambda i: (0, 0))],
        out_shape=[jax.ShapeDtypeStruct((S_, NP), BF16),
                   jax.ShapeDtypeStruct((S_, 128), BF16), jax.ShapeDtypeStruct((S_, 128), BF16),
                   jax.ShapeDtypeStruct((128, 128), BF16), jax.ShapeDtypeStruct((128, 128), BF16),
                   jax.ShapeDtypeStruct((8, 128), F32)],
        scratch_shapes=[pltpu.VMEM((128, 128), F32)] * 4
        + [pltpu.VMEM((T + 128, 128), F32), pltpu.VMEM((T + 128, 128), F32), pltpu.VMEM((T, 512), F32)],
        compiler_params=_cp(("arbitrary",)),
    )(sinks, proj, proj, proj, proj, rope_c, rope_s, dymix)


def _rg_gates(xr, wa_ref, ba_ref, wx_ref, bx_ref, lam_ref):
    xb = _b(xr)
    pre_a = jnp.concatenate([_nn(xb[:, n * 128:(n + 1) * 128], wa_ref[n]) for n in range(R_BLOCKS)], 1) + ba_ref[...]
    pre_x = jnp.concatenate([_nn(xb[:, n * 128:(n + 1) * 128], wx_ref[n]) for n in range(R_BLOCKS)], 1) + bx_ref[...]
    r = _sigmoid(pre_a)
    ig = _sigmoid(pre_x)
    sp = _softplus(-lam_ref[...])
    log_a = -R_C * r * sp
    a = jnp.exp(log_a)
    mult = jnp.sqrt(_one_minus_exp(2.0 * log_a))
    return xb, r, ig, sp, a, mult


def _rg_param_specs():
    C = R_WIDTH
    vec = pl.BlockSpec((1, C), lambda i: (0, 0))
    blk = pl.BlockSpec((R_BLOCKS, 128, 128), lambda i: (0, 0, 0))
    return [pl.BlockSpec((CONV_WIDTH, C), lambda i: (0, 0)), vec, blk, vec, blk, vec, vec]


def _rglru_fwd(proj, cw, cb, wa, ba, wx, bx, lam, *, T, name):
    S_ = proj.shape[0]
    C = R_WIDTH

    def body(rx_ref, rz_ref, cw_ref, cb_ref, wa_ref, ba_ref, wx_ref, bx_ref, lam_ref,
             h_ref, y_ref, halo, hcar):
        i = pl.program_id(0)

        @pl.when(i == 0)
        def _():
            halo[...] = jnp.zeros_like(halo)
            hcar[...] = jnp.zeros_like(hcar)

        rx = rx_ref[...]
        ext = jnp.concatenate([halo[...], rx], 0)
        halo[...] = rx[T - 8:]
        taps = _conv_taps(ext, T)
        xr = cb_ref[...] + sum(cw_ref[k:k + 1, :] * taps[k] for k in range(CONV_WIDTH))
        _, _, ig, _, a, mult = _rg_gates(xr, wa_ref, ba_ref, wx_ref, bx_ref, lam_ref)
        u = mult * (ig * xr)
        acum, hloc = _scan_lin(a, u, False)
        h = hloc + acum * hcar[0:1, :]
        hcar[...] = jnp.broadcast_to(h[T - 1:T, :], (8, C))
        h_ref[...] = h
        y_ref[...] = (h * _silu(rz_ref[...])).astype(BF16)

    row = pl.BlockSpec((T, C), lambda i: (i, 0))
    return _pcall(
        body, name=name, grid=(S_ // T,),
        in_specs=[pl.BlockSpec((T, C), lambda i: (i, OFF_RX // C)),
                  pl.BlockSpec((T, C), lambda i: (i, OFF_RZ // C))] + _rg_param_specs(),
        out_specs=[row, pl.BlockSpec((T, C), lambda i: (i, MIX_R // C))],
        out_shape=[jax.ShapeDtypeStruct((S_, C), F32), jax.ShapeDtypeStruct((S_, MIX_WIDTH), BF16)],
        scratch_shapes=[pltpu.VMEM((8, C), F32), pltpu.VMEM((8, C), F32)],
        compiler_params=_cp(("arbitrary",)),
    )(proj, proj, cw, cb.reshape(1, C), _b(wa), ba.reshape(1, C), _b(wx), bx.reshape(1, C), lam.reshape(1, C))


def _rglru_bwd(proj, h, dymix, dproj, cw, cb, wa, ba, wx, bx, lam, *, T, name):
    S_ = proj.shape[0]
    C = R_WIDTH
    nt = S_ // T
    t8 = T // 8

    def body(rx_ref, rxp_ref, rz_ref, h_ref, hp_ref, dy_ref,
             cw_ref, cb_ref, wa_ref, ba_ref, wx_ref, bx_ref, lam_ref, wat_ref, wxt_ref,
             _, dp_ref, dcw_ref, dcb_ref, dwa_ref, dba_ref, dwx_ref, dbx_ref, dlam_ref,
             afirst, gfirst, dhalo):
        i = pl.program_id(0)
        first_tile = (i == nt - 1)

        @pl.when(i == 0)
        def _():
            afirst[...] = jnp.zeros_like(afirst)
            gfirst[...] = jnp.zeros_like(gfirst)
            dhalo[...] = jnp.zeros_like(dhalo)
            for r in (dcw_ref, dcb_ref, dwa_ref, dba_ref, dwx_ref, dbx_ref, dlam_ref):
                r[...] = jnp.zeros_like(r)

        keep = jnp.where(first_tile, 0.0, 1.0)
        rx = rx_ref[...]
        ext = jnp.concatenate([rxp_ref[...] * keep, rx], 0)
        taps = _conv_taps(ext, T)
        xr = cb_ref[...] + sum(cw_ref[k:k + 1, :] * taps[k] for k in range(CONV_WIDTH))
        xb, r, ig, sp, a, mult = _rg_gates(xr, wa_ref, ba_ref, wx_ref, bx_ref, lam_ref)
        hh = h_ref[...]
        rz = rz_ref[...]
        dy = dy_ref[...]
        dp_ref[:, C:2 * C] = (dy * hh * _dsilu(rz)).astype(BF16)
        dh = dy * _silu(rz)
        row = lax.broadcasted_iota(jnp.int32, (T, 1), 0)
        c = jnp.where(row == T - 1, afirst[0:1, :], pltpu.roll(a, T - 1, 0))
        ccum, gloc = _scan_lin(c, dh, True)
        g = gloc + ccum * gfirst[0:1, :]
        afirst[...] = jnp.broadcast_to(a[0:1, :], (8, C))
        gfirst[...] = jnp.broadcast_to(g[0:1, :], (8, C))
        hprev = jnp.where(row == 0, hp_ref[7:8, :] * keep, pltpu.roll(hh, 1, 0))
        da = g * hprev
        gx = ig * xr
        dgx = g * mult
        dmult = g * gx
        dlog_a = da * a - dmult * (a * a) * lax.rsqrt(mult * mult)
        dpre_a = dlog_a * (-R_C * sp) * r * (1.0 - r)
        dpre_x = dgx * xr * ig * (1.0 - ig)
        dlam_ref[...] += jnp.sum(dlog_a * (-R_C * r), 0, keepdims=True) * (-_sigmoid(-lam_ref[...]))
        dab = _b(dpre_a)
        dxb = _b(dpre_x)
        dxr = dgx * ig + jnp.concatenate(
            [_nn(dab[:, n * 128:(n + 1) * 128], wat_ref[n]) + _nn(dxb[:, n * 128:(n + 1) * 128], wxt_ref[n])
             for n in range(R_BLOCKS)], 1)
        for n in range(R_BLOCKS):
            cs = slice(n * 128, (n + 1) * 128)
            dwa_ref[n] += _tn(xb[:, cs], dab[:, cs])
            dwx_ref[n] += _tn(xb[:, cs], dxb[:, cs])
        dba_ref[...] += jnp.sum(dpre_a, 0, keepdims=True)
        dbx_ref[...] += jnp.sum(dpre_x, 0, keepdims=True)
        dcb_ref[...] += jnp.sum(dxr, 0, keepdims=True)
        for k in range(CONV_WIDTH):
            dcw_ref[k:k + 1, :] += jnp.sum(dxr * taps[k], 0, keepdims=True)
        ext2 = jnp.concatenate([dxr, dhalo[...]], 0)
        tt = _conv_taps_t(ext2, T)
        dp_ref[:, 0:C] = sum(cw_ref[k:k + 1, :] * tt[k] for k in range(CONV_WIDTH)).astype(BF16)
        dhalo[...] = dxr[0:8]

    def rev(i):
        return nt - 1 - i

    def prev8(i):
        return jnp.maximum(rev(i) * t8 - 1, 0)

    vec = pl.BlockSpec((1, C), lambda i: (0, 0))
    blk = pl.BlockSpec((R_BLOCKS, 128, 128), lambda i: (0, 0, 0))
    row = pl.BlockSpec((T, C), lambda i: (rev(i), 0))
    wat = _b(jnp.swapaxes(wa, 1, 2))
    wxt = _b(jnp.swapaxes(wx, 1, 2))
    return _pcall(
        body, name=name, grid=(nt,),
        in_specs=[pl.BlockSpec((T, C), lambda i: (rev(i), OFF_RX // C)),
                  pl.BlockSpec((8, C), lambda i: (prev8(i), OFF_RX // C)),
                  pl.BlockSpec((T, C), lambda i: (rev(i), OFF_RZ // C)),
                  row,
                  pl.BlockSpec((8, C), lambda i: (prev8(i), 0)),
                  pl.BlockSpec((T, C), lambda i: (rev(i), MIX_R // C)),
                  ] + _rg_param_specs() + [blk, blk, ANY_SPEC],
        out_specs=[pl.BlockSpec((T, 2 * C), lambda i: (rev(i), OFF_RX // (2 * C))),
                   pl.BlockSpec((CONV_WIDTH, C), lambda i: (0, 0)), vec, blk, vec, blk, vec, vec],
        out_shape=[jax.ShapeDtypeStruct(dproj.shape, BF16),
                   jax.ShapeDtypeStruct((CONV_WIDTH, C), F32), jax.ShapeDtypeStruct((1, C), F32),
                   jax.ShapeDtypeStruct((R_BLOCKS, 128, 128), F32), jax.ShapeDtypeStruct((1, C), F32),
                   jax.ShapeDtypeStruct((R_BLOCKS, 128, 128), F32), jax.ShapeDtypeStruct((1, C), F32),
                   jax.ShapeDtypeStruct((1, C), F32)],
        input_output_aliases={15: 0},
        scratch_shapes=[pltpu.VMEM((8, C), F32)] * 3,
        compiler_params=_cp(("arbitrary",)),
    )(proj, proj, proj, h, h, dymix, cw, cb.reshape(1, C), _b(wa), ba.reshape(1, C), _b(wx), bx.reshape(1, C),
      lam.reshape(1, C), wat, wxt, dproj)


GW3 = 3 * G_WIDTH


def _lane_col(x, lane_idx):
    lane = lax.broadcasted_iota(jnp.int32, (1, x.shape[1]), 1)
    return jnp.sum(jnp.where(lane == lane_idx, x, 0.0), 1, keepdims=True)


def _gdn_pre(ext, T, cw_ref, gba, pv_ref):
    taps = _conv_taps(ext, T)
    c = sum(cw_ref[k:k + 1, :] * taps[k] for k in range(CONV_WIDTH))
    qkv = _silu(c)
    beta = _sigmoid(gba)
    sarg = gba + pv_ref[1:2, :]
    nea = -jnp.exp(pv_ref[0:1, :])
    gdec = nea * _softplus(sarg)
    ri = lax.broadcasted_iota(jnp.int32, (T, T), 0)
    cj = lax.broadcasted_iota(jnp.int32, (T, T), 1)
    same = (ri >> 6) == (cj >> 6)
    ltri = jnp.where((ri >= cj) & same, 1.0, 0.0).astype(BF16)
    gc = _dot_exact_lhs(_nn, ltri, gdec)
    return taps, c, qkv, beta, sarg, nea, gdec, gc


def _gdn_masks():
    ri = lax.broadcasted_iota(jnp.int32, (128, 128), 0)
    cj = lax.broadcasted_iota(jnp.int32, (128, 128), 1)
    same = (ri >> 6) == (cj >> 6)
    return (ri >= cj) & same, (ri > cj) & same, ri == cj


def _lockstep(gens):
    out = [None] * len(gens)
    live = list(range(len(gens)))
    while live:
        still = []
        for k in live:
            try:
                next(gens[k])
                still.append(k)
            except StopIteration as stop:
                out[k] = stop.value
        live = still
    return out


def _gdn_chunk(qkv, beta, gc, rs, h, tm=None):
    tril, strict, eye = _gdn_masks()
    rowi = lax.broadcasted_iota(jnp.int32, (128, 1), 0)
    lane = lax.broadcasted_iota(jnp.int32, (1, 128), 1)
    qh = qkv[rs, h * 128:(h + 1) * 128]
    kh = qkv[rs, 512 + h * 128:512 + (h + 1) * 128]
    vh = qkv[rs, 1024 + h * 128:1024 + (h + 1) * 128]
    rq = lax.rsqrt(jnp.sum(qh * qh, 1, keepdims=True) + RMS_EPS)
    rk = lax.rsqrt(jnp.sum(kh * kh, 1, keepdims=True) + RMS_EPS)
    qn = qh * (rq * (G_HEAD_DIM ** -0.5))
    kn = kh * rk
    gcb = gc[rs]
    gcol = _lane_col(gcb, 4 + h)
    bcol = _lane_col(beta[rs], h)
    grow = _dot_exact_lhs(_nt, jnp.ones((128, 128), BF16), jnp.where(lane == 4 + h, gcb, 0.0))
    D = jnp.where(tril, jnp.exp(jnp.minimum(gcol - grow, 0.0)), 0.0)
    kb = kn * bcol
    vb = vh * bcol
    knb = _b(kn)
    A = _nt(_b(kb), knb)
    Bm = _nt(_b(qn), knb)
    yield
    if tm is None:
        N = jnp.where(strict, -(A * D), 0.0)
        tm = jnp.where(eye, 1.0, 0.0) + N
        npow = N
        for _ in range(5):
            npow = _dot3(_nn, npow, npow)
            yield
            tm = tm + _dot3(_nn, tm, npow)
            yield
    eg = jnp.exp(gcol)
    u = _dot3(_nn, tm, vb)
    w = _dot3(_nn, tm, kb * eg)
    yield
    qk = jnp.where(tril, Bm * D, 0.0)
    qd = qn * eg
    gla = jnp.sum(jnp.where(rowi == 63, gcol, 0.0), 0, keepdims=True)
    glb = jnp.sum(jnp.where(rowi == 127, gcol, 0.0), 0, keepdims=True)
    ed = jnp.exp(jnp.where(rowi < 64, gla, glb) - gcol)
    kd = kn * ed
    return dict(qh=qh, kh=kh, vh=vh, rq=rq, rk=rk, qn=qn, kn=kn, gcol=gcol, bcol=bcol, D=D, A=A, Bm=Bm,
                tm=tm, eg=eg, ed=ed, u=u, w=w, qk=qk, qd=qd, kd=kd, kb=kb, vb=vb,
                gla=jnp.exp(gla), glb=jnp.exp(glb))


def _gdn_scan(q, sa):
    sab = _b(sa)
    wb = _b(q["w"])
    vna = q["u"] - _nn(wb, sab)
    yield
    sb = sa * q["gla"] + _tn(_b(q["kd"][0:64]), _b(vna[0:64]))
    yield
    sbb = _b(sb)
    vnb = q["u"] - _nn(wb, sbb)
    yield
    sn = sb * q["glb"] + _tn(_b(q["kd"][64:128]), _b(vnb[64:128]))
    yield
    vn = jnp.concatenate([vna[0:64], vnb[64:128]], 0)
    qdb = _b(q["qd"])
    o = jnp.concatenate([_nn(qdb[0:64], sab), _nn(qdb[64:128], sbb)], 0) + _nn(_b(q["qk"]), _b(vn))
    return sb, sn, vn, o


def _gdn_param_specs():
    return [pl.BlockSpec((CONV_WIDTH, GW3), lambda i: (0, 0)),
            pl.BlockSpec((8, 128), lambda i: (0, 0)),
            pl.BlockSpec((1, 128), lambda i: (0, 0))]


def _gdn_pvec(a_log, dt_bias):
    z = jnp.zeros((8, 128), F32)
    return z.at[0, 4:8].set(a_log).at[1, 4:8].set(dt_bias)


def _gdn_fwd(proj, cw, a_log, dt_bias, nw, ymix, *, T, name):
    S_ = proj.shape[0]
    nu = T // 128

    def body(x_ref, z_ref, g_ref, cw_ref, pv_ref, nw_ref, _, y_ref, st_ref, tm_ref, halo, state):
        i = pl.program_id(0)

        @pl.when(i == 0)
        def _():
            halo[...] = jnp.zeros_like(halo)
            state[...] = jnp.zeros_like(state)

        x = x_ref[...]
        ext = jnp.concatenate([halo[...], x], 0)
        halo[...] = x[T - 8:]
        _, _, qkv, beta, _, _, _, gc = _gdn_pre(ext, T, cw_ref, g_ref[...], pv_ref)
        items = [(dc, h) for dc in range(nu) for h in range(G_HEADS)]
        qs = _lockstep([_gdn_chunk(qkv, beta, gc, slice(dc * 128, (dc + 1) * 128), h) for dc, h in items])

        def head_chain(h):
            s = state[h]
            for dc in range(nu):
                rs = slice(dc * 128, (dc + 1) * 128)
                q = qs[dc * G_HEADS + h]
                sb, sn, _, o = yield from _gdn_scan(q, s)
                st_ref[2 * dc, h] = s
                st_ref[2 * dc + 1, h] = sb
                tm_ref[dc, h] = q["tm"]
                s = sn
                yield
                rn = lax.rsqrt(jnp.mean(o * o, 1, keepdims=True) + RMS_EPS)
                cs = slice(h * 128, (h + 1) * 128)
                y_ref[rs, cs] = (o * rn * nw_ref[...] * _silu(z_ref[rs, cs])).astype(BF16)
                yield
            state[h] = s

        _lockstep([head_chain(h) for h in range(G_HEADS)])

    return _pcall(
        body, name=name, grid=(S_ // T,),
        in_specs=[pl.BlockSpec((T, GW3), lambda i: (i, OFF_GQKV // GW3)),
                  pl.BlockSpec((T, 512), lambda i: (i, OFF_GZ // 512)),
                  pl.BlockSpec((T, 128), lambda i: (i, OFF_GBA // 128))] + _gdn_param_specs() + [ANY_SPEC],
        out_specs=[pl.BlockSpec((T, 512), lambda i: (i, MIX_G // 512)),
                   pl.BlockSpec((2 * nu, G_HEADS, 128, 128), lambda i: (i, 0, 0, 0)),
                   pl.BlockSpec((nu, G_HEADS, 128, 128), lambda i: (i, 0, 0, 0))],
        out_shape=[jax.ShapeDtypeStruct(ymix.shape, BF16),
                   jax.ShapeDtypeStruct((S_ // 64, G_HEADS, 128, 128), F32),
                   jax.ShapeDtypeStruct((S_ // 128, G_HEADS, 128, 128), F32)],
        input_output_aliases={6: 0},
        scratch_shapes=[pltpu.VMEM((8, GW3), F32), pltpu.VMEM((G_HEADS, 128, 128), F32)],
        compiler_params=_cp(("arbitrary",)),
    )(proj, proj, proj, cw, _gdn_pvec(a_log, dt_bias), nw.reshape(1, 128), ymix)


def _gdn_bwd(proj, states, tms, dymix, dproj, cw, a_log, dt_bias, nw, *, T, name):
    S_ = proj.shape[0]
    nt = S_ // T
    nu = T // 128
    t8 = T // 8

    def body(x_ref, xp_ref, z_ref, g_ref, st_ref, tm_ref, dy_ref, cw_ref, pv_ref, nw_ref, _,
             dp_ref, dg_ref, dcw_ref, dpv_ref, dnw_ref, dstate, dhalo, dqkv, dbg):
        i = pl.program_id(0)
        first_tile = (i == nt - 1)

        @pl.when(i == 0)
        def _():
            dstate[...] = jnp.zeros_like(dstate)
            dhalo[...] = jnp.zeros_like(dhalo)
            dcw_ref[...] = jnp.zeros_like(dcw_ref)
            dpv_ref[...] = jnp.zeros_like(dpv_ref)
            dnw_ref[...] = jnp.zeros_like(dnw_ref)

        keep = jnp.where(first_tile, 0.0, 1.0)
        ext = jnp.concatenate([xp_ref[...] * keep, x_ref[...]], 0)
        G = g_ref[...]
        taps, c, qkv, beta, sarg, nea, gdec, gc = _gdn_pre(ext, T, cw_ref, G, pv_ref)
        tril, strict, _ = _gdn_masks()
        rowi = lax.broadcasted_iota(jnp.int32, (128, 1), 0)
        lane = lax.broadcasted_iota(jnp.int32, (1, 128), 1)
        ones_b = jnp.ones((128, 128), BF16)
        nwv = nw_ref[...]
        items = [(dc, h) for dc in range(nu) for h in range(G_HEADS)]

        def recompute(dc, h):
            q = yield from _gdn_chunk(qkv, beta, gc, slice(dc * 128, (dc + 1) * 128), h, tm=tm_ref[dc, h])
            sa = st_ref[2 * dc, h]
            sb, _, vn, o = yield from _gdn_scan(q, sa)
            return q, sa, sb, vn, o

        fw = _lockstep([recompute(dc, h) for dc, h in items])
        chain_out = {}

        def head_chain(h):
            dS = dstate[h]
            for dc in reversed(range(nu)):
                rs = slice(dc * 128, (dc + 1) * 128)
                q, sa, sb, vn, o = fw[dc * G_HEADS + h]
                cs = slice(h * 128, (h + 1) * 128)
                zg = z_ref[rs, cs]
                dy = dy_ref[rs, cs]
                rn = lax.rsqrt(jnp.mean(o * o, 1, keepdims=True) + RMS_EPS)
                don = dy * _silu(zg)
                dp_ref[rs, GW3 + cs.start:GW3 + cs.stop] = (dy * (o * rn * nwv) * _dsilu(zg)).astype(BF16)
                dnw_ref[...] += jnp.sum(don * o * rn, 0, keepdims=True)
                tt = don * nwv
                do = rn * (tt - o * (rn * rn) * jnp.mean(tt * o, 1, keepdims=True))
                yield
                dob = _b(do)
                sab, sbb = _b(sa), _b(sb)
                vnb16 = _b(vn)
                dqk = jnp.where(tril, _nt(dob, vnb16), 0.0)
                dvn_o = _tn(_b(q["qk"]), dob)
                dSb16 = _b(dS)
                kdb = _b(q["kd"])
                wb = _b(q["w"])
                qdb = _b(q["qd"])
                yield
                dvn_b = dvn_o[64:128] + _nn(kdb[64:128], dSb16)
                dkd_b = _nt(vnb16[64:128], dSb16)
                dgl_b = jnp.sum(jnp.sum(dS * sb, 1, keepdims=True), 0, keepdims=True)
                yield
                dvn_b16 = _b(dvn_b)
                dw_b = -_nt(dvn_b16, sbb)
                dqd_b = _nt(dob[64:128], sbb)
                dSm = q["glb"] * dS + _tn(qdb[64:128], dob[64:128]) - _tn(wb[64:128], dvn_b16)
                yield
                dSm16 = _b(dSm)
                dvn_a = dvn_o[0:64] + _nn(kdb[0:64], dSm16)
                dkd_a = _nt(vnb16[0:64], dSm16)
                dgl_a = jnp.sum(jnp.sum(dSm * sa, 1, keepdims=True), 0, keepdims=True)
                yield
                dvn_a16 = _b(dvn_a)
                dw_a = -_nt(dvn_a16, sab)
                dqd_a = _nt(dob[0:64], sab)
                dS = q["gla"] * dSm + _tn(qdb[0:64], dob[0:64]) - _tn(wb[0:64], dvn_a16)
                chain_out[dc, h] = (dqk, jnp.concatenate([dvn_a, dvn_b], 0), jnp.concatenate([dw_a, dw_b], 0),
                                    jnp.concatenate([dkd_a, dkd_b], 0), jnp.concatenate([dqd_a, dqd_b], 0),
                                    dgl_a, dgl_b)
                yield
            dstate[h] = dS

        _lockstep([head_chain(h) for h in range(G_HEADS)])

        def local(dc, h):
            rs = slice(dc * 128, (dc + 1) * 128)
            q = fw[dc * G_HEADS + h][0]
            dqk, du, dw, dkd, dqd, dgl_a, dgl_b = chain_out[dc, h]
            if True:
                dvb = _dot3(_tn, q["tm"], du)
                dkbe = _dot3(_tn, q["tm"], dw)
                yield
                dM = jnp.where(strict, -(_nt(_b(dvb), _b(q["u"])) + _nt(_b(dkbe), _b(q["w"]))), 0.0)
                yield
                D = q["D"]
                dA = dM * D
                dB = dqk * D
                dDD = (dM * q["A"] + dqk * q["Bm"]) * D
                dh_, dm_, dl_ = _split3(dDD)
                colsum = _tn(dh_, ones_b) + (_tn(dm_, ones_b) + _tn(dl_, ones_b))
                dgc = jnp.sum(dDD, 1, keepdims=True) - _lane_col(colsum, 0)
                yield
                dA16, dB16 = _b(dA), _b(dB)
                knb, kbb, qnb = _b(q["kn"]), _b(q["kb"]), _b(q["qn"])
                eg, ed = q["eg"], q["ed"]
                dkb = _nn(dA16, knb) + dkbe * eg
                dkn = _tn(dA16, kbb) + _tn(dB16, qnb) + dkd * ed + dkb * q["bcol"]
                dqn = _nn(dB16, knb) + dqd * eg
                yield
                deg = jnp.sum(dkbe * q["kb"], 1, keepdims=True) + jnp.sum(dqd * q["qn"], 1, keepdims=True)
                ded = jnp.sum(dkd * q["kn"], 1, keepdims=True) * ed
                dgc = dgc + deg * eg - ded
                tail_a = jnp.sum(jnp.where(rowi < 64, ded, 0.0), 0, keepdims=True) + dgl_a * q["gla"]
                tail_b = jnp.sum(jnp.where(rowi >= 64, ded, 0.0), 0, keepdims=True) + dgl_b * q["glb"]
                dgc = dgc + jnp.where(rowi == 63, tail_a, 0.0) + jnp.where(rowi == 127, tail_b, 0.0)
                dbeta = jnp.sum(dkb * q["kn"], 1, keepdims=True) + jnp.sum(dvb * q["vh"], 1, keepdims=True)
                bcol = q["bcol"]
                blk = jnp.where(lane == h, dbeta * bcol * (1.0 - bcol), 0.0) + jnp.where(lane == 4 + h, dgc, 0.0)
                yield
                sc = G_HEAD_DIM ** -0.5
                rq, rk, qh, kh = q["rq"], q["rk"], q["qh"], q["kh"]
                dqh = sc * (dqn * rq - qh * (rq * rq * rq) * jnp.sum(dqn * qh, 1, keepdims=True))
                dkh = dkn * rk - kh * (rk * rk * rk) * jnp.sum(dkn * kh, 1, keepdims=True)
                dqkv[rs, h * 128:(h + 1) * 128] = dqh
                dqkv[rs, 512 + h * 128:512 + (h + 1) * 128] = dkh
                dqkv[rs, 1024 + h * 128:1024 + (h + 1) * 128] = dvb * bcol
            return blk

        blks = _lockstep([local(dc, h) for dc, h in items])
        for dc in range(nu):
            dbg[dc * 128:(dc + 1) * 128, :] = functools.reduce(
                lambda a, b: a + b, [blks[dc * G_HEADS + h] for h in range(G_HEADS)])
        ri = lax.broadcasted_iota(jnp.int32, (T, T), 0)
        cj = lax.broadcasted_iota(jnp.int32, (T, T), 1)
        utri = jnp.where((ri <= cj) & ((ri >> 6) == (cj >> 6)), 1.0, 0.0).astype(BF16)
        dbgv = dbg[...]
        dgd = _dot_exact_lhs(_nn, utri, dbgv)
        is_g = (lane >= 4) & (lane < 8)
        dga = jnp.where(is_g, dgd * nea * _sigmoid(sarg), 0.0)
        dg_ref[...] = jnp.where(lane < 4, dbgv, dga).astype(BF16)
        dpv_ref[0:1, :] += jnp.sum(jnp.where(is_g, dgd * gdec, 0.0), 0, keepdims=True)
        dpv_ref[1:2, :] += jnp.sum(dga, 0, keepdims=True)
        dc_ = dqkv[...] * _dsilu(c)
        for k in range(CONV_WIDTH):
            dcw_ref[k:k + 1, :] += jnp.sum(dc_ * taps[k], 0, keepdims=True)
        ext2 = jnp.concatenate([dc_, dhalo[...]], 0)
        tt2 = _conv_taps_t(ext2, T)
        dp_ref[:, 0:GW3] = sum(cw_ref[k:k + 1, :] * tt2[k] for k in range(CONV_WIDTH)).astype(BF16)
        dhalo[...] = dc_[0:8]

    def rev(i):
        return nt - 1 - i

    def prev8(i):
        return jnp.maximum(rev(i) * t8 - 1, 0)

    return _pcall(
        body, name=name, grid=(nt,),
        in_specs=[pl.BlockSpec((T, GW3), lambda i: (rev(i), OFF_GQKV // GW3)),
                  pl.BlockSpec((8, GW3), lambda i: (prev8(i), OFF_GQKV // GW3)),
                  pl.BlockSpec((T, 512), lambda i: (rev(i), OFF_GZ // 512)),
                  pl.BlockSpec((T, 128), lambda i: (rev(i), OFF_GBA // 128)),
                  pl.BlockSpec((2 * nu, G_HEADS, 128, 128), lambda i: (rev(i), 0, 0, 0)),
                  pl.BlockSpec((nu, G_HEADS, 128, 128), lambda i: (rev(i), 0, 0, 0)),
                  pl.BlockSpec((T, 512), lambda i: (rev(i), MIX_G // 512))] + _gdn_param_specs() + [ANY_SPEC],
        out_specs=[pl.BlockSpec((T, GW3 + 512), lambda i: (rev(i), OFF_GQKV // (GW3 + 512))),
                   pl.BlockSpec((T, 128), lambda i: (rev(i), 0)),
                   pl.BlockSpec((CONV_WIDTH, GW3), lambda i: (0, 0)),
                   pl.BlockSpec((8, 128), lambda i: (0, 0)),
                   pl.BlockSpec((1, 128), lambda i: (0, 0))],
        out_shape=[jax.ShapeDtypeStruct(dproj.shape, BF16),
                   jax.ShapeDtypeStruct((S_, 128), BF16), jax.ShapeDtypeStruct((CONV_WIDTH, GW3), F32),
                   jax.ShapeDtypeStruct((8, 128), F32), jax.ShapeDtypeStruct((1, 128), F32)],
        input_output_aliases={10: 0},
        scratch_shapes=[pltpu.VMEM((G_HEADS, 128, 128), F32), pltpu.VMEM((8, GW3), F32),
                        pltpu.VMEM((T, GW3), F32), pltpu.VMEM((T, 128), F32)],
        compiler_params=_cp(("arbitrary",)),
    )(proj, proj, proj, proj, states, tms, dymix, cw, _gdn_pvec(a_log, dt_bias), nw.reshape(1, 128), dproj)


def _pair_sum_windows(a, b, nsh, width, *, out_dtype, name):
    R_, C = a.shape
    hr = R_ // 2
    nb = width // 128
    assert (3 * nsh) // 128 + nb <= C // 128
    to_perm = _orig_block_to_perm()
    table = jnp.asarray([to_perm[(nsh * t) // 128 + j] for t in range(4) for j in range(nb)], jnp.int32)

    def body(tab_ref, a0_ref, a1_ref, b_ref, o_ref):
        mine = jnp.where(lax.axis_index("c") == 0, a0_ref[...], a1_ref[...])
        o_ref[...] = (mine + b_ref[...]).astype(o_ref.dtype)

    def spec(half):
        return pl.BlockSpec((hr, 128), lambda t, j, tab: (half, tab[t * nb + j]))

    return _pcall(
        body, name=name,
        grid_spec=pltpu.PrefetchScalarGridSpec(
            num_scalar_prefetch=1, grid=(4, nb), in_specs=[spec(0), spec(1), spec(0)],
            out_specs=pl.BlockSpec((None, hr, 128), lambda t, j, tab: (t, 0, j))),
        out_shape=jax.ShapeDtypeStruct((4, hr, width), out_dtype),
        compiler_params=_cp(("parallel", "parallel")))(table, a, a, b)


def _pair_sum_blocks(a, b, *, out_dtype, name):
    L, R_, C = a.shape
    hr = R_ // 2

    def body(a0_ref, a1_ref, b_ref, o_ref):
        mine = jnp.where(lax.axis_index("c") == 0, a0_ref[...], a1_ref[...])
        o_ref[...] = (mine + b_ref[...]).astype(o_ref.dtype)

    def spec(half):
        return pl.BlockSpec((None, hr, C), lambda t: (t, half, 0))

    return _pcall(body, name=name, grid=(L,), in_specs=[spec(0), spec(1), spec(0)], out_specs=spec(0),
                  out_shape=jax.ShapeDtypeStruct((L, hr, C), out_dtype),
                  compiler_params=_cp(("parallel",)))(a, a, b)


def _add_mine(a0, a1, b, *, out_dtype, tr, name):
    R_, C = b.shape

    def body(a0_ref, a1_ref, b_ref, o_ref):
        mine = jnp.where(lax.axis_index("c") == 0, a0_ref[...], a1_ref[...])
        o_ref[...] = (mine + b_ref[...]).astype(o_ref.dtype)

    spec = pl.BlockSpec((tr, C), lambda i: (i, 0))
    return _pcall(body, name=name, grid=(R_ // tr,), in_specs=[spec] * 3, out_specs=spec,
                  out_shape=jax.ShapeDtypeStruct((R_, C), out_dtype), compiler_params=_cp(("parallel",)))(a0, a1, b)


def _sum4(a, mine, *, tr, name):
    _, R_, C = a.shape

    def body(a_ref, m_ref, o_ref):
        s = 2 * lax.axis_index("x") + lax.axis_index("y")
        mv = m_ref[...].astype(F32)
        p = [jnp.where(s == t, mv, a_ref[t].astype(F32)) for t in range(4)]
        o_ref[...] = ((p[0] + p[1]) + p[2]) + p[3]

    return _pcall(body, name=name, grid=(R_ // tr,),
                  in_specs=[pl.BlockSpec((4, tr, C), lambda i: (0, i, 0)), pl.BlockSpec((tr, C), lambda i: (i, 0))],
                  out_specs=pl.BlockSpec((tr, C), lambda i: (i, 0)),
                  out_shape=jax.ShapeDtypeStruct((R_, C), F32), compiler_params=_cp(("parallel",)))(a, mine)


def _adamw_refs(w_ref, g_ref, m_ref, v_ref, d_ref, mo_ref, vo_ref):
    c1 = 1.0 / (1.0 - ADAM_B1 ** ADAM_STEP)
    c2 = 1.0 / (1.0 - ADAM_B2 ** ADAM_STEP)
    gg = g_ref[...]
    mn = ADAM_B1 * m_ref[...] + (1.0 - ADAM_B1) * gg
    vn = ADAM_B2 * v_ref[...] + (1.0 - ADAM_B2) * (gg * gg)
    mo_ref[...] = mn
    vo_ref[...] = vn
    d_ref[...] = -ADAM_LR * ((mn * c1) / (jnp.sqrt(vn * c2) + ADAM_EPS) + ADAM_WD * w_ref[...])


def _adamw_many(ws, gs, ms, vs, *, name):
    n = len(ws)

    def body(*refs):
        for k in range(n):
            _adamw_refs(*[refs[q * n + k] for q in range(7)])

    vm = pl.BlockSpec(memory_space=pltpu.VMEM)
    shp = [jax.ShapeDtypeStruct(w.shape, F32) for w in ws]
    outs = _pcall(body, name=name, in_specs=[vm] * (4 * n), out_specs=[vm] * (3 * n), out_shape=shp * 3,
                  compiler_params=pltpu.CompilerParams(vmem_limit_bytes=VMEM_LIMIT))(*ws, *gs, *ms, *vs)
    return outs[:n], outs[n:2 * n], outs[2 * n:]


def _adamw(w, g, m, v, *, tr, name):
    L, R_, C = w.shape
    body = functools.partial(_adamw_refs)

    spec = pl.BlockSpec((None, tr, C), lambda l, i: (l, i, 0))
    shp = jax.ShapeDtypeStruct((L, R_, C), F32)
    return _pcall(body, name=name, grid=(L, R_ // tr), in_specs=[spec] * 4, out_specs=[spec] * 3,
                  out_shape=[shp] * 3, compiler_params=_cp(("parallel", "parallel")))(w, g, m, v)


def _adamw_cols(w, g, m, v, *, name):
    C, L, R_ = w.shape
    tc = C // 2 if C % 2 == 0 else C
    spec = pl.BlockSpec((tc, L, 128), lambda i, j: (i, 0, j))
    shp = jax.ShapeDtypeStruct((C, L, R_), F32)
    return _pcall(functools.partial(_adamw_refs), name=name, grid=(C // tc, R_ // 128), in_specs=[spec] * 4,
                  out_specs=[spec] * 3, out_shape=[shp] * 3,
                  compiler_params=_cp(("parallel", "parallel")))(w, g, m, v)


HBM_SPEC = pl.BlockSpec(memory_space=pltpu.HBM)


def _place():
    x, y, c = lax.axis_index("x"), lax.axis_index("y"), lax.axis_index("c")
    chips = [(1 - x, y), (x, 1 - y), (1 - x, 1 - y)]
    return x, y, c, 2 * x + y, chips, [2 * cx + cy for cx, cy in chips], (x, y, 1 - c)


def _remote(src, dst, ssem, rsem, dev):
    return pltpu.make_async_remote_copy(src_ref=src, dst_ref=dst, send_sem=ssem, recv_sem=rsem,
                                        device_id=dev, device_id_type=MESH)


def _row_half(ref, lead, hc):
    hl = ref.shape[-2] // 2
    return ref.at[lead, pl.ds(hc * hl, hl), :]


def _gather_side(win, wout, layer):
    def copies(ins, outs, ssem, rsem):
        x, y, c, s, chips, sid, sib = _place()
        cps = []
        for j, chip in enumerate(chips):
            dev = (*chip, c)
            cps.append(_remote(_row_half(ins[0], layer, c), _row_half(outs[0], s, c), ssem.at[j], rsem.at[j], dev))
            cps.append(_remote(_row_half(ins[1], layer, c), _row_half(outs[1], s, c), ssem.at[3 + j], rsem.at[3 + j],
                               dev))
        return cps, c, sid, sib

    def start(ins, outs, ssem, rsem):
        for cp in copies(ins, outs, ssem, rsem)[0]:
            cp.start()

    def finish(ins, outs, ssem, rsem):
        cps, c, sid, sib = copies(ins, outs, ssem, rsem)
        for j in range(3):
            for k in range(2):
                got = _row_half(outs[k], sid[j], c)
                _remote(got, got, ssem.at[3 * k + j], rsem.at[3 * k + j], sib).wait_recv()
        for cp in cps:
            cp.wait_send()

    shapes = [jax.ShapeDtypeStruct((4,) + win.shape[1:], win.dtype), jax.ShapeDtypeStruct((4,) + wout.shape[1:], wout.dtype)]
    return _Side([win, wout], shapes, 6, start, finish)


def _gather_join(gin, gout, name):
    def body(gin_in, gout_in, gin_ref, gout_ref, ssem, rsem):
        x, y, c, s, chips, sid, sib = _place()
        cps = []
        for j in range(3):
            for k, ref in enumerate((gin_ref, gout_ref)):
                mine = _row_half(ref, sid[j], c)
                cps.append(_remote(mine, mine, ssem.at[3 * k + j], rsem.at[3 * k + j], sib))
        for cp in cps:
            cp.start()
        for j in range(3):
            for k, ref in enumerate((gin_ref, gout_ref)):
                other = _row_half(ref, sid[j], 1 - c)
                _remote(other, other, ssem.at[3 * k + j], rsem.at[3 * k + j], sib).wait_recv()
        for cp in cps:
            cp.wait_send()

    return _pcall(
        body, name=name, in_specs=[HBM_SPEC] * 2, out_specs=[HBM_SPEC] * 2,
        out_shape=[jax.ShapeDtypeStruct(gin.shape, gin.dtype), jax.ShapeDtypeStruct(gout.shape, gout.dtype)],
        input_output_aliases={0: 0, 1: 1},
        scratch_shapes=[pltpu.SemaphoreType.DMA((6,)), pltpu.SemaphoreType.DMA((6,))],
    )(gin, gout)


def _gather_layer0(win, wout, conv):
    def body(win_ref, wout_ref, cv_ref, gin_ref, gout_ref, gcv_ref, ssem, rsem):
        x, y, c, s, chips, sid, sib = _place()

        def in_half(slot, hc):
            return _row_half(gin_ref, slot, hc)

        def out_half(slot, hc):
            return _row_half(gout_ref, slot, hc)

        sends = []
        for j, chip in enumerate(chips):
            dev = (*chip, c)
            sends.append(_remote(_row_half(win_ref, 0, c), in_half(s, c), ssem.at[j], rsem.at[j], dev))
            sends.append(_remote(_row_half(wout_ref, 0, c), out_half(s, c), ssem.at[3 + j], rsem.at[3 + j], dev))
            sends.append(_remote(cv_ref, gcv_ref.at[s], ssem.at[6 + j], rsem.at[6 + j], dev))
        for cp in sends:
            cp.start()
        for j in range(3):
            _remote(in_half(sid[j], c), in_half(sid[j], c), ssem.at[j], rsem.at[j], sib).wait_recv()
            f = _remote(in_half(sid[j], c), in_half(sid[j], c), ssem.at[9 + j], rsem.at[9 + j], sib)
            f.start()
            sends.append(f)
            _remote(out_half(sid[j], c), out_half(sid[j], c), ssem.at[3 + j], rsem.at[3 + j], sib).wait_recv()
            f = _remote(out_half(sid[j], c), out_half(sid[j], c), ssem.at[12 + j], rsem.at[12 + j], sib)
            f.start()
            sends.append(f)
        for j in range(3):
            _remote(in_half(sid[j], 1 - c), in_half(sid[j], 1 - c), ssem.at[9 + j], rsem.at[9 + j], sib).wait_recv()
            _remote(out_half(sid[j], 1 - c), out_half(sid[j], 1 - c), ssem.at[12 + j], rsem.at[12 + j], sib).wait_recv()
            _remote(gcv_ref.at[sid[j]], gcv_ref.at[sid[j]], ssem.at[6 + j], rsem.at[6 + j], sib).wait_recv()
        for cp in sends:
            cp.wait_send()

    return _pcall(
        body, name="gather_layer0",
        in_specs=[HBM_SPEC] * 3, out_specs=[HBM_SPEC] * 3,
        out_shape=[jax.ShapeDtypeStruct((4,) + win.shape[1:], win.dtype),
                   jax.ShapeDtypeStruct((4,) + wout.shape[1:], wout.dtype),
                   jax.ShapeDtypeStruct((4,) + conv.shape, conv.dtype)],
        scratch_shapes=[pltpu.SemaphoreType.DMA((15,)), pltpu.SemaphoreType.DMA((15,))],
    )(win, wout, conv)


def _swap_halves(arrs, axes, name):
    n = len(arrs)

    def half_shape(a, ax):
        return a.shape[:ax] + (a.shape[ax] // 2,) + a.shape[ax + 1:]

    def body(*refs):
        src, dst, ssem, rsem = refs[:n], refs[n:2 * n], refs[2 * n], refs[2 * n + 1]
        x, y, c, s, chips, sid, sib = _place()
        cps = []
        for k in range(n):
            hl = src[k].shape[axes[k]] // 2
            idx = [slice(None)] * len(src[k].shape)
            idx[axes[k]] = pl.ds((1 - c) * hl, hl)
            cps.append(_remote(src[k].at[tuple(idx)], dst[k], ssem.at[k], rsem.at[k], sib))
        for cp in cps:
            cp.start()
        for cp in cps:
            cp.wait()

    return _pcall(
        body, name=name, in_specs=[HBM_SPEC] * n, out_specs=[HBM_SPEC] * n,
        out_shape=[jax.ShapeDtypeStruct(half_shape(a, ax), a.dtype) for a, ax in zip(arrs, axes)],
        scratch_shapes=[pltpu.SemaphoreType.DMA((n,)), pltpu.SemaphoreType.DMA((n,))],
    )(*arrs)


def _chips_side(arrs):
    n = len(arrs)

    def copies(ins, outs, ssem, rsem):
        x, y, c, s, chips, sid, sib = _place()
        cps = [_remote(ins[k].at[sid[j]], outs[k].at[s], ssem.at[3 * k + j], rsem.at[3 * k + j], (*chip, c))
               for k in range(n) for j, chip in enumerate(chips)]
        return cps, sid, sib

    def start(ins, outs, ssem, rsem):
        for cp in copies(ins, outs, ssem, rsem)[0]:
            cp.start()

    def finish(ins, outs, ssem, rsem):
        cps, sid, sib = copies(ins, outs, ssem, rsem)
        for k in range(n):
            for j in range(3):
                got = outs[k].at[sid[j]]
                _remote(got, got, ssem.at[3 * k + j], rsem.at[3 * k + j], sib).wait_recv()
        for cp in cps:
            cp.wait_send()

    return _Side(list(arrs), [jax.ShapeDtypeStruct(a.shape, a.dtype) for a in arrs], 3 * n, start, finish)


def _scatter_chips(arrs, per_target, name):
    n = len(arrs)

    def body(*refs):
        src, dst = refs[:n], refs[n:2 * n]
        ssem, rsem = refs[2 * n], refs[2 * n + 1]
        x, y, c, s, chips, sid, sib = _place()
        sends = []
        for k in range(n):
            for j, chip in enumerate(chips):
                piece = src[k].at[sid[j]] if per_target[k] else src[k]
                sends.append(_remote(piece, dst[k].at[s], ssem.at[3 * k + j], rsem.at[3 * k + j], (*chip, c)))
        for cp in sends:
            cp.start()
        for k in range(n):
            for j in range(3):
                _remote(dst[k].at[sid[j]], dst[k].at[sid[j]], ssem.at[3 * k + j], rsem.at[3 * k + j], sib).wait_recv()
        for cp in sends:
            cp.wait_send()

    outs = [jax.ShapeDtypeStruct(a.shape if pt else (4,) + a.shape, a.dtype) for a, pt in zip(arrs, per_target)]
    return _pcall(
        body, name=name, in_specs=[HBM_SPEC] * n, out_specs=[HBM_SPEC] * n, out_shape=outs,
        scratch_shapes=[pltpu.SemaphoreType.DMA((3 * n,)), pltpu.SemaphoreType.DMA((3 * n,))],
    )(*arrs)


def _swap_whole(arrs, name):
    n = len(arrs)

    def body(*refs):
        src, dst, ssem, rsem = refs[:n], refs[n:2 * n], refs[2 * n], refs[2 * n + 1]
        *_, sib = _place()
        cps = [_remote(src[k], dst[k], ssem.at[k], rsem.at[k], sib) for k in range(n)]
        for cp in cps:
            cp.start()
        for cp in cps:
            cp.wait()

    return _pcall(
        body, name=name, in_specs=[HBM_SPEC] * n, out_specs=[HBM_SPEC] * n,
        out_shape=[jax.ShapeDtypeStruct(a.shape, a.dtype) for a in arrs],
        scratch_shapes=[pltpu.SemaphoreType.DMA((n,)), pltpu.SemaphoreType.DMA((n,))],
    )(*arrs)


def _perm_cols(w):
    parts = [w[..., int(_ORIG_OFF[oi]):int(_ORIG_OFF[oi]) + IN_SIZES[oi]] for oi, _ in _PIECES]
    parts.append(jnp.zeros(w.shape[:-1] + (NP - N_IN,), w.dtype))
    return jnp.concatenate(parts, -1)


def _perm_rows(w):
    return jnp.concatenate([w[..., 512:1536, :], w[..., 0:512, :], w[..., 1536:2048, :]], -2)


_SMALL = ("sinks", "r_conv_b", "r_wa", "r_ba", "r_wx", "r_bx", "r_lam", "g_a_log", "g_dt_bias", "g_norm_w",
          "ln_g", "ln_b", "r_conv_w", "g_conv_w")
_PACK_ROWS = 16


def _piece_rows(n):
    return -(-n // (128 * _PACK_ROWS)) * _PACK_ROWS


def _pack(arrs):
    parts = []
    for a in arrs:
        n = int(np.prod(a.shape))
        rows = _piece_rows(n)
        if n % 128 == 0:
            blk = a.reshape(n // 128, 128)
        else:
            blk = jnp.pad(a.reshape(1, n), ((0, 0), (0, (-n) % 128))).reshape(-1, 128)
        if blk.shape[0] < rows:
            blk = jnp.pad(blk, ((0, rows - blk.shape[0]), (0, 0)))
        parts.append(blk)
    return jnp.concatenate(parts, 0)


def _unpack(packed, shapes):
    out = []
    r = 0
    for shp in shapes:
        n = int(np.prod(shp))
        if n % 128 == 0:
            out.append(packed[r:r + n // 128].reshape(shp))
        else:
            nr = -(-n // 128)
            out.append(packed[r:r + nr].reshape(1, nr * 128)[:, :n].reshape(shp))
        r += _piece_rows(n)
    return out


def _tile(n, t):
    return min(n, t)


def _layer_fwd(l, x, xb, wb, wob, rope_c, rope_s, p, side=None):
    S_ = x.shape[0]
    proj = _matmul(xb, wb, ta=False, tb=False, tm=_tile(S_, 1024), tn=NP // 4, tk=wb.shape[0], out_dtype=F32,
                   name=f"in_proj_{l}", side=side)
    side_out = None
    if side:
        proj, side_out = proj
    h, ymix = _rglru_fwd(proj, p["r_conv_w"], p["r_conv_b"], p["r_wa"], p["r_ba"], p["r_wx"], p["r_bx"], p["r_lam"],
                         T=_tile(S_, 256), name=f"rglru_fwd_{l}")
    ymix = _attn_fwd(proj, rope_c, rope_s, p["sinks"], ymix, T=_tile(S_, 512), name=f"attn_fwd_{l}")
    ymix, st, tms = _gdn_fwd(proj, p["g_conv_w"], p["g_a_log"], p["g_dt_bias"], p["g_norm_w"], ymix,
                             T=_tile(S_, 256), name=f"gdn_fwd_{l}")
    z = _outproj(ymix, wob, x, tm=_tile(S_, 256), name=f"out_proj_{l}")
    return dict(proj=proj, h=h, st=st, tms=tms, ymix=ymix, z=z, side=side_out)


def _layer_bwd(l, sv, x_b, dz, dzb, wb, wob, rope_c, rope_s, p, side=None):
    S_, D = dz.shape
    proj = sv["proj"]
    dymix = _matmul(dzb, wob, ta=False, tb=True, tm=_tile(S_, 1024), tn=512, tk=D, out_dtype=F32,
                    name=f"dmix_{l}")
    dwo = _matmul(sv["ymix"], dzb, ta=True, tb=False, tm=512, tn=_tile(D, 2048), tk=_tile(S_, 1024),
                  out_dtype=F32, name=f"dw_out_{l}",
                  out_blocks=((MIX_WIDTH, D), (512, _tile(D, 2048)),
                              lambda i, j: (jnp.where(i == 3, 3, (i + 1) % 3), j)))
    dproj, dk, dv, dkt, dvt, dsk = _attn_bwd(proj, rope_c, rope_s, p["sinks"], dymix, T=_tile(S_, 512),
                                             name=f"attn_bwd_{l}")
    (dproj, dcw_r, dcb_r, dwa, dba, dwx, dbx, dlam) = _rglru_bwd(
        proj, sv["h"], dymix, dproj, p["r_conv_w"], p["r_conv_b"], p["r_wa"], p["r_ba"], p["r_wx"], p["r_bx"],
        p["r_lam"], T=_tile(S_, 256), name=f"rglru_bwd_{l}")
    dproj, dgba, dcw_g, dpv, dnw = _gdn_bwd(proj, sv["st"], sv["tms"], dymix, dproj, p["g_conv_w"], p["g_a_log"],
                                            p["g_dt_bias"], p["g_norm_w"], T=_tile(S_, 256), name=f"gdn_bwd_{l}")
    tail = jnp.concatenate([dk[128:], dkt, dv[128:], dvt], 0).reshape(2, S_, 128)
    tail = jnp.concatenate([tail[0], tail[1], dgba, jnp.zeros((S_, NP - OFF_GBA - 128), BF16)], 1)
    dproj = lax.dynamic_update_slice(dproj, tail, (0, OFF_AK))
    dx = _matmul(dproj, wb, ta=False, tb=True, tm=_tile(S_, 1024), tn=_tile(D, 1024), tk=NP // 2, out_dtype=F32,
                 name=f"dx_{l}", extra=dz, alpha=DEEPNORM_ALPHA)
    dwin = _matmul(x_b, dproj, ta=True, tb=False, tm=_tile(D, 1024), tn=NP // 4, tk=_tile(S_, 1024), out_dtype=F32,
                   name=f"dw_in_{l}", side=side)
    side_out = None
    if side:
        dwin, side_out = dwin
    small = dict(sinks=dsk[:, 0], r_conv_b=dcb_r[0], r_wa=dwa, r_ba=dba[0], r_wx=dwx, r_bx=dbx[0], r_lam=dlam[0],
                 g_a_log=dpv[0, 4:8], g_dt_bias=dpv[1, 4:8], g_norm_w=dnw[0], r_conv_w=dcw_r, g_conv_w=dcw_g)
    return dx, dwin, dwo, small, side_out


def kernel(x, w_in, sinks, r_conv_w, r_conv_b, r_wa, r_ba, r_wx, r_bx, r_lam, g_conv_w, g_a_log, g_dt_bias, g_norm_w, w_out, ln_g, ln_b, loss_target, m_w_in, m_sinks, m_r_conv_w, m_r_conv_b, m_r_wa, m_r_ba, m_r_wx, m_r_bx, m_r_lam, m_g_conv_w, m_g_a_log, m_g_dt_bias, m_g_norm_w, m_w_out, m_ln_g, m_ln_b, v_w_in, v_sinks, v_r_conv_w, v_r_conv_b, v_r_wa, v_r_ba, v_r_wx, v_r_bx, v_r_lam, v_g_conv_w, v_g_a_log, v_g_dt_bias, v_g_norm_w, v_w_out, v_ln_g, v_ln_b):
    S_, D = x.shape[1], x.shape[2]
    nsh = w_in.shape[2]
    rsh = w_out.shape[1]
    cx, cy, cc = lax.axis_index("x"), lax.axis_index("y"), lax.axis_index("c")
    chip = 2 * cx + cy
    rcw_n, gcw_n = r_conv_w.shape[2], g_conv_w.shape[2]

    conv_pack = jnp.concatenate([r_conv_w, g_conv_w], 2)
    w_in_b, w_out_b = w_in.astype(BF16), w_out.astype(BF16)
    g_in0, g_out0, g_conv = _gather_layer0(w_in_b, w_out_b, conv_pack)

    def shards(own, got):
        return [jnp.where(chip == t, own, got[t]) for t in range(4)]

    def layer_weights(l, g_in, g_out):
        w_full = jnp.concatenate(shards(w_in_b[l], g_in), 1)
        return (_perm_cols(w_full),
                _perm_rows(jnp.concatenate(shards(w_out_b[l], g_out), 0)))

    rcw = jnp.concatenate(shards(r_conv_w, g_conv[:, :, :, :rcw_n]), 2)
    gcw = jnp.concatenate(shards(g_conv_w, g_conv[:, :, :, rcw_n:]), 2)

    pos = jnp.arange(S_, dtype=F32)[:, None]
    inv = 1.0 / (ROPE_THETA ** (jnp.arange(0, A_HEAD_DIM, 2, dtype=F32) / A_HEAD_DIM))
    ang = pos * inv[None, :]
    cos, sin = jnp.cos(ang), jnp.sin(ang)
    rope_c = jnp.concatenate([cos, cos, cos, cos], 1)
    rope_s = jnp.concatenate([-sin, sin, -sin, sin], 1)

    def params(l):
        return dict(sinks=sinks[l], r_conv_w=rcw[l], r_conv_b=r_conv_b[l], r_wa=r_wa[l], r_ba=r_ba[l],
                    r_wx=r_wx[l], r_bx=r_bx[l], r_lam=r_lam[l], g_conv_w=gcw[l], g_a_log=g_a_log[l],
                    g_dt_bias=g_dt_bias[l], g_norm_w=g_norm_w[l])

    xs, xbs, saved = [x[0]], [x[0].astype(BF16)], []
    wb, wob = [None] * DEPTH, [None] * DEPTH
    wb[0], wob[0] = layer_weights(0, g_in0, g_out0)
    for l in range(DEPTH):
        nxt = _gather_side(w_in_b, w_out_b, l + 1) if l + 1 < DEPTH else None
        sv = _layer_fwd(l, xs[l], xbs[l], wb[l], wob[l], rope_c, rope_s, params(l), side=nxt)
        if nxt:
            wb[l + 1], wob[l + 1] = layer_weights(l + 1, *_gather_join(*sv["side"], f"gather_join_{l + 1}"))
        saved.append(sv)
        if l + 1 < DEPTH:
            xn, xnb = _ln_fwd(sv["z"], ln_g[l], ln_b[l], tm=_tile(S_, 256), name=f"ln_fwd_{l}")
            xs.append(xn)
            xbs.append(xnb)

    tm_ln = _tile(S_, 256)
    dz, dzb, dg_l, db_l, loss_part = _ln_bwd(saved[-1]["z"], ln_g[-1], ln_b[-1], loss_target[0], from_target=True,
                                             tm=tm_ln, name=f"ln_bwd_{DEPTH - 1}")
    wcov = (-(-nsh // 128) + 1) * 128

    def own(a):
        return lax.dynamic_index_in_dim(a, chip, 0, keepdims=False)

    def pair_reduce(l, dwin_l, dwo_l, extra):
        dwo4 = dwo_l.reshape(4, rsh, D)
        got = _swap_halves([dwin_l, dwo4] + [e for e in extra], [0, 1] + [0] * len(extra), f"reduce_pair_{l}")
        in_cp = _pair_sum_windows(dwin_l, got[0], nsh, wcov, out_dtype=BF16, name=f"pair_sum_w_in_{l}")
        out_cp = _pair_sum_blocks(dwo4, got[1], out_dtype=BF16, name=f"pair_sum_w_out_{l}")
        ex_cp = [_pair_sum_blocks(e[None], g[None], out_dtype=F32, name=f"pair_sum_small_{l}")[0]
                 for e, g in zip(extra, got[2:])]
        return in_cp, out_cp, ex_cp

    def chip_sums(l, in_cp, out_cp, in_all, out_all):
        return (_sum4(in_all, own(in_cp), tr=_tile(D // 2, 256), name=f"chip_sum_w_in_{l}"),
                _sum4(out_all, own(out_cp), tr=rsh // 2, name=f"chip_sum_w_out_{l}"))

    small = [None] * DEPTH
    dlng, dlnb = [None] * DEPTH, [None] * DEPTH
    sums = [None] * DEPTH
    riding = None
    for l in reversed(range(DEPTH)):
        dlng[l], dlnb[l] = dg_l[0], db_l[0]
        side = _chips_side(list(riding[1:])) if riding else None
        dx, dwin_l, dwo_l, small[l], arrived = _layer_bwd(l, saved[l], xbs[l], dz, dzb, wb[l], wob[l], rope_c, rope_s,
                                                          params(l), side=side)
        if riding:
            sums[riding[0]] = chip_sums(riding[0], riding[1], riding[2], *arrived)
        if l > 0:
            riding = (l,) + pair_reduce(l, dwin_l, dwo_l, [])[:2]
            dz, dzb, dg_l, db_l, _ = _ln_bwd(saved[l - 1]["z"], ln_g[l - 1], ln_b[l - 1], dx, from_target=False,
                                             tm=tm_ln, name=f"ln_bwd_{l - 1}")
    grad_x = dx[None]
    loss = lax.psum(loss_part[0, 0], ("x", "y", "c"))

    sm = {k: jnp.stack([small[l][k] for l in range(DEPTH)]) for k in small[0]}
    sm["ln_g"], sm["ln_b"] = jnp.stack(dlng), jnp.stack(dlnb)
    names = list(_SMALL)
    gs = _pack([sm[n] for n in names])
    in_cp, out_cp, (s_cp,) = pair_reduce(0, dwin_l, dwo_l, [gs])
    in_all, out_all, s_all = _scatter_chips([in_cp, out_cp, s_cp], [True, True, False], "reduce_chips_0")
    sums[0] = chip_sums(0, in_cp, out_cp, in_all, out_all)
    s_sum = _sum4(s_all, s_cp, tr=s_cp.shape[0], name="chip_sum_small")
    mine = [a for l in range(DEPTH) for a in sums[l]] + [s_sum]
    other = _swap_whole(mine, "reduce_join")

    def both(k, axis):
        return jnp.where(cc == 0, jnp.concatenate([mine[k], other[k]], axis),
                         jnp.concatenate([other[k], mine[k]], axis))

    g_w_in = lax.dynamic_slice_in_dim(jnp.stack([both(2 * l, 0) for l in range(DEPTH)]), (nsh * chip) % 128, nsh, 2)
    g_w_out = jnp.stack([both(2 * l + 1, 0) for l in range(DEPTH)])
    g_small = both(2 * DEPTH, 0)

    gsm = dict(zip(names, _unpack(g_small, [sm[n].shape for n in names])))
    gsm["r_conv_w"] = lax.dynamic_slice_in_dim(gsm["r_conv_w"], chip * rcw_n, rcw_n, 2)
    gsm["g_conv_w"] = lax.dynamic_slice_in_dim(gsm["g_conv_w"], chip * gcw_n, gcw_n, 2)
    wts = dict(sinks=sinks, r_conv_w=r_conv_w, r_conv_b=r_conv_b, r_wa=r_wa, r_ba=r_ba, r_wx=r_wx, r_bx=r_bx,
               r_lam=r_lam, g_conv_w=g_conv_w, g_a_log=g_a_log, g_dt_bias=g_dt_bias, g_norm_w=g_norm_w,
               ln_g=ln_g, ln_b=ln_b)
    mom = dict(sinks=m_sinks, r_conv_w=m_r_conv_w, r_conv_b=m_r_conv_b, r_wa=m_r_wa, r_ba=m_r_ba, r_wx=m_r_wx,
               r_bx=m_r_bx, r_lam=m_r_lam, g_conv_w=m_g_conv_w, g_a_log=m_g_a_log, g_dt_bias=m_g_dt_bias,
               g_norm_w=m_g_norm_w, ln_g=m_ln_g, ln_b=m_ln_b)
    vel = dict(sinks=v_sinks, r_conv_w=v_r_conv_w, r_conv_b=v_r_conv_b, r_wa=v_r_wa, r_ba=v_r_ba, r_wx=v_r_wx,
               r_bx=v_r_bx, r_lam=v_r_lam, g_conv_w=v_g_conv_w, g_a_log=v_g_a_log, g_dt_bias=v_g_dt_bias,
               g_norm_w=v_g_norm_w, ln_g=v_ln_g, ln_b=v_ln_b)
    d_s, m_s, v_s = _adamw_many(*[[d[n] for n in names] for d in (wts, gsm, mom, vel)], name="adamw_small")
    d_sm, m_sm, v_sm = (dict(zip(names, a)) for a in (d_s, m_s, v_s))
    def cols(a):
        return jnp.transpose(a, (2, 0, 1))

    g_w_in_t = cols(g_w_in)
    outs_t = _adamw_cols(cols(w_in), g_w_in_t, cols(m_w_in), cols(v_w_in), name="adamw_w_in")
    d_in, m_in, v_in = (jnp.transpose(a, (1, 2, 0)) for a in outs_t)
    g_w_in = jnp.transpose(g_w_in_t, (1, 2, 0))
    d_out, m_out, v_out = _adamw(w_out, g_w_out, m_w_out, v_w_out, tr=256, name="adamw_w_out")

    order = ["w_in", "sinks", "r_conv_w", "r_conv_b", "r_wa", "r_ba", "r_wx", "r_bx", "r_lam", "g_conv_w",
             "g_a_log", "g_dt_bias", "g_norm_w", "w_out", "ln_g", "ln_b"]
    grads = dict(gsm, w_in=g_w_in, w_out=g_w_out)
    deltas = dict(d_sm, w_in=d_in, w_out=d_out)
    new_m = dict(m_sm, w_in=m_in, w_out=m_out)
    new_v = dict(v_sm, w_in=v_in, w_out=v_out)
    return (loss, grad_x, *[grads[n] for n in order], *[deltas[n] for n in order],
            *[new_m[n] for n in order], *[new_v[n] for n in order])
```

```python
import functools
import math

import jax
import jax.numpy as jnp
import numpy as np
from jax import lax
from jax.experimental import pallas as pl
from jax.experimental.pallas import tpu as pltpu

F32 = jnp.float32
BF16 = jnp.bfloat16
MESH = pl.DeviceIdType.MESH

DEPTH = 2
A_HEADS, A_KV_HEADS, A_HEAD_DIM = 8, 2, 64
A_WIDTH, A_KV_WIDTH = 512, 128
WINDOW = 128
ROPE_THETA = 10000.0
R_WIDTH, R_BLOCKS, R_BLOCK_DIM, R_C = 1024, 8, 128, 8.0
CONV_WIDTH = 4
G_HEADS, G_HEAD_DIM, G_WIDTH, G_CHUNK = 4, 128, 512, 64
MIX_WIDTH = 2048
IN_SIZES = (512, 128, 128, 512, 1024, 1024, 512, 512, 512, 512, 4, 4)
N_IN = 5384
DEEPNORM_ALPHA = (2 * DEPTH) ** 0.25
LN_EPS = 1e-5
RMS_EPS = 1e-6
ADAM_LR, ADAM_B1, ADAM_B2, ADAM_EPS, ADAM_WD, ADAM_STEP = 0.001, 0.9, 0.999, 1e-08, 0.01, 10

NP = 5632
OFF_GQKV, OFF_GZ, OFF_RX, OFF_RZ, OFF_AQ, OFF_AZ, OFF_AK, OFF_AV, OFF_GBA = (
    0, 1536, 2048, 3072, 4096, 4608, 5120, 5248, 5376)
_ORIG_OFF = np.concatenate([[0], np.cumsum(IN_SIZES)])[:-1]
_PIECES = ((6, OFF_GQKV), (7, OFF_GQKV + 512), (8, OFF_GQKV + 1024), (9, OFF_GZ), (4, OFF_RX), (5, OFF_RZ),
           (0, OFF_AQ), (3, OFF_AZ), (1, OFF_AK), (2, OFF_AV), (10, OFF_GBA), (11, OFF_GBA + 4))


def _orig_block_to_perm():
    table = list(range(NP // 128))
    for oi, off in _PIECES:
        if IN_SIZES[oi] % 128 == 0:
            for k in range(IN_SIZES[oi] // 128):
                table[int(_ORIG_OFF[oi]) // 128 + k] = off // 128 + k
    return table
MIX_R, MIX_A, MIX_G = 0, 1024, 1536
VMEM_LIMIT = 56 * 1024 * 1024
ANY_SPEC = pl.BlockSpec(memory_space=pl.ANY)


def _pcall(body, **kw):
    return pl.pallas_call(body, **kw)


def _cp(sem, limit=VMEM_LIMIT):
    return pltpu.CompilerParams(dimension_semantics=sem, vmem_limit_bytes=limit)


def _sigmoid(x):
    return 0.5 + 0.5 * jnp.tanh(0.5 * x)


def _silu(x):
    return x * _sigmoid(x)


def _dsilu(x):
    s = _sigmoid(x)
    return s * (1.0 + x * (1.0 - s))


def _log1p(x):
    u = 1.0 + x
    d = jnp.where(u == 1.0, 1.0, u - 1.0)
    return jnp.where(u == 1.0, x, jnp.log(u) * (x / d))


def _softplus(x):
    return jnp.maximum(x, 0.0) + _log1p(jnp.exp(-jnp.abs(x)))


def _one_minus_exp(x):
    series = -x * (1.0 + x * (0.5 + x * (1.0 / 6.0 + x * (1.0 / 24.0))))
    return jnp.where(x > -0.05, series, 1.0 - jnp.exp(x))


def _nn(a, b):
    return lax.dot_general(a, b, (((1,), (0,)), ((), ())), preferred_element_type=F32)


def _nt(a, b):
    return lax.dot_general(a, b, (((1,), (1,)), ((), ())), preferred_element_type=F32)


def _tn(a, b):
    return lax.dot_general(a, b, (((0,), (0,)), ((), ())), preferred_element_type=F32)


def _b(x):
    return x.astype(BF16)


def _split3(x):
    hi = x.astype(BF16)
    r1 = x - hi.astype(F32)
    mid = r1.astype(BF16)
    lo = (r1 - mid.astype(F32)).astype(BF16)
    return hi, mid, lo


def _dot3(f, a, b):
    ah, am, _ = _split3(a)
    bh, bm, _ = _split3(b)
    return f(ah, bh) + (f(ah, bm) + f(am, bh))


def _dot_exact_lhs(f, a_bf16, b):
    bh, bm, bl = _split3(b)
    return f(a_bf16, bh) + (f(a_bf16, bm) + f(a_bf16, bl))


def _rot(x):
    w = x.shape[-1]
    lane = lax.broadcasted_iota(jnp.int32, (1, w), 1)
    return jnp.where((lane & 63) < 32, pltpu.roll(x, w - 32, 1), pltpu.roll(x, 32, 1))


def _conv_taps(ext, n):
    return [pltpu.roll(ext, 3 - k, 0)[8:8 + n] if k < 3 else ext[8:8 + n] for k in range(CONV_WIDTH)]


def _conv_taps_t(ext, n):
    m = ext.shape[0]
    return [pltpu.roll(ext, m - (3 - k), 0)[0:n] if k < 3 else ext[0:n] for k in range(CONV_WIDTH)]


def _scan_steps(a, b, pos, span, shifts, reverse):
    n = a.shape[0]
    for s in shifts:
        if reverse:
            a_sh = pltpu.roll(a, n - s, 0)
            b_sh = pltpu.roll(b, n - s, 0)
            ok = pos < (span - s)
        else:
            a_sh = pltpu.roll(a, s, 0)
            b_sh = pltpu.roll(b, s, 0)
            ok = pos >= s
        b = jnp.where(ok, a * b_sh + b, b)
        a = jnp.where(ok, a * a_sh, a)
    return a, b


def _scan_lin(a, b, reverse):
    n = a.shape[0]
    shifts = []
    s = 1
    while s < n:
        shifts.append(s)
        s *= 2
    return _scan_steps(a, b, lax.broadcasted_iota(jnp.int32, (n, 1), 0), n, shifts, reverse)


class _Side:
    def __init__(self, inputs, out_shapes, n_sems, start, finish):
        self.inputs, self.out_shapes, self.n_sems, self.start, self.finish = inputs, out_shapes, n_sems, start, finish


def _matmul(a, b, *, ta, tb, tm, tn, tk, out_dtype, name, extra=None, alpha=0.0, out_blocks=None, side=None,
            rows=None, into=None):
    if ta:
        K, M = a.shape
    else:
        M, K = a.shape
    if tb:
        N, K2 = b.shape
    else:
        K2, N = b.shape
    assert K == K2 and M % tm == 0 and N % tn == 0 and K % tk == 0, (a.shape, b.shape, tm, tn, tk)
    nk = K // tk
    ca = 0 if ta else 1
    cb = 1 if tb else 0
    has_extra = extra is not None

    assert nk == 1 or out_dtype == F32
    assert rows is None or (not ta and out_blocks is None)
    r0, nrow = rows if rows else (0, M // tm)
    n_in = 2 + int(has_extra) + int(into is not None)
    ns_in = len(side.inputs) if side else 0
    ns_out = len(side.out_shapes) if side else 0
    grid = (nrow, N // tn, nk)

    def body(*refs):
        a_ref, b_ref = refs[0], refs[1]
        e_ref = refs[2] if has_extra else None
        o_ref = refs[n_in + ns_in]
        k = pl.program_id(2)
        if side:
            s_in = refs[n_in:n_in + ns_in]
            s_out = refs[n_in + ns_in + 1:n_in + ns_in + 1 + ns_out]
            ssem, rsem = refs[-2], refs[-1]
            i, j = pl.program_id(0), pl.program_id(1)

            @pl.when((i == 0) & (j == 0) & (k == 0))
            def _():
                side.start(s_in, s_out, ssem, rsem)

            @pl.when((i == grid[0] - 1) & (j == grid[1] - 1) & (k == grid[2] - 1))
            def _():
                side.finish(s_in, s_out, ssem, rsem)

        part = lax.dot_general(a_ref[...], b_ref[...], (((ca,), (cb,)), ((), ())), preferred_element_type=F32)
        if nk == 1:
            if e_ref is not None:
                part = part + alpha * e_ref[...]
            o_ref[...] = part.astype(o_ref.dtype)
            return

        @pl.when(k == 0)
        def _():
            o_ref[...] = part

        @pl.when((k > 0) & (k < nk - 1))
        def _():
            o_ref[...] += part

        @pl.when(k == nk - 1)
        def _():
            last = o_ref[...] + part
            if e_ref is not None:
                last = last + alpha * e_ref[...]
            o_ref[...] = last

    a_spec = (pl.BlockSpec((tk, tm), lambda i, j, k: (k, i)) if ta
              else pl.BlockSpec((tm, tk), lambda i, j, k: (i + r0, k)))
    b_spec = (pl.BlockSpec((tn, tk), lambda i, j, k: (j, k)) if tb
              else pl.BlockSpec((tk, tn), lambda i, j, k: (k, j)))
    e_spec = pl.BlockSpec((tm, tn), lambda i, j, k: (i + r0, j))
    if out_blocks is None:
        o_spec, o_shape = e_spec, (M, N)
    else:
        o_shape, o_block, o_map = out_blocks
        o_spec = pl.BlockSpec(o_block, lambda i, j, k: o_map(i, j))
    in_specs = [a_spec, b_spec] + ([e_spec] if has_extra else []) + ([ANY_SPEC] if into is not None else [])
    args = (a, b) + ((extra,) if has_extra else ()) + ((into,) if into is not None else ())
    alias = {n_in - 1: 0} if into is not None else {}
    if not side:
        return _pcall(
            body, name=name, grid=grid, in_specs=in_specs, out_specs=o_spec,
            out_shape=jax.ShapeDtypeStruct(o_shape, out_dtype), input_output_aliases=alias,
            compiler_params=_cp(("parallel", "parallel", "arbitrary")),
        )(*args)
    outs = _pcall(
        body, name=name, grid=grid, in_specs=in_specs + [HBM_SPEC] * ns_in,
        out_specs=[o_spec] + [HBM_SPEC] * ns_out,
        out_shape=[jax.ShapeDtypeStruct(o_shape, out_dtype)] + list(side.out_shapes), input_output_aliases=alias,
        scratch_shapes=[pltpu.SemaphoreType.DMA((side.n_sems,)), pltpu.SemaphoreType.DMA((side.n_sems,))],
        compiler_params=_cp(("arbitrary", "arbitrary", "arbitrary")),
    )(*args, *side.inputs)
    return outs[0], outs[1:]


def _outproj(ymix, wo, x, *, tm, name):
    S_, D = x.shape

    def body(y_ref, w_ref, x_ref, z_ref):
        z_ref[...] = DEEPNORM_ALPHA * x_ref[...] + _nn(y_ref[...], w_ref[...])

    return _pcall(
        body, name=name, grid=(S_ // tm,),
        in_specs=[pl.BlockSpec((tm, MIX_WIDTH), lambda i: (i, 0)),
                  pl.BlockSpec((MIX_WIDTH, D), lambda i: (0, 0)),
                  pl.BlockSpec((tm, D), lambda i: (i, 0))],
        out_specs=pl.BlockSpec((tm, D), lambda i: (i, 0)),
        out_shape=jax.ShapeDtypeStruct((S_, D), F32),
        compiler_params=_cp(("parallel",)),
    )(ymix, wo, x)


def _ln_stats(z):
    mu = jnp.mean(z, -1, keepdims=True)
    zc = z - mu
    var = jnp.mean(zc * zc, -1, keepdims=True)
    rstd = lax.rsqrt(var + LN_EPS)
    return zc * rstd, rstd


def _ln_fwd(z, g, b, *, tm, name):
    S_, D = z.shape

    def body(z_ref, g_ref, b_ref, y_ref, yb_ref):
        xh, _ = _ln_stats(z_ref[...])
        y = xh * g_ref[...] + b_ref[...]
        y_ref[...] = y
        yb_ref[...] = y.astype(BF16)

    row = pl.BlockSpec((tm, D), lambda i: (i, 0))
    vec = pl.BlockSpec((1, D), lambda i: (0, 0))
    return _pcall(
        body, name=name, grid=(S_ // tm,), in_specs=[row, vec, vec], out_specs=[row, row],
        out_shape=[jax.ShapeDtypeStruct((S_, D), F32), jax.ShapeDtypeStruct((S_, D), BF16)],
        compiler_params=_cp(("parallel",)),
    )(z, g.reshape(1, D), b.reshape(1, D))


def _ln_bwd(z, g, b, other, *, from_target, tm, name):
    S_, D = z.shape

    def body(z_ref, g_ref, b_ref, o_ref, dz_ref, dzb_ref, dg_ref, db_ref, loss_ref):
        i = pl.program_id(0)

        @pl.when(i == 0)
        def _():
            dg_ref[...] = jnp.zeros_like(dg_ref)
            db_ref[...] = jnp.zeros_like(db_ref)
            loss_ref[...] = jnp.zeros_like(loss_ref)

        xh, rstd = _ln_stats(z_ref[...])
        gam = g_ref[...]
        if from_target:
            err = xh * gam + b_ref[...] - o_ref[...]
            per_tok = jnp.mean(err * err, -1, keepdims=True)
            loss_ref[...] += 0.5 * jnp.sum(per_tok, 0, keepdims=True)
            dy = err * (1.0 / D)
        else:
            dy = o_ref[...]
        dxh = dy * gam
        m1 = jnp.mean(dxh, -1, keepdims=True)
        m2 = jnp.mean(dxh * xh, -1, keepdims=True)
        dz = rstd * (dxh - m1 - xh * m2)
        dz_ref[...] = dz
        dzb_ref[...] = dz.astype(BF16)
        dg_ref[...] += jnp.sum(dy * xh, 0, keepdims=True)
        db_ref[...] += jnp.sum(dy, 0, keepdims=True)

    row = pl.BlockSpec((tm, D), lambda i: (i, 0))
    vec = pl.BlockSpec((1, D), lambda i: (0, 0))
    one = pl.BlockSpec((1, 1), lambda i: (0, 0))
    return _pcall(
        body, name=name, grid=(S_ // tm,), in_specs=[row, vec, vec, row],
        out_specs=[row, row, vec, vec, one],
        out_shape=[jax.ShapeDtypeStruct((S_, D), F32), jax.ShapeDtypeStruct((S_, D), BF16),
                   jax.ShapeDtypeStruct((1, D), F32), jax.ShapeDtypeStruct((1, D), F32),
                   jax.ShapeDtypeStruct((1, 1), F32)],
        compiler_params=_cp(("arbitrary",)),
    )(z, g.reshape(1, D), b.reshape(1, D), other)


def _attn_masks(i, sk_ref):
    ri = lax.broadcasted_iota(jnp.int32, (512, 256), 0)
    cj = lax.broadcasted_iota(jnp.int32, (512, 256), 1)
    diff = (ri & 127) - cj + 128
    band = (diff >= 0) & (diff < WINDOW)
    bias = jnp.where(band, 0.0, -jnp.inf)
    bias0 = jnp.where(band & ((i > 0) | (cj >= 128)), 0.0, -jnp.inf)
    grp = lax.broadcasted_iota(jnp.int32, (512, 1), 0) >> 7
    skvs = []
    for h in range(A_KV_HEADS):
        skv = jnp.zeros((512, 1), F32)
        for g in range(4):
            skv = jnp.where(grp == g, sk_ref[h * 4 + g], skv)
        skvs.append(skv)
    return bias0, bias, skvs


def _attn_common(masks, b, h, qr, kd, vd):
    lane = lax.broadcasted_iota(jnp.int32, (1, 128), 1)
    lof = (lane < 64).astype(F32)
    hif = 1.0 - lof
    r0 = b * 128
    skv = masks[2][h]
    pairs = [qr[r0:r0 + 128, h * 256 + p * 128:h * 256 + (p + 1) * 128] for p in (0, 1)]
    qs = _b(jnp.concatenate([pairs[0] * lof, pairs[0] * hif, pairs[1] * lof, pairs[1] * hif], 0))
    k2 = kd[h][r0:r0 + 256]
    v2 = vd[h][r0:r0 + 256]
    s = _nt(qs, k2) * (A_HEAD_DIM ** -0.5) + (masks[0] if b == 0 else masks[1])
    m = jnp.maximum(jnp.max(s, 1, keepdims=True), skv)
    p = jnp.exp(s - m)
    esk = jnp.exp(skv - m)
    rz = 1.0 / (jnp.sum(p, 1, keepdims=True) + esk)
    prob = p * rz
    o4 = _nn(_b(prob), v2)
    return lof, hif, qs, k2, v2, prob, esk * rz, o4


def _attn_prep(T, q_ref, k_ref, v_ref, c_ref, s_ref, kprev, vprev):
    C = c_ref[...]
    Sg = s_ref[...]
    C4 = jnp.concatenate([C] * 4, 1)
    S4 = jnp.concatenate([Sg] * 4, 1)
    q = q_ref[...]
    qr = q * C4 + _rot(q) * S4
    k = k_ref[...]
    kr = k * C + _rot(k) * Sg
    v = v_ref[...]
    kext = jnp.concatenate([kprev[...], kr], 0)
    vext = jnp.concatenate([vprev[...], v], 0)
    kprev[...] = kr[T - 128:]
    vprev[...] = v[T - 128:]
    lo = lax.broadcasted_iota(jnp.int32, (1, 128), 1) < 64
    kroll = pltpu.roll(kext, 64, 1)
    vroll = pltpu.roll(vext, 64, 1)
    kd = [_b(jnp.where(lo, kext, kroll)), _b(jnp.where(lo, kroll, kext))]
    vd = [_b(jnp.where(lo, vext, vroll)), _b(jnp.where(lo, vroll, vext))]
    return C, Sg, C4, S4, qr, kd, vd


def _attn_specs(T):
    return [pl.BlockSpec(memory_space=pltpu.SMEM),
            pl.BlockSpec((T, 512), lambda i: (i, OFF_AQ // 512)),
            pl.BlockSpec((T, 512), lambda i: (i, OFF_AZ // 512)),
            pl.BlockSpec((T, 128), lambda i: (i, OFF_AK // 128)),
            pl.BlockSpec((T, 128), lambda i: (i, OFF_AV // 128)),
            pl.BlockSpec((T, 128), lambda i: (i, 0)),
            pl.BlockSpec((T, 128), lambda i: (i, 0))]


def _attn_fwd(proj, rope_c, rope_s, sinks, ymix, *, T, name):
    S_ = proj.shape[0]
    nb = T // 128

    def body(sk_ref, q_ref, z_ref, k_ref, v_ref, c_ref, s_ref, _, y_ref, kprev, vprev):
        i = pl.program_id(0)

        @pl.when(i == 0)
        def _():
            kprev[...] = jnp.zeros_like(kprev)
            vprev[...] = jnp.zeros_like(vprev)

        _, _, _, _, qr, kd, vd = _attn_prep(T, q_ref, k_ref, v_ref, c_ref, s_ref, kprev, vprev)
        masks = _attn_masks(i, sk_ref)
        for b in range(nb):
            r0 = b * 128
            for h in range(2):
                lof, hif, _, _, _, _, _, o4 = _attn_common(masks, b, h, qr, kd, vd)
                for p in range(2):
                    cs = slice(h * 256 + p * 128, h * 256 + (p + 1) * 128)
                    o = o4[2 * p * 128:(2 * p + 1) * 128] * lof + o4[(2 * p + 1) * 128:(2 * p + 2) * 128] * hif
                    y_ref[r0:r0 + 128, cs] = (o * _silu(z_ref[r0:r0 + 128, cs])).astype(BF16)

    return _pcall(
        body, name=name, grid=(S_ // T,), in_specs=_attn_specs(T) + [ANY_SPEC],
        out_specs=pl.BlockSpec((T, 512), lambda i: (i, MIX_A // 512)),
        out_shape=jax.ShapeDtypeStruct(ymix.shape, BF16),
        input_output_aliases={7: 0},
        scratch_shapes=[pltpu.VMEM((128, 128), F32), pltpu.VMEM((128, 128), F32)],
        compiler_params=_cp(("arbitrary",)),
    )(sinks, proj, proj, proj, proj, rope_c, rope_s, ymix)


def _attn_bwd(proj, rope_c, rope_s, sinks, dymix, *, T, name):
    S_ = proj.shape[0]
    nb = T // 128
    nt = S_ // T

    def body(sk_ref, q_ref, z_ref, k_ref, v_ref, c_ref, s_ref, dy_ref,
             dp_ref, dk_ref, dv_ref, dkt_ref, dvt_ref, dsk_ref,
             kprev, vprev, cprev, sprev, dkacc, dvacc, dqacc):
        i = pl.program_id(0)

        @pl.when(i == 0)
        def _():
            kprev[...] = jnp.zeros_like(kprev)
            vprev[...] = jnp.zeros_like(vprev)
            cprev[...] = jnp.zeros_like(cprev)
            sprev[...] = jnp.zeros_like(sprev)
            dkacc[...] = jnp.zeros_like(dkacc)
            dvacc[...] = jnp.zeros_like(dvacc)
            dsk_ref[...] = jnp.zeros_like(dsk_ref)

        @pl.when(i > 0)
        def _():
            dkacc[0:128, :] = dkacc[T:T + 128, :]
            dvacc[0:128, :] = dvacc[T:T + 128, :]
            dkacc[128:, :] = jnp.zeros((T, 128), F32)
            dvacc[128:, :] = jnp.zeros((T, 128), F32)

        C, Sg, C4, S4, qr, kd, vd = _attn_prep(T, q_ref, k_ref, v_ref, c_ref, s_ref, kprev, vprev)
        masks = _attn_masks(i, sk_ref)
        lane = lax.broadcasted_iota(jnp.int32, (1, 128), 1)
        for b in range(nb):
            r0 = b * 128
            for h in range(2):
                lof, hif, qs, k2, v2, prob, psink, o4 = _attn_common(masks, b, h, qr, kd, vd)
                dos = []
                for p in range(2):
                    cs = slice(h * 256 + p * 128, h * 256 + (p + 1) * 128)
                    o = o4[2 * p * 128:(2 * p + 1) * 128] * lof + o4[(2 * p + 1) * 128:(2 * p + 2) * 128] * hif
                    zc = z_ref[r0:r0 + 128, cs]
                    dyc = dy_ref[r0:r0 + 128, cs]
                    dp_ref[r0:r0 + 128, 512 + cs.start:512 + cs.stop] = (dyc * o * _dsilu(zc)).astype(BF16)
                    do = dyc * _silu(zc)
                    dos += [do * lof, do * hif]
                dos = jnp.concatenate(dos, 0)
                os_ = jnp.concatenate([o4[0:128] * lof, o4[128:256] * hif, o4[256:384] * lof, o4[384:512] * hif], 0)
                delta = jnp.sum(dos * os_, 1, keepdims=True)
                dosb = _b(dos)
                dp = _nt(dosb, v2)
                ds = prob * (dp - delta)
                dsv = -psink * delta
                for g in range(4):
                    sg = jnp.sum(dsv[g * 128:(g + 1) * 128], 0, keepdims=True)
                    hd = h * 4 + g
                    dsk_ref[hd:hd + 1, :] += jnp.broadcast_to(sg, (1, 128))
                dsb = _b(ds * (A_HEAD_DIM ** -0.5))
                dqs = _nn(dsb, k2)
                for p in range(2):
                    cs = slice(h * 256 + p * 128, h * 256 + (p + 1) * 128)
                    dqacc[r0:r0 + 128, cs] = (dqs[2 * p * 128:(2 * p + 1) * 128] * lof
                                              + dqs[(2 * p + 1) * 128:(2 * p + 2) * 128] * hif)
                dkdup = _tn(dsb, qs)
                dvdup = _tn(_b(prob), dosb)
                half = (lane < 64) if h == 0 else (lane >= 64)
                dkacc[r0:r0 + 256, :] += jnp.where(half, dkdup + pltpu.roll(dkdup, 64, 1), 0.0)
                dvacc[r0:r0 + 256, :] += jnp.where(half, dvdup + pltpu.roll(dvdup, 64, 1), 0.0)
        dqr = dqacc[...]
        dp_ref[:, 0:512] = (dqr * C4 + _rot(dqr * S4)).astype(BF16)
        cext = jnp.concatenate([cprev[...], C], 0)
        sext = jnp.concatenate([sprev[...], Sg], 0)
        dke = dkacc[...]
        dkp = dke * cext + _rot(dke * sext)
        dk_ref[...] = dkp[0:T].astype(BF16)
        dkt_ref[...] = dkp[T:T + 128].astype(BF16)
        dve = dvacc[...]
        dv_ref[...] = dve[0:T].astype(BF16)
        dvt_ref[...] = dve[T:T + 128].astype(BF16)
        cprev[...] = C[T - 128:]
        sprev[...] = Sg[T - 128:]

    nar = pl.BlockSpec((T, 128), lambda i: (i, 0))
    tail = pl.BlockSpec((128, 128), lambda i: (0, 0))
    return _pcall(
        body, name=name, grid=(nt,),
        in_specs=_attn_specs(T) + [pl.BlockSpec((T, 512), lambda i: (i, MIX_A // 512))],
        out_specs=[pl.BlockSpec((T, 1024), lambda i: (i, OFF_AQ // 1024)), nar, nar, tail, tail,
                   pl.BlockSpec((8, 128), lambda i: (0, 0))],
        out_shape=[jax.ShapeDtypeStruct((S_, NP), BF16),
                   jax.ShapeDtypeStruct((S_, 128), BF16), jax.ShapeDtypeStruct((S_, 128), BF16),
                   jax.ShapeDtypeStruct((128, 128), BF16), jax.ShapeDtypeStruct((128, 128), BF16),
                   jax.ShapeDtypeStruct((8, 128), F32)],
        scratch_shapes=[pltpu.VMEM((128, 128), F32)] * 4
        + [pltpu.VMEM((T + 128, 128), F32), pltpu.VMEM((T + 128, 128), F32), pltpu.VMEM((T, 512), F32)],
        compiler_params=_cp(("arbitrary",)),
    )(sinks, proj, proj, proj, proj, rope_c, rope_s, dymix)


def _rg_gates(xr, wa_ref, ba_ref, wx_ref, bx_ref, lam_ref):
    xb = _b(xr)
    pre_a = jnp.concatenate([_nn(xb[:, n * 128:(n + 1) * 128], wa_ref[n]) for n in range(R_BLOCKS)], 1) + ba_ref[...]
    pre_x = jnp.concatenate([_nn(xb[:, n * 128:(n + 1) * 128], wx_ref[n]) for n in range(R_BLOCKS)], 1) + bx_ref[...]
    r = _sigmoid(pre_a)
    ig = _sigmoid(pre_x)
    sp = _softplus(-lam_ref[...])
    log_a = -R_C * r * sp
    a = jnp.exp(log_a)
    mult = jnp.sqrt(_one_minus_exp(2.0 * log_a))
    return xb, r, ig, sp, a, mult


def _rg_param_specs():
    C = R_WIDTH
    vec = pl.BlockSpec((1, C), lambda i: (0, 0))
    blk = pl.BlockSpec((R_BLOCKS, 128, 128), lambda i: (0, 0, 0))
    return [pl.BlockSpec((CONV_WIDTH, C), lambda i: (0, 0)), vec, blk, vec, blk, vec, vec]


def _rglru_fwd(proj, cw, cb, wa, ba, wx, bx, lam, *, T, name):
    S_ = proj.shape[0]
    C = R_WIDTH

    def body(rx_ref, rz_ref, cw_ref, cb_ref, wa_ref, ba_ref, wx_ref, bx_ref, lam_ref,
             h_ref, y_ref, halo, hcar):
        i = pl.program_id(0)

        @pl.when(i == 0)
        def _():
            halo[...] = jnp.zeros_like(halo)
            hcar[...] = jnp.zeros_like(hcar)

        rx = rx_ref[...]
        ext = jnp.concatenate([halo[...], rx], 0)
        halo[...] = rx[T - 8:]
        taps = _conv_taps(ext, T)
        xr = cb_ref[...] + sum(cw_ref[k:k + 1, :] * taps[k] for k in range(CONV_WIDTH))
        _, _, ig, _, a, mult = _rg_gates(xr, wa_ref, ba_ref, wx_ref, bx_ref, lam_ref)
        u = mult * (ig * xr)
        acum, hloc = _scan_lin(a, u, False)
        h = hloc + acum * hcar[0:1, :]
        hcar[...] = jnp.broadcast_to(h[T - 1:T, :], (8, C))
        h_ref[...] = h
        y_ref[...] = (h * _silu(rz_ref[...])).astype(BF16)

    row = pl.BlockSpec((T, C), lambda i: (i, 0))
    return _pcall(
        body, name=name, grid=(S_ // T,),
        in_specs=[pl.BlockSpec((T, C), lambda i: (i, OFF_RX // C)),
                  pl.BlockSpec((T, C), lambda i: (i, OFF_RZ // C))] + _rg_param_specs(),
        out_specs=[row, pl.BlockSpec((T, C), lambda i: (i, MIX_R // C))],
        out_shape=[jax.ShapeDtypeStruct((S_, C), F32), jax.ShapeDtypeStruct((S_, MIX_WIDTH), BF16)],
        scratch_shapes=[pltpu.VMEM((8, C), F32), pltpu.VMEM((8, C), F32)],
        compiler_params=_cp(("arbitrary",)),
    )(proj, proj, cw, cb.reshape(1, C), _b(wa), ba.reshape(1, C), _b(wx), bx.reshape(1, C), lam.reshape(1, C))


def _rglru_bwd(proj, h, dymix, dproj, cw, cb, wa, ba, wx, bx, lam, *, T, name):
    S_ = proj.shape[0]
    C = R_WIDTH
    nt = S_ // T
    t8 = T // 8

    def body(rx_ref, rxp_ref, rz_ref, h_ref, hp_ref, dy_ref,
             cw_ref, cb_ref, wa_ref, ba_ref, wx_ref, bx_ref, lam_ref, wat_ref, wxt_ref,
             _, dp_ref, dcw_ref, dcb_ref, dwa_ref, dba_ref, dwx_ref, dbx_ref, dlam_ref,
             afirst, gfirst, dhalo):
        i = pl.program_id(0)
        first_tile = (i == nt - 1)

        @pl.when(i == 0)
        def _():
            afirst[...] = jnp.zeros_like(afirst)
            gfirst[...] = jnp.zeros_like(gfirst)
            dhalo[...] = jnp.zeros_like(dhalo)
            for r in (dcw_ref, dcb_ref, dwa_ref, dba_ref, dwx_ref, dbx_ref, dlam_ref):
                r[...] = jnp.zeros_like(r)

        keep = jnp.where(first_tile, 0.0, 1.0)
        rx = rx_ref[...]
        ext = jnp.concatenate([rxp_ref[...] * keep, rx], 0)
        taps = _conv_taps(ext, T)
        xr = cb_ref[...] + sum(cw_ref[k:k + 1, :] * taps[k] for k in range(CONV_WIDTH))
        xb, r, ig, sp, a, mult = _rg_gates(xr, wa_ref, ba_ref, wx_ref, bx_ref, lam_ref)
        hh = h_ref[...]
        rz = rz_ref[...]
        dy = dy_ref[...]
        dp_ref[:, C:2 * C] = (dy * hh * _dsilu(rz)).astype(BF16)
        dh = dy * _silu(rz)
        row = lax.broadcasted_iota(jnp.int32, (T, 1), 0)
        c = jnp.where(row == T - 1, afirst[0:1, :], pltpu.roll(a, T - 1, 0))
        ccum, gloc = _scan_lin(c, dh, True)
        g = gloc + ccum * gfirst[0:1, :]
        afirst[...] = jnp.broadcast_to(a[0:1, :], (8, C))
        gfirst[...] = jnp.broadcast_to(g[0:1, :], (8, C))
        hprev = jnp.where(row == 0, hp_ref[7:8, :] * keep, pltpu.roll(hh, 1, 0))
        da = g * hprev
        gx = ig * xr
        dgx = g * mult
        dmult = g * gx
        dlog_a = da * a - dmult * (a * a) * lax.rsqrt(mult * mult)
        dpre_a = dlog_a * (-R_C * sp) * r * (1.0 - r)
        dpre_x = dgx * xr * ig * (1.0 - ig)
        dlam_ref[...] += jnp.sum(dlog_a * (-R_C * r), 0, keepdims=True) * (-_sigmoid(-lam_ref[...]))
        dab = _b(dpre_a)
        dxb = _b(dpre_x)
        dxr = dgx * ig + jnp.concatenate(
            [_nn(dab[:, n * 128:(n + 1) * 128], wat_ref[n]) + _nn(dxb[:, n * 128:(n + 1) * 128], wxt_ref[n])
             for n in range(R_BLOCKS)], 1)
        for n in range(R_BLOCKS):
            cs = slice(n * 128, (n + 1) * 128)
            dwa_ref[n] += _tn(xb[:, cs], dab[:, cs])
            dwx_ref[n] += _tn(xb[:, cs], dxb[:, cs])
        dba_ref[...] += jnp.sum(dpre_a, 0, keepdims=True)
        dbx_ref[...] += jnp.sum(dpre_x, 0, keepdims=True)
        dcb_ref[...] += jnp.sum(dxr, 0, keepdims=True)
        for k in range(CONV_WIDTH):
            dcw_ref[k:k + 1, :] += jnp.sum(dxr * taps[k], 0, keepdims=True)
        ext2 = jnp.concatenate([dxr, dhalo[...]], 0)
        tt = _conv_taps_t(ext2, T)
        dp_ref[:, 0:C] = sum(cw_ref[k:k + 1, :] * tt[k] for k in range(CONV_WIDTH)).astype(BF16)
        dhalo[...] = dxr[0:8]

    def rev(i):
        return nt - 1 - i

    def prev8(i):
        return jnp.maximum(rev(i) * t8 - 1, 0)

    vec = pl.BlockSpec((1, C), lambda i: (0, 0))
    blk = pl.BlockSpec((R_BLOCKS, 128, 128), lambda i: (0, 0, 0))
    row = pl.BlockSpec((T, C), lambda i: (rev(i), 0))
    wat = _b(jnp.swapaxes(wa, 1, 2))
    wxt = _b(jnp.swapaxes(wx, 1, 2))
    return _pcall(
        body, name=name, grid=(nt,),
        in_specs=[pl.BlockSpec((T, C), lambda i: (rev(i), OFF_RX // C)),
                  pl.BlockSpec((8, C), lambda i: (prev8(i), OFF_RX // C)),
                  pl.BlockSpec((T, C), lambda i: (rev(i), OFF_RZ // C)),
                  row,
                  pl.BlockSpec((8, C), lambda i: (prev8(i), 0)),
                  pl.BlockSpec((T, C), lambda i: (rev(i), MIX_R // C)),
                  ] + _rg_param_specs() + [blk, blk, ANY_SPEC],
        out_specs=[pl.BlockSpec((T, 2 * C), lambda i: (rev(i), OFF_RX // (2 * C))),
                   pl.BlockSpec((CONV_WIDTH, C), lambda i: (0, 0)), vec, blk, vec, blk, vec, vec],
        out_shape=[jax.ShapeDtypeStruct(dproj.shape, BF16),
                   jax.ShapeDtypeStruct((CONV_WIDTH, C), F32), jax.ShapeDtypeStruct((1, C), F32),
                   jax.ShapeDtypeStruct((R_BLOCKS, 128, 128), F32), jax.ShapeDtypeStruct((1, C), F32),
                   jax.ShapeDtypeStruct((R_BLOCKS, 128, 128), F32), jax.ShapeDtypeStruct((1, C), F32),
                   jax.ShapeDtypeStruct((1, C), F32)],
        input_output_aliases={15: 0},
        scratch_shapes=[pltpu.VMEM((8, C), F32)] * 3,
        compiler_params=_cp(("arbitrary",)),
    )(proj, proj, proj, h, h, dymix, cw, cb.reshape(1, C), _b(wa), ba.reshape(1, C), _b(wx), bx.reshape(1, C),
      lam.reshape(1, C), wat, wxt, dproj)


GW3 = 3 * G_WIDTH


def _lane_col(x, lane_idx):
    lane = lax.broadcasted_iota(jnp.int32, (1, x.shape[1]), 1)
    return jnp.sum(jnp.where(lane == lane_idx, x, 0.0), 1, keepdims=True)


def _gdn_pre(ext, T, cw_ref, gba, pv_ref):
    taps = _conv_taps(ext, T)
    c = sum(cw_ref[k:k + 1, :] * taps[k] for k in range(CONV_WIDTH))
    qkv = _silu(c)
    beta = _sigmoid(gba)
    sarg = gba + pv_ref[1:2, :]
    nea = -jnp.exp(pv_ref[0:1, :])
    gdec = nea * _softplus(sarg)
    ri = lax.broadcasted_iota(jnp.int32, (T, T), 0)
    cj = lax.broadcasted_iota(jnp.int32, (T, T), 1)
    same = (ri >> 6) == (cj >> 6)
    ltri = jnp.where((ri >= cj) & same, 1.0, 0.0).astype(BF16)
    gc = _dot_exact_lhs(_nn, ltri, gdec)
    return taps, c, qkv, beta, sarg, nea, gdec, gc


def _gdn_masks():
    ri = lax.broadcasted_iota(jnp.int32, (128, 128), 0)
    cj = lax.broadcasted_iota(jnp.int32, (128, 128), 1)
    same = (ri >> 6) == (cj >> 6)
    return (ri >= cj) & same, (ri > cj) & same, ri == cj


def _lockstep(gens):
    out = [None] * len(gens)
    live = list(range(len(gens)))
    while live:
        still = []
        for k in live:
            try:
                next(gens[k])
                still.append(k)
            except StopIteration as stop:
                out[k] = stop.value
        live = still
    return out


def _gdn_chunk(qkv, beta, gc, rs, h, tm=None):
    tril, strict, eye = _gdn_masks()
    rowi = lax.broadcasted_iota(jnp.int32, (128, 1), 0)
    lane = lax.broadcasted_iota(jnp.int32, (1, 128), 1)
    qh = qkv[rs, h * 128:(h + 1) * 128]
    kh = qkv[rs, 512 + h * 128:512 + (h + 1) * 128]
    vh = qkv[rs, 1024 + h * 128:1024 + (h + 1) * 128]
    rq = lax.rsqrt(jnp.sum(qh * qh, 1, keepdims=True) + RMS_EPS)
    rk = lax.rsqrt(jnp.sum(kh * kh, 1, keepdims=True) + RMS_EPS)
    qn = qh * (rq * (G_HEAD_DIM ** -0.5))
    kn = kh * rk
    gcb = gc[rs]
    gcol = _lane_col(gcb, 4 + h)
    bcol = _lane_col(beta[rs], h)
    grow = _dot_exact_lhs(_nt, jnp.ones((128, 128), BF16), jnp.where(lane == 4 + h, gcb, 0.0))
    D = jnp.where(tril, jnp.exp(jnp.minimum(gcol - grow, 0.0)), 0.0)
    kb = kn * bcol
    vb = vh * bcol
    knb = _b(kn)
    A = _nt(_b(kb), knb)
    Bm = _nt(_b(qn), knb)
    yield
    if tm is None:
        N = jnp.where(strict, -(A * D), 0.0)
        tm = jnp.where(eye, 1.0, 0.0) + N
        npow = N
        for _ in range(5):
            npow = _dot3(_nn, npow, npow)
            yield
            tm = tm + _dot3(_nn, tm, npow)
            yield
    eg = jnp.exp(gcol)
    u = _dot3(_nn, tm, vb)
    w = _dot3(_nn, tm, kb * eg)
    yield
    qk = jnp.where(tril, Bm * D, 0.0)
    qd = qn * eg
    gla = jnp.sum(jnp.where(rowi == 63, gcol, 0.0), 0, keepdims=True)
    glb = jnp.sum(jnp.where(rowi == 127, gcol, 0.0), 0, keepdims=True)
    ed = jnp.exp(jnp.where(rowi < 64, gla, glb) - gcol)
    kd = kn * ed
    return dict(qh=qh, kh=kh, vh=vh, rq=rq, rk=rk, qn=qn, kn=kn, gcol=gcol, bcol=bcol, D=D, A=A, Bm=Bm,
                tm=tm, eg=eg, ed=ed, u=u, w=w, qk=qk, qd=qd, kd=kd, kb=kb, vb=vb,
                gla=jnp.exp(gla), glb=jnp.exp(glb))


def _gdn_scan(q, sa):
    sab = _b(sa)
    wb = _b(q["w"])
    vna = q["u"] - _nn(wb, sab)
    yield
    sb = sa * q["gla"] + _tn(_b(q["kd"][0:64]), _b(vna[0:64]))
    yield
    sbb = _b(sb)
    vnb = q["u"] - _nn(wb, sbb)
    yield
    sn = sb * q["glb"] + _tn(_b(q["kd"][64:128]), _b(vnb[64:128]))
    yield
    vn = jnp.concatenate([vna[0:64], vnb[64:128]], 0)
    qdb = _b(q["qd"])
    o = jnp.concatenate([_nn(qdb[0:64], sab), _nn(qdb[64:128], sbb)], 0) + _nn(_b(q["qk"]), _b(vn))
    return sb, sn, vn, o


def _gdn_param_specs():
    return [pl.BlockSpec((CONV_WIDTH, GW3), lambda i: (0, 0)),
            pl.BlockSpec((8, 128), lambda i: (0, 0)),
            pl.BlockSpec((1, 128), lambda i: (0, 0))]


def _gdn_pvec(a_log, dt_bias):
    z = jnp.zeros((8, 128), F32)
    return z.at[0, 4:8].set(a_log).at[1, 4:8].set(dt_bias)


def _gdn_fwd(proj, cw, a_log, dt_bias, nw, ymix, *, T, name):
    S_ = proj.shape[0]
    nu = T // 128

    def body(x_ref, z_ref, g_ref, cw_ref, pv_ref, nw_ref, _, y_ref, st_ref, tm_ref, halo, state):
        i = pl.program_id(0)

        @pl.when(i == 0)
        def _():
            halo[...] = jnp.zeros_like(halo)
            state[...] = jnp.zeros_like(state)

        x = x_ref[...]
        ext = jnp.concatenate([halo[...], x], 0)
        halo[...] = x[T - 8:]
        _, _, qkv, beta, _, _, _, gc = _gdn_pre(ext, T, cw_ref, g_ref[...], pv_ref)
        items = [(dc, h) for dc in range(nu) for h in range(G_HEADS)]
        qs = _lockstep([_gdn_chunk(qkv, beta, gc, slice(dc * 128, (dc + 1) * 128), h) for dc, h in items])

        def head_chain(h):
            s = state[h]
            for dc in range(nu):
                rs = slice(dc * 128, (dc + 1) * 128)
                q = qs[dc * G_HEADS + h]
                sb, sn, _, o = yield from _gdn_scan(q, s)
                st_ref[2 * dc, h] = s
                st_ref[2 * dc + 1, h] = sb
                tm_ref[dc, h] = q["tm"]
                s = sn
                yield
                rn = lax.rsqrt(jnp.mean(o * o, 1, keepdims=True) + RMS_EPS)
                cs = slice(h * 128, (h + 1) * 128)
                y_ref[rs, cs] = (o * rn * nw_ref[...] * _silu(z_ref[rs, cs])).astype(BF16)
                yield
            state[h] = s

        _lockstep([head_chain(h) for h in range(G_HEADS)])

    return _pcall(
        body, name=name, grid=(S_ // T,),
        in_specs=[pl.BlockSpec((T, GW3), lambda i: (i, OFF_GQKV // GW3)),
                  pl.BlockSpec((T, 512), lambda i: (i, OFF_GZ // 512)),
                  pl.BlockSpec((T, 128), lambda i: (i, OFF_GBA // 128))] + _gdn_param_specs() + [ANY_SPEC],
        out_specs=[pl.BlockSpec((T, 512), lambda i: (i, MIX_G // 512)),
                   pl.BlockSpec((2 * nu, G_HEADS, 128, 128), lambda i: (i, 0, 0, 0)),
                   pl.BlockSpec((nu, G_HEADS, 128, 128), lambda i: (i, 0, 0, 0))],
        out_shape=[jax.ShapeDtypeStruct(ymix.shape, BF16),
                   jax.ShapeDtypeStruct((S_ // 64, G_HEADS, 128, 128), F32),
                   jax.ShapeDtypeStruct((S_ // 128, G_HEADS, 128, 128), F32)],
        input_output_aliases={6: 0},
        scratch_shapes=[pltpu.VMEM((8, GW3), F32), pltpu.VMEM((G_HEADS, 128, 128), F32)],
        compiler_params=_cp(("arbitrary",)),
    )(proj, proj, proj, cw, _gdn_pvec(a_log, dt_bias), nw.reshape(1, 128), ymix)


def _gdn_bwd(proj, states, tms, dymix, dproj, cw, a_log, dt_bias, nw, *, T, name):
    S_ = proj.shape[0]
    nt = S_ // T
    nu = T // 128
    t8 = T // 8

    def body(x_ref, xp_ref, z_ref, g_ref, st_ref, tm_ref, dy_ref, cw_ref, pv_ref, nw_ref, _,
             dp_ref, dg_ref, dcw_ref, dpv_ref, dnw_ref, dstate, dhalo, dqkv, dbg):
        i = pl.program_id(0)
        first_tile = (i == nt - 1)

        @pl.when(i == 0)
        def _():
            dstate[...] = jnp.zeros_like(dstate)
            dhalo[...] = jnp.zeros_like(dhalo)
            dcw_ref[...] = jnp.zeros_like(dcw_ref)
            dpv_ref[...] = jnp.zeros_like(dpv_ref)
            dnw_ref[...] = jnp.zeros_like(dnw_ref)

        keep = jnp.where(first_tile, 0.0, 1.0)
        ext = jnp.concatenate([xp_ref[...] * keep, x_ref[...]], 0)
        G = g_ref[...]
        taps, c, qkv, beta, sarg, nea, gdec, gc = _gdn_pre(ext, T, cw_ref, G, pv_ref)
        tril, strict, _ = _gdn_masks()
        rowi = lax.broadcasted_iota(jnp.int32, (128, 1), 0)
        lane = lax.broadcasted_iota(jnp.int32, (1, 128), 1)
        ones_b = jnp.ones((128, 128), BF16)
        nwv = nw_ref[...]
        items = [(dc, h) for dc in range(nu) for h in range(G_HEADS)]

        def recompute(dc, h):
            q = yield from _gdn_chunk(qkv, beta, gc, slice(dc * 128, (dc + 1) * 128), h, tm=tm_ref[dc, h])
            sa = st_ref[2 * dc, h]
            sb, _, vn, o = yield from _gdn_scan(q, sa)
            return q, sa, sb, vn, o

        fw = _lockstep([recompute(dc, h) for dc, h in items])
        chain_out = {}

        def head_chain(h):
            dS = dstate[h]
            for dc in reversed(range(nu)):
                rs = slice(dc * 128, (dc + 1) * 128)
                q, sa, sb, vn, o = fw[dc * G_HEADS + h]
                cs = slice(h * 128, (h + 1) * 128)
                zg = z_ref[rs, cs]
                dy = dy_ref[rs, cs]
                rn = lax.rsqrt(jnp.mean(o * o, 1, keepdims=True) + RMS_EPS)
                don = dy * _silu(zg)
                dp_ref[rs, GW3 + cs.start:GW3 + cs.stop] = (dy * (o * rn * nwv) * _dsilu(zg)).astype(BF16)
                dnw_ref[...] += jnp.sum(don * o * rn, 0, keepdims=True)
                tt = don * nwv
                do = rn * (tt - o * (rn * rn) * jnp.mean(tt * o, 1, keepdims=True))
                yield
                dob = _b(do)
                sab, sbb = _b(sa), _b(sb)
                vnb16 = _b(vn)
                dqk = jnp.where(tril, _nt(dob, vnb16), 0.0)
                dvn_o = _tn(_b(q["qk"]), dob)
                dSb16 = _b(dS)
                kdb = _b(q["kd"])
                wb = _b(q["w"])
                qdb = _b(q["qd"])
                yield
                dvn_b = dvn_o[64:128] + _nn(kdb[64:128], dSb16)
                dkd_b = _nt(vnb16[64:128], dSb16)
                dgl_b = jnp.sum(jnp.sum(dS * sb, 1, keepdims=True), 0, keepdims=True)
                yield
                dvn_b16 = _b(dvn_b)
                dw_b = -_nt(dvn_b16, sbb)
                dqd_b = _nt(dob[64:128], sbb)
                dSm = q["glb"] * dS + _tn(qdb[64:128], dob[64:128]) - _tn(wb[64:128], dvn_b16)
                yield
                dSm16 = _b(dSm)
                dvn_a = dvn_o[0:64] + _nn(kdb[0:64], dSm16)
                dkd_a = _nt(vnb16[0:64], dSm16)
                dgl_a = jnp.sum(jnp.sum(dSm * sa, 1, keepdims=True), 0, keepdims=True)
                yield
                dvn_a16 = _b(dvn_a)
                dw_a = -_nt(dvn_a16, sab)
                dqd_a = _nt(dob[0:64], sab)
                dS = q["gla"] * dSm + _tn(qdb[0:64], dob[0:64]) - _tn(wb[0:64], dvn_a16)
                chain_out[dc, h] = (dqk, jnp.concatenate([dvn_a, dvn_b], 0), jnp.concatenate([dw_a, dw_b], 0),
                                    jnp.concatenate([dkd_a, dkd_b], 0), jnp.concatenate([dqd_a, dqd_b], 0),
                                    dgl_a, dgl_b)
                yield
            dstate[h] = dS

        _lockstep([head_chain(h) for h in range(G_HEADS)])

        def local(dc, h):
            rs = slice(dc * 128, (dc + 1) * 128)
            q = fw[dc * G_HEADS + h][0]
            dqk, du, dw, dkd, dqd, dgl_a, dgl_b = chain_out[dc, h]
            if True:
                dvb = _dot3(_tn, q["tm"], du)
                dkbe = _dot3(_tn, q["tm"], dw)
                yield
                dM = jnp.where(strict, -(_nt(_b(dvb), _b(q["u"])) + _nt(_b(dkbe), _b(q["w"]))), 0.0)
                yield
                D = q["D"]
                dA = dM * D
                dB = dqk * D
                dDD = (dM * q["A"] + dqk * q["Bm"]) * D
                dh_, dm_, dl_ = _split3(dDD)
                colsum = _tn(dh_, ones_b) + (_tn(dm_, ones_b) + _tn(dl_, ones_b))
                dgc = jnp.sum(dDD, 1, keepdims=True) - _lane_col(colsum, 0)
                yield
                dA16, dB16 = _b(dA), _b(dB)
                knb, kbb, qnb = _b(q["kn"]), _b(q["kb"]), _b(q["qn"])
                eg, ed = q["eg"], q["ed"]
                dkb = _nn(dA16, knb) + dkbe * eg
                dkn = _tn(dA16, kbb) + _tn(dB16, qnb) + dkd * ed + dkb * q["bcol"]
                dqn = _nn(dB16, knb) + dqd * eg
                yield
                deg = jnp.sum(dkbe * q["kb"], 1, keepdims=True) + jnp.sum(dqd * q["qn"], 1, keepdims=True)
                ded = jnp.sum(dkd * q["kn"], 1, keepdims=True) * ed
                dgc = dgc + deg * eg - ded
                tail_a = jnp.sum(jnp.where(rowi < 64, ded, 0.0), 0, keepdims=True) + dgl_a * q["gla"]
                tail_b = jnp.sum(jnp.where(rowi >= 64, ded, 0.0), 0, keepdims=True) + dgl_b * q["glb"]
                dgc = dgc + jnp.where(rowi == 63, tail_a, 0.0) + jnp.where(rowi == 127, tail_b, 0.0)
                dbeta = jnp.sum(dkb * q["kn"], 1, keepdims=True) + jnp.sum(dvb * q["vh"], 1, keepdims=True)
                bcol = q["bcol"]
                blk = jnp.where(lane == h, dbeta * bcol * (1.0 - bcol), 0.0) + jnp.where(lane == 4 + h, dgc, 0.0)
                yield
                sc = G_HEAD_DIM ** -0.5
                rq, rk, qh, kh = q["rq"], q["rk"], q["qh"], q["kh"]
                dqh = sc * (dqn * rq - qh * (rq * rq * rq) * jnp.sum(dqn * qh, 1, keepdims=True))
                dkh = dkn * rk - kh * (rk * rk * rk) * jnp.sum(dkn * kh, 1, keepdims=True)
                dqkv[rs, h * 128:(h + 1) * 128] = dqh
                dqkv[rs, 512 + h * 128:512 + (h + 1) * 128] = dkh
                dqkv[rs, 1024 + h * 128:1024 + (h + 1) * 128] = dvb * bcol
            return blk

        blks = _lockstep([local(dc, h) for dc, h in items])
        for dc in range(nu):
            dbg[dc * 128:(dc + 1) * 128, :] = functools.reduce(
                lambda a, b: a + b, [blks[dc * G_HEADS + h] for h in range(G_HEADS)])
        ri = lax.broadcasted_iota(jnp.int32, (T, T), 0)
        cj = lax.broadcasted_iota(jnp.int32, (T, T), 1)
        utri = jnp.where((ri <= cj) & ((ri >> 6) == (cj >> 6)), 1.0, 0.0).astype(BF16)
        dbgv = dbg[...]
        dgd = _dot_exact_lhs(_nn, utri, dbgv)
        is_g = (lane >= 4) & (lane < 8)
        dga = jnp.where(is_g, dgd * nea * _sigmoid(sarg), 0.0)
        dg_ref[...] = jnp.where(lane < 4, dbgv, dga).astype(BF16)
        dpv_ref[0:1, :] += jnp.sum(jnp.where(is_g, dgd * gdec, 0.0), 0, keepdims=True)
        dpv_ref[1:2, :] += jnp.sum(dga, 0, keepdims=True)
        dc_ = dqkv[...] * _dsilu(c)
        for k in range(CONV_WIDTH):
            dcw_ref[k:k + 1, :] += jnp.sum(dc_ * taps[k], 0, keepdims=True)
        ext2 = jnp.concatenate([dc_, dhalo[...]], 0)
        tt2 = _conv_taps_t(ext2, T)
        dp_ref[:, 0:GW3] = sum(cw_ref[k:k + 1, :] * tt2[k] for k in range(CONV_WIDTH)).astype(BF16)
        dhalo[...] = dc_[0:8]

    def rev(i):
        return nt - 1 - i

    def prev8(i):
        return jnp.maximum(rev(i) * t8 - 1, 0)

    return _pcall(
        body, name=name, grid=(nt,),
        in_specs=[pl.BlockSpec((T, GW3), lambda i: (rev(i), OFF_GQKV // GW3)),
                  pl.BlockSpec((8, GW3), lambda i: (prev8(i), OFF_GQKV // GW3)),
                  pl.BlockSpec((T, 512), lambda i: (rev(i), OFF_GZ // 512)),
                  pl.BlockSpec((T, 128), lambda i: (rev(i), OFF_GBA // 128)),
                  pl.BlockSpec((2 * nu, G_HEADS, 128, 128), lambda i: (rev(i), 0, 0, 0)),
                  pl.BlockSpec((nu, G_HEADS, 128, 128), lambda i: (rev(i), 0, 0, 0)),
                  pl.BlockSpec((T, 512), lambda i: (rev(i), MIX_G // 512))] + _gdn_param_specs() + [ANY_SPEC],
        out_specs=[pl.BlockSpec((T, GW3 + 512), lambda i: (rev(i), OFF_GQKV // (GW3 + 512))),
                   pl.BlockSpec((T, 128), lambda i: (rev(i), 0)),
                   pl.BlockSpec((CONV_WIDTH, GW3), lambda i: (0, 0)),
                   pl.BlockSpec((8, 128), lambda i: (0, 0)),
                   pl.BlockSpec((1, 128), lambda i: (0, 0))],
        out_shape=[jax.ShapeDtypeStruct(dproj.shape, BF16),
                   jax.ShapeDtypeStruct((S_, 128), BF16), jax.ShapeDtypeStruct((CONV_WIDTH, GW3), F32),
                   jax.ShapeDtypeStruct((8, 128), F32), jax.ShapeDtypeStruct((1, 128), F32)],
        input_output_aliases={10: 0},
        scratch_shapes=[pltpu.VMEM((G_HEADS, 128, 128), F32), pltpu.VMEM((8, GW3), F32),
                        pltpu.VMEM((T, GW3), F32), pltpu.VMEM((T, 128), F32)],
        compiler_params=_cp(("arbitrary",)),
    )(proj, proj, proj, proj, states, tms, dymix, cw, _gdn_pvec(a_log, dt_bias), nw.reshape(1, 128), dproj)


def _pair_sum_windows(a, b, nsh, width, *, out_dtype, name):
    R_, C = a.shape
    hr = R_ // 2
    nb = width // 128
    assert (3 * nsh) // 128 + nb <= C // 128
    to_perm = _orig_block_to_perm()
    table = jnp.asarray([to_perm[(nsh * t) // 128 + j] for t in range(4) for j in range(nb)], jnp.int32)

    def body(tab_ref, a0_ref, a1_ref, b_ref, o_ref):
        mine = jnp.where(lax.axis_index("c") == 0, a0_ref[...], a1_ref[...])
        o_ref[...] = (mine + b_ref[...]).astype(o_ref.dtype)

    def spec(half):
        return pl.BlockSpec((hr, 128), lambda t, j, tab: (half, tab[t * nb + j]))

    return _pcall(
        body, name=name,
        grid_spec=pltpu.PrefetchScalarGridSpec(
            num_scalar_prefetch=1, grid=(4, nb), in_specs=[spec(0), spec(1), spec(0)],
            out_specs=pl.BlockSpec((None, hr, 128), lambda t, j, tab: (t, 0, j))),
        out_shape=jax.ShapeDtypeStruct((4, hr, width), out_dtype),
        compiler_params=_cp(("parallel", "parallel")))(table, a, a, b)


def _pair_sum_blocks(a, b, *, out_dtype, name):
    L, R_, C = a.shape
    hr = R_ // 2

    def body(a0_ref, a1_ref, b_ref, o_ref):
        mine = jnp.where(lax.axis_index("c") == 0, a0_ref[...], a1_ref[...])
        o_ref[...] = (mine + b_ref[...]).astype(o_ref.dtype)

    def spec(half):
        return pl.BlockSpec((None, hr, C), lambda t: (t, half, 0))

    return _pcall(body, name=name, grid=(L,), in_specs=[spec(0), spec(1), spec(0)], out_specs=spec(0),
                  out_shape=jax.ShapeDtypeStruct((L, hr, C), out_dtype),
                  compiler_params=_cp(("parallel",)))(a, a, b)


def _add_mine(a0, a1, b, *, out_dtype, tr, name):
    R_, C = b.shape

    def body(a0_ref, a1_ref, b_ref, o_ref):
        mine = jnp.where(lax.axis_index("c") == 0, a0_ref[...], a1_ref[...])
        o_ref[...] = (mine + b_ref[...]).astype(o_ref.dtype)

    spec = pl.BlockSpec((tr, C), lambda i: (i, 0))
    return _pcall(body, name=name, grid=(R_ // tr,), in_specs=[spec] * 3, out_specs=spec,
                  out_shape=jax.ShapeDtypeStruct((R_, C), out_dtype), compiler_params=_cp(("parallel",)))(a0, a1, b)


def _sum4(a, mine, *, tr, name):
    _, R_, C = a.shape

    def body(a_ref, m_ref, o_ref):
        s = 2 * lax.axis_index("x") + lax.axis_index("y")
        mv = m_ref[...].astype(F32)
        p = [jnp.where(s == t, mv, a_ref[t].astype(F32)) for t in range(4)]
        o_ref[...] = ((p[0] + p[1]) + p[2]) + p[3]

    return _pcall(body, name=name, grid=(R_ // tr,),
                  in_specs=[pl.BlockSpec((4, tr, C), lambda i: (0, i, 0)), pl.BlockSpec((tr, C), lambda i: (i, 0))],
                  out_specs=pl.BlockSpec((tr, C), lambda i: (i, 0)),
                  out_shape=jax.ShapeDtypeStruct((R_, C), F32), compiler_params=_cp(("parallel",)))(a, mine)


def _adamw_refs(w_ref, g_ref, m_ref, v_ref, d_ref, mo_ref, vo_ref):
    c1 = 1.0 / (1.0 - ADAM_B1 ** ADAM_STEP)
    c2 = 1.0 / (1.0 - ADAM_B2 ** ADAM_STEP)
    gg = g_ref[...]
    mn = ADAM_B1 * m_ref[...] + (1.0 - ADAM_B1) * gg
    vn = ADAM_B2 * v_ref[...] + (1.0 - ADAM_B2) * (gg * gg)
    mo_ref[...] = mn
    vo_ref[...] = vn
    d_ref[...] = -ADAM_LR * ((mn * c1) / (jnp.sqrt(vn * c2) + ADAM_EPS) + ADAM_WD * w_ref[...])


def _adamw_many(ws, gs, ms, vs, *, name):
    n = len(ws)

    def body(*refs):
        for k in range(n):
            _adamw_refs(*[refs[q * n + k] for q in range(7)])

    vm = pl.BlockSpec(memory_space=pltpu.VMEM)
    shp = [jax.ShapeDtypeStruct(w.shape, F32) for w in ws]
    outs = _pcall(body, name=name, in_specs=[vm] * (4 * n), out_specs=[vm] * (3 * n), out_shape=shp * 3,
                  compiler_params=pltpu.CompilerParams(vmem_limit_bytes=VMEM_LIMIT))(*ws, *gs, *ms, *vs)
    return outs[:n], outs[n:2 * n], outs[2 * n:]


def _adamw(w, g, m, v, *, tr, name):
    L, R_, C = w.shape
    body = functools.partial(_adamw_refs)

    spec = pl.BlockSpec((None, tr, C), lambda l, i: (l, i, 0))
    shp = jax.ShapeDtypeStruct((L, R_, C), F32)
    return _pcall(body, name=name, grid=(L, R_ // tr), in_specs=[spec] * 4, out_specs=[spec] * 3,
                  out_shape=[shp] * 3, compiler_params=_cp(("parallel", "parallel")))(w, g, m, v)


def _adamw_cols(w, g, m, v, *, name):
    C, L, R_ = w.shape
    tc = C // 2 if C % 2 == 0 else C
    spec = pl.BlockSpec((tc, L, 128), lambda i, j: (i, 0, j))
    shp = jax.ShapeDtypeStruct((C, L, R_), F32)
    return _pcall(functools.partial(_adamw_refs), name=name, grid=(C // tc, R_ // 128), in_specs=[spec] * 4,
                  out_specs=[spec] * 3, out_shape=[shp] * 3,
                  compiler_params=_cp(("parallel", "parallel")))(w, g, m, v)


HBM_SPEC = pl.BlockSpec(memory_space=pltpu.HBM)


def _place():
    x, y, c = lax.axis_index("x"), lax.axis_index("y"), lax.axis_index("c")
    chips = [(1 - x, y), (x, 1 - y), (1 - x, 1 - y)]
    return x, y, c, 2 * x + y, chips, [2 * cx + cy for cx, cy in chips], (x, y, 1 - c)


def _remote(src, dst, ssem, rsem, dev):
    return pltpu.make_async_remote_copy(src_ref=src, dst_ref=dst, send_sem=ssem, recv_sem=rsem,
                                        device_id=dev, device_id_type=MESH)


def _row_half(ref, lead, hc):
    hl = ref.shape[-2] // 2
    return ref.at[lead, pl.ds(hc * hl, hl), :]


def _gather_side(win, wout, layer):
    def copies(ins, outs, ssem, rsem):
        x, y, c, s, chips, sid, sib = _place()
        cps = []
        for j, chip in enumerate(chips):
            dev = (*chip, c)
            cps.append(_remote(_row_half(ins[0], layer, c), _row_half(outs[0], s, c), ssem.at[j], rsem.at[j], dev))
            cps.append(_remote(_row_half(ins[1], layer, c), _row_half(outs[1], s, c), ssem.at[3 + j], rsem.at[3 + j],
                               dev))
        return cps, c, sid, sib

    def start(ins, outs, ssem, rsem):
        for cp in copies(ins, outs, ssem, rsem)[0]:
            cp.start()

    def finish(ins, outs, ssem, rsem):
        cps, c, sid, sib = copies(ins, outs, ssem, rsem)
        for j in range(3):
            for k in range(2):
                got = _row_half(outs[k], sid[j], c)
                _remote(got, got, ssem.at[3 * k + j], rsem.at[3 * k + j], sib).wait_recv()
        for cp in cps:
            cp.wait_send()

    shapes = [jax.ShapeDtypeStruct((4,) + win.shape[1:], win.dtype), jax.ShapeDtypeStruct((4,) + wout.shape[1:], wout.dtype)]
    return _Side([win, wout], shapes, 6, start, finish)


def _gather_join(gin, gout, name):
    def body(gin_in, gout_in, gin_ref, gout_ref, ssem, rsem):
        x, y, c, s, chips, sid, sib = _place()
        cps = []
        for j in range(3):
            for k, ref in enumerate((gin_ref, gout_ref)):
                mine = _row_half(ref, sid[j], c)
                cps.append(_remote(mine, mine, ssem.at[3 * k + j], rsem.at[3 * k + j], sib))
        for cp in cps:
            cp.start()
        for j in range(3):
            for k, ref in enumerate((gin_ref, gout_ref)):
                other = _row_half(ref, sid[j], 1 - c)
                _remote(other, other, ssem.at[3 * k + j], rsem.at[3 * k + j], sib).wait_recv()
        for cp in cps:
            cp.wait_send()

    return _pcall(
        body, name=name, in_specs=[HBM_SPEC] * 2, out_specs=[HBM_SPEC] * 2,
        out_shape=[jax.ShapeDtypeStruct(gin.shape, gin.dtype), jax.ShapeDtypeStruct(gout.shape, gout.dtype)],
        input_output_aliases={0: 0, 1: 1},
        scratch_shapes=[pltpu.SemaphoreType.DMA((6,)), pltpu.SemaphoreType.DMA((6,))],
    )(gin, gout)


def _gather_layer0(win, wout, conv):
    def body(win_ref, wout_ref, cv_ref, gin_ref, gout_ref, gcv_ref, ssem, rsem):
        x, y, c, s, chips, sid, sib = _place()

        def in_half(slot, hc):
            return _row_half(gin_ref, slot, hc)

        def out_half(slot, hc):
            return _row_half(gout_ref, slot, hc)

        sends = []
        for j, chip in enumerate(chips):
            dev = (*chip, c)
            sends.append(_remote(_row_half(win_ref, 0, c), in_half(s, c), ssem.at[j], rsem.at[j], dev))
            sends.append(_remote(_row_half(wout_ref, 0, c), out_half(s, c), ssem.at[3 + j], rsem.at[3 + j], dev))
            sends.append(_remote(cv_ref, gcv_ref.at[s], ssem.at[6 + j], rsem.at[6 + j], dev))
        for cp in sends:
            cp.start()
        for j in range(3):
            _remote(in_half(sid[j], c), in_half(sid[j], c), ssem.at[j], rsem.at[j], sib).wait_recv()
            f = _remote(in_half(sid[j], c), in_half(sid[j], c), ssem.at[9 + j], rsem.at[9 + j], sib)
            f.start()
            sends.append(f)
            _remote(out_half(sid[j], c), out_half(sid[j], c), ssem.at[3 + j], rsem.at[3 + j], sib).wait_recv()
            f = _remote(out_half(sid[j], c), out_half(sid[j], c), ssem.at[12 + j], rsem.at[12 + j], sib)
            f.start()
            sends.append(f)
        for j in range(3):
            _remote(in_half(sid[j], 1 - c), in_half(sid[j], 1 - c), ssem.at[9 + j], rsem.at[9 + j], sib).wait_recv()
            _remote(out_half(sid[j], 1 - c), out_half(sid[j], 1 - c), ssem.at[12 + j], rsem.at[12 + j], sib).wait_recv()
            _remote(gcv_ref.at[sid[j]], gcv_ref.at[sid[j]], ssem.at[6 + j], rsem.at[6 + j], sib).wait_recv()
        for cp in sends:
            cp.wait_send()

    return _pcall(
        body, name="gather_layer0",
        in_specs=[HBM_SPEC] * 3, out_specs=[HBM_SPEC] * 3,
        out_shape=[jax.ShapeDtypeStruct((4,) + win.shape[1:], win.dtype),
                   jax.ShapeDtypeStruct((4,) + wout.shape[1:], wout.dtype),
                   jax.ShapeDtypeStruct((4,) + conv.shape, conv.dtype)],
        scratch_shapes=[pltpu.SemaphoreType.DMA((15,)), pltpu.SemaphoreType.DMA((15,))],
    )(win, wout, conv)


def _swap_halves(arrs, axes, name):
    n = len(arrs)

    def half_shape(a, ax):
        return a.shape[:ax] + (a.shape[ax] // 2,) + a.shape[ax + 1:]

    def body(*refs):
        src, dst, ssem, rsem = refs[:n], refs[n:2 * n], refs[2 * n], refs[2 * n + 1]
        x, y, c, s, chips, sid, sib = _place()
        cps = []
        for k in range(n):
            hl = src[k].shape[axes[k]] // 2
            idx = [slice(None)] * len(src[k].shape)
            idx[axes[k]] = pl.ds((1 - c) * hl, hl)
            cps.append(_remote(src[k].at[tuple(idx)], dst[k], ssem.at[k], rsem.at[k], sib))
        for cp in cps:
            cp.start()
        for cp in cps:
            cp.wait()

    return _pcall(
        body, name=name, in_specs=[HBM_SPEC] * n, out_specs=[HBM_SPEC] * n,
        out_shape=[jax.ShapeDtypeStruct(half_shape(a, ax), a.dtype) for a, ax in zip(arrs, axes)],
        scratch_shapes=[pltpu.SemaphoreType.DMA((n,)), pltpu.SemaphoreType.DMA((n,))],
    )(*arrs)


def _chips_side(arrs, per_target):
    n = len(arrs)

    def copies(ins, outs, ssem, rsem):
        x, y, c, s, chips, sid, sib = _place()
        cps = [_remote(ins[k].at[sid[j]] if per_target[k] else ins[k], outs[k].at[s],
                       ssem.at[3 * k + j], rsem.at[3 * k + j], (*chip, c))
               for k in range(n) for j, chip in enumerate(chips)]
        return cps, sid, sib

    def start(ins, outs, ssem, rsem):
        for cp in copies(ins, outs, ssem, rsem)[0]:
            cp.start()

    def finish(ins, outs, ssem, rsem):
        cps, sid, sib = copies(ins, outs, ssem, rsem)
        for k in range(n):
            for j in range(3):
                got = outs[k].at[sid[j]]
                _remote(got, got, ssem.at[3 * k + j], rsem.at[3 * k + j], sib).wait_recv()
        for cp in cps:
            cp.wait_send()

    shapes = [jax.ShapeDtypeStruct(a.shape if pt else (4,) + a.shape, a.dtype) for a, pt in zip(arrs, per_target)]
    return _Side(list(arrs), shapes, 3 * n, start, finish)


def _scatter_chips(arrs, per_target, name):
    n = len(arrs)

    def body(*refs):
        src, dst = refs[:n], refs[n:2 * n]
        ssem, rsem = refs[2 * n], refs[2 * n + 1]
        x, y, c, s, chips, sid, sib = _place()
        sends = []
        for k in range(n):
            for j, chip in enumerate(chips):
                piece = src[k].at[sid[j]] if per_target[k] else src[k]
                sends.append(_remote(piece, dst[k].at[s], ssem.at[3 * k + j], rsem.at[3 * k + j], (*chip, c)))
        for cp in sends:
            cp.start()
        for k in range(n):
            for j in range(3):
                _remote(dst[k].at[sid[j]], dst[k].at[sid[j]], ssem.at[3 * k + j], rsem.at[3 * k + j], sib).wait_recv()
        for cp in sends:
            cp.wait_send()

    outs = [jax.ShapeDtypeStruct(a.shape if pt else (4,) + a.shape, a.dtype) for a, pt in zip(arrs, per_target)]
    return _pcall(
        body, name=name, in_specs=[HBM_SPEC] * n, out_specs=[HBM_SPEC] * n, out_shape=outs,
        scratch_shapes=[pltpu.SemaphoreType.DMA((3 * n,)), pltpu.SemaphoreType.DMA((3 * n,))],
    )(*arrs)


def _swap_whole(arrs, name):
    n = len(arrs)

    def body(*refs):
        src, dst, ssem, rsem = refs[:n], refs[n:2 * n], refs[2 * n], refs[2 * n + 1]
        *_, sib = _place()
        cps = [_remote(src[k], dst[k], ssem.at[k], rsem.at[k], sib) for k in range(n)]
        for cp in cps:
            cp.start()
        for cp in cps:
            cp.wait()

    return _pcall(
        body, name=name, in_specs=[HBM_SPEC] * n, out_specs=[HBM_SPEC] * n,
        out_shape=[jax.ShapeDtypeStruct(a.shape, a.dtype) for a in arrs],
        scratch_shapes=[pltpu.SemaphoreType.DMA((n,)), pltpu.SemaphoreType.DMA((n,))],
    )(*arrs)


def _perm_cols(w):
    parts = [w[..., int(_ORIG_OFF[oi]):int(_ORIG_OFF[oi]) + IN_SIZES[oi]] for oi, _ in _PIECES]
    parts.append(jnp.zeros(w.shape[:-1] + (NP - N_IN,), w.dtype))
    return jnp.concatenate(parts, -1)


def _perm_rows(w):
    return jnp.concatenate([w[..., 512:1536, :], w[..., 0:512, :], w[..., 1536:2048, :]], -2)


_SMALL = ("sinks", "r_conv_b", "r_wa", "r_ba", "r_wx", "r_bx", "r_lam", "g_a_log", "g_dt_bias", "g_norm_w",
          "ln_g", "ln_b", "r_conv_w", "g_conv_w")
_PACK_ROWS = 16


def _piece_rows(n):
    return -(-n // (128 * _PACK_ROWS)) * _PACK_ROWS


def _pack(arrs):
    parts = []
    for a in arrs:
        n = int(np.prod(a.shape))
        rows = _piece_rows(n)
        if n % 128 == 0:
            blk = a.reshape(n // 128, 128)
        else:
            blk = jnp.pad(a.reshape(1, n), ((0, 0), (0, (-n) % 128))).reshape(-1, 128)
        if blk.shape[0] < rows:
            blk = jnp.pad(blk, ((0, rows - blk.shape[0]), (0, 0)))
        parts.append(blk)
    return jnp.concatenate(parts, 0)


def _unpack(packed, shapes):
    out = []
    r = 0
    for shp in shapes:
        n = int(np.prod(shp))
        if n % 128 == 0:
            out.append(packed[r:r + n // 128].reshape(shp))
        else:
            nr = -(-n // 128)
            out.append(packed[r:r + nr].reshape(1, nr * 128)[:, :n].reshape(shp))
        r += _piece_rows(n)
    return out


def _tile(n, t):
    return min(n, t)


def _layer_fwd(l, x, xb, wb, wob, rope_c, rope_s, p, side=None):
    S_ = x.shape[0]
    proj = _matmul(xb, wb, ta=False, tb=False, tm=_tile(S_, 1024), tn=NP // 4, tk=wb.shape[0], out_dtype=F32,
                   name=f"in_proj_{l}", side=side)
    side_out = None
    if side:
        proj, side_out = proj
    h, ymix = _rglru_fwd(proj, p["r_conv_w"], p["r_conv_b"], p["r_wa"], p["r_ba"], p["r_wx"], p["r_bx"], p["r_lam"],
                         T=_tile(S_, 256), name=f"rglru_fwd_{l}")
    ymix = _attn_fwd(proj, rope_c, rope_s, p["sinks"], ymix, T=_tile(S_, 512), name=f"attn_fwd_{l}")
    ymix, st, tms = _gdn_fwd(proj, p["g_conv_w"], p["g_a_log"], p["g_dt_bias"], p["g_norm_w"], ymix,
                             T=_tile(S_, 256), name=f"gdn_fwd_{l}")
    z = _outproj(ymix, wob, x, tm=_tile(S_, 256), name=f"out_proj_{l}")
    return dict(proj=proj, h=h, st=st, tms=tms, ymix=ymix, z=z, side=side_out)


def _layer_bwd(l, sv, x_b, dz, dzb, wb, wob, rope_c, rope_s, p, side_dw_in=None, side_dx=None):
    S_, D = dz.shape
    proj = sv["proj"]
    dymix = _matmul(dzb, wob, ta=False, tb=True, tm=_tile(S_, 1024), tn=512, tk=D, out_dtype=F32,
                    name=f"dmix_{l}")
    dwo = _matmul(sv["ymix"], dzb, ta=True, tb=False, tm=512, tn=_tile(D, 2048), tk=_tile(S_, 1024),
                  out_dtype=F32, name=f"dw_out_{l}",
                  out_blocks=((MIX_WIDTH, D), (512, _tile(D, 2048)),
                              lambda i, j: (jnp.where(i == 3, 3, (i + 1) % 3), j)))
    dproj, dk, dv, dkt, dvt, dsk = _attn_bwd(proj, rope_c, rope_s, p["sinks"], dymix, T=_tile(S_, 512),
                                             name=f"attn_bwd_{l}")
    (dproj, dcw_r, dcb_r, dwa, dba, dwx, dbx, dlam) = _rglru_bwd(
        proj, sv["h"], dymix, dproj, p["r_conv_w"], p["r_conv_b"], p["r_wa"], p["r_ba"], p["r_wx"], p["r_bx"],
        p["r_lam"], T=_tile(S_, 256), name=f"rglru_bwd_{l}")
    dproj, dgba, dcw_g, dpv, dnw = _gdn_bwd(proj, sv["st"], sv["tms"], dymix, dproj, p["g_conv_w"], p["g_a_log"],
                                            p["g_dt_bias"], p["g_norm_w"], T=_tile(S_, 256), name=f"gdn_bwd_{l}")
    tail = jnp.concatenate([dk[128:], dkt, dv[128:], dvt], 0).reshape(2, S_, 128)
    tail = jnp.concatenate([tail[0], tail[1], dgba, jnp.zeros((S_, NP - OFF_GBA - 128), BF16)], 1)
    dproj = lax.dynamic_update_slice(dproj, tail, (0, OFF_AK))
    small = dict(sinks=dsk[:, 0], r_conv_b=dcb_r[0], r_wa=dwa, r_ba=dba[0], r_wx=dwx, r_bx=dbx[0], r_lam=dlam[0],
                 g_a_log=dpv[0, 4:8], g_dt_bias=dpv[1, 4:8], g_norm_w=dnw[0], r_conv_w=dcw_r, g_conv_w=dcw_g)
    side = side_dw_in(small, dwo) if side_dw_in else None
    dwin = _matmul(x_b, dproj, ta=True, tb=False, tm=_tile(D, 1024), tn=NP // 4, tk=_tile(S_, 1024), out_dtype=F32,
                   name=f"dw_in_{l}", side=side)
    out_dw_in = None
    if side:
        dwin, out_dw_in = dwin
    side = side_dx(dwin) if side_dx else None
    tmx = _tile(S_, 1024)
    nblk = S_ // tmx
    dx_args = dict(ta=False, tb=True, tm=tmx, tn=_tile(D, 1024), tk=NP // 2, out_dtype=F32, extra=dz,
                   alpha=DEEPNORM_ALPHA)
    out_dx = None
    if side and nblk >= 4:
        head = nblk - nblk // 4
        dx, out_dx = _matmul(dproj, wb, name=f"dx_{l}", side=side, rows=(0, head), **dx_args)
        dx = _matmul(dproj, wb, name=f"dx_{l}_rest", rows=(head, nblk - head), into=dx, **dx_args)
    elif side:
        dx, out_dx = _matmul(dproj, wb, name=f"dx_{l}", side=side, **dx_args)
    else:
        dx = _matmul(dproj, wb, name=f"dx_{l}", **dx_args)
    return dx, dwin, dwo, small, out_dw_in, out_dx


def kernel(x, w_in, sinks, r_conv_w, r_conv_b, r_wa, r_ba, r_wx, r_bx, r_lam, g_conv_w, g_a_log, g_dt_bias, g_norm_w, w_out, ln_g, ln_b, loss_target, m_w_in, m_sinks, m_r_conv_w, m_r_conv_b, m_r_wa, m_r_ba, m_r_wx, m_r_bx, m_r_lam, m_g_conv_w, m_g_a_log, m_g_dt_bias, m_g_norm_w, m_w_out, m_ln_g, m_ln_b, v_w_in, v_sinks, v_r_conv_w, v_r_conv_b, v_r_wa, v_r_ba, v_r_wx, v_r_bx, v_r_lam, v_g_conv_w, v_g_a_log, v_g_dt_bias, v_g_norm_w, v_w_out, v_ln_g, v_ln_b):
    S_, D = x.shape[1], x.shape[2]
    nsh = w_in.shape[2]
    rsh = w_out.shape[1]
    cx, cy, cc = lax.axis_index("x"), lax.axis_index("y"), lax.axis_index("c")
    chip = 2 * cx + cy
    rcw_n, gcw_n = r_conv_w.shape[2], g_conv_w.shape[2]

    conv_pack = jnp.concatenate([r_conv_w, g_conv_w], 2)
    w_in_b, w_out_b = w_in.astype(BF16), w_out.astype(BF16)
    g_in0, g_out0, g_conv = _gather_layer0(w_in_b, w_out_b, conv_pack)

    def shards(own, got):
        return [jnp.where(chip == t, own, got[t]) for t in range(4)]

    def layer_weights(l, g_in, g_out):
        w_full = jnp.concatenate(shards(w_in_b[l], g_in), 1)
        return (_perm_cols(w_full),
                _perm_rows(jnp.concatenate(shards(w_out_b[l], g_out), 0)))

    rcw = jnp.concatenate(shards(r_conv_w, g_conv[:, :, :, :rcw_n]), 2)
    gcw = jnp.concatenate(shards(g_conv_w, g_conv[:, :, :, rcw_n:]), 2)

    pos = jnp.arange(S_, dtype=F32)[:, None]
    inv = 1.0 / (ROPE_THETA ** (jnp.arange(0, A_HEAD_DIM, 2, dtype=F32) / A_HEAD_DIM))
    ang = pos * inv[None, :]
    cos, sin = jnp.cos(ang), jnp.sin(ang)
    rope_c = jnp.concatenate([cos, cos, cos, cos], 1)
    rope_s = jnp.concatenate([-sin, sin, -sin, sin], 1)

    def params(l):
        return dict(sinks=sinks[l], r_conv_w=rcw[l], r_conv_b=r_conv_b[l], r_wa=r_wa[l], r_ba=r_ba[l],
                    r_wx=r_wx[l], r_bx=r_bx[l], r_lam=r_lam[l], g_conv_w=gcw[l], g_a_log=g_a_log[l],
                    g_dt_bias=g_dt_bias[l], g_norm_w=g_norm_w[l])

    xs, xbs, saved = [x[0]], [x[0].astype(BF16)], []
    wb, wob = [None] * DEPTH, [None] * DEPTH
    wb[0], wob[0] = layer_weights(0, g_in0, g_out0)
    for l in range(DEPTH):
        nxt = _gather_side(w_in_b, w_out_b, l + 1) if l + 1 < DEPTH else None
        sv = _layer_fwd(l, xs[l], xbs[l], wb[l], wob[l], rope_c, rope_s, params(l), side=nxt)
        if nxt:
            wb[l + 1], wob[l + 1] = layer_weights(l + 1, *_gather_join(*sv["side"], f"gather_join_{l + 1}"))
        saved.append(sv)
        if l + 1 < DEPTH:
            xn, xnb = _ln_fwd(sv["z"], ln_g[l], ln_b[l], tm=_tile(S_, 256), name=f"ln_fwd_{l}")
            xs.append(xn)
            xbs.append(xnb)

    tm_ln = _tile(S_, 256)
    dz, dzb, dg_l, db_l, loss_part = _ln_bwd(saved[-1]["z"], ln_g[-1], ln_b[-1], loss_target[0], from_target=True,
                                             tm=tm_ln, name=f"ln_bwd_{DEPTH - 1}")
    assert DEPTH == 2
    wcov = (-(-nsh // 128) + 1) * 128
    names = list(_SMALL)

    def own(a):
        return lax.dynamic_index_in_dim(a, chip, 0, keepdims=False)

    def sum_in(l, cp, arrived):
        return _sum4(arrived, own(cp), tr=_tile(D // 2, 256), name=f"chip_sum_w_in_{l}")

    def sum_out(l, cp, arrived):
        return _sum4(arrived, own(cp), tr=rsh // 2, name=f"chip_sum_w_out_{l}")

    dlng, dlnb = [None, dg_l[0]], [None, db_l[0]]
    dx, dwin1, dwo1, small1, _, _ = _layer_bwd(1, saved[1], xbs[1], dz, dzb, wb[1], wob[1], rope_c, rope_s, params(1))
    dwo1_4 = dwo1.reshape(4, rsh, D)
    got = _swap_halves([dwin1, dwo1_4], [0, 1], "reduce_pair_1")
    in_cp1 = _pair_sum_windows(dwin1, got[0], nsh, wcov, out_dtype=BF16, name="pair_sum_w_in_1")
    out_cp1 = _pair_sum_blocks(dwo1_4, got[1], out_dtype=BF16, name="pair_sum_w_out_1")
    dz, dzb, dg_l, db_l, _ = _ln_bwd(saved[0]["z"], ln_g[0], ln_b[0], dx, from_target=False, tm=tm_ln, name="ln_bwd_0")
    dlng[0], dlnb[0] = dg_l[0], db_l[0]

    held = {}

    def side_dw_in(small0, dwo0):
        sm = {k: jnp.stack([small0[k], small1[k]]) for k in small0}
        sm["ln_g"], sm["ln_b"] = jnp.stack(dlng), jnp.stack(dlnb)
        gs = _pack([sm[n] for n in names])
        dwo0_4 = dwo0.reshape(4, rsh, D)
        got = _swap_halves([dwo0_4, gs], [1, 0], "reduce_pair_0a")
        held["out_cp0"] = _pair_sum_blocks(dwo0_4, got[0], out_dtype=BF16, name="pair_sum_w_out_0")
        held["s_cp"] = _pair_sum_blocks(gs[None], got[1][None], out_dtype=F32, name="pair_sum_small")[0]
        held["shapes"] = [sm[n].shape for n in names]
        return _chips_side([in_cp1, out_cp1, held["out_cp0"], held["s_cp"]], [True, True, True, False])

    def side_dx(dwin0):
        got = _swap_halves([dwin0], [0], "reduce_pair_0b")
        held["in_cp0"] = _pair_sum_windows(dwin0, got[0], nsh, wcov, out_dtype=BF16, name="pair_sum_w_in_0")
        return _chips_side([held["in_cp0"]], [True])

    dx, _, _, _, arrived_a, arrived_b = _layer_bwd(0, saved[0], xbs[0], dz, dzb, wb[0], wob[0], rope_c, rope_s,
                                                   params(0), side_dw_in=side_dw_in, side_dx=side_dx)
    grad_x = dx[None]
    loss = lax.psum(loss_part[0, 0], ("x", "y", "c"))
    s_cp = held["s_cp"]
    mine = [sum_in(0, held["in_cp0"], arrived_b[0]), sum_out(0, held["out_cp0"], arrived_a[2]),
            sum_in(1, in_cp1, arrived_a[0]), sum_out(1, out_cp1, arrived_a[1]),
            _sum4(arrived_a[3], s_cp, tr=s_cp.shape[0], name="chip_sum_small")]
    other = _swap_whole(mine, "reduce_join")

    def both(k, axis):
        return jnp.where(cc == 0, jnp.concatenate([mine[k], other[k]], axis),
                         jnp.concatenate([other[k], mine[k]], axis))

    g_w_in = lax.dynamic_slice_in_dim(jnp.stack([both(2 * l, 0) for l in range(DEPTH)]), (nsh * chip) % 128, nsh, 2)
    g_w_out = jnp.stack([both(2 * l + 1, 0) for l in range(DEPTH)])
    g_small = both(2 * DEPTH, 0)

    gsm = dict(zip(names, _unpack(g_small, held["shapes"])))
    gsm["r_conv_w"] = lax.dynamic_slice_in_dim(gsm["r_conv_w"], chip * rcw_n, rcw_n, 2)
    gsm["g_conv_w"] = lax.dynamic_slice_in_dim(gsm["g_conv_w"], chip * gcw_n, gcw_n, 2)
    wts = dict(sinks=sinks, r_conv_w=r_conv_w, r_conv_b=r_conv_b, r_wa=r_wa, r_ba=r_ba, r_wx=r_wx, r_bx=r_bx,
               r_lam=r_lam, g_conv_w=g_conv_w, g_a_log=g_a_log, g_dt_bias=g_dt_bias, g_norm_w=g_norm_w,
               ln_g=ln_g, ln_b=ln_b)
    mom = dict(sinks=m_sinks, r_conv_w=m_r_conv_w, r_conv_b=m_r_conv_b, r_wa=m_r_wa, r_ba=m_r_ba, r_wx=m_r_wx,
               r_bx=m_r_bx, r_lam=m_r_lam, g_conv_w=m_g_conv_w, g_a_log=m_g_a_log, g_dt_bias=m_g_dt_bias,
               g_norm_w=m_g_norm_w, ln_g=m_ln_g, ln_b=m_ln_b)
    vel = dict(sinks=v_sinks, r_conv_w=v_r_conv_w, r_conv_b=v_r_conv_b, r_wa=v_r_wa, r_ba=v_r_ba, r_wx=v_r_wx,
               r_bx=v_r_bx, r_lam=v_r_lam, g_conv_w=v_g_conv_w, g_a_log=v_g_a_log, g_dt_bias=v_g_dt_bias,
               g_norm_w=v_g_norm_w, ln_g=v_ln_g, ln_b=v_ln_b)
    d_s, m_s, v_s = _adamw_many(*[[d[n] for n in names] for d in (wts, gsm, mom, vel)], name="adamw_small")
    d_sm, m_sm, v_sm = (dict(zip(names, a)) for a in (d_s, m_s, v_s))
    def cols(a):
        return jnp.transpose(a, (2, 0, 1))

    g_w_in_t = cols(g_w_in)
    outs_t = _adamw_cols(cols(w_in), g_w_in_t, cols(m_w_in), cols(v_w_in), name="adamw_w_in")
    d_in, m_in, v_in = (jnp.transpose(a, (1, 2, 0)) for a in outs_t)
    g_w_in = jnp.transpose(g_w_in_t, (1, 2, 0))
    d_out, m_out, v_out = _adamw(w_out, g_w_out, m_w_out, v_w_out, tr=256, name="adamw_w_out")

    order = ["w_in", "sinks", "r_conv_w", "r_conv_b", "r_wa", "r_ba", "r_wx", "r_bx", "r_lam", "g_conv_w",
             "g_a_log", "g_dt_bias", "g_norm_w", "w_out", "ln_g", "ln_b"]
    grads = dict(gsm, w_in=g_w_in, w_out=g_w_out)
    deltas = dict(d_sm, w_in=d_in, w_out=d_out)
    new_m = dict(m_sm, w_in=m_in, w_out=m_out)
    new_v = dict(v_sm, w_in=v_in, w_out=v_out)
    return (loss, grad_x, *[grads[n] for n in order], *[deltas[n] for n in order],
            *[new_m[n] for n in order], *[new_v[n] for n in order])
```

```python
import functools
import math

import jax
import jax.numpy as jnp
import numpy as np
from jax import lax
from jax.experimental import pallas as pl
from jax.experimental.pallas import tpu as pltpu

F32 = jnp.float32
BF16 = jnp.bfloat16
MESH = pl.DeviceIdType.MESH

DEPTH = 2
A_HEADS, A_KV_HEADS, A_HEAD_DIM = 8, 2, 64
A_WIDTH, A_KV_WIDTH = 512, 128
WINDOW = 128
ROPE_THETA = 10000.0
R_WIDTH, R_BLOCKS, R_BLOCK_DIM, R_C = 1024, 8, 128, 8.0
CONV_WIDTH = 4
G_HEADS, G_HEAD_DIM, G_WIDTH, G_CHUNK = 4, 128, 512, 64
MIX_WIDTH = 2048
IN_SIZES = (512, 128, 128, 512, 1024, 1024, 512, 512, 512, 512, 4, 4)
N_IN = 5384
DEEPNORM_ALPHA = (2 * DEPTH) ** 0.25
LN_EPS = 1e-5
RMS_EPS = 1e-6
ADAM_LR, ADAM_B1, ADAM_B2, ADAM_EPS, ADAM_WD, ADAM_STEP = 0.001, 0.9, 0.999, 1e-08, 0.01, 10

NP = 5632
OFF_GQKV, OFF_GZ, OFF_RX, OFF_RZ, OFF_AQ, OFF_AZ, OFF_AK, OFF_AV, OFF_GBA = (
    0, 1536, 2048, 3072, 4096, 4608, 5120, 5248, 5376)
_ORIG_OFF = np.concatenate([[0], np.cumsum(IN_SIZES)])[:-1]
_PIECES = ((6, OFF_GQKV), (7, OFF_GQKV + 512), (8, OFF_GQKV + 1024), (9, OFF_GZ), (4, OFF_RX), (5, OFF_RZ),
           (0, OFF_AQ), (3, OFF_AZ), (1, OFF_AK), (2, OFF_AV), (10, OFF_GBA), (11, OFF_GBA + 4))


def _orig_block_to_perm():
    table = list(range(NP // 128))
    for oi, off in _PIECES:
        if IN_SIZES[oi] % 128 == 0:
            for k in range(IN_SIZES[oi] // 128):
                table[int(_ORIG_OFF[oi]) // 128 + k] = off // 128 + k
    return table
MIX_R, MIX_A, MIX_G = 0, 1024, 1536
VMEM_LIMIT = 56 * 1024 * 1024
ANY_SPEC = pl.BlockSpec(memory_space=pl.ANY)


def _pcall(body, **kw):
    return pl.pallas_call(body, **kw)


def _cp(sem, limit=VMEM_LIMIT):
    return pltpu.CompilerParams(dimension_semantics=sem, vmem_limit_bytes=limit)


def _sigmoid(x):
    return 0.5 + 0.5 * jnp.tanh(0.5 * x)


def _silu(x):
    return x * _sigmoid(x)


def _dsilu(x):
    s = _sigmoid(x)
    return s * (1.0 + x * (1.0 - s))


def _log1p(x):
    u = 1.0 + x
    d = jnp.where(u == 1.0, 1.0, u - 1.0)
    return jnp.where(u == 1.0, x, jnp.log(u) * (x / d))


def _softplus(x):
    return jnp.maximum(x, 0.0) + _log1p(jnp.exp(-jnp.abs(x)))


def _one_minus_exp(x):
    series = -x * (1.0 + x * (0.5 + x * (1.0 / 6.0 + x * (1.0 / 24.0))))
    return jnp.where(x > -0.05, series, 1.0 - jnp.exp(x))


def _nn(a, b):
    return lax.dot_general(a, b, (((1,), (0,)), ((), ())), preferred_element_type=F32)


def _nt(a, b):
    return lax.dot_general(a, b, (((1,), (1,)), ((), ())), preferred_element_type=F32)


def _tn(a, b):
    return lax.dot_general(a, b, (((0,), (0,)), ((), ())), preferred_element_type=F32)


def _b(x):
    return x.astype(BF16)


def _split3(x):
    hi = x.astype(BF16)
    r1 = x - hi.astype(F32)
    mid = r1.astype(BF16)
    lo = (r1 - mid.astype(F32)).astype(BF16)
    return hi, mid, lo


def _dot3(f, a, b):
    ah, am, _ = _split3(a)
    bh, bm, _ = _split3(b)
    return f(ah, bh) + (f(ah, bm) + f(am, bh))


def _dot_exact_lhs(f, a_bf16, b):
    bh, bm, bl = _split3(b)
    return f(a_bf16, bh) + (f(a_bf16, bm) + f(a_bf16, bl))


def _rot(x):
    w = x.shape[-1]
    lane = lax.broadcasted_iota(jnp.int32, (1, w), 1)
    return jnp.where((lane & 63) < 32, pltpu.roll(x, w - 32, 1), pltpu.roll(x, 32, 1))


def _conv_taps(ext, n):
    return [pltpu.roll(ext, 3 - k, 0)[8:8 + n] if k < 3 else ext[8:8 + n] for k in range(CONV_WIDTH)]


def _conv_taps_t(ext, n):
    m = ext.shape[0]
    return [pltpu.roll(ext, m - (3 - k), 0)[0:n] if k < 3 else ext[0:n] for k in range(CONV_WIDTH)]


def _scan_steps(a, b, pos, span, shifts, reverse):
    n = a.shape[0]
    for s in shifts:
        if reverse:
            a_sh = pltpu.roll(a, n - s, 0)
            b_sh = pltpu.roll(b, n - s, 0)
            ok = pos < (span - s)
        else:
            a_sh = pltpu.roll(a, s, 0)
            b_sh = pltpu.roll(b, s, 0)
            ok = pos >= s
        b = jnp.where(ok, a * b_sh + b, b)
        a = jnp.where(ok, a * a_sh, a)
    return a, b


def _scan_lin(a, b, reverse):
    n = a.shape[0]
    shifts = []
    s = 1
    while s < n:
        shifts.append(s)
        s *= 2
    return _scan_steps(a, b, lax.broadcasted_iota(jnp.int32, (n, 1), 0), n, shifts, reverse)


class _Side:
    def __init__(self, inputs, out_shapes, n_sems, start, finish):
        self.inputs, self.out_shapes, self.n_sems, self.start, self.finish = inputs, out_shapes, n_sems, start, finish


def _matmul(a, b, *, ta, tb, tm, tn, tk, out_dtype, name, extra=None, alpha=0.0, out_blocks=None, side=None,
            rows=None, into=None):
    if ta:
        K, M = a.shape
    else:
        M, K = a.shape
    if tb:
        N, K2 = b.shape
    else:
        K2, N = b.shape
    assert K == K2 and M % tm == 0 and N % tn == 0 and K % tk == 0, (a.shape, b.shape, tm, tn, tk)
    nk = K // tk
    ca = 0 if ta else 1
    cb = 1 if tb else 0
    has_extra = extra is not None

    assert nk == 1 or out_dtype == F32
    assert rows is None or (not ta and out_blocks is None)
    r0, nrow = rows if rows else (0, M // tm)
    n_in = 2 + int(has_extra) + int(into is not None)
    ns_in = len(side.inputs) if side else 0
    ns_out = len(side.out_shapes) if side else 0
    grid = (nrow, N // tn, nk)

    def body(*refs):
        a_ref, b_ref = refs[0], refs[1]
        e_ref = refs[2] if has_extra else None
        o_ref = refs[n_in + ns_in]
        k = pl.program_id(2)
        if side:
            s_in = refs[n_in:n_in + ns_in]
            s_out = refs[n_in + ns_in + 1:n_in + ns_in + 1 + ns_out]
            ssem, rsem = refs[-2], refs[-1]
            i, j = pl.program_id(0), pl.program_id(1)

            @pl.when((i == 0) & (j == 0) & (k == 0))
            def _():
                side.start(s_in, s_out, ssem, rsem)

            @pl.when((i == grid[0] - 1) & (j == grid[1] - 1) & (k == grid[2] - 1))
            def _():
                side.finish(s_in, s_out, ssem, rsem)

        part = lax.dot_general(a_ref[...], b_ref[...], (((ca,), (cb,)), ((), ())), preferred_element_type=F32)
        if nk == 1:
            if e_ref is not None:
                part = part + alpha * e_ref[...]
            o_ref[...] = part.astype(o_ref.dtype)
            return

        @pl.when(k == 0)
        def _():
            o_ref[...] = part

        @pl.when((k > 0) & (k < nk - 1))
        def _():
            o_ref[...] += part

        @pl.when(k == nk - 1)
        def _():
            last = o_ref[...] + part
            if e_ref is not None:
                last = last + alpha * e_ref[...]
            o_ref[...] = last

    a_spec = (pl.BlockSpec((tk, tm), lambda i, j, k: (k, i)) if ta
              else pl.BlockSpec((tm, tk), lambda i, j, k: (i + r0, k)))
    b_spec = (pl.BlockSpec((tn, tk), lambda i, j, k: (j, k)) if tb
              else pl.BlockSpec((tk, tn), lambda i, j, k: (k, j)))
    e_spec = pl.BlockSpec((tm, tn), lambda i, j, k: (i + r0, j))
    if out_blocks is None:
        o_spec, o_shape = e_spec, (M, N)
    else:
        o_shape, o_block, o_map = out_blocks
        o_spec = pl.BlockSpec(o_block, lambda i, j, k: o_map(i, j))
    in_specs = [a_spec, b_spec] + ([e_spec] if has_extra else []) + ([ANY_SPEC] if into is not None else [])
    args = (a, b) + ((extra,) if has_extra else ()) + ((into,) if into is not None else ())
    alias = {n_in - 1: 0} if into is not None else {}
    if not side:
        return _pcall(
            body, name=name, grid=grid, in_specs=in_specs, out_specs=o_spec,
            out_shape=jax.ShapeDtypeStruct(o_shape, out_dtype), input_output_aliases=alias,
            compiler_params=_cp(("parallel", "parallel", "arbitrary")),
        )(*args)
    outs = _pcall(
        body, name=name, grid=grid, in_specs=in_specs + [HBM_SPEC] * ns_in,
        out_specs=[o_spec] + [HBM_SPEC] * ns_out,
        out_shape=[jax.ShapeDtypeStruct(o_shape, out_dtype)] + list(side.out_shapes), input_output_aliases=alias,
        scratch_shapes=[pltpu.SemaphoreType.DMA((side.n_sems,)), pltpu.SemaphoreType.DMA((side.n_sems,))],
        compiler_params=_cp(("arbitrary", "arbitrary", "arbitrary")),
    )(*args, *side.inputs)
    return outs[0], outs[1:]


def _ln_stats(z):
    mu = jnp.mean(z, -1, keepdims=True)
    zc = z - mu
    var = jnp.mean(zc * zc, -1, keepdims=True)
    rstd = lax.rsqrt(var + LN_EPS)
    return zc * rstd, rstd


def _ln_bwd_tile(z, gam, bet, other, from_target, dz_ref, dzb_ref, dg_ref, db_ref, loss_ref):
    i = pl.program_id(0)

    @pl.when(i == 0)
    def _():
        dg_ref[...] = jnp.zeros_like(dg_ref)
        db_ref[...] = jnp.zeros_like(db_ref)
        loss_ref[...] = jnp.zeros_like(loss_ref)

    xh, rstd = _ln_stats(z)
    if from_target:
        err = xh * gam + bet - other
        per_tok = jnp.mean(err * err, -1, keepdims=True)
        loss_ref[...] += 0.5 * jnp.sum(per_tok, 0, keepdims=True)
        dy = err * (1.0 / z.shape[-1])
    else:
        dy = other
    dxh = dy * gam
    m1 = jnp.mean(dxh, -1, keepdims=True)
    m2 = jnp.mean(dxh * xh, -1, keepdims=True)
    dz = rstd * (dxh - m1 - xh * m2)
    dz_ref[...] = dz
    dzb_ref[...] = dz.astype(BF16)
    dg_ref[...] += jnp.sum(dy * xh, 0, keepdims=True)
    db_ref[...] += jnp.sum(dy, 0, keepdims=True)


def _outproj(ymix, wo, x, g, b, *, tm, name, target=None):
    S_, D = x.shape
    last = target is not None

    def body(*refs):
        y_ref, w_ref, x_ref, g_ref, b_ref = refs[:5]
        z = DEEPNORM_ALPHA * x_ref[...] + _nn(y_ref[...], w_ref[...])
        if last:
            _ln_bwd_tile(z, g_ref[...], b_ref[...], refs[5][...], True, *refs[6:])
            return
        z_ref, o_ref, ob_ref = refs[5:]
        z_ref[...] = z
        xh, _ = _ln_stats(z)
        y = xh * g_ref[...] + b_ref[...]
        o_ref[...] = y
        ob_ref[...] = y.astype(BF16)

    row = pl.BlockSpec((tm, D), lambda i: (i, 0))
    vec = pl.BlockSpec((1, D), lambda i: (0, 0))
    one = pl.BlockSpec((1, 1), lambda i: (0, 0))
    in_specs = [pl.BlockSpec((tm, MIX_WIDTH), lambda i: (i, 0)), pl.BlockSpec((MIX_WIDTH, D), lambda i: (0, 0)),
                row, vec, vec]
    f32s, b16s = jax.ShapeDtypeStruct((S_, D), F32), jax.ShapeDtypeStruct((S_, D), BF16)
    v32s = jax.ShapeDtypeStruct((1, D), F32)
    args = (ymix, wo, x, g.reshape(1, D), b.reshape(1, D))
    if last:
        return _pcall(body, name=name, grid=(S_ // tm,), in_specs=in_specs + [row],
                      out_specs=[row, row, vec, vec, one],
                      out_shape=[f32s, b16s, v32s, v32s, jax.ShapeDtypeStruct((1, 1), F32)],
                      compiler_params=_cp(("arbitrary",)))(*args, target)
    return _pcall(body, name=name, grid=(S_ // tm,), in_specs=in_specs, out_specs=[row, row, row],
                  out_shape=[f32s, f32s, b16s], compiler_params=_cp(("parallel",)))(*args)


def _ln_bwd(z, g, b, dy, *, tm, name):
    S_, D = z.shape

    def body(z_ref, g_ref, b_ref, o_ref, *outs):
        _ln_bwd_tile(z_ref[...], g_ref[...], b_ref[...], o_ref[...], False, *outs)

    row = pl.BlockSpec((tm, D), lambda i: (i, 0))
    vec = pl.BlockSpec((1, D), lambda i: (0, 0))
    one = pl.BlockSpec((1, 1), lambda i: (0, 0))
    return _pcall(
        body, name=name, grid=(S_ // tm,), in_specs=[row, vec, vec, row],
        out_specs=[row, row, vec, vec, one],
        out_shape=[jax.ShapeDtypeStruct((S_, D), F32), jax.ShapeDtypeStruct((S_, D), BF16),
                   jax.ShapeDtypeStruct((1, D), F32), jax.ShapeDtypeStruct((1, D), F32),
                   jax.ShapeDtypeStruct((1, 1), F32)],
        compiler_params=_cp(("arbitrary",)),
    )(z, g.reshape(1, D), b.reshape(1, D), dy)


def _attn_masks(i, sk_ref):
    ri = lax.broadcasted_iota(jnp.int32, (512, 256), 0)
    cj = lax.broadcasted_iota(jnp.int32, (512, 256), 1)
    diff = (ri & 127) - cj + 128
    band = (diff >= 0) & (diff < WINDOW)
    bias = jnp.where(band, 0.0, -jnp.inf)
    bias0 = jnp.where(band & ((i > 0) | (cj >= 128)), 0.0, -jnp.inf)
    grp = lax.broadcasted_iota(jnp.int32, (512, 1), 0) >> 7
    skvs = []
    for h in range(A_KV_HEADS):
        skv = jnp.zeros((512, 1), F32)
        for g in range(4):
            skv = jnp.where(grp == g, sk_ref[h * 4 + g], skv)
        skvs.append(skv)
    return bias0, bias, skvs


def _attn_common(masks, b, h, qr, kd, vd):
    lane = lax.broadcasted_iota(jnp.int32, (1, 128), 1)
    lof = (lane < 64).astype(F32)
    hif = 1.0 - lof
    r0 = b * 128
    skv = masks[2][h]
    pairs = [qr[r0:r0 + 128, h * 256 + p * 128:h * 256 + (p + 1) * 128] for p in (0, 1)]
    qs = _b(jnp.concatenate([pairs[0] * lof, pairs[0] * hif, pairs[1] * lof, pairs[1] * hif], 0))
    k2 = kd[h][r0:r0 + 256]
    v2 = vd[h][r0:r0 + 256]
    s = _nt(qs, k2) * (A_HEAD_DIM ** -0.5) + (masks[0] if b == 0 else masks[1])
    m = jnp.maximum(jnp.max(s, 1, keepdims=True), skv)
    p = jnp.exp(s - m)
    esk = jnp.exp(skv - m)
    rz = 1.0 / (jnp.sum(p, 1, keepdims=True) + esk)
    prob = p * rz
    o4 = _nn(_b(prob), v2)
    return lof, hif, qs, k2, v2, prob, esk * rz, o4


def _attn_prep(T, q_ref, k_ref, v_ref, c_ref, s_ref, kprev, vprev):
    C = c_ref[...]
    Sg = s_ref[...]
    C4 = jnp.concatenate([C] * 4, 1)
    S4 = jnp.concatenate([Sg] * 4, 1)
    q = q_ref[...]
    qr = q * C4 + _rot(q) * S4
    k = k_ref[...]
    kr = k * C + _rot(k) * Sg
    v = v_ref[...]
    kext = jnp.concatenate([kprev[...], kr], 0)
    vext = jnp.concatenate([vprev[...], v], 0)
    kprev[...] = kr[T - 128:]
    vprev[...] = v[T - 128:]
    lo = lax.broadcasted_iota(jnp.int32, (1, 128), 1) < 64
    kroll = pltpu.roll(kext, 64, 1)
    vroll = pltpu.roll(vext, 64, 1)
    kd = [_b(jnp.where(lo, kext, kroll)), _b(jnp.where(lo, kroll, kext))]
    vd = [_b(jnp.where(lo, vext, vroll)), _b(jnp.where(lo, vroll, vext))]
    return C, Sg, C4, S4, qr, kd, vd


def _attn_specs(T):
    return [pl.BlockSpec(memory_space=pltpu.SMEM),
            pl.BlockSpec((T, 512), lambda i: (i, OFF_AQ // 512)),
            pl.BlockSpec((T, 512), lambda i: (i, OFF_AZ // 512)),
            pl.BlockSpec((T, 128), lambda i: (i, OFF_AK // 128)),
            pl.BlockSpec((T, 128), lambda i: (i, OFF_AV // 128)),
            pl.BlockSpec((T, 128), lambda i: (i, 0)),
            pl.BlockSpec((T, 128), lambda i: (i, 0))]


def _attn_fwd(proj, rope_c, rope_s, sinks, ymix, *, T, name):
    S_ = proj.shape[0]
    nb = T // 128

    def body(sk_ref, q_ref, z_ref, k_ref, v_ref, c_ref, s_ref, _, y_ref, kprev, vprev):
        i = pl.program_id(0)

        @pl.when(i == 0)
        def _():
            kprev[...] = jnp.zeros_like(kprev)
            vprev[...] = jnp.zeros_like(vprev)

        _, _, _, _, qr, kd, vd = _attn_prep(T, q_ref, k_ref, v_ref, c_ref, s_ref, kprev, vprev)
        masks = _attn_masks(i, sk_ref)
        for b in range(nb):
            r0 = b * 128
            for h in range(2):
                lof, hif, _, _, _, _, _, o4 = _attn_common(masks, b, h, qr, kd, vd)
                for p in range(2):
                    cs = slice(h * 256 + p * 128, h * 256 + (p + 1) * 128)
                    o = o4[2 * p * 128:(2 * p + 1) * 128] * lof + o4[(2 * p + 1) * 128:(2 * p + 2) * 128] * hif
                    y_ref[r0:r0 + 128, cs] = (o * _silu(z_ref[r0:r0 + 128, cs])).astype(BF16)

    return _pcall(
        body, name=name, grid=(S_ // T,), in_specs=_attn_specs(T) + [ANY_SPEC],
        out_specs=pl.BlockSpec((T, 512), lambda i: (i, MIX_A // 512)),
        out_shape=jax.ShapeDtypeStruct(ymix.shape, BF16),
        input_output_aliases={7: 0},
        scratch_shapes=[pltpu.VMEM((128, 128), F32), pltpu.VMEM((128, 128), F32)],
        compiler_params=_cp(("arbitrary",)),
    )(sinks, proj, proj, proj, proj, rope_c, rope_s, ymix)


def _attn_bwd(proj, rope_c, rope_s, sinks, dymix, *, T, name):
    S_ = proj.shape[0]
    nb = T // 128
    nt = S_ // T

    def body(sk_ref, q_ref, z_ref, k_ref, v_ref, c_ref, s_ref, dy_ref,
             dp_ref, dk_ref, dv_ref, dkt_ref, dvt_ref, dsk_ref,
             kprev, vprev, cprev, sprev, dkacc, dvacc, dqacc):
        i = pl.program_id(0)

        @pl.when(i == 0)
        def _():
            kprev[...] = jnp.zeros_like(kprev)
            vprev[...] = jnp.zeros_like(vprev)
            cprev[...] = jnp.zeros_like(cprev)
            sprev[...] = jnp.zeros_like(sprev)
            dkacc[...] = jnp.zeros_like(dkacc)
            dvacc[...] = jnp.zeros_like(dvacc)
            dsk_ref[...] = jnp.zeros_like(dsk_ref)

        @pl.when(i > 0)
        def _():
            dkacc[0:128, :] = dkacc[T:T + 128, :]
            dvacc[0:128, :] = dvacc[T:T + 128, :]
            dkacc[128:, :] = jnp.zeros((T, 128), F32)
            dvacc[128:, :] = jnp.zeros((T, 128), F32)

        C, Sg, C4, S4, qr, kd, vd = _attn_prep(T, q_ref, k_ref, v_ref, c_ref, s_ref, kprev, vprev)
        masks = _attn_masks(i, sk_ref)
        lane = lax.broadcasted_iota(jnp.int32, (1, 128), 1)
        for b in range(nb):
            r0 = b * 128
            for h in range(2):
                lof, hif, qs, k2, v2, prob, psink, o4 = _attn_common(masks, b, h, qr, kd, vd)
                dos = []
                for p in range(2):
                    cs = slice(h * 256 + p * 128, h * 256 + (p + 1) * 128)
                    o = o4[2 * p * 128:(2 * p + 1) * 128] * lof + o4[(2 * p + 1) * 128:(2 * p + 2) * 128] * hif
                    zc = z_ref[r0:r0 + 128, cs]
                    dyc = dy_ref[r0:r0 + 128, cs]
                    dp_ref[r0:r0 + 128, 512 + cs.start:512 + cs.stop] = (dyc * o * _dsilu(zc)).astype(BF16)
                    do = dyc * _silu(zc)
                    dos += [do * lof, do * hif]
                dos = jnp.concatenate(dos, 0)
                os_ = jnp.concatenate([o4[0:128] * lof, o4[128:256] * hif, o4[256:384] * lof, o4[384:512] * hif], 0)
                delta = jnp.sum(dos * os_, 1, keepdims=True)
                dosb = _b(dos)
                dp = _nt(dosb, v2)
                ds = prob * (dp - delta)
                dsv = -psink * delta
                for g in range(4):
                    sg = jnp.sum(dsv[g * 128:(g + 1) * 128], 0, keepdims=True)
                    hd = h * 4 + g
                    dsk_ref[hd:hd + 1, :] += jnp.broadcast_to(sg, (1, 128))
                dsb = _b(ds * (A_HEAD_DIM ** -0.5))
                dqs = _nn(dsb, k2)
                for p in range(2):
                    cs = slice(h * 256 + p * 128, h * 256 + (p + 1) * 128)
                    dqacc[r0:r0 + 128, cs] = (dqs[2 * p * 128:(2 * p + 1) * 128] * lof
                                              + dqs[(2 * p + 1) * 128:(2 * p + 2) * 128] * hif)
                dkdup = _tn(dsb, qs)
                dvdup = _tn(_b(prob), dosb)
                half = (lane < 64) if h == 0 else (lane >= 64)
                dkacc[r0:r0 + 256, :] += jnp.where(half, dkdup + pltpu.roll(dkdup, 64, 1), 0.0)
                dvacc[r0:r0 + 256, :] += jnp.where(half, dvdup + pltpu.roll(dvdup, 64, 1), 0.0)
        dqr = dqacc[...]
        dp_ref[:, 0:512] = (dqr * C4 + _rot(dqr * S4)).astype(BF16)
        cext = jnp.concatenate([cprev[...], C], 0)
        sext = jnp.concatenate([sprev[...], Sg], 0)
        dke = dkacc[...]
        dkp = dke * cext + _rot(dke * sext)
        dk_ref[...] = dkp[0:T].astype(BF16)
        dkt_ref[...] = dkp[T:T + 128].astype(BF16)
        dve = dvacc[...]
        dv_ref[...] = dve[0:T].astype(BF16)
        dvt_ref[...] = dve[T:T + 128].astype(BF16)
        cprev[...] = C[T - 128:]
        sprev[...] = Sg[T - 128:]

    nar = pl.BlockSpec((T, 128), lambda i: (i, 0))
    tail = pl.BlockSpec((128, 128), lambda i: (0, 0))
    return _pcall(
        body, name=name, grid=(nt,),
        in_specs=_attn_specs(T) + [pl.BlockSpec((T, 512), lambda i: (i, MIX_A // 512))],
        out_specs=[pl.BlockSpec((T, 1024), lambda i: (i, OFF_AQ // 1024)), nar, nar, tail, tail,
                   pl.BlockSpec((8, 128), lambda i: (0, 0))],
        out_shape=[jax.ShapeDtypeStruct((S_, NP), BF16),
                   jax.ShapeDtypeStruct((S_, 128), BF16), jax.ShapeDtypeStruct((S_, 128), BF16),
                   jax.ShapeDtypeStruct((128, 128), BF16), jax.ShapeDtypeStruct((128, 128), BF16),
                   jax.ShapeDtypeStruct((8, 128), F32)],
        scratch_shapes=[pltpu.VMEM((128, 128), F32)] * 4
        + [pltpu.VMEM((T + 128, 128), F32), pltpu.VMEM((T + 128, 128), F32), pltpu.VMEM((T, 512), F32)],
        compiler_params=_cp(("arbitrary",)),
    )(sinks, proj, proj, proj, proj, rope_c, rope_s, dymix)


def _rg_gates(xr, wa_ref, ba_ref, wx_ref, bx_ref, lam_ref):
    xb = _b(xr)
    pre_a = jnp.concatenate([_nn(xb[:, n * 128:(n + 1) * 128], wa_ref[n]) for n in range(R_BLOCKS)], 1) + ba_ref[...]
    pre_x = jnp.concatenate([_nn(xb[:, n * 128:(n + 1) * 128], wx_ref[n]) for n in range(R_BLOCKS)], 1) + bx_ref[...]
    r = _sigmoid(pre_a)
    ig = _sigmoid(pre_x)
    sp = _softplus(-lam_ref[...])
    log_a = -R_C * r * sp
    a = jnp.exp(log_a)
    mult = jnp.sqrt(_one_minus_exp(2.0 * log_a))
    return xb, r, ig, sp, a, mult


def _rg_param_specs():
    C = R_WIDTH
    vec = pl.BlockSpec((1, C), lambda i: (0, 0))
    blk = pl.BlockSpec((R_BLOCKS, 128, 128), lambda i: (0, 0, 0))
    return [pl.BlockSpec((CONV_WIDTH, C), lambda i: (0, 0)), vec, blk, vec, blk, vec, vec]


def _rglru_fwd(proj, cw, cb, wa, ba, wx, bx, lam, *, T, name):
    S_ = proj.shape[0]
    C = R_WIDTH

    def body(rx_ref, rz_ref, cw_ref, cb_ref, wa_ref, ba_ref, wx_ref, bx_ref, lam_ref,
             h_ref, y_ref, halo, hcar):
        i = pl.program_id(0)

        @pl.when(i == 0)
        def _():
            halo[...] = jnp.zeros_like(halo)
            hcar[...] = jnp.zeros_like(hcar)

        rx = rx_ref[...]
        ext = jnp.concatenate([halo[...], rx], 0)
        halo[...] = rx[T - 8:]
        taps = _conv_taps(ext, T)
        xr = cb_ref[...] + sum(cw_ref[k:k + 1, :] * taps[k] for k in range(CONV_WIDTH))
        _, _, ig, _, a, mult = _rg_gates(xr, wa_ref, ba_ref, wx_ref, bx_ref, lam_ref)
        u = mult * (ig * xr)
        acum, hloc = _scan_lin(a, u, False)
        h = hloc + acum * hcar[0:1, :]
        hcar[...] = jnp.broadcast_to(h[T - 1:T, :], (8, C))
        h_ref[...] = h
        y_ref[...] = (h * _silu(rz_ref[...])).astype(BF16)

    row = pl.BlockSpec((T, C), lambda i: (i, 0))
    return _pcall(
        body, name=name, grid=(S_ // T,),
        in_specs=[pl.BlockSpec((T, C), lambda i: (i, OFF_RX // C)),
                  pl.BlockSpec((T, C), lambda i: (i, OFF_RZ // C))] + _rg_param_specs(),
        out_specs=[row, pl.BlockSpec((T, C), lambda i: (i, MIX_R // C))],
        out_shape=[jax.ShapeDtypeStruct((S_, C), F32), jax.ShapeDtypeStruct((S_, MIX_WIDTH), BF16)],
        scratch_shapes=[pltpu.VMEM((8, C), F32), pltpu.VMEM((8, C), F32)],
        compiler_params=_cp(("arbitrary",)),
    )(proj, proj, cw, cb.reshape(1, C), _b(wa), ba.reshape(1, C), _b(wx), bx.reshape(1, C), lam.reshape(1, C))


def _rglru_bwd(proj, h, dymix, dproj, cw, cb, wa, ba, wx, bx, lam, *, T, name):
    S_ = proj.shape[0]
    C = R_WIDTH
    nt = S_ // T
    t8 = T // 8

    def body(rx_ref, rxp_ref, rz_ref, h_ref, hp_ref, dy_ref,
             cw_ref, cb_ref, wa_ref, ba_ref, wx_ref, bx_ref, lam_ref, wat_ref, wxt_ref,
             _, dp_ref, dcw_ref, dcb_ref, dwa_ref, dba_ref, dwx_ref, dbx_ref, dlam_ref,
             afirst, gfirst, dhalo):
        i = pl.program_id(0)
        first_tile = (i == nt - 1)

        @pl.when(i == 0)
        def _():
            afirst[...] = jnp.zeros_like(afirst)
            gfirst[...] = jnp.zeros_like(gfirst)
            dhalo[...] = jnp.zeros_like(dhalo)
            for r in (dcw_ref, dcb_ref, dwa_ref, dba_ref, dwx_ref, dbx_ref, dlam_ref):
                r[...] = jnp.zeros_like(r)

        keep = jnp.where(first_tile, 0.0, 1.0)
        rx = rx_ref[...]
        ext = jnp.concatenate([rxp_ref[...] * keep, rx], 0)
        taps = _conv_taps(ext, T)
        xr = cb_ref[...] + sum(cw_ref[k:k + 1, :] * taps[k] for k in range(CONV_WIDTH))
        xb, r, ig, sp, a, mult = _rg_gates(xr, wa_ref, ba_ref, wx_ref, bx_ref, lam_ref)
        hh = h_ref[...]
        rz = rz_ref[...]
        dy = dy_ref[...]
        dp_ref[:, C:2 * C] = (dy * hh * _dsilu(rz)).astype(BF16)
        dh = dy * _silu(rz)
        row = lax.broadcasted_iota(jnp.int32, (T, 1), 0)
        c = jnp.where(row == T - 1, afirst[0:1, :], pltpu.roll(a, T - 1, 0))
        ccum, gloc = _scan_lin(c, dh, True)
        g = gloc + ccum * gfirst[0:1, :]
        afirst[...] = jnp.broadcast_to(a[0:1, :], (8, C))
        gfirst[...] = jnp.broadcast_to(g[0:1, :], (8, C))
        hprev = jnp.where(row == 0, hp_ref[7:8, :] * keep, pltpu.roll(hh, 1, 0))
        da = g * hprev
        gx = ig * xr
        dgx = g * mult
        dmult = g * gx
        dlog_a = da * a - dmult * (a * a) * lax.rsqrt(mult * mult)
        dpre_a = dlog_a * (-R_C * sp) * r * (1.0 - r)
        dpre_x = dgx * xr * ig * (1.0 - ig)
        dlam_ref[...] += jnp.sum(dlog_a * (-R_C * r), 0, keepdims=True) * (-_sigmoid(-lam_ref[...]))
        dab = _b(dpre_a)
        dxb = _b(dpre_x)
        dxr = dgx * ig + jnp.concatenate(
            [_nn(dab[:, n * 128:(n + 1) * 128], wat_ref[n]) + _nn(dxb[:, n * 128:(n + 1) * 128], wxt_ref[n])
             for n in range(R_BLOCKS)], 1)
        for n in range(R_BLOCKS):
            cs = slice(n * 128, (n + 1) * 128)
            dwa_ref[n] += _tn(xb[:, cs], dab[:, cs])
            dwx_ref[n] += _tn(xb[:, cs], dxb[:, cs])
        dba_ref[...] += jnp.sum(dpre_a, 0, keepdims=True)
        dbx_ref[...] += jnp.sum(dpre_x, 0, keepdims=True)
        dcb_ref[...] += jnp.sum(dxr, 0, keepdims=True)
        for k in range(CONV_WIDTH):
            dcw_ref[k:k + 1, :] += jnp.sum(dxr * taps[k], 0, keepdims=True)
        ext2 = jnp.concatenate([dxr, dhalo[...]], 0)
        tt = _conv_taps_t(ext2, T)
        dp_ref[:, 0:C] = sum(cw_ref[k:k + 1, :] * tt[k] for k in range(CONV_WIDTH)).astype(BF16)
        dhalo[...] = dxr[0:8]

    def rev(i):
        return nt - 1 - i

    def prev8(i):
        return jnp.maximum(rev(i) * t8 - 1, 0)

    vec = pl.BlockSpec((1, C), lambda i: (0, 0))
    blk = pl.BlockSpec((R_BLOCKS, 128, 128), lambda i: (0, 0, 0))
    row = pl.BlockSpec((T, C), lambda i: (rev(i), 0))
    wat = _b(jnp.swapaxes(wa, 1, 2))
    wxt = _b(jnp.swapaxes(wx, 1, 2))
    return _pcall(
        body, name=name, grid=(nt,),
        in_specs=[pl.BlockSpec((T, C), lambda i: (rev(i), OFF_RX // C)),
                  pl.BlockSpec((8, C), lambda i: (prev8(i), OFF_RX // C)),
                  pl.BlockSpec((T, C), lambda i: (rev(i), OFF_RZ // C)),
                  row,
                  pl.BlockSpec((8, C), lambda i: (prev8(i), 0)),
                  pl.BlockSpec((T, C), lambda i: (rev(i), MIX_R // C)),
                  ] + _rg_param_specs() + [blk, blk, ANY_SPEC],
        out_specs=[pl.BlockSpec((T, 2 * C), lambda i: (rev(i), OFF_RX // (2 * C))),
                   pl.BlockSpec((CONV_WIDTH, C), lambda i: (0, 0)), vec, blk, vec, blk, vec, vec],
        out_shape=[jax.ShapeDtypeStruct(dproj.shape, BF16),
                   jax.ShapeDtypeStruct((CONV_WIDTH, C), F32), jax.ShapeDtypeStruct((1, C), F32),
                   jax.ShapeDtypeStruct((R_BLOCKS, 128, 128), F32), jax.ShapeDtypeStruct((1, C), F32),
                   jax.ShapeDtypeStruct((R_BLOCKS, 128, 128), F32), jax.ShapeDtypeStruct((1, C), F32),
                   jax.ShapeDtypeStruct((1, C), F32)],
        input_output_aliases={15: 0},
        scratch_shapes=[pltpu.VMEM((8, C), F32)] * 3,
        compiler_params=_cp(("arbitrary",)),
    )(proj, proj, proj, h, h, dymix, cw, cb.reshape(1, C), _b(wa), ba.reshape(1, C), _b(wx), bx.reshape(1, C),
      lam.reshape(1, C), wat, wxt, dproj)


GW3 = 3 * G_WIDTH


def _lane_col(x, lane_idx):
    lane = lax.broadcasted_iota(jnp.int32, (1, x.shape[1]), 1)
    return jnp.sum(jnp.where(lane == lane_idx, x, 0.0), 1, keepdims=True)


def _gdn_pre(ext, T, cw_ref, gba, pv_ref):
    taps = _conv_taps(ext, T)
    c = sum(cw_ref[k:k + 1, :] * taps[k] for k in range(CONV_WIDTH))
    qkv = _silu(c)
    beta = _sigmoid(gba)
    sarg = gba + pv_ref[1:2, :]
    nea = -jnp.exp(pv_ref[0:1, :])
    gdec = nea * _softplus(sarg)
    ri = lax.broadcasted_iota(jnp.int32, (T, T), 0)
    cj = lax.broadcasted_iota(jnp.int32, (T, T), 1)
    same = (ri >> 6) == (cj >> 6)
    ltri = jnp.where((ri >= cj) & same, 1.0, 0.0).astype(BF16)
    gc = _dot_exact_lhs(_nn, ltri, gdec)
    return taps, c, qkv, beta, sarg, nea, gdec, gc


def _gdn_masks():
    ri = lax.broadcasted_iota(jnp.int32, (128, 128), 0)
    cj = lax.broadcasted_iota(jnp.int32, (128, 128), 1)
    same = (ri >> 6) == (cj >> 6)
    return (ri >= cj) & same, (ri > cj) & same, ri == cj


def _lockstep(gens):
    out = [None] * len(gens)
    live = list(range(len(gens)))
    while live:
        still = []
        for k in live:
            try:
                next(gens[k])
                still.append(k)
            except StopIteration as stop:
                out[k] = stop.value
        live = still
    return out


def _gdn_chunk(qkv, beta, gc, rs, h, tm=None):
    tril, strict, eye = _gdn_masks()
    rowi = lax.broadcasted_iota(jnp.int32, (128, 1), 0)
    lane = lax.broadcasted_iota(jnp.int32, (1, 128), 1)
    qh = qkv[rs, h * 128:(h + 1) * 128]
    kh = qkv[rs, 512 + h * 128:512 + (h + 1) * 128]
    vh = qkv[rs, 1024 + h * 128:1024 + (h + 1) * 128]
    rq = lax.rsqrt(jnp.sum(qh * qh, 1, keepdims=True) + RMS_EPS)
    rk = lax.rsqrt(jnp.sum(kh * kh, 1, keepdims=True) + RMS_EPS)
    qn = qh * (rq * (G_HEAD_DIM ** -0.5))
    kn = kh * rk
    gcb = gc[rs]
    gcol = _lane_col(gcb, 4 + h)
    bcol = _lane_col(beta[rs], h)
    grow = _dot_exact_lhs(_nt, jnp.ones((128, 128), BF16), jnp.where(lane == 4 + h, gcb, 0.0))
    D = jnp.where(tril, jnp.exp(jnp.minimum(gcol - grow, 0.0)), 0.0)
    kb = kn * bcol
    vb = vh * bcol
    knb = _b(kn)
    A = _nt(_b(kb), knb)
    Bm = _nt(_b(qn), knb)
    yield
    if tm is None:
        N = jnp.where(strict, -(A * D), 0.0)
        tm = jnp.where(eye, 1.0, 0.0) + N
        npow = N
        for _ in range(5):
            npow = _dot3(_nn, npow, npow)
            yield
            tm = tm + _dot3(_nn, tm, npow)
            yield
    eg = jnp.exp(gcol)
    u = _dot3(_nn, tm, vb)
    w = _dot3(_nn, tm, kb * eg)
    yield
    qk = jnp.where(tril, Bm * D, 0.0)
    qd = qn * eg
    gla = jnp.sum(jnp.where(rowi == 63, gcol, 0.0), 0, keepdims=True)
    glb = jnp.sum(jnp.where(rowi == 127, gcol, 0.0), 0, keepdims=True)
    ed = jnp.exp(jnp.where(rowi < 64, gla, glb) - gcol)
    kd = kn * ed
    return dict(qh=qh, kh=kh, vh=vh, rq=rq, rk=rk, qn=qn, kn=kn, gcol=gcol, bcol=bcol, D=D, A=A, Bm=Bm,
                tm=tm, eg=eg, ed=ed, u=u, w=w, qk=qk, qd=qd, kd=kd, kb=kb, vb=vb,
                gla=jnp.exp(gla), glb=jnp.exp(glb))


def _gdn_scan(q, sa):
    sab = _b(sa)
    wb = _b(q["w"])
    vna = q["u"] - _nn(wb, sab)
    yield
    sb = sa * q["gla"] + _tn(_b(q["kd"][0:64]), _b(vna[0:64]))
    yield
    sbb = _b(sb)
    vnb = q["u"] - _nn(wb, sbb)
    yield
    sn = sb * q["glb"] + _tn(_b(q["kd"][64:128]), _b(vnb[64:128]))
    yield
    vn = jnp.concatenate([vna[0:64], vnb[64:128]], 0)
    qdb = _b(q["qd"])
    o = jnp.concatenate([_nn(qdb[0:64], sab), _nn(qdb[64:128], sbb)], 0) + _nn(_b(q["qk"]), _b(vn))
    return sb, sn, vn, o


def _gdn_param_specs():
    return [pl.BlockSpec((CONV_WIDTH, GW3), lambda i: (0, 0)),
            pl.BlockSpec((8, 128), lambda i: (0, 0)),
            pl.BlockSpec((1, 128), lambda i: (0, 0))]


def _gdn_pvec(a_log, dt_bias):
    z = jnp.zeros((8, 128), F32)
    return z.at[0, 4:8].set(a_log).at[1, 4:8].set(dt_bias)


def _gdn_fwd(proj, cw, a_log, dt_bias, nw, ymix, *, T, name):
    S_ = proj.shape[0]
    nu = T // 128

    def body(x_ref, z_ref, g_ref, cw_ref, pv_ref, nw_ref, _, y_ref, st_ref, tm_ref, halo, state):
        i = pl.program_id(0)

        @pl.when(i == 0)
        def _():
            halo[...] = jnp.zeros_like(halo)
            state[...] = jnp.zeros_like(state)

        x = x_ref[...]
        ext = jnp.concatenate([halo[...], x], 0)
        halo[...] = x[T - 8:]
        _, _, qkv, beta, _, _, _, gc = _gdn_pre(ext, T, cw_ref, g_ref[...], pv_ref)
        items = [(dc, h) for dc in range(nu) for h in range(G_HEADS)]
        qs = _lockstep([_gdn_chunk(qkv, beta, gc, slice(dc * 128, (dc + 1) * 128), h) for dc, h in items])

        def head_chain(h):
            s = state[h]
            for dc in range(nu):
                rs = slice(dc * 128, (dc + 1) * 128)
                q = qs[dc * G_HEADS + h]
                sb, sn, _, o = yield from _gdn_scan(q, s)
                st_ref[2 * dc, h] = s
                st_ref[2 * dc + 1, h] = sb
                tm_ref[dc, h] = q["tm"]
                s = sn
                yield
                rn = lax.rsqrt(jnp.mean(o * o, 1, keepdims=True) + RMS_EPS)
                cs = slice(h * 128, (h + 1) * 128)
                y_ref[rs, cs] = (o * rn * nw_ref[...] * _silu(z_ref[rs, cs])).astype(BF16)
                yield
            state[h] = s

        _lockstep([head_chain(h) for h in range(G_HEADS)])

    return _pcall(
        body, name=name, grid=(S_ // T,),
        in_specs=[pl.BlockSpec((T, GW3), lambda i: (i, OFF_GQKV // GW3)),
                  pl.BlockSpec((T, 512), lambda i: (i, OFF_GZ // 512)),
                  pl.BlockSpec((T, 128), lambda i: (i, OFF_GBA // 128))] + _gdn_param_specs() + [ANY_SPEC],
        out_specs=[pl.BlockSpec((T, 512), lambda i: (i, MIX_G // 512)),
                   pl.BlockSpec((2 * nu, G_HEADS, 128, 128), lambda i: (i, 0, 0, 0)),
                   pl.BlockSpec((nu, G_HEADS, 128, 128), lambda i: (i, 0, 0, 0))],
        out_shape=[jax.ShapeDtypeStruct(ymix.shape, BF16),
                   jax.ShapeDtypeStruct((S_ // 64, G_HEADS, 128, 128), F32),
                   jax.ShapeDtypeStruct((S_ // 128, G_HEADS, 128, 128), F32)],
        input_output_aliases={6: 0},
        scratch_shapes=[pltpu.VMEM((8, GW3), F32), pltpu.VMEM((G_HEADS, 128, 128), F32)],
        compiler_params=_cp(("arbitrary",)),
    )(proj, proj, proj, cw, _gdn_pvec(a_log, dt_bias), nw.reshape(1, 128), ymix)


def _gdn_bwd(proj, states, tms, dymix, dproj, cw, a_log, dt_bias, nw, *, T, name):
    S_ = proj.shape[0]
    nt = S_ // T
    nu = T // 128
    t8 = T // 8

    def body(x_ref, xp_ref, z_ref, g_ref, st_ref, tm_ref, dy_ref, cw_ref, pv_ref, nw_ref, _,
             dp_ref, dg_ref, dcw_ref, dpv_ref, dnw_ref, dstate, dhalo, dqkv, dbg):
        i = pl.program_id(0)
        first_tile = (i == nt - 1)

        @pl.when(i == 0)
        def _():
            dstate[...] = jnp.zeros_like(dstate)
            dhalo[...] = jnp.zeros_like(dhalo)
            dcw_ref[...] = jnp.zeros_like(dcw_ref)
            dpv_ref[...] = jnp.zeros_like(dpv_ref)
            dnw_ref[...] = jnp.zeros_like(dnw_ref)

        keep = jnp.where(first_tile, 0.0, 1.0)
        ext = jnp.concatenate([xp_ref[...] * keep, x_ref[...]], 0)
        G = g_ref[...]
        taps, c, qkv, beta, sarg, nea, gdec, gc = _gdn_pre(ext, T, cw_ref, G, pv_ref)
        tril, strict, _ = _gdn_masks()
        rowi = lax.broadcasted_iota(jnp.int32, (128, 1), 0)
        lane = lax.broadcasted_iota(jnp.int32, (1, 128), 1)
        ones_b = jnp.ones((128, 128), BF16)
        nwv = nw_ref[...]
        items = [(dc, h) for dc in range(nu) for h in range(G_HEADS)]

        def recompute(dc, h):
            q = yield from _gdn_chunk(qkv, beta, gc, slice(dc * 128, (dc + 1) * 128), h, tm=tm_ref[dc, h])
            sa = st_ref[2 * dc, h]
            sb, _, vn, o = yield from _gdn_scan(q, sa)
            return q, sa, sb, vn, o

        fw = _lockstep([recompute(dc, h) for dc, h in items])
        chain_out = {}

        def head_chain(h):
            dS = dstate[h]
            for dc in reversed(range(nu)):
                rs = slice(dc * 128, (dc + 1) * 128)
                q, sa, sb, vn, o = fw[dc * G_HEADS + h]
                cs = slice(h * 128, (h + 1) * 128)
                zg = z_ref[rs, cs]
                dy = dy_ref[rs, cs]
                rn = lax.rsqrt(jnp.mean(o * o, 1, keepdims=True) + RMS_EPS)
                don = dy * _silu(zg)
                dp_ref[rs, GW3 + cs.start:GW3 + cs.stop] = (dy * (o * rn * nwv) * _dsilu(zg)).astype(BF16)
                dnw_ref[...] += jnp.sum(don * o * rn, 0, keepdims=True)
                tt = don * nwv
                do = rn * (tt - o * (rn * rn) * jnp.mean(tt * o, 1, keepdims=True))
                yield
                dob = _b(do)
                sab, sbb = _b(sa), _b(sb)
                vnb16 = _b(vn)
                dqk = jnp.where(tril, _nt(dob, vnb16), 0.0)
                dvn_o = _tn(_b(q["qk"]), dob)
                dSb16 = _b(dS)
                kdb = _b(q["kd"])
                wb = _b(q["w"])
                qdb = _b(q["qd"])
                yield
                dvn_b = dvn_o[64:128] + _nn(kdb[64:128], dSb16)
                dkd_b = _nt(vnb16[64:128], dSb16)
                dgl_b = jnp.sum(jnp.sum(dS * sb, 1, keepdims=True), 0, keepdims=True)
                yield
                dvn_b16 = _b(dvn_b)
                dw_b = -_nt(dvn_b16, sbb)
                dqd_b = _nt(dob[64:128], sbb)
                dSm = q["glb"] * dS + _tn(qdb[64:128], dob[64:128]) - _tn(wb[64:128], dvn_b16)
                yield
                dSm16 = _b(dSm)
                dvn_a = dvn_o[0:64] + _nn(kdb[0:64], dSm16)
                dkd_a = _nt(vnb16[0:64], dSm16)
                dgl_a = jnp.sum(jnp.sum(dSm * sa, 1, keepdims=True), 0, keepdims=True)
                yield
                dvn_a16 = _b(dvn_a)
                dw_a = -_nt(dvn_a16, sab)
                dqd_a = _nt(dob[0:64], sab)
                dS = q["gla"] * dSm + _tn(qdb[0:64], dob[0:64]) - _tn(wb[0:64], dvn_a16)
                chain_out[dc, h] = (dqk, jnp.concatenate([dvn_a, dvn_b], 0), jnp.concatenate([dw_a, dw_b], 0),
                                    jnp.concatenate([dkd_a, dkd_b], 0), jnp.concatenate([dqd_a, dqd_b], 0),
                                    dgl_a, dgl_b)
                yield
            dstate[h] = dS

        _lockstep([head_chain(h) for h in range(G_HEADS)])

        def local(dc, h):
            rs = slice(dc * 128, (dc + 1) * 128)
            q = fw[dc * G_HEADS + h][0]
            dqk, du, dw, dkd, dqd, dgl_a, dgl_b = chain_out[dc, h]
            if True:
                dvb = _dot3(_tn, q["tm"], du)
                dkbe = _dot3(_tn, q["tm"], dw)
                yield
                dM = jnp.where(strict, -(_nt(_b(dvb), _b(q["u"])) + _nt(_b(dkbe), _b(q["w"]))), 0.0)
                yield
                D = q["D"]
                dA = dM * D
                dB = dqk * D
                dDD = (dM * q["A"] + dqk * q["Bm"]) * D
                dh_, dm_, dl_ = _split3(dDD)
                colsum = _tn(dh_, ones_b) + (_tn(dm_, ones_b) + _tn(dl_, ones_b))
                dgc = jnp.sum(dDD, 1, keepdims=True) - _lane_col(colsum, 0)
                yield
                dA16, dB16 = _b(dA), _b(dB)
                knb, kbb, qnb = _b(q["kn"]), _b(q["kb"]), _b(q["qn"])
                eg, ed = q["eg"], q["ed"]
                dkb = _nn(dA16, knb) + dkbe * eg
                dkn = _tn(dA16, kbb) + _tn(dB16, qnb) + dkd * ed + dkb * q["bcol"]
                dqn = _nn(dB16, knb) + dqd * eg
                yield
                deg = jnp.sum(dkbe * q["kb"], 1, keepdims=True) + jnp.sum(dqd * q["qn"], 1, keepdims=True)
                ded = jnp.sum(dkd * q["kn"], 1, keepdims=True) * ed
                dgc = dgc + deg * eg - ded
                tail_a = jnp.sum(jnp.where(rowi < 64, ded, 0.0), 0, keepdims=True) + dgl_a * q["gla"]
                tail_b = jnp.sum(jnp.where(rowi >= 64, ded, 0.0), 0, keepdims=True) + dgl_b * q["glb"]
                dgc = dgc + jnp.where(rowi == 63, tail_a, 0.0) + jnp.where(rowi == 127, tail_b, 0.0)
                dbeta = jnp.sum(dkb * q["kn"], 1, keepdims=True) + jnp.sum(dvb * q["vh"], 1, keepdims=True)
                bcol = q["bcol"]
                blk = jnp.where(lane == h, dbeta * bcol * (1.0 - bcol), 0.0) + jnp.where(lane == 4 + h, dgc, 0.0)
                yield
                sc = G_HEAD_DIM ** -0.5
                rq, rk, qh, kh = q["rq"], q["rk"], q["qh"], q["kh"]
                dqh = sc * (dqn * rq - qh * (rq * rq * rq) * jnp.sum(dqn * qh, 1, keepdims=True))
                dkh = dkn * rk - kh * (rk * rk * rk) * jnp.sum(dkn * kh, 1, keepdims=True)
                dqkv[rs, h * 128:(h + 1) * 128] = dqh
                dqkv[rs, 512 + h * 128:512 + (h + 1) * 128] = dkh
                dqkv[rs, 1024 + h * 128:1024 + (h + 1) * 128] = dvb * bcol
            return blk

        blks = _lockstep([local(dc, h) for dc, h in items])
        for dc in range(nu):
            dbg[dc * 128:(dc + 1) * 128, :] = functools.reduce(
                lambda a, b: a + b, [blks[dc * G_HEADS + h] for h in range(G_HEADS)])
        ri = lax.broadcasted_iota(jnp.int32, (T, T), 0)
        cj = lax.broadcasted_iota(jnp.int32, (T, T), 1)
        utri = jnp.where((ri <= cj) & ((ri >> 6) == (cj >> 6)), 1.0, 0.0).astype(BF16)
        dbgv = dbg[...]
        dgd = _dot_exact_lhs(_nn, utri, dbgv)
        is_g = (lane >= 4) & (lane < 8)
        dga = jnp.where(is_g, dgd * nea * _sigmoid(sarg), 0.0)
        dg_ref[...] = jnp.where(lane < 4, dbgv, dga).astype(BF16)
        dpv_ref[0:1, :] += jnp.sum(jnp.where(is_g, dgd * gdec, 0.0), 0, keepdims=True)
        dpv_ref[1:2, :] += jnp.sum(dga, 0, keepdims=True)
        dc_ = dqkv[...] * _dsilu(c)
        for k in range(CONV_WIDTH):
            dcw_ref[k:k + 1, :] += jnp.sum(dc_ * taps[k], 0, keepdims=True)
        ext2 = jnp.concatenate([dc_, dhalo[...]], 0)
        tt2 = _conv_taps_t(ext2, T)
        dp_ref[:, 0:GW3] = sum(cw_ref[k:k + 1, :] * tt2[k] for k in range(CONV_WIDTH)).astype(BF16)
        dhalo[...] = dc_[0:8]

    def rev(i):
        return nt - 1 - i

    def prev8(i):
        return jnp.maximum(rev(i) * t8 - 1, 0)

    return _pcall(
        body, name=name, grid=(nt,),
        in_specs=[pl.BlockSpec((T, GW3), lambda i: (rev(i), OFF_GQKV // GW3)),
                  pl.BlockSpec((8, GW3), lambda i: (prev8(i), OFF_GQKV // GW3)),
                  pl.BlockSpec((T, 512), lambda i: (rev(i), OFF_GZ // 512)),
                  pl.BlockSpec((T, 128), lambda i: (rev(i), OFF_GBA // 128)),
                  pl.BlockSpec((2 * nu, G_HEADS, 128, 128), lambda i: (rev(i), 0, 0, 0)),
                  pl.BlockSpec((nu, G_HEADS, 128, 128), lambda i: (rev(i), 0, 0, 0)),
                  pl.BlockSpec((T, 512), lambda i: (rev(i), MIX_G // 512))] + _gdn_param_specs() + [ANY_SPEC],
        out_specs=[pl.BlockSpec((T, GW3 + 512), lambda i: (rev(i), OFF_GQKV // (GW3 + 512))),
                   pl.BlockSpec((T, 128), lambda i: (rev(i), 0)),
                   pl.BlockSpec((CONV_WIDTH, GW3), lambda i: (0, 0)),
                   pl.BlockSpec((8, 128), lambda i: (0, 0)),
                   pl.BlockSpec((1, 128), lambda i: (0, 0))],
        out_shape=[jax.ShapeDtypeStruct(dproj.shape, BF16),
                   jax.ShapeDtypeStruct((S_, 128), BF16), jax.ShapeDtypeStruct((CONV_WIDTH, GW3), F32),
                   jax.ShapeDtypeStruct((8, 128), F32), jax.ShapeDtypeStruct((1, 128), F32)],
        input_output_aliases={10: 0},
        scratch_shapes=[pltpu.VMEM((G_HEADS, 128, 128), F32), pltpu.VMEM((8, GW3), F32),
                        pltpu.VMEM((T, GW3), F32), pltpu.VMEM((T, 128), F32)],
        compiler_params=_cp(("arbitrary",)),
    )(proj, proj, proj, proj, states, tms, dymix, cw, _gdn_pvec(a_log, dt_bias), nw.reshape(1, 128), dproj)


def _pair_sum_windows(a, b, nsh, width, *, out_dtype, name):
    R_, C = a.shape
    hr = R_ // 2
    nb = width // 128
    assert (3 * nsh) // 128 + nb <= C // 128
    to_perm = _orig_block_to_perm()
    table = jnp.asarray([to_perm[(nsh * t) // 128 + j] for t in range(4) for j in range(nb)], jnp.int32)

    def body(tab_ref, a0_ref, a1_ref, b_ref, o_ref):
        mine = jnp.where(lax.axis_index("c") == 0, a0_ref[...], a1_ref[...])
        o_ref[...] = (mine + b_ref[...]).astype(o_ref.dtype)

    def spec(half):
        return pl.BlockSpec((hr, 128), lambda t, j, tab: (half, tab[t * nb + j]))

    return _pcall(
        body, name=name,
        grid_spec=pltpu.PrefetchScalarGridSpec(
            num_scalar_prefetch=1, grid=(4, nb), in_specs=[spec(0), spec(1), spec(0)],
            out_specs=pl.BlockSpec((None, hr, 128), lambda t, j, tab: (t, 0, j))),
        out_shape=jax.ShapeDtypeStruct((4, hr, width), out_dtype),
        compiler_params=_cp(("parallel", "parallel")))(table, a, a, b)


def _pair_sum_blocks(a, b, *, out_dtype, name):
    L, R_, C = a.shape
    hr = R_ // 2

    def body(a0_ref, a1_ref, b_ref, o_ref):
        mine = jnp.where(lax.axis_index("c") == 0, a0_ref[...], a1_ref[...])
        o_ref[...] = (mine + b_ref[...]).astype(o_ref.dtype)

    def spec(half):
        return pl.BlockSpec((None, hr, C), lambda t: (t, half, 0))

    return _pcall(body, name=name, grid=(L,), in_specs=[spec(0), spec(1), spec(0)], out_specs=spec(0),
                  out_shape=jax.ShapeDtypeStruct((L, hr, C), out_dtype),
                  compiler_params=_cp(("parallel",)))(a, a, b)


def _add_mine(a0, a1, b, *, out_dtype, tr, name):
    R_, C = b.shape

    def body(a0_ref, a1_ref, b_ref, o_ref):
        mine = jnp.where(lax.axis_index("c") == 0, a0_ref[...], a1_ref[...])
        o_ref[...] = (mine + b_ref[...]).astype(o_ref.dtype)

    spec = pl.BlockSpec((tr, C), lambda i: (i, 0))
    return _pcall(body, name=name, grid=(R_ // tr,), in_specs=[spec] * 3, out_specs=spec,
                  out_shape=jax.ShapeDtypeStruct((R_, C), out_dtype), compiler_params=_cp(("parallel",)))(a0, a1, b)


def _sum4(a, mine, *, tr, name):
    _, R_, C = a.shape

    def body(a_ref, m_ref, o_ref):
        s = 2 * lax.axis_index("x") + lax.axis_index("y")
        mv = m_ref[...].astype(F32)
        p = [jnp.where(s == t, mv, a_ref[t].astype(F32)) for t in range(4)]
        o_ref[...] = ((p[0] + p[1]) + p[2]) + p[3]

    return _pcall(body, name=name, grid=(R_ // tr,),
                  in_specs=[pl.BlockSpec((4, tr, C), lambda i: (0, i, 0)), pl.BlockSpec((tr, C), lambda i: (i, 0))],
                  out_specs=pl.BlockSpec((tr, C), lambda i: (i, 0)),
                  out_shape=jax.ShapeDtypeStruct((R_, C), F32), compiler_params=_cp(("parallel",)))(a, mine)


def _adamw_refs(w_ref, g_ref, m_ref, v_ref, d_ref, mo_ref, vo_ref):
    c1 = 1.0 / (1.0 - ADAM_B1 ** ADAM_STEP)
    c2 = 1.0 / (1.0 - ADAM_B2 ** ADAM_STEP)
    gg = g_ref[...]
    mn = ADAM_B1 * m_ref[...] + (1.0 - ADAM_B1) * gg
    vn = ADAM_B2 * v_ref[...] + (1.0 - ADAM_B2) * (gg * gg)
    mo_ref[...] = mn
    vo_ref[...] = vn
    d_ref[...] = -ADAM_LR * ((mn * c1) / (jnp.sqrt(vn * c2) + ADAM_EPS) + ADAM_WD * w_ref[...])


def _adamw_many(ws, gs, ms, vs, *, name):
    n = len(ws)

    def body(*refs):
        for k in range(n):
            _adamw_refs(*[refs[q * n + k] for q in range(7)])

    vm = pl.BlockSpec(memory_space=pltpu.VMEM)
    shp = [jax.ShapeDtypeStruct(w.shape, F32) for w in ws]
    outs = _pcall(body, name=name, in_specs=[vm] * (4 * n), out_specs=[vm] * (3 * n), out_shape=shp * 3,
                  compiler_params=pltpu.CompilerParams(vmem_limit_bytes=VMEM_LIMIT))(*ws, *gs, *ms, *vs)
    return outs[:n], outs[n:2 * n], outs[2 * n:]


def _adamw(w, g, m, v, *, tr, name):
    L, R_, C = w.shape
    body = functools.partial(_adamw_refs)

    spec = pl.BlockSpec((None, tr, C), lambda l, i: (l, i, 0))
    shp = jax.ShapeDtypeStruct((L, R_, C), F32)
    return _pcall(body, name=name, grid=(L, R_ // tr), in_specs=[spec] * 4, out_specs=[spec] * 3,
                  out_shape=[shp] * 3, compiler_params=_cp(("parallel", "parallel")))(w, g, m, v)


def _adamw_cols(w, g, m, v, *, name):
    C, L, R_ = w.shape
    tc = C // 2 if C % 2 == 0 else C

    spec = pl.BlockSpec((tc, L, 128), lambda i, j: (i, 0, j))
    shp = jax.ShapeDtypeStruct((C, L, R_), F32)
    return _pcall(functools.partial(_adamw_refs), name=name, grid=(C // tc, R_ // 128), in_specs=[spec] * 4,
                  out_specs=[spec] * 3, out_shape=[shp] * 3,
                  compiler_params=_cp(("parallel", "parallel")))(w, g, m, v)


HBM_SPEC = pl.BlockSpec(memory_space=pltpu.HBM)


def _place():
    x, y, c = lax.axis_index("x"), lax.axis_index("y"), lax.axis_index("c")
    chips = [(1 - x, y), (x, 1 - y), (1 - x, 1 - y)]
    return x, y, c, 2 * x + y, chips, [2 * cx + cy for cx, cy in chips], (x, y, 1 - c)


def _remote(src, dst, ssem, rsem, dev):
    return pltpu.make_async_remote_copy(src_ref=src, dst_ref=dst, send_sem=ssem, recv_sem=rsem,
                                        device_id=dev, device_id_type=MESH)


def _row_half(ref, lead, hc):
    hl = ref.shape[-2] // 2
    return ref.at[lead, pl.ds(hc * hl, hl), :]


def _gather_side(win, wout, layer):
    def copies(ins, outs, ssem, rsem):
        x, y, c, s, chips, sid, sib = _place()
        cps = []
        for j, chip in enumerate(chips):
            dev = (*chip, c)
            cps.append(_remote(_row_half(ins[0], layer, c), _row_half(outs[0], s, c), ssem.at[j], rsem.at[j], dev))
            cps.append(_remote(_row_half(ins[1], layer, c), _row_half(outs[1], s, c), ssem.at[3 + j], rsem.at[3 + j],
                               dev))
        return cps, c, sid, sib

    def start(ins, outs, ssem, rsem):
        for cp in copies(ins, outs, ssem, rsem)[0]:
            cp.start()

    def finish(ins, outs, ssem, rsem):
        cps, c, sid, sib = copies(ins, outs, ssem, rsem)
        for j in range(3):
            for k in range(2):
                got = _row_half(outs[k], sid[j], c)
                _remote(got, got, ssem.at[3 * k + j], rsem.at[3 * k + j], sib).wait_recv()
        for cp in cps:
            cp.wait_send()

    shapes = [jax.ShapeDtypeStruct((4,) + win.shape[1:], win.dtype), jax.ShapeDtypeStruct((4,) + wout.shape[1:], wout.dtype)]
    return _Side([win, wout], shapes, 6, start, finish)


def _gather_join(gin, gout, name):
    def body(gin_in, gout_in, gin_ref, gout_ref, ssem, rsem):
        x, y, c, s, chips, sid, sib = _place()
        cps = []
        for j in range(3):
            for k, ref in enumerate((gin_ref, gout_ref)):
                mine = _row_half(ref, sid[j], c)
                cps.append(_remote(mine, mine, ssem.at[3 * k + j], rsem.at[3 * k + j], sib))
        for cp in cps:
            cp.start()
        for j in range(3):
            for k, ref in enumerate((gin_ref, gout_ref)):
                other = _row_half(ref, sid[j], 1 - c)
                _remote(other, other, ssem.at[3 * k + j], rsem.at[3 * k + j], sib).wait_recv()
        for cp in cps:
            cp.wait_send()

    return _pcall(
        body, name=name, in_specs=[HBM_SPEC] * 2, out_specs=[HBM_SPEC] * 2,
        out_shape=[jax.ShapeDtypeStruct(gin.shape, gin.dtype), jax.ShapeDtypeStruct(gout.shape, gout.dtype)],
        input_output_aliases={0: 0, 1: 1},
        scratch_shapes=[pltpu.SemaphoreType.DMA((6,)), pltpu.SemaphoreType.DMA((6,))],
    )(gin, gout)


def _gather_layer0(win, wout, conv):
    def body(win_ref, wout_ref, cv_ref, gin_ref, gout_ref, gcv_ref, ssem, rsem):
        x, y, c, s, chips, sid, sib = _place()

        def in_half(slot, hc):
            return _row_half(gin_ref, slot, hc)

        def out_half(slot, hc):
            return _row_half(gout_ref, slot, hc)

        sends = []
        for j, chip in enumerate(chips):
            dev = (*chip, c)
            sends.append(_remote(_row_half(win_ref, 0, c), in_half(s, c), ssem.at[j], rsem.at[j], dev))
            sends.append(_remote(_row_half(wout_ref, 0, c), out_half(s, c), ssem.at[3 + j], rsem.at[3 + j], dev))
            sends.append(_remote(cv_ref, gcv_ref.at[s], ssem.at[6 + j], rsem.at[6 + j], dev))
        for cp in sends:
            cp.start()
        for j in range(3):
            _remote(in_half(sid[j], c), in_half(sid[j], c), ssem.at[j], rsem.at[j], sib).wait_recv()
            f = _remote(in_half(sid[j], c), in_half(sid[j], c), ssem.at[9 + j], rsem.at[9 + j], sib)
            f.start()
            sends.append(f)
            _remote(out_half(sid[j], c), out_half(sid[j], c), ssem.at[3 + j], rsem.at[3 + j], sib).wait_recv()
            f = _remote(out_half(sid[j], c), out_half(sid[j], c), ssem.at[12 + j], rsem.at[12 + j], sib)
            f.start()
            sends.append(f)
        for j in range(3):
            _remote(in_half(sid[j], 1 - c), in_half(sid[j], 1 - c), ssem.at[9 + j], rsem.at[9 + j], sib).wait_recv()
            _remote(out_half(sid[j], 1 - c), out_half(sid[j], 1 - c), ssem.at[12 + j], rsem.at[12 + j], sib).wait_recv()
            _remote(gcv_ref.at[sid[j]], gcv_ref.at[sid[j]], ssem.at[6 + j], rsem.at[6 + j], sib).wait_recv()
        for cp in sends:
            cp.wait_send()

    return _pcall(
        body, name="gather_layer0",
        in_specs=[HBM_SPEC] * 3, out_specs=[HBM_SPEC] * 3,
        out_shape=[jax.ShapeDtypeStruct((4,) + win.shape[1:], win.dtype),
                   jax.ShapeDtypeStruct((4,) + wout.shape[1:], wout.dtype),
                   jax.ShapeDtypeStruct((4,) + conv.shape, conv.dtype)],
        scratch_shapes=[pltpu.SemaphoreType.DMA((15,)), pltpu.SemaphoreType.DMA((15,))],
    )(win, wout, conv)


def _swap_halves(arrs, axes, name):
    n = len(arrs)

    def half_shape(a, ax):
        return a.shape[:ax] + (a.shape[ax] // 2,) + a.shape[ax + 1:]

    def body(*refs):
        src, dst, ssem, rsem = refs[:n], refs[n:2 * n], refs[2 * n], refs[2 * n + 1]
        x, y, c, s, chips, sid, sib = _place()
        cps = []
        for k in range(n):
            hl = src[k].shape[axes[k]] // 2
            idx = [slice(None)] * len(src[k].shape)
            idx[axes[k]] = pl.ds((1 - c) * hl, hl)
            cps.append(_remote(src[k].at[tuple(idx)], dst[k], ssem.at[k], rsem.at[k], sib))
        for cp in cps:
            cp.start()
        for cp in cps:
            cp.wait()

    return _pcall(
        body, name=name, in_specs=[HBM_SPEC] * n, out_specs=[HBM_SPEC] * n,
        out_shape=[jax.ShapeDtypeStruct(half_shape(a, ax), a.dtype) for a, ax in zip(arrs, axes)],
        scratch_shapes=[pltpu.SemaphoreType.DMA((n,)), pltpu.SemaphoreType.DMA((n,))],
    )(*arrs)


def _swap_side(arrs, axes):
    n = len(arrs)

    def copies(ins, outs, ssem, rsem):
        x, y, c, s, chips, sid, sib = _place()
        cps = []
        for k in range(n):
            hl = ins[k].shape[axes[k]] // 2
            idx = [slice(None)] * len(ins[k].shape)
            idx[axes[k]] = pl.ds((1 - c) * hl, hl)
            cps.append(_remote(ins[k].at[tuple(idx)], outs[k], ssem.at[k], rsem.at[k], sib))
        return cps

    def start(ins, outs, ssem, rsem):
        for cp in copies(ins, outs, ssem, rsem):
            cp.start()

    def finish(ins, outs, ssem, rsem):
        for cp in copies(ins, outs, ssem, rsem):
            cp.wait()

    shapes = [jax.ShapeDtypeStruct(a.shape[:ax] + (a.shape[ax] // 2,) + a.shape[ax + 1:], a.dtype)
              for a, ax in zip(arrs, axes)]
    return _Side(list(arrs), shapes, n, start, finish)


def _chips_side(arrs, per_target):
    n = len(arrs)

    def copies(ins, outs, ssem, rsem):
        x, y, c, s, chips, sid, sib = _place()
        cps = [_remote(ins[k].at[sid[j]] if per_target[k] else ins[k], outs[k].at[s],
                       ssem.at[3 * k + j], rsem.at[3 * k + j], (*chip, c))
               for k in range(n) for j, chip in enumerate(chips)]
        return cps, sid, sib

    def start(ins, outs, ssem, rsem):
        for cp in copies(ins, outs, ssem, rsem)[0]:
            cp.start()

    def finish(ins, outs, ssem, rsem):
        cps, sid, sib = copies(ins, outs, ssem, rsem)
        for k in range(n):
            for j in range(3):
                got = outs[k].at[sid[j]]
                _remote(got, got, ssem.at[3 * k + j], rsem.at[3 * k + j], sib).wait_recv()
        for cp in cps:
            cp.wait_send()

    shapes = [jax.ShapeDtypeStruct(a.shape if pt else (4,) + a.shape, a.dtype) for a, pt in zip(arrs, per_target)]
    return _Side(list(arrs), shapes, 3 * n, start, finish)


def _scatter_chips(arrs, per_target, name):
    n = len(arrs)

    def body(*refs):
        src, dst = refs[:n], refs[n:2 * n]
        ssem, rsem = refs[2 * n], refs[2 * n + 1]
        x, y, c, s, chips, sid, sib = _place()
        sends = []
        for k in range(n):
            for j, chip in enumerate(chips):
                piece = src[k].at[sid[j]] if per_target[k] else src[k]
                sends.append(_remote(piece, dst[k].at[s], ssem.at[3 * k + j], rsem.at[3 * k + j], (*chip, c)))
        for cp in sends:
            cp.start()
        for k in range(n):
            for j in range(3):
                _remote(dst[k].at[sid[j]], dst[k].at[sid[j]], ssem.at[3 * k + j], rsem.at[3 * k + j], sib).wait_recv()
        for cp in sends:
            cp.wait_send()

    outs = [jax.ShapeDtypeStruct(a.shape if pt else (4,) + a.shape, a.dtype) for a, pt in zip(arrs, per_target)]
    return _pcall(
        body, name=name, in_specs=[HBM_SPEC] * n, out_specs=[HBM_SPEC] * n, out_shape=outs,
        scratch_shapes=[pltpu.SemaphoreType.DMA((3 * n,)), pltpu.SemaphoreType.DMA((3 * n,))],
    )(*arrs)


def _swap_whole(arrs, name):
    n = len(arrs)

    def body(*refs):
        src, dst, ssem, rsem = refs[:n], refs[n:2 * n], refs[2 * n], refs[2 * n + 1]
        *_, sib = _place()
        cps = [_remote(src[k], dst[k], ssem.at[k], rsem.at[k], sib) for k in range(n)]
        for cp in cps:
            cp.start()
        for cp in cps:
            cp.wait()

    return _pcall(
        body, name=name, in_specs=[HBM_SPEC] * n, out_specs=[HBM_SPEC] * n,
        out_shape=[jax.ShapeDtypeStruct(a.shape, a.dtype) for a in arrs],
        scratch_shapes=[pltpu.SemaphoreType.DMA((n,)), pltpu.SemaphoreType.DMA((n,))],
    )(*arrs)


def _perm_cols(w):
    parts = [w[..., int(_ORIG_OFF[oi]):int(_ORIG_OFF[oi]) + IN_SIZES[oi]] for oi, _ in _PIECES]
    parts.append(jnp.zeros(w.shape[:-1] + (NP - N_IN,), w.dtype))
    return jnp.concatenate(parts, -1)


def _perm_rows(w):
    return jnp.concatenate([w[..., 512:1536, :], w[..., 0:512, :], w[..., 1536:2048, :]], -2)


_SMALL = ("sinks", "r_conv_b", "r_wa", "r_ba", "r_wx", "r_bx", "r_lam", "g_a_log", "g_dt_bias", "g_norm_w",
          "ln_g", "ln_b", "r_conv_w", "g_conv_w")
_PACK_ROWS = 16


def _piece_rows(n):
    return -(-n // (128 * _PACK_ROWS)) * _PACK_ROWS


def _pack(arrs):
    parts = []
    for a in arrs:
        n = int(np.prod(a.shape))
        rows = _piece_rows(n)
        if n % 128 == 0:
            blk = a.reshape(n // 128, 128)
        else:
            blk = jnp.pad(a.reshape(1, n), ((0, 0), (0, (-n) % 128))).reshape(-1, 128)
        if blk.shape[0] < rows:
            blk = jnp.pad(blk, ((0, rows - blk.shape[0]), (0, 0)))
        parts.append(blk)
    return jnp.concatenate(parts, 0)


def _unpack(packed, shapes):
    out = []
    r = 0
    for shp in shapes:
        n = int(np.prod(shp))
        if n % 128 == 0:
            out.append(packed[r:r + n // 128].reshape(shp))
        else:
            nr = -(-n // 128)
            out.append(packed[r:r + nr].reshape(1, nr * 128)[:, :n].reshape(shp))
        r += _piece_rows(n)
    return out


def _tile(n, t):
    return min(n, t)


def _layer_fwd(l, x, xb, wb, wob, ln, rope_c, rope_s, p, side=None, target=None):
    S_ = x.shape[0]
    proj = _matmul(xb, wb, ta=False, tb=False, tm=_tile(S_, 1024), tn=NP // 4, tk=wb.shape[0], out_dtype=F32,
                   name=f"in_proj_{l}", side=side)
    side_out = None
    if side:
        proj, side_out = proj
    h, ymix = _rglru_fwd(proj, p["r_conv_w"], p["r_conv_b"], p["r_wa"], p["r_ba"], p["r_wx"], p["r_bx"], p["r_lam"],
                         T=_tile(S_, 256), name=f"rglru_fwd_{l}")
    ymix = _attn_fwd(proj, rope_c, rope_s, p["sinks"], ymix, T=_tile(S_, 512), name=f"attn_fwd_{l}")
    ymix, st, tms = _gdn_fwd(proj, p["g_conv_w"], p["g_a_log"], p["g_dt_bias"], p["g_norm_w"], ymix,
                             T=_tile(S_, 256), name=f"gdn_fwd_{l}")
    out = _outproj(ymix, wob, x, ln[0], ln[1], tm=_tile(S_, 256), name=f"out_proj_{l}", target=target)
    sv = dict(proj=proj, h=h, st=st, tms=tms, ymix=ymix, side=side_out)
    if target is None:
        sv["z"], sv["y"], sv["yb"] = out
    else:
        sv["head"] = out
    return sv


def _layer_bwd(l, sv, x_b, dz, dzb, wb, wob, rope_c, rope_s, p, side_dmix=None, side_dw_in=None, side_dx=None):
    S_, D = dz.shape
    proj = sv["proj"]
    dwo = _matmul(sv["ymix"], dzb, ta=True, tb=False, tm=512, tn=_tile(D, 2048), tk=_tile(S_, 1024),
                  out_dtype=F32, name=f"dw_out_{l}",
                  out_blocks=((MIX_WIDTH, D), (512, _tile(D, 2048)),
                              lambda i, j: (jnp.where(i == 3, 3, (i + 1) % 3), j)))
    side = side_dmix(dwo) if side_dmix else None
    dymix = _matmul(dzb, wob, ta=False, tb=True, tm=_tile(S_, 1024), tn=512, tk=D, out_dtype=F32,
                    name=f"dmix_{l}", side=side)
    out_dmix = None
    if side:
        dymix, out_dmix = dymix
    dproj, dk, dv, dkt, dvt, dsk = _attn_bwd(proj, rope_c, rope_s, p["sinks"], dymix, T=_tile(S_, 512),
                                             name=f"attn_bwd_{l}")
    (dproj, dcw_r, dcb_r, dwa, dba, dwx, dbx, dlam) = _rglru_bwd(
        proj, sv["h"], dymix, dproj, p["r_conv_w"], p["r_conv_b"], p["r_wa"], p["r_ba"], p["r_wx"], p["r_bx"],
        p["r_lam"], T=_tile(S_, 256), name=f"rglru_bwd_{l}")
    dproj, dgba, dcw_g, dpv, dnw = _gdn_bwd(proj, sv["st"], sv["tms"], dymix, dproj, p["g_conv_w"], p["g_a_log"],
                                            p["g_dt_bias"], p["g_norm_w"], T=_tile(S_, 256), name=f"gdn_bwd_{l}")
    tail = jnp.concatenate([dk[128:], dkt, dv[128:], dvt], 0).reshape(2, S_, 128)
    tail = jnp.concatenate([tail[0], tail[1], dgba, jnp.zeros((S_, NP - OFF_GBA - 128), BF16)], 1)
    dproj = lax.dynamic_update_slice(dproj, tail, (0, OFF_AK))
    small = dict(sinks=dsk[:, 0], r_conv_b=dcb_r[0], r_wa=dwa, r_ba=dba[0], r_wx=dwx, r_bx=dbx[0], r_lam=dlam[0],
                 g_a_log=dpv[0, 4:8], g_dt_bias=dpv[1, 4:8], g_norm_w=dnw[0], r_conv_w=dcw_r, g_conv_w=dcw_g)
    side = side_dw_in(small, dwo, out_dmix) if side_dw_in else None
    dwin = _matmul(x_b, dproj, ta=True, tb=False, tm=_tile(D, 1024), tn=NP // 4, tk=_tile(S_, 1024), out_dtype=F32,
                   name=f"dw_in_{l}", side=side)
    out_dw_in = None
    if side:
        dwin, out_dw_in = dwin
    side = side_dx(dwin) if side_dx else None
    tmx = _tile(S_, 1024)
    nblk = S_ // tmx
    dx_args = dict(ta=False, tb=True, tm=tmx, tn=_tile(D, 1024), tk=NP // 2, out_dtype=F32, extra=dz,
                   alpha=DEEPNORM_ALPHA)
    out_dx = None
    if side and nblk >= 4:
        head = nblk - nblk // 4
        dx, out_dx = _matmul(dproj, wb, name=f"dx_{l}", side=side, rows=(0, head), **dx_args)
        dx = _matmul(dproj, wb, name=f"dx_{l}_rest", rows=(head, nblk - head), into=dx, **dx_args)
    elif side:
        dx, out_dx = _matmul(dproj, wb, name=f"dx_{l}", side=side, **dx_args)
    else:
        dx = _matmul(dproj, wb, name=f"dx_{l}", **dx_args)
    return dx, dwin, dwo, small, out_dw_in, out_dx


def kernel(x, w_in, sinks, r_conv_w, r_conv_b, r_wa, r_ba, r_wx, r_bx, r_lam, g_conv_w, g_a_log, g_dt_bias, g_norm_w, w_out, ln_g, ln_b, loss_target, m_w_in, m_sinks, m_r_conv_w, m_r_conv_b, m_r_wa, m_r_ba, m_r_wx, m_r_bx, m_r_lam, m_g_conv_w, m_g_a_log, m_g_dt_bias, m_g_norm_w, m_w_out, m_ln_g, m_ln_b, v_w_in, v_sinks, v_r_conv_w, v_r_conv_b, v_r_wa, v_r_ba, v_r_wx, v_r_bx, v_r_lam, v_g_conv_w, v_g_a_log, v_g_dt_bias, v_g_norm_w, v_w_out, v_ln_g, v_ln_b):
    S_, D = x.shape[1], x.shape[2]
    nsh = w_in.shape[2]
    rsh = w_out.shape[1]
    cx, cy, cc = lax.axis_index("x"), lax.axis_index("y"), lax.axis_index("c")
    chip = 2 * cx + cy
    rcw_n, gcw_n = r_conv_w.shape[2], g_conv_w.shape[2]

    conv_pack = jnp.concatenate([r_conv_w, g_conv_w], 2)
    w_in_b, w_out_b = w_in.astype(BF16), w_out.astype(BF16)
    g_in0, g_out0, g_conv = _gather_layer0(w_in_b, w_out_b, conv_pack)

    def shards(own, got):
        return [jnp.where(chip == t, own, got[t]) for t in range(4)]

    def layer_weights(l, g_in, g_out):
        w_full = jnp.concatenate(shards(w_in_b[l], g_in), 1)
        return (_perm_cols(w_full),
                _perm_rows(jnp.concatenate(shards(w_out_b[l], g_out), 0)))

    rcw = jnp.concatenate(shards(r_conv_w, g_conv[:, :, :, :rcw_n]), 2)
    gcw = jnp.concatenate(shards(g_conv_w, g_conv[:, :, :, rcw_n:]), 2)

    pos = jnp.arange(S_, dtype=F32)[:, None]
    inv = 1.0 / (ROPE_THETA ** (jnp.arange(0, A_HEAD_DIM, 2, dtype=F32) / A_HEAD_DIM))
    ang = pos * inv[None, :]
    cos, sin = jnp.cos(ang), jnp.sin(ang)
    rope_c = jnp.concatenate([cos, cos, cos, cos], 1)
    rope_s = jnp.concatenate([-sin, sin, -sin, sin], 1)

    def params(l):
        return dict(sinks=sinks[l], r_conv_w=rcw[l], r_conv_b=r_conv_b[l], r_wa=r_wa[l], r_ba=r_ba[l],
                    r_wx=r_wx[l], r_bx=r_bx[l], r_lam=r_lam[l], g_conv_w=gcw[l], g_a_log=g_a_log[l],
                    g_dt_bias=g_dt_bias[l], g_norm_w=g_norm_w[l])

    xs, xbs, saved = [x[0]], [x[0].astype(BF16)], []
    wb, wob = [None] * DEPTH, [None] * DEPTH
    wb[0], wob[0] = layer_weights(0, g_in0, g_out0)
    for l in range(DEPTH):
        nxt = _gather_side(w_in_b, w_out_b, l + 1) if l + 1 < DEPTH else None
        sv = _layer_fwd(l, xs[l], xbs[l], wb[l], wob[l], (ln_g[l], ln_b[l]), rope_c, rope_s, params(l), side=nxt,
                        target=None if l + 1 < DEPTH else loss_target[0])
        if nxt:
            wb[l + 1], wob[l + 1] = layer_weights(l + 1, *_gather_join(*sv["side"], f"gather_join_{l + 1}"))
        saved.append(sv)
        if l + 1 < DEPTH:
            xs.append(sv["y"])
            xbs.append(sv["yb"])

    tm_ln = _tile(S_, 256)
    dz, dzb, dg_l, db_l, loss_part = saved[-1]["head"]
    assert DEPTH == 2
    wcov = (-(-nsh // 128) + 1) * 128
    names = list(_SMALL)

    def own(a):
        return lax.dynamic_index_in_dim(a, chip, 0, keepdims=False)

    def sum_in(l, cp, arrived):
        return _sum4(arrived, own(cp), tr=_tile(D // 2, 256), name=f"chip_sum_w_in_{l}")

    def sum_out(l, cp, arrived):
        return _sum4(arrived, own(cp), tr=rsh // 2, name=f"chip_sum_w_out_{l}")

    dlng, dlnb = [None, dg_l[0]], [None, db_l[0]]
    dx, dwin1, dwo1, small1, _, _ = _layer_bwd(1, saved[1], xbs[1], dz, dzb, wb[1], wob[1], rope_c, rope_s, params(1))
    dwo1_4 = dwo1.reshape(4, rsh, D)
    dz, dzb, dg_l, db_l, _ = _ln_bwd(saved[0]["z"], ln_g[0], ln_b[0], dx, tm=tm_ln, name="ln_bwd_0")
    dlng[0], dlnb[0] = dg_l[0], db_l[0]

    held = {}

    def side_dmix(dwo0):
        return _swap_side([dwin1, dwo1_4, dwo0.reshape(4, rsh, D)], [0, 1, 1])

    def side_dw_in(small0, dwo0, got):
        sm = {k: jnp.stack([small0[k], small1[k]]) for k in small0}
        sm["ln_g"], sm["ln_b"] = jnp.stack(dlng), jnp.stack(dlnb)
        gs = _pack([sm[n] for n in names])
        (got_s,) = _swap_halves([gs], [0], "reduce_pair_small")
        held["in_cp1"] = _pair_sum_windows(dwin1, got[0], nsh, wcov, out_dtype=BF16, name="pair_sum_w_in_1")
        held["out_cp1"] = _pair_sum_blocks(dwo1_4, got[1], out_dtype=BF16, name="pair_sum_w_out_1")
        held["out_cp0"] = _pair_sum_blocks(dwo0.reshape(4, rsh, D), got[2], out_dtype=BF16, name="pair_sum_w_out_0")
        held["s_cp"] = _pair_sum_blocks(gs[None], got_s[None], out_dtype=F32, name="pair_sum_small")[0]
        held["shapes"] = [sm[n].shape for n in names]
        return _chips_side([held["in_cp1"], held["out_cp1"], held["out_cp0"], held["s_cp"]],
                           [True, True, True, False])

    def side_dx(dwin0):
        got = _swap_halves([dwin0], [0], "reduce_pair_0b")
        held["in_cp0"] = _pair_sum_windows(dwin0, got[0], nsh, wcov, out_dtype=BF16, name="pair_sum_w_in_0")
        return _chips_side([held["in_cp0"]], [True])

    dx, _, _, _, arrived_a, arrived_b = _layer_bwd(0, saved[0], xbs[0], dz, dzb, wb[0], wob[0], rope_c, rope_s,
                                                   params(0), side_dmix=side_dmix, side_dw_in=side_dw_in,
                                                   side_dx=side_dx)
    grad_x = dx[None]
    loss = lax.psum(loss_part[0, 0], ("x", "y", "c"))
    s_cp = held["s_cp"]
    mine = [sum_in(0, held["in_cp0"], arrived_b[0]), sum_out(0, held["out_cp0"], arrived_a[2]),
            sum_in(1, held["in_cp1"], arrived_a[0]), sum_out(1, held["out_cp1"], arrived_a[1]),
            _sum4(arrived_a[3], s_cp, tr=s_cp.shape[0], name="chip_sum_small")]
    other = _swap_whole(mine, "reduce_join")

    def both(k, axis):
        return jnp.where(cc == 0, jnp.concatenate([mine[k], other[k]], axis),
                         jnp.concatenate([other[k], mine[k]], axis))

    g_w_in = lax.dynamic_slice_in_dim(jnp.stack([both(2 * l, 0) for l in range(DEPTH)]), (nsh * chip) % 128, nsh, 2)
    g_w_out = jnp.stack([both(2 * l + 1, 0) for l in range(DEPTH)])
    g_small = both(2 * DEPTH, 0)

    gsm = dict(zip(names, _unpack(g_small, held["shapes"])))
    gsm["r_conv_w"] = lax.dynamic_slice_in_dim(gsm["r_conv_w"], chip * rcw_n, rcw_n, 2)
    gsm["g_conv_w"] = lax.dynamic_slice_in_dim(gsm["g_conv_w"], chip * gcw_n, gcw_n, 2)
    wts = dict(sinks=sinks, r_conv_w=r_conv_w, r_conv_b=r_conv_b, r_wa=r_wa, r_ba=r_ba, r_wx=r_wx, r_bx=r_bx,
               r_lam=r_lam, g_conv_w=g_conv_w, g_a_log=g_a_log, g_dt_bias=g_dt_bias, g_norm_w=g_norm_w,
               ln_g=ln_g, ln_b=ln_b)
    mom = dict(sinks=m_sinks, r_conv_w=m_r_conv_w, r_conv_b=m_r_conv_b, r_wa=m_r_wa, r_ba=m_r_ba, r_wx=m_r_wx,
               r_bx=m_r_bx, r_lam=m_r_lam, g_conv_w=m_g_conv_w, g_a_log=m_g_a_log, g_dt_bias=m_g_dt_bias,
               g_norm_w=m_g_norm_w, ln_g=m_ln_g, ln_b=m_ln_b)
    vel = dict(sinks=v_sinks, r_conv_w=v_r_conv_w, r_conv_b=v_r_conv_b, r_wa=v_r_wa, r_ba=v_r_ba, r_wx=v_r_wx,
               r_bx=v_r_bx, r_lam=v_r_lam, g_conv_w=v_g_conv_w, g_a_log=v_g_a_log, g_dt_bias=v_g_dt_bias,
               g_norm_w=v_g_norm_w, ln_g=v_ln_g, ln_b=v_ln_b)
    d_s, m_s, v_s = _adamw_many(*[[d[n] for n in names] for d in (wts, gsm, mom, vel)], name="adamw_small")
    d_sm, m_sm, v_sm = (dict(zip(names, a)) for a in (d_s, m_s, v_s))
    def cols(a):
        return jnp.transpose(a, (2, 0, 1))

    g_w_in_t = cols(g_w_in)
    outs_t = _adamw_cols(cols(w_in), g_w_in_t, cols(m_w_in), cols(v_w_in), name="adamw_w_in")
    d_in, m_in, v_in = (jnp.transpose(a, (1, 2, 0)) for a in outs_t)
    g_w_in = jnp.transpose(g_w_in_t, (1, 2, 0))
    d_out, m_out, v_out = _adamw(w_out, g_w_out, m_w_out, v_w_out, tr=256, name="adamw_w_out")

    order = ["w_in", "sinks", "r_conv_w", "r_conv_b", "r_wa", "r_ba", "r_wx", "r_bx", "r_lam", "g_conv_w",
             "g_a_log", "g_dt_bias", "g_norm_w", "w_out", "ln_g", "ln_b"]
    grads = dict(gsm, w_in=g_w_in, w_out=g_w_out)
    deltas = dict(d_sm, w_in=d_in, w_out=d_out)
    new_m = dict(m_sm, w_in=m_in, w_out=m_out)
    new_v = dict(v_sm, w_in=v_in, w_out=v_out)
    return (loss, grad_x, *[grads[n] for n in order], *[deltas[n] for n in order],
            *[new_m[n] for n in order], *[new_v[n] for n in order])
```

```python
import functools
import math

import jax
import jax.numpy as jnp
import numpy as np
from jax import lax
from jax.experimental import pallas as pl
from jax.experimental.pallas import tpu as pltpu

F32 = jnp.float32
BF16 = jnp.bfloat16
MESH = pl.DeviceIdType.MESH

DEPTH = 2
A_HEADS, A_KV_HEADS, A_HEAD_DIM = 8, 2, 64
A_WIDTH, A_KV_WIDTH = 512, 128
WINDOW = 128
ROPE_THETA = 10000.0
R_WIDTH, R_BLOCKS, R_BLOCK_DIM, R_C = 1024, 8, 128, 8.0
CONV_WIDTH = 4
G_HEADS, G_HEAD_DIM, G_WIDTH, G_CHUNK = 4, 128, 512, 64
MIX_WIDTH = 2048
IN_SIZES = (512, 128, 128, 512, 1024, 1024, 512, 512, 512, 512, 4, 4)
N_IN = 5384
DEEPNORM_ALPHA = (2 * DEPTH) ** 0.25
LN_EPS = 1e-5
RMS_EPS = 1e-6
ADAM_LR, ADAM_B1, ADAM_B2, ADAM_EPS, ADAM_WD, ADAM_STEP = 0.001, 0.9, 0.999, 1e-08, 0.01, 10

NP = 5632
OFF_GQKV, OFF_GZ, OFF_RX, OFF_RZ, OFF_AQ, OFF_AZ, OFF_AK, OFF_AV, OFF_GBA = (
    0, 1536, 2048, 3072, 4096, 4608, 5120, 5248, 5376)
_ORIG_OFF = np.concatenate([[0], np.cumsum(IN_SIZES)])[:-1]
_PIECES = ((6, OFF_GQKV), (7, OFF_GQKV + 512), (8, OFF_GQKV + 1024), (9, OFF_GZ), (4, OFF_RX), (5, OFF_RZ),
           (0, OFF_AQ), (3, OFF_AZ), (1, OFF_AK), (2, OFF_AV), (10, OFF_GBA), (11, OFF_GBA + 4))


def _orig_block_to_perm():
    table = list(range(NP // 128))
    for oi, off in _PIECES:
        if IN_SIZES[oi] % 128 == 0:
            for k in range(IN_SIZES[oi] // 128):
                table[int(_ORIG_OFF[oi]) // 128 + k] = off // 128 + k
    return table
MIX_R, MIX_A, MIX_G = 0, 1024, 1536
VMEM_LIMIT = 56 * 1024 * 1024
ANY_SPEC = pl.BlockSpec(memory_space=pl.ANY)


def _pcall(body, **kw):
    return pl.pallas_call(body, **kw)


def _cp(sem, limit=VMEM_LIMIT):
    return pltpu.CompilerParams(dimension_semantics=sem, vmem_limit_bytes=limit)


def _sigmoid(x):
    return 0.5 + 0.5 * jnp.tanh(0.5 * x)


def _silu(x):
    return x * _sigmoid(x)


def _dsilu(x):
    s = _sigmoid(x)
    return s * (1.0 + x * (1.0 - s))


def _log1p(x):
    u = 1.0 + x
    d = jnp.where(u == 1.0, 1.0, u - 1.0)
    return jnp.where(u == 1.0, x, jnp.log(u) * (x / d))


def _softplus(x):
    return jnp.maximum(x, 0.0) + _log1p(jnp.exp(-jnp.abs(x)))


def _one_minus_exp(x):
    series = -x * (1.0 + x * (0.5 + x * (1.0 / 6.0 + x * (1.0 / 24.0))))
    return jnp.where(x > -0.05, series, 1.0 - jnp.exp(x))


def _nn(a, b):
    return lax.dot_general(a, b, (((1,), (0,)), ((), ())), preferred_element_type=F32)


def _nt(a, b):
    return lax.dot_general(a, b, (((1,), (1,)), ((), ())), preferred_element_type=F32)


def _tn(a, b):
    return lax.dot_general(a, b, (((0,), (0,)), ((), ())), preferred_element_type=F32)


def _b(x):
    return x.astype(BF16)


def _split3(x):
    hi = x.astype(BF16)
    r1 = x - hi.astype(F32)
    mid = r1.astype(BF16)
    lo = (r1 - mid.astype(F32)).astype(BF16)
    return hi, mid, lo


def _dot3(f, a, b):
    ah, am, _ = _split3(a)
    bh, bm, _ = _split3(b)
    return f(ah, bh) + (f(ah, bm) + f(am, bh))


def _dot_exact_lhs(f, a_bf16, b):
    bh, bm, bl = _split3(b)
    return f(a_bf16, bh) + (f(a_bf16, bm) + f(a_bf16, bl))


def _rot(x):
    w = x.shape[-1]
    lane = lax.broadcasted_iota(jnp.int32, (1, w), 1)
    return jnp.where((lane & 63) < 32, pltpu.roll(x, w - 32, 1), pltpu.roll(x, 32, 1))


def _conv_taps(ext, n):
    return [pltpu.roll(ext, 3 - k, 0)[8:8 + n] if k < 3 else ext[8:8 + n] for k in range(CONV_WIDTH)]


def _conv_taps_t(ext, n):
    m = ext.shape[0]
    return [pltpu.roll(ext, m - (3 - k), 0)[0:n] if k < 3 else ext[0:n] for k in range(CONV_WIDTH)]


def _scan_steps(a, b, pos, span, shifts, reverse):
    n = a.shape[0]
    for s in shifts:
        if reverse:
            a_sh = pltpu.roll(a, n - s, 0)
            b_sh = pltpu.roll(b, n - s, 0)
            ok = pos < (span - s)
        else:
            a_sh = pltpu.roll(a, s, 0)
            b_sh = pltpu.roll(b, s, 0)
            ok = pos >= s
        b = jnp.where(ok, a * b_sh + b, b)
        a = jnp.where(ok, a * a_sh, a)
    return a, b


def _scan_lin(a, b, reverse):
    n = a.shape[0]
    shifts = []
    s = 1
    while s < n:
        shifts.append(s)
        s *= 2
    return _scan_steps(a, b, lax.broadcasted_iota(jnp.int32, (n, 1), 0), n, shifts, reverse)


class _Side:
    def __init__(self, inputs, out_shapes, n_sems, start, finish):
        self.inputs, self.out_shapes, self.n_sems, self.start, self.finish = inputs, out_shapes, n_sems, start, finish


def _matmul(a, b, *, ta, tb, tm, tn, tk, out_dtype, name, extra=None, alpha=0.0, out_blocks=None, side=None,
            rows=None, into=None):
    if ta:
        K, M = a.shape
    else:
        M, K = a.shape
    if tb:
        N, K2 = b.shape
    else:
        K2, N = b.shape
    assert K == K2 and M % tm == 0 and N % tn == 0 and K % tk == 0, (a.shape, b.shape, tm, tn, tk)
    nk = K // tk
    ca = 0 if ta else 1
    cb = 1 if tb else 0
    has_extra = extra is not None

    assert nk == 1 or out_dtype == F32
    assert rows is None or (not ta and out_blocks is None)
    r0, nrow = rows if rows else (0, M // tm)
    n_in = 2 + int(has_extra) + int(into is not None)
    ns_in = len(side.inputs) if side else 0
    ns_out = len(side.out_shapes) if side else 0
    grid = (nrow, N // tn, nk)

    def body(*refs):
        a_ref, b_ref = refs[0], refs[1]
        e_ref = refs[2] if has_extra else None
        o_ref = refs[n_in + ns_in]
        k = pl.program_id(2)
        if side:
            s_in = refs[n_in:n_in + ns_in]
            s_out = refs[n_in + ns_in + 1:n_in + ns_in + 1 + ns_out]
            ssem, rsem = refs[-2], refs[-1]
            i, j = pl.program_id(0), pl.program_id(1)

            @pl.when((i == 0) & (j == 0) & (k == 0))
            def _():
                side.start(s_in, s_out, ssem, rsem)

            @pl.when((i == grid[0] - 1) & (j == grid[1] - 1) & (k == grid[2] - 1))
            def _():
                side.finish(s_in, s_out, ssem, rsem)

        part = lax.dot_general(a_ref[...], b_ref[...], (((ca,), (cb,)), ((), ())), preferred_element_type=F32)
        if nk == 1:
            if e_ref is not None:
                part = part + alpha * e_ref[...]
            o_ref[...] = part.astype(o_ref.dtype)
            return

        @pl.when(k == 0)
        def _():
            o_ref[...] = part

        @pl.when((k > 0) & (k < nk - 1))
        def _():
            o_ref[...] += part

        @pl.when(k == nk - 1)
        def _():
            last = o_ref[...] + part
            if e_ref is not None:
                last = last + alpha * e_ref[...]
            o_ref[...] = last

    a_spec = (pl.BlockSpec((tk, tm), lambda i, j, k: (k, i)) if ta
              else pl.BlockSpec((tm, tk), lambda i, j, k: (i + r0, k)))
    b_spec = (pl.BlockSpec((tn, tk), lambda i, j, k: (j, k)) if tb
              else pl.BlockSpec((tk, tn), lambda i, j, k: (k, j)))
    e_spec = pl.BlockSpec((tm, tn), lambda i, j, k: (i + r0, j))
    if out_blocks is None:
        o_spec, o_shape = e_spec, (M, N)
    else:
        o_shape, o_block, o_map = out_blocks
        o_spec = pl.BlockSpec(o_block, lambda i, j, k: o_map(i, j))
    in_specs = [a_spec, b_spec] + ([e_spec] if has_extra else []) + ([ANY_SPEC] if into is not None else [])
    args = (a, b) + ((extra,) if has_extra else ()) + ((into,) if into is not None else ())
    alias = {n_in - 1: 0} if into is not None else {}
    if not side:
        return _pcall(
            body, name=name, grid=grid, in_specs=in_specs, out_specs=o_spec,
            out_shape=jax.ShapeDtypeStruct(o_shape, out_dtype), input_output_aliases=alias,
            compiler_params=_cp(("parallel", "parallel", "arbitrary")),
        )(*args)
    outs = _pcall(
        body, name=name, grid=grid, in_specs=in_specs + [HBM_SPEC] * ns_in,
        out_specs=[o_spec] + [HBM_SPEC] * ns_out,
        out_shape=[jax.ShapeDtypeStruct(o_shape, out_dtype)] + list(side.out_shapes), input_output_aliases=alias,
        scratch_shapes=[pltpu.SemaphoreType.DMA((side.n_sems,)), pltpu.SemaphoreType.DMA((side.n_sems,))],
        compiler_params=_cp(("arbitrary", "arbitrary", "arbitrary")),
    )(*args, *side.inputs)
    return outs[0], outs[1:]


def _ln_stats(z):
    mu = jnp.mean(z, -1, keepdims=True)
    zc = z - mu
    var = jnp.mean(zc * zc, -1, keepdims=True)
    rstd = lax.rsqrt(var + LN_EPS)
    return zc * rstd, rstd


def _ln_bwd_tile(z, gam, bet, other, from_target, dz_ref, dzb_ref, dg_ref, db_ref, loss_ref):
    i = pl.program_id(0)

    @pl.when(i == 0)
    def _():
        dg_ref[...] = jnp.zeros_like(dg_ref)
        db_ref[...] = jnp.zeros_like(db_ref)
        loss_ref[...] = jnp.zeros_like(loss_ref)

    xh, rstd = _ln_stats(z)
    if from_target:
        err = xh * gam + bet - other
        per_tok = jnp.mean(err * err, -1, keepdims=True)
        loss_ref[...] += 0.5 * jnp.sum(per_tok, 0, keepdims=True)
        dy = err * (1.0 / z.shape[-1])
    else:
        dy = other
    dxh = dy * gam
    m1 = jnp.mean(dxh, -1, keepdims=True)
    m2 = jnp.mean(dxh * xh, -1, keepdims=True)
    dz = rstd * (dxh - m1 - xh * m2)
    dz_ref[...] = dz
    dzb_ref[...] = dz.astype(BF16)
    dg_ref[...] += jnp.sum(dy * xh, 0, keepdims=True)
    db_ref[...] += jnp.sum(dy, 0, keepdims=True)


def _outproj(ymix, wo, x, g, b, *, tm, name, target=None):
    S_, D = x.shape
    last = target is not None

    def body(*refs):
        y_ref, w_ref, x_ref, g_ref, b_ref = refs[:5]
        z = DEEPNORM_ALPHA * x_ref[...] + _nn(y_ref[...], w_ref[...])
        if last:
            _ln_bwd_tile(z, g_ref[...], b_ref[...], refs[5][...], True, *refs[6:])
            return
        z_ref, o_ref, ob_ref, obt_ref = refs[5:]
        z_ref[...] = z
        xh, _ = _ln_stats(z)
        y = xh * g_ref[...] + b_ref[...]
        o_ref[...] = y
        ob_ref[...] = y.astype(BF16)
        obt_ref[...] = y.T.astype(BF16)

    row = pl.BlockSpec((tm, D), lambda i: (i, 0))
    vec = pl.BlockSpec((1, D), lambda i: (0, 0))
    one = pl.BlockSpec((1, 1), lambda i: (0, 0))
    in_specs = [pl.BlockSpec((tm, MIX_WIDTH), lambda i: (i, 0)), pl.BlockSpec((MIX_WIDTH, D), lambda i: (0, 0)),
                row, vec, vec]
    f32s, b16s = jax.ShapeDtypeStruct((S_, D), F32), jax.ShapeDtypeStruct((S_, D), BF16)
    v32s = jax.ShapeDtypeStruct((1, D), F32)
    args = (ymix, wo, x, g.reshape(1, D), b.reshape(1, D))
    if last:
        return _pcall(body, name=name, grid=(S_ // tm,), in_specs=in_specs + [row],
                      out_specs=[row, row, vec, vec, one],
                      out_shape=[f32s, b16s, v32s, v32s, jax.ShapeDtypeStruct((1, 1), F32)],
                      compiler_params=_cp(("arbitrary",)))(*args, target)
    return _pcall(body, name=name, grid=(S_ // tm,), in_specs=in_specs,
                  out_specs=[row, row, row, pl.BlockSpec((D, tm), lambda i: (0, i))],
                  out_shape=[f32s, f32s, b16s, jax.ShapeDtypeStruct((D, S_), BF16)],
                  compiler_params=_cp(("parallel",)))(*args)


def _ln_bwd(z, g, b, dy, *, tm, name):
    S_, D = z.shape

    def body(z_ref, g_ref, b_ref, o_ref, *outs):
        _ln_bwd_tile(z_ref[...], g_ref[...], b_ref[...], o_ref[...], False, *outs)

    row = pl.BlockSpec((tm, D), lambda i: (i, 0))
    vec = pl.BlockSpec((1, D), lambda i: (0, 0))
    one = pl.BlockSpec((1, 1), lambda i: (0, 0))
    return _pcall(
        body, name=name, grid=(S_ // tm,), in_specs=[row, vec, vec, row],
        out_specs=[row, row, vec, vec, one],
        out_shape=[jax.ShapeDtypeStruct((S_, D), F32), jax.ShapeDtypeStruct((S_, D), BF16),
                   jax.ShapeDtypeStruct((1, D), F32), jax.ShapeDtypeStruct((1, D), F32),
                   jax.ShapeDtypeStruct((1, 1), F32)],
        compiler_params=_cp(("arbitrary",)),
    )(z, g.reshape(1, D), b.reshape(1, D), dy)


def _attn_masks(i, sk_ref):
    ri = lax.broadcasted_iota(jnp.int32, (512, 256), 0)
    cj = lax.broadcasted_iota(jnp.int32, (512, 256), 1)
    diff = (ri & 127) - cj + 128
    band = (diff >= 0) & (diff < WINDOW)
    bias = jnp.where(band, 0.0, -jnp.inf)
    bias0 = jnp.where(band & ((i > 0) | (cj >= 128)), 0.0, -jnp.inf)
    grp = lax.broadcasted_iota(jnp.int32, (512, 1), 0) >> 7
    skvs = []
    for h in range(A_KV_HEADS):
        skv = jnp.zeros((512, 1), F32)
        for g in range(4):
            skv = jnp.where(grp == g, sk_ref[h * 4 + g], skv)
        skvs.append(skv)
    return bias0, bias, skvs


def _attn_common(masks, b, h, qr, kd, vd):
    lane = lax.broadcasted_iota(jnp.int32, (1, 128), 1)
    lof = (lane < 64).astype(F32)
    hif = 1.0 - lof
    r0 = b * 128
    skv = masks[2][h]
    pairs = [qr[r0:r0 + 128, h * 256 + p * 128:h * 256 + (p + 1) * 128] for p in (0, 1)]
    qs = _b(jnp.concatenate([pairs[0] * lof, pairs[0] * hif, pairs[1] * lof, pairs[1] * hif], 0))
    k2 = kd[h][r0:r0 + 256]
    v2 = vd[h][r0:r0 + 256]
    s = _nt(qs, k2) * (A_HEAD_DIM ** -0.5) + (masks[0] if b == 0 else masks[1])
    m = jnp.maximum(jnp.max(s, 1, keepdims=True), skv)
    p = jnp.exp(s - m)
    esk = jnp.exp(skv - m)
    rz = 1.0 / (jnp.sum(p, 1, keepdims=True) + esk)
    prob = p * rz
    o4 = _nn(_b(prob), v2)
    return lof, hif, qs, k2, v2, prob, esk * rz, o4


def _attn_prep(T, q_ref, k_ref, v_ref, c_ref, s_ref, kprev, vprev):
    C = c_ref[...]
    Sg = s_ref[...]
    C4 = jnp.concatenate([C] * 4, 1)
    S4 = jnp.concatenate([Sg] * 4, 1)
    q = q_ref[...]
    qr = q * C4 + _rot(q) * S4
    k = k_ref[...]
    kr = k * C + _rot(k) * Sg
    v = v_ref[...]
    kext = jnp.concatenate([kprev[...], kr], 0)
    vext = jnp.concatenate([vprev[...], v], 0)
    kprev[...] = kr[T - 128:]
    vprev[...] = v[T - 128:]
    lo = lax.broadcasted_iota(jnp.int32, (1, 128), 1) < 64
    kroll = pltpu.roll(kext, 64, 1)
    vroll = pltpu.roll(vext, 64, 1)
    kd = [_b(jnp.where(lo, kext, kroll)), _b(jnp.where(lo, kroll, kext))]
    vd = [_b(jnp.where(lo, vext, vroll)), _b(jnp.where(lo, vroll, vext))]
    return C, Sg, C4, S4, qr, kd, vd


def _attn_specs(T):
    return [pl.BlockSpec(memory_space=pltpu.SMEM),
            pl.BlockSpec((T, 512), lambda i: (i, OFF_AQ // 512)),
            pl.BlockSpec((T, 512), lambda i: (i, OFF_AZ // 512)),
            pl.BlockSpec((T, 128), lambda i: (i, OFF_AK // 128)),
            pl.BlockSpec((T, 128), lambda i: (i, OFF_AV // 128)),
            pl.BlockSpec((T, 128), lambda i: (i, 0)),
            pl.BlockSpec((T, 128), lambda i: (i, 0))]


def _attn_fwd(proj, rope_c, rope_s, sinks, ymix, *, T, name):
    S_ = proj.shape[0]
    nb = T // 128

    def body(sk_ref, q_ref, z_ref, k_ref, v_ref, c_ref, s_ref, _, y_ref, kprev, vprev):
        i = pl.program_id(0)

        @pl.when(i == 0)
        def _():
            kprev[...] = jnp.zeros_like(kprev)
            vprev[...] = jnp.zeros_like(vprev)

        _, _, _, _, qr, kd, vd = _attn_prep(T, q_ref, k_ref, v_ref, c_ref, s_ref, kprev, vprev)
        masks = _attn_masks(i, sk_ref)
        for b in range(nb):
            r0 = b * 128
            for h in range(2):
                lof, hif, _, _, _, _, _, o4 = _attn_common(masks, b, h, qr, kd, vd)
                for p in range(2):
                    cs = slice(h * 256 + p * 128, h * 256 + (p + 1) * 128)
                    o = o4[2 * p * 128:(2 * p + 1) * 128] * lof + o4[(2 * p + 1) * 128:(2 * p + 2) * 128] * hif
                    y_ref[r0:r0 + 128, cs] = (o * _silu(z_ref[r0:r0 + 128, cs])).astype(BF16)

    return _pcall(
        body, name=name, grid=(S_ // T,), in_specs=_attn_specs(T) + [ANY_SPEC],
        out_specs=pl.BlockSpec((T, 512), lambda i: (i, MIX_A // 512)),
        out_shape=jax.ShapeDtypeStruct(ymix.shape, BF16),
        input_output_aliases={7: 0},
        scratch_shapes=[pltpu.VMEM((128, 128), F32), pltpu.VMEM((128, 128), F32)],
        compiler_params=_cp(("arbitrary",)),
    )(sinks, proj, proj, proj, proj, rope_c, rope_s, ymix)


def _attn_bwd(proj, rope_c, rope_s, sinks, dymix, *, T, name):
    S_ = proj.shape[0]
    nb = T // 128
    nt = S_ // T

    def body(sk_ref, q_ref, z_ref, k_ref, v_ref, c_ref, s_ref, dy_ref,
             dp_ref, dk_ref, dv_ref, dkt_ref, dvt_ref, dsk_ref,
             kprev, vprev, cprev, sprev, dkacc, dvacc, dqacc):
        i = pl.program_id(0)

        @pl.when(i == 0)
        def _():
            kprev[...] = jnp.zeros_like(kprev)
            vprev[...] = jnp.zeros_like(vprev)
            cprev[...] = jnp.zeros_like(cprev)
            sprev[...] = jnp.zeros_like(sprev)
            dkacc[...] = jnp.zeros_like(dkacc)
            dvacc[...] = jnp.zeros_like(dvacc)
            dsk_ref[...] = jnp.zeros_like(dsk_ref)

        @pl.when(i > 0)
        def _():
            dkacc[0:128, :] = dkacc[T:T + 128, :]
            dvacc[0:128, :] = dvacc[T:T + 128, :]
            dkacc[128:, :] = jnp.zeros((T, 128), F32)
            dvacc[128:, :] = jnp.zeros((T, 128), F32)

        C, Sg, C4, S4, qr, kd, vd = _attn_prep(T, q_ref, k_ref, v_ref, c_ref, s_ref, kprev, vprev)
        masks = _attn_masks(i, sk_ref)
        lane = lax.broadcasted_iota(jnp.int32, (1, 128), 1)
        for b in range(nb):
            r0 = b * 128
            for h in range(2):
                lof, hif, qs, k2, v2, prob, psink, o4 = _attn_common(masks, b, h, qr, kd, vd)
                dos = []
                for p in range(2):
                    cs = slice(h * 256 + p * 128, h * 256 + (p + 1) * 128)
                    o = o4[2 * p * 128:(2 * p + 1) * 128] * lof + o4[(2 * p + 1) * 128:(2 * p + 2) * 128] * hif
                    zc = z_ref[r0:r0 + 128, cs]
                    dyc = dy_ref[r0:r0 + 128, cs]
                    dp_ref[r0:r0 + 128, 512 + cs.start:512 + cs.stop] = (dyc * o * _dsilu(zc)).astype(BF16)
                    do = dyc * _silu(zc)
                    dos += [do * lof, do * hif]
                dos = jnp.concatenate(dos, 0)
                os_ = jnp.concatenate([o4[0:128] * lof, o4[128:256] * hif, o4[256:384] * lof, o4[384:512] * hif], 0)
                delta = jnp.sum(dos * os_, 1, keepdims=True)
                dosb = _b(dos)
                dp = _nt(dosb, v2)
                ds = prob * (dp - delta)
                dsv = -psink * delta
                for g in range(4):
                    sg = jnp.sum(dsv[g * 128:(g + 1) * 128], 0, keepdims=True)
                    hd = h * 4 + g
                    dsk_ref[hd:hd + 1, :] += jnp.broadcast_to(sg, (1, 128))
                dsb = _b(ds * (A_HEAD_DIM ** -0.5))
                dqs = _nn(dsb, k2)
                for p in range(2):
                    cs = slice(h * 256 + p * 128, h * 256 + (p + 1) * 128)
                    dqacc[r0:r0 + 128, cs] = (dqs[2 * p * 128:(2 * p + 1) * 128] * lof
                                              + dqs[(2 * p + 1) * 128:(2 * p + 2) * 128] * hif)
                dkdup = _tn(dsb, qs)
                dvdup = _tn(_b(prob), dosb)
                half = (lane < 64) if h == 0 else (lane >= 64)
                dkacc[r0:r0 + 256, :] += jnp.where(half, dkdup + pltpu.roll(dkdup, 64, 1), 0.0)
                dvacc[r0:r0 + 256, :] += jnp.where(half, dvdup + pltpu.roll(dvdup, 64, 1), 0.0)
        dqr = dqacc[...]
        dp_ref[:, 0:512] = (dqr * C4 + _rot(dqr * S4)).astype(BF16)
        cext = jnp.concatenate([cprev[...], C], 0)
        sext = jnp.concatenate([sprev[...], Sg], 0)
        dke = dkacc[...]
        dkp = dke * cext + _rot(dke * sext)
        dk_ref[...] = dkp[0:T].astype(BF16)
        dkt_ref[...] = dkp[T:T + 128].astype(BF16)
        dve = dvacc[...]
        dv_ref[...] = dve[0:T].astype(BF16)
        dvt_ref[...] = dve[T:T + 128].astype(BF16)
        cprev[...] = C[T - 128:]
        sprev[...] = Sg[T - 128:]

    nar = pl.BlockSpec((T, 128), lambda i: (i, 0))
    tail = pl.BlockSpec((128, 128), lambda i: (0, 0))
    return _pcall(
        body, name=name, grid=(nt,),
        in_specs=_attn_specs(T) + [pl.BlockSpec((T, 512), lambda i: (i, MIX_A // 512))],
        out_specs=[pl.BlockSpec((T, 1024), lambda i: (i, OFF_AQ // 1024)), nar, nar, tail, tail,
                   pl.BlockSpec((8, 128), lambda i: (0, 0))],
        out_shape=[jax.ShapeDtypeStruct((S_, NP), BF16),
                   jax.ShapeDtypeStruct((S_, 128), BF16), jax.ShapeDtypeStruct((S_, 128), BF16),
                   jax.ShapeDtypeStruct((128, 128), BF16), jax.ShapeDtypeStruct((128, 128), BF16),
                   jax.ShapeDtypeStruct((8, 128), F32)],
        scratch_shapes=[pltpu.VMEM((128, 128), F32)] * 4
        + [pltpu.VMEM((T + 128, 128), F32), pltpu.VMEM((T + 128, 128), F32), pltpu.VMEM((T, 512), F32)],
        compiler_params=_cp(("arbitrary",)),
    )(sinks, proj, proj, proj, proj, rope_c, rope_s, dymix)


def _rg_gates(xr, wa_ref, ba_ref, wx_ref, bx_ref, lam_ref):
    xb = _b(xr)
    pre_a = jnp.concatenate([_nn(xb[:, n * 128:(n + 1) * 128], wa_ref[n]) for n in range(R_BLOCKS)], 1) + ba_ref[...]
    pre_x = jnp.concatenate([_nn(xb[:, n * 128:(n + 1) * 128], wx_ref[n]) for n in range(R_BLOCKS)], 1) + bx_ref[...]
    r = _sigmoid(pre_a)
    ig = _sigmoid(pre_x)
    sp = _softplus(-lam_ref[...])
    log_a = -R_C * r * sp
    a = jnp.exp(log_a)
    mult = jnp.sqrt(_one_minus_exp(2.0 * log_a))
    return xb, r, ig, sp, a, mult


def _rg_param_specs():
    C = R_WIDTH
    vec = pl.BlockSpec((1, C), lambda i: (0, 0))
    blk = pl.BlockSpec((R_BLOCKS, 128, 128), lambda i: (0, 0, 0))
    return [pl.BlockSpec((CONV_WIDTH, C), lambda i: (0, 0)), vec, blk, vec, blk, vec, vec]


def _rglru_fwd(proj, cw, cb, wa, ba, wx, bx, lam, *, T, name):
    S_ = proj.shape[0]
    C = R_WIDTH

    def body(rx_ref, rz_ref, cw_ref, cb_ref, wa_ref, ba_ref, wx_ref, bx_ref, lam_ref,
             h_ref, y_ref, halo, hcar):
        i = pl.program_id(0)

        @pl.when(i == 0)
        def _():
            halo[...] = jnp.zeros_like(halo)
            hcar[...] = jnp.zeros_like(hcar)

        rx = rx_ref[...]
        ext = jnp.concatenate([halo[...], rx], 0)
        halo[...] = rx[T - 8:]
        taps = _conv_taps(ext, T)
        xr = cb_ref[...] + sum(cw_ref[k:k + 1, :] * taps[k] for k in range(CONV_WIDTH))
        _, _, ig, _, a, mult = _rg_gates(xr, wa_ref, ba_ref, wx_ref, bx_ref, lam_ref)
        u = mult * (ig * xr)
        acum, hloc = _scan_lin(a, u, False)
        h = hloc + acum * hcar[0:1, :]
        hcar[...] = jnp.broadcast_to(h[T - 1:T, :], (8, C))
        h_ref[...] = h
        y_ref[...] = (h * _silu(rz_ref[...])).astype(BF16)

    row = pl.BlockSpec((T, C), lambda i: (i, 0))
    return _pcall(
        body, name=name, grid=(S_ // T,),
        in_specs=[pl.BlockSpec((T, C), lambda i: (i, OFF_RX // C)),
                  pl.BlockSpec((T, C), lambda i: (i, OFF_RZ // C))] + _rg_param_specs(),
        out_specs=[row, pl.BlockSpec((T, C), lambda i: (i, MIX_R // C))],
        out_shape=[jax.ShapeDtypeStruct((S_, C), F32), jax.ShapeDtypeStruct((S_, MIX_WIDTH), BF16)],
        scratch_shapes=[pltpu.VMEM((8, C), F32), pltpu.VMEM((8, C), F32)],
        compiler_params=_cp(("arbitrary",)),
    )(proj, proj, cw, cb.reshape(1, C), _b(wa), ba.reshape(1, C), _b(wx), bx.reshape(1, C), lam.reshape(1, C))


def _rglru_bwd(proj, h, dymix, dproj, cw, cb, wa, ba, wx, bx, lam, *, T, name):
    S_ = proj.shape[0]
    C = R_WIDTH
    nt = S_ // T
    t8 = T // 8

    def body(rx_ref, rxp_ref, rz_ref, h_ref, hp_ref, dy_ref,
             cw_ref, cb_ref, wa_ref, ba_ref, wx_ref, bx_ref, lam_ref, wat_ref, wxt_ref,
             _, dp_ref, dcw_ref, dcb_ref, dwa_ref, dba_ref, dwx_ref, dbx_ref, dlam_ref,
             afirst, gfirst, dhalo):
        i = pl.program_id(0)
        first_tile = (i == nt - 1)

        @pl.when(i == 0)
        def _():
            afirst[...] = jnp.zeros_like(afirst)
            gfirst[...] = jnp.zeros_like(gfirst)
            dhalo[...] = jnp.zeros_like(dhalo)
            for r in (dcw_ref, dcb_ref, dwa_ref, dba_ref, dwx_ref, dbx_ref, dlam_ref):
                r[...] = jnp.zeros_like(r)

        keep = jnp.where(first_tile, 0.0, 1.0)
        rx = rx_ref[...]
        ext = jnp.concatenate([rxp_ref[...] * keep, rx], 0)
        taps = _conv_taps(ext, T)
        xr = cb_ref[...] + sum(cw_ref[k:k + 1, :] * taps[k] for k in range(CONV_WIDTH))
        xb, r, ig, sp, a, mult = _rg_gates(xr, wa_ref, ba_ref, wx_ref, bx_ref, lam_ref)
        hh = h_ref[...]
        rz = rz_ref[...]
        dy = dy_ref[...]
        dp_ref[:, C:2 * C] = (dy * hh * _dsilu(rz)).astype(BF16)
        dh = dy * _silu(rz)
        row = lax.broadcasted_iota(jnp.int32, (T, 1), 0)
        c = jnp.where(row == T - 1, afirst[0:1, :], pltpu.roll(a, T - 1, 0))
        ccum, gloc = _scan_lin(c, dh, True)
        g = gloc + ccum * gfirst[0:1, :]
        afirst[...] = jnp.broadcast_to(a[0:1, :], (8, C))
        gfirst[...] = jnp.broadcast_to(g[0:1, :], (8, C))
        hprev = jnp.where(row == 0, hp_ref[7:8, :] * keep, pltpu.roll(hh, 1, 0))
        da = g * hprev
        gx = ig * xr
        dgx = g * mult
        dmult = g * gx
        dlog_a = da * a - dmult * (a * a) * lax.rsqrt(mult * mult)
        dpre_a = dlog_a * (-R_C * sp) * r * (1.0 - r)
        dpre_x = dgx * xr * ig * (1.0 - ig)
        dlam_ref[...] += jnp.sum(dlog_a * (-R_C * r), 0, keepdims=True) * (-_sigmoid(-lam_ref[...]))
        dab = _b(dpre_a)
        dxb = _b(dpre_x)
        dxr = dgx * ig + jnp.concatenate(
            [_nn(dab[:, n * 128:(n + 1) * 128], wat_ref[n]) + _nn(dxb[:, n * 128:(n + 1) * 128], wxt_ref[n])
             for n in range(R_BLOCKS)], 1)
        for n in range(R_BLOCKS):
            cs = slice(n * 128, (n + 1) * 128)
            dwa_ref[n] += _tn(xb[:, cs], dab[:, cs])
            dwx_ref[n] += _tn(xb[:, cs], dxb[:, cs])
        dba_ref[...] += jnp.sum(dpre_a, 0, keepdims=True)
        dbx_ref[...] += jnp.sum(dpre_x, 0, keepdims=True)
        dcb_ref[...] += jnp.sum(dxr, 0, keepdims=True)
        for k in range(CONV_WIDTH):
            dcw_ref[k:k + 1, :] += jnp.sum(dxr * taps[k], 0, keepdims=True)
        ext2 = jnp.concatenate([dxr, dhalo[...]], 0)
        tt = _conv_taps_t(ext2, T)
        dp_ref[:, 0:C] = sum(cw_ref[k:k + 1, :] * tt[k] for k in range(CONV_WIDTH)).astype(BF16)
        dhalo[...] = dxr[0:8]

    def rev(i):
        return nt - 1 - i

    def prev8(i):
        return jnp.maximum(rev(i) * t8 - 1, 0)

    vec = pl.BlockSpec((1, C), lambda i: (0, 0))
    blk = pl.BlockSpec((R_BLOCKS, 128, 128), lambda i: (0, 0, 0))
    row = pl.BlockSpec((T, C), lambda i: (rev(i), 0))
    wat = _b(jnp.swapaxes(wa, 1, 2))
    wxt = _b(jnp.swapaxes(wx, 1, 2))
    return _pcall(
        body, name=name, grid=(nt,),
        in_specs=[pl.BlockSpec((T, C), lambda i: (rev(i), OFF_RX // C)),
                  pl.BlockSpec((8, C), lambda i: (prev8(i), OFF_RX // C)),
                  pl.BlockSpec((T, C), lambda i: (rev(i), OFF_RZ // C)),
                  row,
                  pl.BlockSpec((8, C), lambda i: (prev8(i), 0)),
                  pl.BlockSpec((T, C), lambda i: (rev(i), MIX_R // C)),
                  ] + _rg_param_specs() + [blk, blk, ANY_SPEC],
        out_specs=[pl.BlockSpec((T, 2 * C), lambda i: (rev(i), OFF_RX // (2 * C))),
                   pl.BlockSpec((CONV_WIDTH, C), lambda i: (0, 0)), vec, blk, vec, blk, vec, vec],
        out_shape=[jax.ShapeDtypeStruct(dproj.shape, BF16),
                   jax.ShapeDtypeStruct((CONV_WIDTH, C), F32), jax.ShapeDtypeStruct((1, C), F32),
                   jax.ShapeDtypeStruct((R_BLOCKS, 128, 128), F32), jax.ShapeDtypeStruct((1, C), F32),
                   jax.ShapeDtypeStruct((R_BLOCKS, 128, 128), F32), jax.ShapeDtypeStruct((1, C), F32),
                   jax.ShapeDtypeStruct((1, C), F32)],
        input_output_aliases={15: 0},
        scratch_shapes=[pltpu.VMEM((8, C), F32)] * 3,
        compiler_params=_cp(("arbitrary",)),
    )(proj, proj, proj, h, h, dymix, cw, cb.reshape(1, C), _b(wa), ba.reshape(1, C), _b(wx), bx.reshape(1, C),
      lam.reshape(1, C), wat, wxt, dproj)


GW3 = 3 * G_WIDTH


def _lane_col(x, lane_idx):
    lane = lax.broadcasted_iota(jnp.int32, (1, x.shape[1]), 1)
    return jnp.sum(jnp.where(lane == lane_idx, x, 0.0), 1, keepdims=True)


def _gdn_pre(ext, T, cw_ref, gba, pv_ref):
    taps = _conv_taps(ext, T)
    c = sum(cw_ref[k:k + 1, :] * taps[k] for k in range(CONV_WIDTH))
    qkv = _silu(c)
    beta = _sigmoid(gba)
    sarg = gba + pv_ref[1:2, :]
    nea = -jnp.exp(pv_ref[0:1, :])
    gdec = nea * _softplus(sarg)
    ri = lax.broadcasted_iota(jnp.int32, (T, T), 0)
    cj = lax.broadcasted_iota(jnp.int32, (T, T), 1)
    same = (ri >> 6) == (cj >> 6)
    ltri = jnp.where((ri >= cj) & same, 1.0, 0.0).astype(BF16)
    gc = _dot_exact_lhs(_nn, ltri, gdec)
    return taps, c, qkv, beta, sarg, nea, gdec, gc


def _gdn_masks():
    ri = lax.broadcasted_iota(jnp.int32, (128, 128), 0)
    cj = lax.broadcasted_iota(jnp.int32, (128, 128), 1)
    same = (ri >> 6) == (cj >> 6)
    return (ri >= cj) & same, (ri > cj) & same, ri == cj


def _lockstep(gens):
    out = [None] * len(gens)
    live = list(range(len(gens)))
    while live:
        still = []
        for k in live:
            try:
                next(gens[k])
                still.append(k)
            except StopIteration as stop:
                out[k] = stop.value
        live = still
    return out


def _gdn_chunk(qkv, beta, gc, rs, h, tm=None):
    tril, strict, eye = _gdn_masks()
    rowi = lax.broadcasted_iota(jnp.int32, (128, 1), 0)
    lane = lax.broadcasted_iota(jnp.int32, (1, 128), 1)
    qh = qkv[rs, h * 128:(h + 1) * 128]
    kh = qkv[rs, 512 + h * 128:512 + (h + 1) * 128]
    vh = qkv[rs, 1024 + h * 128:1024 + (h + 1) * 128]
    rq = lax.rsqrt(jnp.sum(qh * qh, 1, keepdims=True) + RMS_EPS)
    rk = lax.rsqrt(jnp.sum(kh * kh, 1, keepdims=True) + RMS_EPS)
    qn = qh * (rq * (G_HEAD_DIM ** -0.5))
    kn = kh * rk
    gcb = gc[rs]
    gcol = _lane_col(gcb, 4 + h)
    bcol = _lane_col(beta[rs], h)
    grow = _dot_exact_lhs(_nt, jnp.ones((128, 128), BF16), jnp.where(lane == 4 + h, gcb, 0.0))
    D = jnp.where(tril, jnp.exp(jnp.minimum(gcol - grow, 0.0)), 0.0)
    kb = kn * bcol
    vb = vh * bcol
    knb = _b(kn)
    A = _nt(_b(kb), knb)
    Bm = _nt(_b(qn), knb)
    yield
    if tm is None:
        N = jnp.where(strict, -(A * D), 0.0)
        tm = jnp.where(eye, 1.0, 0.0) + N
        npow = N
        for _ in range(5):
            npow = _dot3(_nn, npow, npow)
            yield
            tm = tm + _dot3(_nn, tm, npow)
            yield
    eg = jnp.exp(gcol)
    u = _dot3(_nn, tm, vb)
    w = _dot3(_nn, tm, kb * eg)
    yield
    qk = jnp.where(tril, Bm * D, 0.0)
    qd = qn * eg
    gla = jnp.sum(jnp.where(rowi == 63, gcol, 0.0), 0, keepdims=True)
    glb = jnp.sum(jnp.where(rowi == 127, gcol, 0.0), 0, keepdims=True)
    ed = jnp.exp(jnp.where(rowi < 64, gla, glb) - gcol)
    kd = kn * ed
    return dict(qh=qh, kh=kh, vh=vh, rq=rq, rk=rk, qn=qn, kn=kn, gcol=gcol, bcol=bcol, D=D, A=A, Bm=Bm,
                tm=tm, eg=eg, ed=ed, u=u, w=w, qk=qk, qd=qd, kd=kd, kb=kb, vb=vb,
                gla=jnp.exp(gla), glb=jnp.exp(glb))


def _gdn_scan(q, sa):
    sab = _b(sa)
    wb = _b(q["w"])
    vna = q["u"] - _nn(wb, sab)
    yield
    sb = sa * q["gla"] + _tn(_b(q["kd"][0:64]), _b(vna[0:64]))
    yield
    sbb = _b(sb)
    vnb = q["u"] - _nn(wb, sbb)
    yield
    sn = sb * q["glb"] + _tn(_b(q["kd"][64:128]), _b(vnb[64:128]))
    yield
    vn = jnp.concatenate([vna[0:64], vnb[64:128]], 0)
    qdb = _b(q["qd"])
    o = jnp.concatenate([_nn(qdb[0:64], sab), _nn(qdb[64:128], sbb)], 0) + _nn(_b(q["qk"]), _b(vn))
    return sb, sn, vn, o


def _gdn_param_specs():
    return [pl.BlockSpec((CONV_WIDTH, GW3), lambda i: (0, 0)),
            pl.BlockSpec((8, 128), lambda i: (0, 0)),
            pl.BlockSpec((1, 128), lambda i: (0, 0))]


def _gdn_pvec(a_log, dt_bias):
    z = jnp.zeros((8, 128), F32)
    return z.at[0, 4:8].set(a_log).at[1, 4:8].set(dt_bias)


def _gdn_fwd(proj, cw, a_log, dt_bias, nw, ymix, *, T, name):
    S_ = proj.shape[0]
    nu = T // 128

    def body(x_ref, z_ref, g_ref, cw_ref, pv_ref, nw_ref, _, y_ref, st_ref, tm_ref, halo, state):
        i = pl.program_id(0)

        @pl.when(i == 0)
        def _():
            halo[...] = jnp.zeros_like(halo)
            state[...] = jnp.zeros_like(state)

        x = x_ref[...]
        ext = jnp.concatenate([halo[...], x], 0)
        halo[...] = x[T - 8:]
        _, _, qkv, beta, _, _, _, gc = _gdn_pre(ext, T, cw_ref, g_ref[...], pv_ref)
        items = [(dc, h) for dc in range(nu) for h in range(G_HEADS)]
        qs = _lockstep([_gdn_chunk(qkv, beta, gc, slice(dc * 128, (dc + 1) * 128), h) for dc, h in items])

        def head_chain(h):
            s = state[h]
            for dc in range(nu):
                rs = slice(dc * 128, (dc + 1) * 128)
                q = qs[dc * G_HEADS + h]
                sb, sn, _, o = yield from _gdn_scan(q, s)
                st_ref[2 * dc, h] = s
                st_ref[2 * dc + 1, h] = sb
                tm_ref[dc, h] = q["tm"]
                s = sn
                yield
                rn = lax.rsqrt(jnp.mean(o * o, 1, keepdims=True) + RMS_EPS)
                cs = slice(h * 128, (h + 1) * 128)
                y_ref[rs, cs] = (o * rn * nw_ref[...] * _silu(z_ref[rs, cs])).astype(BF16)
                yield
            state[h] = s

        _lockstep([head_chain(h) for h in range(G_HEADS)])

    return _pcall(
        body, name=name, grid=(S_ // T,),
        in_specs=[pl.BlockSpec((T, GW3), lambda i: (i, OFF_GQKV // GW3)),
                  pl.BlockSpec((T, 512), lambda i: (i, OFF_GZ // 512)),
                  pl.BlockSpec((T, 128), lambda i: (i, OFF_GBA // 128))] + _gdn_param_specs() + [ANY_SPEC],
        out_specs=[pl.BlockSpec((T, 512), lambda i: (i, MIX_G // 512)),
                   pl.BlockSpec((2 * nu, G_HEADS, 128, 128), lambda i: (i, 0, 0, 0)),
                   pl.BlockSpec((nu, G_HEADS, 128, 128), lambda i: (i, 0, 0, 0))],
        out_shape=[jax.ShapeDtypeStruct(ymix.shape, BF16),
                   jax.ShapeDtypeStruct((S_ // 64, G_HEADS, 128, 128), F32),
                   jax.ShapeDtypeStruct((S_ // 128, G_HEADS, 128, 128), F32)],
        input_output_aliases={6: 0},
        scratch_shapes=[pltpu.VMEM((8, GW3), F32), pltpu.VMEM((G_HEADS, 128, 128), F32)],
        compiler_params=_cp(("arbitrary",)),
    )(proj, proj, proj, cw, _gdn_pvec(a_log, dt_bias), nw.reshape(1, 128), ymix)


def _gdn_bwd(proj, states, tms, dymix, dproj, cw, a_log, dt_bias, nw, *, T, name):
    S_ = proj.shape[0]
    nt = S_ // T
    nu = T // 128
    t8 = T // 8

    def body(x_ref, xp_ref, z_ref, g_ref, st_ref, tm_ref, dy_ref, cw_ref, pv_ref, nw_ref, _,
             dp_ref, dg_ref, dcw_ref, dpv_ref, dnw_ref, dstate, dhalo, dqkv, dbg):
        i = pl.program_id(0)
        first_tile = (i == nt - 1)

        @pl.when(i == 0)
        def _():
            dstate[...] = jnp.zeros_like(dstate)
            dhalo[...] = jnp.zeros_like(dhalo)
            dcw_ref[...] = jnp.zeros_like(dcw_ref)
            dpv_ref[...] = jnp.zeros_like(dpv_ref)
            dnw_ref[...] = jnp.zeros_like(dnw_ref)

        keep = jnp.where(first_tile, 0.0, 1.0)
        ext = jnp.concatenate([xp_ref[...] * keep, x_ref[...]], 0)
        G = g_ref[...]
        taps, c, qkv, beta, sarg, nea, gdec, gc = _gdn_pre(ext, T, cw_ref, G, pv_ref)
        tril, strict, _ = _gdn_masks()
        rowi = lax.broadcasted_iota(jnp.int32, (128, 1), 0)
        lane = lax.broadcasted_iota(jnp.int32, (1, 128), 1)
        ones_b = jnp.ones((128, 128), BF16)
        nwv = nw_ref[...]
        items = [(dc, h) for dc in range(nu) for h in range(G_HEADS)]

        def recompute(dc, h):
            q = yield from _gdn_chunk(qkv, beta, gc, slice(dc * 128, (dc + 1) * 128), h, tm=tm_ref[dc, h])
            sa = st_ref[2 * dc, h]
            sb, _, vn, o = yield from _gdn_scan(q, sa)
            return q, sa, sb, vn, o

        fw = _lockstep([recompute(dc, h) for dc, h in items])
        chain_out = {}

        def head_chain(h):
            dS = dstate[h]
            for dc in reversed(range(nu)):
                rs = slice(dc * 128, (dc + 1) * 128)
                q, sa, sb, vn, o = fw[dc * G_HEADS + h]
                cs = slice(h * 128, (h + 1) * 128)
                zg = z_ref[rs, cs]
                dy = dy_ref[rs, cs]
                rn = lax.rsqrt(jnp.mean(o * o, 1, keepdims=True) + RMS_EPS)
                don = dy * _silu(zg)
                dp_ref[rs, GW3 + cs.start:GW3 + cs.stop] = (dy * (o * rn * nwv) * _dsilu(zg)).astype(BF16)
                dnw_ref[...] += jnp.sum(don * o * rn, 0, keepdims=True)
                tt = don * nwv
                do = rn * (tt - o * (rn * rn) * jnp.mean(tt * o, 1, keepdims=True))
                yield
                dob = _b(do)
                sab, sbb = _b(sa), _b(sb)
                vnb16 = _b(vn)
                dqk = jnp.where(tril, _nt(dob, vnb16), 0.0)
                dvn_o = _tn(_b(q["qk"]), dob)
                dSb16 = _b(dS)
                kdb = _b(q["kd"])
                wb = _b(q["w"])
                qdb = _b(q["qd"])
                yield
                dvn_b = dvn_o[64:128] + _nn(kdb[64:128], dSb16)
                dkd_b = _nt(vnb16[64:128], dSb16)
                dgl_b = jnp.sum(jnp.sum(dS * sb, 1, keepdims=True), 0, keepdims=True)
                yield
                dvn_b16 = _b(dvn_b)
                dw_b = -_nt(dvn_b16, sbb)
                dqd_b = _nt(dob[64:128], sbb)
                dSm = q["glb"] * dS + _tn(qdb[64:128], dob[64:128]) - _tn(wb[64:128], dvn_b16)
                yield
                dSm16 = _b(dSm)
                dvn_a = dvn_o[0:64] + _nn(kdb[0:64], dSm16)
                dkd_a = _nt(vnb16[0:64], dSm16)
                dgl_a = jnp.sum(jnp.sum(dSm * sa, 1, keepdims=True), 0, keepdims=True)
                yield
                dvn_a16 = _b(dvn_a)
                dw_a = -_nt(dvn_a16, sab)
                dqd_a = _nt(dob[0:64], sab)
                dS = q["gla"] * dSm + _tn(qdb[0:64], dob[0:64]) - _tn(wb[0:64], dvn_a16)
                chain_out[dc, h] = (dqk, jnp.concatenate([dvn_a, dvn_b], 0), jnp.concatenate([dw_a, dw_b], 0),
                                    jnp.concatenate([dkd_a, dkd_b], 0), jnp.concatenate([dqd_a, dqd_b], 0),
                                    dgl_a, dgl_b)
                yield
            dstate[h] = dS

        _lockstep([head_chain(h) for h in range(G_HEADS)])

        def local(dc, h):
            rs = slice(dc * 128, (dc + 1) * 128)
            q = fw[dc * G_HEADS + h][0]
            dqk, du, dw, dkd, dqd, dgl_a, dgl_b = chain_out[dc, h]
            if True:
                dvb = _dot3(_tn, q["tm"], du)
                dkbe = _dot3(_tn, q["tm"], dw)
                yield
                dM = jnp.where(strict, -(_nt(_b(dvb), _b(q["u"])) + _nt(_b(dkbe), _b(q["w"]))), 0.0)
                yield
                D = q["D"]
                dA = dM * D
                dB = dqk * D
                dDD = (dM * q["A"] + dqk * q["Bm"]) * D
                dh_, dm_, dl_ = _split3(dDD)
                colsum = _tn(dh_, ones_b) + (_tn(dm_, ones_b) + _tn(dl_, ones_b))
                dgc = jnp.sum(dDD, 1, keepdims=True) - _lane_col(colsum, 0)
                yield
                dA16, dB16 = _b(dA), _b(dB)
                knb, kbb, qnb = _b(q["kn"]), _b(q["kb"]), _b(q["qn"])
                eg, ed = q["eg"], q["ed"]
                dkb = _nn(dA16, knb) + dkbe * eg
                dkn = _tn(dA16, kbb) + _tn(dB16, qnb) + dkd * ed + dkb * q["bcol"]
                dqn = _nn(dB16, knb) + dqd * eg
                yield
                deg = jnp.sum(dkbe * q["kb"], 1, keepdims=True) + jnp.sum(dqd * q["qn"], 1, keepdims=True)
                ded = jnp.sum(dkd * q["kn"], 1, keepdims=True) * ed
                dgc = dgc + deg * eg - ded
                tail_a = jnp.sum(jnp.where(rowi < 64, ded, 0.0), 0, keepdims=True) + dgl_a * q["gla"]
                tail_b = jnp.sum(jnp.where(rowi >= 64, ded, 0.0), 0, keepdims=True) + dgl_b * q["glb"]
                dgc = dgc + jnp.where(rowi == 63, tail_a, 0.0) + jnp.where(rowi == 127, tail_b, 0.0)
                dbeta = jnp.sum(dkb * q["kn"], 1, keepdims=True) + jnp.sum(dvb * q["vh"], 1, keepdims=True)
                bcol = q["bcol"]
                blk = jnp.where(lane == h, dbeta * bcol * (1.0 - bcol), 0.0) + jnp.where(lane == 4 + h, dgc, 0.0)
                yield
                sc = G_HEAD_DIM ** -0.5
                rq, rk, qh, kh = q["rq"], q["rk"], q["qh"], q["kh"]
                dqh = sc * (dqn * rq - qh * (rq * rq * rq) * jnp.sum(dqn * qh, 1, keepdims=True))
                dkh = dkn * rk - kh * (rk * rk * rk) * jnp.sum(dkn * kh, 1, keepdims=True)
                dqkv[rs, h * 128:(h + 1) * 128] = dqh
                dqkv[rs, 512 + h * 128:512 + (h + 1) * 128] = dkh
                dqkv[rs, 1024 + h * 128:1024 + (h + 1) * 128] = dvb * bcol
            return blk

        blks = _lockstep([local(dc, h) for dc, h in items])
        for dc in range(nu):
            dbg[dc * 128:(dc + 1) * 128, :] = functools.reduce(
                lambda a, b: a + b, [blks[dc * G_HEADS + h] for h in range(G_HEADS)])
        ri = lax.broadcasted_iota(jnp.int32, (T, T), 0)
        cj = lax.broadcasted_iota(jnp.int32, (T, T), 1)
        utri = jnp.where((ri <= cj) & ((ri >> 6) == (cj >> 6)), 1.0, 0.0).astype(BF16)
        dbgv = dbg[...]
        dgd = _dot_exact_lhs(_nn, utri, dbgv)
        is_g = (lane >= 4) & (lane < 8)
        dga = jnp.where(is_g, dgd * nea * _sigmoid(sarg), 0.0)
        dg_ref[...] = jnp.where(lane < 4, dbgv, dga).astype(BF16)
        dpv_ref[0:1, :] += jnp.sum(jnp.where(is_g, dgd * gdec, 0.0), 0, keepdims=True)
        dpv_ref[1:2, :] += jnp.sum(dga, 0, keepdims=True)
        dc_ = dqkv[...] * _dsilu(c)
        for k in range(CONV_WIDTH):
            dcw_ref[k:k + 1, :] += jnp.sum(dc_ * taps[k], 0, keepdims=True)
        ext2 = jnp.concatenate([dc_, dhalo[...]], 0)
        tt2 = _conv_taps_t(ext2, T)
        dp_ref[:, 0:GW3] = sum(cw_ref[k:k + 1, :] * tt2[k] for k in range(CONV_WIDTH)).astype(BF16)
        dhalo[...] = dc_[0:8]

    def rev(i):
        return nt - 1 - i

    def prev8(i):
        return jnp.maximum(rev(i) * t8 - 1, 0)

    return _pcall(
        body, name=name, grid=(nt,),
        in_specs=[pl.BlockSpec((T, GW3), lambda i: (rev(i), OFF_GQKV // GW3)),
                  pl.BlockSpec((8, GW3), lambda i: (prev8(i), OFF_GQKV // GW3)),
                  pl.BlockSpec((T, 512), lambda i: (rev(i), OFF_GZ // 512)),
                  pl.BlockSpec((T, 128), lambda i: (rev(i), OFF_GBA // 128)),
                  pl.BlockSpec((2 * nu, G_HEADS, 128, 128), lambda i: (rev(i), 0, 0, 0)),
                  pl.BlockSpec((nu, G_HEADS, 128, 128), lambda i: (rev(i), 0, 0, 0)),
                  pl.BlockSpec((T, 512), lambda i: (rev(i), MIX_G // 512))] + _gdn_param_specs() + [ANY_SPEC],
        out_specs=[pl.BlockSpec((T, GW3 + 512), lambda i: (rev(i), OFF_GQKV // (GW3 + 512))),
                   pl.BlockSpec((T, 128), lambda i: (rev(i), 0)),
                   pl.BlockSpec((CONV_WIDTH, GW3), lambda i: (0, 0)),
                   pl.BlockSpec((8, 128), lambda i: (0, 0)),
                   pl.BlockSpec((1, 128), lambda i: (0, 0))],
        out_shape=[jax.ShapeDtypeStruct(dproj.shape, BF16),
                   jax.ShapeDtypeStruct((S_, 128), BF16), jax.ShapeDtypeStruct((CONV_WIDTH, GW3), F32),
                   jax.ShapeDtypeStruct((8, 128), F32), jax.ShapeDtypeStruct((1, 128), F32)],
        input_output_aliases={10: 0},
        scratch_shapes=[pltpu.VMEM((G_HEADS, 128, 128), F32), pltpu.VMEM((8, GW3), F32),
                        pltpu.VMEM((T, GW3), F32), pltpu.VMEM((T, 128), F32)],
        compiler_params=_cp(("arbitrary",)),
    )(proj, proj, proj, proj, states, tms, dymix, cw, _gdn_pvec(a_log, dt_bias), nw.reshape(1, 128), dproj)


def _pair_sum_windows(a, b, nsh, width, *, out_dtype, name):
    R_, C = a.shape
    hr = R_ // 2
    nb = width // 128
    assert (3 * nsh) // 128 + nb <= C // 128
    to_perm = _orig_block_to_perm()
    table = jnp.asarray([to_perm[(nsh * t) // 128 + j] for t in range(4) for j in range(nb)], jnp.int32)

    def body(tab_ref, a0_ref, a1_ref, b_ref, o_ref):
        mine = jnp.where(lax.axis_index("c") == 0, a0_ref[...], a1_ref[...])
        o_ref[...] = (mine + b_ref[...]).astype(o_ref.dtype)

    def spec(half):
        return pl.BlockSpec((hr, 128), lambda t, j, tab: (half, tab[t * nb + j]))

    return _pcall(
        body, name=name,
        grid_spec=pltpu.PrefetchScalarGridSpec(
            num_scalar_prefetch=1, grid=(4, nb), in_specs=[spec(0), spec(1), spec(0)],
            out_specs=pl.BlockSpec((None, hr, 128), lambda t, j, tab: (t, 0, j))),
        out_shape=jax.ShapeDtypeStruct((4, hr, width), out_dtype),
        compiler_params=_cp(("parallel", "parallel")))(table, a, a, b)


def _pair_sum_blocks(a, b, *, out_dtype, name):
    L, R_, C = a.shape
    hr = R_ // 2

    def body(a0_ref, a1_ref, b_ref, o_ref):
        mine = jnp.where(lax.axis_index("c") == 0, a0_ref[...], a1_ref[...])
        o_ref[...] = (mine + b_ref[...]).astype(o_ref.dtype)

    def spec(half):
        return pl.BlockSpec((None, hr, C), lambda t: (t, half, 0))

    return _pcall(body, name=name, grid=(L,), in_specs=[spec(0), spec(1), spec(0)], out_specs=spec(0),
                  out_shape=jax.ShapeDtypeStruct((L, hr, C), out_dtype),
                  compiler_params=_cp(("parallel",)))(a, a, b)


def _add_mine(a0, a1, b, *, out_dtype, tr, name):
    R_, C = b.shape

    def body(a0_ref, a1_ref, b_ref, o_ref):
        mine = jnp.where(lax.axis_index("c") == 0, a0_ref[...], a1_ref[...])
        o_ref[...] = (mine + b_ref[...]).astype(o_ref.dtype)

    spec = pl.BlockSpec((tr, C), lambda i: (i, 0))
    return _pcall(body, name=name, grid=(R_ // tr,), in_specs=[spec] * 3, out_specs=spec,
                  out_shape=jax.ShapeDtypeStruct((R_, C), out_dtype), compiler_params=_cp(("parallel",)))(a0, a1, b)


def _sum4(a, mine, *, tr, name):
    _, R_, C = a.shape

    def body(a_ref, m_ref, o_ref):
        s = 2 * lax.axis_index("x") + lax.axis_index("y")
        mv = m_ref[...].astype(F32)
        p = [jnp.where(s == t, mv, a_ref[t].astype(F32)) for t in range(4)]
        o_ref[...] = ((p[0] + p[1]) + p[2]) + p[3]

    return _pcall(body, name=name, grid=(R_ // tr,),
                  in_specs=[pl.BlockSpec((4, tr, C), lambda i: (0, i, 0)), pl.BlockSpec((tr, C), lambda i: (i, 0))],
                  out_specs=pl.BlockSpec((tr, C), lambda i: (i, 0)),
                  out_shape=jax.ShapeDtypeStruct((R_, C), F32), compiler_params=_cp(("parallel",)))(a, mine)


def _adamw_refs(w_ref, g_ref, m_ref, v_ref, d_ref, mo_ref, vo_ref):
    c1 = 1.0 / (1.0 - ADAM_B1 ** ADAM_STEP)
    c2 = 1.0 / (1.0 - ADAM_B2 ** ADAM_STEP)
    gg = g_ref[...]
    mn = ADAM_B1 * m_ref[...] + (1.0 - ADAM_B1) * gg
    vn = ADAM_B2 * v_ref[...] + (1.0 - ADAM_B2) * (gg * gg)
    mo_ref[...] = mn
    vo_ref[...] = vn
    d_ref[...] = -ADAM_LR * ((mn * c1) / (jnp.sqrt(vn * c2) + ADAM_EPS) + ADAM_WD * w_ref[...])


def _adamw_many(ws, gs, ms, vs, *, name):
    n = len(ws)

    def body(*refs):
        for k in range(n):
            _adamw_refs(*[refs[q * n + k] for q in range(7)])

    vm = pl.BlockSpec(memory_space=pltpu.VMEM)
    shp = [jax.ShapeDtypeStruct(w.shape, F32) for w in ws]
    outs = _pcall(body, name=name, in_specs=[vm] * (4 * n), out_specs=[vm] * (3 * n), out_shape=shp * 3,
                  compiler_params=pltpu.CompilerParams(vmem_limit_bytes=VMEM_LIMIT))(*ws, *gs, *ms, *vs)
    return outs[:n], outs[n:2 * n], outs[2 * n:]


def _adamw(w, g, m, v, *, tr, name):
    L, R_, C = w.shape
    body = functools.partial(_adamw_refs)

    spec = pl.BlockSpec((None, tr, C), lambda l, i: (l, i, 0))
    shp = jax.ShapeDtypeStruct((L, R_, C), F32)
    return _pcall(body, name=name, grid=(L, R_ // tr), in_specs=[spec] * 4, out_specs=[spec] * 3,
                  out_shape=[shp] * 3, compiler_params=_cp(("parallel", "parallel")))(w, g, m, v)


def _adamw_cols(w, g, m, v, *, name):
    C, L, R_ = w.shape
    tc = C // 2 if C % 2 == 0 else C

    spec = pl.BlockSpec((tc, L, 128), lambda i, j: (i, 0, j))
    shp = jax.ShapeDtypeStruct((C, L, R_), F32)
    return _pcall(functools.partial(_adamw_refs), name=name, grid=(C // tc, R_ // 128), in_specs=[spec] * 4,
                  out_specs=[spec] * 3, out_shape=[shp] * 3,
                  compiler_params=_cp(("parallel", "parallel")))(w, g, m, v)


HBM_SPEC = pl.BlockSpec(memory_space=pltpu.HBM)


def _place():
    x, y, c = lax.axis_index("x"), lax.axis_index("y"), lax.axis_index("c")
    chips = [(1 - x, y), (x, 1 - y), (1 - x, 1 - y)]
    return x, y, c, 2 * x + y, chips, [2 * cx + cy for cx, cy in chips], (x, y, 1 - c)


def _remote(src, dst, ssem, rsem, dev):
    return pltpu.make_async_remote_copy(src_ref=src, dst_ref=dst, send_sem=ssem, recv_sem=rsem,
                                        device_id=dev, device_id_type=MESH)


def _row_half(ref, lead, hc):
    hl = ref.shape[-2] // 2
    return ref.at[lead, pl.ds(hc * hl, hl), :]


def _gather_side(items):
    n = len(items)

    def copies(ins, outs, ssem, rsem):
        x, y, c, s, chips, sid, sib = _place()
        cps = [_remote(_row_half(ins[k], items[k][1], c), _row_half(outs[k], s, c),
                       ssem.at[3 * k + j], rsem.at[3 * k + j], (*chip, c))
               for k in range(n) for j, chip in enumerate(chips)]
        return cps, c, sid, sib

    def start(ins, outs, ssem, rsem):
        for cp in copies(ins, outs, ssem, rsem)[0]:
            cp.start()

    def finish(ins, outs, ssem, rsem):
        cps, c, sid, sib = copies(ins, outs, ssem, rsem)
        for k in range(n):
            for j in range(3):
                got = _row_half(outs[k], sid[j], c)
                _remote(got, got, ssem.at[3 * k + j], rsem.at[3 * k + j], sib).wait_recv()
        for cp in cps:
            cp.wait_send()

    shapes = [jax.ShapeDtypeStruct((4,) + w.shape[1:], w.dtype) for w, _ in items]
    return _Side([w for w, _ in items], shapes, 3 * n, start, finish)


def _gather_join(gathered, name):
    n = len(gathered)

    def body(*refs):
        outs, ssem, rsem = refs[n:2 * n], refs[2 * n], refs[2 * n + 1]
        x, y, c, s, chips, sid, sib = _place()
        cps = []
        for k in range(n):
            for j in range(3):
                mine = _row_half(outs[k], sid[j], c)
                cps.append(_remote(mine, mine, ssem.at[3 * k + j], rsem.at[3 * k + j], sib))
        for cp in cps:
            cp.start()
        for k in range(n):
            for j in range(3):
                other = _row_half(outs[k], sid[j], 1 - c)
                _remote(other, other, ssem.at[3 * k + j], rsem.at[3 * k + j], sib).wait_recv()
        for cp in cps:
            cp.wait_send()

    return _pcall(
        body, name=name, in_specs=[HBM_SPEC] * n, out_specs=[HBM_SPEC] * n,
        out_shape=[jax.ShapeDtypeStruct(g.shape, g.dtype) for g in gathered],
        input_output_aliases={k: k for k in range(n)},
        scratch_shapes=[pltpu.SemaphoreType.DMA((3 * n,)), pltpu.SemaphoreType.DMA((3 * n,))],
    )(*gathered)


def _gather_layer0(win, conv):
    def body(win_ref, cv_ref, gin_ref, gcv_ref, ssem, rsem):
        x, y, c, s, chips, sid, sib = _place()

        def in_half(slot, hc):
            return _row_half(gin_ref, slot, hc)

        sends = []
        for j, chip in enumerate(chips):
            dev = (*chip, c)
            sends.append(_remote(_row_half(win_ref, 0, c), in_half(s, c), ssem.at[j], rsem.at[j], dev))
            sends.append(_remote(cv_ref, gcv_ref.at[s], ssem.at[3 + j], rsem.at[3 + j], dev))
        for cp in sends:
            cp.start()
        for j in range(3):
            _remote(in_half(sid[j], c), in_half(sid[j], c), ssem.at[j], rsem.at[j], sib).wait_recv()
            f = _remote(in_half(sid[j], c), in_half(sid[j], c), ssem.at[6 + j], rsem.at[6 + j], sib)
            f.start()
            sends.append(f)
        for j in range(3):
            _remote(in_half(sid[j], 1 - c), in_half(sid[j], 1 - c), ssem.at[6 + j], rsem.at[6 + j], sib).wait_recv()
            _remote(gcv_ref.at[sid[j]], gcv_ref.at[sid[j]], ssem.at[3 + j], rsem.at[3 + j], sib).wait_recv()
        for cp in sends:
            cp.wait_send()

    return _pcall(
        body, name="gather_layer0",
        in_specs=[HBM_SPEC] * 2, out_specs=[HBM_SPEC] * 2,
        out_shape=[jax.ShapeDtypeStruct((4,) + win.shape[1:], win.dtype),
                   jax.ShapeDtypeStruct((4,) + conv.shape, conv.dtype)],
        scratch_shapes=[pltpu.SemaphoreType.DMA((9,)), pltpu.SemaphoreType.DMA((9,))],
    )(win, conv)


def _swap_halves(arrs, axes, name):
    n = len(arrs)

    def half_shape(a, ax):
        return a.shape[:ax] + (a.shape[ax] // 2,) + a.shape[ax + 1:]

    def body(*refs):
        src, dst, ssem, rsem = refs[:n], refs[n:2 * n], refs[2 * n], refs[2 * n + 1]
        x, y, c, s, chips, sid, sib = _place()
        cps = []
        for k in range(n):
            hl = src[k].shape[axes[k]] // 2
            idx = [slice(None)] * len(src[k].shape)
            idx[axes[k]] = pl.ds((1 - c) * hl, hl)
            cps.append(_remote(src[k].at[tuple(idx)], dst[k], ssem.at[k], rsem.at[k], sib))
        for cp in cps:
            cp.start()
        for cp in cps:
            cp.wait()

    return _pcall(
        body, name=name, in_specs=[HBM_SPEC] * n, out_specs=[HBM_SPEC] * n,
        out_shape=[jax.ShapeDtypeStruct(half_shape(a, ax), a.dtype) for a, ax in zip(arrs, axes)],
        scratch_shapes=[pltpu.SemaphoreType.DMA((n,)), pltpu.SemaphoreType.DMA((n,))],
    )(*arrs)


def _swap_side(arrs, axes):
    n = len(arrs)

    def copies(ins, outs, ssem, rsem):
        x, y, c, s, chips, sid, sib = _place()
        cps = []
        for k in range(n):
            hl = ins[k].shape[axes[k]] // 2
            idx = [slice(None)] * len(ins[k].shape)
            idx[axes[k]] = pl.ds((1 - c) * hl, hl)
            cps.append(_remote(ins[k].at[tuple(idx)], outs[k], ssem.at[k], rsem.at[k], sib))
        return cps

    def start(ins, outs, ssem, rsem):
        for cp in copies(ins, outs, ssem, rsem):
            cp.start()

    def finish(ins, outs, ssem, rsem):
        for cp in copies(ins, outs, ssem, rsem):
            cp.wait()

    shapes = [jax.ShapeDtypeStruct(a.shape[:ax] + (a.shape[ax] // 2,) + a.shape[ax + 1:], a.dtype)
              for a, ax in zip(arrs, axes)]
    return _Side(list(arrs), shapes, n, start, finish)


def _chips_side(arrs, per_target):
    n = len(arrs)

    def copies(ins, outs, ssem, rsem):
        x, y, c, s, chips, sid, sib = _place()
        cps = [_remote(ins[k].at[sid[j]] if per_target[k] else ins[k], outs[k].at[s],
                       ssem.at[3 * k + j], rsem.at[3 * k + j], (*chip, c))
               for k in range(n) for j, chip in enumerate(chips)]
        return cps, sid, sib

    def start(ins, outs, ssem, rsem):
        for cp in copies(ins, outs, ssem, rsem)[0]:
            cp.start()

    def finish(ins, outs, ssem, rsem):
        cps, sid, sib = copies(ins, outs, ssem, rsem)
        for k in range(n):
            for j in range(3):
                got = outs[k].at[sid[j]]
                _remote(got, got, ssem.at[3 * k + j], rsem.at[3 * k + j], sib).wait_recv()
        for cp in cps:
            cp.wait_send()

    shapes = [jax.ShapeDtypeStruct(a.shape if pt else (4,) + a.shape, a.dtype) for a, pt in zip(arrs, per_target)]
    return _Side(list(arrs), shapes, 3 * n, start, finish)


def _scatter_chips(arrs, per_target, name):
    n = len(arrs)

    def body(*refs):
        src, dst = refs[:n], refs[n:2 * n]
        ssem, rsem = refs[2 * n], refs[2 * n + 1]
        x, y, c, s, chips, sid, sib = _place()
        sends = []
        for k in range(n):
            for j, chip in enumerate(chips):
                piece = src[k].at[sid[j]] if per_target[k] else src[k]
                sends.append(_remote(piece, dst[k].at[s], ssem.at[3 * k + j], rsem.at[3 * k + j], (*chip, c)))
        for cp in sends:
            cp.start()
        for k in range(n):
            for j in range(3):
                _remote(dst[k].at[sid[j]], dst[k].at[sid[j]], ssem.at[3 * k + j], rsem.at[3 * k + j], sib).wait_recv()
        for cp in sends:
            cp.wait_send()

    outs = [jax.ShapeDtypeStruct(a.shape if pt else (4,) + a.shape, a.dtype) for a, pt in zip(arrs, per_target)]
    return _pcall(
        body, name=name, in_specs=[HBM_SPEC] * n, out_specs=[HBM_SPEC] * n, out_shape=outs,
        scratch_shapes=[pltpu.SemaphoreType.DMA((3 * n,)), pltpu.SemaphoreType.DMA((3 * n,))],
    )(*arrs)


def _swap_whole(arrs, name):
    n = len(arrs)

    def body(*refs):
        src, dst, ssem, rsem = refs[:n], refs[n:2 * n], refs[2 * n], refs[2 * n + 1]
        *_, sib = _place()
        cps = [_remote(src[k], dst[k], ssem.at[k], rsem.at[k], sib) for k in range(n)]
        for cp in cps:
            cp.start()
        for cp in cps:
            cp.wait()

    return _pcall(
        body, name=name, in_specs=[HBM_SPEC] * n, out_specs=[HBM_SPEC] * n,
        out_shape=[jax.ShapeDtypeStruct(a.shape, a.dtype) for a in arrs],
        scratch_shapes=[pltpu.SemaphoreType.DMA((n,)), pltpu.SemaphoreType.DMA((n,))],
    )(*arrs)


def _perm_cols(w):
    parts = [w[..., int(_ORIG_OFF[oi]):int(_ORIG_OFF[oi]) + IN_SIZES[oi]] for oi, _ in _PIECES]
    parts.append(jnp.zeros(w.shape[:-1] + (NP - N_IN,), w.dtype))
    return jnp.concatenate(parts, -1)


def _perm_rows(w):
    return jnp.concatenate([w[..., 512:1536, :], w[..., 0:512, :], w[..., 1536:2048, :]], -2)


_SMALL = ("sinks", "r_conv_b", "r_wa", "r_ba", "r_wx", "r_bx", "r_lam", "g_a_log", "g_dt_bias", "g_norm_w",
          "ln_g", "ln_b", "r_conv_w", "g_conv_w")
_PACK_ROWS = 16


def _piece_rows(n):
    return -(-n // (128 * _PACK_ROWS)) * _PACK_ROWS


def _pack(arrs):
    parts = []
    for a in arrs:
        n = int(np.prod(a.shape))
        rows = _piece_rows(n)
        if n % 128 == 0:
            blk = a.reshape(n // 128, 128)
        else:
            blk = jnp.pad(a.reshape(1, n), ((0, 0), (0, (-n) % 128))).reshape(-1, 128)
        if blk.shape[0] < rows:
            blk = jnp.pad(blk, ((0, rows - blk.shape[0]), (0, 0)))
        parts.append(blk)
    return jnp.concatenate(parts, 0)


def _unpack(packed, shapes):
    out = []
    r = 0
    for shp in shapes:
        n = int(np.prod(shp))
        if n % 128 == 0:
            out.append(packed[r:r + n // 128].reshape(shp))
        else:
            nr = -(-n // 128)
            out.append(packed[r:r + nr].reshape(1, nr * 128)[:, :n].reshape(shp))
        r += _piece_rows(n)
    return out


def _tile(n, t):
    return min(n, t)


def _layer_fwd(l, x, xb, wb, wob, ln, rope_c, rope_s, p, side=None, target=None):
    S_ = x.shape[0]
    proj = _matmul(xb, wb, ta=False, tb=False, tm=_tile(S_, 1024), tn=NP // 4, tk=wb.shape[0], out_dtype=F32,
                   name=f"in_proj_{l}", side=side)
    side_out = None
    if side:
        proj, side_out = proj
    h, ymix = _rglru_fwd(proj, p["r_conv_w"], p["r_conv_b"], p["r_wa"], p["r_ba"], p["r_wx"], p["r_bx"], p["r_lam"],
                         T=_tile(S_, 256), name=f"rglru_fwd_{l}")
    ymix = _attn_fwd(proj, rope_c, rope_s, p["sinks"], ymix, T=_tile(S_, 512), name=f"attn_fwd_{l}")
    ymix, st, tms = _gdn_fwd(proj, p["g_conv_w"], p["g_a_log"], p["g_dt_bias"], p["g_norm_w"], ymix,
                             T=_tile(S_, 256), name=f"gdn_fwd_{l}")
    out = _outproj(ymix, wob(side_out), x, ln[0], ln[1], tm=_tile(S_, 256), name=f"out_proj_{l}", target=target)
    sv = dict(proj=proj, h=h, st=st, tms=tms, ymix=ymix)
    if target is None:
        sv["z"], sv["y"], sv["yb"], sv["ybt"] = out
    else:
        sv["head"] = out
    return sv


def _layer_bwd(l, sv, x_bt, dz, dzb, wb, wob, rope_c, rope_s, p, side_dmix=None, side_dw_in=None, side_dx=None):
    S_, D = dz.shape
    proj = sv["proj"]
    dwo = _matmul(sv["ymix"], dzb, ta=True, tb=False, tm=512, tn=_tile(D, 2048), tk=_tile(S_, 1024),
                  out_dtype=F32, name=f"dw_out_{l}",
                  out_blocks=((MIX_WIDTH, D), (512, _tile(D, 2048)),
                              lambda i, j: (jnp.where(i == 3, 3, (i + 1) % 3), j)))
    side = side_dmix(dwo) if side_dmix else None
    dymix = _matmul(dzb, wob, ta=False, tb=True, tm=_tile(S_, 1024), tn=512, tk=D, out_dtype=F32,
                    name=f"dmix_{l}", side=side)
    out_dmix = None
    if side:
        dymix, out_dmix = dymix
    dproj, dk, dv, dkt, dvt, dsk = _attn_bwd(proj, rope_c, rope_s, p["sinks"], dymix, T=_tile(S_, 512),
                                             name=f"attn_bwd_{l}")
    (dproj, dcw_r, dcb_r, dwa, dba, dwx, dbx, dlam) = _rglru_bwd(
        proj, sv["h"], dymix, dproj, p["r_conv_w"], p["r_conv_b"], p["r_wa"], p["r_ba"], p["r_wx"], p["r_bx"],
        p["r_lam"], T=_tile(S_, 256), name=f"rglru_bwd_{l}")
    dproj, dgba, dcw_g, dpv, dnw = _gdn_bwd(proj, sv["st"], sv["tms"], dymix, dproj, p["g_conv_w"], p["g_a_log"],
                                            p["g_dt_bias"], p["g_norm_w"], T=_tile(S_, 256), name=f"gdn_bwd_{l}")
    tail = jnp.concatenate([dk[128:], dkt, dv[128:], dvt], 0).reshape(2, S_, 128)
    tail = jnp.concatenate([tail[0], tail[1], dgba, jnp.zeros((S_, NP - OFF_GBA - 128), BF16)], 1)
    dproj = lax.dynamic_update_slice(dproj, tail, (0, OFF_AK))
    small = dict(sinks=dsk[:, 0], r_conv_b=dcb_r[0], r_wa=dwa, r_ba=dba[0], r_wx=dwx, r_bx=dbx[0], r_lam=dlam[0],
                 g_a_log=dpv[0, 4:8], g_dt_bias=dpv[1, 4:8], g_norm_w=dnw[0], r_conv_w=dcw_r, g_conv_w=dcw_g)
    side = side_dw_in(small, dwo, out_dmix) if side_dw_in else None
    dwin = _matmul(x_bt, dproj, ta=False, tb=False, tm=_tile(D, 1024), tn=NP // 4, tk=_tile(S_, 1024),
                   out_dtype=F32, name=f"dw_in_{l}", side=side)
    out_dw_in = None
    if side:
        dwin, out_dw_in = dwin
    side = side_dx(dwin) if side_dx else None
    tmx = _tile(S_, 1024)
    nblk = S_ // tmx
    dx_args = dict(ta=False, tb=True, tm=tmx, tn=_tile(D, 1024), tk=NP // 2, out_dtype=F32, extra=dz,
                   alpha=DEEPNORM_ALPHA)
    out_dx = None
    if side and nblk >= 4:
        head = nblk - nblk // 4
        dx, out_dx = _matmul(dproj, wb, name=f"dx_{l}", side=side, rows=(0, head), **dx_args)
        dx = _matmul(dproj, wb, name=f"dx_{l}_rest", rows=(head, nblk - head), into=dx, **dx_args)
    elif side:
        dx, out_dx = _matmul(dproj, wb, name=f"dx_{l}", side=side, **dx_args)
    else:
        dx = _matmul(dproj, wb, name=f"dx_{l}", **dx_args)
    return dx, dwin, dwo, small, out_dw_in, out_dx


def kernel(x, w_in, sinks, r_conv_w, r_conv_b, r_wa, r_ba, r_wx, r_bx, r_lam, g_conv_w, g_a_log, g_dt_bias, g_norm_w, w_out, ln_g, ln_b, loss_target, m_w_in, m_sinks, m_r_conv_w, m_r_conv_b, m_r_wa, m_r_ba, m_r_wx, m_r_bx, m_r_lam, m_g_conv_w, m_g_a_log, m_g_dt_bias, m_g_norm_w, m_w_out, m_ln_g, m_ln_b, v_w_in, v_sinks, v_r_conv_w, v_r_conv_b, v_r_wa, v_r_ba, v_r_wx, v_r_bx, v_r_lam, v_g_conv_w, v_g_a_log, v_g_dt_bias, v_g_norm_w, v_w_out, v_ln_g, v_ln_b):
    S_, D = x.shape[1], x.shape[2]
    nsh = w_in.shape[2]
    rsh = w_out.shape[1]
    cx, cy, cc = lax.axis_index("x"), lax.axis_index("y"), lax.axis_index("c")
    chip = 2 * cx + cy
    rcw_n, gcw_n = r_conv_w.shape[2], g_conv_w.shape[2]

    conv_pack = jnp.concatenate([r_conv_w, g_conv_w], 2)
    w_in_b, w_out_b = w_in.astype(BF16), w_out.astype(BF16)
    g_in0, g_conv = _gather_layer0(w_in_b, conv_pack)

    def shards(own, got):
        return [jnp.where(chip == t, own, got[t]) for t in range(4)]

    def w_in_of(l, g_in):
        return _perm_cols(jnp.concatenate(shards(w_in_b[l], g_in), 1))

    def w_out_of(l, g_out):
        return _perm_rows(jnp.concatenate(shards(w_out_b[l], g_out), 0))

    rcw = jnp.concatenate(shards(r_conv_w, g_conv[:, :, :, :rcw_n]), 2)
    gcw = jnp.concatenate(shards(g_conv_w, g_conv[:, :, :, rcw_n:]), 2)

    pos = jnp.arange(S_, dtype=F32)[:, None]
    inv = 1.0 / (ROPE_THETA ** (jnp.arange(0, A_HEAD_DIM, 2, dtype=F32) / A_HEAD_DIM))
    ang = pos * inv[None, :]
    cos, sin = jnp.cos(ang), jnp.sin(ang)
    rope_c = jnp.concatenate([cos, cos, cos, cos], 1)
    rope_s = jnp.concatenate([-sin, sin, -sin, sin], 1)

    def params(l):
        return dict(sinks=sinks[l], r_conv_w=rcw[l], r_conv_b=r_conv_b[l], r_wa=r_wa[l], r_ba=r_ba[l],
                    r_wx=r_wx[l], r_bx=r_bx[l], r_lam=r_lam[l], g_conv_w=gcw[l], g_a_log=g_a_log[l],
                    g_dt_bias=g_dt_bias[l], g_norm_w=g_norm_w[l])

    assert DEPTH == 2
    xb0 = x[0].astype(BF16)
    wb, wob = [w_in_of(0, g_in0), None], [None, None]
    late = {}

    def w_out_0(arrived):
        late["w_in_1"], g_out0 = _gather_join(arrived, "gather_join_0")
        wob[0] = w_out_of(0, g_out0)
        return wob[0]

    def w_out_1(arrived):
        wob[1] = w_out_of(1, _gather_join(arrived, "gather_join_1")[0])
        return wob[1]

    sv0 = _layer_fwd(0, x[0], xb0, wb[0], w_out_0, (ln_g[0], ln_b[0]), rope_c, rope_s, params(0),
                     side=_gather_side([(w_in_b, 1), (w_out_b, 0)]))
    wb[1] = w_in_of(1, late["w_in_1"])
    sv1 = _layer_fwd(1, sv0["y"], sv0["yb"], wb[1], w_out_1, (ln_g[1], ln_b[1]), rope_c, rope_s, params(1),
                     side=_gather_side([(w_out_b, 1)]), target=loss_target[0])
    saved, xbs = [sv0, sv1], [xb0.T, sv0["ybt"]]

    tm_ln = _tile(S_, 256)
    dz, dzb, dg_l, db_l, loss_part = saved[-1]["head"]
    assert DEPTH == 2
    wcov = (-(-nsh // 128) + 1) * 128
    names = list(_SMALL)

    def own(a):
        return lax.dynamic_index_in_dim(a, chip, 0, keepdims=False)

    def sum_in(l, cp, arrived):
        return _sum4(arrived, own(cp), tr=_tile(D // 2, 256), name=f"chip_sum_w_in_{l}")

    def sum_out(l, cp, arrived):
        return _sum4(arrived, own(cp), tr=rsh // 2, name=f"chip_sum_w_out_{l}")

    dlng, dlnb = [None, dg_l[0]], [None, db_l[0]]
    dx, dwin1, dwo1, small1, _, _ = _layer_bwd(1, saved[1], xbs[1], dz, dzb, wb[1], wob[1], rope_c, rope_s, params(1))
    dwo1_4 = dwo1.reshape(4, rsh, D)
    dz, dzb, dg_l, db_l, _ = _ln_bwd(saved[0]["z"], ln_g[0], ln_b[0], dx, tm=tm_ln, name="ln_bwd_0")
    dlng[0], dlnb[0] = dg_l[0], db_l[0]

    held = {}

    def side_dmix(dwo0):
        return _swap_side([dwin1, dwo1_4, dwo0.reshape(4, rsh, D)], [0, 1, 1])

    def side_dw_in(small0, dwo0, got):
        sm = {k: jnp.stack([small0[k], small1[k]]) for k in small0}
        sm["ln_g"], sm["ln_b"] = jnp.stack(dlng), jnp.stack(dlnb)
        gs = _pack([sm[n] for n in names])
        (got_s,) = _swap_halves([gs], [0], "reduce_pair_small")
        held["in_cp1"] = _pair_sum_windows(dwin1, got[0], nsh, wcov, out_dtype=BF16, name="pair_sum_w_in_1")
        held["out_cp1"] = _pair_sum_blocks(dwo1_4, got[1], out_dtype=BF16, name="pair_sum_w_out_1")
        held["out_cp0"] = _pair_sum_blocks(dwo0.reshape(4, rsh, D), got[2], out_dtype=BF16, name="pair_sum_w_out_0")
        held["s_cp"] = _pair_sum_blocks(gs[None], got_s[None], out_dtype=F32, name="pair_sum_small")[0]
        held["shapes"] = [sm[n].shape for n in names]
        return _chips_side([held["in_cp1"], held["out_cp1"], held["out_cp0"], held["s_cp"]],
                           [True, True, True, False])

    def side_dx(dwin0):
        got = _swap_halves([dwin0], [0], "reduce_pair_0b")
        held["in_cp0"] = _pair_sum_windows(dwin0, got[0], nsh, wcov, out_dtype=BF16, name="pair_sum_w_in_0")
        return _chips_side([held["in_cp0"]], [True])

    dx, _, _, _, arrived_a, arrived_b = _layer_bwd(0, saved[0], xbs[0], dz, dzb, wb[0], wob[0], rope_c, rope_s,
                                                   params(0), side_dmix=side_dmix, side_dw_in=side_dw_in,
                                                   side_dx=side_dx)
    grad_x = dx[None]
    loss = lax.psum(loss_part[0, 0], ("x", "y", "c"))
    s_cp = held["s_cp"]
    mine = [sum_in(0, held["in_cp0"], arrived_b[0]), sum_out(0, held["out_cp0"], arrived_a[2]),
            sum_in(1, held["in_cp1"], arrived_a[0]), sum_out(1, held["out_cp1"], arrived_a[1]),
            _sum4(arrived_a[3], s_cp, tr=s_cp.shape[0], name="chip_sum_small")]
    other = _swap_whole(mine, "reduce_join")

    def both(k, axis):
        return jnp.where(cc == 0, jnp.concatenate([mine[k], other[k]], axis),
                         jnp.concatenate([other[k], mine[k]], axis))

    g_w_in = lax.dynamic_slice_in_dim(jnp.stack([both(2 * l, 0) for l in range(DEPTH)]), (nsh * chip) % 128, nsh, 2)
    g_w_out = jnp.stack([both(2 * l + 1, 0) for l in range(DEPTH)])
    g_small = both(2 * DEPTH, 0)

    gsm = dict(zip(names, _unpack(g_small, held["shapes"])))
    gsm["r_conv_w"] = lax.dynamic_slice_in_dim(gsm["r_conv_w"], chip * rcw_n, rcw_n, 2)
    gsm["g_conv_w"] = lax.dynamic_slice_in_dim(gsm["g_conv_w"], chip * gcw_n, gcw_n, 2)
    wts = dict(sinks=sinks, r_conv_w=r_conv_w, r_conv_b=r_conv_b, r_wa=r_wa, r_ba=r_ba, r_wx=r_wx, r_bx=r_bx,
               r_lam=r_lam, g_conv_w=g_conv_w, g_a_log=g_a_log, g_dt_bias=g_dt_bias, g_norm_w=g_norm_w,
               ln_g=ln_g, ln_b=ln_b)
    mom = dict(sinks=m_sinks, r_conv_w=m_r_conv_w, r_conv_b=m_r_conv_b, r_wa=m_r_wa, r_ba=m_r_ba, r_wx=m_r_wx,
               r_bx=m_r_bx, r_lam=m_r_lam, g_conv_w=m_g_conv_w, g_a_log=m_g_a_log, g_dt_bias=m_g_dt_bias,
               g_norm_w=m_g_norm_w, ln_g=m_ln_g, ln_b=m_ln_b)
    vel = dict(sinks=v_sinks, r_conv_w=v_r_conv_w, r_conv_b=v_r_conv_b, r_wa=v_r_wa, r_ba=v_r_ba, r_wx=v_r_wx,
               r_bx=v_r_bx, r_lam=v_r_lam, g_conv_w=v_g_conv_w, g_a_log=v_g_a_log, g_dt_bias=v_g_dt_bias,
               g_norm_w=v_g_norm_w, ln_g=v_ln_g, ln_b=v_ln_b)
    d_s, m_s, v_s = _adamw_many(*[[d[n] for n in names] for d in (wts, gsm, mom, vel)], name="adamw_small")
    d_sm, m_sm, v_sm = (dict(zip(names, a)) for a in (d_s, m_s, v_s))
    def cols(a):
        return jnp.transpose(a, (2, 0, 1))

    g_w_in_t = cols(g_w_in)
    outs_t = _adamw_cols(cols(w_in), g_w_in_t, cols(m_w_in), cols(v_w_in), name="adamw_w_in")
    d_in, m_in, v_in = (jnp.transpose(a, (1, 2, 0)) for a in outs_t)
    g_w_in = jnp.transpose(g_w_in_t, (1, 2, 0))
    d_out, m_out, v_out = _adamw(w_out, g_w_out, m_w_out, v_w_out, tr=256, name="adamw_w_out")

    order = ["w_in", "sinks", "r_conv_w", "r_conv_b", "r_wa", "r_ba", "r_wx", "r_bx", "r_lam", "g_conv_w",
             "g_a_log", "g_dt_bias", "g_norm_w", "w_out", "ln_g", "ln_b"]
    grads = dict(gsm, w_in=g_w_in, w_out=g_w_out)
    deltas = dict(d_sm, w_in=d_in, w_out=d_out)
    new_m = dict(m_sm, w_in=m_in, w_out=m_out)
    new_v = dict(v_sm, w_in=v_in, w_out=v_out)
    return (loss, grad_x, *[grads[n] for n in order], *[deltas[n] for n in order],
            *[new_m[n] for n in order], *[new_v[n] for n in order])
```

```python
import functools
import math

import jax
import jax.numpy as jnp
import numpy as np
from jax import lax
from jax.experimental import pallas as pl
from jax.experimental.pallas import tpu as pltpu

F32 = jnp.float32
BF16 = jnp.bfloat16
MESH = pl.DeviceIdType.MESH

DEPTH = 2
A_HEADS, A_KV_HEADS, A_HEAD_DIM = 8, 2, 64
A_WIDTH, A_KV_WIDTH = 512, 128
WINDOW = 128
ROPE_THETA = 10000.0
R_WIDTH, R_BLOCKS, R_BLOCK_DIM, R_C = 1024, 8, 128, 8.0
CONV_WIDTH = 4
G_HEADS, G_HEAD_DIM, G_WIDTH, G_CHUNK = 4, 128, 512, 64
MIX_WIDTH = 2048
IN_SIZES = (512, 128, 128, 512, 1024, 1024, 512, 512, 512, 512, 4, 4)
N_IN = 5384
DEEPNORM_ALPHA = (2 * DEPTH) ** 0.25
LN_EPS = 1e-5
RMS_EPS = 1e-6
ADAM_LR, ADAM_B1, ADAM_B2, ADAM_EPS, ADAM_WD, ADAM_STEP = 0.001, 0.9, 0.999, 1e-08, 0.01, 10

NP = 5632
OFF_GQKV, OFF_GZ, OFF_RX, OFF_RZ, OFF_AQ, OFF_AZ, OFF_AK, OFF_AV, OFF_GBA = (
    0, 1536, 2048, 3072, 4096, 4608, 5120, 5248, 5376)
_ORIG_OFF = np.concatenate([[0], np.cumsum(IN_SIZES)])[:-1]
_PIECES = ((6, OFF_GQKV), (7, OFF_GQKV + 512), (8, OFF_GQKV + 1024), (9, OFF_GZ), (4, OFF_RX), (5, OFF_RZ),
           (0, OFF_AQ), (3, OFF_AZ), (1, OFF_AK), (2, OFF_AV), (10, OFF_GBA), (11, OFF_GBA + 4))


def _orig_block_to_perm():
    table = list(range(NP // 128))
    for oi, off in _PIECES:
        if IN_SIZES[oi] % 128 == 0:
            for k in range(IN_SIZES[oi] // 128):
                table[int(_ORIG_OFF[oi]) // 128 + k] = off // 128 + k
    return table
MIX_R, MIX_A, MIX_G = 0, 1024, 1536
VMEM_LIMIT = 56 * 1024 * 1024
ANY_SPEC = pl.BlockSpec(memory_space=pl.ANY)


def _pcall(body, **kw):
    return pl.pallas_call(body, **kw)


def _cp(sem, limit=VMEM_LIMIT):
    return pltpu.CompilerParams(dimension_semantics=sem, vmem_limit_bytes=limit)


def _sigmoid(x):
    return 0.5 + 0.5 * jnp.tanh(0.5 * x)


def _silu(x):
    return x * _sigmoid(x)


def _dsilu(x):
    s = _sigmoid(x)
    return s * (1.0 + x * (1.0 - s))


def _log1p(x):
    u = 1.0 + x
    d = jnp.where(u == 1.0, 1.0, u - 1.0)
    return jnp.where(u == 1.0, x, jnp.log(u) * (x / d))


def _softplus(x):
    return jnp.maximum(x, 0.0) + _log1p(jnp.exp(-jnp.abs(x)))


def _one_minus_exp(x):
    series = -x * (1.0 + x * (0.5 + x * (1.0 / 6.0 + x * (1.0 / 24.0))))
    return jnp.where(x > -0.05, series, 1.0 - jnp.exp(x))


def _nn(a, b):
    return lax.dot_general(a, b, (((1,), (0,)), ((), ())), preferred_element_type=F32)


def _nt(a, b):
    return lax.dot_general(a, b, (((1,), (1,)), ((), ())), preferred_element_type=F32)


def _tn(a, b):
    return lax.dot_general(a, b, (((0,), (0,)), ((), ())), preferred_element_type=F32)


def _b(x):
    return x.astype(BF16)


def _split3(x):
    hi = x.astype(BF16)
    r1 = x - hi.astype(F32)
    mid = r1.astype(BF16)
    lo = (r1 - mid.astype(F32)).astype(BF16)
    return hi, mid, lo


def _dot3(f, a, b):
    ah, am, _ = _split3(a)
    bh, bm, _ = _split3(b)
    return f(ah, bh) + (f(ah, bm) + f(am, bh))


def _dot_exact_lhs(f, a_bf16, b):
    bh, bm, bl = _split3(b)
    return f(a_bf16, bh) + (f(a_bf16, bm) + f(a_bf16, bl))


def _rot(x):
    w = x.shape[-1]
    lane = lax.broadcasted_iota(jnp.int32, (1, w), 1)
    return jnp.where((lane & 63) < 32, pltpu.roll(x, w - 32, 1), pltpu.roll(x, 32, 1))


def _conv_taps(ext, n):
    return [pltpu.roll(ext, 3 - k, 0)[8:8 + n] if k < 3 else ext[8:8 + n] for k in range(CONV_WIDTH)]


def _conv_taps_t(ext, n):
    m = ext.shape[0]
    return [pltpu.roll(ext, m - (3 - k), 0)[0:n] if k < 3 else ext[0:n] for k in range(CONV_WIDTH)]


def _scan_steps(a, b, pos, span, shifts, reverse):
    n = a.shape[0]
    for s in shifts:
        if reverse:
            a_sh = pltpu.roll(a, n - s, 0)
            b_sh = pltpu.roll(b, n - s, 0)
            ok = pos < (span - s)
        else:
            a_sh = pltpu.roll(a, s, 0)
            b_sh = pltpu.roll(b, s, 0)
            ok = pos >= s
        b = jnp.where(ok, a * b_sh + b, b)
        a = jnp.where(ok, a * a_sh, a)
    return a, b


def _scan_lin(a, b, reverse):
    n = a.shape[0]
    shifts = []
    s = 1
    while s < n:
        shifts.append(s)
        s *= 2
    return _scan_steps(a, b, lax.broadcasted_iota(jnp.int32, (n, 1), 0), n, shifts, reverse)


class _Side:
    def __init__(self, inputs, out_shapes, n_sems, start, finish):
        self.inputs, self.out_shapes, self.n_sems, self.start, self.finish = inputs, out_shapes, n_sems, start, finish


def _matmul(a, b, *, ta, tb, tm, tn, tk, out_dtype, name, extra=None, alpha=0.0, out_blocks=None, side=None,
            rows=None, into=None):
    if ta:
        K, M = a.shape
    else:
        M, K = a.shape
    if tb:
        N, K2 = b.shape
    else:
        K2, N = b.shape
    assert K == K2 and M % tm == 0 and N % tn == 0 and K % tk == 0, (a.shape, b.shape, tm, tn, tk)
    nk = K // tk
    ca = 0 if ta else 1
    cb = 1 if tb else 0
    has_extra = extra is not None

    assert nk == 1 or out_dtype == F32
    assert rows is None or (not ta and out_blocks is None)
    r0, nrow = rows if rows else (0, M // tm)
    n_in = 2 + int(has_extra) + int(into is not None)
    ns_in = len(side.inputs) if side else 0
    ns_out = len(side.out_shapes) if side else 0
    grid = (nrow, N // tn, nk)

    def body(*refs):
        a_ref, b_ref = refs[0], refs[1]
        e_ref = refs[2] if has_extra else None
        o_ref = refs[n_in + ns_in]
        k = pl.program_id(2)
        if side:
            s_in = refs[n_in:n_in + ns_in]
            s_out = refs[n_in + ns_in + 1:n_in + ns_in + 1 + ns_out]
            ssem, rsem = refs[-2], refs[-1]
            i, j = pl.program_id(0), pl.program_id(1)

            @pl.when((i == 0) & (j == 0) & (k == 0))
            def _():
                side.start(s_in, s_out, ssem, rsem)

            @pl.when((i == grid[0] - 1) & (j == grid[1] - 1) & (k == grid[2] - 1))
            def _():
                side.finish(s_in, s_out, ssem, rsem)

        part = lax.dot_general(a_ref[...], b_ref[...], (((ca,), (cb,)), ((), ())), preferred_element_type=F32)
        if nk == 1:
            if e_ref is not None:
                part = part + alpha * e_ref[...]
            o_ref[...] = part.astype(o_ref.dtype)
            return

        @pl.when(k == 0)
        def _():
            o_ref[...] = part

        @pl.when((k > 0) & (k < nk - 1))
        def _():
            o_ref[...] += part

        @pl.when(k == nk - 1)
        def _():
            last = o_ref[...] + part
            if e_ref is not None:
                last = last + alpha * e_ref[...]
            o_ref[...] = last

    a_spec = (pl.BlockSpec((tk, tm), lambda i, j, k: (k, i)) if ta
              else pl.BlockSpec((tm, tk), lambda i, j, k: (i + r0, k)))
    b_spec = (pl.BlockSpec((tn, tk), lambda i, j, k: (j, k)) if tb
              else pl.BlockSpec((tk, tn), lambda i, j, k: (k, j)))
    e_spec = pl.BlockSpec((tm, tn), lambda i, j, k: (i + r0, j))
    if out_blocks is None:
        o_spec, o_shape = e_spec, (M, N)
    else:
        o_shape, o_block, o_map = out_blocks
        o_spec = pl.BlockSpec(o_block, lambda i, j, k: o_map(i, j))
    in_specs = [a_spec, b_spec] + ([e_spec] if has_extra else []) + ([ANY_SPEC] if into is not None else [])
    args = (a, b) + ((extra,) if has_extra else ()) + ((into,) if into is not None else ())
    alias = {n_in - 1: 0} if into is not None else {}
    if not side:
        return _pcall(
            body, name=name, grid=grid, in_specs=in_specs, out_specs=o_spec,
            out_shape=jax.ShapeDtypeStruct(o_shape, out_dtype), input_output_aliases=alias,
            compiler_params=_cp(("parallel", "parallel", "arbitrary")),
        )(*args)
    outs = _pcall(
        body, name=name, grid=grid, in_specs=in_specs + [HBM_SPEC] * ns_in,
        out_specs=[o_spec] + [HBM_SPEC] * ns_out,
        out_shape=[jax.ShapeDtypeStruct(o_shape, out_dtype)] + list(side.out_shapes), input_output_aliases=alias,
        scratch_shapes=[pltpu.SemaphoreType.DMA((side.n_sems,)), pltpu.SemaphoreType.DMA((side.n_sems,))],
        compiler_params=_cp(("arbitrary", "arbitrary", "arbitrary")),
    )(*args, *side.inputs)
    return outs[0], outs[1:]


def _ln_stats(z):
    mu = jnp.mean(z, -1, keepdims=True)
    zc = z - mu
    var = jnp.mean(zc * zc, -1, keepdims=True)
    rstd = lax.rsqrt(var + LN_EPS)
    return zc * rstd, rstd


def _ln_bwd_tile(z, gam, bet, other, from_target, dz_ref, dzb_ref, dg_ref, db_ref, loss_ref):
    i = pl.program_id(0)

    @pl.when(i == 0)
    def _():
        dg_ref[...] = jnp.zeros_like(dg_ref)
        db_ref[...] = jnp.zeros_like(db_ref)
        loss_ref[...] = jnp.zeros_like(loss_ref)

    xh, rstd = _ln_stats(z)
    if from_target:
        err = xh * gam + bet - other
        per_tok = jnp.mean(err * err, -1, keepdims=True)
        loss_ref[...] += 0.5 * jnp.sum(per_tok, 0, keepdims=True)
        dy = err * (1.0 / z.shape[-1])
    else:
        dy = other
    dxh = dy * gam
    m1 = jnp.mean(dxh, -1, keepdims=True)
    m2 = jnp.mean(dxh * xh, -1, keepdims=True)
    dz = rstd * (dxh - m1 - xh * m2)
    dz_ref[...] = dz
    dzb_ref[...] = dz.astype(BF16)
    dg_ref[...] += jnp.sum(dy * xh, 0, keepdims=True)
    db_ref[...] += jnp.sum(dy, 0, keepdims=True)


def _outproj(ymix, wo, x, g, b, *, tm, name, target=None):
    S_, D = x.shape
    last = target is not None

    def body(*refs):
        y_ref, w_ref, x_ref, g_ref, b_ref = refs[:5]
        z = DEEPNORM_ALPHA * x_ref[...] + _nn(y_ref[...], w_ref[...])
        if last:
            _ln_bwd_tile(z, g_ref[...], b_ref[...], refs[5][...], True, *refs[6:])
            return
        z_ref, o_ref, ob_ref = refs[5:]
        z_ref[...] = z
        xh, _ = _ln_stats(z)
        y = xh * g_ref[...] + b_ref[...]
        o_ref[...] = y
        ob_ref[...] = y.astype(BF16)

    row = pl.BlockSpec((tm, D), lambda i: (i, 0))
    vec = pl.BlockSpec((1, D), lambda i: (0, 0))
    one = pl.BlockSpec((1, 1), lambda i: (0, 0))
    in_specs = [pl.BlockSpec((tm, MIX_WIDTH), lambda i: (i, 0)), pl.BlockSpec((MIX_WIDTH, D), lambda i: (0, 0)),
                row, vec, vec]
    f32s, b16s = jax.ShapeDtypeStruct((S_, D), F32), jax.ShapeDtypeStruct((S_, D), BF16)
    v32s = jax.ShapeDtypeStruct((1, D), F32)
    args = (ymix, wo, x, g.reshape(1, D), b.reshape(1, D))
    if last:
        return _pcall(body, name=name, grid=(S_ // tm,), in_specs=in_specs + [row],
                      out_specs=[row, row, vec, vec, one],
                      out_shape=[f32s, b16s, v32s, v32s, jax.ShapeDtypeStruct((1, 1), F32)],
                      compiler_params=_cp(("arbitrary",)))(*args, target)
    return _pcall(body, name=name, grid=(S_ // tm,), in_specs=in_specs, out_specs=[row, row, row],
                  out_shape=[f32s, f32s, b16s], compiler_params=_cp(("parallel",)))(*args)


def _ln_bwd(z, g, b, dy, *, tm, name):
    S_, D = z.shape

    def body(z_ref, g_ref, b_ref, o_ref, *outs):
        _ln_bwd_tile(z_ref[...], g_ref[...], b_ref[...], o_ref[...], False, *outs)

    row = pl.BlockSpec((tm, D), lambda i: (i, 0))
    vec = pl.BlockSpec((1, D), lambda i: (0, 0))
    one = pl.BlockSpec((1, 1), lambda i: (0, 0))
    return _pcall(
        body, name=name, grid=(S_ // tm,), in_specs=[row, vec, vec, row],
        out_specs=[row, row, vec, vec, one],
        out_shape=[jax.ShapeDtypeStruct((S_, D), F32), jax.ShapeDtypeStruct((S_, D), BF16),
                   jax.ShapeDtypeStruct((1, D), F32), jax.ShapeDtypeStruct((1, D), F32),
                   jax.ShapeDtypeStruct((1, 1), F32)],
        compiler_params=_cp(("arbitrary",)),
    )(z, g.reshape(1, D), b.reshape(1, D), dy)


def _attn_masks(i, sk_ref):
    ri = lax.broadcasted_iota(jnp.int32, (512, 256), 0)
    cj = lax.broadcasted_iota(jnp.int32, (512, 256), 1)
    diff = (ri & 127) - cj + 128
    band = (diff >= 0) & (diff < WINDOW)
    bias = jnp.where(band, 0.0, -jnp.inf)
    bias0 = jnp.where(band & ((i > 0) | (cj >= 128)), 0.0, -jnp.inf)
    grp = lax.broadcasted_iota(jnp.int32, (512, 1), 0) >> 7
    skvs = []
    for h in range(A_KV_HEADS):
        skv = jnp.zeros((512, 1), F32)
        for g in range(4):
            skv = jnp.where(grp == g, sk_ref[h * 4 + g], skv)
        skvs.append(skv)
    return bias0, bias, skvs


def _attn_common(masks, b, h, qr, kd, vd):
    lane = lax.broadcasted_iota(jnp.int32, (1, 128), 1)
    lof = (lane < 64).astype(F32)
    hif = 1.0 - lof
    r0 = b * 128
    skv = masks[2][h]
    pairs = [qr[r0:r0 + 128, h * 256 + p * 128:h * 256 + (p + 1) * 128] for p in (0, 1)]
    qs = _b(jnp.concatenate([pairs[0] * lof, pairs[0] * hif, pairs[1] * lof, pairs[1] * hif], 0))
    k2 = kd[h][r0:r0 + 256]
    v2 = vd[h][r0:r0 + 256]
    s = _nt(qs, k2) * (A_HEAD_DIM ** -0.5) + (masks[0] if b == 0 else masks[1])
    m = jnp.maximum(jnp.max(s, 1, keepdims=True), skv)
    p = jnp.exp(s - m)
    esk = jnp.exp(skv - m)
    rz = 1.0 / (jnp.sum(p, 1, keepdims=True) + esk)
    prob = p * rz
    o4 = _nn(_b(prob), v2)
    return lof, hif, qs, k2, v2, prob, esk * rz, o4


def _attn_prep(T, q_ref, k_ref, v_ref, c_ref, s_ref, kprev, vprev):
    C = c_ref[...]
    Sg = s_ref[...]
    C4 = jnp.concatenate([C] * 4, 1)
    S4 = jnp.concatenate([Sg] * 4, 1)
    q = q_ref[...]
    qr = q * C4 + _rot(q) * S4
    k = k_ref[...]
    kr = k * C + _rot(k) * Sg
    v = v_ref[...]
    kext = jnp.concatenate([kprev[...], kr], 0)
    vext = jnp.concatenate([vprev[...], v], 0)
    kprev[...] = kr[T - 128:]
    vprev[...] = v[T - 128:]
    lo = lax.broadcasted_iota(jnp.int32, (1, 128), 1) < 64
    kroll = pltpu.roll(kext, 64, 1)
    vroll = pltpu.roll(vext, 64, 1)
    kd = [_b(jnp.where(lo, kext, kroll)), _b(jnp.where(lo, kroll, kext))]
    vd = [_b(jnp.where(lo, vext, vroll)), _b(jnp.where(lo, vroll, vext))]
    return C, Sg, C4, S4, qr, kd, vd


def _attn_specs(T):
    return [pl.BlockSpec(memory_space=pltpu.SMEM),
            pl.BlockSpec((T, 512), lambda i: (i, OFF_AQ // 512)),
            pl.BlockSpec((T, 512), lambda i: (i, OFF_AZ // 512)),
            pl.BlockSpec((T, 128), lambda i: (i, OFF_AK // 128)),
            pl.BlockSpec((T, 128), lambda i: (i, OFF_AV // 128)),
            pl.BlockSpec((T, 128), lambda i: (i, 0)),
            pl.BlockSpec((T, 128), lambda i: (i, 0))]


def _attn_fwd(proj, rope_c, rope_s, sinks, ymix, *, T, name):
    S_ = proj.shape[0]
    nb = T // 128

    def body(sk_ref, q_ref, z_ref, k_ref, v_ref, c_ref, s_ref, _, y_ref, kprev, vprev):
        i = pl.program_id(0)

        @pl.when(i == 0)
        def _():
            kprev[...] = jnp.zeros_like(kprev)
            vprev[...] = jnp.zeros_like(vprev)

        _, _, _, _, qr, kd, vd = _attn_prep(T, q_ref, k_ref, v_ref, c_ref, s_ref, kprev, vprev)
        masks = _attn_masks(i, sk_ref)
        for b in range(nb):
            r0 = b * 128
            for h in range(2):
                lof, hif, _, _, _, _, _, o4 = _attn_common(masks, b, h, qr, kd, vd)
                for p in range(2):
                    cs = slice(h * 256 + p * 128, h * 256 + (p + 1) * 128)
                    o = o4[2 * p * 128:(2 * p + 1) * 128] * lof + o4[(2 * p + 1) * 128:(2 * p + 2) * 128] * hif
                    y_ref[r0:r0 + 128, cs] = (o * _silu(z_ref[r0:r0 + 128, cs])).astype(BF16)

    return _pcall(
        body, name=name, grid=(S_ // T,), in_specs=_attn_specs(T) + [ANY_SPEC],
        out_specs=pl.BlockSpec((T, 512), lambda i: (i, MIX_A // 512)),
        out_shape=jax.ShapeDtypeStruct(ymix.shape, BF16),
        input_output_aliases={7: 0},
        scratch_shapes=[pltpu.VMEM((128, 128), F32), pltpu.VMEM((128, 128), F32)],
        compiler_params=_cp(("arbitrary",)),
    )(sinks, proj, proj, proj, proj, rope_c, rope_s, ymix)


def _attn_bwd(proj, rope_c, rope_s, sinks, dymix, *, T, name):
    S_ = proj.shape[0]
    nb = T // 128
    nt = S_ // T

    def body(sk_ref, q_ref, z_ref, k_ref, v_ref, c_ref, s_ref, dy_ref,
             dp_ref, dk_ref, dv_ref, dkt_ref, dvt_ref, dsk_ref,
             kprev, vprev, cprev, sprev, dkacc, dvacc, dqacc):
        i = pl.program_id(0)

        @pl.when(i == 0)
        def _():
            kprev[...] = jnp.zeros_like(kprev)
            vprev[...] = jnp.zeros_like(vprev)
            cprev[...] = jnp.zeros_like(cprev)
            sprev[...] = jnp.zeros_like(sprev)
            dkacc[...] = jnp.zeros_like(dkacc)
            dvacc[...] = jnp.zeros_like(dvacc)
            dsk_ref[...] = jnp.zeros_like(dsk_ref)

        @pl.when(i > 0)
        def _():
            dkacc[0:128, :] = dkacc[T:T + 128, :]
            dvacc[0:128, :] = dvacc[T:T + 128, :]
            dkacc[128:, :] = jnp.zeros((T, 128), F32)
            dvacc[128:, :] = jnp.zeros((T, 128), F32)

        C, Sg, C4, S4, qr, kd, vd = _attn_prep(T, q_ref, k_ref, v_ref, c_ref, s_ref, kprev, vprev)
        masks = _attn_masks(i, sk_ref)
        lane = lax.broadcasted_iota(jnp.int32, (1, 128), 1)
        for b in range(nb):
            r0 = b * 128
            for h in range(2):
                lof, hif, qs, k2, v2, prob, psink, o4 = _attn_common(masks, b, h, qr, kd, vd)
                dos = []
                for p in range(2):
                    cs = slice(h * 256 + p * 128, h * 256 + (p + 1) * 128)
                    o = o4[2 * p * 128:(2 * p + 1) * 128] * lof + o4[(2 * p + 1) * 128:(2 * p + 2) * 128] * hif
                    zc = z_ref[r0:r0 + 128, cs]
                    dyc = dy_ref[r0:r0 + 128, cs]
                    dp_ref[r0:r0 + 128, 512 + cs.start:512 + cs.stop] = (dyc * o * _dsilu(zc)).astype(BF16)
                    do = dyc * _silu(zc)
                    dos += [do * lof, do * hif]
                dos = jnp.concatenate(dos, 0)
                os_ = jnp.concatenate([o4[0:128] * lof, o4[128:256] * hif, o4[256:384] * lof, o4[384:512] * hif], 0)
                delta = jnp.sum(dos * os_, 1, keepdims=True)
                dosb = _b(dos)
                dp = _nt(dosb, v2)
                ds = prob * (dp - delta)
                dsv = -psink * delta
                for g in range(4):
                    sg = jnp.sum(dsv[g * 128:(g + 1) * 128], 0, keepdims=True)
                    hd = h * 4 + g
                    dsk_ref[hd:hd + 1, :] += jnp.broadcast_to(sg, (1, 128))
                dsb = _b(ds * (A_HEAD_DIM ** -0.5))
                dqs = _nn(dsb, k2)
                for p in range(2):
                    cs = slice(h * 256 + p * 128, h * 256 + (p + 1) * 128)
                    dqacc[r0:r0 + 128, cs] = (dqs[2 * p * 128:(2 * p + 1) * 128] * lof
                                              + dqs[(2 * p + 1) * 128:(2 * p + 2) * 128] * hif)
                dkdup = _tn(dsb, qs)
                dvdup = _tn(_b(prob), dosb)
                half = (lane < 64) if h == 0 else (lane >= 64)
                dkacc[r0:r0 + 256, :] += jnp.where(half, dkdup + pltpu.roll(dkdup, 64, 1), 0.0)
                dvacc[r0:r0 + 256, :] += jnp.where(half, dvdup + pltpu.roll(dvdup, 64, 1), 0.0)
        dqr = dqacc[...]
        dp_ref[:, 0:512] = (dqr * C4 + _rot(dqr * S4)).astype(BF16)
        cext = jnp.concatenate([cprev[...], C], 0)
        sext = jnp.concatenate([sprev[...], Sg], 0)
        dke = dkacc[...]
        dkp = dke * cext + _rot(dke * sext)
        dk_ref[...] = dkp[0:T].astype(BF16)
        dkt_ref[...] = dkp[T:T + 128].astype(BF16)
        dve = dvacc[...]
        dv_ref[...] = dve[0:T].astype(BF16)
        dvt_ref[...] = dve[T:T + 128].astype(BF16)
        cprev[...] = C[T - 128:]
        sprev[...] = Sg[T - 128:]

    nar = pl.BlockSpec((T, 128), lambda i: (i, 0))
    tail = pl.BlockSpec((128, 128), lambda i: (0, 0))
    return _pcall(
        body, name=name, grid=(nt,),
        in_specs=_attn_specs(T) + [pl.BlockSpec((T, 512), lambda i: (i, MIX_A // 512))],
        out_specs=[pl.BlockSpec((T, 1024), lambda i: (i, OFF_AQ // 1024)), nar, nar, tail, tail,
                   pl.BlockSpec((8, 128), lambda i: (0, 0))],
        out_shape=[jax.ShapeDtypeStruct((S_, NP), BF16),
                   jax.ShapeDtypeStruct((S_, 128), BF16), jax.ShapeDtypeStruct((S_, 128), BF16),
                   jax.ShapeDtypeStruct((128, 128), BF16), jax.ShapeDtypeStruct((128, 128), BF16),
                   jax.ShapeDtypeStruct((8, 128), F32)],
        scratch_shapes=[pltpu.VMEM((128, 128), F32)] * 4
        + [pltpu.VMEM((T + 128, 128), F32), pltpu.VMEM((T + 128, 128), F32), pltpu.VMEM((T, 512), F32)],
        compiler_params=_cp(("arbitrary",)),
    )(sinks, proj, proj, proj, proj, rope_c, rope_s, dymix)


def _rg_gates(xr, wa_ref, ba_ref, wx_ref, bx_ref, lam_ref):
    xb = _b(xr)
    pre_a = jnp.concatenate([_nn(xb[:, n * 128:(n + 1) * 128], wa_ref[n]) for n in range(R_BLOCKS)], 1) + ba_ref[...]
    pre_x = jnp.concatenate([_nn(xb[:, n * 128:(n + 1) * 128], wx_ref[n]) for n in range(R_BLOCKS)], 1) + bx_ref[...]
    r = _sigmoid(pre_a)
    ig = _sigmoid(pre_x)
    sp = _softplus(-lam_ref[...])
    log_a = -R_C * r * sp
    a = jnp.exp(log_a)
    mult = jnp.sqrt(_one_minus_exp(2.0 * log_a))
    return xb, r, ig, sp, a, mult


def _rg_param_specs():
    C = R_WIDTH
    vec = pl.BlockSpec((1, C), lambda i: (0, 0))
    blk = pl.BlockSpec((R_BLOCKS, 128, 128), lambda i: (0, 0, 0))
    return [pl.BlockSpec((CONV_WIDTH, C), lambda i: (0, 0)), vec, blk, vec, blk, vec, vec]


def _rglru_fwd(proj, cw, cb, wa, ba, wx, bx, lam, *, T, name):
    S_ = proj.shape[0]
    C = R_WIDTH

    def body(rx_ref, rz_ref, cw_ref, cb_ref, wa_ref, ba_ref, wx_ref, bx_ref, lam_ref,
             h_ref, y_ref, halo, hcar):
        i = pl.program_id(0)

        @pl.when(i == 0)
        def _():
            halo[...] = jnp.zeros_like(halo)
            hcar[...] = jnp.zeros_like(hcar)

        rx = rx_ref[...]
        ext = jnp.concatenate([halo[...], rx], 0)
        halo[...] = rx[T - 8:]
        taps = _conv_taps(ext, T)
        xr = cb_ref[...] + sum(cw_ref[k:k + 1, :] * taps[k] for k in range(CONV_WIDTH))
        _, _, ig, _, a, mult = _rg_gates(xr, wa_ref, ba_ref, wx_ref, bx_ref, lam_ref)
        u = mult * (ig * xr)
        acum, hloc = _scan_lin(a, u, False)
        h = hloc + acum * hcar[0:1, :]
        hcar[...] = jnp.broadcast_to(h[T - 1:T, :], (8, C))
        h_ref[...] = h
        y_ref[...] = (h * _silu(rz_ref[...])).astype(BF16)

    row = pl.BlockSpec((T, C), lambda i: (i, 0))
    return _pcall(
        body, name=name, grid=(S_ // T,),
        in_specs=[pl.BlockSpec((T, C), lambda i: (i, OFF_RX // C)),
                  pl.BlockSpec((T, C), lambda i: (i, OFF_RZ // C))] + _rg_param_specs(),
        out_specs=[row, pl.BlockSpec((T, C), lambda i: (i, MIX_R // C))],
        out_shape=[jax.ShapeDtypeStruct((S_, C), F32), jax.ShapeDtypeStruct((S_, MIX_WIDTH), BF16)],
        scratch_shapes=[pltpu.VMEM((8, C), F32), pltpu.VMEM((8, C), F32)],
        compiler_params=_cp(("arbitrary",)),
    )(proj, proj, cw, cb.reshape(1, C), _b(wa), ba.reshape(1, C), _b(wx), bx.reshape(1, C), lam.reshape(1, C))


def _rglru_bwd(proj, h, dymix, dproj, cw, cb, wa, ba, wx, bx, lam, *, T, name):
    S_ = proj.shape[0]
    C = R_WIDTH
    nt = S_ // T
    t8 = T // 8

    def body(rx_ref, rxp_ref, rz_ref, h_ref, hp_ref, dy_ref,
             cw_ref, cb_ref, wa_ref, ba_ref, wx_ref, bx_ref, lam_ref, wat_ref, wxt_ref,
             _, dp_ref, dcw_ref, dcb_ref, dwa_ref, dba_ref, dwx_ref, dbx_ref, dlam_ref,
             afirst, gfirst, dhalo):
        i = pl.program_id(0)
        first_tile = (i == nt - 1)

        @pl.when(i == 0)
        def _():
            afirst[...] = jnp.zeros_like(afirst)
            gfirst[...] = jnp.zeros_like(gfirst)
            dhalo[...] = jnp.zeros_like(dhalo)
            for r in (dcw_ref, dcb_ref, dwa_ref, dba_ref, dwx_ref, dbx_ref, dlam_ref):
                r[...] = jnp.zeros_like(r)

        keep = jnp.where(first_tile, 0.0, 1.0)
        rx = rx_ref[...]
        ext = jnp.concatenate([rxp_ref[...] * keep, rx], 0)
        taps = _conv_taps(ext, T)
        xr = cb_ref[...] + sum(cw_ref[k:k + 1, :] * taps[k] for k in range(CONV_WIDTH))
        xb, r, ig, sp, a, mult = _rg_gates(xr, wa_ref, ba_ref, wx_ref, bx_ref, lam_ref)
        hh = h_ref[...]
        rz = rz_ref[...]
        dy = dy_ref[...]
        dp_ref[:, C:2 * C] = (dy * hh * _dsilu(rz)).astype(BF16)
        dh = dy * _silu(rz)
        row = lax.broadcasted_iota(jnp.int32, (T, 1), 0)
        c = jnp.where(row == T - 1, afirst[0:1, :], pltpu.roll(a, T - 1, 0))
        ccum, gloc = _scan_lin(c, dh, True)
        g = gloc + ccum * gfirst[0:1, :]
        afirst[...] = jnp.broadcast_to(a[0:1, :], (8, C))
        gfirst[...] = jnp.broadcast_to(g[0:1, :], (8, C))
        hprev = jnp.where(row == 0, hp_ref[7:8, :] * keep, pltpu.roll(hh, 1, 0))
        da = g * hprev
        gx = ig * xr
        dgx = g * mult
        dmult = g * gx
        dlog_a = da * a - dmult * (a * a) * lax.rsqrt(mult * mult)
        dpre_a = dlog_a * (-R_C * sp) * r * (1.0 - r)
        dpre_x = dgx * xr * ig * (1.0 - ig)
        dlam_ref[...] += jnp.sum(dlog_a * (-R_C * r), 0, keepdims=True) * (-_sigmoid(-lam_ref[...]))
        dab = _b(dpre_a)
        dxb = _b(dpre_x)
        dxr = dgx * ig + jnp.concatenate(
            [_nn(dab[:, n * 128:(n + 1) * 128], wat_ref[n]) + _nn(dxb[:, n * 128:(n + 1) * 128], wxt_ref[n])
             for n in range(R_BLOCKS)], 1)
        for n in range(R_BLOCKS):
            cs = slice(n * 128, (n + 1) * 128)
            dwa_ref[n] += _tn(xb[:, cs], dab[:, cs])
            dwx_ref[n] += _tn(xb[:, cs], dxb[:, cs])
        dba_ref[...] += jnp.sum(dpre_a, 0, keepdims=True)
        dbx_ref[...] += jnp.sum(dpre_x, 0, keepdims=True)
        dcb_ref[...] += jnp.sum(dxr, 0, keepdims=True)
        for k in range(CONV_WIDTH):
            dcw_ref[k:k + 1, :] += jnp.sum(dxr * taps[k], 0, keepdims=True)
        ext2 = jnp.concatenate([dxr, dhalo[...]], 0)
        tt = _conv_taps_t(ext2, T)
        dp_ref[:, 0:C] = sum(cw_ref[k:k + 1, :] * tt[k] for k in range(CONV_WIDTH)).astype(BF16)
        dhalo[...] = dxr[0:8]

    def rev(i):
        return nt - 1 - i

    def prev8(i):
        return jnp.maximum(rev(i) * t8 - 1, 0)

    vec = pl.BlockSpec((1, C), lambda i: (0, 0))
    blk = pl.BlockSpec((R_BLOCKS, 128, 128), lambda i: (0, 0, 0))
    row = pl.BlockSpec((T, C), lambda i: (rev(i), 0))
    wat = _b(jnp.swapaxes(wa, 1, 2))
    wxt = _b(jnp.swapaxes(wx, 1, 2))
    return _pcall(
        body, name=name, grid=(nt,),
        in_specs=[pl.BlockSpec((T, C), lambda i: (rev(i), OFF_RX // C)),
                  pl.BlockSpec((8, C), lambda i: (prev8(i), OFF_RX // C)),
                  pl.BlockSpec((T, C), lambda i: (rev(i), OFF_RZ // C)),
                  row,
                  pl.BlockSpec((8, C), lambda i: (prev8(i), 0)),
                  pl.BlockSpec((T, C), lambda i: (rev(i), MIX_R // C)),
                  ] + _rg_param_specs() + [blk, blk, ANY_SPEC],
        out_specs=[pl.BlockSpec((T, 2 * C), lambda i: (rev(i), OFF_RX // (2 * C))),
                   pl.BlockSpec((CONV_WIDTH, C), lambda i: (0, 0)), vec, blk, vec, blk, vec, vec],
        out_shape=[jax.ShapeDtypeStruct(dproj.shape, BF16),
                   jax.ShapeDtypeStruct((CONV_WIDTH, C), F32), jax.ShapeDtypeStruct((1, C), F32),
                   jax.ShapeDtypeStruct((R_BLOCKS, 128, 128), F32), jax.ShapeDtypeStruct((1, C), F32),
                   jax.ShapeDtypeStruct((R_BLOCKS, 128, 128), F32), jax.ShapeDtypeStruct((1, C), F32),
                   jax.ShapeDtypeStruct((1, C), F32)],
        input_output_aliases={15: 0},
        scratch_shapes=[pltpu.VMEM((8, C), F32)] * 3,
        compiler_params=_cp(("arbitrary",)),
    )(proj, proj, proj, h, h, dymix, cw, cb.reshape(1, C), _b(wa), ba.reshape(1, C), _b(wx), bx.reshape(1, C),
      lam.reshape(1, C), wat, wxt, dproj)


GW3 = 3 * G_WIDTH


def _lane_col(x, lane_idx):
    lane = lax.broadcasted_iota(jnp.int32, (1, x.shape[1]), 1)
    return jnp.sum(jnp.where(lane == lane_idx, x, 0.0), 1, keepdims=True)


def _gdn_pre(ext, T, cw_ref, gba, pv_ref):
    taps = _conv_taps(ext, T)
    c = sum(cw_ref[k:k + 1, :] * taps[k] for k in range(CONV_WIDTH))
    qkv = _silu(c)
    beta = _sigmoid(gba)
    sarg = gba + pv_ref[1:2, :]
    nea = -jnp.exp(pv_ref[0:1, :])
    gdec = nea * _softplus(sarg)
    ri = lax.broadcasted_iota(jnp.int32, (T, T), 0)
    cj = lax.broadcasted_iota(jnp.int32, (T, T), 1)
    same = (ri >> 6) == (cj >> 6)
    ltri = jnp.where((ri >= cj) & same, 1.0, 0.0).astype(BF16)
    gc = _dot_exact_lhs(_nn, ltri, gdec)
    return taps, c, qkv, beta, sarg, nea, gdec, gc


def _gdn_masks():
    ri = lax.broadcasted_iota(jnp.int32, (128, 128), 0)
    cj = lax.broadcasted_iota(jnp.int32, (128, 128), 1)
    same = (ri >> 6) == (cj >> 6)
    return (ri >= cj) & same, (ri > cj) & same, ri == cj


def _lockstep(gens):
    out = [None] * len(gens)
    live = list(range(len(gens)))
    while live:
        still = []
        for k in live:
            try:
                next(gens[k])
                still.append(k)
            except StopIteration as stop:
                out[k] = stop.value
        live = still
    return out


def _gdn_chunk(qkv, beta, gc, rs, h, tm=None):
    tril, strict, eye = _gdn_masks()
    rowi = lax.broadcasted_iota(jnp.int32, (128, 1), 0)
    lane = lax.broadcasted_iota(jnp.int32, (1, 128), 1)
    qh = qkv[rs, h * 128:(h + 1) * 128]
    kh = qkv[rs, 512 + h * 128:512 + (h + 1) * 128]
    vh = qkv[rs, 1024 + h * 128:1024 + (h + 1) * 128]
    rq = lax.rsqrt(jnp.sum(qh * qh, 1, keepdims=True) + RMS_EPS)
    rk = lax.rsqrt(jnp.sum(kh * kh, 1, keepdims=True) + RMS_EPS)
    qn = qh * (rq * (G_HEAD_DIM ** -0.5))
    kn = kh * rk
    gcb = gc[rs]
    gcol = _lane_col(gcb, 4 + h)
    bcol = _lane_col(beta[rs], h)
    grow = _dot_exact_lhs(_nt, jnp.ones((128, 128), BF16), jnp.where(lane == 4 + h, gcb, 0.0))
    D = jnp.where(tril, jnp.exp(jnp.minimum(gcol - grow, 0.0)), 0.0)
    kb = kn * bcol
    vb = vh * bcol
    knb = _b(kn)
    A = _nt(_b(kb), knb)
    Bm = _nt(_b(qn), knb)
    yield
    if tm is None:
        N = jnp.where(strict, -(A * D), 0.0)
        tm = jnp.where(eye, 1.0, 0.0) + N
        npow = N
        for _ in range(5):
            npow = _dot3(_nn, npow, npow)
            yield
            tm = tm + _dot3(_nn, tm, npow)
            yield
    eg = jnp.exp(gcol)
    u = _dot3(_nn, tm, vb)
    w = _dot3(_nn, tm, kb * eg)
    yield
    qk = jnp.where(tril, Bm * D, 0.0)
    qd = qn * eg
    gla = jnp.sum(jnp.where(rowi == 63, gcol, 0.0), 0, keepdims=True)
    glb = jnp.sum(jnp.where(rowi == 127, gcol, 0.0), 0, keepdims=True)
    ed = jnp.exp(jnp.where(rowi < 64, gla, glb) - gcol)
    kd = kn * ed
    return dict(qh=qh, kh=kh, vh=vh, rq=rq, rk=rk, qn=qn, kn=kn, gcol=gcol, bcol=bcol, D=D, A=A, Bm=Bm,
                tm=tm, eg=eg, ed=ed, u=u, w=w, qk=qk, qd=qd, kd=kd, kb=kb, vb=vb,
                gla=jnp.exp(gla), glb=jnp.exp(glb))


def _gdn_scan(q, sa):
    sab = _b(sa)
    wb = _b(q["w"])
    vna = q["u"] - _nn(wb, sab)
    yield
    sb = sa * q["gla"] + _tn(_b(q["kd"][0:64]), _b(vna[0:64]))
    yield
    sbb = _b(sb)
    vnb = q["u"] - _nn(wb, sbb)
    yield
    sn = sb * q["glb"] + _tn(_b(q["kd"][64:128]), _b(vnb[64:128]))
    yield
    vn = jnp.concatenate([vna[0:64], vnb[64:128]], 0)
    qdb = _b(q["qd"])
    o = jnp.concatenate([_nn(qdb[0:64], sab), _nn(qdb[64:128], sbb)], 0) + _nn(_b(q["qk"]), _b(vn))
    return sb, sn, vn, o


def _gdn_param_specs():
    return [pl.BlockSpec((CONV_WIDTH, GW3), lambda i: (0, 0)),
            pl.BlockSpec((8, 128), lambda i: (0, 0)),
            pl.BlockSpec((1, 128), lambda i: (0, 0))]


def _gdn_pvec(a_log, dt_bias):
    z = jnp.zeros((8, 128), F32)
    return z.at[0, 4:8].set(a_log).at[1, 4:8].set(dt_bias)


def _gdn_fwd(proj, cw, a_log, dt_bias, nw, ymix, *, T, name):
    S_ = proj.shape[0]
    nu = T // 128

    def body(x_ref, z_ref, g_ref, cw_ref, pv_ref, nw_ref, _, y_ref, st_ref, tm_ref, halo, state):
        i = pl.program_id(0)

        @pl.when(i == 0)
        def _():
            halo[...] = jnp.zeros_like(halo)
            state[...] = jnp.zeros_like(state)

        x = x_ref[...]
        ext = jnp.concatenate([halo[...], x], 0)
        halo[...] = x[T - 8:]
        _, _, qkv, beta, _, _, _, gc = _gdn_pre(ext, T, cw_ref, g_ref[...], pv_ref)
        items = [(dc, h) for dc in range(nu) for h in range(G_HEADS)]
        qs = _lockstep([_gdn_chunk(qkv, beta, gc, slice(dc * 128, (dc + 1) * 128), h) for dc, h in items])

        def head_chain(h):
            s = state[h]
            for dc in range(nu):
                rs = slice(dc * 128, (dc + 1) * 128)
                q = qs[dc * G_HEADS + h]
                sb, sn, _, o = yield from _gdn_scan(q, s)
                st_ref[2 * dc, h] = s
                st_ref[2 * dc + 1, h] = sb
                tm_ref[dc, h] = q["tm"]
                s = sn
                yield
                rn = lax.rsqrt(jnp.mean(o * o, 1, keepdims=True) + RMS_EPS)
                cs = slice(h * 128, (h + 1) * 128)
                y_ref[rs, cs] = (o * rn * nw_ref[...] * _silu(z_ref[rs, cs])).astype(BF16)
                yield
            state[h] = s

        _lockstep([head_chain(h) for h in range(G_HEADS)])

    return _pcall(
        body, name=name, grid=(S_ // T,),
        in_specs=[pl.BlockSpec((T, GW3), lambda i: (i, OFF_GQKV // GW3)),
                  pl.BlockSpec((T, 512), lambda i: (i, OFF_GZ // 512)),
                  pl.BlockSpec((T, 128), lambda i: (i, OFF_GBA // 128))] + _gdn_param_specs() + [ANY_SPEC],
        out_specs=[pl.BlockSpec((T, 512), lambda i: (i, MIX_G // 512)),
                   pl.BlockSpec((2 * nu, G_HEADS, 128, 128), lambda i: (i, 0, 0, 0)),
                   pl.BlockSpec((nu, G_HEADS, 128, 128), lambda i: (i, 0, 0, 0))],
        out_shape=[jax.ShapeDtypeStruct(ymix.shape, BF16),
                   jax.ShapeDtypeStruct((S_ // 64, G_HEADS, 128, 128), F32),
                   jax.ShapeDtypeStruct((S_ // 128, G_HEADS, 128, 128), F32)],
        input_output_aliases={6: 0},
        scratch_shapes=[pltpu.VMEM((8, GW3), F32), pltpu.VMEM((G_HEADS, 128, 128), F32)],
        compiler_params=_cp(("arbitrary",)),
    )(proj, proj, proj, cw, _gdn_pvec(a_log, dt_bias), nw.reshape(1, 128), ymix)


def _gdn_bwd(proj, states, tms, dymix, dproj, cw, a_log, dt_bias, nw, *, T, name):
    S_ = proj.shape[0]
    nt = S_ // T
    nu = T // 128
    t8 = T // 8

    def body(x_ref, xp_ref, z_ref, g_ref, st_ref, tm_ref, dy_ref, cw_ref, pv_ref, nw_ref, _,
             dp_ref, dg_ref, dcw_ref, dpv_ref, dnw_ref, dstate, dhalo, dqkv, dbg):
        i = pl.program_id(0)
        first_tile = (i == nt - 1)

        @pl.when(i == 0)
        def _():
            dstate[...] = jnp.zeros_like(dstate)
            dhalo[...] = jnp.zeros_like(dhalo)
            dcw_ref[...] = jnp.zeros_like(dcw_ref)
            dpv_ref[...] = jnp.zeros_like(dpv_ref)
            dnw_ref[...] = jnp.zeros_like(dnw_ref)

        keep = jnp.where(first_tile, 0.0, 1.0)
        ext = jnp.concatenate([xp_ref[...] * keep, x_ref[...]], 0)
        G = g_ref[...]
        taps, c, qkv, beta, sarg, nea, gdec, gc = _gdn_pre(ext, T, cw_ref, G, pv_ref)
        tril, strict, _ = _gdn_masks()
        rowi = lax.broadcasted_iota(jnp.int32, (128, 1), 0)
        lane = lax.broadcasted_iota(jnp.int32, (1, 128), 1)
        ones_b = jnp.ones((128, 128), BF16)
        nwv = nw_ref[...]
        items = [(dc, h) for dc in range(nu) for h in range(G_HEADS)]

        def recompute(dc, h):
            q = yield from _gdn_chunk(qkv, beta, gc, slice(dc * 128, (dc + 1) * 128), h, tm=tm_ref[dc, h])
            sa = st_ref[2 * dc, h]
            sb, _, vn, o = yield from _gdn_scan(q, sa)
            return q, sa, sb, vn, o

        fw = _lockstep([recompute(dc, h) for dc, h in items])
        chain_out = {}

        def head_chain(h):
            dS = dstate[h]
            for dc in reversed(range(nu)):
                rs = slice(dc * 128, (dc + 1) * 128)
                q, sa, sb, vn, o = fw[dc * G_HEADS + h]
                cs = slice(h * 128, (h + 1) * 128)
                zg = z_ref[rs, cs]
                dy = dy_ref[rs, cs]
                rn = lax.rsqrt(jnp.mean(o * o, 1, keepdims=True) + RMS_EPS)
                don = dy * _silu(zg)
                dp_ref[rs, GW3 + cs.start:GW3 + cs.stop] = (dy * (o * rn * nwv) * _dsilu(zg)).astype(BF16)
                dnw_ref[...] += jnp.sum(don * o * rn, 0, keepdims=True)
                tt = don * nwv
                do = rn * (tt - o * (rn * rn) * jnp.mean(tt * o, 1, keepdims=True))
                yield
                dob = _b(do)
                sab, sbb = _b(sa), _b(sb)
                vnb16 = _b(vn)
                dqk = jnp.where(tril, _nt(dob, vnb16), 0.0)
                dvn_o = _tn(_b(q["qk"]), dob)
                dSb16 = _b(dS)
                kdb = _b(q["kd"])
                wb = _b(q["w"])
                qdb = _b(q["qd"])
                yield
                dvn_b = dvn_o[64:128] + _nn(kdb[64:128], dSb16)
                dkd_b = _nt(vnb16[64:128], dSb16)
                dgl_b = jnp.sum(jnp.sum(dS * sb, 1, keepdims=True), 0, keepdims=True)
                yield
                dvn_b16 = _b(dvn_b)
                dw_b = -_nt(dvn_b16, sbb)
                dqd_b = _nt(dob[64:128], sbb)
                dSm = q["glb"] * dS + _tn(qdb[64:128], dob[64:128]) - _tn(wb[64:128], dvn_b16)
                yield
                dSm16 = _b(dSm)
                dvn_a = dvn_o[0:64] + _nn(kdb[0:64], dSm16)
                dkd_a = _nt(vnb16[0:64], dSm16)
                dgl_a = jnp.sum(jnp.sum(dSm * sa, 1, keepdims=True), 0, keepdims=True)
                yield
                dvn_a16 = _b(dvn_a)
                dw_a = -_nt(dvn_a16, sab)
                dqd_a = _nt(dob[0:64], sab)
                dS = q["gla"] * dSm + _tn(qdb[0:64], dob[0:64]) - _tn(wb[0:64], dvn_a16)
                chain_out[dc, h] = (dqk, jnp.concatenate([dvn_a, dvn_b], 0), jnp.concatenate([dw_a, dw_b], 0),
                                    jnp.concatenate([dkd_a, dkd_b], 0), jnp.concatenate([dqd_a, dqd_b], 0),
                                    dgl_a, dgl_b)
                yield
            dstate[h] = dS

        _lockstep([head_chain(h) for h in range(G_HEADS)])

        def local(dc, h):
            rs = slice(dc * 128, (dc + 1) * 128)
            q = fw[dc * G_HEADS + h][0]
            dqk, du, dw, dkd, dqd, dgl_a, dgl_b = chain_out[dc, h]
            if True:
                dvb = _dot3(_tn, q["tm"], du)
                dkbe = _dot3(_tn, q["tm"], dw)
                yield
                dM = jnp.where(strict, -(_nt(_b(dvb), _b(q["u"])) + _nt(_b(dkbe), _b(q["w"]))), 0.0)
                yield
                D = q["D"]
                dA = dM * D
                dB = dqk * D
                dDD = (dM * q["A"] + dqk * q["Bm"]) * D
                dh_, dm_, dl_ = _split3(dDD)
                colsum = _tn(dh_, ones_b) + (_tn(dm_, ones_b) + _tn(dl_, ones_b))
                dgc = jnp.sum(dDD, 1, keepdims=True) - _lane_col(colsum, 0)
                yield
                dA16, dB16 = _b(dA), _b(dB)
                knb, kbb, qnb = _b(q["kn"]), _b(q["kb"]), _b(q["qn"])
                eg, ed = q["eg"], q["ed"]
                dkb = _nn(dA16, knb) + dkbe * eg
                dkn = _tn(dA16, kbb) + _tn(dB16, qnb) + dkd * ed + dkb * q["bcol"]
                dqn = _nn(dB16, knb) + dqd * eg
                yield
                deg = jnp.sum(dkbe * q["kb"], 1, keepdims=True) + jnp.sum(dqd * q["qn"], 1, keepdims=True)
                ded = jnp.sum(dkd * q["kn"], 1, keepdims=True) * ed
                dgc = dgc + deg * eg - ded
                tail_a = jnp.sum(jnp.where(rowi < 64, ded, 0.0), 0, keepdims=True) + dgl_a * q["gla"]
                tail_b = jnp.sum(jnp.where(rowi >= 64, ded, 0.0), 0, keepdims=True) + dgl_b * q["glb"]
                dgc = dgc + jnp.where(rowi == 63, tail_a, 0.0) + jnp.where(rowi == 127, tail_b, 0.0)
                dbeta = jnp.sum(dkb * q["kn"], 1, keepdims=True) + jnp.sum(dvb * q["vh"], 1, keepdims=True)
                bcol = q["bcol"]
                blk = jnp.where(lane == h, dbeta * bcol * (1.0 - bcol), 0.0) + jnp.where(lane == 4 + h, dgc, 0.0)
                yield
                sc = G_HEAD_DIM ** -0.5
                rq, rk, qh, kh = q["rq"], q["rk"], q["qh"], q["kh"]
                dqh = sc * (dqn * rq - qh * (rq * rq * rq) * jnp.sum(dqn * qh, 1, keepdims=True))
                dkh = dkn * rk - kh * (rk * rk * rk) * jnp.sum(dkn * kh, 1, keepdims=True)
                dqkv[rs, h * 128:(h + 1) * 128] = dqh
                dqkv[rs, 512 + h * 128:512 + (h + 1) * 128] = dkh
                dqkv[rs, 1024 + h * 128:1024 + (h + 1) * 128] = dvb * bcol
            return blk

        blks = _lockstep([local(dc, h) for dc, h in items])
        for dc in range(nu):
            dbg[dc * 128:(dc + 1) * 128, :] = functools.reduce(
                lambda a, b: a + b, [blks[dc * G_HEADS + h] for h in range(G_HEADS)])
        ri = lax.broadcasted_iota(jnp.int32, (T, T), 0)
        cj = lax.broadcasted_iota(jnp.int32, (T, T), 1)
        utri = jnp.where((ri <= cj) & ((ri >> 6) == (cj >> 6)), 1.0, 0.0).astype(BF16)
        dbgv = dbg[...]
        dgd = _dot_exact_lhs(_nn, utri, dbgv)
        is_g = (lane >= 4) & (lane < 8)
        dga = jnp.where(is_g, dgd * nea * _sigmoid(sarg), 0.0)
        dg_ref[...] = jnp.where(lane < 4, dbgv, dga).astype(BF16)
        dpv_ref[0:1, :] += jnp.sum(jnp.where(is_g, dgd * gdec, 0.0), 0, keepdims=True)
        dpv_ref[1:2, :] += jnp.sum(dga, 0, keepdims=True)
        dc_ = dqkv[...] * _dsilu(c)
        for k in range(CONV_WIDTH):
            dcw_ref[k:k + 1, :] += jnp.sum(dc_ * taps[k], 0, keepdims=True)
        ext2 = jnp.concatenate([dc_, dhalo[...]], 0)
        tt2 = _conv_taps_t(ext2, T)
        dp_ref[:, 0:GW3] = sum(cw_ref[k:k + 1, :] * tt2[k] for k in range(CONV_WIDTH)).astype(BF16)
        dhalo[...] = dc_[0:8]

    def rev(i):
        return nt - 1 - i

    def prev8(i):
        return jnp.maximum(rev(i) * t8 - 1, 0)

    return _pcall(
        body, name=name, grid=(nt,),
        in_specs=[pl.BlockSpec((T, GW3), lambda i: (rev(i), OFF_GQKV // GW3)),
                  pl.BlockSpec((8, GW3), lambda i: (prev8(i), OFF_GQKV // GW3)),
                  pl.BlockSpec((T, 512), lambda i: (rev(i), OFF_GZ // 512)),
                  pl.BlockSpec((T, 128), lambda i: (rev(i), OFF_GBA // 128)),
                  pl.BlockSpec((2 * nu, G_HEADS, 128, 128), lambda i: (rev(i), 0, 0, 0)),
                  pl.BlockSpec((nu, G_HEADS, 128, 128), lambda i: (rev(i), 0, 0, 0)),
                  pl.BlockSpec((T, 512), lambda i: (rev(i), MIX_G // 512))] + _gdn_param_specs() + [ANY_SPEC],
        out_specs=[pl.BlockSpec((T, GW3 + 512), lambda i: (rev(i), OFF_GQKV // (GW3 + 512))),
                   pl.BlockSpec((T, 128), lambda i: (rev(i), 0)),
                   pl.BlockSpec((CONV_WIDTH, GW3), lambda i: (0, 0)),
                   pl.BlockSpec((8, 128), lambda i: (0, 0)),
                   pl.BlockSpec((1, 128), lambda i: (0, 0))],
        out_shape=[jax.ShapeDtypeStruct(dproj.shape, BF16),
                   jax.ShapeDtypeStruct((S_, 128), BF16), jax.ShapeDtypeStruct((CONV_WIDTH, GW3), F32),
                   jax.ShapeDtypeStruct((8, 128), F32), jax.ShapeDtypeStruct((1, 128), F32)],
        input_output_aliases={10: 0},
        scratch_shapes=[pltpu.VMEM((G_HEADS, 128, 128), F32), pltpu.VMEM((8, GW3), F32),
                        pltpu.VMEM((T, GW3), F32), pltpu.VMEM((T, 128), F32)],
        compiler_params=_cp(("arbitrary",)),
    )(proj, proj, proj, proj, states, tms, dymix, cw, _gdn_pvec(a_log, dt_bias), nw.reshape(1, 128), dproj)


def _pair_sum_windows(a, b, nsh, width, *, out_dtype, name):
    R_, C = a.shape
    hr = R_ // 2
    nb = width // 128
    assert (3 * nsh) // 128 + nb <= C // 128
    to_perm = _orig_block_to_perm()
    table = jnp.asarray([to_perm[(nsh * t) // 128 + j] for t in range(4) for j in range(nb)], jnp.int32)

    def body(tab_ref, a0_ref, a1_ref, b_ref, o_ref):
        mine = jnp.where(lax.axis_index("c") == 0, a0_ref[...], a1_ref[...])
        o_ref[...] = (mine + b_ref[...]).astype(o_ref.dtype)

    def spec(half):
        return pl.BlockSpec((hr, 128), lambda t, j, tab: (half, tab[t * nb + j]))

    return _pcall(
        body, name=name,
        grid_spec=pltpu.PrefetchScalarGridSpec(
            num_scalar_prefetch=1, grid=(4, nb), in_specs=[spec(0), spec(1), spec(0)],
            out_specs=pl.BlockSpec((None, hr, 128), lambda t, j, tab: (t, 0, j))),
        out_shape=jax.ShapeDtypeStruct((4, hr, width), out_dtype),
        compiler_params=_cp(("parallel", "parallel")))(table, a, a, b)


def _pair_sum_blocks(a, b, *, out_dtype, name):
    L, R_, C = a.shape
    hr = R_ // 2

    def body(a0_ref, a1_ref, b_ref, o_ref):
        mine = jnp.where(lax.axis_index("c") == 0, a0_ref[...], a1_ref[...])
        o_ref[...] = (mine + b_ref[...]).astype(o_ref.dtype)

    def spec(half):
        return pl.BlockSpec((None, hr, C), lambda t: (t, half, 0))

    return _pcall(body, name=name, grid=(L,), in_specs=[spec(0), spec(1), spec(0)], out_specs=spec(0),
                  out_shape=jax.ShapeDtypeStruct((L, hr, C), out_dtype),
                  compiler_params=_cp(("parallel",)))(a, a, b)


def _add_mine(a0, a1, b, *, out_dtype, tr, name):
    R_, C = b.shape

    def body(a0_ref, a1_ref, b_ref, o_ref):
        mine = jnp.where(lax.axis_index("c") == 0, a0_ref[...], a1_ref[...])
        o_ref[...] = (mine + b_ref[...]).astype(o_ref.dtype)

    spec = pl.BlockSpec((tr, C), lambda i: (i, 0))
    return _pcall(body, name=name, grid=(R_ // tr,), in_specs=[spec] * 3, out_specs=spec,
                  out_shape=jax.ShapeDtypeStruct((R_, C), out_dtype), compiler_params=_cp(("parallel",)))(a0, a1, b)


def _sum4(a, mine, *, tr, name):
    _, R_, C = a.shape

    def body(a_ref, m_ref, o_ref):
        s = 2 * lax.axis_index("x") + lax.axis_index("y")
        mv = m_ref[...].astype(F32)
        p = [jnp.where(s == t, mv, a_ref[t].astype(F32)) for t in range(4)]
        o_ref[...] = ((p[0] + p[1]) + p[2]) + p[3]

    return _pcall(body, name=name, grid=(R_ // tr,),
                  in_specs=[pl.BlockSpec((4, tr, C), lambda i: (0, i, 0)), pl.BlockSpec((tr, C), lambda i: (i, 0))],
                  out_specs=pl.BlockSpec((tr, C), lambda i: (i, 0)),
                  out_shape=jax.ShapeDtypeStruct((R_, C), F32), compiler_params=_cp(("parallel",)))(a, mine)


def _adamw_refs(w_ref, g_ref, m_ref, v_ref, d_ref, mo_ref, vo_ref):
    c1 = 1.0 / (1.0 - ADAM_B1 ** ADAM_STEP)
    c2 = 1.0 / (1.0 - ADAM_B2 ** ADAM_STEP)
    gg = g_ref[...]
    mn = ADAM_B1 * m_ref[...] + (1.0 - ADAM_B1) * gg
    vn = ADAM_B2 * v_ref[...] + (1.0 - ADAM_B2) * (gg * gg)
    mo_ref[...] = mn
    vo_ref[...] = vn
    d_ref[...] = -ADAM_LR * ((mn * c1) / (jnp.sqrt(vn * c2) + ADAM_EPS) + ADAM_WD * w_ref[...])


def _adamw_many(ws, gs, ms, vs, *, name):
    n = len(ws)

    def body(*refs):
        for k in range(n):
            _adamw_refs(*[refs[q * n + k] for q in range(7)])

    vm = pl.BlockSpec(memory_space=pltpu.VMEM)
    shp = [jax.ShapeDtypeStruct(w.shape, F32) for w in ws]
    outs = _pcall(body, name=name, in_specs=[vm] * (4 * n), out_specs=[vm] * (3 * n), out_shape=shp * 3,
                  compiler_params=pltpu.CompilerParams(vmem_limit_bytes=VMEM_LIMIT))(*ws, *gs, *ms, *vs)
    return outs[:n], outs[n:2 * n], outs[2 * n:]


def _adamw(w, g, m, v, *, tr, name):
    L, R_, C = w.shape
    body = functools.partial(_adamw_refs)

    spec = pl.BlockSpec((None, tr, C), lambda l, i: (l, i, 0))
    shp = jax.ShapeDtypeStruct((L, R_, C), F32)
    return _pcall(body, name=name, grid=(L, R_ // tr), in_specs=[spec] * 4, out_specs=[spec] * 3,
                  out_shape=[shp] * 3, compiler_params=_cp(("parallel", "parallel")))(w, g, m, v)


def _adamw_cols(w, g, m, v, *, name):
    C, L, R_ = w.shape
    tc = C // 2 if C % 2 == 0 else C

    spec = pl.BlockSpec((tc, L, 128), lambda i, j: (i, 0, j))
    shp = jax.ShapeDtypeStruct((C, L, R_), F32)
    return _pcall(functools.partial(_adamw_refs), name=name, grid=(C // tc, R_ // 128), in_specs=[spec] * 4,
                  out_specs=[spec] * 3, out_shape=[shp] * 3,
                  compiler_params=_cp(("parallel", "parallel")))(w, g, m, v)


HBM_SPEC = pl.BlockSpec(memory_space=pltpu.HBM)


def _place():
    x, y, c = lax.axis_index("x"), lax.axis_index("y"), lax.axis_index("c")
    chips = [(1 - x, y), (x, 1 - y), (1 - x, 1 - y)]
    return x, y, c, 2 * x + y, chips, [2 * cx + cy for cx, cy in chips], (x, y, 1 - c)


def _remote(src, dst, ssem, rsem, dev):
    return pltpu.make_async_remote_copy(src_ref=src, dst_ref=dst, send_sem=ssem, recv_sem=rsem,
                                        device_id=dev, device_id_type=MESH)


def _row_half(ref, lead, hc):
    hl = ref.shape[-2] // 2
    return ref.at[lead, pl.ds(hc * hl, hl), :]


def _gather_side(items):
    n = len(items)

    def copies(ins, outs, ssem, rsem):
        x, y, c, s, chips, sid, sib = _place()
        cps = [_remote(_row_half(ins[k], items[k][1], c), _row_half(outs[k], s, c),
                       ssem.at[3 * k + j], rsem.at[3 * k + j], (*chip, c))
               for k in range(n) for j, chip in enumerate(chips)]
        return cps, c, sid, sib

    def start(ins, outs, ssem, rsem):
        for cp in copies(ins, outs, ssem, rsem)[0]:
            cp.start()

    def finish(ins, outs, ssem, rsem):
        cps, c, sid, sib = copies(ins, outs, ssem, rsem)
        for k in range(n):
            for j in range(3):
                got = _row_half(outs[k], sid[j], c)
                _remote(got, got, ssem.at[3 * k + j], rsem.at[3 * k + j], sib).wait_recv()
        for cp in cps:
            cp.wait_send()

    shapes = [jax.ShapeDtypeStruct((4,) + w.shape[1:], w.dtype) for w, _ in items]
    return _Side([w for w, _ in items], shapes, 3 * n, start, finish)


def _gather_join(gathered, name):
    n = len(gathered)

    def body(*refs):
        outs, ssem, rsem = refs[n:2 * n], refs[2 * n], refs[2 * n + 1]
        x, y, c, s, chips, sid, sib = _place()
        cps = []
        for k in range(n):
            for j in range(3):
                mine = _row_half(outs[k], sid[j], c)
                cps.append(_remote(mine, mine, ssem.at[3 * k + j], rsem.at[3 * k + j], sib))
        for cp in cps:
            cp.start()
        for k in range(n):
            for j in range(3):
                other = _row_half(outs[k], sid[j], 1 - c)
                _remote(other, other, ssem.at[3 * k + j], rsem.at[3 * k + j], sib).wait_recv()
        for cp in cps:
            cp.wait_send()

    return _pcall(
        body, name=name, in_specs=[HBM_SPEC] * n, out_specs=[HBM_SPEC] * n,
        out_shape=[jax.ShapeDtypeStruct(g.shape, g.dtype) for g in gathered],
        input_output_aliases={k: k for k in range(n)},
        scratch_shapes=[pltpu.SemaphoreType.DMA((3 * n,)), pltpu.SemaphoreType.DMA((3 * n,))],
    )(*gathered)


def _gather_layer0(win, conv):
    def body(win_ref, cv_ref, gin_ref, gcv_ref, ssem, rsem):
        x, y, c, s, chips, sid, sib = _place()

        def in_half(slot, hc):
            return _row_half(gin_ref, slot, hc)

        sends = []
        for j, chip in enumerate(chips):
            dev = (*chip, c)
            sends.append(_remote(_row_half(win_ref, 0, c), in_half(s, c), ssem.at[j], rsem.at[j], dev))
            sends.append(_remote(cv_ref, gcv_ref.at[s], ssem.at[3 + j], rsem.at[3 + j], dev))
        for cp in sends:
            cp.start()
        for j in range(3):
            _remote(in_half(sid[j], c), in_half(sid[j], c), ssem.at[j], rsem.at[j], sib).wait_recv()
            f = _remote(in_half(sid[j], c), in_half(sid[j], c), ssem.at[6 + j], rsem.at[6 + j], sib)
            f.start()
            sends.append(f)
        for j in range(3):
            _remote(in_half(sid[j], 1 - c), in_half(sid[j], 1 - c), ssem.at[6 + j], rsem.at[6 + j], sib).wait_recv()
            _remote(gcv_ref.at[sid[j]], gcv_ref.at[sid[j]], ssem.at[3 + j], rsem.at[3 + j], sib).wait_recv()
        for cp in sends:
            cp.wait_send()

    return _pcall(
        body, name="gather_layer0",
        in_specs=[HBM_SPEC] * 2, out_specs=[HBM_SPEC] * 2,
        out_shape=[jax.ShapeDtypeStruct((4,) + win.shape[1:], win.dtype),
                   jax.ShapeDtypeStruct((4,) + conv.shape, conv.dtype)],
        scratch_shapes=[pltpu.SemaphoreType.DMA((9,)), pltpu.SemaphoreType.DMA((9,))],
    )(win, conv)


def _swap_halves(arrs, axes, name):
    n = len(arrs)

    def half_shape(a, ax):
        return a.shape[:ax] + (a.shape[ax] // 2,) + a.shape[ax + 1:]

    def body(*refs):
        src, dst, ssem, rsem = refs[:n], refs[n:2 * n], refs[2 * n], refs[2 * n + 1]
        x, y, c, s, chips, sid, sib = _place()
        cps = []
        for k in range(n):
            hl = src[k].shape[axes[k]] // 2
            idx = [slice(None)] * len(src[k].shape)
            idx[axes[k]] = pl.ds((1 - c) * hl, hl)
            cps.append(_remote(src[k].at[tuple(idx)], dst[k], ssem.at[k], rsem.at[k], sib))
        for cp in cps:
            cp.start()
        for cp in cps:
            cp.wait()

    return _pcall(
        body, name=name, in_specs=[HBM_SPEC] * n, out_specs=[HBM_SPEC] * n,
        out_shape=[jax.ShapeDtypeStruct(half_shape(a, ax), a.dtype) for a, ax in zip(arrs, axes)],
        scratch_shapes=[pltpu.SemaphoreType.DMA((n,)), pltpu.SemaphoreType.DMA((n,))],
    )(*arrs)


def _swap_side(arrs, axes):
    n = len(arrs)

    def copies(ins, outs, ssem, rsem):
        x, y, c, s, chips, sid, sib = _place()
        cps = []
        for k in range(n):
            hl = ins[k].shape[axes[k]] // 2
            idx = [slice(None)] * len(ins[k].shape)
            idx[axes[k]] = pl.ds((1 - c) * hl, hl)
            cps.append(_remote(ins[k].at[tuple(idx)], outs[k], ssem.at[k], rsem.at[k], sib))
        return cps

    def start(ins, outs, ssem, rsem):
        for cp in copies(ins, outs, ssem, rsem):
            cp.start()

    def finish(ins, outs, ssem, rsem):
        for cp in copies(ins, outs, ssem, rsem):
            cp.wait()

    shapes = [jax.ShapeDtypeStruct(a.shape[:ax] + (a.shape[ax] // 2,) + a.shape[ax + 1:], a.dtype)
              for a, ax in zip(arrs, axes)]
    return _Side(list(arrs), shapes, n, start, finish)


def _chips_side(arrs, per_target):
    n = len(arrs)

    def copies(ins, outs, ssem, rsem):
        x, y, c, s, chips, sid, sib = _place()
        cps = [_remote(ins[k].at[sid[j]] if per_target[k] else ins[k], outs[k].at[s],
                       ssem.at[3 * k + j], rsem.at[3 * k + j], (*chip, c))
               for k in range(n) for j, chip in enumerate(chips)]
        return cps, sid, sib

    def start(ins, outs, ssem, rsem):
        for cp in copies(ins, outs, ssem, rsem)[0]:
            cp.start()

    def finish(ins, outs, ssem, rsem):
        cps, sid, sib = copies(ins, outs, ssem, rsem)
        for k in range(n):
            for j in range(3):
                got = outs[k].at[sid[j]]
                _remote(got, got, ssem.at[3 * k + j], rsem.at[3 * k + j], sib).wait_recv()
        for cp in cps:
            cp.wait_send()

    shapes = [jax.ShapeDtypeStruct(a.shape if pt else (4,) + a.shape, a.dtype) for a, pt in zip(arrs, per_target)]
    return _Side(list(arrs), shapes, 3 * n, start, finish)


def _scatter_chips(arrs, per_target, name):
    n = len(arrs)

    def body(*refs):
        src, dst = refs[:n], refs[n:2 * n]
        ssem, rsem = refs[2 * n], refs[2 * n + 1]
        x, y, c, s, chips, sid, sib = _place()
        sends = []
        for k in range(n):
            for j, chip in enumerate(chips):
                piece = src[k].at[sid[j]] if per_target[k] else src[k]
                sends.append(_remote(piece, dst[k].at[s], ssem.at[3 * k + j], rsem.at[3 * k + j], (*chip, c)))
        for cp in sends:
            cp.start()
        for k in range(n):
            for j in range(3):
                _remote(dst[k].at[sid[j]], dst[k].at[sid[j]], ssem.at[3 * k + j], rsem.at[3 * k + j], sib).wait_recv()
        for cp in sends:
            cp.wait_send()

    outs = [jax.ShapeDtypeStruct(a.shape if pt else (4,) + a.shape, a.dtype) for a, pt in zip(arrs, per_target)]
    return _pcall(
        body, name=name, in_specs=[HBM_SPEC] * n, out_specs=[HBM_SPEC] * n, out_shape=outs,
        scratch_shapes=[pltpu.SemaphoreType.DMA((3 * n,)), pltpu.SemaphoreType.DMA((3 * n,))],
    )(*arrs)


def _swap_whole(arrs, name):
    n = len(arrs)

    def body(*refs):
        src, dst, ssem, rsem = refs[:n], refs[n:2 * n], refs[2 * n], refs[2 * n + 1]
        *_, sib = _place()
        cps = [_remote(src[k], dst[k], ssem.at[k], rsem.at[k], sib) for k in range(n)]
        for cp in cps:
            cp.start()
        for cp in cps:
            cp.wait()

    return _pcall(
        body, name=name, in_specs=[HBM_SPEC] * n, out_specs=[HBM_SPEC] * n,
        out_shape=[jax.ShapeDtypeStruct(a.shape, a.dtype) for a in arrs],
        scratch_shapes=[pltpu.SemaphoreType.DMA((n,)), pltpu.SemaphoreType.DMA((n,))],
    )(*arrs)


def _perm_cols(w):
    parts = [w[..., int(_ORIG_OFF[oi]):int(_ORIG_OFF[oi]) + IN_SIZES[oi]] for oi, _ in _PIECES]
    parts.append(jnp.zeros(w.shape[:-1] + (NP - N_IN,), w.dtype))
    return jnp.concatenate(parts, -1)


def _perm_rows(w):
    return jnp.concatenate([w[..., 512:1536, :], w[..., 0:512, :], w[..., 1536:2048, :]], -2)


_SMALL = ("sinks", "r_conv_b", "r_wa", "r_ba", "r_wx", "r_bx", "r_lam", "g_a_log", "g_dt_bias", "g_norm_w",
          "ln_g", "ln_b", "r_conv_w", "g_conv_w")
_PACK_ROWS = 16


def _piece_rows(n):
    return -(-n // (128 * _PACK_ROWS)) * _PACK_ROWS


def _pack(arrs):
    parts = []
    for a in arrs:
        n = int(np.prod(a.shape))
        rows = _piece_rows(n)
        if n % 128 == 0:
            blk = a.reshape(n // 128, 128)
        else:
            blk = jnp.pad(a.reshape(1, n), ((0, 0), (0, (-n) % 128))).reshape(-1, 128)
        if blk.shape[0] < rows:
            blk = jnp.pad(blk, ((0, rows - blk.shape[0]), (0, 0)))
        parts.append(blk)
    return jnp.concatenate(parts, 0)


def _unpack(packed, shapes):
    out = []
    r = 0
    for shp in shapes:
        n = int(np.prod(shp))
        if n % 128 == 0:
            out.append(packed[r:r + n // 128].reshape(shp))
        else:
            nr = -(-n // 128)
            out.append(packed[r:r + nr].reshape(1, nr * 128)[:, :n].reshape(shp))
        r += _piece_rows(n)
    return out


def _tile(n, t):
    return min(n, t)


def _layer_fwd(l, x, xb, wb, wob, ln, rope_c, rope_s, p, side=None, target=None):
    S_ = x.shape[0]
    proj = _matmul(xb, wb, ta=False, tb=False, tm=_tile(S_, 1024), tn=NP // 4, tk=wb.shape[0], out_dtype=F32,
                   name=f"in_proj_{l}", side=side)
    side_out = None
    if side:
        proj, side_out = proj
    h, ymix = _rglru_fwd(proj, p["r_conv_w"], p["r_conv_b"], p["r_wa"], p["r_ba"], p["r_wx"], p["r_bx"], p["r_lam"],
                         T=_tile(S_, 256), name=f"rglru_fwd_{l}")
    ymix = _attn_fwd(proj, rope_c, rope_s, p["sinks"], ymix, T=_tile(S_, 512), name=f"attn_fwd_{l}")
    ymix, st, tms = _gdn_fwd(proj, p["g_conv_w"], p["g_a_log"], p["g_dt_bias"], p["g_norm_w"], ymix,
                             T=_tile(S_, 256), name=f"gdn_fwd_{l}")
    out = _outproj(ymix, wob(side_out), x, ln[0], ln[1], tm=_tile(S_, 256), name=f"out_proj_{l}", target=target)
    sv = dict(proj=proj, h=h, st=st, tms=tms, ymix=ymix)
    if target is None:
        sv["z"], sv["y"], sv["yb"] = out
    else:
        sv["head"] = out
    return sv


def _layer_bwd(l, sv, x_b, dz, dzb, wb, wob, rope_c, rope_s, p, side_dmix=None, side_dw_in=None, side_dx=None):
    S_, D = dz.shape
    proj = sv["proj"]
    dwo = _matmul(sv["ymix"], dzb, ta=True, tb=False, tm=512, tn=_tile(D, 2048), tk=_tile(S_, 2048),
                  out_dtype=F32, name=f"dw_out_{l}",
                  out_blocks=((MIX_WIDTH, D), (512, _tile(D, 2048)),
                              lambda i, j: (jnp.where(i == 3, 3, (i + 1) % 3), j)))
    side = side_dmix(dwo) if side_dmix else None
    dymix = _matmul(dzb, wob, ta=False, tb=True, tm=_tile(S_, 1024), tn=512, tk=D, out_dtype=F32,
                    name=f"dmix_{l}", side=side)
    out_dmix = None
    if side:
        dymix, out_dmix = dymix
    dproj, dk, dv, dkt, dvt, dsk = _attn_bwd(proj, rope_c, rope_s, p["sinks"], dymix, T=_tile(S_, 512),
                                             name=f"attn_bwd_{l}")
    (dproj, dcw_r, dcb_r, dwa, dba, dwx, dbx, dlam) = _rglru_bwd(
        proj, sv["h"], dymix, dproj, p["r_conv_w"], p["r_conv_b"], p["r_wa"], p["r_ba"], p["r_wx"], p["r_bx"],
        p["r_lam"], T=_tile(S_, 256), name=f"rglru_bwd_{l}")
    dproj, dgba, dcw_g, dpv, dnw = _gdn_bwd(proj, sv["st"], sv["tms"], dymix, dproj, p["g_conv_w"], p["g_a_log"],
                                            p["g_dt_bias"], p["g_norm_w"], T=_tile(S_, 256), name=f"gdn_bwd_{l}")
    tail = jnp.concatenate([dk[128:], dkt, dv[128:], dvt], 0).reshape(2, S_, 128)
    tail = jnp.concatenate([tail[0], tail[1], dgba, jnp.zeros((S_, NP - OFF_GBA - 128), BF16)], 1)
    dproj = lax.dynamic_update_slice(dproj, tail, (0, OFF_AK))
    small = dict(sinks=dsk[:, 0], r_conv_b=dcb_r[0], r_wa=dwa, r_ba=dba[0], r_wx=dwx, r_bx=dbx[0], r_lam=dlam[0],
                 g_a_log=dpv[0, 4:8], g_dt_bias=dpv[1, 4:8], g_norm_w=dnw[0], r_conv_w=dcw_r, g_conv_w=dcw_g)
    side = side_dw_in(small, dwo, out_dmix) if side_dw_in else None
    dwin = _matmul(x_b, dproj, ta=True, tb=False, tm=_tile(D, 1024), tn=NP // 4, tk=_tile(S_, 2048),
                   out_dtype=F32, name=f"dw_in_{l}", side=side)
    out_dw_in = None
    if side:
        dwin, out_dw_in = dwin
    side = side_dx(dwin) if side_dx else None
    tmx = _tile(S_, 1024)
    nblk = S_ // tmx
    dx_args = dict(ta=False, tb=True, tm=tmx, tn=_tile(D, 1024), tk=NP // 2, out_dtype=F32, extra=dz,
                   alpha=DEEPNORM_ALPHA)
    out_dx = None
    if side and nblk >= 4:
        head = nblk - nblk // 4
        dx, out_dx = _matmul(dproj, wb, name=f"dx_{l}", side=side, rows=(0, head), **dx_args)
        dx = _matmul(dproj, wb, name=f"dx_{l}_rest", rows=(head, nblk - head), into=dx, **dx_args)
    elif side:
        dx, out_dx = _matmul(dproj, wb, name=f"dx_{l}", side=side, **dx_args)
    else:
        dx = _matmul(dproj, wb, name=f"dx_{l}", **dx_args)
    return dx, dwin, dwo, small, out_dw_in, out_dx


def kernel(x, w_in, sinks, r_conv_w, r_conv_b, r_wa, r_ba, r_wx, r_bx, r_lam, g_conv_w, g_a_log, g_dt_bias, g_norm_w, w_out, ln_g, ln_b, loss_target, m_w_in, m_sinks, m_r_conv_w, m_r_conv_b, m_r_wa, m_r_ba, m_r_wx, m_r_bx, m_r_lam, m_g_conv_w, m_g_a_log, m_g_dt_bias, m_g_norm_w, m_w_out, m_ln_g, m_ln_b, v_w_in, v_sinks, v_r_conv_w, v_r_conv_b, v_r_wa, v_r_ba, v_r_wx, v_r_bx, v_r_lam, v_g_conv_w, v_g_a_log, v_g_dt_bias, v_g_norm_w, v_w_out, v_ln_g, v_ln_b):
    S_, D = x.shape[1], x.shape[2]
    nsh = w_in.shape[2]
    rsh = w_out.shape[1]
    cx, cy, cc = lax.axis_index("x"), lax.axis_index("y"), lax.axis_index("c")
    chip = 2 * cx + cy
    rcw_n, gcw_n = r_conv_w.shape[2], g_conv_w.shape[2]

    conv_pack = jnp.concatenate([r_conv_w, g_conv_w], 2)
    w_in_b, w_out_b = w_in.astype(BF16), w_out.astype(BF16)
    g_in0, g_conv = _gather_layer0(w_in_b, conv_pack)

    def shards(own, got):
        return [jnp.where(chip == t, own, got[t]) for t in range(4)]

    def w_in_of(l, g_in):
        return _perm_cols(jnp.concatenate(shards(w_in_b[l], g_in), 1))

    def w_out_of(l, g_out):
        return _perm_rows(jnp.concatenate(shards(w_out_b[l], g_out), 0))

    rcw = jnp.concatenate(shards(r_conv_w, g_conv[:, :, :, :rcw_n]), 2)
    gcw = jnp.concatenate(shards(g_conv_w, g_conv[:, :, :, rcw_n:]), 2)

    pos = jnp.arange(S_, dtype=F32)[:, None]
    inv = 1.0 / (ROPE_THETA ** (jnp.arange(0, A_HEAD_DIM, 2, dtype=F32) / A_HEAD_DIM))
    ang = pos * inv[None, :]
    cos, sin = jnp.cos(ang), jnp.sin(ang)
    rope_c = jnp.concatenate([cos, cos, cos, cos], 1)
    rope_s = jnp.concatenate([-sin, sin, -sin, sin], 1)

    def params(l):
        return dict(sinks=sinks[l], r_conv_w=rcw[l], r_conv_b=r_conv_b[l], r_wa=r_wa[l], r_ba=r_ba[l],
                    r_wx=r_wx[l], r_bx=r_bx[l], r_lam=r_lam[l], g_conv_w=gcw[l], g_a_log=g_a_log[l],
                    g_dt_bias=g_dt_bias[l], g_norm_w=g_norm_w[l])

    assert DEPTH == 2
    xb0 = x[0].astype(BF16)
    wb, wob = [w_in_of(0, g_in0), None], [None, None]
    late = {}

    def w_out_0(arrived):
        late["w_in_1"], g_out0 = _gather_join(arrived, "gather_join_0")
        wob[0] = w_out_of(0, g_out0)
        return wob[0]

    def w_out_1(arrived):
        wob[1] = w_out_of(1, _gather_join(arrived, "gather_join_1")[0])
        return wob[1]

    sv0 = _layer_fwd(0, x[0], xb0, wb[0], w_out_0, (ln_g[0], ln_b[0]), rope_c, rope_s, params(0),
                     side=_gather_side([(w_in_b, 1), (w_out_b, 0)]))
    wb[1] = w_in_of(1, late["w_in_1"])
    sv1 = _layer_fwd(1, sv0["y"], sv0["yb"], wb[1], w_out_1, (ln_g[1], ln_b[1]), rope_c, rope_s, params(1),
                     side=_gather_side([(w_out_b, 1)]), target=loss_target[0])
    saved, xbs = [sv0, sv1], [xb0, sv0["yb"]]

    tm_ln = _tile(S_, 256)
    dz, dzb, dg_l, db_l, loss_part = saved[-1]["head"]
    assert DEPTH == 2
    wcov = (-(-nsh // 128) + 1) * 128
    names = list(_SMALL)

    def own(a):
        return lax.dynamic_index_in_dim(a, chip, 0, keepdims=False)

    def sum_in(l, cp, arrived):
        return _sum4(arrived, own(cp), tr=_tile(D // 2, 256), name=f"chip_sum_w_in_{l}")

    def sum_out(l, cp, arrived):
        return _sum4(arrived, own(cp), tr=rsh // 2, name=f"chip_sum_w_out_{l}")

    dlng, dlnb = [None, dg_l[0]], [None, db_l[0]]
    dx, dwin1, dwo1, small1, _, _ = _layer_bwd(1, saved[1], xbs[1], dz, dzb, wb[1], wob[1], rope_c, rope_s, params(1))
    dwo1_4 = dwo1.reshape(4, rsh, D)
    dz, dzb, dg_l, db_l, _ = _ln_bwd(saved[0]["z"], ln_g[0], ln_b[0], dx, tm=tm_ln, name="ln_bwd_0")
    dlng[0], dlnb[0] = dg_l[0], db_l[0]

    held = {}

    def side_dmix(dwo0):
        return _swap_side([dwin1, dwo1_4, dwo0.reshape(4, rsh, D)], [0, 1, 1])

    def side_dw_in(small0, dwo0, got):
        sm = {k: jnp.stack([small0[k], small1[k]]) for k in small0}
        sm["ln_g"], sm["ln_b"] = jnp.stack(dlng), jnp.stack(dlnb)
        gs = _pack([sm[n] for n in names])
        (got_s,) = _swap_halves([gs], [0], "reduce_pair_small")
        held["in_cp1"] = _pair_sum_windows(dwin1, got[0], nsh, wcov, out_dtype=BF16, name="pair_sum_w_in_1")
        held["out_cp1"] = _pair_sum_blocks(dwo1_4, got[1], out_dtype=BF16, name="pair_sum_w_out_1")
        held["out_cp0"] = _pair_sum_blocks(dwo0.reshape(4, rsh, D), got[2], out_dtype=BF16, name="pair_sum_w_out_0")
        held["s_cp"] = _pair_sum_blocks(gs[None], got_s[None], out_dtype=F32, name="pair_sum_small")[0]
        held["shapes"] = [sm[n].shape for n in names]
        return _chips_side([held["in_cp1"], held["out_cp1"], held["out_cp0"], held["s_cp"]],
                           [True, True, True, False])

    def side_dx(dwin0):
        got = _swap_halves([dwin0], [0], "reduce_pair_0b")
        held["in_cp0"] = _pair_sum_windows(dwin0, got[0], nsh, wcov, out_dtype=BF16, name="pair_sum_w_in_0")
        return _chips_side([held["in_cp0"]], [True])

    dx, _, _, _, arrived_a, arrived_b = _layer_bwd(0, saved[0], xbs[0], dz, dzb, wb[0], wob[0], rope_c, rope_s,
                                                   params(0), side_dmix=side_dmix, side_dw_in=side_dw_in,
                                                   side_dx=side_dx)
    grad_x = dx[None]
    loss = lax.psum(loss_part[0, 0], ("x", "y", "c"))
    s_cp = held["s_cp"]
    mine = [sum_in(0, held["in_cp0"], arrived_b[0]), sum_out(0, held["out_cp0"], arrived_a[2]),
            sum_in(1, held["in_cp1"], arrived_a[0]), sum_out(1, held["out_cp1"], arrived_a[1]),
            _sum4(arrived_a[3], s_cp, tr=s_cp.shape[0], name="chip_sum_small")]
    other = _swap_whole(mine, "reduce_join")

    def both(k, axis):
        return jnp.where(cc == 0, jnp.concatenate([mine[k], other[k]], axis),
                         jnp.concatenate([other[k], mine[k]], axis))

    g_w_in = lax.dynamic_slice_in_dim(jnp.stack([both(2 * l, 0) for l in range(DEPTH)]), (nsh * chip) % 128, nsh, 2)
    g_w_out = jnp.stack([both(2 * l + 1, 0) for l in range(DEPTH)])
    g_small = both(2 * DEPTH, 0)

    gsm = dict(zip(names, _unpack(g_small, held["shapes"])))
    gsm["r_conv_w"] = lax.dynamic_slice_in_dim(gsm["r_conv_w"], chip * rcw_n, rcw_n, 2)
    gsm["g_conv_w"] = lax.dynamic_slice_in_dim(gsm["g_conv_w"], chip * gcw_n, gcw_n, 2)
    wts = dict(sinks=sinks, r_conv_w=r_conv_w, r_conv_b=r_conv_b, r_wa=r_wa, r_ba=r_ba, r_wx=r_wx, r_bx=r_bx,
               r_lam=r_lam, g_conv_w=g_conv_w, g_a_log=g_a_log, g_dt_bias=g_dt_bias, g_norm_w=g_norm_w,
               ln_g=ln_g, ln_b=ln_b)
    mom = dict(sinks=m_sinks, r_conv_w=m_r_conv_w, r_conv_b=m_r_conv_b, r_wa=m_r_wa, r_ba=m_r_ba, r_wx=m_r_wx,
               r_bx=m_r_bx, r_lam=m_r_lam, g_conv_w=m_g_conv_w, g_a_log=m_g_a_log, g_dt_bias=m_g_dt_bias,
               g_norm_w=m_g_norm_w, ln_g=m_ln_g, ln_b=m_ln_b)
    vel = dict(sinks=v_sinks, r_conv_w=v_r_conv_w, r_conv_b=v_r_conv_b, r_wa=v_r_wa, r_ba=v_r_ba, r_wx=v_r_wx,
               r_bx=v_r_bx, r_lam=v_r_lam, g_conv_w=v_g_conv_w, g_a_log=v_g_a_log, g_dt_bias=v_g_dt_bias,
               g_norm_w=v_g_norm_w, ln_g=v_ln_g, ln_b=v_ln_b)
    d_s, m_s, v_s = _adamw_many(*[[d[n] for n in names] for d in (wts, gsm, mom, vel)], name="adamw_small")
    d_sm, m_sm, v_sm = (dict(zip(names, a)) for a in (d_s, m_s, v_s))
    def cols(a):
        return jnp.transpose(a, (2, 0, 1))

    g_w_in_t = cols(g_w_in)
    outs_t = _adamw_cols(cols(w_in), g_w_in_t, cols(m_w_in), cols(v_w_in), name="adamw_w_in")
    d_in, m_in, v_in = (jnp.transpose(a, (1, 2, 0)) for a in outs_t)
    g_w_in = jnp.transpose(g_w_in_t, (1, 2, 0))
    d_out, m_out, v_out = _adamw(w_out, g_w_out, m_w_out, v_w_out, tr=256, name="adamw_w_out")

    order = ["w_in", "sinks", "r_conv_w", "r_conv_b", "r_wa", "r_ba", "r_wx", "r_bx", "r_lam", "g_conv_w",
             "g_a_log", "g_dt_bias", "g_norm_w", "w_out", "ln_g", "ln_b"]
    grads = dict(gsm, w_in=g_w_in, w_out=g_w_out)
    deltas = dict(d_sm, w_in=d_in, w_out=d_out)
    new_m = dict(m_sm, w_in=m_in, w_out=m_out)
    new_v = dict(v_sm, w_in=v_in, w_out=v_out)
    return (loss, grad_x, *[grads[n] for n in order], *[deltas[n] for n in order],
            *[new_m[n] for n in order], *[new_v[n] for n in order])
```

```python
import functools
import math

import jax
import jax.numpy as jnp
import numpy as np
from jax import lax
from jax.experimental import pallas as pl
from jax.experimental.pallas import tpu as pltpu

F32 = jnp.float32
BF16 = jnp.bfloat16
MESH = pl.DeviceIdType.MESH

DEPTH = 2
A_HEADS, A_KV_HEADS, A_HEAD_DIM = 8, 2, 64
A_WIDTH, A_KV_WIDTH = 512, 128
WINDOW = 128
ROPE_THETA = 10000.0
R_WIDTH, R_BLOCKS, R_BLOCK_DIM, R_C = 1024, 8, 128, 8.0
CONV_WIDTH = 4
G_HEADS, G_HEAD_DIM, G_WIDTH, G_CHUNK = 4, 128, 512, 64
MIX_WIDTH = 2048
IN_SIZES = (512, 128, 128, 512, 1024, 1024, 512, 512, 512, 512, 4, 4)
N_IN = 5384
DEEPNORM_ALPHA = (2 * DEPTH) ** 0.25
LN_EPS = 1e-5
RMS_EPS = 1e-6
ADAM_LR, ADAM_B1, ADAM_B2, ADAM_EPS, ADAM_WD, ADAM_STEP = 0.001, 0.9, 0.999, 1e-08, 0.01, 10

NP = 5632
OFF_GQKV, OFF_GZ, OFF_RX, OFF_RZ, OFF_AQ, OFF_AZ, OFF_AK, OFF_AV, OFF_GBA = (
    0, 1536, 2048, 3072, 4096, 4608, 5120, 5248, 5376)
_ORIG_OFF = np.concatenate([[0], np.cumsum(IN_SIZES)])[:-1]
_PIECES = ((6, OFF_GQKV), (7, OFF_GQKV + 512), (8, OFF_GQKV + 1024), (9, OFF_GZ), (4, OFF_RX), (5, OFF_RZ),
           (0, OFF_AQ), (3, OFF_AZ), (1, OFF_AK), (2, OFF_AV), (10, OFF_GBA), (11, OFF_GBA + 4))


def _orig_block_to_perm():
    table = list(range(NP // 128))
    for oi, off in _PIECES:
        if IN_SIZES[oi] % 128 == 0:
            for k in range(IN_SIZES[oi] // 128):
                table[int(_ORIG_OFF[oi]) // 128 + k] = off // 128 + k
    return table
MIX_R, MIX_A, MIX_G = 0, 1024, 1536
VMEM_LIMIT = 56 * 1024 * 1024
ANY_SPEC = pl.BlockSpec(memory_space=pl.ANY)


def _pcall(body, **kw):
    return pl.pallas_call(body, **kw)


def _cp(sem, limit=VMEM_LIMIT):
    return pltpu.CompilerParams(dimension_semantics=sem, vmem_limit_bytes=limit)


def _sigmoid(x):
    return 0.5 + 0.5 * jnp.tanh(0.5 * x)


def _silu(x):
    return x * _sigmoid(x)


def _dsilu(x):
    s = _sigmoid(x)
    return s * (1.0 + x * (1.0 - s))


def _log1p(x):
    u = 1.0 + x
    d = jnp.where(u == 1.0, 1.0, u - 1.0)
    return jnp.where(u == 1.0, x, jnp.log(u) * (x / d))


def _softplus(x):
    return jnp.maximum(x, 0.0) + _log1p(jnp.exp(-jnp.abs(x)))


def _one_minus_exp(x):
    series = -x * (1.0 + x * (0.5 + x * (1.0 / 6.0 + x * (1.0 / 24.0))))
    return jnp.where(x > -0.05, series, 1.0 - jnp.exp(x))


def _nn(a, b):
    return lax.dot_general(a, b, (((1,), (0,)), ((), ())), preferred_element_type=F32)


def _nt(a, b):
    return lax.dot_general(a, b, (((1,), (1,)), ((), ())), preferred_element_type=F32)


def _tn(a, b):
    return lax.dot_general(a, b, (((0,), (0,)), ((), ())), preferred_element_type=F32)


def _b(x):
    return x.astype(BF16)


def _split3(x):
    hi = x.astype(BF16)
    r1 = x - hi.astype(F32)
    mid = r1.astype(BF16)
    lo = (r1 - mid.astype(F32)).astype(BF16)
    return hi, mid, lo


def _dot3(f, a, b):
    ah, am, _ = _split3(a)
    bh, bm, _ = _split3(b)
    return f(ah, bh) + (f(ah, bm) + f(am, bh))


def _dot_exact_lhs(f, a_bf16, b):
    bh, bm, bl = _split3(b)
    return f(a_bf16, bh) + (f(a_bf16, bm) + f(a_bf16, bl))


def _rot(x):
    w = x.shape[-1]
    lane = lax.broadcasted_iota(jnp.int32, (1, w), 1)
    return jnp.where((lane & 63) < 32, pltpu.roll(x, w - 32, 1), pltpu.roll(x, 32, 1))


def _conv_taps(ext, n):
    return [pltpu.roll(ext, 3 - k, 0)[8:8 + n] if k < 3 else ext[8:8 + n] for k in range(CONV_WIDTH)]


def _conv_taps_t(ext, n):
    m = ext.shape[0]
    return [pltpu.roll(ext, m - (3 - k), 0)[0:n] if k < 3 else ext[0:n] for k in range(CONV_WIDTH)]


def _scan_steps(a, b, pos, span, shifts, reverse):
    n = a.shape[0]
    for s in shifts:
        if reverse:
            a_sh = pltpu.roll(a, n - s, 0)
            b_sh = pltpu.roll(b, n - s, 0)
            ok = pos < (span - s)
        else:
            a_sh = pltpu.roll(a, s, 0)
            b_sh = pltpu.roll(b, s, 0)
            ok = pos >= s
        b = jnp.where(ok, a * b_sh + b, b)
        a = jnp.where(ok, a * a_sh, a)
    return a, b


def _scan_lin(a, b, reverse):
    n = a.shape[0]
    shifts = []
    s = 1
    while s < n:
        shifts.append(s)
        s *= 2
    return _scan_steps(a, b, lax.broadcasted_iota(jnp.int32, (n, 1), 0), n, shifts, reverse)


class _Side:
    def __init__(self, inputs, out_shapes, n_sems, start, finish):
        self.inputs, self.out_shapes, self.n_sems, self.start, self.finish = inputs, out_shapes, n_sems, start, finish


def _matmul(a, b, *, ta, tb, tm, tn, tk, out_dtype, name, extra=None, alpha=0.0, out_blocks=None, side=None,
            rows=None, into=None):
    if ta:
        K, M = a.shape
    else:
        M, K = a.shape
    if tb:
        N, K2 = b.shape
    else:
        K2, N = b.shape
    assert K == K2 and M % tm == 0 and N % tn == 0 and K % tk == 0, (a.shape, b.shape, tm, tn, tk)
    nk = K // tk
    ca = 0 if ta else 1
    cb = 1 if tb else 0
    has_extra = extra is not None

    assert nk == 1 or out_dtype == F32
    assert rows is None or (not ta and out_blocks is None)
    r0, nrow = rows if rows else (0, M // tm)
    n_in = 2 + int(has_extra) + int(into is not None)
    ns_in = len(side.inputs) if side else 0
    ns_out = len(side.out_shapes) if side else 0
    grid = (nrow, N // tn, nk)

    def body(*refs):
        a_ref, b_ref = refs[0], refs[1]
        e_ref = refs[2] if has_extra else None
        o_ref = refs[n_in + ns_in]
        k = pl.program_id(2)
        if side:
            s_in = refs[n_in:n_in + ns_in]
            s_out = refs[n_in + ns_in + 1:n_in + ns_in + 1 + ns_out]
            ssem, rsem = refs[-2], refs[-1]
            i, j = pl.program_id(0), pl.program_id(1)

            @pl.when((i == 0) & (j == 0) & (k == 0))
            def _():
                side.start(s_in, s_out, ssem, rsem)

            @pl.when((i == grid[0] - 1) & (j == grid[1] - 1) & (k == grid[2] - 1))
            def _():
                side.finish(s_in, s_out, ssem, rsem)

        part = lax.dot_general(a_ref[...], b_ref[...], (((ca,), (cb,)), ((), ())), preferred_element_type=F32)
        if nk == 1:
            if e_ref is not None:
                part = part + alpha * e_ref[...]
            o_ref[...] = part.astype(o_ref.dtype)
            return

        @pl.when(k == 0)
        def _():
            o_ref[...] = part

        @pl.when((k > 0) & (k < nk - 1))
        def _():
            o_ref[...] += part

        @pl.when(k == nk - 1)
        def _():
            last = o_ref[...] + part
            if e_ref is not None:
                last = last + alpha * e_ref[...]
            o_ref[...] = last

    a_spec = (pl.BlockSpec((tk, tm), lambda i, j, k: (k, i)) if ta
              else pl.BlockSpec((tm, tk), lambda i, j, k: (i + r0, k)))
    b_spec = (pl.BlockSpec((tn, tk), lambda i, j, k: (j, k)) if tb
              else pl.BlockSpec((tk, tn), lambda i, j, k: (k, j)))
    e_spec = pl.BlockSpec((tm, tn), lambda i, j, k: (i + r0, j))
    if out_blocks is None:
        o_spec, o_shape = e_spec, (M, N)
    else:
        o_shape, o_block, o_map = out_blocks
        o_spec = pl.BlockSpec(o_block, lambda i, j, k: o_map(i, j))
    in_specs = [a_spec, b_spec] + ([e_spec] if has_extra else []) + ([ANY_SPEC] if into is not None else [])
    args = (a, b) + ((extra,) if has_extra else ()) + ((into,) if into is not None else ())
    alias = {n_in - 1: 0} if into is not None else {}
    if not side:
        return _pcall(
            body, name=name, grid=grid, in_specs=in_specs, out_specs=o_spec,
            out_shape=jax.ShapeDtypeStruct(o_shape, out_dtype), input_output_aliases=alias,
            compiler_params=_cp(("parallel", "parallel", "arbitrary")),
        )(*args)
    outs = _pcall(
        body, name=name, grid=grid, in_specs=in_specs + [HBM_SPEC] * ns_in,
        out_specs=[o_spec] + [HBM_SPEC] * ns_out,
        out_shape=[jax.ShapeDtypeStruct(o_shape, out_dtype)] + list(side.out_shapes), input_output_aliases=alias,
        scratch_shapes=[pltpu.SemaphoreType.DMA((side.n_sems,)), pltpu.SemaphoreType.DMA((side.n_sems,))],
        compiler_params=_cp(("arbitrary", "arbitrary", "arbitrary")),
    )(*args, *side.inputs)
    return outs[0], outs[1:]


def _ln_stats(z):
    mu = jnp.mean(z, -1, keepdims=True)
    zc = z - mu
    var = jnp.mean(zc * zc, -1, keepdims=True)
    rstd = lax.rsqrt(var + LN_EPS)
    return zc * rstd, rstd


def _ln_bwd_tile(z, gam, bet, other, from_target, dz_ref, dzb_ref, dg_ref, db_ref, loss_ref):
    i = pl.program_id(0)

    @pl.when(i == 0)
    def _():
        dg_ref[...] = jnp.zeros_like(dg_ref)
        db_ref[...] = jnp.zeros_like(db_ref)
        loss_ref[...] = jnp.zeros_like(loss_ref)

    xh, rstd = _ln_stats(z)
    if from_target:
        err = xh * gam + bet - other
        per_tok = jnp.mean(err * err, -1, keepdims=True)
        loss_ref[...] += 0.5 * jnp.sum(per_tok, 0, keepdims=True)
        dy = err * (1.0 / z.shape[-1])
    else:
        dy = other
    dxh = dy * gam
    m1 = jnp.mean(dxh, -1, keepdims=True)
    m2 = jnp.mean(dxh * xh, -1, keepdims=True)
    dz = rstd * (dxh - m1 - xh * m2)
    dz_ref[...] = dz
    dzb_ref[...] = dz.astype(BF16)
    dg_ref[...] += jnp.sum(dy * xh, 0, keepdims=True)
    db_ref[...] += jnp.sum(dy, 0, keepdims=True)


def _outproj(ymix, wo, x, g, b, *, tm, name, target=None):
    S_, D = x.shape
    last = target is not None

    def body(*refs):
        y_ref, w_ref, x_ref, g_ref, b_ref = refs[:5]
        z = DEEPNORM_ALPHA * x_ref[...] + _nn(y_ref[...], w_ref[...])
        if last:
            _ln_bwd_tile(z, g_ref[...], b_ref[...], refs[5][...], True, *refs[6:])
            return
        z_ref, o_ref, ob_ref = refs[5:]
        z_ref[...] = z
        xh, _ = _ln_stats(z)
        y = xh * g_ref[...] + b_ref[...]
        o_ref[...] = y
        ob_ref[...] = y.astype(BF16)

    row = pl.BlockSpec((tm, D), lambda i: (i, 0))
    vec = pl.BlockSpec((1, D), lambda i: (0, 0))
    one = pl.BlockSpec((1, 1), lambda i: (0, 0))
    in_specs = [pl.BlockSpec((tm, MIX_WIDTH), lambda i: (i, 0)), pl.BlockSpec((MIX_WIDTH, D), lambda i: (0, 0)),
                row, vec, vec]
    f32s, b16s = jax.ShapeDtypeStruct((S_, D), F32), jax.ShapeDtypeStruct((S_, D), BF16)
    v32s = jax.ShapeDtypeStruct((1, D), F32)
    args = (ymix, wo, x, g.reshape(1, D), b.reshape(1, D))
    if last:
        return _pcall(body, name=name, grid=(S_ // tm,), in_specs=in_specs + [row],
                      out_specs=[row, row, vec, vec, one],
                      out_shape=[f32s, b16s, v32s, v32s, jax.ShapeDtypeStruct((1, 1), F32)],
                      compiler_params=_cp(("arbitrary",)))(*args, target)
    return _pcall(body, name=name, grid=(S_ // tm,), in_specs=in_specs, out_specs=[row, row, row],
                  out_shape=[f32s, f32s, b16s], compiler_params=_cp(("parallel",)))(*args)


def _ln_bwd(z, g, b, dy, *, tm, name):
    S_, D = z.shape

    def body(z_ref, g_ref, b_ref, o_ref, *outs):
        _ln_bwd_tile(z_ref[...], g_ref[...], b_ref[...], o_ref[...], False, *outs)

    row = pl.BlockSpec((tm, D), lambda i: (i, 0))
    vec = pl.BlockSpec((1, D), lambda i: (0, 0))
    one = pl.BlockSpec((1, 1), lambda i: (0, 0))
    return _pcall(
        body, name=name, grid=(S_ // tm,), in_specs=[row, vec, vec, row],
        out_specs=[row, row, vec, vec, one],
        out_shape=[jax.ShapeDtypeStruct((S_, D), F32), jax.ShapeDtypeStruct((S_, D), BF16),
                   jax.ShapeDtypeStruct((1, D), F32), jax.ShapeDtypeStruct((1, D), F32),
                   jax.ShapeDtypeStruct((1, 1), F32)],
        compiler_params=_cp(("arbitrary",)),
    )(z, g.reshape(1, D), b.reshape(1, D), dy)


def _attn_masks(i, sk_ref):
    ri = lax.broadcasted_iota(jnp.int32, (512, 256), 0)
    cj = lax.broadcasted_iota(jnp.int32, (512, 256), 1)
    diff = (ri & 127) - cj + 128
    band = (diff >= 0) & (diff < WINDOW)
    bias = jnp.where(band, 0.0, -jnp.inf)
    bias0 = jnp.where(band & ((i > 0) | (cj >= 128)), 0.0, -jnp.inf)
    grp = lax.broadcasted_iota(jnp.int32, (512, 1), 0) >> 7
    skvs = []
    for h in range(A_KV_HEADS):
        skv = jnp.zeros((512, 1), F32)
        for g in range(4):
            skv = jnp.where(grp == g, sk_ref[h * 4 + g], skv)
        skvs.append(skv)
    return bias0, bias, skvs


def _attn_common(masks, b, h, qr, kd, vd):
    lane = lax.broadcasted_iota(jnp.int32, (1, 128), 1)
    lof = (lane < 64).astype(F32)
    hif = 1.0 - lof
    r0 = b * 128
    skv = masks[2][h]
    pairs = [qr[r0:r0 + 128, h * 256 + p * 128:h * 256 + (p + 1) * 128] for p in (0, 1)]
    qs = _b(jnp.concatenate([pairs[0] * lof, pairs[0] * hif, pairs[1] * lof, pairs[1] * hif], 0))
    k2 = kd[h][r0:r0 + 256]
    v2 = vd[h][r0:r0 + 256]
    s = _nt(qs, k2) * (A_HEAD_DIM ** -0.5) + (masks[0] if b == 0 else masks[1])
    m = jnp.maximum(jnp.max(s, 1, keepdims=True), skv)
    p = jnp.exp(s - m)
    esk = jnp.exp(skv - m)
    rz = 1.0 / (jnp.sum(p, 1, keepdims=True) + esk)
    prob = p * rz
    o4 = _nn(_b(prob), v2)
    return lof, hif, qs, k2, v2, prob, esk * rz, o4


def _attn_prep(T, q_ref, k_ref, v_ref, c_ref, s_ref, kprev, vprev):
    C = c_ref[...]
    Sg = s_ref[...]
    C4 = jnp.concatenate([C] * 4, 1)
    S4 = jnp.concatenate([Sg] * 4, 1)
    q = q_ref[...]
    qr = q * C4 + _rot(q) * S4
    k = k_ref[...]
    kr = k * C + _rot(k) * Sg
    v = v_ref[...]
    kext = jnp.concatenate([kprev[...], kr], 0)
    vext = jnp.concatenate([vprev[...], v], 0)
    kprev[...] = kr[T - 128:]
    vprev[...] = v[T - 128:]
    lo = lax.broadcasted_iota(jnp.int32, (1, 128), 1) < 64
    kroll = pltpu.roll(kext, 64, 1)
    vroll = pltpu.roll(vext, 64, 1)
    kd = [_b(jnp.where(lo, kext, kroll)), _b(jnp.where(lo, kroll, kext))]
    vd = [_b(jnp.where(lo, vext, vroll)), _b(jnp.where(lo, vroll, vext))]
    return C, Sg, C4, S4, qr, kd, vd


def _attn_specs(T):
    return [pl.BlockSpec(memory_space=pltpu.SMEM),
            pl.BlockSpec((T, 512), lambda i: (i, OFF_AQ // 512)),
            pl.BlockSpec((T, 512), lambda i: (i, OFF_AZ // 512)),
            pl.BlockSpec((T, 128), lambda i: (i, OFF_AK // 128)),
            pl.BlockSpec((T, 128), lambda i: (i, OFF_AV // 128)),
            pl.BlockSpec((T, 128), lambda i: (i, 0)),
            pl.BlockSpec((T, 128), lambda i: (i, 0))]


def _attn_fwd(proj, rope_c, rope_s, sinks, ymix, *, T, name):
    S_ = proj.shape[0]
    nb = T // 128

    def body(sk_ref, q_ref, z_ref, k_ref, v_ref, c_ref, s_ref, _, y_ref, kprev, vprev):
        i = pl.program_id(0)

        @pl.when(i == 0)
        def _():
            kprev[...] = jnp.zeros_like(kprev)
            vprev[...] = jnp.zeros_like(vprev)

        _, _, _, _, qr, kd, vd = _attn_prep(T, q_ref, k_ref, v_ref, c_ref, s_ref, kprev, vprev)
        masks = _attn_masks(i, sk_ref)
        for b in range(nb):
            r0 = b * 128
            for h in range(2):
                lof, hif, _, _, _, _, _, o4 = _attn_common(masks, b, h, qr, kd, vd)
                for p in range(2):
                    cs = slice(h * 256 + p * 128, h * 256 + (p + 1) * 128)
                    o = o4[2 * p * 128:(2 * p + 1) * 128] * lof + o4[(2 * p + 1) * 128:(2 * p + 2) * 128] * hif
                    y_ref[r0:r0 + 128, cs] = (o * _silu(z_ref[r0:r0 + 128, cs])).astype(BF16)

    return _pcall(
        body, name=name, grid=(S_ // T,), in_specs=_attn_specs(T) + [ANY_SPEC],
        out_specs=pl.BlockSpec((T, 512), lambda i: (i, MIX_A // 512)),
        out_shape=jax.ShapeDtypeStruct(ymix.shape, BF16),
        input_output_aliases={7: 0},
        scratch_shapes=[pltpu.VMEM((128, 128), F32), pltpu.VMEM((128, 128), F32)],
        compiler_params=_cp(("arbitrary",)),
    )(sinks, proj, proj, proj, proj, rope_c, rope_s, ymix)


def _attn_bwd(proj, rope_c, rope_s, sinks, dymix, *, T, name):
    S_ = proj.shape[0]
    nb = T // 128
    nt = S_ // T

    def body(sk_ref, q_ref, z_ref, k_ref, v_ref, c_ref, s_ref, dy_ref,
             dp_ref, dk_ref, dv_ref, dkt_ref, dvt_ref, dsk_ref,
             kprev, vprev, cprev, sprev, dkacc, dvacc, dqacc):
        i = pl.program_id(0)

        @pl.when(i == 0)
        def _():
            kprev[...] = jnp.zeros_like(kprev)
            vprev[...] = jnp.zeros_like(vprev)
            cprev[...] = jnp.zeros_like(cprev)
            sprev[...] = jnp.zeros_like(sprev)
            dkacc[...] = jnp.zeros_like(dkacc)
            dvacc[...] = jnp.zeros_like(dvacc)
            dsk_ref[...] = jnp.zeros_like(dsk_ref)

        @pl.when(i > 0)
        def _():
            dkacc[0:128, :] = dkacc[T:T + 128, :]
            dvacc[0:128, :] = dvacc[T:T + 128, :]
            dkacc[128:, :] = jnp.zeros((T, 128), F32)
            dvacc[128:, :] = jnp.zeros((T, 128), F32)

        C, Sg, C4, S4, qr, kd, vd = _attn_prep(T, q_ref, k_ref, v_ref, c_ref, s_ref, kprev, vprev)
        masks = _attn_masks(i, sk_ref)
        lane = lax.broadcasted_iota(jnp.int32, (1, 128), 1)
        for b in range(nb):
            r0 = b * 128
            for h in range(2):
                lof, hif, qs, k2, v2, prob, psink, o4 = _attn_common(masks, b, h, qr, kd, vd)
                dos = []
                for p in range(2):
                    cs = slice(h * 256 + p * 128, h * 256 + (p + 1) * 128)
                    o = o4[2 * p * 128:(2 * p + 1) * 128] * lof + o4[(2 * p + 1) * 128:(2 * p + 2) * 128] * hif
                    zc = z_ref[r0:r0 + 128, cs]
                    dyc = dy_ref[r0:r0 + 128, cs]
                    dp_ref[r0:r0 + 128, 512 + cs.start:512 + cs.stop] = (dyc * o * _dsilu(zc)).astype(BF16)
                    do = dyc * _silu(zc)
                    dos += [do * lof, do * hif]
                dos = jnp.concatenate(dos, 0)
                os_ = jnp.concatenate([o4[0:128] * lof, o4[128:256] * hif, o4[256:384] * lof, o4[384:512] * hif], 0)
                delta = jnp.sum(dos * os_, 1, keepdims=True)
                dosb = _b(dos)
                dp = _nt(dosb, v2)
                ds = prob * (dp - delta)
                dsv = -psink * delta
                for g in range(4):
                    sg = jnp.sum(dsv[g * 128:(g + 1) * 128], 0, keepdims=True)
                    hd = h * 4 + g
                    dsk_ref[hd:hd + 1, :] += jnp.broadcast_to(sg, (1, 128))
                dsb = _b(ds * (A_HEAD_DIM ** -0.5))
                dqs = _nn(dsb, k2)
                for p in range(2):
                    cs = slice(h * 256 + p * 128, h * 256 + (p + 1) * 128)
                    dqacc[r0:r0 + 128, cs] = (dqs[2 * p * 128:(2 * p + 1) * 128] * lof
                                              + dqs[(2 * p + 1) * 128:(2 * p + 2) * 128] * hif)
                dkdup = _tn(dsb, qs)
                dvdup = _tn(_b(prob), dosb)
                half = (lane < 64) if h == 0 else (lane >= 64)
                dkacc[r0:r0 + 256, :] += jnp.where(half, dkdup + pltpu.roll(dkdup, 64, 1), 0.0)
                dvacc[r0:r0 + 256, :] += jnp.where(half, dvdup + pltpu.roll(dvdup, 64, 1), 0.0)
        dqr = dqacc[...]
        dp_ref[:, 0:512] = (dqr * C4 + _rot(dqr * S4)).astype(BF16)
        cext = jnp.concatenate([cprev[...], C], 0)
        sext = jnp.concatenate([sprev[...], Sg], 0)
        dke = dkacc[...]
        dkp = dke * cext + _rot(dke * sext)
        dk_ref[...] = dkp[0:T].astype(BF16)
        dkt_ref[...] = dkp[T:T + 128].astype(BF16)
        dve = dvacc[...]
        dv_ref[...] = dve[0:T].astype(BF16)
        dvt_ref[...] = dve[T:T + 128].astype(BF16)
        cprev[...] = C[T - 128:]
        sprev[...] = Sg[T - 128:]

    nar = pl.BlockSpec((T, 128), lambda i: (i, 0))
    tail = pl.BlockSpec((128, 128), lambda i: (0, 0))
    return _pcall(
        body, name=name, grid=(nt,),
        in_specs=_attn_specs(T) + [pl.BlockSpec((T, 512), lambda i: (i, MIX_A // 512))],
        out_specs=[pl.BlockSpec((T, 1024), lambda i: (i, OFF_AQ // 1024)), nar, nar, tail, tail,
                   pl.BlockSpec((8, 128), lambda i: (0, 0))],
        out_shape=[jax.ShapeDtypeStruct((S_, NP), BF16),
                   jax.ShapeDtypeStruct((S_, 128), BF16), jax.ShapeDtypeStruct((S_, 128), BF16),
                   jax.ShapeDtypeStruct((128, 128), BF16), jax.ShapeDtypeStruct((128, 128), BF16),
                   jax.ShapeDtypeStruct((8, 128), F32)],
        scratch_shapes=[pltpu.VMEM((128, 128), F32)] * 4
        + [pltpu.VMEM((T + 128, 128), F32), pltpu.VMEM((T + 128, 128), F32), pltpu.VMEM((T, 512), F32)],
        compiler_params=_cp(("arbitrary",)),
    )(sinks, proj, proj, proj, proj, rope_c, rope_s, dymix)


def _rg_gates(xr, wa_ref, ba_ref, wx_ref, bx_ref, lam_ref):
    xb = _b(xr)
    pre_a = jnp.concatenate([_nn(xb[:, n * 128:(n + 1) * 128], wa_ref[n]) for n in range(R_BLOCKS)], 1) + ba_ref[...]
    pre_x = jnp.concatenate([_nn(xb[:, n * 128:(n + 1) * 128], wx_ref[n]) for n in range(R_BLOCKS)], 1) + bx_ref[...]
    r = _sigmoid(pre_a)
    ig = _sigmoid(pre_x)
    sp = _softplus(-lam_ref[...])
    log_a = -R_C * r * sp
    a = jnp.exp(log_a)
    mult = jnp.sqrt(_one_minus_exp(2.0 * log_a))
    return xb, r, ig, sp, a, mult


def _rg_param_specs():
    C = R_WIDTH
    vec = pl.BlockSpec((1, C), lambda i: (0, 0))
    blk = pl.BlockSpec((R_BLOCKS, 128, 128), lambda i: (0, 0, 0))
    return [pl.BlockSpec((CONV_WIDTH, C), lambda i: (0, 0)), vec, blk, vec, blk, vec, vec]


def _rglru_fwd(proj, cw, cb, wa, ba, wx, bx, lam, *, T, name):
    S_ = proj.shape[0]
    C = R_WIDTH

    def body(rx_ref, rz_ref, cw_ref, cb_ref, wa_ref, ba_ref, wx_ref, bx_ref, lam_ref,
             h_ref, y_ref, halo, hcar):
        i = pl.program_id(0)

        @pl.when(i == 0)
        def _():
            halo[...] = jnp.zeros_like(halo)
            hcar[...] = jnp.zeros_like(hcar)

        rx = rx_ref[...]
        ext = jnp.concatenate([halo[...], rx], 0)
        halo[...] = rx[T - 8:]
        taps = _conv_taps(ext, T)
        xr = cb_ref[...] + sum(cw_ref[k:k + 1, :] * taps[k] for k in range(CONV_WIDTH))
        _, _, ig, _, a, mult = _rg_gates(xr, wa_ref, ba_ref, wx_ref, bx_ref, lam_ref)
        u = mult * (ig * xr)
        acum, hloc = _scan_lin(a, u, False)
        h = hloc + acum * hcar[0:1, :]
        hcar[...] = jnp.broadcast_to(h[T - 1:T, :], (8, C))
        h_ref[...] = h
        y_ref[...] = (h * _silu(rz_ref[...])).astype(BF16)

    row = pl.BlockSpec((T, C), lambda i: (i, 0))
    return _pcall(
        body, name=name, grid=(S_ // T,),
        in_specs=[pl.BlockSpec((T, C), lambda i: (i, OFF_RX // C)),
                  pl.BlockSpec((T, C), lambda i: (i, OFF_RZ // C))] + _rg_param_specs(),
        out_specs=[row, pl.BlockSpec((T, C), lambda i: (i, MIX_R // C))],
        out_shape=[jax.ShapeDtypeStruct((S_, C), F32), jax.ShapeDtypeStruct((S_, MIX_WIDTH), BF16)],
        scratch_shapes=[pltpu.VMEM((8, C), F32), pltpu.VMEM((8, C), F32)],
        compiler_params=_cp(("arbitrary",)),
    )(proj, proj, cw, cb.reshape(1, C), _b(wa), ba.reshape(1, C), _b(wx), bx.reshape(1, C), lam.reshape(1, C))


def _rglru_bwd(proj, h, dymix, dproj, cw, cb, wa, ba, wx, bx, lam, *, T, name):
    S_ = proj.shape[0]
    C = R_WIDTH
    nt = S_ // T
    t8 = T // 8

    def body(rx_ref, rxp_ref, rz_ref, h_ref, hp_ref, dy_ref,
             cw_ref, cb_ref, wa_ref, ba_ref, wx_ref, bx_ref, lam_ref, wat_ref, wxt_ref,
             _, dp_ref, dcw_ref, dcb_ref, dwa_ref, dba_ref, dwx_ref, dbx_ref, dlam_ref,
             afirst, gfirst, dhalo):
        i = pl.program_id(0)
        first_tile = (i == nt - 1)

        @pl.when(i == 0)
        def _():
            afirst[...] = jnp.zeros_like(afirst)
            gfirst[...] = jnp.zeros_like(gfirst)
            dhalo[...] = jnp.zeros_like(dhalo)
            for r in (dcw_ref, dcb_ref, dwa_ref, dba_ref, dwx_ref, dbx_ref, dlam_ref):
                r[...] = jnp.zeros_like(r)

        keep = jnp.where(first_tile, 0.0, 1.0)
        rx = rx_ref[...]
        ext = jnp.concatenate([rxp_ref[...] * keep, rx], 0)
        taps = _conv_taps(ext, T)
        xr = cb_ref[...] + sum(cw_ref[k:k + 1, :] * taps[k] for k in range(CONV_WIDTH))
        xb, r, ig, sp, a, mult = _rg_gates(xr, wa_ref, ba_ref, wx_ref, bx_ref, lam_ref)
        hh = h_ref[...]
        rz = rz_ref[...]
        dy = dy_ref[...]
        dp_ref[:, C:2 * C] = (dy * hh * _dsilu(rz)).astype(BF16)
        dh = dy * _silu(rz)
        row = lax.broadcasted_iota(jnp.int32, (T, 1), 0)
        c = jnp.where(row == T - 1, afirst[0:1, :], pltpu.roll(a, T - 1, 0))
        ccum, gloc = _scan_lin(c, dh, True)
        g = gloc + ccum * gfirst[0:1, :]
        afirst[...] = jnp.broadcast_to(a[0:1, :], (8, C))
        gfirst[...] = jnp.broadcast_to(g[0:1, :], (8, C))
        hprev = jnp.where(row == 0, hp_ref[7:8, :] * keep, pltpu.roll(hh, 1, 0))
        da = g * hprev
        gx = ig * xr
        dgx = g * mult
        dmult = g * gx
        dlog_a = da * a - dmult * (a * a) * lax.rsqrt(mult * mult)
        dpre_a = dlog_a * (-R_C * sp) * r * (1.0 - r)
        dpre_x = dgx * xr * ig * (1.0 - ig)
        dlam_ref[...] += jnp.sum(dlog_a * (-R_C * r), 0, keepdims=True) * (-_sigmoid(-lam_ref[...]))
        dab = _b(dpre_a)
        dxb = _b(dpre_x)
        dxr = dgx * ig + jnp.concatenate(
            [_nn(dab[:, n * 128:(n + 1) * 128], wat_ref[n]) + _nn(dxb[:, n * 128:(n + 1) * 128], wxt_ref[n])
             for n in range(R_BLOCKS)], 1)
        for n in range(R_BLOCKS):
            cs = slice(n * 128, (n + 1) * 128)
            dwa_ref[n] += _tn(xb[:, cs], dab[:, cs])
            dwx_ref[n] += _tn(xb[:, cs], dxb[:, cs])
        dba_ref[...] += jnp.sum(dpre_a, 0, keepdims=True)
        dbx_ref[...] += jnp.sum(dpre_x, 0, keepdims=True)
        dcb_ref[...] += jnp.sum(dxr, 0, keepdims=True)
        for k in range(CONV_WIDTH):
            dcw_ref[k:k + 1, :] += jnp.sum(dxr * taps[k], 0, keepdims=True)
        ext2 = jnp.concatenate([dxr, dhalo[...]], 0)
        tt = _conv_taps_t(ext2, T)
        dp_ref[:, 0:C] = sum(cw_ref[k:k + 1, :] * tt[k] for k in range(CONV_WIDTH)).astype(BF16)
        dhalo[...] = dxr[0:8]

    def rev(i):
        return nt - 1 - i

    def prev8(i):
        return jnp.maximum(rev(i) * t8 - 1, 0)

    vec = pl.BlockSpec((1, C), lambda i: (0, 0))
    blk = pl.BlockSpec((R_BLOCKS, 128, 128), lambda i: (0, 0, 0))
    row = pl.BlockSpec((T, C), lambda i: (rev(i), 0))
    wat = _b(jnp.swapaxes(wa, 1, 2))
    wxt = _b(jnp.swapaxes(wx, 1, 2))
    return _pcall(
        body, name=name, grid=(nt,),
        in_specs=[pl.BlockSpec((T, C), lambda i: (rev(i), OFF_RX // C)),
                  pl.BlockSpec((8, C), lambda i: (prev8(i), OFF_RX // C)),
                  pl.BlockSpec((T, C), lambda i: (rev(i), OFF_RZ // C)),
                  row,
                  pl.BlockSpec((8, C), lambda i: (prev8(i), 0)),
                  pl.BlockSpec((T, C), lambda i: (rev(i), MIX_R // C)),
                  ] + _rg_param_specs() + [blk, blk, ANY_SPEC],
        out_specs=[pl.BlockSpec((T, 2 * C), lambda i: (rev(i), OFF_RX // (2 * C))),
                   pl.BlockSpec((CONV_WIDTH, C), lambda i: (0, 0)), vec, blk, vec, blk, vec, vec],
        out_shape=[jax.ShapeDtypeStruct(dproj.shape, BF16),
                   jax.ShapeDtypeStruct((CONV_WIDTH, C), F32), jax.ShapeDtypeStruct((1, C), F32),
                   jax.ShapeDtypeStruct((R_BLOCKS, 128, 128), F32), jax.ShapeDtypeStruct((1, C), F32),
                   jax.ShapeDtypeStruct((R_BLOCKS, 128, 128), F32), jax.ShapeDtypeStruct((1, C), F32),
                   jax.ShapeDtypeStruct((1, C), F32)],
        input_output_aliases={15: 0},
        scratch_shapes=[pltpu.VMEM((8, C), F32)] * 3,
        compiler_params=_cp(("arbitrary",)),
    )(proj, proj, proj, h, h, dymix, cw, cb.reshape(1, C), _b(wa), ba.reshape(1, C), _b(wx), bx.reshape(1, C),
      lam.reshape(1, C), wat, wxt, dproj)


GW3 = 3 * G_WIDTH


def _lane_col(x, lane_idx):
    lane = lax.broadcasted_iota(jnp.int32, (1, x.shape[1]), 1)
    return jnp.sum(jnp.where(lane == lane_idx, x, 0.0), 1, keepdims=True)


def _gdn_pre(ext, T, cw_ref, gba, pv_ref):
    taps = _conv_taps(ext, T)
    c = sum(cw_ref[k:k + 1, :] * taps[k] for k in range(CONV_WIDTH))
    qkv = _silu(c)
    beta = _sigmoid(gba)
    sarg = gba + pv_ref[1:2, :]
    nea = -jnp.exp(pv_ref[0:1, :])
    gdec = nea * _softplus(sarg)
    ri = lax.broadcasted_iota(jnp.int32, (T, T), 0)
    cj = lax.broadcasted_iota(jnp.int32, (T, T), 1)
    same = (ri >> 6) == (cj >> 6)
    ltri = jnp.where((ri >= cj) & same, 1.0, 0.0).astype(BF16)
    gc = _dot_exact_lhs(_nn, ltri, gdec)
    return taps, c, qkv, beta, sarg, nea, gdec, gc


def _gdn_masks():
    ri = lax.broadcasted_iota(jnp.int32, (128, 128), 0)
    cj = lax.broadcasted_iota(jnp.int32, (128, 128), 1)
    same = (ri >> 6) == (cj >> 6)
    return (ri >= cj) & same, (ri > cj) & same, ri == cj


def _lockstep(gens):
    out = [None] * len(gens)
    live = list(range(len(gens)))
    while live:
        still = []
        for k in live:
            try:
                next(gens[k])
                still.append(k)
            except StopIteration as stop:
                out[k] = stop.value
        live = still
    return out


def _gdn_chunk(qkv, beta, gc, rs, h, tm=None):
    tril, strict, eye = _gdn_masks()
    rowi = lax.broadcasted_iota(jnp.int32, (128, 1), 0)
    lane = lax.broadcasted_iota(jnp.int32, (1, 128), 1)
    qh = qkv[rs, h * 128:(h + 1) * 128]
    kh = qkv[rs, 512 + h * 128:512 + (h + 1) * 128]
    vh = qkv[rs, 1024 + h * 128:1024 + (h + 1) * 128]
    rq = lax.rsqrt(jnp.sum(qh * qh, 1, keepdims=True) + RMS_EPS)
    rk = lax.rsqrt(jnp.sum(kh * kh, 1, keepdims=True) + RMS_EPS)
    qn = qh * (rq * (G_HEAD_DIM ** -0.5))
    kn = kh * rk
    gcb = gc[rs]
    gcol = _lane_col(gcb, 4 + h)
    bcol = _lane_col(beta[rs], h)
    grow = _dot_exact_lhs(_nt, jnp.ones((128, 128), BF16), jnp.where(lane == 4 + h, gcb, 0.0))
    D = jnp.where(tril, jnp.exp(jnp.minimum(gcol - grow, 0.0)), 0.0)
    kb = kn * bcol
    vb = vh * bcol
    knb = _b(kn)
    A = _nt(_b(kb), knb)
    Bm = _nt(_b(qn), knb)
    yield
    if tm is None:
        N = jnp.where(strict, -(A * D), 0.0)
        tm = jnp.where(eye, 1.0, 0.0) + N
        npow = N
        for _ in range(5):
            npow = _dot3(_nn, npow, npow)
            yield
            tm = tm + _dot3(_nn, tm, npow)
            yield
    eg = jnp.exp(gcol)
    u = _dot3(_nn, tm, vb)
    w = _dot3(_nn, tm, kb * eg)
    yield
    qk = jnp.where(tril, Bm * D, 0.0)
    qd = qn * eg
    gla = jnp.sum(jnp.where(rowi == 63, gcol, 0.0), 0, keepdims=True)
    glb = jnp.sum(jnp.where(rowi == 127, gcol, 0.0), 0, keepdims=True)
    ed = jnp.exp(jnp.where(rowi < 64, gla, glb) - gcol)
    kd = kn * ed
    return dict(qh=qh, kh=kh, vh=vh, rq=rq, rk=rk, qn=qn, kn=kn, gcol=gcol, bcol=bcol, D=D, A=A, Bm=Bm,
                tm=tm, eg=eg, ed=ed, u=u, w=w, qk=qk, qd=qd, kd=kd, kb=kb, vb=vb,
                gla=jnp.exp(gla), glb=jnp.exp(glb))


def _gdn_scan(q, sa):
    sab = _b(sa)
    wb = _b(q["w"])
    vna = q["u"] - _nn(wb, sab)
    yield
    sb = sa * q["gla"] + _tn(_b(q["kd"][0:64]), _b(vna[0:64]))
    yield
    sbb = _b(sb)
    vnb = q["u"] - _nn(wb, sbb)
    yield
    sn = sb * q["glb"] + _tn(_b(q["kd"][64:128]), _b(vnb[64:128]))
    yield
    vn = jnp.concatenate([vna[0:64], vnb[64:128]], 0)
    qdb = _b(q["qd"])
    o = jnp.concatenate([_nn(qdb[0:64], sab), _nn(qdb[64:128], sbb)], 0) + _nn(_b(q["qk"]), _b(vn))
    return sb, sn, vn, o


def _gdn_param_specs():
    return [pl.BlockSpec((CONV_WIDTH, GW3), lambda i: (0, 0)),
            pl.BlockSpec((8, 128), lambda i: (0, 0)),
            pl.BlockSpec((1, 128), lambda i: (0, 0))]


def _gdn_pvec(a_log, dt_bias):
    z = jnp.zeros((8, 128), F32)
    return z.at[0, 4:8].set(a_log).at[1, 4:8].set(dt_bias)


def _gdn_fwd(proj, cw, a_log, dt_bias, nw, ymix, *, T, name):
    S_ = proj.shape[0]
    nu = T // 128

    def body(x_ref, z_ref, g_ref, cw_ref, pv_ref, nw_ref, _, y_ref, st_ref, tm_ref, halo, state):
        i = pl.program_id(0)

        @pl.when(i == 0)
        def _():
            halo[...] = jnp.zeros_like(halo)
            state[...] = jnp.zeros_like(state)

        x = x_ref[...]
        ext = jnp.concatenate([halo[...], x], 0)
        halo[...] = x[T - 8:]
        _, _, qkv, beta, _, _, _, gc = _gdn_pre(ext, T, cw_ref, g_ref[...], pv_ref)
        items = [(dc, h) for dc in range(nu) for h in range(G_HEADS)]
        qs = _lockstep([_gdn_chunk(qkv, beta, gc, slice(dc * 128, (dc + 1) * 128), h) for dc, h in items])

        def head_chain(h):
            s = state[h]
            for dc in range(nu):
                rs = slice(dc * 128, (dc + 1) * 128)
                q = qs[dc * G_HEADS + h]
                sb, sn, _, o = yield from _gdn_scan(q, s)
                st_ref[2 * dc, h] = s
                st_ref[2 * dc + 1, h] = sb
                tm_ref[dc, h] = q["tm"]
                s = sn
                yield
                rn = lax.rsqrt(jnp.mean(o * o, 1, keepdims=True) + RMS_EPS)
                cs = slice(h * 128, (h + 1) * 128)
                y_ref[rs, cs] = (o * rn * nw_ref[...] * _silu(z_ref[rs, cs])).astype(BF16)
                yield
            state[h] = s

        _lockstep([head_chain(h) for h in range(G_HEADS)])

    return _pcall(
        body, name=name, grid=(S_ // T,),
        in_specs=[pl.BlockSpec((T, GW3), lambda i: (i, OFF_GQKV // GW3)),
                  pl.BlockSpec((T, 512), lambda i: (i, OFF_GZ // 512)),
                  pl.BlockSpec((T, 128), lambda i: (i, OFF_GBA // 128))] + _gdn_param_specs() + [ANY_SPEC],
        out_specs=[pl.BlockSpec((T, 512), lambda i: (i, MIX_G // 512)),
                   pl.BlockSpec((2 * nu, G_HEADS, 128, 128), lambda i: (i, 0, 0, 0)),
                   pl.BlockSpec((nu, G_HEADS, 128, 128), lambda i: (i, 0, 0, 0))],
        out_shape=[jax.ShapeDtypeStruct(ymix.shape, BF16),
                   jax.ShapeDtypeStruct((S_ // 64, G_HEADS, 128, 128), F32),
                   jax.ShapeDtypeStruct((S_ // 128, G_HEADS, 128, 128), F32)],
        input_output_aliases={6: 0},
        scratch_shapes=[pltpu.VMEM((8, GW3), F32), pltpu.VMEM((G_HEADS, 128, 128), F32)],
        compiler_params=_cp(("arbitrary",)),
    )(proj, proj, proj, cw, _gdn_pvec(a_log, dt_bias), nw.reshape(1, 128), ymix)


def _gdn_bwd(proj, states, tms, dymix, dproj, cw, a_log, dt_bias, nw, *, T, name):
    S_ = proj.shape[0]
    nt = S_ // T
    nu = T // 128
    t8 = T // 8

    def body(x_ref, xp_ref, z_ref, g_ref, st_ref, tm_ref, dy_ref, cw_ref, pv_ref, nw_ref, _,
             dp_ref, dg_ref, dcw_ref, dpv_ref, dnw_ref, dstate, dhalo, dqkv, dbg):
        i = pl.program_id(0)
        first_tile = (i == nt - 1)

        @pl.when(i == 0)
        def _():
            dstate[...] = jnp.zeros_like(dstate)
            dhalo[...] = jnp.zeros_like(dhalo)
            dcw_ref[...] = jnp.zeros_like(dcw_ref)
            dpv_ref[...] = jnp.zeros_like(dpv_ref)
            dnw_ref[...] = jnp.zeros_like(dnw_ref)

        keep = jnp.where(first_tile, 0.0, 1.0)
        ext = jnp.concatenate([xp_ref[...] * keep, x_ref[...]], 0)
        G = g_ref[...]
        taps, c, qkv, beta, sarg, nea, gdec, gc = _gdn_pre(ext, T, cw_ref, G, pv_ref)
        tril, strict, _ = _gdn_masks()
        rowi = lax.broadcasted_iota(jnp.int32, (128, 1), 0)
        lane = lax.broadcasted_iota(jnp.int32, (1, 128), 1)
        ones_b = jnp.ones((128, 128), BF16)
        nwv = nw_ref[...]
        items = [(dc, h) for dc in range(nu) for h in range(G_HEADS)]

        def recompute(dc, h):
            q = yield from _gdn_chunk(qkv, beta, gc, slice(dc * 128, (dc + 1) * 128), h, tm=tm_ref[dc, h])
            sa = st_ref[2 * dc, h]
            sb, _, vn, o = yield from _gdn_scan(q, sa)
            return q, sa, sb, vn, o

        fw = _lockstep([recompute(dc, h) for dc, h in items])
        chain_out = {}

        def head_chain(h):
            dS = dstate[h]
            for dc in reversed(range(nu)):
                rs = slice(dc * 128, (dc + 1) * 128)
                q, sa, sb, vn, o = fw[dc * G_HEADS + h]
                cs = slice(h * 128, (h + 1) * 128)
                zg = z_ref[rs, cs]
                dy = dy_ref[rs, cs]
                rn = lax.rsqrt(jnp.mean(o * o, 1, keepdims=True) + RMS_EPS)
                don = dy * _silu(zg)
                dp_ref[rs, GW3 + cs.start:GW3 + cs.stop] = (dy * (o * rn * nwv) * _dsilu(zg)).astype(BF16)
                dnw_ref[...] += jnp.sum(don * o * rn, 0, keepdims=True)
                tt = don * nwv
                do = rn * (tt - o * (rn * rn) * jnp.mean(tt * o, 1, keepdims=True))
                yield
                dob = _b(do)
                sab, sbb = _b(sa), _b(sb)
                vnb16 = _b(vn)
                dqk = jnp.where(tril, _nt(dob, vnb16), 0.0)
                dvn_o = _tn(_b(q["qk"]), dob)
                dSb16 = _b(dS)
                kdb = _b(q["kd"])
                wb = _b(q["w"])
                qdb = _b(q["qd"])
                yield
                dvn_b = dvn_o[64:128] + _nn(kdb[64:128], dSb16)
                dkd_b = _nt(vnb16[64:128], dSb16)
                dgl_b = jnp.sum(jnp.sum(dS * sb, 1, keepdims=True), 0, keepdims=True)
                yield
                dvn_b16 = _b(dvn_b)
                dw_b = -_nt(dvn_b16, sbb)
                dqd_b = _nt(dob[64:128], sbb)
                dSm = q["glb"] * dS + _tn(qdb[64:128], dob[64:128]) - _tn(wb[64:128], dvn_b16)
                yield
                dSm16 = _b(dSm)
                dvn_a = dvn_o[0:64] + _nn(kdb[0:64], dSm16)
                dkd_a = _nt(vnb16[0:64], dSm16)
                dgl_a = jnp.sum(jnp.sum(dSm * sa, 1, keepdims=True), 0, keepdims=True)
                yield
                dvn_a16 = _b(dvn_a)
                dw_a = -_nt(dvn_a16, sab)
                dqd_a = _nt(dob[0:64], sab)
                dS = q["gla"] * dSm + _tn(qdb[0:64], dob[0:64]) - _tn(wb[0:64], dvn_a16)
                chain_out[dc, h] = (dqk, jnp.concatenate([dvn_a, dvn_b], 0), jnp.concatenate([dw_a, dw_b], 0),
                                    jnp.concatenate([dkd_a, dkd_b], 0), jnp.concatenate([dqd_a, dqd_b], 0),
                                    dgl_a, dgl_b)
                yield
            dstate[h] = dS

        _lockstep([head_chain(h) for h in range(G_HEADS)])

        def local(dc, h):
            rs = slice(dc * 128, (dc + 1) * 128)
            q = fw[dc * G_HEADS + h][0]
            dqk, du, dw, dkd, dqd, dgl_a, dgl_b = chain_out[dc, h]
            if True:
                dvb = _dot3(_tn, q["tm"], du)
                dkbe = _dot3(_tn, q["tm"], dw)
                yield
                dM = jnp.where(strict, -(_nt(_b(dvb), _b(q["u"])) + _nt(_b(dkbe), _b(q["w"]))), 0.0)
                yield
                D = q["D"]
                dA = dM * D
                dB = dqk * D
                dDD = (dM * q["A"] + dqk * q["Bm"]) * D
                dh_, dm_, dl_ = _split3(dDD)
                colsum = _tn(dh_, ones_b) + (_tn(dm_, ones_b) + _tn(dl_, ones_b))
                dgc = jnp.sum(dDD, 1, keepdims=True) - _lane_col(colsum, 0)
                yield
                dA16, dB16 = _b(dA), _b(dB)
                knb, kbb, qnb = _b(q["kn"]), _b(q["kb"]), _b(q["qn"])
                eg, ed = q["eg"], q["ed"]
                dkb = _nn(dA16, knb) + dkbe * eg
                dkn = _tn(dA16, kbb) + _tn(dB16, qnb) + dkd * ed + dkb * q["bcol"]
                dqn = _nn(dB16, knb) + dqd * eg
                yield
                deg = jnp.sum(dkbe * q["kb"], 1, keepdims=True) + jnp.sum(dqd * q["qn"], 1, keepdims=True)
                ded = jnp.sum(dkd * q["kn"], 1, keepdims=True) * ed
                dgc = dgc + deg * eg - ded
                tail_a = jnp.sum(jnp.where(rowi < 64, ded, 0.0), 0, keepdims=True) + dgl_a * q["gla"]
                tail_b = jnp.sum(jnp.where(rowi >= 64, ded, 0.0), 0, keepdims=True) + dgl_b * q["glb"]
                dgc = dgc + jnp.where(rowi == 63, tail_a, 0.0) + jnp.where(rowi == 127, tail_b, 0.0)
                dbeta = jnp.sum(dkb * q["kn"], 1, keepdims=True) + jnp.sum(dvb * q["vh"], 1, keepdims=True)
                bcol = q["bcol"]
                blk = jnp.where(lane == h, dbeta * bcol * (1.0 - bcol), 0.0) + jnp.where(lane == 4 + h, dgc, 0.0)
                yield
                sc = G_HEAD_DIM ** -0.5
                rq, rk, qh, kh = q["rq"], q["rk"], q["qh"], q["kh"]
                dqh = sc * (dqn * rq - qh * (rq * rq * rq) * jnp.sum(dqn * qh, 1, keepdims=True))
                dkh = dkn * rk - kh * (rk * rk * rk) * jnp.sum(dkn * kh, 1, keepdims=True)
                dqkv[rs, h * 128:(h + 1) * 128] = dqh
                dqkv[rs, 512 + h * 128:512 + (h + 1) * 128] = dkh
                dqkv[rs, 1024 + h * 128:1024 + (h + 1) * 128] = dvb * bcol
            return blk

        blks = _lockstep([local(dc, h) for dc, h in items])
        for dc in range(nu):
            dbg[dc * 128:(dc + 1) * 128, :] = functools.reduce(
                lambda a, b: a + b, [blks[dc * G_HEADS + h] for h in range(G_HEADS)])
        ri = lax.broadcasted_iota(jnp.int32, (T, T), 0)
        cj = lax.broadcasted_iota(jnp.int32, (T, T), 1)
        utri = jnp.where((ri <= cj) & ((ri >> 6) == (cj >> 6)), 1.0, 0.0).astype(BF16)
        dbgv = dbg[...]
        dgd = _dot_exact_lhs(_nn, utri, dbgv)
        is_g = (lane >= 4) & (lane < 8)
        dga = jnp.where(is_g, dgd * nea * _sigmoid(sarg), 0.0)
        dg_ref[...] = jnp.where(lane < 4, dbgv, dga).astype(BF16)
        dpv_ref[0:1, :] += jnp.sum(jnp.where(is_g, dgd * gdec, 0.0), 0, keepdims=True)
        dpv_ref[1:2, :] += jnp.sum(dga, 0, keepdims=True)
        dc_ = dqkv[...] * _dsilu(c)
        for k in range(CONV_WIDTH):
            dcw_ref[k:k + 1, :] += jnp.sum(dc_ * taps[k], 0, keepdims=True)
        ext2 = jnp.concatenate([dc_, dhalo[...]], 0)
        tt2 = _conv_taps_t(ext2, T)
        dp_ref[:, 0:GW3] = sum(cw_ref[k:k + 1, :] * tt2[k] for k in range(CONV_WIDTH)).astype(BF16)
        dhalo[...] = dc_[0:8]

    def rev(i):
        return nt - 1 - i

    def prev8(i):
        return jnp.maximum(rev(i) * t8 - 1, 0)

    return _pcall(
        body, name=name, grid=(nt,),
        in_specs=[pl.BlockSpec((T, GW3), lambda i: (rev(i), OFF_GQKV // GW3)),
                  pl.BlockSpec((8, GW3), lambda i: (prev8(i), OFF_GQKV // GW3)),
                  pl.BlockSpec((T, 512), lambda i: (rev(i), OFF_GZ // 512)),
                  pl.BlockSpec((T, 128), lambda i: (rev(i), OFF_GBA // 128)),
                  pl.BlockSpec((2 * nu, G_HEADS, 128, 128), lambda i: (rev(i), 0, 0, 0)),
                  pl.BlockSpec((nu, G_HEADS, 128, 128), lambda i: (rev(i), 0, 0, 0)),
                  pl.BlockSpec((T, 512), lambda i: (rev(i), MIX_G // 512))] + _gdn_param_specs() + [ANY_SPEC],
        out_specs=[pl.BlockSpec((T, GW3 + 512), lambda i: (rev(i), OFF_GQKV // (GW3 + 512))),
                   pl.BlockSpec((T, 128), lambda i: (rev(i), 0)),
                   pl.BlockSpec((CONV_WIDTH, GW3), lambda i: (0, 0)),
                   pl.BlockSpec((8, 128), lambda i: (0, 0)),
                   pl.BlockSpec((1, 128), lambda i: (0, 0))],
        out_shape=[jax.ShapeDtypeStruct(dproj.shape, BF16),
                   jax.ShapeDtypeStruct((S_, 128), BF16), jax.ShapeDtypeStruct((CONV_WIDTH, GW3), F32),
                   jax.ShapeDtypeStruct((8, 128), F32), jax.ShapeDtypeStruct((1, 128), F32)],
        input_output_aliases={10: 0},
        scratch_shapes=[pltpu.VMEM((G_HEADS, 128, 128), F32), pltpu.VMEM((8, GW3), F32),
                        pltpu.VMEM((T, GW3), F32), pltpu.VMEM((T, 128), F32)],
        compiler_params=_cp(("arbitrary",)),
    )(proj, proj, proj, proj, states, tms, dymix, cw, _gdn_pvec(a_log, dt_bias), nw.reshape(1, 128), dproj)


def _pair_sum_windows(a, b, nsh, width, *, out_dtype, name):
    R_, C = a.shape
    hr = R_ // 2
    nb = width // 128
    assert (3 * nsh) // 128 + nb <= C // 128
    to_perm = _orig_block_to_perm()
    blocks = jnp.asarray([to_perm[(nsh * t) // 128 + j] for t in range(4) for j in range(nb)], jnp.int32)
    table = jnp.concatenate([blocks, lax.axis_index("c").astype(jnp.int32)[None]])

    def body(tab_ref, a_ref, b_ref, o_ref):
        o_ref[...] = (a_ref[...] + b_ref[...]).astype(o_ref.dtype)

    return _pcall(
        body, name=name,
        grid_spec=pltpu.PrefetchScalarGridSpec(
            num_scalar_prefetch=1, grid=(4, nb),
            in_specs=[pl.BlockSpec((hr, 128), lambda t, j, tab: (tab[4 * nb], tab[t * nb + j])),
                      pl.BlockSpec((hr, 128), lambda t, j, tab: (0, tab[t * nb + j]))],
            out_specs=pl.BlockSpec((None, hr, 128), lambda t, j, tab: (t, 0, j))),
        out_shape=jax.ShapeDtypeStruct((4, hr, width), out_dtype),
        compiler_params=_cp(("parallel", "parallel")))(table, a, b)


def _pair_sum_blocks(a, b, *, out_dtype, name):
    L, R_, C = a.shape
    hr = R_ // 2

    def body(a0_ref, a1_ref, b_ref, o_ref):
        mine = jnp.where(lax.axis_index("c") == 0, a0_ref[...], a1_ref[...])
        o_ref[...] = (mine + b_ref[...]).astype(o_ref.dtype)

    def spec(half):
        return pl.BlockSpec((None, hr, C), lambda t: (t, half, 0))

    return _pcall(body, name=name, grid=(L,), in_specs=[spec(0), spec(1), spec(0)], out_specs=spec(0),
                  out_shape=jax.ShapeDtypeStruct((L, hr, C), out_dtype),
                  compiler_params=_cp(("parallel",)))(a, a, b)


def _add_mine(a0, a1, b, *, out_dtype, tr, name):
    R_, C = b.shape

    def body(a0_ref, a1_ref, b_ref, o_ref):
        mine = jnp.where(lax.axis_index("c") == 0, a0_ref[...], a1_ref[...])
        o_ref[...] = (mine + b_ref[...]).astype(o_ref.dtype)

    spec = pl.BlockSpec((tr, C), lambda i: (i, 0))
    return _pcall(body, name=name, grid=(R_ // tr,), in_specs=[spec] * 3, out_specs=spec,
                  out_shape=jax.ShapeDtypeStruct((R_, C), out_dtype), compiler_params=_cp(("parallel",)))(a0, a1, b)


def _sum4(a, mine, *, tr, name):
    _, R_, C = a.shape

    def body(a_ref, m_ref, o_ref):
        s = 2 * lax.axis_index("x") + lax.axis_index("y")
        mv = m_ref[...].astype(F32)
        p = [jnp.where(s == t, mv, a_ref[t].astype(F32)) for t in range(4)]
        o_ref[...] = ((p[0] + p[1]) + p[2]) + p[3]

    return _pcall(body, name=name, grid=(R_ // tr,),
                  in_specs=[pl.BlockSpec((4, tr, C), lambda i: (0, i, 0)), pl.BlockSpec((tr, C), lambda i: (i, 0))],
                  out_specs=pl.BlockSpec((tr, C), lambda i: (i, 0)),
                  out_shape=jax.ShapeDtypeStruct((R_, C), F32), compiler_params=_cp(("parallel",)))(a, mine)


def _adamw_refs(w_ref, g_ref, m_ref, v_ref, d_ref, mo_ref, vo_ref):
    c1 = 1.0 / (1.0 - ADAM_B1 ** ADAM_STEP)
    c2 = 1.0 / (1.0 - ADAM_B2 ** ADAM_STEP)
    gg = g_ref[...]
    mn = ADAM_B1 * m_ref[...] + (1.0 - ADAM_B1) * gg
    vn = ADAM_B2 * v_ref[...] + (1.0 - ADAM_B2) * (gg * gg)
    mo_ref[...] = mn
    vo_ref[...] = vn
    d_ref[...] = -ADAM_LR * ((mn * c1) / (jnp.sqrt(vn * c2) + ADAM_EPS) + ADAM_WD * w_ref[...])


def _adamw_many(ws, gs, ms, vs, *, name):
    n = len(ws)

    def body(*refs):
        for k in range(n):
            _adamw_refs(*[refs[q * n + k] for q in range(7)])

    vm = pl.BlockSpec(memory_space=pltpu.VMEM)
    shp = [jax.ShapeDtypeStruct(w.shape, F32) for w in ws]
    outs = _pcall(body, name=name, in_specs=[vm] * (4 * n), out_specs=[vm] * (3 * n), out_shape=shp * 3,
                  compiler_params=pltpu.CompilerParams(vmem_limit_bytes=VMEM_LIMIT))(*ws, *gs, *ms, *vs)
    return outs[:n], outs[n:2 * n], outs[2 * n:]


def _adamw(w, g, m, v, *, tr, name):
    L, R_, C = w.shape
    body = functools.partial(_adamw_refs)

    spec = pl.BlockSpec((None, tr, C), lambda l, i: (l, i, 0))
    shp = jax.ShapeDtypeStruct((L, R_, C), F32)
    return _pcall(body, name=name, grid=(L, R_ // tr), in_specs=[spec] * 4, out_specs=[spec] * 3,
                  out_shape=[shp] * 3, compiler_params=_cp(("parallel", "parallel")))(w, g, m, v)


def _adamw_cols(w, g, m, v, *, name):
    C, L, R_ = w.shape
    tc = C // 2 if C % 2 == 0 else C

    spec = pl.BlockSpec((tc, L, 128), lambda i, j: (i, 0, j))
    shp = jax.ShapeDtypeStruct((C, L, R_), F32)
    return _pcall(functools.partial(_adamw_refs), name=name, grid=(C // tc, R_ // 128), in_specs=[spec] * 4,
                  out_specs=[spec] * 3, out_shape=[shp] * 3,
                  compiler_params=_cp(("parallel", "parallel")))(w, g, m, v)


HBM_SPEC = pl.BlockSpec(memory_space=pltpu.HBM)


def _place():
    x, y, c = lax.axis_index("x"), lax.axis_index("y"), lax.axis_index("c")
    chips = [(1 - x, y), (x, 1 - y), (1 - x, 1 - y)]
    return x, y, c, 2 * x + y, chips, [2 * cx + cy for cx, cy in chips], (x, y, 1 - c)


def _remote(src, dst, ssem, rsem, dev):
    return pltpu.make_async_remote_copy(src_ref=src, dst_ref=dst, send_sem=ssem, recv_sem=rsem,
                                        device_id=dev, device_id_type=MESH)


def _row_half(ref, lead, hc):
    hl = ref.shape[-2] // 2
    return ref.at[lead, pl.ds(hc * hl, hl), :]


def _gather_side(items):
    n = len(items)

    def copies(ins, outs, ssem, rsem):
        x, y, c, s, chips, sid, sib = _place()
        cps = [_remote(_row_half(ins[k], items[k][1], c), _row_half(outs[k], s, c),
                       ssem.at[3 * k + j], rsem.at[3 * k + j], (*chip, c))
               for k in range(n) for j, chip in enumerate(chips)]
        return cps, c, sid, sib

    def start(ins, outs, ssem, rsem):
        for cp in copies(ins, outs, ssem, rsem)[0]:
            cp.start()

    def finish(ins, outs, ssem, rsem):
        cps, c, sid, sib = copies(ins, outs, ssem, rsem)
        for k in range(n):
            for j in range(3):
                got = _row_half(outs[k], sid[j], c)
                _remote(got, got, ssem.at[3 * k + j], rsem.at[3 * k + j], sib).wait_recv()
        for cp in cps:
            cp.wait_send()

    shapes = [jax.ShapeDtypeStruct((4,) + w.shape[1:], w.dtype) for w, _ in items]
    return _Side([w for w, _ in items], shapes, 3 * n, start, finish)


def _gather_join(gathered, name):
    n = len(gathered)

    def body(*refs):
        outs, ssem, rsem = refs[n:2 * n], refs[2 * n], refs[2 * n + 1]
        x, y, c, s, chips, sid, sib = _place()
        cps = []
        for k in range(n):
            for j in range(3):
                mine = _row_half(outs[k], sid[j], c)
                cps.append(_remote(mine, mine, ssem.at[3 * k + j], rsem.at[3 * k + j], sib))
        for cp in cps:
            cp.start()
        for k in range(n):
            for j in range(3):
                other = _row_half(outs[k], sid[j], 1 - c)
                _remote(other, other, ssem.at[3 * k + j], rsem.at[3 * k + j], sib).wait_recv()
        for cp in cps:
            cp.wait_send()

    return _pcall(
        body, name=name, in_specs=[HBM_SPEC] * n, out_specs=[HBM_SPEC] * n,
        out_shape=[jax.ShapeDtypeStruct(g.shape, g.dtype) for g in gathered],
        input_output_aliases={k: k for k in range(n)},
        scratch_shapes=[pltpu.SemaphoreType.DMA((3 * n,)), pltpu.SemaphoreType.DMA((3 * n,))],
    )(*gathered)


def _gather_layer0(win, conv):
    def body(win_ref, cv_ref, gin_ref, gcv_ref, ssem, rsem):
        x, y, c, s, chips, sid, sib = _place()

        def in_half(slot, hc):
            return _row_half(gin_ref, slot, hc)

        sends = []
        for j, chip in enumerate(chips):
            dev = (*chip, c)
            sends.append(_remote(_row_half(win_ref, 0, c), in_half(s, c), ssem.at[j], rsem.at[j], dev))
            sends.append(_remote(cv_ref, gcv_ref.at[s], ssem.at[3 + j], rsem.at[3 + j], dev))
        for cp in sends:
            cp.start()
        for j in range(3):
            _remote(in_half(sid[j], c), in_half(sid[j], c), ssem.at[j], rsem.at[j], sib).wait_recv()
            f = _remote(in_half(sid[j], c), in_half(sid[j], c), ssem.at[6 + j], rsem.at[6 + j], sib)
            f.start()
            sends.append(f)
        for j in range(3):
            _remote(in_half(sid[j], 1 - c), in_half(sid[j], 1 - c), ssem.at[6 + j], rsem.at[6 + j], sib).wait_recv()
            _remote(gcv_ref.at[sid[j]], gcv_ref.at[sid[j]], ssem.at[3 + j], rsem.at[3 + j], sib).wait_recv()
        for cp in sends:
            cp.wait_send()

    return _pcall(
        body, name="gather_layer0",
        in_specs=[HBM_SPEC] * 2, out_specs=[HBM_SPEC] * 2,
        out_shape=[jax.ShapeDtypeStruct((4,) + win.shape[1:], win.dtype),
                   jax.ShapeDtypeStruct((4,) + conv.shape, conv.dtype)],
        scratch_shapes=[pltpu.SemaphoreType.DMA((9,)), pltpu.SemaphoreType.DMA((9,))],
    )(win, conv)


def _swap_halves(arrs, axes, name):
    n = len(arrs)

    def half_shape(a, ax):
        return a.shape[:ax] + (a.shape[ax] // 2,) + a.shape[ax + 1:]

    def body(*refs):
        src, dst, ssem, rsem = refs[:n], refs[n:2 * n], refs[2 * n], refs[2 * n + 1]
        x, y, c, s, chips, sid, sib = _place()
        cps = []
        for k in range(n):
            hl = src[k].shape[axes[k]] // 2
            idx = [slice(None)] * len(src[k].shape)
            idx[axes[k]] = pl.ds((1 - c) * hl, hl)
            cps.append(_remote(src[k].at[tuple(idx)], dst[k], ssem.at[k], rsem.at[k], sib))
        for cp in cps:
            cp.start()
        for cp in cps:
            cp.wait()

    return _pcall(
        body, name=name, in_specs=[HBM_SPEC] * n, out_specs=[HBM_SPEC] * n,
        out_shape=[jax.ShapeDtypeStruct(half_shape(a, ax), a.dtype) for a, ax in zip(arrs, axes)],
        scratch_shapes=[pltpu.SemaphoreType.DMA((n,)), pltpu.SemaphoreType.DMA((n,))],
    )(*arrs)


def _swap_side(arrs, axes):
    n = len(arrs)

    def copies(ins, outs, ssem, rsem):
        x, y, c, s, chips, sid, sib = _place()
        cps = []
        for k in range(n):
            hl = ins[k].shape[axes[k]] // 2
            idx = [slice(None)] * len(ins[k].shape)
            idx[axes[k]] = pl.ds((1 - c) * hl, hl)
            cps.append(_remote(ins[k].at[tuple(idx)], outs[k], ssem.at[k], rsem.at[k], sib))
        return cps

    def start(ins, outs, ssem, rsem):
        for cp in copies(ins, outs, ssem, rsem):
            cp.start()

    def finish(ins, outs, ssem, rsem):
        for cp in copies(ins, outs, ssem, rsem):
            cp.wait()

    shapes = [jax.ShapeDtypeStruct(a.shape[:ax] + (a.shape[ax] // 2,) + a.shape[ax + 1:], a.dtype)
              for a, ax in zip(arrs, axes)]
    return _Side(list(arrs), shapes, n, start, finish)


def _chips_side(arrs, per_target):
    n = len(arrs)

    def copies(ins, outs, ssem, rsem):
        x, y, c, s, chips, sid, sib = _place()
        cps = [_remote(ins[k].at[sid[j]] if per_target[k] else ins[k], outs[k].at[s],
                       ssem.at[3 * k + j], rsem.at[3 * k + j], (*chip, c))
               for k in range(n) for j, chip in enumerate(chips)]
        return cps, sid, sib

    def start(ins, outs, ssem, rsem):
        for cp in copies(ins, outs, ssem, rsem)[0]:
            cp.start()

    def finish(ins, outs, ssem, rsem):
        cps, sid, sib = copies(ins, outs, ssem, rsem)
        for k in range(n):
            for j in range(3):
                got = outs[k].at[sid[j]]
                _remote(got, got, ssem.at[3 * k + j], rsem.at[3 * k + j], sib).wait_recv()
        for cp in cps:
            cp.wait_send()

    shapes = [jax.ShapeDtypeStruct(a.shape if pt else (4,) + a.shape, a.dtype) for a, pt in zip(arrs, per_target)]
    return _Side(list(arrs), shapes, 3 * n, start, finish)


def _scatter_chips(arrs, per_target, name):
    n = len(arrs)

    def body(*refs):
        src, dst = refs[:n], refs[n:2 * n]
        ssem, rsem = refs[2 * n], refs[2 * n + 1]
        x, y, c, s, chips, sid, sib = _place()
        sends = []
        for k in range(n):
            for j, chip in enumerate(chips):
                piece = src[k].at[sid[j]] if per_target[k] else src[k]
                sends.append(_remote(piece, dst[k].at[s], ssem.at[3 * k + j], rsem.at[3 * k + j], (*chip, c)))
        for cp in sends:
            cp.start()
        for k in range(n):
            for j in range(3):
                _remote(dst[k].at[sid[j]], dst[k].at[sid[j]], ssem.at[3 * k + j], rsem.at[3 * k + j], sib).wait_recv()
        for cp in sends:
            cp.wait_send()

    outs = [jax.ShapeDtypeStruct(a.shape if pt else (4,) + a.shape, a.dtype) for a, pt in zip(arrs, per_target)]
    return _pcall(
        body, name=name, in_specs=[HBM_SPEC] * n, out_specs=[HBM_SPEC] * n, out_shape=outs,
        scratch_shapes=[pltpu.SemaphoreType.DMA((3 * n,)), pltpu.SemaphoreType.DMA((3 * n,))],
    )(*arrs)


def _swap_whole(arrs, name):
    n = len(arrs)

    def body(*refs):
        src, dst, ssem, rsem = refs[:n], refs[n:2 * n], refs[2 * n], refs[2 * n + 1]
        *_, sib = _place()
        cps = [_remote(src[k], dst[k], ssem.at[k], rsem.at[k], sib) for k in range(n)]
        for cp in cps:
            cp.start()
        for cp in cps:
            cp.wait()

    return _pcall(
        body, name=name, in_specs=[HBM_SPEC] * n, out_specs=[HBM_SPEC] * n,
        out_shape=[jax.ShapeDtypeStruct(a.shape, a.dtype) for a in arrs],
        scratch_shapes=[pltpu.SemaphoreType.DMA((n,)), pltpu.SemaphoreType.DMA((n,))],
    )(*arrs)


def _perm_cols(w):
    parts = [w[..., int(_ORIG_OFF[oi]):int(_ORIG_OFF[oi]) + IN_SIZES[oi]] for oi, _ in _PIECES]
    parts.append(jnp.zeros(w.shape[:-1] + (NP - N_IN,), w.dtype))
    return jnp.concatenate(parts, -1)


def _perm_rows(w):
    return jnp.concatenate([w[..., 512:1536, :], w[..., 0:512, :], w[..., 1536:2048, :]], -2)


_SMALL = ("sinks", "r_conv_b", "r_wa", "r_ba", "r_wx", "r_bx", "r_lam", "g_a_log", "g_dt_bias", "g_norm_w",
          "ln_g", "ln_b", "r_conv_w", "g_conv_w")
_PACK_ROWS = 16


def _piece_rows(n):
    return -(-n // (128 * _PACK_ROWS)) * _PACK_ROWS


def _pack(arrs):
    parts = []
    for a in arrs:
        n = int(np.prod(a.shape))
        rows = _piece_rows(n)
        if n % 128 == 0:
            blk = a.reshape(n // 128, 128)
        else:
            blk = jnp.pad(a.reshape(1, n), ((0, 0), (0, (-n) % 128))).reshape(-1, 128)
        if blk.shape[0] < rows:
            blk = jnp.pad(blk, ((0, rows - blk.shape[0]), (0, 0)))
        parts.append(blk)
    return jnp.concatenate(parts, 0)


def _unpack(packed, shapes):
    out = []
    r = 0
    for shp in shapes:
        n = int(np.prod(shp))
        if n % 128 == 0:
            out.append(packed[r:r + n // 128].reshape(shp))
        else:
            nr = -(-n // 128)
            out.append(packed[r:r + nr].reshape(1, nr * 128)[:, :n].reshape(shp))
        r += _piece_rows(n)
    return out


def _tile(n, t):
    return min(n, t)


def _layer_fwd(l, x, xb, wb, wob, ln, rope_c, rope_s, p, side=None, target=None):
    S_ = x.shape[0]
    proj = _matmul(xb, wb, ta=False, tb=False, tm=_tile(S_, 1024), tn=NP // 4, tk=wb.shape[0], out_dtype=F32,
                   name=f"in_proj_{l}", side=side)
    side_out = None
    if side:
        proj, side_out = proj
    h, ymix = _rglru_fwd(proj, p["r_conv_w"], p["r_conv_b"], p["r_wa"], p["r_ba"], p["r_wx"], p["r_bx"], p["r_lam"],
                         T=_tile(S_, 256), name=f"rglru_fwd_{l}")
    ymix = _attn_fwd(proj, rope_c, rope_s, p["sinks"], ymix, T=_tile(S_, 512), name=f"attn_fwd_{l}")
    ymix, st, tms = _gdn_fwd(proj, p["g_conv_w"], p["g_a_log"], p["g_dt_bias"], p["g_norm_w"], ymix,
                             T=_tile(S_, 256), name=f"gdn_fwd_{l}")
    out = _outproj(ymix, wob(side_out), x, ln[0], ln[1], tm=_tile(S_, 256), name=f"out_proj_{l}", target=target)
    sv = dict(proj=proj, h=h, st=st, tms=tms, ymix=ymix)
    if target is None:
        sv["z"], sv["y"], sv["yb"] = out
    else:
        sv["head"] = out
    return sv


def _layer_bwd(l, sv, x_b, dz, dzb, wb, wob, rope_c, rope_s, p, side_dmix=None, side_dw_in=None, side_dx=None):
    S_, D = dz.shape
    proj = sv["proj"]
    dwo = _matmul(sv["ymix"], dzb, ta=True, tb=False, tm=512, tn=_tile(D, 2048), tk=_tile(S_, 2048),
                  out_dtype=F32, name=f"dw_out_{l}",
                  out_blocks=((MIX_WIDTH, D), (512, _tile(D, 2048)),
                              lambda i, j: (jnp.where(i == 3, 3, (i + 1) % 3), j)))
    side = side_dmix(dwo) if side_dmix else None
    dymix = _matmul(dzb, wob, ta=False, tb=True, tm=_tile(S_, 1024), tn=512, tk=D, out_dtype=F32,
                    name=f"dmix_{l}", side=side)
    out_dmix = None
    if side:
        dymix, out_dmix = dymix
    dproj, dk, dv, dkt, dvt, dsk = _attn_bwd(proj, rope_c, rope_s, p["sinks"], dymix, T=_tile(S_, 512),
                                             name=f"attn_bwd_{l}")
    (dproj, dcw_r, dcb_r, dwa, dba, dwx, dbx, dlam) = _rglru_bwd(
        proj, sv["h"], dymix, dproj, p["r_conv_w"], p["r_conv_b"], p["r_wa"], p["r_ba"], p["r_wx"], p["r_bx"],
        p["r_lam"], T=_tile(S_, 256), name=f"rglru_bwd_{l}")
    dproj, dgba, dcw_g, dpv, dnw = _gdn_bwd(proj, sv["st"], sv["tms"], dymix, dproj, p["g_conv_w"], p["g_a_log"],
                                            p["g_dt_bias"], p["g_norm_w"], T=_tile(S_, 256), name=f"gdn_bwd_{l}")
    tail = jnp.concatenate([dk[128:], dkt, dv[128:], dvt], 0).reshape(2, S_, 128)
    tail = jnp.concatenate([tail[0], tail[1], dgba, jnp.zeros((S_, NP - OFF_GBA - 128), BF16)], 1)
    dproj = lax.dynamic_update_slice(dproj, tail, (0, OFF_AK))
    small = dict(sinks=dsk[:, 0], r_conv_b=dcb_r[0], r_wa=dwa, r_ba=dba[0], r_wx=dwx, r_bx=dbx[0], r_lam=dlam[0],
                 g_a_log=dpv[0, 4:8], g_dt_bias=dpv[1, 4:8], g_norm_w=dnw[0], r_conv_w=dcw_r, g_conv_w=dcw_g)
    side = side_dw_in(small, dwo, out_dmix) if side_dw_in else None
    dwin = _matmul(x_b, dproj, ta=True, tb=False, tm=_tile(D, 1024), tn=NP // 4, tk=_tile(S_, 2048),
                   out_dtype=F32, name=f"dw_in_{l}", side=side)
    out_dw_in = None
    if side:
        dwin, out_dw_in = dwin
    side = side_dx(dwin) if side_dx else None
    tmx = _tile(S_, 1024)
    nblk = S_ // tmx
    dx_args = dict(ta=False, tb=True, tm=tmx, tn=_tile(D, 1024), tk=NP // 2, out_dtype=F32, extra=dz,
                   alpha=DEEPNORM_ALPHA)
    out_dx = None
    if side and nblk >= 4:
        head = nblk - max(1, nblk // 8)
        dx, out_dx = _matmul(dproj, wb, name=f"dx_{l}", side=side, rows=(0, head), **dx_args)
        dx = _matmul(dproj, wb, name=f"dx_{l}_rest", rows=(head, nblk - head), into=dx, **dx_args)
    elif side:
        dx, out_dx = _matmul(dproj, wb, name=f"dx_{l}", side=side, **dx_args)
    else:
        dx = _matmul(dproj, wb, name=f"dx_{l}", **dx_args)
    return dx, dwin, dwo, small, out_dw_in, out_dx


def kernel(x, w_in, sinks, r_conv_w, r_conv_b, r_wa, r_ba, r_wx, r_bx, r_lam, g_conv_w, g_a_log, g_dt_bias, g_norm_w, w_out, ln_g, ln_b, loss_target, m_w_in, m_sinks, m_r_conv_w, m_r_conv_b, m_r_wa, m_r_ba, m_r_wx, m_r_bx, m_r_lam, m_g_conv_w, m_g_a_log, m_g_dt_bias, m_g_norm_w, m_w_out, m_ln_g, m_ln_b, v_w_in, v_sinks, v_r_conv_w, v_r_conv_b, v_r_wa, v_r_ba, v_r_wx, v_r_bx, v_r_lam, v_g_conv_w, v_g_a_log, v_g_dt_bias, v_g_norm_w, v_w_out, v_ln_g, v_ln_b):
    S_, D = x.shape[1], x.shape[2]
    nsh = w_in.shape[2]
    rsh = w_out.shape[1]
    cx, cy, cc = lax.axis_index("x"), lax.axis_index("y"), lax.axis_index("c")
    chip = 2 * cx + cy
    rcw_n, gcw_n = r_conv_w.shape[2], g_conv_w.shape[2]

    conv_pack = jnp.concatenate([r_conv_w, g_conv_w], 2)
    w_in_b, w_out_b = w_in.astype(BF16), w_out.astype(BF16)
    g_in0, g_conv = _gather_layer0(w_in_b, conv_pack)

    def shards(own, got):
        return [jnp.where(chip == t, own, got[t]) for t in range(4)]

    def w_in_of(l, g_in):
        return _perm_cols(jnp.concatenate(shards(w_in_b[l], g_in), 1))

    def w_out_of(l, g_out):
        return _perm_rows(jnp.concatenate(shards(w_out_b[l], g_out), 0))

    rcw = jnp.concatenate(shards(r_conv_w, g_conv[:, :, :, :rcw_n]), 2)
    gcw = jnp.concatenate(shards(g_conv_w, g_conv[:, :, :, rcw_n:]), 2)

    pos = jnp.arange(S_, dtype=F32)[:, None]
    inv = 1.0 / (ROPE_THETA ** (jnp.arange(0, A_HEAD_DIM, 2, dtype=F32) / A_HEAD_DIM))
    ang = pos * inv[None, :]
    cos, sin = jnp.cos(ang), jnp.sin(ang)
    rope_c = jnp.concatenate([cos, cos, cos, cos], 1)
    rope_s = jnp.concatenate([-sin, sin, -sin, sin], 1)

    def params(l):
        return dict(sinks=sinks[l], r_conv_w=rcw[l], r_conv_b=r_conv_b[l], r_wa=r_wa[l], r_ba=r_ba[l],
                    r_wx=r_wx[l], r_bx=r_bx[l], r_lam=r_lam[l], g_conv_w=gcw[l], g_a_log=g_a_log[l],
                    g_dt_bias=g_dt_bias[l], g_norm_w=g_norm_w[l])

    assert DEPTH == 2
    xb0 = x[0].astype(BF16)
    wb, wob = [w_in_of(0, g_in0), None], [None, None]
    late = {}

    def w_out_0(arrived):
        late["w_in_1"], g_out0 = _gather_join(arrived, "gather_join_0")
        wob[0] = w_out_of(0, g_out0)
        return wob[0]

    def w_out_1(arrived):
        wob[1] = w_out_of(1, _gather_join(arrived, "gather_join_1")[0])
        return wob[1]

    sv0 = _layer_fwd(0, x[0], xb0, wb[0], w_out_0, (ln_g[0], ln_b[0]), rope_c, rope_s, params(0),
                     side=_gather_side([(w_in_b, 1), (w_out_b, 0)]))
    wb[1] = w_in_of(1, late["w_in_1"])
    sv1 = _layer_fwd(1, sv0["y"], sv0["yb"], wb[1], w_out_1, (ln_g[1], ln_b[1]), rope_c, rope_s, params(1),
                     side=_gather_side([(w_out_b, 1)]), target=loss_target[0])
    saved, xbs = [sv0, sv1], [xb0, sv0["yb"]]

    tm_ln = _tile(S_, 256)
    dz, dzb, dg_l, db_l, loss_part = saved[-1]["head"]
    assert DEPTH == 2
    wcov = (-(-nsh // 128) + 1) * 128
    names = list(_SMALL)

    def own(a):
        return lax.dynamic_index_in_dim(a, chip, 0, keepdims=False)

    def sum_in(l, cp, arrived):
        return _sum4(arrived, own(cp), tr=_tile(D // 2, 256), name=f"chip_sum_w_in_{l}")

    def sum_out(l, cp, arrived):
        return _sum4(arrived, own(cp), tr=rsh // 2, name=f"chip_sum_w_out_{l}")

    dlng, dlnb = [None, dg_l[0]], [None, db_l[0]]
    dx, dwin1, dwo1, small1, _, _ = _layer_bwd(1, saved[1], xbs[1], dz, dzb, wb[1], wob[1], rope_c, rope_s, params(1))
    dwo1_4 = dwo1.reshape(4, rsh, D)
    dz, dzb, dg_l, db_l, _ = _ln_bwd(saved[0]["z"], ln_g[0], ln_b[0], dx, tm=tm_ln, name="ln_bwd_0")
    dlng[0], dlnb[0] = dg_l[0], db_l[0]

    held = {}

    def side_dmix(dwo0):
        return _swap_side([dwin1, dwo1_4, dwo0.reshape(4, rsh, D)], [0, 1, 1])

    def side_dw_in(small0, dwo0, got):
        sm = {k: jnp.stack([small0[k], small1[k]]) for k in small0}
        sm["ln_g"], sm["ln_b"] = jnp.stack(dlng), jnp.stack(dlnb)
        gs = _pack([sm[n] for n in names])
        (got_s,) = _swap_halves([gs], [0], "reduce_pair_small")
        held["in_cp1"] = _pair_sum_windows(dwin1, got[0], nsh, wcov, out_dtype=BF16, name="pair_sum_w_in_1")
        held["out_cp1"] = _pair_sum_blocks(dwo1_4, got[1], out_dtype=BF16, name="pair_sum_w_out_1")
        held["out_cp0"] = _pair_sum_blocks(dwo0.reshape(4, rsh, D), got[2], out_dtype=BF16, name="pair_sum_w_out_0")
        held["s_cp"] = _pair_sum_blocks(gs[None], got_s[None], out_dtype=F32, name="pair_sum_small")[0]
        held["shapes"] = [sm[n].shape for n in names]
        return _chips_side([held["in_cp1"], held["out_cp1"], held["out_cp0"]], [True, True, True])

    def side_dx(dwin0):
        got = _swap_halves([dwin0], [0], "reduce_pair_0b")
        held["in_cp0"] = _pair_sum_windows(dwin0, got[0], nsh, wcov, out_dtype=BF16, name="pair_sum_w_in_0")
        return _chips_side([held["in_cp0"], held["s_cp"]], [True, False])

    dx, _, _, _, arrived_a, arrived_b = _layer_bwd(0, saved[0], xbs[0], dz, dzb, wb[0], wob[0], rope_c, rope_s,
                                                   params(0), side_dmix=side_dmix, side_dw_in=side_dw_in,
                                                   side_dx=side_dx)
    grad_x = dx[None]
    loss = lax.psum(loss_part[0, 0], ("x", "y", "c"))
    s_cp = held["s_cp"]
    mine = [sum_in(0, held["in_cp0"], arrived_b[0]), sum_out(0, held["out_cp0"], arrived_a[2]),
            sum_in(1, held["in_cp1"], arrived_a[0]), sum_out(1, held["out_cp1"], arrived_a[1]),
            _sum4(arrived_b[1], s_cp, tr=s_cp.shape[0], name="chip_sum_small")]
    other = _swap_whole(mine, "reduce_join")

    def both(k, axis):
        return jnp.where(cc == 0, jnp.concatenate([mine[k], other[k]], axis),
                         jnp.concatenate([other[k], mine[k]], axis))

    g_w_in = lax.dynamic_slice_in_dim(jnp.stack([both(2 * l, 0) for l in range(DEPTH)]), (nsh * chip) % 128, nsh, 2)
    g_w_out = jnp.stack([both(2 * l + 1, 0) for l in range(DEPTH)])
    g_small = both(2 * DEPTH, 0)

    gsm = dict(zip(names, _unpack(g_small, held["shapes"])))
    gsm["r_conv_w"] = lax.dynamic_slice_in_dim(gsm["r_conv_w"], chip * rcw_n, rcw_n, 2)
    gsm["g_conv_w"] = lax.dynamic_slice_in_dim(gsm["g_conv_w"], chip * gcw_n, gcw_n, 2)
    wts = dict(sinks=sinks, r_conv_w=r_conv_w, r_conv_b=r_conv_b, r_wa=r_wa, r_ba=r_ba, r_wx=r_wx, r_bx=r_bx,
               r_lam=r_lam, g_conv_w=g_conv_w, g_a_log=g_a_log, g_dt_bias=g_dt_bias, g_norm_w=g_norm_w,
               ln_g=ln_g, ln_b=ln_b)
    mom = dict(sinks=m_sinks, r_conv_w=m_r_conv_w, r_conv_b=m_r_conv_b, r_wa=m_r_wa, r_ba=m_r_ba, r_wx=m_r_wx,
               r_bx=m_r_bx, r_lam=m_r_lam, g_conv_w=m_g_conv_w, g_a_log=m_g_a_log, g_dt_bias=m_g_dt_bias,
               g_norm_w=m_g_norm_w, ln_g=m_ln_g, ln_b=m_ln_b)
    vel = dict(sinks=v_sinks, r_conv_w=v_r_conv_w, r_conv_b=v_r_conv_b, r_wa=v_r_wa, r_ba=v_r_ba, r_wx=v_r_wx,
               r_bx=v_r_bx, r_lam=v_r_lam, g_conv_w=v_g_conv_w, g_a_log=v_g_a_log, g_dt_bias=v_g_dt_bias,
               g_norm_w=v_g_norm_w, ln_g=v_ln_g, ln_b=v_ln_b)
    d_s, m_s, v_s = _adamw_many(*[[d[n] for n in names] for d in (wts, gsm, mom, vel)], name="adamw_small")
    d_sm, m_sm, v_sm = (dict(zip(names, a)) for a in (d_s, m_s, v_s))
    def cols(a):
        return jnp.transpose(a, (2, 0, 1))

    g_w_in_t = cols(g_w_in)
    outs_t = _adamw_cols(cols(w_in), g_w_in_t, cols(m_w_in), cols(v_w_in), name="adamw_w_in")
    d_in, m_in, v_in = (jnp.transpose(a, (1, 2, 0)) for a in outs_t)
    g_w_in = jnp.transpose(g_w_in_t, (1, 2, 0))
    d_out, m_out, v_out = _adamw(w_out, g_w_out, m_w_out, v_w_out, tr=256, name="adamw_w_out")

    order = ["w_in", "sinks", "r_conv_w", "r_conv_b", "r_wa", "r_ba", "r_wx", "r_bx", "r_lam", "g_conv_w",
             "g_a_log", "g_dt_bias", "g_norm_w", "w_out", "ln_g", "ln_b"]
    grads = dict(gsm, w_in=g_w_in, w_out=g_w_out)
    deltas = dict(d_sm, w_in=d_in, w_out=d_out)
    new_m = dict(m_sm, w_in=m_in, w_out=m_out)
    new_v = dict(v_sm, w_in=v_in, w_out=v_out)
    return (loss, grad_x, *[grads[n] for n in order], *[deltas[n] for n in order],
            *[new_m[n] for n in order], *[new_v[n] for n in order])
```

```python
import functools
import math

import jax
import jax.numpy as jnp
import numpy as np
from jax import lax
from jax.experimental import pallas as pl
from jax.experimental.pallas import tpu as pltpu

F32 = jnp.float32
BF16 = jnp.bfloat16
MESH = pl.DeviceIdType.MESH

DEPTH = 2
A_HEADS, A_KV_HEADS, A_HEAD_DIM = 8, 2, 64
A_WIDTH, A_KV_WIDTH = 512, 128
WINDOW = 128
ROPE_THETA = 10000.0
R_WIDTH, R_BLOCKS, R_BLOCK_DIM, R_C = 1024, 8, 128, 8.0
CONV_WIDTH = 4
G_HEADS, G_HEAD_DIM, G_WIDTH, G_CHUNK = 4, 128, 512, 64
MIX_WIDTH = 2048
IN_SIZES = (512, 128, 128, 512, 1024, 1024, 512, 512, 512, 512, 4, 4)
N_IN = 5384
DEEPNORM_ALPHA = (2 * DEPTH) ** 0.25
LN_EPS = 1e-5
RMS_EPS = 1e-6
ADAM_LR, ADAM_B1, ADAM_B2, ADAM_EPS, ADAM_WD, ADAM_STEP = 0.001, 0.9, 0.999, 1e-08, 0.01, 10

NP = 5632
OFF_GQKV, OFF_GZ, OFF_RX, OFF_RZ, OFF_AQ, OFF_AZ, OFF_AK, OFF_AV, OFF_GBA = (
    0, 1536, 2048, 3072, 4096, 4608, 5120, 5248, 5376)
_ORIG_OFF = np.concatenate([[0], np.cumsum(IN_SIZES)])[:-1]
_PIECES = ((6, OFF_GQKV), (7, OFF_GQKV + 512), (8, OFF_GQKV + 1024), (9, OFF_GZ), (4, OFF_RX), (5, OFF_RZ),
           (0, OFF_AQ), (3, OFF_AZ), (1, OFF_AK), (2, OFF_AV), (10, OFF_GBA), (11, OFF_GBA + 4))


def _orig_block_to_perm():
    table = list(range(NP // 128))
    for oi, off in _PIECES:
        if IN_SIZES[oi] % 128 == 0:
            for k in range(IN_SIZES[oi] // 128):
                table[int(_ORIG_OFF[oi]) // 128 + k] = off // 128 + k
    return table
MIX_R, MIX_A, MIX_G = 0, 1024, 1536
VMEM_LIMIT = 56 * 1024 * 1024
ANY_SPEC = pl.BlockSpec(memory_space=pl.ANY)


def _pcall(body, **kw):
    return pl.pallas_call(body, **kw)


def _cp(sem, limit=VMEM_LIMIT):
    return pltpu.CompilerParams(dimension_semantics=sem, vmem_limit_bytes=limit)


def _sigmoid(x):
    return 0.5 + 0.5 * jnp.tanh(0.5 * x)


def _silu(x):
    return x * _sigmoid(x)


def _dsilu(x):
    s = _sigmoid(x)
    return s * (1.0 + x * (1.0 - s))


def _log1p(x):
    u = 1.0 + x
    d = jnp.where(u == 1.0, 1.0, u - 1.0)
    return jnp.where(u == 1.0, x, jnp.log(u) * (x / d))


def _softplus(x):
    return jnp.maximum(x, 0.0) + _log1p(jnp.exp(-jnp.abs(x)))


def _one_minus_exp(x):
    series = -x * (1.0 + x * (0.5 + x * (1.0 / 6.0 + x * (1.0 / 24.0))))
    return jnp.where(x > -0.05, series, 1.0 - jnp.exp(x))


def _nn(a, b):
    return lax.dot_general(a, b, (((1,), (0,)), ((), ())), preferred_element_type=F32)


def _nt(a, b):
    return lax.dot_general(a, b, (((1,), (1,)), ((), ())), preferred_element_type=F32)


def _tn(a, b):
    return lax.dot_general(a, b, (((0,), (0,)), ((), ())), preferred_element_type=F32)


def _b(x):
    return x.astype(BF16)


def _split3(x):
    hi = x.astype(BF16)
    r1 = x - hi.astype(F32)
    mid = r1.astype(BF16)
    lo = (r1 - mid.astype(F32)).astype(BF16)
    return hi, mid, lo


def _dot3(f, a, b):
    ah, am, _ = _split3(a)
    bh, bm, _ = _split3(b)
    return f(ah, bh) + (f(ah, bm) + f(am, bh))


def _dot_exact_lhs(f, a_bf16, b):
    bh, bm, bl = _split3(b)
    return f(a_bf16, bh) + (f(a_bf16, bm) + f(a_bf16, bl))


def _rot(x):
    w = x.shape[-1]
    lane = lax.broadcasted_iota(jnp.int32, (1, w), 1)
    return jnp.where((lane & 63) < 32, pltpu.roll(x, w - 32, 1), pltpu.roll(x, 32, 1))


def _conv_taps(ext, n):
    return [pltpu.roll(ext, 3 - k, 0)[8:8 + n] if k < 3 else ext[8:8 + n] for k in range(CONV_WIDTH)]


def _conv_taps_t(ext, n):
    m = ext.shape[0]
    return [pltpu.roll(ext, m - (3 - k), 0)[0:n] if k < 3 else ext[0:n] for k in range(CONV_WIDTH)]


def _scan_steps(a, b, pos, span, shifts, reverse):
    n = a.shape[0]
    for s in shifts:
        if reverse:
            a_sh = pltpu.roll(a, n - s, 0)
            b_sh = pltpu.roll(b, n - s, 0)
            ok = pos < (span - s)
        else:
            a_sh = pltpu.roll(a, s, 0)
            b_sh = pltpu.roll(b, s, 0)
            ok = pos >= s
        b = jnp.where(ok, a * b_sh + b, b)
        a = jnp.where(ok, a * a_sh, a)
    return a, b


def _scan_lin(a, b, reverse):
    n = a.shape[0]
    shifts = []
    s = 1
    while s < n:
        shifts.append(s)
        s *= 2
    return _scan_steps(a, b, lax.broadcasted_iota(jnp.int32, (n, 1), 0), n, shifts, reverse)


class _Side:
    def __init__(self, inputs, out_shapes, n_sems, start, finish):
        self.inputs, self.out_shapes, self.n_sems, self.start, self.finish = inputs, out_shapes, n_sems, start, finish


def _matmul(a, b, *, ta, tb, tm, tn, tk, out_dtype, name, extra=None, alpha=0.0, out_blocks=None, side=None,
            rows=None, into=None):
    if ta:
        K, M = a.shape
    else:
        M, K = a.shape
    if tb:
        N, K2 = b.shape
    else:
        K2, N = b.shape
    assert K == K2 and M % tm == 0 and N % tn == 0 and K % tk == 0, (a.shape, b.shape, tm, tn, tk)
    nk = K // tk
    ca = 0 if ta else 1
    cb = 1 if tb else 0
    has_extra = extra is not None

    assert nk == 1 or out_dtype == F32
    assert rows is None or (not ta and out_blocks is None)
    r0, nrow = rows if rows else (0, M // tm)
    n_in = 2 + int(has_extra) + int(into is not None)
    ns_in = len(side.inputs) if side else 0
    ns_out = len(side.out_shapes) if side else 0
    grid = (nrow, N // tn, nk)

    def body(*refs):
        a_ref, b_ref = refs[0], refs[1]
        e_ref = refs[2] if has_extra else None
        o_ref = refs[n_in + ns_in]
        k = pl.program_id(2)
        if side:
            s_in = refs[n_in:n_in + ns_in]
            s_out = refs[n_in + ns_in + 1:n_in + ns_in + 1 + ns_out]
            ssem, rsem = refs[-2], refs[-1]
            i, j = pl.program_id(0), pl.program_id(1)

            @pl.when((i == 0) & (j == 0) & (k == 0))
            def _():
                side.start(s_in, s_out, ssem, rsem)

            @pl.when((i == grid[0] - 1) & (j == grid[1] - 1) & (k == grid[2] - 1))
            def _():
                side.finish(s_in, s_out, ssem, rsem)

        part = lax.dot_general(a_ref[...], b_ref[...], (((ca,), (cb,)), ((), ())), preferred_element_type=F32)
        if nk == 1:
            if e_ref is not None:
                part = part + alpha * e_ref[...]
            o_ref[...] = part.astype(o_ref.dtype)
            return

        @pl.when(k == 0)
        def _():
            o_ref[...] = part

        @pl.when((k > 0) & (k < nk - 1))
        def _():
            o_ref[...] += part

        @pl.when(k == nk - 1)
        def _():
            last = o_ref[...] + part
            if e_ref is not None:
                last = last + alpha * e_ref[...]
            o_ref[...] = last

    a_spec = (pl.BlockSpec((tk, tm), lambda i, j, k: (k, i)) if ta
              else pl.BlockSpec((tm, tk), lambda i, j, k: (i + r0, k)))
    b_spec = (pl.BlockSpec((tn, tk), lambda i, j, k: (j, k)) if tb
              else pl.BlockSpec((tk, tn), lambda i, j, k: (k, j)))
    e_spec = pl.BlockSpec((tm, tn), lambda i, j, k: (i + r0, j))
    if out_blocks is None:
        o_spec, o_shape = e_spec, (M, N)
    else:
        o_shape, o_block, o_map = out_blocks
        o_spec = pl.BlockSpec(o_block, lambda i, j, k: o_map(i, j))
    in_specs = [a_spec, b_spec] + ([e_spec] if has_extra else []) + ([ANY_SPEC] if into is not None else [])
    args = (a, b) + ((extra,) if has_extra else ()) + ((into,) if into is not None else ())
    alias = {n_in - 1: 0} if into is not None else {}
    if not side:
        return _pcall(
            body, name=name, grid=grid, in_specs=in_specs, out_specs=o_spec,
            out_shape=jax.ShapeDtypeStruct(o_shape, out_dtype), input_output_aliases=alias,
            compiler_params=_cp(("parallel", "parallel", "arbitrary")),
        )(*args)
    outs = _pcall(
        body, name=name, grid=grid, in_specs=in_specs + [HBM_SPEC] * ns_in,
        out_specs=[o_spec] + [HBM_SPEC] * ns_out,
        out_shape=[jax.ShapeDtypeStruct(o_shape, out_dtype)] + list(side.out_shapes), input_output_aliases=alias,
        scratch_shapes=[pltpu.SemaphoreType.DMA((side.n_sems,)), pltpu.SemaphoreType.DMA((side.n_sems,))],
        compiler_params=_cp(("arbitrary", "arbitrary", "arbitrary")),
    )(*args, *side.inputs)
    return outs[0], outs[1:]


def _ln_stats(z):
    mu = jnp.mean(z, -1, keepdims=True)
    zc = z - mu
    var = jnp.mean(zc * zc, -1, keepdims=True)
    rstd = lax.rsqrt(var + LN_EPS)
    return zc * rstd, rstd


def _ln_bwd_tile(z, gam, bet, other, from_target, dz_ref, dzb_ref, dg_ref, db_ref, loss_ref):
    i = pl.program_id(0)

    @pl.when(i == 0)
    def _():
        dg_ref[...] = jnp.zeros_like(dg_ref)
        db_ref[...] = jnp.zeros_like(db_ref)
        loss_ref[...] = jnp.zeros_like(loss_ref)

    xh, rstd = _ln_stats(z)
    if from_target:
        err = xh * gam + bet - other
        per_tok = jnp.mean(err * err, -1, keepdims=True)
        loss_ref[...] += 0.5 * jnp.sum(per_tok, 0, keepdims=True)
        dy = err * (1.0 / z.shape[-1])
    else:
        dy = other
    dxh = dy * gam
    m1 = jnp.mean(dxh, -1, keepdims=True)
    m2 = jnp.mean(dxh * xh, -1, keepdims=True)
    dz = rstd * (dxh - m1 - xh * m2)
    dz_ref[...] = dz
    dzb_ref[...] = dz.astype(BF16)
    dg_ref[...] += jnp.sum(dy * xh, 0, keepdims=True)
    db_ref[...] += jnp.sum(dy, 0, keepdims=True)


def _outproj(ymix, wo, x, g, b, *, tm, name, target=None):
    S_, D = x.shape
    last = target is not None

    def body(*refs):
        y_ref, w_ref, x_ref, g_ref, b_ref = refs[:5]
        z = DEEPNORM_ALPHA * x_ref[...] + _nn(y_ref[...], w_ref[...])
        if last:
            _ln_bwd_tile(z, g_ref[...], b_ref[...], refs[5][...], True, *refs[6:])
            return
        z_ref, o_ref, ob_ref = refs[5:]
        z_ref[...] = z
        xh, _ = _ln_stats(z)
        y = xh * g_ref[...] + b_ref[...]
        o_ref[...] = y
        ob_ref[...] = y.astype(BF16)

    row = pl.BlockSpec((tm, D), lambda i: (i, 0))
    vec = pl.BlockSpec((1, D), lambda i: (0, 0))
    one = pl.BlockSpec((1, 1), lambda i: (0, 0))
    in_specs = [pl.BlockSpec((tm, MIX_WIDTH), lambda i: (i, 0)), pl.BlockSpec((MIX_WIDTH, D), lambda i: (0, 0)),
                row, vec, vec]
    f32s, b16s = jax.ShapeDtypeStruct((S_, D), F32), jax.ShapeDtypeStruct((S_, D), BF16)
    v32s = jax.ShapeDtypeStruct((1, D), F32)
    args = (ymix, wo, x, g.reshape(1, D), b.reshape(1, D))
    if last:
        return _pcall(body, name=name, grid=(S_ // tm,), in_specs=in_specs + [row],
                      out_specs=[row, row, vec, vec, one],
                      out_shape=[f32s, b16s, v32s, v32s, jax.ShapeDtypeStruct((1, 1), F32)],
                      compiler_params=_cp(("arbitrary",)))(*args, target)
    return _pcall(body, name=name, grid=(S_ // tm,), in_specs=in_specs, out_specs=[row, row, row],
                  out_shape=[f32s, f32s, b16s], compiler_params=_cp(("parallel",)))(*args)


def _ln_bwd(z, g, b, dy, *, tm, name):
    S_, D = z.shape

    def body(z_ref, g_ref, b_ref, o_ref, *outs):
        _ln_bwd_tile(z_ref[...], g_ref[...], b_ref[...], o_ref[...], False, *outs)

    row = pl.BlockSpec((tm, D), lambda i: (i, 0))
    vec = pl.BlockSpec((1, D), lambda i: (0, 0))
    one = pl.BlockSpec((1, 1), lambda i: (0, 0))
    return _pcall(
        body, name=name, grid=(S_ // tm,), in_specs=[row, vec, vec, row],
        out_specs=[row, row, vec, vec, one],
        out_shape=[jax.ShapeDtypeStruct((S_, D), F32), jax.ShapeDtypeStruct((S_, D), BF16),
                   jax.ShapeDtypeStruct((1, D), F32), jax.ShapeDtypeStruct((1, D), F32),
                   jax.ShapeDtypeStruct((1, 1), F32)],
        compiler_params=_cp(("arbitrary",)),
    )(z, g.reshape(1, D), b.reshape(1, D), dy)


def _attn_masks(i, sk_ref):
    ri = lax.broadcasted_iota(jnp.int32, (512, 256), 0)
    cj = lax.broadcasted_iota(jnp.int32, (512, 256), 1)
    diff = (ri & 127) - cj + 128
    band = (diff >= 0) & (diff < WINDOW)
    bias = jnp.where(band, 0.0, -jnp.inf)
    bias0 = jnp.where(band & ((i > 0) | (cj >= 128)), 0.0, -jnp.inf)
    grp = lax.broadcasted_iota(jnp.int32, (512, 1), 0) >> 7
    skvs = []
    for h in range(A_KV_HEADS):
        skv = jnp.zeros((512, 1), F32)
        for g in range(4):
            skv = jnp.where(grp == g, sk_ref[h * 4 + g], skv)
        skvs.append(skv)
    return bias0, bias, skvs


def _attn_common(masks, b, h, qr, kd, vd):
    lane = lax.broadcasted_iota(jnp.int32, (1, 128), 1)
    lof = (lane < 64).astype(F32)
    hif = 1.0 - lof
    r0 = b * 128
    skv = masks[2][h]
    pairs = [qr[r0:r0 + 128, h * 256 + p * 128:h * 256 + (p + 1) * 128] for p in (0, 1)]
    qs = _b(jnp.concatenate([pairs[0] * lof, pairs[0] * hif, pairs[1] * lof, pairs[1] * hif], 0))
    k2 = kd[h][r0:r0 + 256]
    v2 = vd[h][r0:r0 + 256]
    s = _nt(qs, k2) * (A_HEAD_DIM ** -0.5) + (masks[0] if b == 0 else masks[1])
    m = jnp.maximum(jnp.max(s, 1, keepdims=True), skv)
    p = jnp.exp(s - m)
    esk = jnp.exp(skv - m)
    rz = 1.0 / (jnp.sum(p, 1, keepdims=True) + esk)
    prob = p * rz
    o4 = _nn(_b(prob), v2)
    return lof, hif, qs, k2, v2, prob, esk * rz, o4


def _attn_prep(T, q_ref, k_ref, v_ref, c_ref, s_ref, kprev, vprev):
    C = c_ref[...]
    Sg = s_ref[...]
    C4 = jnp.concatenate([C] * 4, 1)
    S4 = jnp.concatenate([Sg] * 4, 1)
    q = q_ref[...]
    qr = q * C4 + _rot(q) * S4
    k = k_ref[...]
    kr = k * C + _rot(k) * Sg
    v = v_ref[...]
    kext = jnp.concatenate([kprev[...], kr], 0)
    vext = jnp.concatenate([vprev[...], v], 0)
    kprev[...] = kr[T - 128:]
    vprev[...] = v[T - 128:]
    lo = lax.broadcasted_iota(jnp.int32, (1, 128), 1) < 64
    kroll = pltpu.roll(kext, 64, 1)
    vroll = pltpu.roll(vext, 64, 1)
    kd = [_b(jnp.where(lo, kext, kroll)), _b(jnp.where(lo, kroll, kext))]
    vd = [_b(jnp.where(lo, vext, vroll)), _b(jnp.where(lo, vroll, vext))]
    return C, Sg, C4, S4, qr, kd, vd


def _attn_specs(T):
    return [pl.BlockSpec(memory_space=pltpu.SMEM),
            pl.BlockSpec((T, 512), lambda i: (i, OFF_AQ // 512)),
            pl.BlockSpec((T, 512), lambda i: (i, OFF_AZ // 512)),
            pl.BlockSpec((T, 128), lambda i: (i, OFF_AK // 128)),
            pl.BlockSpec((T, 128), lambda i: (i, OFF_AV // 128)),
            pl.BlockSpec((T, 128), lambda i: (i, 0)),
            pl.BlockSpec((T, 128), lambda i: (i, 0))]


def _attn_fwd(proj, rope_c, rope_s, sinks, ymix, *, T, name):
    S_ = proj.shape[0]
    nb = T // 128

    def body(sk_ref, q_ref, z_ref, k_ref, v_ref, c_ref, s_ref, _, y_ref, kprev, vprev):
        i = pl.program_id(0)

        @pl.when(i == 0)
        def _():
            kprev[...] = jnp.zeros_like(kprev)
            vprev[...] = jnp.zeros_like(vprev)

        _, _, _, _, qr, kd, vd = _attn_prep(T, q_ref, k_ref, v_ref, c_ref, s_ref, kprev, vprev)
        masks = _attn_masks(i, sk_ref)
        for b in range(nb):
            r0 = b * 128
            for h in range(2):
                lof, hif, _, _, _, _, _, o4 = _attn_common(masks, b, h, qr, kd, vd)
                for p in range(2):
                    cs = slice(h * 256 + p * 128, h * 256 + (p + 1) * 128)
                    o = o4[2 * p * 128:(2 * p + 1) * 128] * lof + o4[(2 * p + 1) * 128:(2 * p + 2) * 128] * hif
                    y_ref[r0:r0 + 128, cs] = (o * _silu(z_ref[r0:r0 + 128, cs])).astype(BF16)

    return _pcall(
        body, name=name, grid=(S_ // T,), in_specs=_attn_specs(T) + [ANY_SPEC],
        out_specs=pl.BlockSpec((T, 512), lambda i: (i, MIX_A // 512)),
        out_shape=jax.ShapeDtypeStruct(ymix.shape, BF16),
        input_output_aliases={7: 0},
        scratch_shapes=[pltpu.VMEM((128, 128), F32), pltpu.VMEM((128, 128), F32)],
        compiler_params=_cp(("arbitrary",)),
    )(sinks, proj, proj, proj, proj, rope_c, rope_s, ymix)


def _attn_bwd(proj, rope_c, rope_s, sinks, dymix, *, T, name):
    S_ = proj.shape[0]
    nb = T // 128
    nt = S_ // T

    def body(sk_ref, q_ref, z_ref, k_ref, v_ref, c_ref, s_ref, dy_ref,
             dp_ref, dk_ref, dv_ref, dkt_ref, dvt_ref, dsk_ref,
             kprev, vprev, cprev, sprev, dkacc, dvacc, dqacc):
        i = pl.program_id(0)

        @pl.when(i == 0)
        def _():
            kprev[...] = jnp.zeros_like(kprev)
            vprev[...] = jnp.zeros_like(vprev)
            cprev[...] = jnp.zeros_like(cprev)
            sprev[...] = jnp.zeros_like(sprev)
            dkacc[...] = jnp.zeros_like(dkacc)
            dvacc[...] = jnp.zeros_like(dvacc)
            dsk_ref[...] = jnp.zeros_like(dsk_ref)

        @pl.when(i > 0)
        def _():
            dkacc[0:128, :] = dkacc[T:T + 128, :]
            dvacc[0:128, :] = dvacc[T:T + 128, :]
            dkacc[128:, :] = jnp.zeros((T, 128), F32)
            dvacc[128:, :] = jnp.zeros((T, 128), F32)

        C, Sg, C4, S4, qr, kd, vd = _attn_prep(T, q_ref, k_ref, v_ref, c_ref, s_ref, kprev, vprev)
        masks = _attn_masks(i, sk_ref)
        lane = lax.broadcasted_iota(jnp.int32, (1, 128), 1)
        for b in range(nb):
            r0 = b * 128
            for h in range(2):
                lof, hif, qs, k2, v2, prob, psink, o4 = _attn_common(masks, b, h, qr, kd, vd)
                dos = []
                for p in range(2):
                    cs = slice(h * 256 + p * 128, h * 256 + (p + 1) * 128)
                    o = o4[2 * p * 128:(2 * p + 1) * 128] * lof + o4[(2 * p + 1) * 128:(2 * p + 2) * 128] * hif
                    zc = z_ref[r0:r0 + 128, cs]
                    dyc = dy_ref[r0:r0 + 128, cs]
                    dp_ref[r0:r0 + 128, 512 + cs.start:512 + cs.stop] = (dyc * o * _dsilu(zc)).astype(BF16)
                    do = dyc * _silu(zc)
                    dos += [do * lof, do * hif]
                dos = jnp.concatenate(dos, 0)
                os_ = jnp.concatenate([o4[0:128] * lof, o4[128:256] * hif, o4[256:384] * lof, o4[384:512] * hif], 0)
                delta = jnp.sum(dos * os_, 1, keepdims=True)
                dosb = _b(dos)
                dp = _nt(dosb, v2)
                ds = prob * (dp - delta)
                dsv = -psink * delta
                for g in range(4):
                    sg = jnp.sum(dsv[g * 128:(g + 1) * 128], 0, keepdims=True)
                    hd = h * 4 + g
                    dsk_ref[hd:hd + 1, :] += jnp.broadcast_to(sg, (1, 128))
                dsb = _b(ds * (A_HEAD_DIM ** -0.5))
                dqs = _nn(dsb, k2)
                for p in range(2):
                    cs = slice(h * 256 + p * 128, h * 256 + (p + 1) * 128)
                    dqacc[r0:r0 + 128, cs] = (dqs[2 * p * 128:(2 * p + 1) * 128] * lof
                                              + dqs[(2 * p + 1) * 128:(2 * p + 2) * 128] * hif)
                dkdup = _tn(dsb, qs)
                dvdup = _tn(_b(prob), dosb)
                half = (lane < 64) if h == 0 else (lane >= 64)
                dkacc[r0:r0 + 256, :] += jnp.where(half, dkdup + pltpu.roll(dkdup, 64, 1), 0.0)
                dvacc[r0:r0 + 256, :] += jnp.where(half, dvdup + pltpu.roll(dvdup, 64, 1), 0.0)
        dqr = dqacc[...]
        dp_ref[:, 0:512] = (dqr * C4 + _rot(dqr * S4)).astype(BF16)
        cext = jnp.concatenate([cprev[...], C], 0)
        sext = jnp.concatenate([sprev[...], Sg], 0)
        dke = dkacc[...]
        dkp = dke * cext + _rot(dke * sext)
        dk_ref[...] = dkp[0:T].astype(BF16)
        dkt_ref[...] = dkp[T:T + 128].astype(BF16)
        dve = dvacc[...]
        dv_ref[...] = dve[0:T].astype(BF16)
        dvt_ref[...] = dve[T:T + 128].astype(BF16)
        cprev[...] = C[T - 128:]
        sprev[...] = Sg[T - 128:]

    nar = pl.BlockSpec((T, 128), lambda i: (i, 0))
    tail = pl.BlockSpec((128, 128), lambda i: (0, 0))
    return _pcall(
        body, name=name, grid=(nt,),
        in_specs=_attn_specs(T) + [pl.BlockSpec((T, 512), lambda i: (i, MIX_A // 512))],
        out_specs=[pl.BlockSpec((T, 1024), lambda i: (i, OFF_AQ // 1024)), nar, nar, tail, tail,
                   pl.BlockSpec((8, 128), lambda i: (0, 0))],
        out_shape=[jax.ShapeDtypeStruct((S_, NP), BF16),
                   jax.ShapeDtypeStruct((S_, 128), BF16), jax.ShapeDtypeStruct((S_, 128), BF16),
                   jax.ShapeDtypeStruct((128, 128), BF16), jax.ShapeDtypeStruct((128, 128), BF16),
                   jax.ShapeDtypeStruct((8, 128), F32)],
        scratch_shapes=[pltpu.VMEM((128, 128), F32)] * 4
        + [pltpu.VMEM((T + 128, 128), F32), pltpu.VMEM((T + 128, 128), F32), pltpu.VMEM((T, 512), F32)],
        compiler_params=_cp(("arbitrary",)),
    )(sinks, proj, proj, proj, proj, rope_c, rope_s, dymix)


def _rg_gates(xr, wa_ref, ba_ref, wx_ref, bx_ref, lam_ref):
    xb = _b(xr)
    pre_a = jnp.concatenate([_nn(xb[:, n * 128:(n + 1) * 128], wa_ref[n]) for n in range(R_BLOCKS)], 1) + ba_ref[...]
    pre_x = jnp.concatenate([_nn(xb[:, n * 128:(n + 1) * 128], wx_ref[n]) for n in range(R_BLOCKS)], 1) + bx_ref[...]
    r = _sigmoid(pre_a)
    ig = _sigmoid(pre_x)
    sp = _softplus(-lam_ref[...])
    log_a = -R_C * r * sp
    a = jnp.exp(log_a)
    mult = jnp.sqrt(_one_minus_exp(2.0 * log_a))
    return xb, r, ig, sp, a, mult


def _rg_param_specs():
    C = R_WIDTH
    vec = pl.BlockSpec((1, C), lambda i: (0, 0))
    blk = pl.BlockSpec((R_BLOCKS, 128, 128), lambda i: (0, 0, 0))
    return [pl.BlockSpec((CONV_WIDTH, C), lambda i: (0, 0)), vec, blk, vec, blk, vec, vec]


def _rglru_fwd(proj, cw, cb, wa, ba, wx, bx, lam, *, T, name):
    S_ = proj.shape[0]
    C = R_WIDTH

    def body(rx_ref, rz_ref, cw_ref, cb_ref, wa_ref, ba_ref, wx_ref, bx_ref, lam_ref,
             h_ref, y_ref, halo, hcar):
        i = pl.program_id(0)

        @pl.when(i == 0)
        def _():
            halo[...] = jnp.zeros_like(halo)
            hcar[...] = jnp.zeros_like(hcar)

        rx = rx_ref[...]
        ext = jnp.concatenate([halo[...], rx], 0)
        halo[...] = rx[T - 8:]
        taps = _conv_taps(ext, T)
        xr = cb_ref[...] + sum(cw_ref[k:k + 1, :] * taps[k] for k in range(CONV_WIDTH))
        _, _, ig, _, a, mult = _rg_gates(xr, wa_ref, ba_ref, wx_ref, bx_ref, lam_ref)
        u = mult * (ig * xr)
        acum, hloc = _scan_lin(a, u, False)
        h = hloc + acum * hcar[0:1, :]
        hcar[...] = jnp.broadcast_to(h[T - 1:T, :], (8, C))
        h_ref[...] = h
        y_ref[...] = (h * _silu(rz_ref[...])).astype(BF16)

    row = pl.BlockSpec((T, C), lambda i: (i, 0))
    return _pcall(
        body, name=name, grid=(S_ // T,),
        in_specs=[pl.BlockSpec((T, C), lambda i: (i, OFF_RX // C)),
                  pl.BlockSpec((T, C), lambda i: (i, OFF_RZ // C))] + _rg_param_specs(),
        out_specs=[row, pl.BlockSpec((T, C), lambda i: (i, MIX_R // C))],
        out_shape=[jax.ShapeDtypeStruct((S_, C), F32), jax.ShapeDtypeStruct((S_, MIX_WIDTH), BF16)],
        scratch_shapes=[pltpu.VMEM((8, C), F32), pltpu.VMEM((8, C), F32)],
        compiler_params=_cp(("arbitrary",)),
    )(proj, proj, cw, cb.reshape(1, C), _b(wa), ba.reshape(1, C), _b(wx), bx.reshape(1, C), lam.reshape(1, C))


def _rglru_bwd(proj, h, dymix, dproj, cw, cb, wa, ba, wx, bx, lam, *, T, name):
    S_ = proj.shape[0]
    C = R_WIDTH
    nt = S_ // T
    t8 = T // 8

    def body(rx_ref, rxp_ref, rz_ref, h_ref, hp_ref, dy_ref,
             cw_ref, cb_ref, wa_ref, ba_ref, wx_ref, bx_ref, lam_ref, wat_ref, wxt_ref,
             _, dp_ref, dcw_ref, dcb_ref, dwa_ref, dba_ref, dwx_ref, dbx_ref, dlam_ref,
             afirst, gfirst, dhalo):
        i = pl.program_id(0)
        first_tile = (i == nt - 1)

        @pl.when(i == 0)
        def _():
            afirst[...] = jnp.zeros_like(afirst)
            gfirst[...] = jnp.zeros_like(gfirst)
            dhalo[...] = jnp.zeros_like(dhalo)
            for r in (dcw_ref, dcb_ref, dwa_ref, dba_ref, dwx_ref, dbx_ref, dlam_ref):
                r[...] = jnp.zeros_like(r)

        keep = jnp.where(first_tile, 0.0, 1.0)
        rx = rx_ref[...]
        ext = jnp.concatenate([rxp_ref[...] * keep, rx], 0)
        taps = _conv_taps(ext, T)
        xr = cb_ref[...] + sum(cw_ref[k:k + 1, :] * taps[k] for k in range(CONV_WIDTH))
        xb, r, ig, sp, a, mult = _rg_gates(xr, wa_ref, ba_ref, wx_ref, bx_ref, lam_ref)
        hh = h_ref[...]
        rz = rz_ref[...]
        dy = dy_ref[...]
        dp_ref[:, C:2 * C] = (dy * hh * _dsilu(rz)).astype(BF16)
        dh = dy * _silu(rz)
        row = lax.broadcasted_iota(jnp.int32, (T, 1), 0)
        c = jnp.where(row == T - 1, afirst[0:1, :], pltpu.roll(a, T - 1, 0))
        ccum, gloc = _scan_lin(c, dh, True)
        g = gloc + ccum * gfirst[0:1, :]
        afirst[...] = jnp.broadcast_to(a[0:1, :], (8, C))
        gfirst[...] = jnp.broadcast_to(g[0:1, :], (8, C))
        hprev = jnp.where(row == 0, hp_ref[7:8, :] * keep, pltpu.roll(hh, 1, 0))
        da = g * hprev
        gx = ig * xr
        dgx = g * mult
        dmult = g * gx
        dlog_a = da * a - dmult * (a * a) * lax.rsqrt(mult * mult)
        dpre_a = dlog_a * (-R_C * sp) * r * (1.0 - r)
        dpre_x = dgx * xr * ig * (1.0 - ig)
        dlam_ref[...] += jnp.sum(dlog_a * (-R_C * r), 0, keepdims=True) * (-_sigmoid(-lam_ref[...]))
        dab = _b(dpre_a)
        dxb = _b(dpre_x)
        dxr = dgx * ig + jnp.concatenate(
            [_nn(dab[:, n * 128:(n + 1) * 128], wat_ref[n]) + _nn(dxb[:, n * 128:(n + 1) * 128], wxt_ref[n])
             for n in range(R_BLOCKS)], 1)
        for n in range(R_BLOCKS):
            cs = slice(n * 128, (n + 1) * 128)
            dwa_ref[n] += _tn(xb[:, cs], dab[:, cs])
            dwx_ref[n] += _tn(xb[:, cs], dxb[:, cs])
        dba_ref[...] += jnp.sum(dpre_a, 0, keepdims=True)
        dbx_ref[...] += jnp.sum(dpre_x, 0, keepdims=True)
        dcb_ref[...] += jnp.sum(dxr, 0, keepdims=True)
        for k in range(CONV_WIDTH):
            dcw_ref[k:k + 1, :] += jnp.sum(dxr * taps[k], 0, keepdims=True)
        ext2 = jnp.concatenate([dxr, dhalo[...]], 0)
        tt = _conv_taps_t(ext2, T)
        dp_ref[:, 0:C] = sum(cw_ref[k:k + 1, :] * tt[k] for k in range(CONV_WIDTH)).astype(BF16)
        dhalo[...] = dxr[0:8]

    def rev(i):
        return nt - 1 - i

    def prev8(i):
        return jnp.maximum(rev(i) * t8 - 1, 0)

    vec = pl.BlockSpec((1, C), lambda i: (0, 0))
    blk = pl.BlockSpec((R_BLOCKS, 128, 128), lambda i: (0, 0, 0))
    row = pl.BlockSpec((T, C), lambda i: (rev(i), 0))
    wat = _b(jnp.swapaxes(wa, 1, 2))
    wxt = _b(jnp.swapaxes(wx, 1, 2))
    return _pcall(
        body, name=name, grid=(nt,),
        in_specs=[pl.BlockSpec((T, C), lambda i: (rev(i), OFF_RX // C)),
                  pl.BlockSpec((8, C), lambda i: (prev8(i), OFF_RX // C)),
                  pl.BlockSpec((T, C), lambda i: (rev(i), OFF_RZ // C)),
                  row,
                  pl.BlockSpec((8, C), lambda i: (prev8(i), 0)),
                  pl.BlockSpec((T, C), lambda i: (rev(i), MIX_R // C)),
                  ] + _rg_param_specs() + [blk, blk, ANY_SPEC],
        out_specs=[pl.BlockSpec((T, 2 * C), lambda i: (rev(i), OFF_RX // (2 * C))),
                   pl.BlockSpec((CONV_WIDTH, C), lambda i: (0, 0)), vec, blk, vec, blk, vec, vec],
        out_shape=[jax.ShapeDtypeStruct(dproj.shape, BF16),
                   jax.ShapeDtypeStruct((CONV_WIDTH, C), F32), jax.ShapeDtypeStruct((1, C), F32),
                   jax.ShapeDtypeStruct((R_BLOCKS, 128, 128), F32), jax.ShapeDtypeStruct((1, C), F32),
                   jax.ShapeDtypeStruct((R_BLOCKS, 128, 128), F32), jax.ShapeDtypeStruct((1, C), F32),
                   jax.ShapeDtypeStruct((1, C), F32)],
        input_output_aliases={15: 0},
        scratch_shapes=[pltpu.VMEM((8, C), F32)] * 3,
        compiler_params=_cp(("arbitrary",)),
    )(proj, proj, proj, h, h, dymix, cw, cb.reshape(1, C), _b(wa), ba.reshape(1, C), _b(wx), bx.reshape(1, C),
      lam.reshape(1, C), wat, wxt, dproj)


GW3 = 3 * G_WIDTH


def _lane_col(x, lane_idx):
    lane = lax.broadcasted_iota(jnp.int32, (1, x.shape[1]), 1)
    return jnp.sum(jnp.where(lane == lane_idx, x, 0.0), 1, keepdims=True)


def _gdn_pre(ext, T, cw_ref, gba, pv_ref):
    taps = _conv_taps(ext, T)
    c = sum(cw_ref[k:k + 1, :] * taps[k] for k in range(CONV_WIDTH))
    qkv = _silu(c)
    beta = _sigmoid(gba)
    sarg = gba + pv_ref[1:2, :]
    nea = -jnp.exp(pv_ref[0:1, :])
    gdec = nea * _softplus(sarg)
    ri = lax.broadcasted_iota(jnp.int32, (T, T), 0)
    cj = lax.broadcasted_iota(jnp.int32, (T, T), 1)
    same = (ri >> 6) == (cj >> 6)
    ltri = jnp.where((ri >= cj) & same, 1.0, 0.0).astype(BF16)
    gc = _dot_exact_lhs(_nn, ltri, gdec)
    return taps, c, qkv, beta, sarg, nea, gdec, gc


def _gdn_masks():
    ri = lax.broadcasted_iota(jnp.int32, (128, 128), 0)
    cj = lax.broadcasted_iota(jnp.int32, (128, 128), 1)
    same = (ri >> 6) == (cj >> 6)
    return (ri >= cj) & same, (ri > cj) & same, ri == cj


def _lockstep(gens):
    out = [None] * len(gens)
    live = list(range(len(gens)))
    while live:
        still = []
        for k in live:
            try:
                next(gens[k])
                still.append(k)
            except StopIteration as stop:
                out[k] = stop.value
        live = still
    return out


def _gdn_chunk(qkv, beta, gc, rs, h, tm=None):
    tril, strict, eye = _gdn_masks()
    rowi = lax.broadcasted_iota(jnp.int32, (128, 1), 0)
    lane = lax.broadcasted_iota(jnp.int32, (1, 128), 1)
    qh = qkv[rs, h * 128:(h + 1) * 128]
    kh = qkv[rs, 512 + h * 128:512 + (h + 1) * 128]
    vh = qkv[rs, 1024 + h * 128:1024 + (h + 1) * 128]
    rq = lax.rsqrt(jnp.sum(qh * qh, 1, keepdims=True) + RMS_EPS)
    rk = lax.rsqrt(jnp.sum(kh * kh, 1, keepdims=True) + RMS_EPS)
    qn = qh * (rq * (G_HEAD_DIM ** -0.5))
    kn = kh * rk
    gcb = gc[rs]
    gcol = _lane_col(gcb, 4 + h)
    bcol = _lane_col(beta[rs], h)
    grow = _dot_exact_lhs(_nt, jnp.ones((128, 128), BF16), jnp.where(lane == 4 + h, gcb, 0.0))
    D = jnp.where(tril, jnp.exp(jnp.minimum(gcol - grow, 0.0)), 0.0)
    kb = kn * bcol
    vb = vh * bcol
    knb = _b(kn)
    A = _nt(_b(kb), knb)
    Bm = _nt(_b(qn), knb)
    yield
    if tm is None:
        N = jnp.where(strict, -(A * D), 0.0)
        tm = jnp.where(eye, 1.0, 0.0) + N
        npow = N
        for _ in range(5):
            npow = _dot3(_nn, npow, npow)
            yield
            tm = tm + _dot3(_nn, tm, npow)
            yield
    eg = jnp.exp(gcol)
    u = _dot3(_nn, tm, vb)
    w = _dot3(_nn, tm, kb * eg)
    yield
    qk = jnp.where(tril, Bm * D, 0.0)
    qd = qn * eg
    gla = jnp.sum(jnp.where(rowi == 63, gcol, 0.0), 0, keepdims=True)
    glb = jnp.sum(jnp.where(rowi == 127, gcol, 0.0), 0, keepdims=True)
    ed = jnp.exp(jnp.where(rowi < 64, gla, glb) - gcol)
    kd = kn * ed
    return dict(qh=qh, kh=kh, vh=vh, rq=rq, rk=rk, qn=qn, kn=kn, gcol=gcol, bcol=bcol, D=D, A=A, Bm=Bm,
                tm=tm, eg=eg, ed=ed, u=u, w=w, qk=qk, qd=qd, kd=kd, kb=kb, vb=vb,
                gla=jnp.exp(gla), glb=jnp.exp(glb))


def _gdn_scan(q, sa):
    sab = _b(sa)
    wb = _b(q["w"])
    vna = q["u"] - _nn(wb, sab)
    yield
    sb = sa * q["gla"] + _tn(_b(q["kd"][0:64]), _b(vna[0:64]))
    yield
    sbb = _b(sb)
    vnb = q["u"] - _nn(wb, sbb)
    yield
    sn = sb * q["glb"] + _tn(_b(q["kd"][64:128]), _b(vnb[64:128]))
    yield
    vn = jnp.concatenate([vna[0:64], vnb[64:128]], 0)
    qdb = _b(q["qd"])
    o = jnp.concatenate([_nn(qdb[0:64], sab), _nn(qdb[64:128], sbb)], 0) + _nn(_b(q["qk"]), _b(vn))
    return sb, sn, vn, o


def _gdn_param_specs():
    return [pl.BlockSpec((CONV_WIDTH, GW3), lambda i: (0, 0)),
            pl.BlockSpec((8, 128), lambda i: (0, 0)),
            pl.BlockSpec((1, 128), lambda i: (0, 0))]


def _gdn_pvec(a_log, dt_bias):
    z = jnp.zeros((8, 128), F32)
    return z.at[0, 4:8].set(a_log).at[1, 4:8].set(dt_bias)


def _gdn_fwd(proj, cw, a_log, dt_bias, nw, ymix, *, T, name):
    S_ = proj.shape[0]
    nu = T // 128

    def body(x_ref, z_ref, g_ref, cw_ref, pv_ref, nw_ref, _, y_ref, st_ref, tm_ref, halo, state):
        i = pl.program_id(0)

        @pl.when(i == 0)
        def _():
            halo[...] = jnp.zeros_like(halo)
            state[...] = jnp.zeros_like(state)

        x = x_ref[...]
        ext = jnp.concatenate([halo[...], x], 0)
        halo[...] = x[T - 8:]
        _, _, qkv, beta, _, _, _, gc = _gdn_pre(ext, T, cw_ref, g_ref[...], pv_ref)
        items = [(dc, h) for dc in range(nu) for h in range(G_HEADS)]
        qs = _lockstep([_gdn_chunk(qkv, beta, gc, slice(dc * 128, (dc + 1) * 128), h) for dc, h in items])

        def head_chain(h):
            s = state[h]
            for dc in range(nu):
                rs = slice(dc * 128, (dc + 1) * 128)
                q = qs[dc * G_HEADS + h]
                sb, sn, _, o = yield from _gdn_scan(q, s)
                st_ref[2 * dc, h] = s
                st_ref[2 * dc + 1, h] = sb
                tm_ref[dc, h] = q["tm"]
                s = sn
                yield
                rn = lax.rsqrt(jnp.mean(o * o, 1, keepdims=True) + RMS_EPS)
                cs = slice(h * 128, (h + 1) * 128)
                y_ref[rs, cs] = (o * rn * nw_ref[...] * _silu(z_ref[rs, cs])).astype(BF16)
                yield
            state[h] = s

        _lockstep([head_chain(h) for h in range(G_HEADS)])

    return _pcall(
        body, name=name, grid=(S_ // T,),
        in_specs=[pl.BlockSpec((T, GW3), lambda i: (i, OFF_GQKV // GW3)),
                  pl.BlockSpec((T, 512), lambda i: (i, OFF_GZ // 512)),
                  pl.BlockSpec((T, 128), lambda i: (i, OFF_GBA // 128))] + _gdn_param_specs() + [ANY_SPEC],
        out_specs=[pl.BlockSpec((T, 512), lambda i: (i, MIX_G // 512)),
                   pl.BlockSpec((2 * nu, G_HEADS, 128, 128), lambda i: (i, 0, 0, 0)),
                   pl.BlockSpec((nu, G_HEADS, 128, 128), lambda i: (i, 0, 0, 0))],
        out_shape=[jax.ShapeDtypeStruct(ymix.shape, BF16),
                   jax.ShapeDtypeStruct((S_ // 64, G_HEADS, 128, 128), F32),
                   jax.ShapeDtypeStruct((S_ // 128, G_HEADS, 128, 128), F32)],
        input_output_aliases={6: 0},
        scratch_shapes=[pltpu.VMEM((8, GW3), F32), pltpu.VMEM((G_HEADS, 128, 128), F32)],
        compiler_params=_cp(("arbitrary",)),
    )(proj, proj, proj, cw, _gdn_pvec(a_log, dt_bias), nw.reshape(1, 128), ymix)


def _gdn_bwd(proj, states, tms, dymix, dproj, cw, a_log, dt_bias, nw, *, T, name):
    S_ = proj.shape[0]
    nt = S_ // T
    nu = T // 128
    t8 = T // 8

    def body(x_ref, xp_ref, z_ref, g_ref, st_ref, tm_ref, dy_ref, cw_ref, pv_ref, nw_ref, _,
             dp_ref, dg_ref, dcw_ref, dpv_ref, dnw_ref, dstate, dhalo, dqkv, dbg):
        i = pl.program_id(0)
        first_tile = (i == nt - 1)

        @pl.when(i == 0)
        def _():
            dstate[...] = jnp.zeros_like(dstate)
            dhalo[...] = jnp.zeros_like(dhalo)
            dcw_ref[...] = jnp.zeros_like(dcw_ref)
            dpv_ref[...] = jnp.zeros_like(dpv_ref)
            dnw_ref[...] = jnp.zeros_like(dnw_ref)

        keep = jnp.where(first_tile, 0.0, 1.0)
        ext = jnp.concatenate([xp_ref[...] * keep, x_ref[...]], 0)
        G = g_ref[...]
        taps, c, qkv, beta, sarg, nea, gdec, gc = _gdn_pre(ext, T, cw_ref, G, pv_ref)
        tril, strict, _ = _gdn_masks()
        rowi = lax.broadcasted_iota(jnp.int32, (128, 1), 0)
        lane = lax.broadcasted_iota(jnp.int32, (1, 128), 1)
        ones_b = jnp.ones((128, 128), BF16)
        nwv = nw_ref[...]
        items = [(dc, h) for dc in range(nu) for h in range(G_HEADS)]

        def recompute(dc, h):
            q = yield from _gdn_chunk(qkv, beta, gc, slice(dc * 128, (dc + 1) * 128), h, tm=tm_ref[dc, h])
            sa = st_ref[2 * dc, h]
            sb, _, vn, o = yield from _gdn_scan(q, sa)
            return q, sa, sb, vn, o

        fw = _lockstep([recompute(dc, h) for dc, h in items])
        chain_out = {}

        def head_chain(h):
            dS = dstate[h]
            for dc in reversed(range(nu)):
                rs = slice(dc * 128, (dc + 1) * 128)
                q, sa, sb, vn, o = fw[dc * G_HEADS + h]
                cs = slice(h * 128, (h + 1) * 128)
                zg = z_ref[rs, cs]
                dy = dy_ref[rs, cs]
                rn = lax.rsqrt(jnp.mean(o * o, 1, keepdims=True) + RMS_EPS)
                don = dy * _silu(zg)
                dp_ref[rs, GW3 + cs.start:GW3 + cs.stop] = (dy * (o * rn * nwv) * _dsilu(zg)).astype(BF16)
                dnw_ref[...] += jnp.sum(don * o * rn, 0, keepdims=True)
                tt = don * nwv
                do = rn * (tt - o * (rn * rn) * jnp.mean(tt * o, 1, keepdims=True))
                yield
                dob = _b(do)
                sab, sbb = _b(sa), _b(sb)
                vnb16 = _b(vn)
                dqk = jnp.where(tril, _nt(dob, vnb16), 0.0)
                dvn_o = _tn(_b(q["qk"]), dob)
                dSb16 = _b(dS)
                kdb = _b(q["kd"])
                wb = _b(q["w"])
                qdb = _b(q["qd"])
                yield
                dvn_b = dvn_o[64:128] + _nn(kdb[64:128], dSb16)
                dkd_b = _nt(vnb16[64:128], dSb16)
                dgl_b = jnp.sum(jnp.sum(dS * sb, 1, keepdims=True), 0, keepdims=True)
                yield
                dvn_b16 = _b(dvn_b)
                dw_b = -_nt(dvn_b16, sbb)
                dqd_b = _nt(dob[64:128], sbb)
                dSm = q["glb"] * dS + _tn(qdb[64:128], dob[64:128]) - _tn(wb[64:128], dvn_b16)
                yield
                dSm16 = _b(dSm)
                dvn_a = dvn_o[0:64] + _nn(kdb[0:64], dSm16)
                dkd_a = _nt(vnb16[0:64], dSm16)
                dgl_a = jnp.sum(jnp.sum(dSm * sa, 1, keepdims=True), 0, keepdims=True)
                yield
                dvn_a16 = _b(dvn_a)
                dw_a = -_nt(dvn_a16, sab)
                dqd_a = _nt(dob[0:64], sab)
                dS = q["gla"] * dSm + _tn(qdb[0:64], dob[0:64]) - _tn(wb[0:64], dvn_a16)
                chain_out[dc, h] = (dqk, jnp.concatenate([dvn_a, dvn_b], 0), jnp.concatenate([dw_a, dw_b], 0),
                                    jnp.concatenate([dkd_a, dkd_b], 0), jnp.concatenate([dqd_a, dqd_b], 0),
                                    dgl_a, dgl_b)
                yield
            dstate[h] = dS

        _lockstep([head_chain(h) for h in range(G_HEADS)])

        def local(dc, h):
            rs = slice(dc * 128, (dc + 1) * 128)
            q = fw[dc * G_HEADS + h][0]
            dqk, du, dw, dkd, dqd, dgl_a, dgl_b = chain_out[dc, h]
            if True:
                dvb = _dot3(_tn, q["tm"], du)
                dkbe = _dot3(_tn, q["tm"], dw)
                yield
                dM = jnp.where(strict, -(_nt(_b(dvb), _b(q["u"])) + _nt(_b(dkbe), _b(q["w"]))), 0.0)
                yield
                D = q["D"]
                dA = dM * D
                dB = dqk * D
                dDD = (dM * q["A"] + dqk * q["Bm"]) * D
                dh_, dm_, dl_ = _split3(dDD)
                colsum = _tn(dh_, ones_b) + (_tn(dm_, ones_b) + _tn(dl_, ones_b))
                dgc = jnp.sum(dDD, 1, keepdims=True) - _lane_col(colsum, 0)
                yield
                dA16, dB16 = _b(dA), _b(dB)
                knb, kbb, qnb = _b(q["kn"]), _b(q["kb"]), _b(q["qn"])
                eg, ed = q["eg"], q["ed"]
                dkb = _nn(dA16, knb) + dkbe * eg
                dkn = _tn(dA16, kbb) + _tn(dB16, qnb) + dkd * ed + dkb * q["bcol"]
                dqn = _nn(dB16, knb) + dqd * eg
                yield
                deg = jnp.sum(dkbe * q["kb"], 1, keepdims=True) + jnp.sum(dqd * q["qn"], 1, keepdims=True)
                ded = jnp.sum(dkd * q["kn"], 1, keepdims=True) * ed
                dgc = dgc + deg * eg - ded
                tail_a = jnp.sum(jnp.where(rowi < 64, ded, 0.0), 0, keepdims=True) + dgl_a * q["gla"]
                tail_b = jnp.sum(jnp.where(rowi >= 64, ded, 0.0), 0, keepdims=True) + dgl_b * q["glb"]
                dgc = dgc + jnp.where(rowi == 63, tail_a, 0.0) + jnp.where(rowi == 127, tail_b, 0.0)
                dbeta = jnp.sum(dkb * q["kn"], 1, keepdims=True) + jnp.sum(dvb * q["vh"], 1, keepdims=True)
                bcol = q["bcol"]
                blk = jnp.where(lane == h, dbeta * bcol * (1.0 - bcol), 0.0) + jnp.where(lane == 4 + h, dgc, 0.0)
                yield
                sc = G_HEAD_DIM ** -0.5
                rq, rk, qh, kh = q["rq"], q["rk"], q["qh"], q["kh"]
                dqh = sc * (dqn * rq - qh * (rq * rq * rq) * jnp.sum(dqn * qh, 1, keepdims=True))
                dkh = dkn * rk - kh * (rk * rk * rk) * jnp.sum(dkn * kh, 1, keepdims=True)
                dqkv[rs, h * 128:(h + 1) * 128] = dqh
                dqkv[rs, 512 + h * 128:512 + (h + 1) * 128] = dkh
                dqkv[rs, 1024 + h * 128:1024 + (h + 1) * 128] = dvb * bcol
            return blk

        blks = _lockstep([local(dc, h) for dc, h in items])
        for dc in range(nu):
            dbg[dc * 128:(dc + 1) * 128, :] = functools.reduce(
                lambda a, b: a + b, [blks[dc * G_HEADS + h] for h in range(G_HEADS)])
        ri = lax.broadcasted_iota(jnp.int32, (T, T), 0)
        cj = lax.broadcasted_iota(jnp.int32, (T, T), 1)
        utri = jnp.where((ri <= cj) & ((ri >> 6) == (cj >> 6)), 1.0, 0.0).astype(BF16)
        dbgv = dbg[...]
        dgd = _dot_exact_lhs(_nn, utri, dbgv)
        is_g = (lane >= 4) & (lane < 8)
        dga = jnp.where(is_g, dgd * nea * _sigmoid(sarg), 0.0)
        dg_ref[...] = jnp.where(lane < 4, dbgv, dga).astype(BF16)
        dpv_ref[0:1, :] += jnp.sum(jnp.where(is_g, dgd * gdec, 0.0), 0, keepdims=True)
        dpv_ref[1:2, :] += jnp.sum(dga, 0, keepdims=True)
        dc_ = dqkv[...] * _dsilu(c)
        for k in range(CONV_WIDTH):
            dcw_ref[k:k + 1, :] += jnp.sum(dc_ * taps[k], 0, keepdims=True)
        ext2 = jnp.concatenate([dc_, dhalo[...]], 0)
        tt2 = _conv_taps_t(ext2, T)
        dp_ref[:, 0:GW3] = sum(cw_ref[k:k + 1, :] * tt2[k] for k in range(CONV_WIDTH)).astype(BF16)
        dhalo[...] = dc_[0:8]

    def rev(i):
        return nt - 1 - i

    def prev8(i):
        return jnp.maximum(rev(i) * t8 - 1, 0)

    return _pcall(
        body, name=name, grid=(nt,),
        in_specs=[pl.BlockSpec((T, GW3), lambda i: (rev(i), OFF_GQKV // GW3)),
                  pl.BlockSpec((8, GW3), lambda i: (prev8(i), OFF_GQKV // GW3)),
                  pl.BlockSpec((T, 512), lambda i: (rev(i), OFF_GZ // 512)),
                  pl.BlockSpec((T, 128), lambda i: (rev(i), OFF_GBA // 128)),
                  pl.BlockSpec((2 * nu, G_HEADS, 128, 128), lambda i: (rev(i), 0, 0, 0)),
                  pl.BlockSpec((nu, G_HEADS, 128, 128), lambda i: (rev(i), 0, 0, 0)),
                  pl.BlockSpec((T, 512), lambda i: (rev(i), MIX_G // 512))] + _gdn_param_specs() + [ANY_SPEC],
        out_specs=[pl.BlockSpec((T, GW3 + 512), lambda i: (rev(i), OFF_GQKV // (GW3 + 512))),
                   pl.BlockSpec((T, 128), lambda i: (rev(i), 0)),
                   pl.BlockSpec((CONV_WIDTH, GW3), lambda i: (0, 0)),
                   pl.BlockSpec((8, 128), lambda i: (0, 0)),
                   pl.BlockSpec((1, 128), lambda i: (0, 0))],
        out_shape=[jax.ShapeDtypeStruct(dproj.shape, BF16),
                   jax.ShapeDtypeStruct((S_, 128), BF16), jax.ShapeDtypeStruct((CONV_WIDTH, GW3), F32),
                   jax.ShapeDtypeStruct((8, 128), F32), jax.ShapeDtypeStruct((1, 128), F32)],
        input_output_aliases={10: 0},
        scratch_shapes=[pltpu.VMEM((G_HEADS, 128, 128), F32), pltpu.VMEM((8, GW3), F32),
                        pltpu.VMEM((T, GW3), F32), pltpu.VMEM((T, 128), F32)],
        compiler_params=_cp(("arbitrary",)),
    )(proj, proj, proj, proj, states, tms, dymix, cw, _gdn_pvec(a_log, dt_bias), nw.reshape(1, 128), dproj)


def _pair_sum_windows(a, b, nsh, width, *, out_dtype, name):
    R_, C = a.shape
    hr = R_ // 2
    nb = width // 128
    assert (3 * nsh) // 128 + nb <= C // 128
    to_perm = _orig_block_to_perm()
    blocks = jnp.asarray([to_perm[(nsh * t) // 128 + j] for t in range(4) for j in range(nb)], jnp.int32)
    table = jnp.concatenate([blocks, lax.axis_index("c").astype(jnp.int32)[None]])

    def body(tab_ref, a_ref, b_ref, o_ref):
        o_ref[...] = (a_ref[...] + b_ref[...]).astype(o_ref.dtype)

    return _pcall(
        body, name=name,
        grid_spec=pltpu.PrefetchScalarGridSpec(
            num_scalar_prefetch=1, grid=(4, nb),
            in_specs=[pl.BlockSpec((hr, 128), lambda t, j, tab: (tab[4 * nb], tab[t * nb + j])),
                      pl.BlockSpec((hr, 128), lambda t, j, tab: (0, tab[t * nb + j]))],
            out_specs=pl.BlockSpec((None, hr, 128), lambda t, j, tab: (t, 0, j))),
        out_shape=jax.ShapeDtypeStruct((4, hr, width), out_dtype),
        compiler_params=_cp(("parallel", "parallel")))(table, a, b)


def _pair_sum_blocks(a, b, *, out_dtype, name):
    L, R_, C = a.shape
    hr = R_ // 2

    def body(a0_ref, a1_ref, b_ref, o_ref):
        mine = jnp.where(lax.axis_index("c") == 0, a0_ref[...], a1_ref[...])
        o_ref[...] = (mine + b_ref[...]).astype(o_ref.dtype)

    def spec(half):
        return pl.BlockSpec((None, hr, C), lambda t: (t, half, 0))

    return _pcall(body, name=name, grid=(L,), in_specs=[spec(0), spec(1), spec(0)], out_specs=spec(0),
                  out_shape=jax.ShapeDtypeStruct((L, hr, C), out_dtype),
                  compiler_params=_cp(("parallel",)))(a, a, b)


def _add_mine(a0, a1, b, *, out_dtype, tr, name):
    R_, C = b.shape

    def body(a0_ref, a1_ref, b_ref, o_ref):
        mine = jnp.where(lax.axis_index("c") == 0, a0_ref[...], a1_ref[...])
        o_ref[...] = (mine + b_ref[...]).astype(o_ref.dtype)

    spec = pl.BlockSpec((tr, C), lambda i: (i, 0))
    return _pcall(body, name=name, grid=(R_ // tr,), in_specs=[spec] * 3, out_specs=spec,
                  out_shape=jax.ShapeDtypeStruct((R_, C), out_dtype), compiler_params=_cp(("parallel",)))(a0, a1, b)


def _sum4(a, mine, *, tr, name):
    _, R_, C = a.shape

    def body(a_ref, m_ref, o_ref):
        s = 2 * lax.axis_index("x") + lax.axis_index("y")
        mv = m_ref[...].astype(F32)
        p = [jnp.where(s == t, mv, a_ref[t].astype(F32)) for t in range(4)]
        o_ref[...] = ((p[0] + p[1]) + p[2]) + p[3]

    return _pcall(body, name=name, grid=(R_ // tr,),
                  in_specs=[pl.BlockSpec((4, tr, C), lambda i: (0, i, 0)), pl.BlockSpec((tr, C), lambda i: (i, 0))],
                  out_specs=pl.BlockSpec((tr, C), lambda i: (i, 0)),
                  out_shape=jax.ShapeDtypeStruct((R_, C), F32), compiler_params=_cp(("parallel",)))(a, mine)


def _adamw_refs(w_ref, g_ref, m_ref, v_ref, d_ref, mo_ref, vo_ref):
    c1 = 1.0 / (1.0 - ADAM_B1 ** ADAM_STEP)
    c2 = 1.0 / (1.0 - ADAM_B2 ** ADAM_STEP)
    gg = g_ref[...]
    mn = ADAM_B1 * m_ref[...] + (1.0 - ADAM_B1) * gg
    vn = ADAM_B2 * v_ref[...] + (1.0 - ADAM_B2) * (gg * gg)
    mo_ref[...] = mn
    vo_ref[...] = vn
    d_ref[...] = -ADAM_LR * ((mn * c1) / (jnp.sqrt(vn * c2) + ADAM_EPS) + ADAM_WD * w_ref[...])


def _adamw_many(ws, gs, ms, vs, *, name):
    n = len(ws)

    def body(*refs):
        for k in range(n):
            _adamw_refs(*[refs[q * n + k] for q in range(7)])

    vm = pl.BlockSpec(memory_space=pltpu.VMEM)
    shp = [jax.ShapeDtypeStruct(w.shape, F32) for w in ws]
    outs = _pcall(body, name=name, in_specs=[vm] * (4 * n), out_specs=[vm] * (3 * n), out_shape=shp * 3,
                  compiler_params=pltpu.CompilerParams(vmem_limit_bytes=VMEM_LIMIT))(*ws, *gs, *ms, *vs)
    return outs[:n], outs[n:2 * n], outs[2 * n:]


def _adamw(w, g, m, v, *, tr, name):
    L, R_, C = w.shape
    body = functools.partial(_adamw_refs)

    spec = pl.BlockSpec((None, tr, C), lambda l, i: (l, i, 0))
    shp = jax.ShapeDtypeStruct((L, R_, C), F32)
    return _pcall(body, name=name, grid=(L, R_ // tr), in_specs=[spec] * 4, out_specs=[spec] * 3,
                  out_shape=[shp] * 3, compiler_params=_cp(("parallel", "parallel")))(w, g, m, v)


def _adamw_cols(w, g, m, v, *, name):
    C, L, R_ = w.shape
    tc = C // 2 if C % 2 == 0 else C

    spec = pl.BlockSpec((tc, L, 128), lambda i, j: (i, 0, j))
    shp = jax.ShapeDtypeStruct((C, L, R_), F32)
    return _pcall(functools.partial(_adamw_refs), name=name, grid=(C // tc, R_ // 128), in_specs=[spec] * 4,
                  out_specs=[spec] * 3, out_shape=[shp] * 3,
                  compiler_params=_cp(("parallel", "parallel")))(w, g, m, v)


HBM_SPEC = pl.BlockSpec(memory_space=pltpu.HBM)


def _place():
    x, y, c = lax.axis_index("x"), lax.axis_index("y"), lax.axis_index("c")
    chips = [(1 - x, y), (x, 1 - y), (1 - x, 1 - y)]
    return x, y, c, 2 * x + y, chips, [2 * cx + cy for cx, cy in chips], (x, y, 1 - c)


def _remote(src, dst, ssem, rsem, dev):
    return pltpu.make_async_remote_copy(src_ref=src, dst_ref=dst, send_sem=ssem, recv_sem=rsem,
                                        device_id=dev, device_id_type=MESH)


def _row_half(ref, lead, hc):
    hl = ref.shape[-2] // 2
    return ref.at[lead, pl.ds(hc * hl, hl), :]


def _gather_side(items):
    n = len(items)

    def copies(ins, outs, ssem, rsem):
        x, y, c, s, chips, sid, sib = _place()
        cps = [_remote(_row_half(ins[k], items[k][1], c), _row_half(outs[k], s, c),
                       ssem.at[3 * k + j], rsem.at[3 * k + j], (*chip, c))
               for k in range(n) for j, chip in enumerate(chips)]
        return cps, c, sid, sib

    def start(ins, outs, ssem, rsem):
        for cp in copies(ins, outs, ssem, rsem)[0]:
            cp.start()

    def finish(ins, outs, ssem, rsem):
        cps, c, sid, sib = copies(ins, outs, ssem, rsem)
        for k in range(n):
            for j in range(3):
                got = _row_half(outs[k], sid[j], c)
                _remote(got, got, ssem.at[3 * k + j], rsem.at[3 * k + j], sib).wait_recv()
        for cp in cps:
            cp.wait_send()

    shapes = [jax.ShapeDtypeStruct((4,) + w.shape[1:], w.dtype) for w, _ in items]
    return _Side([w for w, _ in items], shapes, 3 * n, start, finish)


def _gather_join(gathered, name):
    n = len(gathered)

    def body(*refs):
        outs, ssem, rsem = refs[n:2 * n], refs[2 * n], refs[2 * n + 1]
        x, y, c, s, chips, sid, sib = _place()
        cps = []
        for k in range(n):
            for j in range(3):
                mine = _row_half(outs[k], sid[j], c)
                cps.append(_remote(mine, mine, ssem.at[3 * k + j], rsem.at[3 * k + j], sib))
        for cp in cps:
            cp.start()
        for k in range(n):
            for j in range(3):
                other = _row_half(outs[k], sid[j], 1 - c)
                _remote(other, other, ssem.at[3 * k + j], rsem.at[3 * k + j], sib).wait_recv()
        for cp in cps:
            cp.wait_send()

    return _pcall(
        body, name=name, in_specs=[HBM_SPEC] * n, out_specs=[HBM_SPEC] * n,
        out_shape=[jax.ShapeDtypeStruct(g.shape, g.dtype) for g in gathered],
        input_output_aliases={k: k for k in range(n)},
        scratch_shapes=[pltpu.SemaphoreType.DMA((3 * n,)), pltpu.SemaphoreType.DMA((3 * n,))],
    )(*gathered)


def _gather_layer0(win, conv):
    def body(win_ref, cv_ref, gin_ref, gcv_ref, ssem, rsem):
        x, y, c, s, chips, sid, sib = _place()

        def in_half(slot, hc):
            return _row_half(gin_ref, slot, hc)

        sends = []
        for j, chip in enumerate(chips):
            dev = (*chip, c)
            sends.append(_remote(_row_half(win_ref, 0, c), in_half(s, c), ssem.at[j], rsem.at[j], dev))
            sends.append(_remote(cv_ref, gcv_ref.at[s], ssem.at[3 + j], rsem.at[3 + j], dev))
        for cp in sends:
            cp.start()
        for j in range(3):
            _remote(in_half(sid[j], c), in_half(sid[j], c), ssem.at[j], rsem.at[j], sib).wait_recv()
            f = _remote(in_half(sid[j], c), in_half(sid[j], c), ssem.at[6 + j], rsem.at[6 + j], sib)
            f.start()
            sends.append(f)
        for j in range(3):
            _remote(in_half(sid[j], 1 - c), in_half(sid[j], 1 - c), ssem.at[6 + j], rsem.at[6 + j], sib).wait_recv()
            _remote(gcv_ref.at[sid[j]], gcv_ref.at[sid[j]], ssem.at[3 + j], rsem.at[3 + j], sib).wait_recv()
        for cp in sends:
            cp.wait_send()

    return _pcall(
        body, name="gather_layer0",
        in_specs=[HBM_SPEC] * 2, out_specs=[HBM_SPEC] * 2,
        out_shape=[jax.ShapeDtypeStruct((4,) + win.shape[1:], win.dtype),
                   jax.ShapeDtypeStruct((4,) + conv.shape, conv.dtype)],
        scratch_shapes=[pltpu.SemaphoreType.DMA((9,)), pltpu.SemaphoreType.DMA((9,))],
    )(win, conv)


def _swap_halves(arrs, axes, name):
    n = len(arrs)

    def half_shape(a, ax):
        return a.shape[:ax] + (a.shape[ax] // 2,) + a.shape[ax + 1:]

    def body(*refs):
        src, dst, ssem, rsem = refs[:n], refs[n:2 * n], refs[2 * n], refs[2 * n + 1]
        x, y, c, s, chips, sid, sib = _place()
        cps = []
        for k in range(n):
            hl = src[k].shape[axes[k]] // 2
            idx = [slice(None)] * len(src[k].shape)
            idx[axes[k]] = pl.ds((1 - c) * hl, hl)
            cps.append(_remote(src[k].at[tuple(idx)], dst[k], ssem.at[k], rsem.at[k], sib))
        for cp in cps:
            cp.start()
        for cp in cps:
            cp.wait()

    return _pcall(
        body, name=name, in_specs=[HBM_SPEC] * n, out_specs=[HBM_SPEC] * n,
        out_shape=[jax.ShapeDtypeStruct(half_shape(a, ax), a.dtype) for a, ax in zip(arrs, axes)],
        scratch_shapes=[pltpu.SemaphoreType.DMA((n,)), pltpu.SemaphoreType.DMA((n,))],
    )(*arrs)


def _swap_side(arrs, axes):
    n = len(arrs)

    def copies(ins, outs, ssem, rsem):
        x, y, c, s, chips, sid, sib = _place()
        cps = []
        for k in range(n):
            hl = ins[k].shape[axes[k]] // 2
            idx = [slice(None)] * len(ins[k].shape)
            idx[axes[k]] = pl.ds((1 - c) * hl, hl)
            cps.append(_remote(ins[k].at[tuple(idx)], outs[k], ssem.at[k], rsem.at[k], sib))
        return cps

    def start(ins, outs, ssem, rsem):
        for cp in copies(ins, outs, ssem, rsem):
            cp.start()

    def finish(ins, outs, ssem, rsem):
        for cp in copies(ins, outs, ssem, rsem):
            cp.wait()

    shapes = [jax.ShapeDtypeStruct(a.shape[:ax] + (a.shape[ax] // 2,) + a.shape[ax + 1:], a.dtype)
              for a, ax in zip(arrs, axes)]
    return _Side(list(arrs), shapes, n, start, finish)


def _chips_side(arrs, per_target):
    n = len(arrs)

    def copies(ins, outs, ssem, rsem):
        x, y, c, s, chips, sid, sib = _place()
        cps = [_remote(ins[k].at[sid[j]] if per_target[k] else ins[k], outs[k].at[s],
                       ssem.at[3 * k + j], rsem.at[3 * k + j], (*chip, c))
               for k in range(n) for j, chip in enumerate(chips)]
        return cps, sid, sib

    def start(ins, outs, ssem, rsem):
        for cp in copies(ins, outs, ssem, rsem)[0]:
            cp.start()

    def finish(ins, outs, ssem, rsem):
        cps, sid, sib = copies(ins, outs, ssem, rsem)
        for k in range(n):
            for j in range(3):
                got = outs[k].at[sid[j]]
                _remote(got, got, ssem.at[3 * k + j], rsem.at[3 * k + j], sib).wait_recv()
        for cp in cps:
            cp.wait_send()

    shapes = [jax.ShapeDtypeStruct(a.shape if pt else (4,) + a.shape, a.dtype) for a, pt in zip(arrs, per_target)]
    return _Side(list(arrs), shapes, 3 * n, start, finish)


def _scatter_chips(arrs, per_target, name):
    n = len(arrs)

    def body(*refs):
        src, dst = refs[:n], refs[n:2 * n]
        ssem, rsem = refs[2 * n], refs[2 * n + 1]
        x, y, c, s, chips, sid, sib = _place()
        sends = []
        for k in range(n):
            for j, chip in enumerate(chips):
                piece = src[k].at[sid[j]] if per_target[k] else src[k]
                sends.append(_remote(piece, dst[k].at[s], ssem.at[3 * k + j], rsem.at[3 * k + j], (*chip, c)))
        for cp in sends:
            cp.start()
        for k in range(n):
            for j in range(3):
                _remote(dst[k].at[sid[j]], dst[k].at[sid[j]], ssem.at[3 * k + j], rsem.at[3 * k + j], sib).wait_recv()
        for cp in sends:
            cp.wait_send()

    outs = [jax.ShapeDtypeStruct(a.shape if pt else (4,) + a.shape, a.dtype) for a, pt in zip(arrs, per_target)]
    return _pcall(
        body, name=name, in_specs=[HBM_SPEC] * n, out_specs=[HBM_SPEC] * n, out_shape=outs,
        scratch_shapes=[pltpu.SemaphoreType.DMA((3 * n,)), pltpu.SemaphoreType.DMA((3 * n,))],
    )(*arrs)


def _swap_whole(arrs, name):
    n = len(arrs)

    def body(*refs):
        src, dst, ssem, rsem = refs[:n], refs[n:2 * n], refs[2 * n], refs[2 * n + 1]
        *_, sib = _place()
        cps = [_remote(src[k], dst[k], ssem.at[k], rsem.at[k], sib) for k in range(n)]
        for cp in cps:
            cp.start()
        for cp in cps:
            cp.wait()

    return _pcall(
        body, name=name, in_specs=[HBM_SPEC] * n, out_specs=[HBM_SPEC] * n,
        out_shape=[jax.ShapeDtypeStruct(a.shape, a.dtype) for a in arrs],
        scratch_shapes=[pltpu.SemaphoreType.DMA((n,)), pltpu.SemaphoreType.DMA((n,))],
    )(*arrs)


def _perm_cols(w):
    parts = [w[..., int(_ORIG_OFF[oi]):int(_ORIG_OFF[oi]) + IN_SIZES[oi]] for oi, _ in _PIECES]
    parts.append(jnp.zeros(w.shape[:-1] + (NP - N_IN,), w.dtype))
    return jnp.concatenate(parts, -1)


def _perm_rows(w):
    return jnp.concatenate([w[..., 512:1536, :], w[..., 0:512, :], w[..., 1536:2048, :]], -2)


_SMALL = ("sinks", "r_conv_b", "r_wa", "r_ba", "r_wx", "r_bx", "r_lam", "g_a_log", "g_dt_bias", "g_norm_w",
          "ln_g", "ln_b", "r_conv_w", "g_conv_w")
_PACK_ROWS = 16


def _piece_rows(n):
    return -(-n // (128 * _PACK_ROWS)) * _PACK_ROWS


def _pack(arrs):
    parts = []
    for a in arrs:
        n = int(np.prod(a.shape))
        rows = _piece_rows(n)
        if n % 128 == 0:
            blk = a.reshape(n // 128, 128)
        else:
            blk = jnp.pad(a.reshape(1, n), ((0, 0), (0, (-n) % 128))).reshape(-1, 128)
        if blk.shape[0] < rows:
            blk = jnp.pad(blk, ((0, rows - blk.shape[0]), (0, 0)))
        parts.append(blk)
    return jnp.concatenate(parts, 0)


def _unpack(packed, shapes):
    out = []
    r = 0
    for shp in shapes:
        n = int(np.prod(shp))
        if n % 128 == 0:
            out.append(packed[r:r + n // 128].reshape(shp))
        else:
            nr = -(-n // 128)
            out.append(packed[r:r + nr].reshape(1, nr * 128)[:, :n].reshape(shp))
        r += _piece_rows(n)
    return out


def _tile(n, t):
    return min(n, t)


def _layer_fwd(l, x, xb, wb, wob, ln, rope_c, rope_s, p, side=None, target=None):
    S_ = x.shape[0]
    proj = _matmul(xb, wb, ta=False, tb=False, tm=_tile(S_, 1024), tn=NP // 4, tk=wb.shape[0], out_dtype=F32,
                   name=f"in_proj_{l}", side=side)
    side_out = None
    if side:
        proj, side_out = proj
    h, ymix = _rglru_fwd(proj, p["r_conv_w"], p["r_conv_b"], p["r_wa"], p["r_ba"], p["r_wx"], p["r_bx"], p["r_lam"],
                         T=_tile(S_, 256), name=f"rglru_fwd_{l}")
    ymix = _attn_fwd(proj, rope_c, rope_s, p["sinks"], ymix, T=_tile(S_, 512), name=f"attn_fwd_{l}")
    ymix, st, tms = _gdn_fwd(proj, p["g_conv_w"], p["g_a_log"], p["g_dt_bias"], p["g_norm_w"], ymix,
                             T=_tile(S_, 256), name=f"gdn_fwd_{l}")
    out = _outproj(ymix, wob(side_out), x, ln[0], ln[1], tm=_tile(S_, 512), name=f"out_proj_{l}", target=target)
    sv = dict(proj=proj, h=h, st=st, tms=tms, ymix=ymix)
    if target is None:
        sv["z"], sv["y"], sv["yb"] = out
    else:
        sv["head"] = out
    return sv


def _layer_bwd(l, sv, x_b, dz, dzb, wb, wob, rope_c, rope_s, p, side_dmix=None, side_dw_in=None, side_dx=None):
    S_, D = dz.shape
    proj = sv["proj"]
    dwo = _matmul(sv["ymix"], dzb, ta=True, tb=False, tm=512, tn=_tile(D, 2048), tk=_tile(S_, 2048),
                  out_dtype=F32, name=f"dw_out_{l}",
                  out_blocks=((MIX_WIDTH, D), (512, _tile(D, 2048)),
                              lambda i, j: (jnp.where(i == 3, 3, (i + 1) % 3), j)))
    side = side_dmix(dwo) if side_dmix else None
    dymix = _matmul(dzb, wob, ta=False, tb=True, tm=_tile(S_, 1024), tn=1024, tk=D, out_dtype=F32,
                    name=f"dmix_{l}", side=side)
    out_dmix = None
    if side:
        dymix, out_dmix = dymix
    dproj, dk, dv, dkt, dvt, dsk = _attn_bwd(proj, rope_c, rope_s, p["sinks"], dymix, T=_tile(S_, 512),
                                             name=f"attn_bwd_{l}")
    (dproj, dcw_r, dcb_r, dwa, dba, dwx, dbx, dlam) = _rglru_bwd(
        proj, sv["h"], dymix, dproj, p["r_conv_w"], p["r_conv_b"], p["r_wa"], p["r_ba"], p["r_wx"], p["r_bx"],
        p["r_lam"], T=_tile(S_, 256), name=f"rglru_bwd_{l}")
    dproj, dgba, dcw_g, dpv, dnw = _gdn_bwd(proj, sv["st"], sv["tms"], dymix, dproj, p["g_conv_w"], p["g_a_log"],
                                            p["g_dt_bias"], p["g_norm_w"], T=_tile(S_, 256), name=f"gdn_bwd_{l}")
    tail = jnp.concatenate([dk[128:], dkt, dv[128:], dvt], 0).reshape(2, S_, 128)
    tail = jnp.concatenate([tail[0], tail[1], dgba, jnp.zeros((S_, NP - OFF_GBA - 128), BF16)], 1)
    dproj = lax.dynamic_update_slice(dproj, tail, (0, OFF_AK))
    small = dict(sinks=dsk[:, 0], r_conv_b=dcb_r[0], r_wa=dwa, r_ba=dba[0], r_wx=dwx, r_bx=dbx[0], r_lam=dlam[0],
                 g_a_log=dpv[0, 4:8], g_dt_bias=dpv[1, 4:8], g_norm_w=dnw[0], r_conv_w=dcw_r, g_conv_w=dcw_g)
    side = side_dw_in(small, dwo, out_dmix) if side_dw_in else None
    dwin = _matmul(x_b, dproj, ta=True, tb=False, tm=_tile(D, 1024), tn=NP // 4, tk=_tile(S_, 2048),
                   out_dtype=F32, name=f"dw_in_{l}", side=side)
    out_dw_in = None
    if side:
        dwin, out_dw_in = dwin
    side = side_dx(dwin) if side_dx else None
    dx_args = dict(ta=False, tb=True, tm=_tile(S_, 1024), tn=_tile(D, 1024), tk=NP // 2, out_dtype=F32, extra=dz,
                   alpha=DEEPNORM_ALPHA)
    out_dx = None
    if side:
        dx, out_dx = _matmul(dproj, wb, name=f"dx_{l}", side=side, **dx_args)
    else:
        dx = _matmul(dproj, wb, name=f"dx_{l}", **dx_args)
    return dx, dwin, dwo, small, out_dw_in, out_dx


def kernel(x, w_in, sinks, r_conv_w, r_conv_b, r_wa, r_ba, r_wx, r_bx, r_lam, g_conv_w, g_a_log, g_dt_bias, g_norm_w, w_out, ln_g, ln_b, loss_target, m_w_in, m_sinks, m_r_conv_w, m_r_conv_b, m_r_wa, m_r_ba, m_r_wx, m_r_bx, m_r_lam, m_g_conv_w, m_g_a_log, m_g_dt_bias, m_g_norm_w, m_w_out, m_ln_g, m_ln_b, v_w_in, v_sinks, v_r_conv_w, v_r_conv_b, v_r_wa, v_r_ba, v_r_wx, v_r_bx, v_r_lam, v_g_conv_w, v_g_a_log, v_g_dt_bias, v_g_norm_w, v_w_out, v_ln_g, v_ln_b):
    S_, D = x.shape[1], x.shape[2]
    nsh = w_in.shape[2]
    rsh = w_out.shape[1]
    cx, cy, cc = lax.axis_index("x"), lax.axis_index("y"), lax.axis_index("c")
    chip = 2 * cx + cy
    rcw_n, gcw_n = r_conv_w.shape[2], g_conv_w.shape[2]

    conv_pack = jnp.concatenate([r_conv_w, g_conv_w], 2)
    w_in_b, w_out_b = w_in.astype(BF16), w_out.astype(BF16)
    g_in0, g_conv = _gather_layer0(w_in_b, conv_pack)

    def shards(own, got):
        return [jnp.where(chip == t, own, got[t]) for t in range(4)]

    def w_in_of(l, g_in):
        return _perm_cols(jnp.concatenate(shards(w_in_b[l], g_in), 1))

    def w_out_of(l, g_out):
        return _perm_rows(jnp.concatenate(shards(w_out_b[l], g_out), 0))

    rcw = jnp.concatenate(shards(r_conv_w, g_conv[:, :, :, :rcw_n]), 2)
    gcw = jnp.concatenate(shards(g_conv_w, g_conv[:, :, :, rcw_n:]), 2)

    pos = jnp.arange(S_, dtype=F32)[:, None]
    inv = 1.0 / (ROPE_THETA ** (jnp.arange(0, A_HEAD_DIM, 2, dtype=F32) / A_HEAD_DIM))
    ang = pos * inv[None, :]
    cos, sin = jnp.cos(ang), jnp.sin(ang)
    rope_c = jnp.concatenate([cos, cos, cos, cos], 1)
    rope_s = jnp.concatenate([-sin, sin, -sin, sin], 1)

    def params(l):
        return dict(sinks=sinks[l], r_conv_w=rcw[l], r_conv_b=r_conv_b[l], r_wa=r_wa[l], r_ba=r_ba[l],
                    r_wx=r_wx[l], r_bx=r_bx[l], r_lam=r_lam[l], g_conv_w=gcw[l], g_a_log=g_a_log[l],
                    g_dt_bias=g_dt_bias[l], g_norm_w=g_norm_w[l])

    assert DEPTH == 2
    xb0 = x[0].astype(BF16)
    wb, wob = [w_in_of(0, g_in0), None], [None, None]
    late = {}

    def w_out_0(arrived):
        late["w_in_1"], g_out0 = _gather_join(arrived, "gather_join_0")
        wob[0] = w_out_of(0, g_out0)
        return wob[0]

    def w_out_1(arrived):
        wob[1] = w_out_of(1, _gather_join(arrived, "gather_join_1")[0])
        return wob[1]

    sv0 = _layer_fwd(0, x[0], xb0, wb[0], w_out_0, (ln_g[0], ln_b[0]), rope_c, rope_s, params(0),
                     side=_gather_side([(w_in_b, 1), (w_out_b, 0)]))
    wb[1] = w_in_of(1, late["w_in_1"])
    sv1 = _layer_fwd(1, sv0["y"], sv0["yb"], wb[1], w_out_1, (ln_g[1], ln_b[1]), rope_c, rope_s, params(1),
                     side=_gather_side([(w_out_b, 1)]), target=loss_target[0])
    saved, xbs = [sv0, sv1], [xb0, sv0["yb"]]

    tm_ln = _tile(S_, 256)
    dz, dzb, dg_l, db_l, loss_part = saved[-1]["head"]
    assert DEPTH == 2
    wcov = (-(-nsh // 128) + 1) * 128
    names = list(_SMALL)

    def own(a):
        return lax.dynamic_index_in_dim(a, chip, 0, keepdims=False)

    def sum_in(l, cp, arrived):
        return _sum4(arrived, own(cp), tr=_tile(D // 2, 256), name=f"chip_sum_w_in_{l}")

    def sum_out(l, cp, arrived):
        return _sum4(arrived, own(cp), tr=rsh // 2, name=f"chip_sum_w_out_{l}")

    dlng, dlnb = [None, dg_l[0]], [None, db_l[0]]
    dx, dwin1, dwo1, small1, _, _ = _layer_bwd(1, saved[1], xbs[1], dz, dzb, wb[1], wob[1], rope_c, rope_s, params(1))
    dwo1_4 = dwo1.reshape(4, rsh, D)
    dz, dzb, dg_l, db_l, _ = _ln_bwd(saved[0]["z"], ln_g[0], ln_b[0], dx, tm=tm_ln, name="ln_bwd_0")
    dlng[0], dlnb[0] = dg_l[0], db_l[0]

    held = {}

    def side_dmix(dwo0):
        return _swap_side([dwin1, dwo1_4, dwo0.reshape(4, rsh, D)], [0, 1, 1])

    def side_dw_in(small0, dwo0, got):
        sm = {k: jnp.stack([small0[k], small1[k]]) for k in small0}
        sm["ln_g"], sm["ln_b"] = jnp.stack(dlng), jnp.stack(dlnb)
        gs = _pack([sm[n] for n in names])
        (got_s,) = _swap_halves([gs], [0], "reduce_pair_small")
        held["in_cp1"] = _pair_sum_windows(dwin1, got[0], nsh, wcov, out_dtype=BF16, name="pair_sum_w_in_1")
        held["out_cp1"] = _pair_sum_blocks(dwo1_4, got[1], out_dtype=BF16, name="pair_sum_w_out_1")
        held["out_cp0"] = _pair_sum_blocks(dwo0.reshape(4, rsh, D), got[2], out_dtype=BF16, name="pair_sum_w_out_0")
        held["s_cp"] = _pair_sum_blocks(gs[None], got_s[None], out_dtype=F32, name="pair_sum_small")[0]
        held["shapes"] = [sm[n].shape for n in names]
        return _chips_side([held["in_cp1"], held["out_cp1"], held["out_cp0"]], [True, True, True])

    def side_dx(dwin0):
        got = _swap_halves([dwin0], [0], "reduce_pair_0b")
        held["in_cp0"] = _pair_sum_windows(dwin0, got[0], nsh, wcov, out_dtype=BF16, name="pair_sum_w_in_0")
        return _chips_side([held["in_cp0"], held["s_cp"]], [True, False])

    dx, _, _, _, arrived_a, arrived_b = _layer_bwd(0, saved[0], xbs[0], dz, dzb, wb[0], wob[0], rope_c, rope_s,
                                                   params(0), side_dmix=side_dmix, side_dw_in=side_dw_in,
                                                   side_dx=side_dx)
    grad_x = dx[None]
    loss = lax.psum(loss_part[0, 0], ("x", "y", "c"))
    s_cp = held["s_cp"]
    mine = [sum_in(0, held["in_cp0"], arrived_b[0]), sum_out(0, held["out_cp0"], arrived_a[2]),
            sum_in(1, held["in_cp1"], arrived_a[0]), sum_out(1, held["out_cp1"], arrived_a[1]),
            _sum4(arrived_b[1], s_cp, tr=s_cp.shape[0], name="chip_sum_small")]
    other = _swap_whole(mine, "reduce_join")

    def both(k, axis):
        return jnp.where(cc == 0, jnp.concatenate([mine[k], other[k]], axis),
                         jnp.concatenate([other[k], mine[k]], axis))

    g_w_in = lax.dynamic_slice_in_dim(jnp.stack([both(2 * l, 0) for l in range(DEPTH)]), (nsh * chip) % 128, nsh, 2)
    g_w_out = jnp.stack([both(2 * l + 1, 0) for l in range(DEPTH)])
    g_small = both(2 * DEPTH, 0)

    gsm = dict(zip(names, _unpack(g_small, held["shapes"])))
    gsm["r_conv_w"] = lax.dynamic_slice_in_dim(gsm["r_conv_w"], chip * rcw_n, rcw_n, 2)
    gsm["g_conv_w"] = lax.dynamic_slice_in_dim(gsm["g_conv_w"], chip * gcw_n, gcw_n, 2)
    wts = dict(sinks=sinks, r_conv_w=r_conv_w, r_conv_b=r_conv_b, r_wa=r_wa, r_ba=r_ba, r_wx=r_wx, r_bx=r_bx,
               r_lam=r_lam, g_conv_w=g_conv_w, g_a_log=g_a_log, g_dt_bias=g_dt_bias, g_norm_w=g_norm_w,
               ln_g=ln_g, ln_b=ln_b)
    mom = dict(sinks=m_sinks, r_conv_w=m_r_conv_w, r_conv_b=m_r_conv_b, r_wa=m_r_wa, r_ba=m_r_ba, r_wx=m_r_wx,
               r_bx=m_r_bx, r_lam=m_r_lam, g_conv_w=m_g_conv_w, g_a_log=m_g_a_log, g_dt_bias=m_g_dt_bias,
               g_norm_w=m_g_norm_w, ln_g=m_ln_g, ln_b=m_ln_b)
    vel = dict(sinks=v_sinks, r_conv_w=v_r_conv_w, r_conv_b=v_r_conv_b, r_wa=v_r_wa, r_ba=v_r_ba, r_wx=v_r_wx,
               r_bx=v_r_bx, r_lam=v_r_lam, g_conv_w=v_g_conv_w, g_a_log=v_g_a_log, g_dt_bias=v_g_dt_bias,
               g_norm_w=v_g_norm_w, ln_g=v_ln_g, ln_b=v_ln_b)
    d_s, m_s, v_s = _adamw_many(*[[d[n] for n in names] for d in (wts, gsm, mom, vel)], name="adamw_small")
    d_sm, m_sm, v_sm = (dict(zip(names, a)) for a in (d_s, m_s, v_s))
    def cols(a):
        return jnp.transpose(a, (2, 0, 1))

    g_w_in_t = cols(g_w_in)
    outs_t = _adamw_cols(cols(w_in), g_w_in_t, cols(m_w_in), cols(v_w_in), name="adamw_w_in")
    d_in, m_in, v_in = (jnp.transpose(a, (1, 2, 0)) for a in outs_t)
    g_w_in = jnp.transpose(g_w_in_t, (1, 2, 0))
    d_out, m_out, v_out = _adamw(w_out, g_w_out, m_w_out, v_w_out, tr=256, name="adamw_w_out")

    order = ["w_in", "sinks", "r_conv_w", "r_conv_b", "r_wa", "r_ba", "r_wx", "r_bx", "r_lam", "g_conv_w",
             "g_a_log", "g_dt_bias", "g_norm_w", "w_out", "ln_g", "ln_b"]
    grads = dict(gsm, w_in=g_w_in, w_out=g_w_out)
    deltas = dict(d_sm, w_in=d_in, w_out=d_out)
    new_m = dict(m_sm, w_in=m_in, w_out=m_out)
    new_v = dict(v_sm, w_in=v_in, w_out=v_out)
    return (loss, grad_x, *[grads[n] for n in order], *[deltas[n] for n in order],
            *[new_m[n] for n in order], *[new_v[n] for n in order])
```

```python
import functools

import jax
import jax.numpy as jnp
import numpy as np
from jax import lax
from jax.experimental import pallas as pl
from jax.experimental.pallas import tpu as pltpu

F32 = jnp.float32
BF16 = jnp.bfloat16
MESH = pl.DeviceIdType.MESH

DEPTH = 2
A_HEADS, A_KV_HEADS, A_HEAD_DIM = 8, 2, 64
A_WIDTH, A_KV_WIDTH = 512, 128
WINDOW = 128
ROPE_THETA = 10000.0
R_WIDTH, R_BLOCKS, R_BLOCK_DIM, R_C = 1024, 8, 128, 8.0
CONV_WIDTH = 4
G_HEADS, G_HEAD_DIM, G_WIDTH, G_CHUNK = 4, 128, 512, 64
MIX_WIDTH = 2048
IN_SIZES = (512, 128, 128, 512, 1024, 1024, 512, 512, 512, 512, 4, 4)
N_IN = 5384
DEEPNORM_ALPHA = (2 * DEPTH) ** 0.25
LN_EPS = 1e-5
RMS_EPS = 1e-6
ADAM_LR, ADAM_B1, ADAM_B2, ADAM_EPS, ADAM_WD, ADAM_STEP = 0.001, 0.9, 0.999, 1e-08, 0.01, 10

NP = 5632
OFF_GQKV, OFF_GZ, OFF_RX, OFF_RZ, OFF_AQ, OFF_AZ, OFF_AK, OFF_AV, OFF_GBA = (
    0, 1536, 2048, 3072, 4096, 4608, 5120, 5248, 5376)
_ORIG_OFF = np.concatenate([[0], np.cumsum(IN_SIZES)])[:-1]
_PIECES = ((6, OFF_GQKV), (7, OFF_GQKV + 512), (8, OFF_GQKV + 1024), (9, OFF_GZ), (4, OFF_RX), (5, OFF_RZ),
           (0, OFF_AQ), (3, OFF_AZ), (1, OFF_AK), (2, OFF_AV), (10, OFF_GBA), (11, OFF_GBA + 4))


def _orig_block_to_perm():
    table = list(range(NP // 128))
    for oi, off in _PIECES:
        if IN_SIZES[oi] % 128 == 0:
            for k in range(IN_SIZES[oi] // 128):
                table[int(_ORIG_OFF[oi]) // 128 + k] = off // 128 + k
    return table
MIX_R, MIX_A, MIX_G = 0, 1024, 1536
VMEM_LIMIT = 56 * 1024 * 1024
ANY_SPEC = pl.BlockSpec(memory_space=pl.ANY)


def _pcall(body, **kw):
    return pl.pallas_call(body, **kw)


def _cp(sem, limit=VMEM_LIMIT):
    return pltpu.CompilerParams(dimension_semantics=sem, vmem_limit_bytes=limit)


def _sigmoid(x):
    return 0.5 + 0.5 * jnp.tanh(0.5 * x)


def _silu(x):
    return x * _sigmoid(x)


def _dsilu(x):
    s = _sigmoid(x)
    return s * (1.0 + x * (1.0 - s))


def _log1p(x):
    u = 1.0 + x
    d = jnp.where(u == 1.0, 1.0, u - 1.0)
    return jnp.where(u == 1.0, x, jnp.log(u) * (x / d))


def _softplus(x):
    return jnp.maximum(x, 0.0) + _log1p(jnp.exp(-jnp.abs(x)))


def _one_minus_exp(x):
    series = -x * (1.0 + x * (0.5 + x * (1.0 / 6.0 + x * (1.0 / 24.0))))
    return jnp.where(x > -0.05, series, 1.0 - jnp.exp(x))


def _nn(a, b):
    return lax.dot_general(a, b, (((1,), (0,)), ((), ())), preferred_element_type=F32)


def _nt(a, b):
    return lax.dot_general(a, b, (((1,), (1,)), ((), ())), preferred_element_type=F32)


def _tn(a, b):
    return lax.dot_general(a, b, (((0,), (0,)), ((), ())), preferred_element_type=F32)


def _b(x):
    return x.astype(BF16)


def _split3(x):
    hi = x.astype(BF16)
    r1 = x - hi.astype(F32)
    mid = r1.astype(BF16)
    lo = (r1 - mid.astype(F32)).astype(BF16)
    return hi, mid, lo


def _dot3(f, a, b):
    ah, am, _ = _split3(a)
    bh, bm, _ = _split3(b)
    return f(ah, bh) + (f(ah, bm) + f(am, bh))


def _dot_exact_lhs(f, a_bf16, b):
    bh, bm, bl = _split3(b)
    return f(a_bf16, bh) + (f(a_bf16, bm) + f(a_bf16, bl))


def _rot(x):
    w = x.shape[-1]
    lane = lax.broadcasted_iota(jnp.int32, (1, w), 1)
    return jnp.where((lane & 63) < 32, pltpu.roll(x, w - 32, 1), pltpu.roll(x, 32, 1))


def _conv_taps(ext, n):
    return [pltpu.roll(ext, 3 - k, 0)[8:8 + n] if k < 3 else ext[8:8 + n] for k in range(CONV_WIDTH)]


def _conv_taps_t(ext, n):
    m = ext.shape[0]
    return [pltpu.roll(ext, m - (3 - k), 0)[0:n] if k < 3 else ext[0:n] for k in range(CONV_WIDTH)]


def _scan_steps(a, b, pos, span, shifts, reverse):
    n = a.shape[0]
    for s in shifts:
        if reverse:
            a_sh = pltpu.roll(a, n - s, 0)
            b_sh = pltpu.roll(b, n - s, 0)
            ok = pos < (span - s)
        else:
            a_sh = pltpu.roll(a, s, 0)
            b_sh = pltpu.roll(b, s, 0)
            ok = pos >= s
        b = jnp.where(ok, a * b_sh + b, b)
        a = jnp.where(ok, a * a_sh, a)
    return a, b


def _scan_lin(a, b, reverse):
    n = a.shape[0]
    shifts = []
    s = 1
    while s < n:
        shifts.append(s)
        s *= 2
    return _scan_steps(a, b, lax.broadcasted_iota(jnp.int32, (n, 1), 0), n, shifts, reverse)


class _Side:
    def __init__(self, inputs, out_shapes, n_sems, start, finish):
        self.inputs, self.out_shapes, self.n_sems, self.start, self.finish = inputs, out_shapes, n_sems, start, finish


def _matmul(a, b, *, ta, tb, tm, tn, tk, out_dtype, name, extra=None, alpha=0.0, out_blocks=None, side=None):
    if ta:
        K, M = a.shape
    else:
        M, K = a.shape
    if tb:
        N, K2 = b.shape
    else:
        K2, N = b.shape
    assert K == K2 and M % tm == 0 and N % tn == 0 and K % tk == 0, (a.shape, b.shape, tm, tn, tk)
    nk = K // tk
    ca = 0 if ta else 1
    cb = 1 if tb else 0
    has_extra = extra is not None

    assert nk == 1 or out_dtype == F32
    n_in = 2 + int(has_extra)
    ns_in = len(side.inputs) if side else 0
    ns_out = len(side.out_shapes) if side else 0
    grid = (M // tm, N // tn, nk)

    def body(*refs):
        a_ref, b_ref = refs[0], refs[1]
        e_ref = refs[2] if has_extra else None
        o_ref = refs[n_in + ns_in]
        k = pl.program_id(2)
        if side:
            s_in = refs[n_in:n_in + ns_in]
            s_out = refs[n_in + ns_in + 1:n_in + ns_in + 1 + ns_out]
            ssem, rsem = refs[-2], refs[-1]
            i, j = pl.program_id(0), pl.program_id(1)

            @pl.when((i == 0) & (j == 0) & (k == 0))
            def _():
                side.start(s_in, s_out, ssem, rsem)

            @pl.when((i == grid[0] - 1) & (j == grid[1] - 1) & (k == grid[2] - 1))
            def _():
                side.finish(s_in, s_out, ssem, rsem)

        part = lax.dot_general(a_ref[...], b_ref[...], (((ca,), (cb,)), ((), ())), preferred_element_type=F32)
        if nk == 1:
            if e_ref is not None:
                part = part + alpha * e_ref[...]
            o_ref[...] = part.astype(o_ref.dtype)
            return

        @pl.when(k == 0)
        def _():
            o_ref[...] = part

        @pl.when((k > 0) & (k < nk - 1))
        def _():
            o_ref[...] += part

        @pl.when(k == nk - 1)
        def _():
            last = o_ref[...] + part
            if e_ref is not None:
                last = last + alpha * e_ref[...]
            o_ref[...] = last

    a_spec = (pl.BlockSpec((tk, tm), lambda i, j, k: (k, i)) if ta
              else pl.BlockSpec((tm, tk), lambda i, j, k: (i, k)))
    b_spec = (pl.BlockSpec((tn, tk), lambda i, j, k: (j, k)) if tb
              else pl.BlockSpec((tk, tn), lambda i, j, k: (k, j)))
    e_spec = pl.BlockSpec((tm, tn), lambda i, j, k: (i, j))
    if out_blocks is None:
        o_spec, o_shape = e_spec, (M, N)
    else:
        o_shape, o_block, o_map = out_blocks
        o_spec = pl.BlockSpec(o_block, lambda i, j, k: o_map(i, j))
    in_specs = [a_spec, b_spec] + ([e_spec] if has_extra else [])
    args = (a, b) + ((extra,) if has_extra else ())
    if not side:
        return _pcall(
            body, name=name, grid=grid, in_specs=in_specs, out_specs=o_spec,
            out_shape=jax.ShapeDtypeStruct(o_shape, out_dtype),
            compiler_params=_cp(("parallel", "parallel", "arbitrary")),
        )(*args)
    outs = _pcall(
        body, name=name, grid=grid, in_specs=in_specs + [HBM_SPEC] * ns_in,
        out_specs=[o_spec] + [HBM_SPEC] * ns_out,
        out_shape=[jax.ShapeDtypeStruct(o_shape, out_dtype)] + list(side.out_shapes),
        scratch_shapes=[pltpu.SemaphoreType.DMA((side.n_sems,)), pltpu.SemaphoreType.DMA((side.n_sems,))],
        compiler_params=_cp(("arbitrary", "arbitrary", "arbitrary")),
    )(*args, *side.inputs)
    return outs[0], outs[1:]


def _ln_stats(z):
    mu = jnp.mean(z, -1, keepdims=True)
    zc = z - mu
    var = jnp.mean(zc * zc, -1, keepdims=True)
    rstd = lax.rsqrt(var + LN_EPS)
    return zc * rstd, rstd


def _ln_bwd_tile(z, gam, bet, other, from_target, dz_ref, dzb_ref, dg_ref, db_ref, loss_ref):
    i = pl.program_id(0)

    @pl.when(i == 0)
    def _():
        dg_ref[...] = jnp.zeros_like(dg_ref)
        db_ref[...] = jnp.zeros_like(db_ref)
        loss_ref[...] = jnp.zeros_like(loss_ref)

    xh, rstd = _ln_stats(z)
    if from_target:
        err = xh * gam + bet - other
        per_tok = jnp.mean(err * err, -1, keepdims=True)
        loss_ref[...] += 0.5 * jnp.sum(per_tok, 0, keepdims=True)
        dy = err * (1.0 / z.shape[-1])
    else:
        dy = other
    dxh = dy * gam
    m1 = jnp.mean(dxh, -1, keepdims=True)
    m2 = jnp.mean(dxh * xh, -1, keepdims=True)
    dz = rstd * (dxh - m1 - xh * m2)
    dz_ref[...] = dz
    dzb_ref[...] = dz.astype(BF16)
    dg_ref[...] += jnp.sum(dy * xh, 0, keepdims=True)
    db_ref[...] += jnp.sum(dy, 0, keepdims=True)


def _outproj(ymix, wo, x, g, b, *, tm, name, target=None):
    S_, D = x.shape
    last = target is not None

    def body(*refs):
        y_ref, w_ref, x_ref, g_ref, b_ref = refs[:5]
        z = DEEPNORM_ALPHA * x_ref[...] + _nn(y_ref[...], w_ref[...])
        if last:
            _ln_bwd_tile(z, g_ref[...], b_ref[...], refs[5][...], True, *refs[6:])
            return
        z_ref, o_ref, ob_ref = refs[5:]
        z_ref[...] = z
        xh, _ = _ln_stats(z)
        y = xh * g_ref[...] + b_ref[...]
        o_ref[...] = y
        ob_ref[...] = y.astype(BF16)

    row = pl.BlockSpec((tm, D), lambda i: (i, 0))
    vec = pl.BlockSpec((1, D), lambda i: (0, 0))
    one = pl.BlockSpec((1, 1), lambda i: (0, 0))
    in_specs = [pl.BlockSpec((tm, MIX_WIDTH), lambda i: (i, 0)), pl.BlockSpec((MIX_WIDTH, D), lambda i: (0, 0)),
                row, vec, vec]
    f32s, b16s = jax.ShapeDtypeStruct((S_, D), F32), jax.ShapeDtypeStruct((S_, D), BF16)
    v32s = jax.ShapeDtypeStruct((1, D), F32)
    args = (ymix, wo, x, g.reshape(1, D), b.reshape(1, D))
    if last:
        return _pcall(body, name=name, grid=(S_ // tm,), in_specs=in_specs + [row],
                      out_specs=[row, row, vec, vec, one],
                      out_shape=[f32s, b16s, v32s, v32s, jax.ShapeDtypeStruct((1, 1), F32)],
                      compiler_params=_cp(("arbitrary",)))(*args, target)
    return _pcall(body, name=name, grid=(S_ // tm,), in_specs=in_specs, out_specs=[row, row, row],
                  out_shape=[f32s, f32s, b16s], compiler_params=_cp(("parallel",)))(*args)


def _ln_bwd(z, g, b, dy, *, tm, name):
    S_, D = z.shape

    def body(z_ref, g_ref, b_ref, o_ref, *outs):
        _ln_bwd_tile(z_ref[...], g_ref[...], b_ref[...], o_ref[...], False, *outs)

    row = pl.BlockSpec((tm, D), lambda i: (i, 0))
    vec = pl.BlockSpec((1, D), lambda i: (0, 0))
    one = pl.BlockSpec((1, 1), lambda i: (0, 0))
    return _pcall(
        body, name=name, grid=(S_ // tm,), in_specs=[row, vec, vec, row],
        out_specs=[row, row, vec, vec, one],
        out_shape=[jax.ShapeDtypeStruct((S_, D), F32), jax.ShapeDtypeStruct((S_, D), BF16),
                   jax.ShapeDtypeStruct((1, D), F32), jax.ShapeDtypeStruct((1, D), F32),
                   jax.ShapeDtypeStruct((1, 1), F32)],
        compiler_params=_cp(("arbitrary",)),
    )(z, g.reshape(1, D), b.reshape(1, D), dy)


def _attn_masks(i, sk_ref):
    ri = lax.broadcasted_iota(jnp.int32, (512, 256), 0)
    cj = lax.broadcasted_iota(jnp.int32, (512, 256), 1)
    diff = (ri & 127) - cj + 128
    band = (diff >= 0) & (diff < WINDOW)
    bias = jnp.where(band, 0.0, -jnp.inf)
    bias0 = jnp.where(band & ((i > 0) | (cj >= 128)), 0.0, -jnp.inf)
    grp = lax.broadcasted_iota(jnp.int32, (512, 1), 0) >> 7
    skvs = []
    for h in range(A_KV_HEADS):
        skv = jnp.zeros((512, 1), F32)
        for g in range(4):
            skv = jnp.where(grp == g, sk_ref[h * 4 + g], skv)
        skvs.append(skv)
    return bias0, bias, skvs


def _attn_common(masks, b, h, qr, kd, vd):
    lane = lax.broadcasted_iota(jnp.int32, (1, 128), 1)
    lof = (lane < 64).astype(F32)
    hif = 1.0 - lof
    r0 = b * 128
    skv = masks[2][h]
    pairs = [qr[r0:r0 + 128, h * 256 + p * 128:h * 256 + (p + 1) * 128] for p in (0, 1)]
    qs = _b(jnp.concatenate([pairs[0] * lof, pairs[0] * hif, pairs[1] * lof, pairs[1] * hif], 0))
    k2 = kd[h][r0:r0 + 256]
    v2 = vd[h][r0:r0 + 256]
    s = _nt(qs, k2) * (A_HEAD_DIM ** -0.5) + (masks[0] if b == 0 else masks[1])
    m = jnp.maximum(jnp.max(s, 1, keepdims=True), skv)
    p = jnp.exp(s - m)
    esk = jnp.exp(skv - m)
    rz = 1.0 / (jnp.sum(p, 1, keepdims=True) + esk)
    prob = p * rz
    o4 = _nn(_b(prob), v2)
    return lof, hif, qs, k2, v2, prob, esk * rz, o4


def _attn_prep(T, q_ref, k_ref, v_ref, c_ref, s_ref, kprev, vprev):
    C = c_ref[...]
    Sg = s_ref[...]
    C4 = jnp.concatenate([C] * 4, 1)
    S4 = jnp.concatenate([Sg] * 4, 1)
    q = q_ref[...]
    qr = q * C4 + _rot(q) * S4
    k = k_ref[...]
    kr = k * C + _rot(k) * Sg
    v = v_ref[...]
    kext = jnp.concatenate([kprev[...], kr], 0)
    vext = jnp.concatenate([vprev[...], v], 0)
    kprev[...] = kr[T - 128:]
    vprev[...] = v[T - 128:]
    lo = lax.broadcasted_iota(jnp.int32, (1, 128), 1) < 64
    kroll = pltpu.roll(kext, 64, 1)
    vroll = pltpu.roll(vext, 64, 1)
    kd = [_b(jnp.where(lo, kext, kroll)), _b(jnp.where(lo, kroll, kext))]
    vd = [_b(jnp.where(lo, vext, vroll)), _b(jnp.where(lo, vroll, vext))]
    return C, Sg, C4, S4, qr, kd, vd


def _attn_specs(T):
    return [pl.BlockSpec(memory_space=pltpu.SMEM),
            pl.BlockSpec((T, 512), lambda i: (i, OFF_AQ // 512)),
            pl.BlockSpec((T, 512), lambda i: (i, OFF_AZ // 512)),
            pl.BlockSpec((T, 128), lambda i: (i, OFF_AK // 128)),
            pl.BlockSpec((T, 128), lambda i: (i, OFF_AV // 128)),
            pl.BlockSpec((T, 128), lambda i: (i, 0)),
            pl.BlockSpec((T, 128), lambda i: (i, 0))]


def _attn_fwd(proj, rope_c, rope_s, sinks, ymix, *, T, name):
    S_ = proj.shape[0]
    nb = T // 128

    def body(sk_ref, q_ref, z_ref, k_ref, v_ref, c_ref, s_ref, _, y_ref, kprev, vprev):
        i = pl.program_id(0)

        @pl.when(i == 0)
        def _():
            kprev[...] = jnp.zeros_like(kprev)
            vprev[...] = jnp.zeros_like(vprev)

        _, _, _, _, qr, kd, vd = _attn_prep(T, q_ref, k_ref, v_ref, c_ref, s_ref, kprev, vprev)
        masks = _attn_masks(i, sk_ref)
        for b in range(nb):
            r0 = b * 128
            for h in range(2):
                lof, hif, _, _, _, _, _, o4 = _attn_common(masks, b, h, qr, kd, vd)
                for p in range(2):
                    cs = slice(h * 256 + p * 128, h * 256 + (p + 1) * 128)
                    o = o4[2 * p * 128:(2 * p + 1) * 128] * lof + o4[(2 * p + 1) * 128:(2 * p + 2) * 128] * hif
                    y_ref[r0:r0 + 128, cs] = (o * _silu(z_ref[r0:r0 + 128, cs])).astype(BF16)

    return _pcall(
        body, name=name, grid=(S_ // T,), in_specs=_attn_specs(T) + [ANY_SPEC],
        out_specs=pl.BlockSpec((T, 512), lambda i: (i, MIX_A // 512)),
        out_shape=jax.ShapeDtypeStruct(ymix.shape, BF16),
        input_output_aliases={7: 0},
        scratch_shapes=[pltpu.VMEM((128, 128), F32), pltpu.VMEM((128, 128), F32)],
        compiler_params=_cp(("arbitrary",)),
    )(sinks, proj, proj, proj, proj, rope_c, rope_s, ymix)


def _attn_bwd(proj, rope_c, rope_s, sinks, dymix, *, T, name):
    S_ = proj.shape[0]
    nb = T // 128
    nt = S_ // T

    def body(sk_ref, q_ref, z_ref, k_ref, v_ref, c_ref, s_ref, dy_ref,
             dp_ref, dk_ref, dv_ref, dkt_ref, dvt_ref, dsk_ref,
             kprev, vprev, cprev, sprev, dkacc, dvacc, dqacc):
        i = pl.program_id(0)

        @pl.when(i == 0)
        def _():
            kprev[...] = jnp.zeros_like(kprev)
            vprev[...] = jnp.zeros_like(vprev)
            cprev[...] = jnp.zeros_like(cprev)
            sprev[...] = jnp.zeros_like(sprev)
            dkacc[...] = jnp.zeros_like(dkacc)
            dvacc[...] = jnp.zeros_like(dvacc)
            dsk_ref[...] = jnp.zeros_like(dsk_ref)

        @pl.when(i > 0)
        def _():
            dkacc[0:128, :] = dkacc[T:T + 128, :]
            dvacc[0:128, :] = dvacc[T:T + 128, :]
            dkacc[128:, :] = jnp.zeros((T, 128), F32)
            dvacc[128:, :] = jnp.zeros((T, 128), F32)

        C, Sg, C4, S4, qr, kd, vd = _attn_prep(T, q_ref, k_ref, v_ref, c_ref, s_ref, kprev, vprev)
        masks = _attn_masks(i, sk_ref)
        lane = lax.broadcasted_iota(jnp.int32, (1, 128), 1)
        for b in range(nb):
            r0 = b * 128
            for h in range(2):
                lof, hif, qs, k2, v2, prob, psink, o4 = _attn_common(masks, b, h, qr, kd, vd)
                dos = []
                for p in range(2):
                    cs = slice(h * 256 + p * 128, h * 256 + (p + 1) * 128)
                    o = o4[2 * p * 128:(2 * p + 1) * 128] * lof + o4[(2 * p + 1) * 128:(2 * p + 2) * 128] * hif
                    zc = z_ref[r0:r0 + 128, cs]
                    dyc = dy_ref[r0:r0 + 128, cs]
                    dp_ref[r0:r0 + 128, 512 + cs.start:512 + cs.stop] = (dyc * o * _dsilu(zc)).astype(BF16)
                    do = dyc * _silu(zc)
                    dos += [do * lof, do * hif]
                dos = jnp.concatenate(dos, 0)
                os_ = jnp.concatenate([o4[0:128] * lof, o4[128:256] * hif, o4[256:384] * lof, o4[384:512] * hif], 0)
                delta = jnp.sum(dos * os_, 1, keepdims=True)
                dosb = _b(dos)
                dp = _nt(dosb, v2)
                ds = prob * (dp - delta)
                dsv = -psink * delta
                for g in range(4):
                    sg = jnp.sum(dsv[g * 128:(g + 1) * 128], 0, keepdims=True)
                    hd = h * 4 + g
                    dsk_ref[hd:hd + 1, :] += jnp.broadcast_to(sg, (1, 128))
                dsb = _b(ds * (A_HEAD_DIM ** -0.5))
                dqs = _nn(dsb, k2)
                for p in range(2):
                    cs = slice(h * 256 + p * 128, h * 256 + (p + 1) * 128)
                    dqacc[r0:r0 + 128, cs] = (dqs[2 * p * 128:(2 * p + 1) * 128] * lof
                                              + dqs[(2 * p + 1) * 128:(2 * p + 2) * 128] * hif)
                dkdup = _tn(dsb, qs)
                dvdup = _tn(_b(prob), dosb)
                half = (lane < 64) if h == 0 else (lane >= 64)
                dkacc[r0:r0 + 256, :] += jnp.where(half, dkdup + pltpu.roll(dkdup, 64, 1), 0.0)
                dvacc[r0:r0 + 256, :] += jnp.where(half, dvdup + pltpu.roll(dvdup, 64, 1), 0.0)
        dqr = dqacc[...]
        dp_ref[:, 0:512] = (dqr * C4 + _rot(dqr * S4)).astype(BF16)
        cext = jnp.concatenate([cprev[...], C], 0)
        sext = jnp.concatenate([sprev[...], Sg], 0)
        dke = dkacc[...]
        dkp = dke * cext + _rot(dke * sext)
        dk_ref[...] = dkp[0:T].astype(BF16)
        dkt_ref[...] = dkp[T:T + 128].astype(BF16)
        dve = dvacc[...]
        dv_ref[...] = dve[0:T].astype(BF16)
        dvt_ref[...] = dve[T:T + 128].astype(BF16)
        cprev[...] = C[T - 128:]
        sprev[...] = Sg[T - 128:]

    nar = pl.BlockSpec((T, 128), lambda i: (i, 0))
    tail = pl.BlockSpec((128, 128), lambda i: (0, 0))
    return _pcall(
        body, name=name, grid=(nt,),
        in_specs=_attn_specs(T) + [pl.BlockSpec((T, 512), lambda i: (i, MIX_A // 512))],
        out_specs=[pl.BlockSpec((T, 1024), lambda i: (i, OFF_AQ // 1024)), nar, nar, tail, tail,
                   pl.BlockSpec((8, 128), lambda i: (0, 0))],
        out_shape=[jax.ShapeDtypeStruct((S_, NP), BF16),
                   jax.ShapeDtypeStruct((S_, 128), BF16), jax.ShapeDtypeStruct((S_, 128), BF16),
                   jax.ShapeDtypeStruct((128, 128), BF16), jax.ShapeDtypeStruct((128, 128), BF16),
                   jax.ShapeDtypeStruct((8, 128), F32)],
        scratch_shapes=[pltpu.VMEM((128, 128), F32)] * 4
        + [pltpu.VMEM((T + 128, 128), F32), pltpu.VMEM((T + 128, 128), F32), pltpu.VMEM((T, 512), F32)],
        compiler_params=_cp(("arbitrary",)),
    )(sinks, proj, proj, proj, proj, rope_c, rope_s, dymix)


def _rg_gates(xr, wa_ref, ba_ref, wx_ref, bx_ref, lam_ref):
    xb = _b(xr)
    pre_a = jnp.concatenate([_nn(xb[:, n * 128:(n + 1) * 128], wa_ref[n]) for n in range(R_BLOCKS)], 1) + ba_ref[...]
    pre_x = jnp.concatenate([_nn(xb[:, n * 128:(n + 1) * 128], wx_ref[n]) for n in range(R_BLOCKS)], 1) + bx_ref[...]
    r = _sigmoid(pre_a)
    ig = _sigmoid(pre_x)
    sp = _softplus(-lam_ref[...])
    log_a = -R_C * r * sp
    a = jnp.exp(log_a)
    mult = jnp.sqrt(_one_minus_exp(2.0 * log_a))
    return xb, r, ig, sp, a, mult


def _rg_param_specs():
    C = R_WIDTH
    vec = pl.BlockSpec((1, C), lambda i: (0, 0))
    blk = pl.BlockSpec((R_BLOCKS, 128, 128), lambda i: (0, 0, 0))
    return [pl.BlockSpec((CONV_WIDTH, C), lambda i: (0, 0)), vec, blk, vec, blk, vec, vec]


def _rglru_fwd(proj, cw, cb, wa, ba, wx, bx, lam, *, T, name):
    S_ = proj.shape[0]
    C = R_WIDTH

    def body(rx_ref, rz_ref, cw_ref, cb_ref, wa_ref, ba_ref, wx_ref, bx_ref, lam_ref,
             h_ref, y_ref, halo, hcar):
        i = pl.program_id(0)

        @pl.when(i == 0)
        def _():
            halo[...] = jnp.zeros_like(halo)
            hcar[...] = jnp.zeros_like(hcar)

        rx = rx_ref[...]
        ext = jnp.concatenate([halo[...], rx], 0)
        halo[...] = rx[T - 8:]
        taps = _conv_taps(ext, T)
        xr = cb_ref[...] + sum(cw_ref[k:k + 1, :] * taps[k] for k in range(CONV_WIDTH))
        _, _, ig, _, a, mult = _rg_gates(xr, wa_ref, ba_ref, wx_ref, bx_ref, lam_ref)
        u = mult * (ig * xr)
        acum, hloc = _scan_lin(a, u, False)
        h = hloc + acum * hcar[0:1, :]
        hcar[...] = jnp.broadcast_to(h[T - 1:T, :], (8, C))
        h_ref[...] = h
        y_ref[...] = (h * _silu(rz_ref[...])).astype(BF16)

    row = pl.BlockSpec((T, C), lambda i: (i, 0))
    return _pcall(
        body, name=name, grid=(S_ // T,),
        in_specs=[pl.BlockSpec((T, C), lambda i: (i, OFF_RX // C)),
                  pl.BlockSpec((T, C), lambda i: (i, OFF_RZ // C))] + _rg_param_specs(),
        out_specs=[row, pl.BlockSpec((T, C), lambda i: (i, MIX_R // C))],
        out_shape=[jax.ShapeDtypeStruct((S_, C), F32), jax.ShapeDtypeStruct((S_, MIX_WIDTH), BF16)],
        scratch_shapes=[pltpu.VMEM((8, C), F32), pltpu.VMEM((8, C), F32)],
        compiler_params=_cp(("arbitrary",)),
    )(proj, proj, cw, cb.reshape(1, C), _b(wa), ba.reshape(1, C), _b(wx), bx.reshape(1, C), lam.reshape(1, C))


def _rglru_bwd(proj, h, dymix, dproj, cw, cb, wa, ba, wx, bx, lam, *, T, name):
    S_ = proj.shape[0]
    C = R_WIDTH
    nt = S_ // T
    t8 = T // 8

    def body(rx_ref, rxp_ref, rz_ref, h_ref, hp_ref, dy_ref,
             cw_ref, cb_ref, wa_ref, ba_ref, wx_ref, bx_ref, lam_ref, wat_ref, wxt_ref,
             _, dp_ref, dcw_ref, dcb_ref, dwa_ref, dba_ref, dwx_ref, dbx_ref, dlam_ref,
             afirst, gfirst, dhalo):
        i = pl.program_id(0)
        first_tile = (i == nt - 1)

        @pl.when(i == 0)
        def _():
            afirst[...] = jnp.zeros_like(afirst)
            gfirst[...] = jnp.zeros_like(gfirst)
            dhalo[...] = jnp.zeros_like(dhalo)
            for r in (dcw_ref, dcb_ref, dwa_ref, dba_ref, dwx_ref, dbx_ref, dlam_ref):
                r[...] = jnp.zeros_like(r)

        keep = jnp.where(first_tile, 0.0, 1.0)
        rx = rx_ref[...]
        ext = jnp.concatenate([rxp_ref[...] * keep, rx], 0)
        taps = _conv_taps(ext, T)
        xr = cb_ref[...] + sum(cw_ref[k:k + 1, :] * taps[k] for k in range(CONV_WIDTH))
        xb, r, ig, sp, a, mult = _rg_gates(xr, wa_ref, ba_ref, wx_ref, bx_ref, lam_ref)
        hh = h_ref[...]
        rz = rz_ref[...]
        dy = dy_ref[...]
        dp_ref[:, C:2 * C] = (dy * hh * _dsilu(rz)).astype(BF16)
        dh = dy * _silu(rz)
        row = lax.broadcasted_iota(jnp.int32, (T, 1), 0)
        c = jnp.where(row == T - 1, afirst[0:1, :], pltpu.roll(a, T - 1, 0))
        ccum, gloc = _scan_lin(c, dh, True)
        g = gloc + ccum * gfirst[0:1, :]
        afirst[...] = jnp.broadcast_to(a[0:1, :], (8, C))
        gfirst[...] = jnp.broadcast_to(g[0:1, :], (8, C))
        hprev = jnp.where(row == 0, hp_ref[7:8, :] * keep, pltpu.roll(hh, 1, 0))
        da = g * hprev
        gx = ig * xr
        dgx = g * mult
        dmult = g * gx
        dlog_a = da * a - dmult * (a * a) * lax.rsqrt(mult * mult)
        dpre_a = dlog_a * (-R_C * sp) * r * (1.0 - r)
        dpre_x = dgx * xr * ig * (1.0 - ig)
        dlam_ref[...] += jnp.sum(dlog_a * (-R_C * r), 0, keepdims=True) * (-_sigmoid(-lam_ref[...]))
        dab = _b(dpre_a)
        dxb = _b(dpre_x)
        dxr = dgx * ig + jnp.concatenate(
            [_nn(dab[:, n * 128:(n + 1) * 128], wat_ref[n]) + _nn(dxb[:, n * 128:(n + 1) * 128], wxt_ref[n])
             for n in range(R_BLOCKS)], 1)
        for n in range(R_BLOCKS):
            cs = slice(n * 128, (n + 1) * 128)
            dwa_ref[n] += _tn(xb[:, cs], dab[:, cs])
            dwx_ref[n] += _tn(xb[:, cs], dxb[:, cs])
        dba_ref[...] += jnp.sum(dpre_a, 0, keepdims=True)
        dbx_ref[...] += jnp.sum(dpre_x, 0, keepdims=True)
        dcb_ref[...] += jnp.sum(dxr, 0, keepdims=True)
        for k in range(CONV_WIDTH):
            dcw_ref[k:k + 1, :] += jnp.sum(dxr * taps[k], 0, keepdims=True)
        ext2 = jnp.concatenate([dxr, dhalo[...]], 0)
        tt = _conv_taps_t(ext2, T)
        dp_ref[:, 0:C] = sum(cw_ref[k:k + 1, :] * tt[k] for k in range(CONV_WIDTH)).astype(BF16)
        dhalo[...] = dxr[0:8]

    def rev(i):
        return nt - 1 - i

    def prev8(i):
        return jnp.maximum(rev(i) * t8 - 1, 0)

    vec = pl.BlockSpec((1, C), lambda i: (0, 0))
    blk = pl.BlockSpec((R_BLOCKS, 128, 128), lambda i: (0, 0, 0))
    row = pl.BlockSpec((T, C), lambda i: (rev(i), 0))
    wat = _b(jnp.swapaxes(wa, 1, 2))
    wxt = _b(jnp.swapaxes(wx, 1, 2))
    return _pcall(
        body, name=name, grid=(nt,),
        in_specs=[pl.BlockSpec((T, C), lambda i: (rev(i), OFF_RX // C)),
                  pl.BlockSpec((8, C), lambda i: (prev8(i), OFF_RX // C)),
                  pl.BlockSpec((T, C), lambda i: (rev(i), OFF_RZ // C)),
                  row,
                  pl.BlockSpec((8, C), lambda i: (prev8(i), 0)),
                  pl.BlockSpec((T, C), lambda i: (rev(i), MIX_R // C)),
                  ] + _rg_param_specs() + [blk, blk, ANY_SPEC],
        out_specs=[pl.BlockSpec((T, 2 * C), lambda i: (rev(i), OFF_RX // (2 * C))),
                   pl.BlockSpec((CONV_WIDTH, C), lambda i: (0, 0)), vec, blk, vec, blk, vec, vec],
        out_shape=[jax.ShapeDtypeStruct(dproj.shape, BF16),
                   jax.ShapeDtypeStruct((CONV_WIDTH, C), F32), jax.ShapeDtypeStruct((1, C), F32),
                   jax.ShapeDtypeStruct((R_BLOCKS, 128, 128), F32), jax.ShapeDtypeStruct((1, C), F32),
                   jax.ShapeDtypeStruct((R_BLOCKS, 128, 128), F32), jax.ShapeDtypeStruct((1, C), F32),
                   jax.ShapeDtypeStruct((1, C), F32)],
        input_output_aliases={15: 0},
        scratch_shapes=[pltpu.VMEM((8, C), F32)] * 3,
        compiler_params=_cp(("arbitrary",)),
    )(proj, proj, proj, h, h, dymix, cw, cb.reshape(1, C), _b(wa), ba.reshape(1, C), _b(wx), bx.reshape(1, C),
      lam.reshape(1, C), wat, wxt, dproj)


GW3 = 3 * G_WIDTH


def _lane_col(x, lane_idx):
    lane = lax.broadcasted_iota(jnp.int32, (1, x.shape[1]), 1)
    return jnp.sum(jnp.where(lane == lane_idx, x, 0.0), 1, keepdims=True)


def _gdn_pre(ext, T, cw_ref, gba, pv_ref):
    taps = _conv_taps(ext, T)
    c = sum(cw_ref[k:k + 1, :] * taps[k] for k in range(CONV_WIDTH))
    qkv = _silu(c)
    beta = _sigmoid(gba)
    sarg = gba + pv_ref[1:2, :]
    nea = -jnp.exp(pv_ref[0:1, :])
    gdec = nea * _softplus(sarg)
    ri = lax.broadcasted_iota(jnp.int32, (T, T), 0)
    cj = lax.broadcasted_iota(jnp.int32, (T, T), 1)
    same = (ri >> 6) == (cj >> 6)
    ltri = jnp.where((ri >= cj) & same, 1.0, 0.0).astype(BF16)
    gc = _dot_exact_lhs(_nn, ltri, gdec)
    return taps, c, qkv, beta, sarg, nea, gdec, gc


def _gdn_masks():
    ri = lax.broadcasted_iota(jnp.int32, (128, 128), 0)
    cj = lax.broadcasted_iota(jnp.int32, (128, 128), 1)
    same = (ri >> 6) == (cj >> 6)
    return (ri >= cj) & same, (ri > cj) & same, ri == cj


def _lockstep(gens):
    out = [None] * len(gens)
    live = list(range(len(gens)))
    while live:
        still = []
        for k in live:
            try:
                next(gens[k])
                still.append(k)
            except StopIteration as stop:
                out[k] = stop.value
        live = still
    return out


def _gdn_chunk(qkv, beta, gc, rs, h, tm=None):
    tril, strict, eye = _gdn_masks()
    rowi = lax.broadcasted_iota(jnp.int32, (128, 1), 0)
    lane = lax.broadcasted_iota(jnp.int32, (1, 128), 1)
    qh = qkv[rs, h * 128:(h + 1) * 128]
    kh = qkv[rs, 512 + h * 128:512 + (h + 1) * 128]
    vh = qkv[rs, 1024 + h * 128:1024 + (h + 1) * 128]
    rq = lax.rsqrt(jnp.sum(qh * qh, 1, keepdims=True) + RMS_EPS)
    rk = lax.rsqrt(jnp.sum(kh * kh, 1, keepdims=True) + RMS_EPS)
    qn = qh * (rq * (G_HEAD_DIM ** -0.5))
    kn = kh * rk
    gcb = gc[rs]
    gcol = _lane_col(gcb, 4 + h)
    bcol = _lane_col(beta[rs], h)
    grow = _dot_exact_lhs(_nt, jnp.ones((128, 128), BF16), jnp.where(lane == 4 + h, gcb, 0.0))
    D = jnp.where(tril, jnp.exp(jnp.minimum(gcol - grow, 0.0)), 0.0)
    kb = kn * bcol
    vb = vh * bcol
    knb = _b(kn)
    A = _nt(_b(kb), knb)
    Bm = _nt(_b(qn), knb)
    yield
    if tm is None:
        N = jnp.where(strict, -(A * D), 0.0)
        tm = jnp.where(eye, 1.0, 0.0) + N
        npow = N
        for _ in range(5):
            npow = _dot3(_nn, npow, npow)
            yield
            tm = tm + _dot3(_nn, tm, npow)
            yield
    eg = jnp.exp(gcol)
    u = _dot3(_nn, tm, vb)
    w = _dot3(_nn, tm, kb * eg)
    yield
    qk = jnp.where(tril, Bm * D, 0.0)
    qd = qn * eg
    gla = jnp.sum(jnp.where(rowi == 63, gcol, 0.0), 0, keepdims=True)
    glb = jnp.sum(jnp.where(rowi == 127, gcol, 0.0), 0, keepdims=True)
    ed = jnp.exp(jnp.where(rowi < 64, gla, glb) - gcol)
    kd = kn * ed
    return dict(qh=qh, kh=kh, vh=vh, rq=rq, rk=rk, qn=qn, kn=kn, gcol=gcol, bcol=bcol, D=D, A=A, Bm=Bm,
                tm=tm, eg=eg, ed=ed, u=u, w=w, qk=qk, qd=qd, kd=kd, kb=kb, vb=vb,
                gla=jnp.exp(gla), glb=jnp.exp(glb))


def _gdn_scan(q, sa):
    sab = _b(sa)
    wb = _b(q["w"])
    vna = q["u"] - _nn(wb, sab)
    yield
    sb = sa * q["gla"] + _tn(_b(q["kd"][0:64]), _b(vna[0:64]))
    yield
    sbb = _b(sb)
    vnb = q["u"] - _nn(wb, sbb)
    yield
    sn = sb * q["glb"] + _tn(_b(q["kd"][64:128]), _b(vnb[64:128]))
    yield
    vn = jnp.concatenate([vna[0:64], vnb[64:128]], 0)
    qdb = _b(q["qd"])
    o = jnp.concatenate([_nn(qdb[0:64], sab), _nn(qdb[64:128], sbb)], 0) + _nn(_b(q["qk"]), _b(vn))
    return sb, sn, vn, o


def _gdn_param_specs():
    return [pl.BlockSpec((CONV_WIDTH, GW3), lambda i: (0, 0)),
            pl.BlockSpec((8, 128), lambda i: (0, 0)),
            pl.BlockSpec((1, 128), lambda i: (0, 0))]


def _gdn_pvec(a_log, dt_bias):
    z = jnp.zeros((8, 128), F32)
    return z.at[0, 4:8].set(a_log).at[1, 4:8].set(dt_bias)


def _gdn_fwd(proj, cw, a_log, dt_bias, nw, ymix, *, T, name):
    S_ = proj.shape[0]
    nu = T // 128

    def body(x_ref, z_ref, g_ref, cw_ref, pv_ref, nw_ref, _, y_ref, st_ref, tm_ref, halo, state):
        i = pl.program_id(0)

        @pl.when(i == 0)
        def _():
            halo[...] = jnp.zeros_like(halo)
            state[...] = jnp.zeros_like(state)

        x = x_ref[...]
        ext = jnp.concatenate([halo[...], x], 0)
        halo[...] = x[T - 8:]
        _, _, qkv, beta, _, _, _, gc = _gdn_pre(ext, T, cw_ref, g_ref[...], pv_ref)
        items = [(dc, h) for dc in range(nu) for h in range(G_HEADS)]
        qs = _lockstep([_gdn_chunk(qkv, beta, gc, slice(dc * 128, (dc + 1) * 128), h) for dc, h in items])

        def head_chain(h):
            s = state[h]
            for dc in range(nu):
                rs = slice(dc * 128, (dc + 1) * 128)
                q = qs[dc * G_HEADS + h]
                sb, sn, _, o = yield from _gdn_scan(q, s)
                st_ref[2 * dc, h] = s
                st_ref[2 * dc + 1, h] = sb
                tm_ref[dc, h] = q["tm"]
                s = sn
                yield
                rn = lax.rsqrt(jnp.mean(o * o, 1, keepdims=True) + RMS_EPS)
                cs = slice(h * 128, (h + 1) * 128)
                y_ref[rs, cs] = (o * rn * nw_ref[...] * _silu(z_ref[rs, cs])).astype(BF16)
                yield
            state[h] = s

        _lockstep([head_chain(h) for h in range(G_HEADS)])

    return _pcall(
        body, name=name, grid=(S_ // T,),
        in_specs=[pl.BlockSpec((T, GW3), lambda i: (i, OFF_GQKV // GW3)),
                  pl.BlockSpec((T, 512), lambda i: (i, OFF_GZ // 512)),
                  pl.BlockSpec((T, 128), lambda i: (i, OFF_GBA // 128))] + _gdn_param_specs() + [ANY_SPEC],
        out_specs=[pl.BlockSpec((T, 512), lambda i: (i, MIX_G // 512)),
                   pl.BlockSpec((2 * nu, G_HEADS, 128, 128), lambda i: (i, 0, 0, 0)),
                   pl.BlockSpec((nu, G_HEADS, 128, 128), lambda i: (i, 0, 0, 0))],
        out_shape=[jax.ShapeDtypeStruct(ymix.shape, BF16),
                   jax.ShapeDtypeStruct((S_ // 64, G_HEADS, 128, 128), F32),
                   jax.ShapeDtypeStruct((S_ // 128, G_HEADS, 128, 128), F32)],
        input_output_aliases={6: 0},
        scratch_shapes=[pltpu.VMEM((8, GW3), F32), pltpu.VMEM((G_HEADS, 128, 128), F32)],
        compiler_params=_cp(("arbitrary",)),
    )(proj, proj, proj, cw, _gdn_pvec(a_log, dt_bias), nw.reshape(1, 128), ymix)


def _gdn_bwd(proj, states, tms, dymix, dproj, cw, a_log, dt_bias, nw, *, T, name):
    S_ = proj.shape[0]
    nt = S_ // T
    nu = T // 128
    t8 = T // 8

    def body(x_ref, xp_ref, z_ref, g_ref, st_ref, tm_ref, dy_ref, cw_ref, pv_ref, nw_ref, _,
             dp_ref, dg_ref, dcw_ref, dpv_ref, dnw_ref, dstate, dhalo, dqkv, dbg):
        i = pl.program_id(0)
        first_tile = (i == nt - 1)

        @pl.when(i == 0)
        def _():
            dstate[...] = jnp.zeros_like(dstate)
            dhalo[...] = jnp.zeros_like(dhalo)
            dcw_ref[...] = jnp.zeros_like(dcw_ref)
            dpv_ref[...] = jnp.zeros_like(dpv_ref)
            dnw_ref[...] = jnp.zeros_like(dnw_ref)

        keep = jnp.where(first_tile, 0.0, 1.0)
        ext = jnp.concatenate([xp_ref[...] * keep, x_ref[...]], 0)
        G = g_ref[...]
        taps, c, qkv, beta, sarg, nea, gdec, gc = _gdn_pre(ext, T, cw_ref, G, pv_ref)
        tril, strict, _ = _gdn_masks()
        rowi = lax.broadcasted_iota(jnp.int32, (128, 1), 0)
        lane = lax.broadcasted_iota(jnp.int32, (1, 128), 1)
        ones_b = jnp.ones((128, 128), BF16)
        nwv = nw_ref[...]
        items = [(dc, h) for dc in range(nu) for h in range(G_HEADS)]

        def recompute(dc, h):
            q = yield from _gdn_chunk(qkv, beta, gc, slice(dc * 128, (dc + 1) * 128), h, tm=tm_ref[dc, h])
            sa = st_ref[2 * dc, h]
            sb, _, vn, o = yield from _gdn_scan(q, sa)
            return q, sa, sb, vn, o

        fw = _lockstep([recompute(dc, h) for dc, h in items])
        chain_out = {}

        def head_chain(h):
            dS = dstate[h]
            for dc in reversed(range(nu)):
                rs = slice(dc * 128, (dc + 1) * 128)
                q, sa, sb, vn, o = fw[dc * G_HEADS + h]
                cs = slice(h * 128, (h + 1) * 128)
                zg = z_ref[rs, cs]
                dy = dy_ref[rs, cs]
                rn = lax.rsqrt(jnp.mean(o * o, 1, keepdims=True) + RMS_EPS)
                don = dy * _silu(zg)
                dp_ref[rs, GW3 + cs.start:GW3 + cs.stop] = (dy * (o * rn * nwv) * _dsilu(zg)).astype(BF16)
                dnw_ref[...] += jnp.sum(don * o * rn, 0, keepdims=True)
                tt = don * nwv
                do = rn * (tt - o * (rn * rn) * jnp.mean(tt * o, 1, keepdims=True))
                yield
                dob = _b(do)
                sab, sbb = _b(sa), _b(sb)
                vnb16 = _b(vn)
                dqk = jnp.where(tril, _nt(dob, vnb16), 0.0)
                dvn_o = _tn(_b(q["qk"]), dob)
                dSb16 = _b(dS)
                kdb = _b(q["kd"])
                wb = _b(q["w"])
                qdb = _b(q["qd"])
                yield
                dvn_b = dvn_o[64:128] + _nn(kdb[64:128], dSb16)
                dkd_b = _nt(vnb16[64:128], dSb16)
                dgl_b = jnp.sum(jnp.sum(dS * sb, 1, keepdims=True), 0, keepdims=True)
                yield
                dvn_b16 = _b(dvn_b)
                dw_b = -_nt(dvn_b16, sbb)
                dqd_b = _nt(dob[64:128], sbb)
                dSm = q["glb"] * dS + _tn(qdb[64:128], dob[64:128]) - _tn(wb[64:128], dvn_b16)
                yield
                dSm16 = _b(dSm)
                dvn_a = dvn_o[0:64] + _nn(kdb[0:64], dSm16)
                dkd_a = _nt(vnb16[0:64], dSm16)
                dgl_a = jnp.sum(jnp.sum(dSm * sa, 1, keepdims=True), 0, keepdims=True)
                yield
                dvn_a16 = _b(dvn_a)
                dw_a = -_nt(dvn_a16, sab)
                dqd_a = _nt(dob[0:64], sab)
                dS = q["gla"] * dSm + _tn(qdb[0:64], dob[0:64]) - _tn(wb[0:64], dvn_a16)
                chain_out[dc, h] = (dqk, jnp.concatenate([dvn_a, dvn_b], 0), jnp.concatenate([dw_a, dw_b], 0),
                                    jnp.concatenate([dkd_a, dkd_b], 0), jnp.concatenate([dqd_a, dqd_b], 0),
                                    dgl_a, dgl_b)
                yield
            dstate[h] = dS

        _lockstep([head_chain(h) for h in range(G_HEADS)])

        def local(dc, h):
            rs = slice(dc * 128, (dc + 1) * 128)
            q = fw[dc * G_HEADS + h][0]
            dqk, du, dw, dkd, dqd, dgl_a, dgl_b = chain_out[dc, h]
            if True:
                dvb = _dot3(_tn, q["tm"], du)
                dkbe = _dot3(_tn, q["tm"], dw)
                yield
                dM = jnp.where(strict, -(_nt(_b(dvb), _b(q["u"])) + _nt(_b(dkbe), _b(q["w"]))), 0.0)
                yield
                D = q["D"]
                dA = dM * D
                dB = dqk * D
                dDD = (dM * q["A"] + dqk * q["Bm"]) * D
                dh_, dm_, dl_ = _split3(dDD)
                colsum = _tn(dh_, ones_b) + (_tn(dm_, ones_b) + _tn(dl_, ones_b))
                dgc = jnp.sum(dDD, 1, keepdims=True) - _lane_col(colsum, 0)
                yield
                dA16, dB16 = _b(dA), _b(dB)
                knb, kbb, qnb = _b(q["kn"]), _b(q["kb"]), _b(q["qn"])
                eg, ed = q["eg"], q["ed"]
                dkb = _nn(dA16, knb) + dkbe * eg
                dkn = _tn(dA16, kbb) + _tn(dB16, qnb) + dkd * ed + dkb * q["bcol"]
                dqn = _nn(dB16, knb) + dqd * eg
                yield
                deg = jnp.sum(dkbe * q["kb"], 1, keepdims=True) + jnp.sum(dqd * q["qn"], 1, keepdims=True)
                ded = jnp.sum(dkd * q["kn"], 1, keepdims=True) * ed
                dgc = dgc + deg * eg - ded
                tail_a = jnp.sum(jnp.where(rowi < 64, ded, 0.0), 0, keepdims=True) + dgl_a * q["gla"]
                tail_b = jnp.sum(jnp.where(rowi >= 64, ded, 0.0), 0, keepdims=True) + dgl_b * q["glb"]
                dgc = dgc + jnp.where(rowi == 63, tail_a, 0.0) + jnp.where(rowi == 127, tail_b, 0.0)
                dbeta = jnp.sum(dkb * q["kn"], 1, keepdims=True) + jnp.sum(dvb * q["vh"], 1, keepdims=True)
                bcol = q["bcol"]
                blk = jnp.where(lane == h, dbeta * bcol * (1.0 - bcol), 0.0) + jnp.where(lane == 4 + h, dgc, 0.0)
                yield
                sc = G_HEAD_DIM ** -0.5
                rq, rk, qh, kh = q["rq"], q["rk"], q["qh"], q["kh"]
                dqh = sc * (dqn * rq - qh * (rq * rq * rq) * jnp.sum(dqn * qh, 1, keepdims=True))
                dkh = dkn * rk - kh * (rk * rk * rk) * jnp.sum(dkn * kh, 1, keepdims=True)
                dqkv[rs, h * 128:(h + 1) * 128] = dqh
                dqkv[rs, 512 + h * 128:512 + (h + 1) * 128] = dkh
                dqkv[rs, 1024 + h * 128:1024 + (h + 1) * 128] = dvb * bcol
            return blk

        blks = _lockstep([local(dc, h) for dc, h in items])
        for dc in range(nu):
            dbg[dc * 128:(dc + 1) * 128, :] = functools.reduce(
                lambda a, b: a + b, [blks[dc * G_HEADS + h] for h in range(G_HEADS)])
        ri = lax.broadcasted_iota(jnp.int32, (T, T), 0)
        cj = lax.broadcasted_iota(jnp.int32, (T, T), 1)
        utri = jnp.where((ri <= cj) & ((ri >> 6) == (cj >> 6)), 1.0, 0.0).astype(BF16)
        dbgv = dbg[...]
        dgd = _dot_exact_lhs(_nn, utri, dbgv)
        is_g = (lane >= 4) & (lane < 8)
        dga = jnp.where(is_g, dgd * nea * _sigmoid(sarg), 0.0)
        dg_ref[...] = jnp.where(lane < 4, dbgv, dga).astype(BF16)
        dpv_ref[0:1, :] += jnp.sum(jnp.where(is_g, dgd * gdec, 0.0), 0, keepdims=True)
        dpv_ref[1:2, :] += jnp.sum(dga, 0, keepdims=True)
        dc_ = dqkv[...] * _dsilu(c)
        for k in range(CONV_WIDTH):
            dcw_ref[k:k + 1, :] += jnp.sum(dc_ * taps[k], 0, keepdims=True)
        ext2 = jnp.concatenate([dc_, dhalo[...]], 0)
        tt2 = _conv_taps_t(ext2, T)
        dp_ref[:, 0:GW3] = sum(cw_ref[k:k + 1, :] * tt2[k] for k in range(CONV_WIDTH)).astype(BF16)
        dhalo[...] = dc_[0:8]

    def rev(i):
        return nt - 1 - i

    def prev8(i):
        return jnp.maximum(rev(i) * t8 - 1, 0)

    return _pcall(
        body, name=name, grid=(nt,),
        in_specs=[pl.BlockSpec((T, GW3), lambda i: (rev(i), OFF_GQKV // GW3)),
                  pl.BlockSpec((8, GW3), lambda i: (prev8(i), OFF_GQKV // GW3)),
                  pl.BlockSpec((T, 512), lambda i: (rev(i), OFF_GZ // 512)),
                  pl.BlockSpec((T, 128), lambda i: (rev(i), OFF_GBA // 128)),
                  pl.BlockSpec((2 * nu, G_HEADS, 128, 128), lambda i: (rev(i), 0, 0, 0)),
                  pl.BlockSpec((nu, G_HEADS, 128, 128), lambda i: (rev(i), 0, 0, 0)),
                  pl.BlockSpec((T, 512), lambda i: (rev(i), MIX_G // 512))] + _gdn_param_specs() + [ANY_SPEC],
        out_specs=[pl.BlockSpec((T, GW3 + 512), lambda i: (rev(i), OFF_GQKV // (GW3 + 512))),
                   pl.BlockSpec((T, 128), lambda i: (rev(i), 0)),
                   pl.BlockSpec((CONV_WIDTH, GW3), lambda i: (0, 0)),
                   pl.BlockSpec((8, 128), lambda i: (0, 0)),
                   pl.BlockSpec((1, 128), lambda i: (0, 0))],
        out_shape=[jax.ShapeDtypeStruct(dproj.shape, BF16),
                   jax.ShapeDtypeStruct((S_, 128), BF16), jax.ShapeDtypeStruct((CONV_WIDTH, GW3), F32),
                   jax.ShapeDtypeStruct((8, 128), F32), jax.ShapeDtypeStruct((1, 128), F32)],
        input_output_aliases={10: 0},
        scratch_shapes=[pltpu.VMEM((G_HEADS, 128, 128), F32), pltpu.VMEM((8, GW3), F32),
                        pltpu.VMEM((T, GW3), F32), pltpu.VMEM((T, 128), F32)],
        compiler_params=_cp(("arbitrary",)),
    )(proj, proj, proj, proj, states, tms, dymix, cw, _gdn_pvec(a_log, dt_bias), nw.reshape(1, 128), dproj)


def _pair_sum_windows(a, b, nsh, width, *, out_dtype, name):
    R_, C = a.shape
    hr = R_ // 2
    nb = width // 128
    assert (3 * nsh) // 128 + nb <= C // 128
    to_perm = _orig_block_to_perm()
    blocks = jnp.asarray([to_perm[(nsh * t) // 128 + j] for t in range(4) for j in range(nb)], jnp.int32)
    table = jnp.concatenate([blocks, lax.axis_index("c").astype(jnp.int32)[None]])

    def body(tab_ref, a_ref, b_ref, o_ref):
        o_ref[...] = (a_ref[...] + b_ref[...]).astype(o_ref.dtype)

    return _pcall(
        body, name=name,
        grid_spec=pltpu.PrefetchScalarGridSpec(
            num_scalar_prefetch=1, grid=(4, nb),
            in_specs=[pl.BlockSpec((hr, 128), lambda t, j, tab: (tab[4 * nb], tab[t * nb + j])),
                      pl.BlockSpec((hr, 128), lambda t, j, tab: (0, tab[t * nb + j]))],
            out_specs=pl.BlockSpec((None, hr, 128), lambda t, j, tab: (t, 0, j))),
        out_shape=jax.ShapeDtypeStruct((4, hr, width), out_dtype),
        compiler_params=_cp(("parallel", "parallel")))(table, a, b)


def _pair_sum_blocks(a, b, *, out_dtype, name):
    L, R_, C = a.shape
    hr = R_ // 2

    def body(a0_ref, a1_ref, b_ref, o_ref):
        mine = jnp.where(lax.axis_index("c") == 0, a0_ref[...], a1_ref[...])
        o_ref[...] = (mine + b_ref[...]).astype(o_ref.dtype)

    def spec(half):
        return pl.BlockSpec((None, hr, C), lambda t: (t, half, 0))

    return _pcall(body, name=name, grid=(L,), in_specs=[spec(0), spec(1), spec(0)], out_specs=spec(0),
                  out_shape=jax.ShapeDtypeStruct((L, hr, C), out_dtype),
                  compiler_params=_cp(("parallel",)))(a, a, b)


def _sum4(a, mine, *, tr, name):
    _, R_, C = a.shape

    def body(a_ref, m_ref, o_ref):
        s = 2 * lax.axis_index("x") + lax.axis_index("y")
        mv = m_ref[...].astype(F32)
        p = [jnp.where(s == t, mv, a_ref[t].astype(F32)) for t in range(4)]
        o_ref[...] = ((p[0] + p[1]) + p[2]) + p[3]

    return _pcall(body, name=name, grid=(R_ // tr,),
                  in_specs=[pl.BlockSpec((4, tr, C), lambda i: (0, i, 0)), pl.BlockSpec((tr, C), lambda i: (i, 0))],
                  out_specs=pl.BlockSpec((tr, C), lambda i: (i, 0)),
                  out_shape=jax.ShapeDtypeStruct((R_, C), F32), compiler_params=_cp(("parallel",)))(a, mine)


def _adamw_refs(w_ref, g_ref, m_ref, v_ref, d_ref, mo_ref, vo_ref):
    c1 = 1.0 / (1.0 - ADAM_B1 ** ADAM_STEP)
    c2 = 1.0 / (1.0 - ADAM_B2 ** ADAM_STEP)
    gg = g_ref[...]
    mn = ADAM_B1 * m_ref[...] + (1.0 - ADAM_B1) * gg
    vn = ADAM_B2 * v_ref[...] + (1.0 - ADAM_B2) * (gg * gg)
    mo_ref[...] = mn
    vo_ref[...] = vn
    d_ref[...] = -ADAM_LR * ((mn * c1) / (jnp.sqrt(vn * c2) + ADAM_EPS) + ADAM_WD * w_ref[...])


def _adamw_many(ws, gs, ms, vs, *, name):
    n = len(ws)

    def body(*refs):
        for k in range(n):
            _adamw_refs(*[refs[q * n + k] for q in range(7)])

    vm = pl.BlockSpec(memory_space=pltpu.VMEM)
    shp = [jax.ShapeDtypeStruct(w.shape, F32) for w in ws]
    outs = _pcall(body, name=name, in_specs=[vm] * (4 * n), out_specs=[vm] * (3 * n), out_shape=shp * 3,
                  compiler_params=pltpu.CompilerParams(vmem_limit_bytes=VMEM_LIMIT))(*ws, *gs, *ms, *vs)
    return outs[:n], outs[n:2 * n], outs[2 * n:]


def _adamw(w, g, m, v, *, tr, name):
    L, R_, C = w.shape
    body = functools.partial(_adamw_refs)

    spec = pl.BlockSpec((None, tr, C), lambda l, i: (l, i, 0))
    shp = jax.ShapeDtypeStruct((L, R_, C), F32)
    return _pcall(body, name=name, grid=(L, R_ // tr), in_specs=[spec] * 4, out_specs=[spec] * 3,
                  out_shape=[shp] * 3, compiler_params=_cp(("parallel", "parallel")))(w, g, m, v)


def _adamw_cols(w, g, m, v, *, name):
    C, L, R_ = w.shape
    tc = C // 2 if C % 2 == 0 else C

    spec = pl.BlockSpec((tc, L, 128), lambda i, j: (i, 0, j))
    shp = jax.ShapeDtypeStruct((C, L, R_), F32)
    return _pcall(functools.partial(_adamw_refs), name=name, grid=(C // tc, R_ // 128), in_specs=[spec] * 4,
                  out_specs=[spec] * 3, out_shape=[shp] * 3,
                  compiler_params=_cp(("parallel", "parallel")))(w, g, m, v)


HBM_SPEC = pl.BlockSpec(memory_space=pltpu.HBM)


def _place():
    x, y, c = lax.axis_index("x"), lax.axis_index("y"), lax.axis_index("c")
    chips = [(1 - x, y), (x, 1 - y), (1 - x, 1 - y)]
    return x, y, c, 2 * x + y, chips, [2 * cx + cy for cx, cy in chips], (x, y, 1 - c)


def _remote(src, dst, ssem, rsem, dev):
    return pltpu.make_async_remote_copy(src_ref=src, dst_ref=dst, send_sem=ssem, recv_sem=rsem,
                                        device_id=dev, device_id_type=MESH)


def _row_half(ref, lead, hc):
    hl = ref.shape[-2] // 2
    return ref.at[lead, pl.ds(hc * hl, hl), :]


def _gather_side(items):
    n = len(items)

    def copies(ins, outs, ssem, rsem):
        x, y, c, s, chips, sid, sib = _place()
        cps = [_remote(_row_half(ins[k], items[k][1], c), _row_half(outs[k], s, c),
                       ssem.at[3 * k + j], rsem.at[3 * k + j], (*chip, c))
               for k in range(n) for j, chip in enumerate(chips)]
        return cps, c, sid, sib

    def start(ins, outs, ssem, rsem):
        for cp in copies(ins, outs, ssem, rsem)[0]:
            cp.start()

    def finish(ins, outs, ssem, rsem):
        cps, c, sid, sib = copies(ins, outs, ssem, rsem)
        for k in range(n):
            for j in range(3):
                got = _row_half(outs[k], sid[j], c)
                _remote(got, got, ssem.at[3 * k + j], rsem.at[3 * k + j], sib).wait_recv()
        for cp in cps:
            cp.wait_send()

    shapes = [jax.ShapeDtypeStruct((4,) + w.shape[1:], w.dtype) for w, _ in items]
    return _Side([w for w, _ in items], shapes, 3 * n, start, finish)


def _gather_join(gathered, name):
    n = len(gathered)

    def body(*refs):
        outs, ssem, rsem = refs[n:2 * n], refs[2 * n], refs[2 * n + 1]
        x, y, c, s, chips, sid, sib = _place()
        cps = []
        for k in range(n):
            for j in range(3):
                mine = _row_half(outs[k], sid[j], c)
                cps.append(_remote(mine, mine, ssem.at[3 * k + j], rsem.at[3 * k + j], sib))
        for cp in cps:
            cp.start()
        for k in range(n):
            for j in range(3):
                other = _row_half(outs[k], sid[j], 1 - c)
                _remote(other, other, ssem.at[3 * k + j], rsem.at[3 * k + j], sib).wait_recv()
        for cp in cps:
            cp.wait_send()

    return _pcall(
        body, name=name, in_specs=[HBM_SPEC] * n, out_specs=[HBM_SPEC] * n,
        out_shape=[jax.ShapeDtypeStruct(g.shape, g.dtype) for g in gathered],
        input_output_aliases={k: k for k in range(n)},
        scratch_shapes=[pltpu.SemaphoreType.DMA((3 * n,)), pltpu.SemaphoreType.DMA((3 * n,))],
    )(*gathered)


def _gather_layer0(win, conv):
    def body(win_ref, cv_ref, gin_ref, gcv_ref, ssem, rsem):
        x, y, c, s, chips, sid, sib = _place()

        def in_half(slot, hc):
            return _row_half(gin_ref, slot, hc)

        sends = []
        for j, chip in enumerate(chips):
            dev = (*chip, c)
            sends.append(_remote(_row_half(win_ref, 0, c), in_half(s, c), ssem.at[j], rsem.at[j], dev))
            sends.append(_remote(cv_ref, gcv_ref.at[s], ssem.at[3 + j], rsem.at[3 + j], dev))
        for cp in sends:
            cp.start()
        for j in range(3):
            _remote(in_half(sid[j], c), in_half(sid[j], c), ssem.at[j], rsem.at[j], sib).wait_recv()
            f = _remote(in_half(sid[j], c), in_half(sid[j], c), ssem.at[6 + j], rsem.at[6 + j], sib)
            f.start()
            sends.append(f)
        for j in range(3):
            _remote(in_half(sid[j], 1 - c), in_half(sid[j], 1 - c), ssem.at[6 + j], rsem.at[6 + j], sib).wait_recv()
            _remote(gcv_ref.at[sid[j]], gcv_ref.at[sid[j]], ssem.at[3 + j], rsem.at[3 + j], sib).wait_recv()
        for cp in sends:
            cp.wait_send()

    return _pcall(
        body, name="gather_layer0",
        in_specs=[HBM_SPEC] * 2, out_specs=[HBM_SPEC] * 2,
        out_shape=[jax.ShapeDtypeStruct((4,) + win.shape[1:], win.dtype),
                   jax.ShapeDtypeStruct((4,) + conv.shape, conv.dtype)],
        scratch_shapes=[pltpu.SemaphoreType.DMA((9,)), pltpu.SemaphoreType.DMA((9,))],
    )(win, conv)


def _swap_halves(arrs, axes, name):
    n = len(arrs)

    def half_shape(a, ax):
        return a.shape[:ax] + (a.shape[ax] // 2,) + a.shape[ax + 1:]

    def body(*refs):
        src, dst, ssem, rsem = refs[:n], refs[n:2 * n], refs[2 * n], refs[2 * n + 1]
        x, y, c, s, chips, sid, sib = _place()
        cps = []
        for k in range(n):
            hl = src[k].shape[axes[k]] // 2
            idx = [slice(None)] * len(src[k].shape)
            idx[axes[k]] = pl.ds((1 - c) * hl, hl)
            cps.append(_remote(src[k].at[tuple(idx)], dst[k], ssem.at[k], rsem.at[k], sib))
        for cp in cps:
            cp.start()
        for cp in cps:
            cp.wait()

    return _pcall(
        body, name=name, in_specs=[HBM_SPEC] * n, out_specs=[HBM_SPEC] * n,
        out_shape=[jax.ShapeDtypeStruct(half_shape(a, ax), a.dtype) for a, ax in zip(arrs, axes)],
        scratch_shapes=[pltpu.SemaphoreType.DMA((n,)), pltpu.SemaphoreType.DMA((n,))],
    )(*arrs)


def _swap_side(arrs, axes):
    n = len(arrs)

    def copies(ins, outs, ssem, rsem):
        x, y, c, s, chips, sid, sib = _place()
        cps = []
        for k in range(n):
            hl = ins[k].shape[axes[k]] // 2
            idx = [slice(None)] * len(ins[k].shape)
            idx[axes[k]] = pl.ds((1 - c) * hl, hl)
            cps.append(_remote(ins[k].at[tuple(idx)], outs[k], ssem.at[k], rsem.at[k], sib))
        return cps

    def start(ins, outs, ssem, rsem):
        for cp in copies(ins, outs, ssem, rsem):
            cp.start()

    def finish(ins, outs, ssem, rsem):
        for cp in copies(ins, outs, ssem, rsem):
            cp.wait()

    shapes = [jax.ShapeDtypeStruct(a.shape[:ax] + (a.shape[ax] // 2,) + a.shape[ax + 1:], a.dtype)
              for a, ax in zip(arrs, axes)]
    return _Side(list(arrs), shapes, n, start, finish)


def _chips_side(arrs, per_target):
    n = len(arrs)

    def copies(ins, outs, ssem, rsem):
        x, y, c, s, chips, sid, sib = _place()
        cps = [_remote(ins[k].at[sid[j]] if per_target[k] else ins[k], outs[k].at[s],
                       ssem.at[3 * k + j], rsem.at[3 * k + j], (*chip, c))
               for k in range(n) for j, chip in enumerate(chips)]
        return cps, sid, sib

    def start(ins, outs, ssem, rsem):
        for cp in copies(ins, outs, ssem, rsem)[0]:
            cp.start()

    def finish(ins, outs, ssem, rsem):
        cps, sid, sib = copies(ins, outs, ssem, rsem)
        for k in range(n):
            for j in range(3):
                got = outs[k].at[sid[j]]
                _remote(got, got, ssem.at[3 * k + j], rsem.at[3 * k + j], sib).wait_recv()
        for cp in cps:
            cp.wait_send()

    shapes = [jax.ShapeDtypeStruct(a.shape if pt else (4,) + a.shape, a.dtype) for a, pt in zip(arrs, per_target)]
    return _Side(list(arrs), shapes, 3 * n, start, finish)


def _swap_whole(arrs, name):
    n = len(arrs)

    def body(*refs):
        src, dst, ssem, rsem = refs[:n], refs[n:2 * n], refs[2 * n], refs[2 * n + 1]
        *_, sib = _place()
        cps = [_remote(src[k], dst[k], ssem.at[k], rsem.at[k], sib) for k in range(n)]
        for cp in cps:
            cp.start()
        for cp in cps:
            cp.wait()

    return _pcall(
        body, name=name, in_specs=[HBM_SPEC] * n, out_specs=[HBM_SPEC] * n,
        out_shape=[jax.ShapeDtypeStruct(a.shape, a.dtype) for a in arrs],
        scratch_shapes=[pltpu.SemaphoreType.DMA((n,)), pltpu.SemaphoreType.DMA((n,))],
    )(*arrs)


def _perm_cols(w):
    parts = [w[..., int(_ORIG_OFF[oi]):int(_ORIG_OFF[oi]) + IN_SIZES[oi]] for oi, _ in _PIECES]
    parts.append(jnp.zeros(w.shape[:-1] + (NP - N_IN,), w.dtype))
    return jnp.concatenate(parts, -1)


def _perm_rows(w):
    return jnp.concatenate([w[..., 512:1536, :], w[..., 0:512, :], w[..., 1536:2048, :]], -2)


_SMALL = ("sinks", "r_conv_b", "r_wa", "r_ba", "r_wx", "r_bx", "r_lam", "g_a_log", "g_dt_bias", "g_norm_w",
          "ln_g", "ln_b", "r_conv_w", "g_conv_w")
_PACK_ROWS = 16


def _piece_rows(n):
    return -(-n // (128 * _PACK_ROWS)) * _PACK_ROWS


def _pack(arrs):
    parts = []
    for a in arrs:
        n = int(np.prod(a.shape))
        rows = _piece_rows(n)
        if n % 128 == 0:
            blk = a.reshape(n // 128, 128)
        else:
            blk = jnp.pad(a.reshape(1, n), ((0, 0), (0, (-n) % 128))).reshape(-1, 128)
        if blk.shape[0] < rows:
            blk = jnp.pad(blk, ((0, rows - blk.shape[0]), (0, 0)))
        parts.append(blk)
    return jnp.concatenate(parts, 0)


def _unpack(packed, shapes):
    out = []
    r = 0
    for shp in shapes:
        n = int(np.prod(shp))
        if n % 128 == 0:
            out.append(packed[r:r + n // 128].reshape(shp))
        else:
            nr = -(-n // 128)
            out.append(packed[r:r + nr].reshape(1, nr * 128)[:, :n].reshape(shp))
        r += _piece_rows(n)
    return out


def _tile(n, t):
    return min(n, t)


def _layer_fwd(l, x, xb, wb, wob, ln, rope_c, rope_s, p, side=None, target=None):
    S_ = x.shape[0]
    proj = _matmul(xb, wb, ta=False, tb=False, tm=_tile(S_, 1024), tn=NP // 4, tk=wb.shape[0], out_dtype=F32,
                   name=f"in_proj_{l}", side=side)
    side_out = None
    if side:
        proj, side_out = proj
    h, ymix = _rglru_fwd(proj, p["r_conv_w"], p["r_conv_b"], p["r_wa"], p["r_ba"], p["r_wx"], p["r_bx"], p["r_lam"],
                         T=_tile(S_, 256), name=f"rglru_fwd_{l}")
    ymix = _attn_fwd(proj, rope_c, rope_s, p["sinks"], ymix, T=_tile(S_, 512), name=f"attn_fwd_{l}")
    ymix, st, tms = _gdn_fwd(proj, p["g_conv_w"], p["g_a_log"], p["g_dt_bias"], p["g_norm_w"], ymix,
                             T=_tile(S_, 256), name=f"gdn_fwd_{l}")
    out = _outproj(ymix, wob(side_out), x, ln[0], ln[1], tm=_tile(S_, 512), name=f"out_proj_{l}", target=target)
    sv = dict(proj=proj, h=h, st=st, tms=tms, ymix=ymix)
    if target is None:
        sv["z"], sv["y"], sv["yb"] = out
    else:
        sv["head"] = out
    return sv


def _layer_bwd(l, sv, x_b, dz, dzb, wb, wob, rope_c, rope_s, p, side_dmix=None, side_dw_in=None, side_dx=None):
    S_, D = dz.shape
    proj = sv["proj"]
    dwo = _matmul(sv["ymix"], dzb, ta=True, tb=False, tm=512, tn=_tile(D, 2048), tk=_tile(S_, 2048),
                  out_dtype=F32, name=f"dw_out_{l}",
                  out_blocks=((MIX_WIDTH, D), (512, _tile(D, 2048)),
                              lambda i, j: (jnp.where(i == 3, 3, (i + 1) % 3), j)))
    side = side_dmix(dwo) if side_dmix else None
    dymix = _matmul(dzb, wob, ta=False, tb=True, tm=_tile(S_, 1024), tn=1024, tk=D, out_dtype=F32,
                    name=f"dmix_{l}", side=side)
    out_dmix = None
    if side:
        dymix, out_dmix = dymix
    dproj, dk, dv, dkt, dvt, dsk = _attn_bwd(proj, rope_c, rope_s, p["sinks"], dymix, T=_tile(S_, 512),
                                             name=f"attn_bwd_{l}")
    (dproj, dcw_r, dcb_r, dwa, dba, dwx, dbx, dlam) = _rglru_bwd(
        proj, sv["h"], dymix, dproj, p["r_conv_w"], p["r_conv_b"], p["r_wa"], p["r_ba"], p["r_wx"], p["r_bx"],
        p["r_lam"], T=_tile(S_, 256), name=f"rglru_bwd_{l}")
    dproj, dgba, dcw_g, dpv, dnw = _gdn_bwd(proj, sv["st"], sv["tms"], dymix, dproj, p["g_conv_w"], p["g_a_log"],
                                            p["g_dt_bias"], p["g_norm_w"], T=_tile(S_, 256), name=f"gdn_bwd_{l}")
    tail = jnp.concatenate([dk[128:], dkt, dv[128:], dvt], 0).reshape(2, S_, 128)
    tail = jnp.concatenate([tail[0], tail[1], dgba, jnp.zeros((S_, NP - OFF_GBA - 128), BF16)], 1)
    dproj = lax.dynamic_update_slice(dproj, tail, (0, OFF_AK))
    small = dict(sinks=dsk[:, 0], r_conv_b=dcb_r[0], r_wa=dwa, r_ba=dba[0], r_wx=dwx, r_bx=dbx[0], r_lam=dlam[0],
                 g_a_log=dpv[0, 4:8], g_dt_bias=dpv[1, 4:8], g_norm_w=dnw[0], r_conv_w=dcw_r, g_conv_w=dcw_g)
    side = side_dw_in(small, dwo, out_dmix) if side_dw_in else None
    dwin = _matmul(x_b, dproj, ta=True, tb=False, tm=_tile(D, 1024), tn=NP // 4, tk=_tile(S_, 2048),
                   out_dtype=F32, name=f"dw_in_{l}", side=side)
    out_dw_in = None
    if side:
        dwin, out_dw_in = dwin
    side = side_dx(dwin) if side_dx else None
    dx_args = dict(ta=False, tb=True, tm=_tile(S_, 1024), tn=_tile(D, 1024), tk=NP // 2, out_dtype=F32, extra=dz,
                   alpha=DEEPNORM_ALPHA)
    out_dx = None
    if side:
        dx, out_dx = _matmul(dproj, wb, name=f"dx_{l}", side=side, **dx_args)
    else:
        dx = _matmul(dproj, wb, name=f"dx_{l}", **dx_args)
    return dx, dwin, dwo, small, out_dw_in, out_dx


def kernel(x, w_in, sinks, r_conv_w, r_conv_b, r_wa, r_ba, r_wx, r_bx, r_lam, g_conv_w, g_a_log, g_dt_bias, g_norm_w, w_out, ln_g, ln_b, loss_target, m_w_in, m_sinks, m_r_conv_w, m_r_conv_b, m_r_wa, m_r_ba, m_r_wx, m_r_bx, m_r_lam, m_g_conv_w, m_g_a_log, m_g_dt_bias, m_g_norm_w, m_w_out, m_ln_g, m_ln_b, v_w_in, v_sinks, v_r_conv_w, v_r_conv_b, v_r_wa, v_r_ba, v_r_wx, v_r_bx, v_r_lam, v_g_conv_w, v_g_a_log, v_g_dt_bias, v_g_norm_w, v_w_out, v_ln_g, v_ln_b):
    S_, D = x.shape[1], x.shape[2]
    nsh = w_in.shape[2]
    rsh = w_out.shape[1]
    cx, cy, cc = lax.axis_index("x"), lax.axis_index("y"), lax.axis_index("c")
    chip = 2 * cx + cy
    rcw_n, gcw_n = r_conv_w.shape[2], g_conv_w.shape[2]

    conv_pack = jnp.concatenate([r_conv_w, g_conv_w], 2)
    w_in_b, w_out_b = w_in.astype(BF16), w_out.astype(BF16)
    g_in0, g_conv = _gather_layer0(w_in_b, conv_pack)

    def shards(own, got):
        return [jnp.where(chip == t, own, got[t]) for t in range(4)]

    def w_in_of(l, g_in):
        return _perm_cols(jnp.concatenate(shards(w_in_b[l], g_in), 1))

    def w_out_of(l, g_out):
        return _perm_rows(jnp.concatenate(shards(w_out_b[l], g_out), 0))

    rcw = jnp.concatenate(shards(r_conv_w, g_conv[:, :, :, :rcw_n]), 2)
    gcw = jnp.concatenate(shards(g_conv_w, g_conv[:, :, :, rcw_n:]), 2)

    pos = jnp.arange(S_, dtype=F32)[:, None]
    inv = 1.0 / (ROPE_THETA ** (jnp.arange(0, A_HEAD_DIM, 2, dtype=F32) / A_HEAD_DIM))
    ang = pos * inv[None, :]
    cos, sin = jnp.cos(ang), jnp.sin(ang)
    rope_c = jnp.concatenate([cos, cos, cos, cos], 1)
    rope_s = jnp.concatenate([-sin, sin, -sin, sin], 1)

    def params(l):
        return dict(sinks=sinks[l], r_conv_w=rcw[l], r_conv_b=r_conv_b[l], r_wa=r_wa[l], r_ba=r_ba[l],
                    r_wx=r_wx[l], r_bx=r_bx[l], r_lam=r_lam[l], g_conv_w=gcw[l], g_a_log=g_a_log[l],
                    g_dt_bias=g_dt_bias[l], g_norm_w=g_norm_w[l])

    assert DEPTH == 2
    xb0 = x[0].astype(BF16)
    wb, wob = [w_in_of(0, g_in0), None], [None, None]
    late = {}

    def w_out_0(arrived):
        late["w_in_1"], g_out0 = _gather_join(arrived, "gather_join_0")
        wob[0] = w_out_of(0, g_out0)
        return wob[0]

    def w_out_1(arrived):
        wob[1] = w_out_of(1, _gather_join(arrived, "gather_join_1")[0])
        return wob[1]

    sv0 = _layer_fwd(0, x[0], xb0, wb[0], w_out_0, (ln_g[0], ln_b[0]), rope_c, rope_s, params(0),
                     side=_gather_side([(w_in_b, 1), (w_out_b, 0)]))
    wb[1] = w_in_of(1, late["w_in_1"])
    sv1 = _layer_fwd(1, sv0["y"], sv0["yb"], wb[1], w_out_1, (ln_g[1], ln_b[1]), rope_c, rope_s, params(1),
                     side=_gather_side([(w_out_b, 1)]), target=loss_target[0])
    saved, xbs = [sv0, sv1], [xb0, sv0["yb"]]

    tm_ln = _tile(S_, 512)
    dz, dzb, dg_l, db_l, loss_part = saved[-1]["head"]
    assert DEPTH == 2
    wcov = (-(-nsh // 128) + 1) * 128
    names = list(_SMALL)

    def own(a):
        return lax.dynamic_index_in_dim(a, chip, 0, keepdims=False)

    def sum_in(l, cp, arrived):
        return _sum4(arrived, own(cp), tr=_tile(D // 2, 256), name=f"chip_sum_w_in_{l}")

    def sum_out(l, cp, arrived):
        return _sum4(arrived, own(cp), tr=rsh // 2, name=f"chip_sum_w_out_{l}")

    dlng, dlnb = [None, dg_l[0]], [None, db_l[0]]
    dx, dwin1, dwo1, small1, _, _ = _layer_bwd(1, saved[1], xbs[1], dz, dzb, wb[1], wob[1], rope_c, rope_s, params(1))
    dwo1_4 = dwo1.reshape(4, rsh, D)
    dz, dzb, dg_l, db_l, _ = _ln_bwd(saved[0]["z"], ln_g[0], ln_b[0], dx, tm=tm_ln, name="ln_bwd_0")
    dlng[0], dlnb[0] = dg_l[0], db_l[0]

    held = {}

    def side_dmix(dwo0):
        return _swap_side([dwin1, dwo1_4, dwo0.reshape(4, rsh, D)], [0, 1, 1])

    def side_dw_in(small0, dwo0, got):
        sm = {k: jnp.stack([small0[k], small1[k]]) for k in small0}
        sm["ln_g"], sm["ln_b"] = jnp.stack(dlng), jnp.stack(dlnb)
        gs = _pack([sm[n] for n in names])
        (got_s,) = _swap_halves([gs], [0], "reduce_pair_small")
        held["in_cp1"] = _pair_sum_windows(dwin1, got[0], nsh, wcov, out_dtype=BF16, name="pair_sum_w_in_1")
        held["out_cp1"] = _pair_sum_blocks(dwo1_4, got[1], out_dtype=BF16, name="pair_sum_w_out_1")
        held["out_cp0"] = _pair_sum_blocks(dwo0.reshape(4, rsh, D), got[2], out_dtype=BF16, name="pair_sum_w_out_0")
        held["s_cp"] = _pair_sum_blocks(gs[None], got_s[None], out_dtype=F32, name="pair_sum_small")[0]
        held["shapes"] = [sm[n].shape for n in names]
        return _chips_side([held["in_cp1"], held["out_cp1"], held["out_cp0"]], [True, True, True])

    def side_dx(dwin0):
        got = _swap_halves([dwin0], [0], "reduce_pair_0b")
        held["in_cp0"] = _pair_sum_windows(dwin0, got[0], nsh, wcov, out_dtype=BF16, name="pair_sum_w_in_0")
        return _chips_side([held["in_cp0"], held["s_cp"]], [True, False])

    dx, _, _, _, arrived_a, arrived_b = _layer_bwd(0, saved[0], xbs[0], dz, dzb, wb[0], wob[0], rope_c, rope_s,
                                                   params(0), side_dmix=side_dmix, side_dw_in=side_dw_in,
                                                   side_dx=side_dx)
    grad_x = dx[None]
    loss = lax.psum(loss_part[0, 0], ("x", "y", "c"))
    s_cp = held["s_cp"]
    mine = [sum_in(0, held["in_cp0"], arrived_b[0]), sum_out(0, held["out_cp0"], arrived_a[2]),
            sum_in(1, held["in_cp1"], arrived_a[0]), sum_out(1, held["out_cp1"], arrived_a[1]),
            _sum4(arrived_b[1], s_cp, tr=s_cp.shape[0], name="chip_sum_small")]
    other = _swap_whole(mine, "reduce_join")

    def both(k, axis):
        return jnp.where(cc == 0, jnp.concatenate([mine[k], other[k]], axis),
                         jnp.concatenate([other[k], mine[k]], axis))

    g_w_in = lax.dynamic_slice_in_dim(jnp.stack([both(2 * l, 0) for l in range(DEPTH)]), (nsh * chip) % 128, nsh, 2)
    g_w_out = jnp.stack([both(2 * l + 1, 0) for l in range(DEPTH)])
    g_small = both(2 * DEPTH, 0)

    gsm = dict(zip(names, _unpack(g_small, held["shapes"])))
    gsm["r_conv_w"] = lax.dynamic_slice_in_dim(gsm["r_conv_w"], chip * rcw_n, rcw_n, 2)
    gsm["g_conv_w"] = lax.dynamic_slice_in_dim(gsm["g_conv_w"], chip * gcw_n, gcw_n, 2)
    wts = dict(sinks=sinks, r_conv_w=r_conv_w, r_conv_b=r_conv_b, r_wa=r_wa, r_ba=r_ba, r_wx=r_wx, r_bx=r_bx,
               r_lam=r_lam, g_conv_w=g_conv_w, g_a_log=g_a_log, g_dt_bias=g_dt_bias, g_norm_w=g_norm_w,
               ln_g=ln_g, ln_b=ln_b)
    mom = dict(sinks=m_sinks, r_conv_w=m_r_conv_w, r_conv_b=m_r_conv_b, r_wa=m_r_wa, r_ba=m_r_ba, r_wx=m_r_wx,
               r_bx=m_r_bx, r_lam=m_r_lam, g_conv_w=m_g_conv_w, g_a_log=m_g_a_log, g_dt_bias=m_g_dt_bias,
               g_norm_w=m_g_norm_w, ln_g=m_ln_g, ln_b=m_ln_b)
    vel = dict(sinks=v_sinks, r_conv_w=v_r_conv_w, r_conv_b=v_r_conv_b, r_wa=v_r_wa, r_ba=v_r_ba, r_wx=v_r_wx,
               r_bx=v_r_bx, r_lam=v_r_lam, g_conv_w=v_g_conv_w, g_a_log=v_g_a_log, g_dt_bias=v_g_dt_bias,
               g_norm_w=v_g_norm_w, ln_g=v_ln_g, ln_b=v_ln_b)
    d_s, m_s, v_s = _adamw_many(*[[d[n] for n in names] for d in (wts, gsm, mom, vel)], name="adamw_small")
    d_sm, m_sm, v_sm = (dict(zip(names, a)) for a in (d_s, m_s, v_s))
    def cols(a):
        return jnp.transpose(a, (2, 0, 1))

    g_w_in_t = cols(g_w_in)
    outs_t = _adamw_cols(cols(w_in), g_w_in_t, cols(m_w_in), cols(v_w_in), name="adamw_w_in")
    d_in, m_in, v_in = (jnp.transpose(a, (1, 2, 0)) for a in outs_t)
    g_w_in = jnp.transpose(g_w_in_t, (1, 2, 0))
    d_out, m_out, v_out = _adamw(w_out, g_w_out, m_w_out, v_w_out, tr=256, name="adamw_w_out")

    order = ["w_in", "sinks", "r_conv_w", "r_conv_b", "r_wa", "r_ba", "r_wx", "r_bx", "r_lam", "g_conv_w",
             "g_a_log", "g_dt_bias", "g_norm_w", "w_out", "ln_g", "ln_b"]
    grads = dict(gsm, w_in=g_w_in, w_out=g_w_out)
    deltas = dict(d_sm, w_in=d_in, w_out=d_out)
    new_m = dict(m_sm, w_in=m_in, w_out=m_out)
    new_v = dict(v_sm, w_in=v_in, w_out=v_out)
    return (loss, grad_x, *[grads[n] for n in order], *[deltas[n] for n in order],
            *[new_m[n] for n in order], *[new_v[n] for n in order])
```

```python
import functools

import jax
import jax.numpy as jnp
import numpy as np
from jax import lax
from jax.experimental import pallas as pl
from jax.experimental.pallas import tpu as pltpu

F32 = jnp.float32
BF16 = jnp.bfloat16
MESH = pl.DeviceIdType.MESH

DEPTH = 2
A_HEADS, A_KV_HEADS, A_HEAD_DIM = 8, 2, 64
A_WIDTH, A_KV_WIDTH = 512, 128
WINDOW = 128
ROPE_THETA = 10000.0
R_WIDTH, R_BLOCKS, R_BLOCK_DIM, R_C = 1024, 8, 128, 8.0
CONV_WIDTH = 4
G_HEADS, G_HEAD_DIM, G_WIDTH, G_CHUNK = 4, 128, 512, 64
MIX_WIDTH = 2048
IN_SIZES = (512, 128, 128, 512, 1024, 1024, 512, 512, 512, 512, 4, 4)
N_IN = 5384
DEEPNORM_ALPHA = (2 * DEPTH) ** 0.25
LN_EPS = 1e-5
RMS_EPS = 1e-6
ADAM_LR, ADAM_B1, ADAM_B2, ADAM_EPS, ADAM_WD, ADAM_STEP = 0.001, 0.9, 0.999, 1e-08, 0.01, 10

NP = 5632
OFF_GQKV, OFF_GZ, OFF_RX, OFF_RZ, OFF_AQ, OFF_AZ, OFF_AK, OFF_AV, OFF_GBA = (
    0, 1536, 2048, 3072, 4096, 4608, 5120, 5248, 5376)
_ORIG_OFF = np.concatenate([[0], np.cumsum(IN_SIZES)])[:-1]
_PIECES = ((6, OFF_GQKV), (7, OFF_GQKV + 512), (8, OFF_GQKV + 1024), (9, OFF_GZ), (4, OFF_RX), (5, OFF_RZ),
           (0, OFF_AQ), (3, OFF_AZ), (1, OFF_AK), (2, OFF_AV), (10, OFF_GBA), (11, OFF_GBA + 4))


def _orig_block_to_perm():
    table = list(range(NP // 128))
    for oi, off in _PIECES:
        if IN_SIZES[oi] % 128 == 0:
            for k in range(IN_SIZES[oi] // 128):
                table[int(_ORIG_OFF[oi]) // 128 + k] = off // 128 + k
    return table
MIX_R, MIX_A, MIX_G = 0, 1024, 1536
VMEM_LIMIT = 56 * 1024 * 1024
ANY_SPEC = pl.BlockSpec(memory_space=pl.ANY)


def _pcall(body, **kw):
    return pl.pallas_call(body, **kw)


def _cp(sem, limit=VMEM_LIMIT):
    return pltpu.CompilerParams(dimension_semantics=sem, vmem_limit_bytes=limit)


def _sigmoid(x):
    return 0.5 + 0.5 * jnp.tanh(0.5 * x)


def _silu(x):
    return x * _sigmoid(x)


def _dsilu(x):
    s = _sigmoid(x)
    return s * (1.0 + x * (1.0 - s))


def _log1p(x):
    u = 1.0 + x
    d = jnp.where(u == 1.0, 1.0, u - 1.0)
    return jnp.where(u == 1.0, x, jnp.log(u) * (x / d))


def _softplus(x):
    return jnp.maximum(x, 0.0) + _log1p(jnp.exp(-jnp.abs(x)))


def _one_minus_exp(x):
    series = -x * (1.0 + x * (0.5 + x * (1.0 / 6.0 + x * (1.0 / 24.0))))
    return jnp.where(x > -0.05, series, 1.0 - jnp.exp(x))


def _nn(a, b):
    return lax.dot_general(a, b, (((1,), (0,)), ((), ())), preferred_element_type=F32)


def _nt(a, b):
    return lax.dot_general(a, b, (((1,), (1,)), ((), ())), preferred_element_type=F32)


def _tn(a, b):
    return lax.dot_general(a, b, (((0,), (0,)), ((), ())), preferred_element_type=F32)


def _b(x):
    return x.astype(BF16)


def _split3(x):
    hi = x.astype(BF16)
    r1 = x - hi.astype(F32)
    mid = r1.astype(BF16)
    lo = (r1 - mid.astype(F32)).astype(BF16)
    return hi, mid, lo


def _dot3(f, a, b):
    ah, am, _ = _split3(a)
    bh, bm, _ = _split3(b)
    return f(ah, bh) + (f(ah, bm) + f(am, bh))


def _dot_exact_lhs(f, a_bf16, b):
    bh, bm, bl = _split3(b)
    return f(a_bf16, bh) + (f(a_bf16, bm) + f(a_bf16, bl))


def _rot(x):
    w = x.shape[-1]
    lane = lax.broadcasted_iota(jnp.int32, (1, w), 1)
    return jnp.where((lane & 63) < 32, pltpu.roll(x, w - 32, 1), pltpu.roll(x, 32, 1))


def _conv_taps(ext, n):
    return [pltpu.roll(ext, 3 - k, 0)[8:8 + n] if k < 3 else ext[8:8 + n] for k in range(CONV_WIDTH)]


def _conv_taps_t(ext, n):
    m = ext.shape[0]
    return [pltpu.roll(ext, m - (3 - k), 0)[0:n] if k < 3 else ext[0:n] for k in range(CONV_WIDTH)]


def _scan_steps(a, b, pos, span, shifts, reverse):
    n = a.shape[0]
    for s in shifts:
        if reverse:
            a_sh = pltpu.roll(a, n - s, 0)
            b_sh = pltpu.roll(b, n - s, 0)
            ok = pos < (span - s)
        else:
            a_sh = pltpu.roll(a, s, 0)
            b_sh = pltpu.roll(b, s, 0)
            ok = pos >= s
        b = jnp.where(ok, a * b_sh + b, b)
        a = jnp.where(ok, a * a_sh, a)
    return a, b


def _scan_lin(a, b, reverse):
    n = a.shape[0]
    shifts = []
    s = 1
    while s < n:
        shifts.append(s)
        s *= 2
    return _scan_steps(a, b, lax.broadcasted_iota(jnp.int32, (n, 1), 0), n, shifts, reverse)


class _Side:
    def __init__(self, inputs, out_shapes, n_sems, start, finish):
        self.inputs, self.out_shapes, self.n_sems, self.start, self.finish = inputs, out_shapes, n_sems, start, finish


def _matmul(a, b, *, ta, tb, tm, tn, tk, out_dtype, name, extra=None, alpha=0.0, out_blocks=None, side=None):
    if ta:
        K, M = a.shape
    else:
        M, K = a.shape
    if tb:
        N, K2 = b.shape
    else:
        K2, N = b.shape
    assert K == K2 and M % tm == 0 and N % tn == 0 and K % tk == 0, (a.shape, b.shape, tm, tn, tk)
    nk = K // tk
    ca = 0 if ta else 1
    cb = 1 if tb else 0
    has_extra = extra is not None

    assert nk == 1 or out_dtype == F32
    n_in = 2 + int(has_extra)
    ns_in = len(side.inputs) if side else 0
    ns_out = len(side.out_shapes) if side else 0
    grid = (M // tm, N // tn, nk)

    def body(*refs):
        a_ref, b_ref = refs[0], refs[1]
        e_ref = refs[2] if has_extra else None
        o_ref = refs[n_in + ns_in]
        k = pl.program_id(2)
        if side:
            s_in = refs[n_in:n_in + ns_in]
            s_out = refs[n_in + ns_in + 1:n_in + ns_in + 1 + ns_out]
            ssem, rsem = refs[-2], refs[-1]
            i, j = pl.program_id(0), pl.program_id(1)

            @pl.when((i == 0) & (j == 0) & (k == 0))
            def _():
                side.start(s_in, s_out, ssem, rsem)

            @pl.when((i == grid[0] - 1) & (j == grid[1] - 1) & (k == grid[2] - 1))
            def _():
                side.finish(s_in, s_out, ssem, rsem)

        part = lax.dot_general(a_ref[...], b_ref[...], (((ca,), (cb,)), ((), ())), preferred_element_type=F32)
        if nk == 1:
            if e_ref is not None:
                part = part + alpha * e_ref[...]
            o_ref[...] = part.astype(o_ref.dtype)
            return

        @pl.when(k == 0)
        def _():
            o_ref[...] = part

        @pl.when((k > 0) & (k < nk - 1))
        def _():
            o_ref[...] += part

        @pl.when(k == nk - 1)
        def _():
            last = o_ref[...] + part
            if e_ref is not None:
                last = last + alpha * e_ref[...]
            o_ref[...] = last

    a_spec = (pl.BlockSpec((tk, tm), lambda i, j, k: (k, i)) if ta
              else pl.BlockSpec((tm, tk), lambda i, j, k: (i, k)))
    b_spec = (pl.BlockSpec((tn, tk), lambda i, j, k: (j, k)) if tb
              else pl.BlockSpec((tk, tn), lambda i, j, k: (k, j)))
    e_spec = pl.BlockSpec((tm, tn), lambda i, j, k: (i, j))
    if out_blocks is None:
        o_spec, o_shape = e_spec, (M, N)
    else:
        o_shape, o_block, o_map = out_blocks
        o_spec = pl.BlockSpec(o_block, lambda i, j, k: o_map(i, j))
    in_specs = [a_spec, b_spec] + ([e_spec] if has_extra else [])
    args = (a, b) + ((extra,) if has_extra else ())
    if not side:
        return _pcall(
            body, name=name, grid=grid, in_specs=in_specs, out_specs=o_spec,
            out_shape=jax.ShapeDtypeStruct(o_shape, out_dtype),
            compiler_params=_cp(("parallel", "parallel", "arbitrary")),
        )(*args)
    outs = _pcall(
        body, name=name, grid=grid, in_specs=in_specs + [HBM_SPEC] * ns_in,
        out_specs=[o_spec] + [HBM_SPEC] * ns_out,
        out_shape=[jax.ShapeDtypeStruct(o_shape, out_dtype)] + list(side.out_shapes),
        scratch_shapes=[pltpu.SemaphoreType.DMA((side.n_sems,)), pltpu.SemaphoreType.DMA((side.n_sems,))],
        compiler_params=_cp(("arbitrary", "arbitrary", "arbitrary")),
    )(*args, *side.inputs)
    return outs[0], outs[1:]


def _ln_stats(z):
    mu = jnp.mean(z, -1, keepdims=True)
    zc = z - mu
    var = jnp.mean(zc * zc, -1, keepdims=True)
    rstd = lax.rsqrt(var + LN_EPS)
    return zc * rstd, rstd


def _ln_bwd_tile(z, gam, bet, other, from_target, dz_ref, dzb_ref, dg_ref, db_ref, loss_ref):
    i = pl.program_id(0)

    @pl.when(i == 0)
    def _():
        dg_ref[...] = jnp.zeros_like(dg_ref)
        db_ref[...] = jnp.zeros_like(db_ref)
        loss_ref[...] = jnp.zeros_like(loss_ref)

    xh, rstd = _ln_stats(z)
    if from_target:
        err = xh * gam + bet - other
        per_tok = jnp.mean(err * err, -1, keepdims=True)
        loss_ref[...] += 0.5 * jnp.sum(per_tok, 0, keepdims=True)
        dy = err * (1.0 / z.shape[-1])
    else:
        dy = other
    dxh = dy * gam
    m1 = jnp.mean(dxh, -1, keepdims=True)
    m2 = jnp.mean(dxh * xh, -1, keepdims=True)
    dz = rstd * (dxh - m1 - xh * m2)
    dz_ref[...] = dz
    dzb_ref[...] = dz.astype(BF16)
    dg_ref[...] += jnp.sum(dy * xh, 0, keepdims=True)
    db_ref[...] += jnp.sum(dy, 0, keepdims=True)


def _outproj(ymix, wo, x, g, b, *, tm, name, target=None):
    S_, D = x.shape
    last = target is not None

    def body(*refs):
        y_ref, w_ref, x_ref, g_ref, b_ref = refs[:5]
        z = DEEPNORM_ALPHA * x_ref[...] + _nn(y_ref[...], w_ref[...])
        if last:
            _ln_bwd_tile(z, g_ref[...], b_ref[...], refs[5][...], True, *refs[6:])
            return
        z_ref, o_ref, ob_ref = refs[5:]
        z_ref[...] = z
        xh, _ = _ln_stats(z)
        y = xh * g_ref[...] + b_ref[...]
        o_ref[...] = y
        ob_ref[...] = y.astype(BF16)

    row = pl.BlockSpec((tm, D), lambda i: (i, 0))
    vec = pl.BlockSpec((1, D), lambda i: (0, 0))
    one = pl.BlockSpec((1, 1), lambda i: (0, 0))
    in_specs = [pl.BlockSpec((tm, MIX_WIDTH), lambda i: (i, 0)), pl.BlockSpec((MIX_WIDTH, D), lambda i: (0, 0)),
                row, vec, vec]
    f32s, b16s = jax.ShapeDtypeStruct((S_, D), F32), jax.ShapeDtypeStruct((S_, D), BF16)
    v32s = jax.ShapeDtypeStruct((1, D), F32)
    args = (ymix, wo, x, g.reshape(1, D), b.reshape(1, D))
    if last:
        return _pcall(body, name=name, grid=(S_ // tm,), in_specs=in_specs + [row],
                      out_specs=[row, row, vec, vec, one],
                      out_shape=[f32s, b16s, v32s, v32s, jax.ShapeDtypeStruct((1, 1), F32)],
                      compiler_params=_cp(("arbitrary",)))(*args, target)
    return _pcall(body, name=name, grid=(S_ // tm,), in_specs=in_specs, out_specs=[row, row, row],
                  out_shape=[f32s, f32s, b16s], compiler_params=_cp(("parallel",)))(*args)


def _ln_bwd(z, g, b, dy, *, tm, name):
    S_, D = z.shape

    def body(z_ref, g_ref, b_ref, o_ref, *outs):
        _ln_bwd_tile(z_ref[...], g_ref[...], b_ref[...], o_ref[...], False, *outs)

    row = pl.BlockSpec((tm, D), lambda i: (i, 0))
    vec = pl.BlockSpec((1, D), lambda i: (0, 0))
    one = pl.BlockSpec((1, 1), lambda i: (0, 0))
    return _pcall(
        body, name=name, grid=(S_ // tm,), in_specs=[row, vec, vec, row],
        out_specs=[row, row, vec, vec, one],
        out_shape=[jax.ShapeDtypeStruct((S_, D), F32), jax.ShapeDtypeStruct((S_, D), BF16),
                   jax.ShapeDtypeStruct((1, D), F32), jax.ShapeDtypeStruct((1, D), F32),
                   jax.ShapeDtypeStruct((1, 1), F32)],
        compiler_params=_cp(("arbitrary",)),
    )(z, g.reshape(1, D), b.reshape(1, D), dy)


def _attn_masks(i, sk_ref):
    ri = lax.broadcasted_iota(jnp.int32, (512, 256), 0)
    cj = lax.broadcasted_iota(jnp.int32, (512, 256), 1)
    diff = (ri & 127) - cj + 128
    band = (diff >= 0) & (diff < WINDOW)
    bias = jnp.where(band, 0.0, -jnp.inf)
    bias0 = jnp.where(band & ((i > 0) | (cj >= 128)), 0.0, -jnp.inf)
    grp = lax.broadcasted_iota(jnp.int32, (512, 1), 0) >> 7
    skvs = []
    for h in range(A_KV_HEADS):
        skv = jnp.zeros((512, 1), F32)
        for g in range(4):
            skv = jnp.where(grp == g, sk_ref[h * 4 + g], skv)
        skvs.append(skv)
    return bias0, bias, skvs


def _attn_common(masks, b, h, qr, kd, vd):
    lane = lax.broadcasted_iota(jnp.int32, (1, 128), 1)
    lof = (lane < 64).astype(F32)
    hif = 1.0 - lof
    r0 = b * 128
    skv = masks[2][h]
    pairs = [qr[r0:r0 + 128, h * 256 + p * 128:h * 256 + (p + 1) * 128] for p in (0, 1)]
    qs = _b(jnp.concatenate([pairs[0] * lof, pairs[0] * hif, pairs[1] * lof, pairs[1] * hif], 0))
    k2 = kd[h][r0:r0 + 256]
    v2 = vd[h][r0:r0 + 256]
    s = _nt(qs, k2) * (A_HEAD_DIM ** -0.5) + (masks[0] if b == 0 else masks[1])
    m = jnp.maximum(jnp.max(s, 1, keepdims=True), skv)
    p = jnp.exp(s - m)
    esk = jnp.exp(skv - m)
    rz = 1.0 / (jnp.sum(p, 1, keepdims=True) + esk)
    prob = p * rz
    o4 = _nn(_b(prob), v2)
    return lof, hif, qs, k2, v2, prob, esk * rz, o4


def _attn_prep(T, q_ref, k_ref, v_ref, c_ref, s_ref, kprev, vprev):
    C = c_ref[...]
    Sg = s_ref[...]
    C4 = jnp.concatenate([C] * 4, 1)
    S4 = jnp.concatenate([Sg] * 4, 1)
    q = q_ref[...]
    qr = q * C4 + _rot(q) * S4
    k = k_ref[...]
    kr = k * C + _rot(k) * Sg
    v = v_ref[...]
    kext = jnp.concatenate([kprev[...], kr], 0)
    vext = jnp.concatenate([vprev[...], v], 0)
    kprev[...] = kr[T - 128:]
    vprev[...] = v[T - 128:]
    lo = lax.broadcasted_iota(jnp.int32, (1, 128), 1) < 64
    kroll = pltpu.roll(kext, 64, 1)
    vroll = pltpu.roll(vext, 64, 1)
    kd = [_b(jnp.where(lo, kext, kroll)), _b(jnp.where(lo, kroll, kext))]
    vd = [_b(jnp.where(lo, vext, vroll)), _b(jnp.where(lo, vroll, vext))]
    return C, Sg, C4, S4, qr, kd, vd


def _attn_specs(T):
    return [pl.BlockSpec(memory_space=pltpu.SMEM),
            pl.BlockSpec((T, 512), lambda i: (i, OFF_AQ // 512)),
            pl.BlockSpec((T, 512), lambda i: (i, OFF_AZ // 512)),
            pl.BlockSpec((T, 128), lambda i: (i, OFF_AK // 128)),
            pl.BlockSpec((T, 128), lambda i: (i, OFF_AV // 128)),
            pl.BlockSpec((T, 128), lambda i: (i, 0)),
            pl.BlockSpec((T, 128), lambda i: (i, 0))]


def _attn_fwd(proj, rope_c, rope_s, sinks, ymix, *, T, name):
    S_ = proj.shape[0]
    nb = T // 128

    def body(sk_ref, q_ref, z_ref, k_ref, v_ref, c_ref, s_ref, _, y_ref, kprev, vprev):
        i = pl.program_id(0)

        @pl.when(i == 0)
        def _():
            kprev[...] = jnp.zeros_like(kprev)
            vprev[...] = jnp.zeros_like(vprev)

        _, _, _, _, qr, kd, vd = _attn_prep(T, q_ref, k_ref, v_ref, c_ref, s_ref, kprev, vprev)
        masks = _attn_masks(i, sk_ref)
        for b in range(nb):
            r0 = b * 128
            for h in range(2):
                lof, hif, _, _, _, _, _, o4 = _attn_common(masks, b, h, qr, kd, vd)
                for p in range(2):
                    cs = slice(h * 256 + p * 128, h * 256 + (p + 1) * 128)
                    o = o4[2 * p * 128:(2 * p + 1) * 128] * lof + o4[(2 * p + 1) * 128:(2 * p + 2) * 128] * hif
                    y_ref[r0:r0 + 128, cs] = (o * _silu(z_ref[r0:r0 + 128, cs])).astype(BF16)

    return _pcall(
        body, name=name, grid=(S_ // T,), in_specs=_attn_specs(T) + [ANY_SPEC],
        out_specs=pl.BlockSpec((T, 512), lambda i: (i, MIX_A // 512)),
        out_shape=jax.ShapeDtypeStruct(ymix.shape, BF16),
        input_output_aliases={7: 0},
        scratch_shapes=[pltpu.VMEM((128, 128), F32), pltpu.VMEM((128, 128), F32)],
        compiler_params=_cp(("arbitrary",)),
    )(sinks, proj, proj, proj, proj, rope_c, rope_s, ymix)


def _attn_bwd(proj, rope_c, rope_s, sinks, dymix, *, T, name):
    S_ = proj.shape[0]
    nb = T // 128
    nt = S_ // T

    def body(sk_ref, q_ref, z_ref, k_ref, v_ref, c_ref, s_ref, dy_ref,
             dp_ref, dk_ref, dv_ref, dkt_ref, dvt_ref, dsk_ref,
             kprev, vprev, cprev, sprev, dkacc, dvacc, dqacc):
        i = pl.program_id(0)

        @pl.when(i == 0)
        def _():
            kprev[...] = jnp.zeros_like(kprev)
            vprev[...] = jnp.zeros_like(vprev)
            cprev[...] = jnp.zeros_like(cprev)
            sprev[...] = jnp.zeros_like(sprev)
            dkacc[...] = jnp.zeros_like(dkacc)
            dvacc[...] = jnp.zeros_like(dvacc)
            dsk_ref[...] = jnp.zeros_like(dsk_ref)

        @pl.when(i > 0)
        def _():
            dkacc[0:128, :] = dkacc[T:T + 128, :]
            dvacc[0:128, :] = dvacc[T:T + 128, :]
            dkacc[128:, :] = jnp.zeros((T, 128), F32)
            dvacc[128:, :] = jnp.zeros((T, 128), F32)

        C, Sg, C4, S4, qr, kd, vd = _attn_prep(T, q_ref, k_ref, v_ref, c_ref, s_ref, kprev, vprev)
        masks = _attn_masks(i, sk_ref)
        lane = lax.broadcasted_iota(jnp.int32, (1, 128), 1)
        for b in range(nb):
            r0 = b * 128
            for h in range(2):
                lof, hif, qs, k2, v2, prob, psink, o4 = _attn_common(masks, b, h, qr, kd, vd)
                dos = []
                for p in range(2):
                    cs = slice(h * 256 + p * 128, h * 256 + (p + 1) * 128)
                    o = o4[2 * p * 128:(2 * p + 1) * 128] * lof + o4[(2 * p + 1) * 128:(2 * p + 2) * 128] * hif
                    zc = z_ref[r0:r0 + 128, cs]
                    dyc = dy_ref[r0:r0 + 128, cs]
                    dp_ref[r0:r0 + 128, 512 + cs.start:512 + cs.stop] = (dyc * o * _dsilu(zc)).astype(BF16)
                    do = dyc * _silu(zc)
                    dos += [do * lof, do * hif]
                dos = jnp.concatenate(dos, 0)
                os_ = jnp.concatenate([o4[0:128] * lof, o4[128:256] * hif, o4[256:384] * lof, o4[384:512] * hif], 0)
                delta = jnp.sum(dos * os_, 1, keepdims=True)
                dosb = _b(dos)
                dp = _nt(dosb, v2)
                ds = prob * (dp - delta)
                dsv = -psink * delta
                for g in range(4):
                    sg = jnp.sum(dsv[g * 128:(g + 1) * 128], 0, keepdims=True)
                    hd = h * 4 + g
                    dsk_ref[hd:hd + 1, :] += jnp.broadcast_to(sg, (1, 128))
                dsb = _b(ds * (A_HEAD_DIM ** -0.5))
                dqs = _nn(dsb, k2)
                for p in range(2):
                    cs = slice(h * 256 + p * 128, h * 256 + (p + 1) * 128)
                    dqacc[r0:r0 + 128, cs] = (dqs[2 * p * 128:(2 * p + 1) * 128] * lof
                                              + dqs[(2 * p + 1) * 128:(2 * p + 2) * 128] * hif)
                dkdup = _tn(dsb, qs)
                dvdup = _tn(_b(prob), dosb)
                half = (lane < 64) if h == 0 else (lane >= 64)
                dkacc[r0:r0 + 256, :] += jnp.where(half, dkdup + pltpu.roll(dkdup, 64, 1), 0.0)
                dvacc[r0:r0 + 256, :] += jnp.where(half, dvdup + pltpu.roll(dvdup, 64, 1), 0.0)
        dqr = dqacc[...]
        dp_ref[:, 0:512] = (dqr * C4 + _rot(dqr * S4)).astype(BF16)
        cext = jnp.concatenate([cprev[...], C], 0)
        sext = jnp.concatenate([sprev[...], Sg], 0)
        dke = dkacc[...]
        dkp = dke * cext + _rot(dke * sext)
        dk_ref[...] = dkp[0:T].astype(BF16)
        dkt_ref[...] = dkp[T:T + 128].astype(BF16)
        dve = dvacc[...]
        dv_ref[...] = dve[0:T].astype(BF16)
        dvt_ref[...] = dve[T:T + 128].astype(BF16)
        cprev[...] = C[T - 128:]
        sprev[...] = Sg[T - 128:]

    nar = pl.BlockSpec((T, 128), lambda i: (i, 0))
    tail = pl.BlockSpec((128, 128), lambda i: (0, 0))
    return _pcall(
        body, name=name, grid=(nt,),
        in_specs=_attn_specs(T) + [pl.BlockSpec((T, 512), lambda i: (i, MIX_A // 512))],
        out_specs=[pl.BlockSpec((T, 1024), lambda i: (i, OFF_AQ // 1024)), nar, nar, tail, tail,
                   pl.BlockSpec((8, 128), lambda i: (0, 0))],
        out_shape=[jax.ShapeDtypeStruct((S_, NP), BF16),
                   jax.ShapeDtypeStruct((S_, 128), BF16), jax.ShapeDtypeStruct((S_, 128), BF16),
                   jax.ShapeDtypeStruct((128, 128), BF16), jax.ShapeDtypeStruct((128, 128), BF16),
                   jax.ShapeDtypeStruct((8, 128), F32)],
        scratch_shapes=[pltpu.VMEM((128, 128), F32)] * 4
        + [pltpu.VMEM((T + 128, 128), F32), pltpu.VMEM((T + 128, 128), F32), pltpu.VMEM((T, 512), F32)],
        compiler_params=_cp(("arbitrary",)),
    )(sinks, proj, proj, proj, proj, rope_c, rope_s, dymix)


def _rg_gates(xr, wa_ref, ba_ref, wx_ref, bx_ref, lam_ref):
    xb = _b(xr)
    pre_a = jnp.concatenate([_nn(xb[:, n * 128:(n + 1) * 128], wa_ref[n]) for n in range(R_BLOCKS)], 1) + ba_ref[...]
    pre_x = jnp.concatenate([_nn(xb[:, n * 128:(n + 1) * 128], wx_ref[n]) for n in range(R_BLOCKS)], 1) + bx_ref[...]
    r = _sigmoid(pre_a)
    ig = _sigmoid(pre_x)
    sp = _softplus(-lam_ref[...])
    log_a = -R_C * r * sp
    a = jnp.exp(log_a)
    mult = jnp.sqrt(_one_minus_exp(2.0 * log_a))
    return xb, r, ig, sp, a, mult


def _rg_param_specs():
    C = R_WIDTH
    vec = pl.BlockSpec((1, C), lambda i: (0, 0))
    blk = pl.BlockSpec((R_BLOCKS, 128, 128), lambda i: (0, 0, 0))
    return [pl.BlockSpec((CONV_WIDTH, C), lambda i: (0, 0)), vec, blk, vec, blk, vec, vec]


def _rglru_fwd(proj, cw, cb, wa, ba, wx, bx, lam, *, T, name):
    S_ = proj.shape[0]
    C = R_WIDTH

    def body(rx_ref, rz_ref, cw_ref, cb_ref, wa_ref, ba_ref, wx_ref, bx_ref, lam_ref,
             h_ref, y_ref, halo, hcar):
        i = pl.program_id(0)

        @pl.when(i == 0)
        def _():
            halo[...] = jnp.zeros_like(halo)
            hcar[...] = jnp.zeros_like(hcar)

        rx = rx_ref[...]
        ext = jnp.concatenate([halo[...], rx], 0)
        halo[...] = rx[T - 8:]
        taps = _conv_taps(ext, T)
        xr = cb_ref[...] + sum(cw_ref[k:k + 1, :] * taps[k] for k in range(CONV_WIDTH))
        _, _, ig, _, a, mult = _rg_gates(xr, wa_ref, ba_ref, wx_ref, bx_ref, lam_ref)
        u = mult * (ig * xr)
        acum, hloc = _scan_lin(a, u, False)
        h = hloc + acum * hcar[0:1, :]
        hcar[...] = jnp.broadcast_to(h[T - 1:T, :], (8, C))
        h_ref[...] = h
        y_ref[...] = (h * _silu(rz_ref[...])).astype(BF16)

    row = pl.BlockSpec((T, C), lambda i: (i, 0))
    return _pcall(
        body, name=name, grid=(S_ // T,),
        in_specs=[pl.BlockSpec((T, C), lambda i: (i, OFF_RX // C)),
                  pl.BlockSpec((T, C), lambda i: (i, OFF_RZ // C))] + _rg_param_specs(),
        out_specs=[row, pl.BlockSpec((T, C), lambda i: (i, MIX_R // C))],
        out_shape=[jax.ShapeDtypeStruct((S_, C), F32), jax.ShapeDtypeStruct((S_, MIX_WIDTH), BF16)],
        scratch_shapes=[pltpu.VMEM((8, C), F32), pltpu.VMEM((8, C), F32)],
        compiler_params=_cp(("arbitrary",)),
    )(proj, proj, cw, cb.reshape(1, C), _b(wa), ba.reshape(1, C), _b(wx), bx.reshape(1, C), lam.reshape(1, C))


def _rglru_bwd(proj, h, dymix, dproj, cw, cb, wa, ba, wx, bx, lam, *, T, name):
    S_ = proj.shape[0]
    C = R_WIDTH
    nt = S_ // T
    t8 = T // 8

    def body(rx_ref, rxp_ref, rz_ref, h_ref, hp_ref, dy_ref,
             cw_ref, cb_ref, wa_ref, ba_ref, wx_ref, bx_ref, lam_ref, wat_ref, wxt_ref,
             _, dp_ref, dcw_ref, dcb_ref, dwa_ref, dba_ref, dwx_ref, dbx_ref, dlam_ref,
             afirst, gfirst, dhalo):
        i = pl.program_id(0)
        first_tile = (i == nt - 1)

        @pl.when(i == 0)
        def _():
            afirst[...] = jnp.zeros_like(afirst)
            gfirst[...] = jnp.zeros_like(gfirst)
            dhalo[...] = jnp.zeros_like(dhalo)
            for r in (dcw_ref, dcb_ref, dwa_ref, dba_ref, dwx_ref, dbx_ref, dlam_ref):
                r[...] = jnp.zeros_like(r)

        keep = jnp.where(first_tile, 0.0, 1.0)
        rx = rx_ref[...]
        ext = jnp.concatenate([rxp_ref[...] * keep, rx], 0)
        taps = _conv_taps(ext, T)
        xr = cb_ref[...] + sum(cw_ref[k:k + 1, :] * taps[k] for k in range(CONV_WIDTH))
        xb, r, ig, sp, a, mult = _rg_gates(xr, wa_ref, ba_ref, wx_ref, bx_ref, lam_ref)
        hh = h_ref[...]
        rz = rz_ref[...]
        dy = dy_ref[...]
        dp_ref[:, C:2 * C] = (dy * hh * _dsilu(rz)).astype(BF16)
        dh = dy * _silu(rz)
        row = lax.broadcasted_iota(jnp.int32, (T, 1), 0)
        c = jnp.where(row == T - 1, afirst[0:1, :], pltpu.roll(a, T - 1, 0))
        ccum, gloc = _scan_lin(c, dh, True)
        g = gloc + ccum * gfirst[0:1, :]
        afirst[...] = jnp.broadcast_to(a[0:1, :], (8, C))
        gfirst[...] = jnp.broadcast_to(g[0:1, :], (8, C))
        hprev = jnp.where(row == 0, hp_ref[7:8, :] * keep, pltpu.roll(hh, 1, 0))
        da = g * hprev
        gx = ig * xr
        dgx = g * mult
        dmult = g * gx
        dlog_a = da * a - dmult * (a * a) * lax.rsqrt(mult * mult)
        dpre_a = dlog_a * (-R_C * sp) * r * (1.0 - r)
        dpre_x = dgx * xr * ig * (1.0 - ig)
        dlam_ref[...] += jnp.sum(dlog_a * (-R_C * r), 0, keepdims=True) * (-_sigmoid(-lam_ref[...]))
        dab = _b(dpre_a)
        dxb = _b(dpre_x)
        dxr = dgx * ig + jnp.concatenate(
            [_nn(dab[:, n * 128:(n + 1) * 128], wat_ref[n]) + _nn(dxb[:, n * 128:(n + 1) * 128], wxt_ref[n])
             for n in range(R_BLOCKS)], 1)
        for n in range(R_BLOCKS):
            cs = slice(n * 128, (n + 1) * 128)
            dwa_ref[n] += _tn(xb[:, cs], dab[:, cs])
            dwx_ref[n] += _tn(xb[:, cs], dxb[:, cs])
        dba_ref[...] += jnp.sum(dpre_a, 0, keepdims=True)
        dbx_ref[...] += jnp.sum(dpre_x, 0, keepdims=True)
        dcb_ref[...] += jnp.sum(dxr, 0, keepdims=True)
        for k in range(CONV_WIDTH):
            dcw_ref[k:k + 1, :] += jnp.sum(dxr * taps[k], 0, keepdims=True)
        ext2 = jnp.concatenate([dxr, dhalo[...]], 0)
        tt = _conv_taps_t(ext2, T)
        dp_ref[:, 0:C] = sum(cw_ref[k:k + 1, :] * tt[k] for k in range(CONV_WIDTH)).astype(BF16)
        dhalo[...] = dxr[0:8]

    def rev(i):
        return nt - 1 - i

    def prev8(i):
        return jnp.maximum(rev(i) * t8 - 1, 0)

    vec = pl.BlockSpec((1, C), lambda i: (0, 0))
    blk = pl.BlockSpec((R_BLOCKS, 128, 128), lambda i: (0, 0, 0))
    row = pl.BlockSpec((T, C), lambda i: (rev(i), 0))
    wat = _b(jnp.swapaxes(wa, 1, 2))
    wxt = _b(jnp.swapaxes(wx, 1, 2))
    return _pcall(
        body, name=name, grid=(nt,),
        in_specs=[pl.BlockSpec((T, C), lambda i: (rev(i), OFF_RX // C)),
                  pl.BlockSpec((8, C), lambda i: (prev8(i), OFF_RX // C)),
                  pl.BlockSpec((T, C), lambda i: (rev(i), OFF_RZ // C)),
                  row,
                  pl.BlockSpec((8, C), lambda i: (prev8(i), 0)),
                  pl.BlockSpec((T, C), lambda i: (rev(i), MIX_R // C)),
                  ] + _rg_param_specs() + [blk, blk, ANY_SPEC],
        out_specs=[pl.BlockSpec((T, 2 * C), lambda i: (rev(i), OFF_RX // (2 * C))),
                   pl.BlockSpec((CONV_WIDTH, C), lambda i: (0, 0)), vec, blk, vec, blk, vec, vec],
        out_shape=[jax.ShapeDtypeStruct(dproj.shape, BF16),
                   jax.ShapeDtypeStruct((CONV_WIDTH, C), F32), jax.ShapeDtypeStruct((1, C), F32),
                   jax.ShapeDtypeStruct((R_BLOCKS, 128, 128), F32), jax.ShapeDtypeStruct((1, C), F32),
                   jax.ShapeDtypeStruct((R_BLOCKS, 128, 128), F32), jax.ShapeDtypeStruct((1, C), F32),
                   jax.ShapeDtypeStruct((1, C), F32)],
        input_output_aliases={15: 0},
        scratch_shapes=[pltpu.VMEM((8, C), F32)] * 3,
        compiler_params=_cp(("arbitrary",)),
    )(proj, proj, proj, h, h, dymix, cw, cb.reshape(1, C), _b(wa), ba.reshape(1, C), _b(wx), bx.reshape(1, C),
      lam.reshape(1, C), wat, wxt, dproj)


GW3 = 3 * G_WIDTH


def _lane_col(x, lane_idx):
    lane = lax.broadcasted_iota(jnp.int32, (1, x.shape[1]), 1)
    return jnp.sum(jnp.where(lane == lane_idx, x, 0.0), 1, keepdims=True)


def _gdn_pre(ext, T, cw_ref, gba, pv_ref):
    taps = _conv_taps(ext, T)
    c = sum(cw_ref[k:k + 1, :] * taps[k] for k in range(CONV_WIDTH))
    qkv = _silu(c)
    beta = _sigmoid(gba)
    sarg = gba + pv_ref[1:2, :]
    nea = -jnp.exp(pv_ref[0:1, :])
    gdec = nea * _softplus(sarg)
    ri = lax.broadcasted_iota(jnp.int32, (T, T), 0)
    cj = lax.broadcasted_iota(jnp.int32, (T, T), 1)
    same = (ri >> 6) == (cj >> 6)
    ltri = jnp.where((ri >= cj) & same, 1.0, 0.0).astype(BF16)
    gc = _dot_exact_lhs(_nn, ltri, gdec)
    return taps, c, qkv, beta, sarg, nea, gdec, gc


def _gdn_masks():
    ri = lax.broadcasted_iota(jnp.int32, (128, 128), 0)
    cj = lax.broadcasted_iota(jnp.int32, (128, 128), 1)
    same = (ri >> 6) == (cj >> 6)
    return (ri >= cj) & same, (ri > cj) & same, ri == cj


def _lockstep(gens):
    out = [None] * len(gens)
    live = list(range(len(gens)))
    while live:
        still = []
        for k in live:
            try:
                next(gens[k])
                still.append(k)
            except StopIteration as stop:
                out[k] = stop.value
        live = still
    return out


def _gdn_chunk(qkv, beta, gc, rs, h, tm=None):
    tril, strict, eye = _gdn_masks()
    rowi = lax.broadcasted_iota(jnp.int32, (128, 1), 0)
    lane = lax.broadcasted_iota(jnp.int32, (1, 128), 1)
    qh = qkv[rs, h * 128:(h + 1) * 128]
    kh = qkv[rs, 512 + h * 128:512 + (h + 1) * 128]
    vh = qkv[rs, 1024 + h * 128:1024 + (h + 1) * 128]
    rq = lax.rsqrt(jnp.sum(qh * qh, 1, keepdims=True) + RMS_EPS)
    rk = lax.rsqrt(jnp.sum(kh * kh, 1, keepdims=True) + RMS_EPS)
    qn = qh * (rq * (G_HEAD_DIM ** -0.5))
    kn = kh * rk
    gcb = gc[rs]
    gcol = _lane_col(gcb, 4 + h)
    bcol = _lane_col(beta[rs], h)
    grow = _dot_exact_lhs(_nt, jnp.ones((128, 128), BF16), jnp.where(lane == 4 + h, gcb, 0.0))
    D = jnp.where(tril, jnp.exp(jnp.minimum(gcol - grow, 0.0)), 0.0)
    kb = kn * bcol
    vb = vh * bcol
    knb = _b(kn)
    A = _nt(_b(kb), knb)
    Bm = _nt(_b(qn), knb)
    yield
    if tm is None:
        N = jnp.where(strict, -(A * D), 0.0)
        tm = jnp.where(eye, 1.0, 0.0) + N
        npow = N
        for _ in range(5):
            npow = _dot3(_nn, npow, npow)
            yield
            tm = tm + _dot3(_nn, tm, npow)
            yield
    eg = jnp.exp(gcol)
    u = _dot3(_nn, tm, vb)
    w = _dot3(_nn, tm, kb * eg)
    yield
    qk = jnp.where(tril, Bm * D, 0.0)
    qd = qn * eg
    gla = jnp.sum(jnp.where(rowi == 63, gcol, 0.0), 0, keepdims=True)
    glb = jnp.sum(jnp.where(rowi == 127, gcol, 0.0), 0, keepdims=True)
    ed = jnp.exp(jnp.where(rowi < 64, gla, glb) - gcol)
    kd = kn * ed
    return dict(qh=qh, kh=kh, vh=vh, rq=rq, rk=rk, qn=qn, kn=kn, gcol=gcol, bcol=bcol, D=D, A=A, Bm=Bm,
                tm=tm, eg=eg, ed=ed, u=u, w=w, qk=qk, qd=qd, kd=kd, kb=kb, vb=vb,
                gla=jnp.exp(gla), glb=jnp.exp(glb))


def _gdn_scan(q, sa):
    sab = _b(sa)
    wb = _b(q["w"])
    vna = q["u"] - _nn(wb, sab)
    yield
    sb = sa * q["gla"] + _tn(_b(q["kd"][0:64]), _b(vna[0:64]))
    yield
    sbb = _b(sb)
    vnb = q["u"] - _nn(wb, sbb)
    yield
    sn = sb * q["glb"] + _tn(_b(q["kd"][64:128]), _b(vnb[64:128]))
    yield
    vn = jnp.concatenate([vna[0:64], vnb[64:128]], 0)
    qdb = _b(q["qd"])
    o = jnp.concatenate([_nn(qdb[0:64], sab), _nn(qdb[64:128], sbb)], 0) + _nn(_b(q["qk"]), _b(vn))
    return sb, sn, vn, o


def _gdn_param_specs():
    return [pl.BlockSpec((CONV_WIDTH, GW3), lambda i: (0, 0)),
            pl.BlockSpec((8, 128), lambda i: (0, 0)),
            pl.BlockSpec((1, 128), lambda i: (0, 0))]


def _gdn_pvec(a_log, dt_bias):
    z = jnp.zeros((8, 128), F32)
    return z.at[0, 4:8].set(a_log).at[1, 4:8].set(dt_bias)


def _gdn_fwd(proj, cw, a_log, dt_bias, nw, ymix, *, T, name):
    S_ = proj.shape[0]
    nu = T // 128

    def body(x_ref, z_ref, g_ref, cw_ref, pv_ref, nw_ref, _, y_ref, st_ref, tm_ref, halo, state):
        i = pl.program_id(0)

        @pl.when(i == 0)
        def _():
            halo[...] = jnp.zeros_like(halo)
            state[...] = jnp.zeros_like(state)

        x = x_ref[...]
        ext = jnp.concatenate([halo[...], x], 0)
        halo[...] = x[T - 8:]
        _, _, qkv, beta, _, _, _, gc = _gdn_pre(ext, T, cw_ref, g_ref[...], pv_ref)
        items = [(dc, h) for dc in range(nu) for h in range(G_HEADS)]
        qs = _lockstep([_gdn_chunk(qkv, beta, gc, slice(dc * 128, (dc + 1) * 128), h) for dc, h in items])

        def head_chain(h):
            s = state[h]
            for dc in range(nu):
                rs = slice(dc * 128, (dc + 1) * 128)
                q = qs[dc * G_HEADS + h]
                sb, sn, _, o = yield from _gdn_scan(q, s)
                st_ref[2 * dc, h] = s
                st_ref[2 * dc + 1, h] = sb
                tm_ref[dc, h] = q["tm"]
                s = sn
                yield
                rn = lax.rsqrt(jnp.mean(o * o, 1, keepdims=True) + RMS_EPS)
                cs = slice(h * 128, (h + 1) * 128)
                y_ref[rs, cs] = (o * rn * nw_ref[...] * _silu(z_ref[rs, cs])).astype(BF16)
                yield
            state[h] = s

        _lockstep([head_chain(h) for h in range(G_HEADS)])

    return _pcall(
        body, name=name, grid=(S_ // T,),
        in_specs=[pl.BlockSpec((T, GW3), lambda i: (i, OFF_GQKV // GW3)),
                  pl.BlockSpec((T, 512), lambda i: (i, OFF_GZ // 512)),
                  pl.BlockSpec((T, 128), lambda i: (i, OFF_GBA // 128))] + _gdn_param_specs() + [ANY_SPEC],
        out_specs=[pl.BlockSpec((T, 512), lambda i: (i, MIX_G // 512)),
                   pl.BlockSpec((2 * nu, G_HEADS, 128, 128), lambda i: (i, 0, 0, 0)),
                   pl.BlockSpec((nu, G_HEADS, 128, 128), lambda i: (i, 0, 0, 0))],
        out_shape=[jax.ShapeDtypeStruct(ymix.shape, BF16),
                   jax.ShapeDtypeStruct((S_ // 64, G_HEADS, 128, 128), F32),
                   jax.ShapeDtypeStruct((S_ // 128, G_HEADS, 128, 128), F32)],
        input_output_aliases={6: 0},
        scratch_shapes=[pltpu.VMEM((8, GW3), F32), pltpu.VMEM((G_HEADS, 128, 128), F32)],
        compiler_params=_cp(("arbitrary",)),
    )(proj, proj, proj, cw, _gdn_pvec(a_log, dt_bias), nw.reshape(1, 128), ymix)


def _gdn_bwd(proj, states, tms, dymix, dproj, cw, a_log, dt_bias, nw, *, T, name):
    S_ = proj.shape[0]
    nt = S_ // T
    nu = T // 128
    t8 = T // 8

    def body(x_ref, xp_ref, z_ref, g_ref, st_ref, tm_ref, dy_ref, cw_ref, pv_ref, nw_ref, _,
             dp_ref, dg_ref, dcw_ref, dpv_ref, dnw_ref, dstate, dhalo, dqkv, dbg):
        i = pl.program_id(0)
        first_tile = (i == nt - 1)

        @pl.when(i == 0)
        def _():
            dstate[...] = jnp.zeros_like(dstate)
            dhalo[...] = jnp.zeros_like(dhalo)
            dcw_ref[...] = jnp.zeros_like(dcw_ref)
            dpv_ref[...] = jnp.zeros_like(dpv_ref)
            dnw_ref[...] = jnp.zeros_like(dnw_ref)

        keep = jnp.where(first_tile, 0.0, 1.0)
        ext = jnp.concatenate([xp_ref[...] * keep, x_ref[...]], 0)
        G = g_ref[...]
        taps, c, qkv, beta, sarg, nea, gdec, gc = _gdn_pre(ext, T, cw_ref, G, pv_ref)
        tril, strict, _ = _gdn_masks()
        rowi = lax.broadcasted_iota(jnp.int32, (128, 1), 0)
        lane = lax.broadcasted_iota(jnp.int32, (1, 128), 1)
        ones_b = jnp.ones((128, 128), BF16)
        nwv = nw_ref[...]
        items = [(dc, h) for dc in range(nu) for h in range(G_HEADS)]

        def recompute(dc, h):
            q = yield from _gdn_chunk(qkv, beta, gc, slice(dc * 128, (dc + 1) * 128), h, tm=tm_ref[dc, h])
            sa = st_ref[2 * dc, h]
            sb, _, vn, o = yield from _gdn_scan(q, sa)
            return q, sa, sb, vn, o

        fw = _lockstep([recompute(dc, h) for dc, h in items])
        chain_out = {}

        def head_chain(h):
            dS = dstate[h]
            for dc in reversed(range(nu)):
                rs = slice(dc * 128, (dc + 1) * 128)
                q, sa, sb, vn, o = fw[dc * G_HEADS + h]
                cs = slice(h * 128, (h + 1) * 128)
                zg = z_ref[rs, cs]
                dy = dy_ref[rs, cs]
                rn = lax.rsqrt(jnp.mean(o * o, 1, keepdims=True) + RMS_EPS)
                don = dy * _silu(zg)
                dp_ref[rs, GW3 + cs.start:GW3 + cs.stop] = (dy * (o * rn * nwv) * _dsilu(zg)).astype(BF16)
                dnw_ref[...] += jnp.sum(don * o * rn, 0, keepdims=True)
                tt = don * nwv
                do = rn * (tt - o * (rn * rn) * jnp.mean(tt * o, 1, keepdims=True))
                yield
                dob = _b(do)
                sab, sbb = _b(sa), _b(sb)
                vnb16 = _b(vn)
                dqk = jnp.where(tril, _nt(dob, vnb16), 0.0)
                dvn_o = _tn(_b(q["qk"]), dob)
                dSb16 = _b(dS)
                kdb = _b(q["kd"])
                wb = _b(q["w"])
                qdb = _b(q["qd"])
                yield
                dvn_b = dvn_o[64:128] + _nn(kdb[64:128], dSb16)
                dkd_b = _nt(vnb16[64:128], dSb16)
                dgl_b = jnp.sum(jnp.sum(dS * sb, 1, keepdims=True), 0, keepdims=True)
                yield
                dvn_b16 = _b(dvn_b)
                dw_b = -_nt(dvn_b16, sbb)
                dqd_b = _nt(dob[64:128], sbb)
                dSm = q["glb"] * dS + _tn(qdb[64:128], dob[64:128]) - _tn(wb[64:128], dvn_b16)
                yield
                dSm16 = _b(dSm)
                dvn_a = dvn_o[0:64] + _nn(kdb[0:64], dSm16)
                dkd_a = _nt(vnb16[0:64], dSm16)
                dgl_a = jnp.sum(jnp.sum(dSm * sa, 1, keepdims=True), 0, keepdims=True)
                yield
                dvn_a16 = _b(dvn_a)
                dw_a = -_nt(dvn_a16, sab)
                dqd_a = _nt(dob[0:64], sab)
                dS = q["gla"] * dSm + _tn(qdb[0:64], dob[0:64]) - _tn(wb[0:64], dvn_a16)
                chain_out[dc, h] = (dqk, jnp.concatenate([dvn_a, dvn_b], 0), jnp.concatenate([dw_a, dw_b], 0),
                                    jnp.concatenate([dkd_a, dkd_b], 0), jnp.concatenate([dqd_a, dqd_b], 0),
                                    dgl_a, dgl_b)
                yield
            dstate[h] = dS

        _lockstep([head_chain(h) for h in range(G_HEADS)])

        def local(dc, h):
            rs = slice(dc * 128, (dc + 1) * 128)
            q = fw[dc * G_HEADS + h][0]
            dqk, du, dw, dkd, dqd, dgl_a, dgl_b = chain_out[dc, h]
            if True:
                dvb = _dot3(_tn, q["tm"], du)
                dkbe = _dot3(_tn, q["tm"], dw)
                yield
                dM = jnp.where(strict, -(_nt(_b(dvb), _b(q["u"])) + _nt(_b(dkbe), _b(q["w"]))), 0.0)
                yield
                D = q["D"]
                dA = dM * D
                dB = dqk * D
                dDD = (dM * q["A"] + dqk * q["Bm"]) * D
                dh_, dm_, dl_ = _split3(dDD)
                colsum = _tn(dh_, ones_b) + (_tn(dm_, ones_b) + _tn(dl_, ones_b))
                dgc = jnp.sum(dDD, 1, keepdims=True) - _lane_col(colsum, 0)
                yield
                dA16, dB16 = _b(dA), _b(dB)
                knb, kbb, qnb = _b(q["kn"]), _b(q["kb"]), _b(q["qn"])
                eg, ed = q["eg"], q["ed"]
                dkb = _nn(dA16, knb) + dkbe * eg
                dkn = _tn(dA16, kbb) + _tn(dB16, qnb) + dkd * ed + dkb * q["bcol"]
                dqn = _nn(dB16, knb) + dqd * eg
                yield
                deg = jnp.sum(dkbe * q["kb"], 1, keepdims=True) + jnp.sum(dqd * q["qn"], 1, keepdims=True)
                ded = jnp.sum(dkd * q["kn"], 1, keepdims=True) * ed
                dgc = dgc + deg * eg - ded
                tail_a = jnp.sum(jnp.where(rowi < 64, ded, 0.0), 0, keepdims=True) + dgl_a * q["gla"]
                tail_b = jnp.sum(jnp.where(rowi >= 64, ded, 0.0), 0, keepdims=True) + dgl_b * q["glb"]
                dgc = dgc + jnp.where(rowi == 63, tail_a, 0.0) + jnp.where(rowi == 127, tail_b, 0.0)
                dbeta = jnp.sum(dkb * q["kn"], 1, keepdims=True) + jnp.sum(dvb * q["vh"], 1, keepdims=True)
                bcol = q["bcol"]
                blk = jnp.where(lane == h, dbeta * bcol * (1.0 - bcol), 0.0) + jnp.where(lane == 4 + h, dgc, 0.0)
                yield
                sc = G_HEAD_DIM ** -0.5
                rq, rk, qh, kh = q["rq"], q["rk"], q["qh"], q["kh"]
                dqh = sc * (dqn * rq - qh * (rq * rq * rq) * jnp.sum(dqn * qh, 1, keepdims=True))
                dkh = dkn * rk - kh * (rk * rk * rk) * jnp.sum(dkn * kh, 1, keepdims=True)
                dqkv[rs, h * 128:(h + 1) * 128] = dqh
                dqkv[rs, 512 + h * 128:512 + (h + 1) * 128] = dkh
                dqkv[rs, 1024 + h * 128:1024 + (h + 1) * 128] = dvb * bcol
            return blk

        blks = _lockstep([local(dc, h) for dc, h in items])
        for dc in range(nu):
            dbg[dc * 128:(dc + 1) * 128, :] = functools.reduce(
                lambda a, b: a + b, [blks[dc * G_HEADS + h] for h in range(G_HEADS)])
        ri = lax.broadcasted_iota(jnp.int32, (T, T), 0)
        cj = lax.broadcasted_iota(jnp.int32, (T, T), 1)
        utri = jnp.where((ri <= cj) & ((ri >> 6) == (cj >> 6)), 1.0, 0.0).astype(BF16)
        dbgv = dbg[...]
        dgd = _dot_exact_lhs(_nn, utri, dbgv)
        is_g = (lane >= 4) & (lane < 8)
        dga = jnp.where(is_g, dgd * nea * _sigmoid(sarg), 0.0)
        dg_ref[...] = jnp.where(lane < 4, dbgv, dga).astype(BF16)
        dpv_ref[0:1, :] += jnp.sum(jnp.where(is_g, dgd * gdec, 0.0), 0, keepdims=True)
        dpv_ref[1:2, :] += jnp.sum(dga, 0, keepdims=True)
        dc_ = dqkv[...] * _dsilu(c)
        for k in range(CONV_WIDTH):
            dcw_ref[k:k + 1, :] += jnp.sum(dc_ * taps[k], 0, keepdims=True)
        ext2 = jnp.concatenate([dc_, dhalo[...]], 0)
        tt2 = _conv_taps_t(ext2, T)
        dp_ref[:, 0:GW3] = sum(cw_ref[k:k + 1, :] * tt2[k] for k in range(CONV_WIDTH)).astype(BF16)
        dhalo[...] = dc_[0:8]

    def rev(i):
        return nt - 1 - i

    def prev8(i):
        return jnp.maximum(rev(i) * t8 - 1, 0)

    return _pcall(
        body, name=name, grid=(nt,),
        in_specs=[pl.BlockSpec((T, GW3), lambda i: (rev(i), OFF_GQKV // GW3)),
                  pl.BlockSpec((8, GW3), lambda i: (prev8(i), OFF_GQKV // GW3)),
                  pl.BlockSpec((T, 512), lambda i: (rev(i), OFF_GZ // 512)),
                  pl.BlockSpec((T, 128), lambda i: (rev(i), OFF_GBA // 128)),
                  pl.BlockSpec((2 * nu, G_HEADS, 128, 128), lambda i: (rev(i), 0, 0, 0)),
                  pl.BlockSpec((nu, G_HEADS, 128, 128), lambda i: (rev(i), 0, 0, 0)),
                  pl.BlockSpec((T, 512), lambda i: (rev(i), MIX_G // 512))] + _gdn_param_specs() + [ANY_SPEC],
        out_specs=[pl.BlockSpec((T, GW3 + 512), lambda i: (rev(i), OFF_GQKV // (GW3 + 512))),
                   pl.BlockSpec((T, 128), lambda i: (rev(i), 0)),
                   pl.BlockSpec((CONV_WIDTH, GW3), lambda i: (0, 0)),
                   pl.BlockSpec((8, 128), lambda i: (0, 0)),
                   pl.BlockSpec((1, 128), lambda i: (0, 0))],
        out_shape=[jax.ShapeDtypeStruct(dproj.shape, BF16),
                   jax.ShapeDtypeStruct((S_, 128), BF16), jax.ShapeDtypeStruct((CONV_WIDTH, GW3), F32),
                   jax.ShapeDtypeStruct((8, 128), F32), jax.ShapeDtypeStruct((1, 128), F32)],
        input_output_aliases={10: 0},
        scratch_shapes=[pltpu.VMEM((G_HEADS, 128, 128), F32), pltpu.VMEM((8, GW3), F32),
                        pltpu.VMEM((T, GW3), F32), pltpu.VMEM((T, 128), F32)],
        compiler_params=_cp(("arbitrary",)),
    )(proj, proj, proj, proj, states, tms, dymix, cw, _gdn_pvec(a_log, dt_bias), nw.reshape(1, 128), dproj)


def _pair_sum_windows(a, b, nsh, width, *, out_dtype, name):
    R_, C = a.shape
    hr = R_ // 2
    nb = width // 128
    assert (3 * nsh) // 128 + nb <= C // 128
    to_perm = _orig_block_to_perm()
    blocks = jnp.asarray([to_perm[(nsh * t) // 128 + j] for t in range(4) for j in range(nb)], jnp.int32)
    table = jnp.concatenate([blocks, lax.axis_index("c").astype(jnp.int32)[None]])

    def body(tab_ref, a_ref, b_ref, o_ref):
        o_ref[...] = (a_ref[...] + b_ref[...]).astype(o_ref.dtype)

    return _pcall(
        body, name=name,
        grid_spec=pltpu.PrefetchScalarGridSpec(
            num_scalar_prefetch=1, grid=(4, nb),
            in_specs=[pl.BlockSpec((hr, 128), lambda t, j, tab: (tab[4 * nb], tab[t * nb + j])),
                      pl.BlockSpec((hr, 128), lambda t, j, tab: (0, tab[t * nb + j]))],
            out_specs=pl.BlockSpec((None, hr, 128), lambda t, j, tab: (t, 0, j))),
        out_shape=jax.ShapeDtypeStruct((4, hr, width), out_dtype),
        compiler_params=_cp(("parallel", "parallel")))(table, a, b)


def _pair_sum_blocks(a, b, *, out_dtype, name):
    L, R_, C = a.shape
    hr = R_ // 2

    def body(a0_ref, a1_ref, b_ref, o_ref):
        mine = jnp.where(lax.axis_index("c") == 0, a0_ref[...], a1_ref[...])
        o_ref[...] = (mine + b_ref[...]).astype(o_ref.dtype)

    def spec(half):
        return pl.BlockSpec((None, hr, C), lambda t: (t, half, 0))

    return _pcall(body, name=name, grid=(L,), in_specs=[spec(0), spec(1), spec(0)], out_specs=spec(0),
                  out_shape=jax.ShapeDtypeStruct((L, hr, C), out_dtype),
                  compiler_params=_cp(("parallel",)))(a, a, b)


def _sum4(a, mine, *, tr, name):
    _, R_, C = a.shape

    def body(a_ref, m_ref, o_ref):
        s = 2 * lax.axis_index("x") + lax.axis_index("y")
        mv = m_ref[...].astype(F32)
        p = [jnp.where(s == t, mv, a_ref[t].astype(F32)) for t in range(4)]
        o_ref[...] = ((p[0] + p[1]) + p[2]) + p[3]

    return _pcall(body, name=name, grid=(R_ // tr,),
                  in_specs=[pl.BlockSpec((4, tr, C), lambda i: (0, i, 0)), pl.BlockSpec((tr, C), lambda i: (i, 0))],
                  out_specs=pl.BlockSpec((tr, C), lambda i: (i, 0)),
                  out_shape=jax.ShapeDtypeStruct((R_, C), F32), compiler_params=_cp(("parallel",)))(a, mine)


def _adamw_refs(w_ref, g_ref, m_ref, v_ref, d_ref, mo_ref, vo_ref):
    c1 = 1.0 / (1.0 - ADAM_B1 ** ADAM_STEP)
    c2 = 1.0 / (1.0 - ADAM_B2 ** ADAM_STEP)
    gg = g_ref[...]
    mn = ADAM_B1 * m_ref[...] + (1.0 - ADAM_B1) * gg
    vn = ADAM_B2 * v_ref[...] + (1.0 - ADAM_B2) * (gg * gg)
    mo_ref[...] = mn
    vo_ref[...] = vn
    d_ref[...] = -ADAM_LR * ((mn * c1) / (jnp.sqrt(vn * c2) + ADAM_EPS) + ADAM_WD * w_ref[...])


def _adamw_many(ws, gs, ms, vs, *, name):
    n = len(ws)

    def body(*refs):
        for k in range(n):
            _adamw_refs(*[refs[q * n + k] for q in range(7)])

    vm = pl.BlockSpec(memory_space=pltpu.VMEM)
    shp = [jax.ShapeDtypeStruct(w.shape, F32) for w in ws]
    outs = _pcall(body, name=name, in_specs=[vm] * (4 * n), out_specs=[vm] * (3 * n), out_shape=shp * 3,
                  compiler_params=pltpu.CompilerParams(vmem_limit_bytes=VMEM_LIMIT))(*ws, *gs, *ms, *vs)
    return outs[:n], outs[n:2 * n], outs[2 * n:]


def _adamw(w, g, m, v, *, tr, name):
    L, R_, C = w.shape
    body = functools.partial(_adamw_refs)

    spec = pl.BlockSpec((None, tr, C), lambda l, i: (l, i, 0))
    shp = jax.ShapeDtypeStruct((L, R_, C), F32)
    return _pcall(body, name=name, grid=(L, R_ // tr), in_specs=[spec] * 4, out_specs=[spec] * 3,
                  out_shape=[shp] * 3, compiler_params=_cp(("parallel", "parallel")))(w, g, m, v)


def _adamw_cols(w, g, m, v, *, name):
    C, L, R_ = w.shape
    tc = C // 2 if C % 2 == 0 else C

    spec = pl.BlockSpec((tc, L, 128), lambda i, j: (i, 0, j))
    shp = jax.ShapeDtypeStruct((C, L, R_), F32)
    return _pcall(functools.partial(_adamw_refs), name=name, grid=(C // tc, R_ // 128), in_specs=[spec] * 4,
                  out_specs=[spec] * 3, out_shape=[shp] * 3,
                  compiler_params=_cp(("parallel", "parallel")))(w, g, m, v)


HBM_SPEC = pl.BlockSpec(memory_space=pltpu.HBM)


def _place():
    x, y, c = lax.axis_index("x"), lax.axis_index("y"), lax.axis_index("c")
    chips = [(1 - x, y), (x, 1 - y), (1 - x, 1 - y)]
    return x, y, c, 2 * x + y, chips, [2 * cx + cy for cx, cy in chips], (x, y, 1 - c)


def _remote(src, dst, ssem, rsem, dev):
    return pltpu.make_async_remote_copy(src_ref=src, dst_ref=dst, send_sem=ssem, recv_sem=rsem,
                                        device_id=dev, device_id_type=MESH)


def _row_half(ref, lead, hc):
    hl = ref.shape[-2] // 2
    return ref.at[lead, pl.ds(hc * hl, hl), :]


def _gather_side(items):
    n = len(items)

    def copies(ins, outs, ssem, rsem):
        x, y, c, s, chips, sid, sib = _place()
        cps = [_remote(_row_half(ins[k], items[k][1], c), _row_half(outs[k], s, c),
                       ssem.at[3 * k + j], rsem.at[3 * k + j], (*chip, c))
               for k in range(n) for j, chip in enumerate(chips)]
        return cps, c, sid, sib

    def start(ins, outs, ssem, rsem):
        for cp in copies(ins, outs, ssem, rsem)[0]:
            cp.start()

    def finish(ins, outs, ssem, rsem):
        cps, c, sid, sib = copies(ins, outs, ssem, rsem)
        for k in range(n):
            for j in range(3):
                got = _row_half(outs[k], sid[j], c)
                _remote(got, got, ssem.at[3 * k + j], rsem.at[3 * k + j], sib).wait_recv()
        for cp in cps:
            cp.wait_send()

    shapes = [jax.ShapeDtypeStruct((4,) + w.shape[1:], w.dtype) for w, _ in items]
    return _Side([w for w, _ in items], shapes, 3 * n, start, finish)


def _gather_join(gathered, name):
    n = len(gathered)

    def body(*refs):
        outs, ssem, rsem = refs[n:2 * n], refs[2 * n], refs[2 * n + 1]
        x, y, c, s, chips, sid, sib = _place()
        cps = []
        for k in range(n):
            for j in range(3):
                mine = _row_half(outs[k], sid[j], c)
                cps.append(_remote(mine, mine, ssem.at[3 * k + j], rsem.at[3 * k + j], sib))
        for cp in cps:
            cp.start()
        for k in range(n):
            for j in range(3):
                other = _row_half(outs[k], sid[j], 1 - c)
                _remote(other, other, ssem.at[3 * k + j], rsem.at[3 * k + j], sib).wait_recv()
        for cp in cps:
            cp.wait_send()

    return _pcall(
        body, name=name, in_specs=[HBM_SPEC] * n, out_specs=[HBM_SPEC] * n,
        out_shape=[jax.ShapeDtypeStruct(g.shape, g.dtype) for g in gathered],
        input_output_aliases={k: k for k in range(n)},
        scratch_shapes=[pltpu.SemaphoreType.DMA((3 * n,)), pltpu.SemaphoreType.DMA((3 * n,))],
    )(*gathered)


def _gather_layer0(win, conv):
    def body(win_ref, cv_ref, gin_ref, gcv_ref, ssem, rsem):
        x, y, c, s, chips, sid, sib = _place()

        def in_half(slot, hc):
            return _row_half(gin_ref, slot, hc)

        sends = []
        for j, chip in enumerate(chips):
            dev = (*chip, c)
            sends.append(_remote(_row_half(win_ref, 0, c), in_half(s, c), ssem.at[j], rsem.at[j], dev))
            sends.append(_remote(cv_ref, gcv_ref.at[s], ssem.at[3 + j], rsem.at[3 + j], dev))
        for cp in sends:
            cp.start()
        for j in range(3):
            _remote(in_half(sid[j], c), in_half(sid[j], c), ssem.at[j], rsem.at[j], sib).wait_recv()
            f = _remote(in_half(sid[j], c), in_half(sid[j], c), ssem.at[6 + j], rsem.at[6 + j], sib)
            f.start()
            sends.append(f)
        for j in range(3):
            _remote(in_half(sid[j], 1 - c), in_half(sid[j], 1 - c), ssem.at[6 + j], rsem.at[6 + j], sib).wait_recv()
            _remote(gcv_ref.at[sid[j]], gcv_ref.at[sid[j]], ssem.at[3 + j], rsem.at[3 + j], sib).wait_recv()
        for cp in sends:
            cp.wait_send()

    return _pcall(
        body, name="gather_layer0",
        in_specs=[HBM_SPEC] * 2, out_specs=[HBM_SPEC] * 2,
        out_shape=[jax.ShapeDtypeStruct((4,) + win.shape[1:], win.dtype),
                   jax.ShapeDtypeStruct((4,) + conv.shape, conv.dtype)],
        scratch_shapes=[pltpu.SemaphoreType.DMA((9,)), pltpu.SemaphoreType.DMA((9,))],
    )(win, conv)


def _swap_halves(arrs, axes, name):
    n = len(arrs)

    def half_shape(a, ax):
        return a.shape[:ax] + (a.shape[ax] // 2,) + a.shape[ax + 1:]

    def body(*refs):
        src, dst, ssem, rsem = refs[:n], refs[n:2 * n], refs[2 * n], refs[2 * n + 1]
        x, y, c, s, chips, sid, sib = _place()
        cps = []
        for k in range(n):
            hl = src[k].shape[axes[k]] // 2
            idx = [slice(None)] * len(src[k].shape)
            idx[axes[k]] = pl.ds((1 - c) * hl, hl)
            cps.append(_remote(src[k].at[tuple(idx)], dst[k], ssem.at[k], rsem.at[k], sib))
        for cp in cps:
            cp.start()
        for cp in cps:
            cp.wait()

    return _pcall(
        body, name=name, in_specs=[HBM_SPEC] * n, out_specs=[HBM_SPEC] * n,
        out_shape=[jax.ShapeDtypeStruct(half_shape(a, ax), a.dtype) for a, ax in zip(arrs, axes)],
        scratch_shapes=[pltpu.SemaphoreType.DMA((n,)), pltpu.SemaphoreType.DMA((n,))],
    )(*arrs)


def _swap_side(arrs, axes):
    n = len(arrs)

    def copies(ins, outs, ssem, rsem):
        x, y, c, s, chips, sid, sib = _place()
        cps = []
        for k in range(n):
            hl = ins[k].shape[axes[k]] // 2
            idx = [slice(None)] * len(ins[k].shape)
            idx[axes[k]] = pl.ds((1 - c) * hl, hl)
            cps.append(_remote(ins[k].at[tuple(idx)], outs[k], ssem.at[k], rsem.at[k], sib))
        return cps

    def start(ins, outs, ssem, rsem):
        for cp in copies(ins, outs, ssem, rsem):
            cp.start()

    def finish(ins, outs, ssem, rsem):
        for cp in copies(ins, outs, ssem, rsem):
            cp.wait()

    shapes = [jax.ShapeDtypeStruct(a.shape[:ax] + (a.shape[ax] // 2,) + a.shape[ax + 1:], a.dtype)
              for a, ax in zip(arrs, axes)]
    return _Side(list(arrs), shapes, n, start, finish)


def _chips_side(arrs, per_target):
    n = len(arrs)

    def copies(ins, outs, ssem, rsem):
        x, y, c, s, chips, sid, sib = _place()
        cps = [_remote(ins[k].at[sid[j]] if per_target[k] else ins[k], outs[k].at[s],
                       ssem.at[3 * k + j], rsem.at[3 * k + j], (*chip, c))
               for k in range(n) for j, chip in enumerate(chips)]
        return cps, sid, sib

    def start(ins, outs, ssem, rsem):
        for cp in copies(ins, outs, ssem, rsem)[0]:
            cp.start()

    def finish(ins, outs, ssem, rsem):
        cps, sid, sib = copies(ins, outs, ssem, rsem)
        for k in range(n):
            for j in range(3):
                got = outs[k].at[sid[j]]
                _remote(got, got, ssem.at[3 * k + j], rsem.at[3 * k + j], sib).wait_recv()
        for cp in cps:
            cp.wait_send()

    shapes = [jax.ShapeDtypeStruct(a.shape if pt else (4,) + a.shape, a.dtype) for a, pt in zip(arrs, per_target)]
    return _Side(list(arrs), shapes, 3 * n, start, finish)


def _swap_whole(arrs, name):
    n = len(arrs)

    def body(*refs):
        src, dst, ssem, rsem = refs[:n], refs[n:2 * n], refs[2 * n], refs[2 * n + 1]
        *_, sib = _place()
        cps = [_remote(src[k], dst[k], ssem.at[k], rsem.at[k], sib) for k in range(n)]
        for cp in cps:
            cp.start()
        for cp in cps:
            cp.wait()

    return _pcall(
        body, name=name, in_specs=[HBM_SPEC] * n, out_specs=[HBM_SPEC] * n,
        out_shape=[jax.ShapeDtypeStruct(a.shape, a.dtype) for a in arrs],
        scratch_shapes=[pltpu.SemaphoreType.DMA((n,)), pltpu.SemaphoreType.DMA((n,))],
    )(*arrs)


def _perm_cols(w):
    parts = [w[..., int(_ORIG_OFF[oi]):int(_ORIG_OFF[oi]) + IN_SIZES[oi]] for oi, _ in _PIECES]
    parts.append(jnp.zeros(w.shape[:-1] + (NP - N_IN,), w.dtype))
    return jnp.concatenate(parts, -1)


def _perm_rows(w):
    return jnp.concatenate([w[..., 512:1536, :], w[..., 0:512, :], w[..., 1536:2048, :]], -2)


_SMALL = ("sinks", "r_conv_b", "r_wa", "r_ba", "r_wx", "r_bx", "r_lam", "g_a_log", "g_dt_bias", "g_norm_w",
          "ln_g", "ln_b", "r_conv_w", "g_conv_w")
_PACK_ROWS = 16


def _piece_rows(n):
    return -(-n // (128 * _PACK_ROWS)) * _PACK_ROWS


def _pack(arrs):
    parts = []
    for a in arrs:
        n = int(np.prod(a.shape))
        rows = _piece_rows(n)
        if n % 128 == 0:
            blk = a.reshape(n // 128, 128)
        else:
            blk = jnp.pad(a.reshape(1, n), ((0, 0), (0, (-n) % 128))).reshape(-1, 128)
        if blk.shape[0] < rows:
            blk = jnp.pad(blk, ((0, rows - blk.shape[0]), (0, 0)))
        parts.append(blk)
    return jnp.concatenate(parts, 0)


def _unpack(packed, shapes):
    out = []
    r = 0
    for shp in shapes:
        n = int(np.prod(shp))
        if n % 128 == 0:
            out.append(packed[r:r + n // 128].reshape(shp))
        else:
            nr = -(-n // 128)
            out.append(packed[r:r + nr].reshape(1, nr * 128)[:, :n].reshape(shp))
        r += _piece_rows(n)
    return out


def _tile(n, t):
    return min(n, t)


def _layer_fwd(l, x, xb, wb, wob, ln, rope_c, rope_s, p, side=None, target=None):
    S_ = x.shape[0]
    proj = _matmul(xb, wb, ta=False, tb=False, tm=_tile(S_, 1024), tn=NP // 4, tk=wb.shape[0], out_dtype=F32,
                   name=f"in_proj_{l}", side=side)
    side_out = None
    if side:
        proj, side_out = proj
    h, ymix = _rglru_fwd(proj, p["r_conv_w"], p["r_conv_b"], p["r_wa"], p["r_ba"], p["r_wx"], p["r_bx"], p["r_lam"],
                         T=_tile(S_, 256), name=f"rglru_fwd_{l}")
    ymix = _attn_fwd(proj, rope_c, rope_s, p["sinks"], ymix, T=_tile(S_, 1024), name=f"attn_fwd_{l}")
    ymix, st, tms = _gdn_fwd(proj, p["g_conv_w"], p["g_a_log"], p["g_dt_bias"], p["g_norm_w"], ymix,
                             T=_tile(S_, 256), name=f"gdn_fwd_{l}")
    out = _outproj(ymix, wob(side_out), x, ln[0], ln[1], tm=_tile(S_, 512), name=f"out_proj_{l}", target=target)
    sv = dict(proj=proj, h=h, st=st, tms=tms, ymix=ymix)
    if target is None:
        sv["z"], sv["y"], sv["yb"] = out
    else:
        sv["head"] = out
    return sv


def _layer_bwd(l, sv, x_b, dz, dzb, wb, wob, rope_c, rope_s, p, side_dmix=None, side_dw_in=None, side_dx=None):
    S_, D = dz.shape
    proj = sv["proj"]
    dwo = _matmul(sv["ymix"], dzb, ta=True, tb=False, tm=512, tn=_tile(D, 2048), tk=_tile(S_, 2048),
                  out_dtype=F32, name=f"dw_out_{l}",
                  out_blocks=((MIX_WIDTH, D), (512, _tile(D, 2048)),
                              lambda i, j: (jnp.where(i == 3, 3, (i + 1) % 3), j)))
    side = side_dmix(dwo) if side_dmix else None
    dymix = _matmul(dzb, wob, ta=False, tb=True, tm=_tile(S_, 1024), tn=1024, tk=D, out_dtype=F32,
                    name=f"dmix_{l}", side=side)
    out_dmix = None
    if side:
        dymix, out_dmix = dymix
    dproj, dk, dv, dkt, dvt, dsk = _attn_bwd(proj, rope_c, rope_s, p["sinks"], dymix, T=_tile(S_, 1024),
                                             name=f"attn_bwd_{l}")
    (dproj, dcw_r, dcb_r, dwa, dba, dwx, dbx, dlam) = _rglru_bwd(
        proj, sv["h"], dymix, dproj, p["r_conv_w"], p["r_conv_b"], p["r_wa"], p["r_ba"], p["r_wx"], p["r_bx"],
        p["r_lam"], T=_tile(S_, 256), name=f"rglru_bwd_{l}")
    dproj, dgba, dcw_g, dpv, dnw = _gdn_bwd(proj, sv["st"], sv["tms"], dymix, dproj, p["g_conv_w"], p["g_a_log"],
                                            p["g_dt_bias"], p["g_norm_w"], T=_tile(S_, 256), name=f"gdn_bwd_{l}")
    tail = jnp.concatenate([dk[128:], dkt, dv[128:], dvt], 0).reshape(2, S_, 128)
    tail = jnp.concatenate([tail[0], tail[1], dgba, jnp.zeros((S_, NP - OFF_GBA - 128), BF16)], 1)
    dproj = lax.dynamic_update_slice(dproj, tail, (0, OFF_AK))
    small = dict(sinks=dsk[:, 0], r_conv_b=dcb_r[0], r_wa=dwa, r_ba=dba[0], r_wx=dwx, r_bx=dbx[0], r_lam=dlam[0],
                 g_a_log=dpv[0, 4:8], g_dt_bias=dpv[1, 4:8], g_norm_w=dnw[0], r_conv_w=dcw_r, g_conv_w=dcw_g)
    side = side_dw_in(small, dwo, out_dmix) if side_dw_in else None
    dwin = _matmul(x_b, dproj, ta=True, tb=False, tm=_tile(D, 1024), tn=NP // 4, tk=_tile(S_, 2048),
                   out_dtype=F32, name=f"dw_in_{l}", side=side)
    out_dw_in = None
    if side:
        dwin, out_dw_in = dwin
    side = side_dx(dwin) if side_dx else None
    dx_args = dict(ta=False, tb=True, tm=_tile(S_, 1024), tn=_tile(D, 1024), tk=NP // 2, out_dtype=F32, extra=dz,
                   alpha=DEEPNORM_ALPHA)
    out_dx = None
    if side:
        dx, out_dx = _matmul(dproj, wb, name=f"dx_{l}", side=side, **dx_args)
    else:
        dx = _matmul(dproj, wb, name=f"dx_{l}", **dx_args)
    return dx, dwin, dwo, small, out_dw_in, out_dx


def kernel(x, w_in, sinks, r_conv_w, r_conv_b, r_wa, r_ba, r_wx, r_bx, r_lam, g_conv_w, g_a_log, g_dt_bias, g_norm_w, w_out, ln_g, ln_b, loss_target, m_w_in, m_sinks, m_r_conv_w, m_r_conv_b, m_r_wa, m_r_ba, m_r_wx, m_r_bx, m_r_lam, m_g_conv_w, m_g_a_log, m_g_dt_bias, m_g_norm_w, m_w_out, m_ln_g, m_ln_b, v_w_in, v_sinks, v_r_conv_w, v_r_conv_b, v_r_wa, v_r_ba, v_r_wx, v_r_bx, v_r_lam, v_g_conv_w, v_g_a_log, v_g_dt_bias, v_g_norm_w, v_w_out, v_ln_g, v_ln_b):
    S_, D = x.shape[1], x.shape[2]
    nsh = w_in.shape[2]
    rsh = w_out.shape[1]
    cx, cy, cc = lax.axis_index("x"), lax.axis_index("y"), lax.axis_index("c")
    chip = 2 * cx + cy
    rcw_n, gcw_n = r_conv_w.shape[2], g_conv_w.shape[2]

    conv_pack = jnp.concatenate([r_conv_w, g_conv_w], 2)
    w_in_b, w_out_b = w_in.astype(BF16), w_out.astype(BF16)
    g_in0, g_conv = _gather_layer0(w_in_b, conv_pack)

    def shards(own, got):
        return [jnp.where(chip == t, own, got[t]) for t in range(4)]

    def w_in_of(l, g_in):
        return _perm_cols(jnp.concatenate(shards(w_in_b[l], g_in), 1))

    def w_out_of(l, g_out):
        return _perm_rows(jnp.concatenate(shards(w_out_b[l], g_out), 0))

    rcw = jnp.concatenate(shards(r_conv_w, g_conv[:, :, :, :rcw_n]), 2)
    gcw = jnp.concatenate(shards(g_conv_w, g_conv[:, :, :, rcw_n:]), 2)

    pos = jnp.arange(S_, dtype=F32)[:, None]
    inv = 1.0 / (ROPE_THETA ** (jnp.arange(0, A_HEAD_DIM, 2, dtype=F32) / A_HEAD_DIM))
    ang = pos * inv[None, :]
    cos, sin = jnp.cos(ang), jnp.sin(ang)
    rope_c = jnp.concatenate([cos, cos, cos, cos], 1)
    rope_s = jnp.concatenate([-sin, sin, -sin, sin], 1)

    def params(l):
        return dict(sinks=sinks[l], r_conv_w=rcw[l], r_conv_b=r_conv_b[l], r_wa=r_wa[l], r_ba=r_ba[l],
                    r_wx=r_wx[l], r_bx=r_bx[l], r_lam=r_lam[l], g_conv_w=gcw[l], g_a_log=g_a_log[l],
                    g_dt_bias=g_dt_bias[l], g_norm_w=g_norm_w[l])

    assert DEPTH == 2
    xb0 = x[0].astype(BF16)
    wb, wob = [w_in_of(0, g_in0), None], [None, None]
    late = {}

    def w_out_0(arrived):
        late["w_in_1"], g_out0 = _gather_join(arrived, "gather_join_0")
        wob[0] = w_out_of(0, g_out0)
        return wob[0]

    def w_out_1(arrived):
        wob[1] = w_out_of(1, _gather_join(arrived, "gather_join_1")[0])
        return wob[1]

    sv0 = _layer_fwd(0, x[0], xb0, wb[0], w_out_0, (ln_g[0], ln_b[0]), rope_c, rope_s, params(0),
                     side=_gather_side([(w_in_b, 1), (w_out_b, 0)]))
    wb[1] = w_in_of(1, late["w_in_1"])
    sv1 = _layer_fwd(1, sv0["y"], sv0["yb"], wb[1], w_out_1, (ln_g[1], ln_b[1]), rope_c, rope_s, params(1),
                     side=_gather_side([(w_out_b, 1)]), target=loss_target[0])
    saved, xbs = [sv0, sv1], [xb0, sv0["yb"]]

    tm_ln = _tile(S_, 512)
    dz, dzb, dg_l, db_l, loss_part = saved[-1]["head"]
    assert DEPTH == 2
    wcov = (-(-nsh // 128) + 1) * 128
    names = list(_SMALL)

    def own(a):
        return lax.dynamic_index_in_dim(a, chip, 0, keepdims=False)

    def sum_in(l, cp, arrived):
        return _sum4(arrived, own(cp), tr=_tile(D // 2, 256), name=f"chip_sum_w_in_{l}")

    def sum_out(l, cp, arrived):
        return _sum4(arrived, own(cp), tr=rsh // 2, name=f"chip_sum_w_out_{l}")

    dlng, dlnb = [None, dg_l[0]], [None, db_l[0]]
    dx, dwin1, dwo1, small1, _, _ = _layer_bwd(1, saved[1], xbs[1], dz, dzb, wb[1], wob[1], rope_c, rope_s, params(1))
    dwo1_4 = dwo1.reshape(4, rsh, D)
    dz, dzb, dg_l, db_l, _ = _ln_bwd(saved[0]["z"], ln_g[0], ln_b[0], dx, tm=tm_ln, name="ln_bwd_0")
    dlng[0], dlnb[0] = dg_l[0], db_l[0]

    held = {}

    def side_dmix(dwo0):
        return _swap_side([dwin1, dwo1_4, dwo0.reshape(4, rsh, D)], [0, 1, 1])

    def side_dw_in(small0, dwo0, got):
        sm = {k: jnp.stack([small0[k], small1[k]]) for k in small0}
        sm["ln_g"], sm["ln_b"] = jnp.stack(dlng), jnp.stack(dlnb)
        gs = _pack([sm[n] for n in names])
        (got_s,) = _swap_halves([gs], [0], "reduce_pair_small")
        held["in_cp1"] = _pair_sum_windows(dwin1, got[0], nsh, wcov, out_dtype=BF16, name="pair_sum_w_in_1")
        held["out_cp1"] = _pair_sum_blocks(dwo1_4, got[1], out_dtype=BF16, name="pair_sum_w_out_1")
        held["out_cp0"] = _pair_sum_blocks(dwo0.reshape(4, rsh, D), got[2], out_dtype=BF16, name="pair_sum_w_out_0")
        held["s_cp"] = _pair_sum_blocks(gs[None], got_s[None], out_dtype=F32, name="pair_sum_small")[0]
        held["shapes"] = [sm[n].shape for n in names]
        return _chips_side([held["in_cp1"], held["out_cp1"], held["out_cp0"]], [True, True, True])

    def side_dx(dwin0):
        got = _swap_halves([dwin0], [0], "reduce_pair_0b")
        held["in_cp0"] = _pair_sum_windows(dwin0, got[0], nsh, wcov, out_dtype=BF16, name="pair_sum_w_in_0")
        return _chips_side([held["in_cp0"], held["s_cp"]], [True, False])

    dx, _, _, _, arrived_a, arrived_b = _layer_bwd(0, saved[0], xbs[0], dz, dzb, wb[0], wob[0], rope_c, rope_s,
                                                   params(0), side_dmix=side_dmix, side_dw_in=side_dw_in,
                                                   side_dx=side_dx)
    grad_x = dx[None]
    loss = lax.psum(loss_part[0, 0], ("x", "y", "c"))
    s_cp = held["s_cp"]
    mine = [sum_in(0, held["in_cp0"], arrived_b[0]), sum_out(0, held["out_cp0"], arrived_a[2]),
            sum_in(1, held["in_cp1"], arrived_a[0]), sum_out(1, held["out_cp1"], arrived_a[1]),
            _sum4(arrived_b[1], s_cp, tr=s_cp.shape[0], name="chip_sum_small")]
    other = _swap_whole(mine, "reduce_join")

    def both(k, axis):
        return jnp.where(cc == 0, jnp.concatenate([mine[k], other[k]], axis),
                         jnp.concatenate([other[k], mine[k]], axis))

    g_w_in = lax.dynamic_slice_in_dim(jnp.stack([both(2 * l, 0) for l in range(DEPTH)]), (nsh * chip) % 128, nsh, 2)
    g_w_out = jnp.stack([both(2 * l + 1, 0) for l in range(DEPTH)])
    g_small = both(2 * DEPTH, 0)

    gsm = dict(zip(names, _unpack(g_small, held["shapes"])))
    gsm["r_conv_w"] = lax.dynamic_slice_in_dim(gsm["r_conv_w"], chip * rcw_n, rcw_n, 2)
    gsm["g_conv_w"] = lax.dynamic_slice_in_dim(gsm["g_conv_w"], chip * gcw_n, gcw_n, 2)
    wts = dict(sinks=sinks, r_conv_w=r_conv_w, r_conv_b=r_conv_b, r_wa=r_wa, r_ba=r_ba, r_wx=r_wx, r_bx=r_bx,
               r_lam=r_lam, g_conv_w=g_conv_w, g_a_log=g_a_log, g_dt_bias=g_dt_bias, g_norm_w=g_norm_w,
               ln_g=ln_g, ln_b=ln_b)
    mom = dict(sinks=m_sinks, r_conv_w=m_r_conv_w, r_conv_b=m_r_conv_b, r_wa=m_r_wa, r_ba=m_r_ba, r_wx=m_r_wx,
               r_bx=m_r_bx, r_lam=m_r_lam, g_conv_w=m_g_conv_w, g_a_log=m_g_a_log, g_dt_bias=m_g_dt_bias,
               g_norm_w=m_g_norm_w, ln_g=m_ln_g, ln_b=m_ln_b)
    vel = dict(sinks=v_sinks, r_conv_w=v_r_conv_w, r_conv_b=v_r_conv_b, r_wa=v_r_wa, r_ba=v_r_ba, r_wx=v_r_wx,
               r_bx=v_r_bx, r_lam=v_r_lam, g_conv_w=v_g_conv_w, g_a_log=v_g_a_log, g_dt_bias=v_g_dt_bias,
               g_norm_w=v_g_norm_w, ln_g=v_ln_g, ln_b=v_ln_b)
    d_s, m_s, v_s = _adamw_many(*[[d[n] for n in names] for d in (wts, gsm, mom, vel)], name="adamw_small")
    d_sm, m_sm, v_sm = (dict(zip(names, a)) for a in (d_s, m_s, v_s))
    def cols(a):
        return jnp.transpose(a, (2, 0, 1))

    g_w_in_t = cols(g_w_in)
    outs_t = _adamw_cols(cols(w_in), g_w_in_t, cols(m_w_in), cols(v_w_in), name="adamw_w_in")
    d_in, m_in, v_in = (jnp.transpose(a, (1, 2, 0)) for a in outs_t)
    g_w_in = jnp.transpose(g_w_in_t, (1, 2, 0))
    d_out, m_out, v_out = _adamw(w_out, g_w_out, m_w_out, v_w_out, tr=256, name="adamw_w_out")

    order = ["w_in", "sinks", "r_conv_w", "r_conv_b", "r_wa", "r_ba", "r_wx", "r_bx", "r_lam", "g_conv_w",
             "g_a_log", "g_dt_bias", "g_norm_w", "w_out", "ln_g", "ln_b"]
    grads = dict(gsm, w_in=g_w_in, w_out=g_w_out)
    deltas = dict(d_sm, w_in=d_in, w_out=d_out)
    new_m = dict(m_sm, w_in=m_in, w_out=m_out)
    new_v = dict(v_sm, w_in=v_in, w_out=v_out)
    return (loss, grad_x, *[grads[n] for n in order], *[deltas[n] for n in order],
            *[new_m[n] for n in order], *[new_v[n] for n in order])
```

```python
import functools

import jax
import jax.numpy as jnp
import numpy as np
from jax import lax
from jax.experimental import pallas as pl
from jax.experimental.pallas import tpu as pltpu

F32 = jnp.float32
BF16 = jnp.bfloat16
MESH = pl.DeviceIdType.MESH

DEPTH = 2
A_HEADS, A_KV_HEADS, A_HEAD_DIM = 8, 2, 64
A_WIDTH, A_KV_WIDTH = 512, 128
WINDOW = 128
ROPE_THETA = 10000.0
R_WIDTH, R_BLOCKS, R_BLOCK_DIM, R_C = 1024, 8, 128, 8.0
CONV_WIDTH = 4
G_HEADS, G_HEAD_DIM, G_WIDTH, G_CHUNK = 4, 128, 512, 64
MIX_WIDTH = 2048
IN_SIZES = (512, 128, 128, 512, 1024, 1024, 512, 512, 512, 512, 4, 4)
N_IN = 5384
DEEPNORM_ALPHA = (2 * DEPTH) ** 0.25
LN_EPS = 1e-5
RMS_EPS = 1e-6
ADAM_LR, ADAM_B1, ADAM_B2, ADAM_EPS, ADAM_WD, ADAM_STEP = 0.001, 0.9, 0.999, 1e-08, 0.01, 10

NP = 5632
OFF_GQKV, OFF_GZ, OFF_RX, OFF_RZ, OFF_AQ, OFF_AZ, OFF_AK, OFF_AV, OFF_GBA = (
    0, 1536, 2048, 3072, 4096, 4608, 5120, 5248, 5376)
_ORIG_OFF = np.concatenate([[0], np.cumsum(IN_SIZES)])[:-1]
_PIECES = ((6, OFF_GQKV), (7, OFF_GQKV + 512), (8, OFF_GQKV + 1024), (9, OFF_GZ), (4, OFF_RX), (5, OFF_RZ),
           (0, OFF_AQ), (3, OFF_AZ), (1, OFF_AK), (2, OFF_AV), (10, OFF_GBA), (11, OFF_GBA + 4))


def _orig_block_to_perm():
    table = list(range(NP // 128))
    for oi, off in _PIECES:
        if IN_SIZES[oi] % 128 == 0:
            for k in range(IN_SIZES[oi] // 128):
                table[int(_ORIG_OFF[oi]) // 128 + k] = off // 128 + k
    return table
MIX_R, MIX_A, MIX_G = 0, 1024, 1536
VMEM_LIMIT = 56 * 1024 * 1024
ANY_SPEC = pl.BlockSpec(memory_space=pl.ANY)


def _pcall(body, **kw):
    return pl.pallas_call(body, **kw)


def _cp(sem, limit=VMEM_LIMIT):
    return pltpu.CompilerParams(dimension_semantics=sem, vmem_limit_bytes=limit)


def _sigmoid(x):
    return 0.5 + 0.5 * jnp.tanh(0.5 * x)


def _silu(x):
    return x * _sigmoid(x)


def _dsilu(x):
    s = _sigmoid(x)
    return s * (1.0 + x * (1.0 - s))


def _log1p(x):
    u = 1.0 + x
    d = jnp.where(u == 1.0, 1.0, u - 1.0)
    return jnp.where(u == 1.0, x, jnp.log(u) * (x / d))


def _softplus(x):
    return jnp.maximum(x, 0.0) + _log1p(jnp.exp(-jnp.abs(x)))


def _one_minus_exp(x):
    series = -x * (1.0 + x * (0.5 + x * (1.0 / 6.0 + x * (1.0 / 24.0))))
    return jnp.where(x > -0.05, series, 1.0 - jnp.exp(x))


def _nn(a, b):
    return lax.dot_general(a, b, (((1,), (0,)), ((), ())), preferred_element_type=F32)


def _nt(a, b):
    return lax.dot_general(a, b, (((1,), (1,)), ((), ())), preferred_element_type=F32)


def _tn(a, b):
    return lax.dot_general(a, b, (((0,), (0,)), ((), ())), preferred_element_type=F32)


def _b(x):
    return x.astype(BF16)


def _split3(x):
    hi = x.astype(BF16)
    r1 = x - hi.astype(F32)
    mid = r1.astype(BF16)
    lo = (r1 - mid.astype(F32)).astype(BF16)
    return hi, mid, lo


def _dot3(f, a, b):
    ah, am, _ = _split3(a)
    bh, bm, _ = _split3(b)
    return f(ah, bh) + (f(ah, bm) + f(am, bh))


def _dot_exact_lhs(f, a_bf16, b):
    bh, bm, bl = _split3(b)
    return f(a_bf16, bh) + (f(a_bf16, bm) + f(a_bf16, bl))


def _rot(x):
    w = x.shape[-1]
    lane = lax.broadcasted_iota(jnp.int32, (1, w), 1)
    return jnp.where((lane & 63) < 32, pltpu.roll(x, w - 32, 1), pltpu.roll(x, 32, 1))


def _conv_taps(ext, n):
    return [pltpu.roll(ext, 3 - k, 0)[8:8 + n] if k < 3 else ext[8:8 + n] for k in range(CONV_WIDTH)]


def _conv_taps_t(ext, n):
    m = ext.shape[0]
    return [pltpu.roll(ext, m - (3 - k), 0)[0:n] if k < 3 else ext[0:n] for k in range(CONV_WIDTH)]


def _scan_steps(a, b, pos, span, shifts, reverse):
    n = a.shape[0]
    for s in shifts:
        if reverse:
            a_sh = pltpu.roll(a, n - s, 0)
            b_sh = pltpu.roll(b, n - s, 0)
            ok = pos < (span - s)
        else:
            a_sh = pltpu.roll(a, s, 0)
            b_sh = pltpu.roll(b, s, 0)
            ok = pos >= s
        b = jnp.where(ok, a * b_sh + b, b)
        a = jnp.where(ok, a * a_sh, a)
    return a, b


def _scan_lin(a, b, reverse):
    n = a.shape[0]
    shifts = []
    s = 1
    while s < n:
        shifts.append(s)
        s *= 2
    return _scan_steps(a, b, lax.broadcasted_iota(jnp.int32, (n, 1), 0), n, shifts, reverse)


class _Side:
    def __init__(self, inputs, out_shapes, n_sems, start, finish):
        self.inputs, self.out_shapes, self.n_sems, self.start, self.finish = inputs, out_shapes, n_sems, start, finish


def _matmul(a, b, *, ta, tb, tm, tn, tk, out_dtype, name, extra=None, alpha=0.0, out_blocks=None, side=None):
    if ta:
        K, M = a.shape
    else:
        M, K = a.shape
    if tb:
        N, K2 = b.shape
    else:
        K2, N = b.shape
    assert K == K2 and M % tm == 0 and N % tn == 0 and K % tk == 0, (a.shape, b.shape, tm, tn, tk)
    nk = K // tk
    ca = 0 if ta else 1
    cb = 1 if tb else 0
    has_extra = extra is not None

    assert nk == 1 or out_dtype == F32
    n_in = 2 + int(has_extra)
    ns_in = len(side.inputs) if side else 0
    ns_out = len(side.out_shapes) if side else 0
    grid = (M // tm, N // tn, nk)

    def body(*refs):
        a_ref, b_ref = refs[0], refs[1]
        e_ref = refs[2] if has_extra else None
        o_ref = refs[n_in + ns_in]
        k = pl.program_id(2)
        if side:
            s_in = refs[n_in:n_in + ns_in]
            s_out = refs[n_in + ns_in + 1:n_in + ns_in + 1 + ns_out]
            ssem, rsem = refs[-2], refs[-1]
            i, j = pl.program_id(0), pl.program_id(1)

            @pl.when((i == 0) & (j == 0) & (k == 0))
            def _():
                side.start(s_in, s_out, ssem, rsem)

            @pl.when((i == grid[0] - 1) & (j == grid[1] - 1) & (k == grid[2] - 1))
            def _():
                side.finish(s_in, s_out, ssem, rsem)

        part = lax.dot_general(a_ref[...], b_ref[...], (((ca,), (cb,)), ((), ())), preferred_element_type=F32)
        if nk == 1:
            if e_ref is not None:
                part = part + alpha * e_ref[...]
            o_ref[...] = part.astype(o_ref.dtype)
            return

        @pl.when(k == 0)
        def _():
            o_ref[...] = part

        @pl.when((k > 0) & (k < nk - 1))
        def _():
            o_ref[...] += part

        @pl.when(k == nk - 1)
        def _():
            last = o_ref[...] + part
            if e_ref is not None:
                last = last + alpha * e_ref[...]
            o_ref[...] = last

    a_spec = (pl.BlockSpec((tk, tm), lambda i, j, k: (k, i)) if ta
              else pl.BlockSpec((tm, tk), lambda i, j, k: (i, k)))
    b_spec = (pl.BlockSpec((tn, tk), lambda i, j, k: (j, k)) if tb
              else pl.BlockSpec((tk, tn), lambda i, j, k: (k, j)))
    e_spec = pl.BlockSpec((tm, tn), lambda i, j, k: (i, j))
    if out_blocks is None:
        o_spec, o_shape = e_spec, (M, N)
    else:
        o_shape, o_block, o_map = out_blocks
        o_spec = pl.BlockSpec(o_block, lambda i, j, k: o_map(i, j))
    in_specs = [a_spec, b_spec] + ([e_spec] if has_extra else [])
    args = (a, b) + ((extra,) if has_extra else ())
    if not side:
        return _pcall(
            body, name=name, grid=grid, in_specs=in_specs, out_specs=o_spec,
            out_shape=jax.ShapeDtypeStruct(o_shape, out_dtype),
            compiler_params=_cp(("parallel", "parallel", "arbitrary")),
        )(*args)
    outs = _pcall(
        body, name=name, grid=grid, in_specs=in_specs + [HBM_SPEC] * ns_in,
        out_specs=[o_spec] + [HBM_SPEC] * ns_out,
        out_shape=[jax.ShapeDtypeStruct(o_shape, out_dtype)] + list(side.out_shapes),
        scratch_shapes=[pltpu.SemaphoreType.DMA((side.n_sems,)), pltpu.SemaphoreType.DMA((side.n_sems,))],
        compiler_params=_cp(("arbitrary", "arbitrary", "arbitrary")),
    )(*args, *side.inputs)
    return outs[0], outs[1:]


def _ln_stats(z):
    mu = jnp.mean(z, -1, keepdims=True)
    zc = z - mu
    var = jnp.mean(zc * zc, -1, keepdims=True)
    rstd = lax.rsqrt(var + LN_EPS)
    return zc * rstd, rstd


def _ln_bwd_tile(z, gam, bet, other, from_target, dz_ref, dzb_ref, dg_ref, db_ref, loss_ref):
    i = pl.program_id(0)

    @pl.when(i == 0)
    def _():
        dg_ref[...] = jnp.zeros_like(dg_ref)
        db_ref[...] = jnp.zeros_like(db_ref)
        loss_ref[...] = jnp.zeros_like(loss_ref)

    xh, rstd = _ln_stats(z)
    if from_target:
        err = xh * gam + bet - other
        per_tok = jnp.mean(err * err, -1, keepdims=True)
        loss_ref[...] += 0.5 * jnp.sum(per_tok, 0, keepdims=True)
        dy = err * (1.0 / z.shape[-1])
    else:
        dy = other
    dxh = dy * gam
    m1 = jnp.mean(dxh, -1, keepdims=True)
    m2 = jnp.mean(dxh * xh, -1, keepdims=True)
    dz = rstd * (dxh - m1 - xh * m2)
    dz_ref[...] = dz
    dzb_ref[...] = dz.astype(BF16)
    dg_ref[...] += jnp.sum(dy * xh, 0, keepdims=True)
    db_ref[...] += jnp.sum(dy, 0, keepdims=True)


def _outproj(ymix, wo, x, g, b, *, tm, name, target=None):
    S_, D = x.shape
    last = target is not None

    def body(*refs):
        y_ref, w_ref, x_ref, g_ref, b_ref = refs[:5]
        z = DEEPNORM_ALPHA * x_ref[...] + _nn(y_ref[...], w_ref[...])
        if last:
            _ln_bwd_tile(z, g_ref[...], b_ref[...], refs[5][...], True, *refs[6:])
            return
        z_ref, o_ref, ob_ref = refs[5:]
        z_ref[...] = z
        xh, _ = _ln_stats(z)
        y = xh * g_ref[...] + b_ref[...]
        o_ref[...] = y
        ob_ref[...] = y.astype(BF16)

    row = pl.BlockSpec((tm, D), lambda i: (i, 0))
    vec = pl.BlockSpec((1, D), lambda i: (0, 0))
    one = pl.BlockSpec((1, 1), lambda i: (0, 0))
    in_specs = [pl.BlockSpec((tm, MIX_WIDTH), lambda i: (i, 0)), pl.BlockSpec((MIX_WIDTH, D), lambda i: (0, 0)),
                row, vec, vec]
    f32s, b16s = jax.ShapeDtypeStruct((S_, D), F32), jax.ShapeDtypeStruct((S_, D), BF16)
    v32s = jax.ShapeDtypeStruct((1, D), F32)
    args = (ymix, wo, x, g.reshape(1, D), b.reshape(1, D))
    if last:
        return _pcall(body, name=name, grid=(S_ // tm,), in_specs=in_specs + [row],
                      out_specs=[row, row, vec, vec, one],
                      out_shape=[f32s, b16s, v32s, v32s, jax.ShapeDtypeStruct((1, 1), F32)],
                      compiler_params=_cp(("arbitrary",)))(*args, target)
    return _pcall(body, name=name, grid=(S_ // tm,), in_specs=in_specs, out_specs=[row, row, row],
                  out_shape=[f32s, f32s, b16s], compiler_params=_cp(("parallel",)))(*args)


def _ln_bwd(z, g, b, dy, *, tm, name):
    S_, D = z.shape

    def body(z_ref, g_ref, b_ref, o_ref, *outs):
        _ln_bwd_tile(z_ref[...], g_ref[...], b_ref[...], o_ref[...], False, *outs)

    row = pl.BlockSpec((tm, D), lambda i: (i, 0))
    vec = pl.BlockSpec((1, D), lambda i: (0, 0))
    one = pl.BlockSpec((1, 1), lambda i: (0, 0))
    return _pcall(
        body, name=name, grid=(S_ // tm,), in_specs=[row, vec, vec, row],
        out_specs=[row, row, vec, vec, one],
        out_shape=[jax.ShapeDtypeStruct((S_, D), F32), jax.ShapeDtypeStruct((S_, D), BF16),
                   jax.ShapeDtypeStruct((1, D), F32), jax.ShapeDtypeStruct((1, D), F32),
                   jax.ShapeDtypeStruct((1, 1), F32)],
        compiler_params=_cp(("arbitrary",)),
    )(z, g.reshape(1, D), b.reshape(1, D), dy)


def _attn_masks(i, sk_ref):
    ri = lax.broadcasted_iota(jnp.int32, (512, 256), 0)
    cj = lax.broadcasted_iota(jnp.int32, (512, 256), 1)
    diff = (ri & 127) - cj + 128
    band = (diff >= 0) & (diff < WINDOW)
    bias = jnp.where(band, 0.0, -jnp.inf)
    bias0 = jnp.where(band & ((i > 0) | (cj >= 128)), 0.0, -jnp.inf)
    grp = lax.broadcasted_iota(jnp.int32, (512, 1), 0) >> 7
    skvs = []
    for h in range(A_KV_HEADS):
        skv = jnp.zeros((512, 1), F32)
        for g in range(4):
            skv = jnp.where(grp == g, sk_ref[h * 4 + g], skv)
        skvs.append(skv)
    return bias0, bias, skvs


def _attn_common(masks, b, h, qr, kd, vd):
    lane = lax.broadcasted_iota(jnp.int32, (1, 128), 1)
    lof = (lane < 64).astype(F32)
    hif = 1.0 - lof
    r0 = b * 128
    skv = masks[2][h]
    pairs = [qr[r0:r0 + 128, h * 256 + p * 128:h * 256 + (p + 1) * 128] for p in (0, 1)]
    qs = _b(jnp.concatenate([pairs[0] * lof, pairs[0] * hif, pairs[1] * lof, pairs[1] * hif], 0))
    k2 = kd[h][r0:r0 + 256]
    v2 = vd[h][r0:r0 + 256]
    s = _nt(qs, k2) * (A_HEAD_DIM ** -0.5) + (masks[0] if b == 0 else masks[1])
    m = jnp.maximum(jnp.max(s, 1, keepdims=True), skv)
    p = jnp.exp(s - m)
    esk = jnp.exp(skv - m)
    rz = 1.0 / (jnp.sum(p, 1, keepdims=True) + esk)
    prob = p * rz
    o4 = _nn(_b(prob), v2)
    return lof, hif, qs, k2, v2, prob, esk * rz, o4


def _attn_prep(T, q_ref, k_ref, v_ref, c_ref, s_ref, kprev, vprev):
    C = c_ref[...]
    Sg = s_ref[...]
    C4 = jnp.concatenate([C] * 4, 1)
    S4 = jnp.concatenate([Sg] * 4, 1)
    q = q_ref[...]
    qr = q * C4 + _rot(q) * S4
    k = k_ref[...]
    kr = k * C + _rot(k) * Sg
    v = v_ref[...]
    kext = jnp.concatenate([kprev[...], kr], 0)
    vext = jnp.concatenate([vprev[...], v], 0)
    kprev[...] = kr[T - 128:]
    vprev[...] = v[T - 128:]
    lo = lax.broadcasted_iota(jnp.int32, (1, 128), 1) < 64
    kroll = pltpu.roll(kext, 64, 1)
    vroll = pltpu.roll(vext, 64, 1)
    kd = [_b(jnp.where(lo, kext, kroll)), _b(jnp.where(lo, kroll, kext))]
    vd = [_b(jnp.where(lo, vext, vroll)), _b(jnp.where(lo, vroll, vext))]
    return C, Sg, C4, S4, qr, kd, vd


def _attn_specs(T):
    return [pl.BlockSpec(memory_space=pltpu.SMEM),
            pl.BlockSpec((T, 512), lambda i: (i, OFF_AQ // 512)),
            pl.BlockSpec((T, 512), lambda i: (i, OFF_AZ // 512)),
            pl.BlockSpec((T, 128), lambda i: (i, OFF_AK // 128)),
            pl.BlockSpec((T, 128), lambda i: (i, OFF_AV // 128)),
            pl.BlockSpec((T, 128), lambda i: (i, 0)),
            pl.BlockSpec((T, 128), lambda i: (i, 0))]


def _attn_fwd(proj, rope_c, rope_s, sinks, ymix, *, T, name):
    S_ = proj.shape[0]
    nb = T // 128

    def body(sk_ref, q_ref, z_ref, k_ref, v_ref, c_ref, s_ref, _, y_ref, kprev, vprev):
        i = pl.program_id(0)

        @pl.when(i == 0)
        def _():
            kprev[...] = jnp.zeros_like(kprev)
            vprev[...] = jnp.zeros_like(vprev)

        _, _, _, _, qr, kd, vd = _attn_prep(T, q_ref, k_ref, v_ref, c_ref, s_ref, kprev, vprev)
        masks = _attn_masks(i, sk_ref)
        for b in range(nb):
            r0 = b * 128
            for h in range(2):
                lof, hif, _, _, _, _, _, o4 = _attn_common(masks, b, h, qr, kd, vd)
                for p in range(2):
                    cs = slice(h * 256 + p * 128, h * 256 + (p + 1) * 128)
                    o = o4[2 * p * 128:(2 * p + 1) * 128] * lof + o4[(2 * p + 1) * 128:(2 * p + 2) * 128] * hif
                    y_ref[r0:r0 + 128, cs] = (o * _silu(z_ref[r0:r0 + 128, cs])).astype(BF16)

    return _pcall(
        body, name=name, grid=(S_ // T,), in_specs=_attn_specs(T) + [ANY_SPEC],
        out_specs=pl.BlockSpec((T, 512), lambda i: (i, MIX_A // 512)),
        out_shape=jax.ShapeDtypeStruct(ymix.shape, BF16),
        input_output_aliases={7: 0},
        scratch_shapes=[pltpu.VMEM((128, 128), F32), pltpu.VMEM((128, 128), F32)],
        compiler_params=_cp(("arbitrary",)),
    )(sinks, proj, proj, proj, proj, rope_c, rope_s, ymix)


def _attn_bwd(proj, rope_c, rope_s, sinks, dymix, *, T, name):
    S_ = proj.shape[0]
    nb = T // 128
    nt = S_ // T

    def body(sk_ref, q_ref, z_ref, k_ref, v_ref, c_ref, s_ref, dy_ref,
             dp_ref, dk_ref, dv_ref, dkt_ref, dvt_ref, dsk_ref,
             kprev, vprev, cprev, sprev, dkacc, dvacc, dqacc):
        i = pl.program_id(0)

        @pl.when(i == 0)
        def _():
            kprev[...] = jnp.zeros_like(kprev)
            vprev[...] = jnp.zeros_like(vprev)
            cprev[...] = jnp.zeros_like(cprev)
            sprev[...] = jnp.zeros_like(sprev)
            dkacc[...] = jnp.zeros_like(dkacc)
            dvacc[...] = jnp.zeros_like(dvacc)
            dsk_ref[...] = jnp.zeros_like(dsk_ref)

        @pl.when(i > 0)
        def _():
            dkacc[0:128, :] = dkacc[T:T + 128, :]
            dvacc[0:128, :] = dvacc[T:T + 128, :]
            dkacc[128:, :] = jnp.zeros((T, 128), F32)
            dvacc[128:, :] = jnp.zeros((T, 128), F32)

        C, Sg, C4, S4, qr, kd, vd = _attn_prep(T, q_ref, k_ref, v_ref, c_ref, s_ref, kprev, vprev)
        masks = _attn_masks(i, sk_ref)
        lane = lax.broadcasted_iota(jnp.int32, (1, 128), 1)
        for b in range(nb):
            r0 = b * 128
            for h in range(2):
                lof, hif, qs, k2, v2, prob, psink, o4 = _attn_common(masks, b, h, qr, kd, vd)
                dos = []
                for p in range(2):
                    cs = slice(h * 256 + p * 128, h * 256 + (p + 1) * 128)
                    o = o4[2 * p * 128:(2 * p + 1) * 128] * lof + o4[(2 * p + 1) * 128:(2 * p + 2) * 128] * hif
                    zc = z_ref[r0:r0 + 128, cs]
                    dyc = dy_ref[r0:r0 + 128, cs]
                    dp_ref[r0:r0 + 128, 512 + cs.start:512 + cs.stop] = (dyc * o * _dsilu(zc)).astype(BF16)
                    do = dyc * _silu(zc)
                    dos += [do * lof, do * hif]
                dos = jnp.concatenate(dos, 0)
                os_ = jnp.concatenate([o4[0:128] * lof, o4[128:256] * hif, o4[256:384] * lof, o4[384:512] * hif], 0)
                delta = jnp.sum(dos * os_, 1, keepdims=True)
                dosb = _b(dos)
                dp = _nt(dosb, v2)
                ds = prob * (dp - delta)
                dsv = -psink * delta
                for g in range(4):
                    sg = jnp.sum(dsv[g * 128:(g + 1) * 128], 0, keepdims=True)
                    hd = h * 4 + g
                    dsk_ref[hd:hd + 1, :] += jnp.broadcast_to(sg, (1, 128))
                dsb = _b(ds * (A_HEAD_DIM ** -0.5))
                dqs = _nn(dsb, k2)
                for p in range(2):
                    cs = slice(h * 256 + p * 128, h * 256 + (p + 1) * 128)
                    dqacc[r0:r0 + 128, cs] = (dqs[2 * p * 128:(2 * p + 1) * 128] * lof
                                              + dqs[(2 * p + 1) * 128:(2 * p + 2) * 128] * hif)
                dkdup = _tn(dsb, qs)
                dvdup = _tn(_b(prob), dosb)
                half = (lane < 64) if h == 0 else (lane >= 64)
                dkacc[r0:r0 + 256, :] += jnp.where(half, dkdup + pltpu.roll(dkdup, 64, 1), 0.0)
                dvacc[r0:r0 + 256, :] += jnp.where(half, dvdup + pltpu.roll(dvdup, 64, 1), 0.0)
        dqr = dqacc[...]
        dp_ref[:, 0:512] = (dqr * C4 + _rot(dqr * S4)).astype(BF16)
        cext = jnp.concatenate([cprev[...], C], 0)
        sext = jnp.concatenate([sprev[...], Sg], 0)
        dke = dkacc[...]
        dkp = dke * cext + _rot(dke * sext)
        dk_ref[...] = dkp[0:T].astype(BF16)
        dkt_ref[...] = dkp[T:T + 128].astype(BF16)
        dve = dvacc[...]
        dv_ref[...] = dve[0:T].astype(BF16)
        dvt_ref[...] = dve[T:T + 128].astype(BF16)
        cprev[...] = C[T - 128:]
        sprev[...] = Sg[T - 128:]

    nar = pl.BlockSpec((T, 128), lambda i: (i, 0))
    tail = pl.BlockSpec((128, 128), lambda i: (0, 0))
    return _pcall(
        body, name=name, grid=(nt,),
        in_specs=_attn_specs(T) + [pl.BlockSpec((T, 512), lambda i: (i, MIX_A // 512))],
        out_specs=[pl.BlockSpec((T, 1024), lambda i: (i, OFF_AQ // 1024)), nar, nar, tail, tail,
                   pl.BlockSpec((8, 128), lambda i: (0, 0))],
        out_shape=[jax.ShapeDtypeStruct((S_, NP), BF16),
                   jax.ShapeDtypeStruct((S_, 128), BF16), jax.ShapeDtypeStruct((S_, 128), BF16),
                   jax.ShapeDtypeStruct((128, 128), BF16), jax.ShapeDtypeStruct((128, 128), BF16),
                   jax.ShapeDtypeStruct((8, 128), F32)],
        scratch_shapes=[pltpu.VMEM((128, 128), F32)] * 4
        + [pltpu.VMEM((T + 128, 128), F32), pltpu.VMEM((T + 128, 128), F32), pltpu.VMEM((T, 512), F32)],
        compiler_params=_cp(("arbitrary",)),
    )(sinks, proj, proj, proj, proj, rope_c, rope_s, dymix)


def _rg_gates(xr, wa_ref, ba_ref, wx_ref, bx_ref, lam_ref):
    xb = _b(xr)
    pre_a = jnp.concatenate([_nn(xb[:, n * 128:(n + 1) * 128], wa_ref[n]) for n in range(R_BLOCKS)], 1) + ba_ref[...]
    pre_x = jnp.concatenate([_nn(xb[:, n * 128:(n + 1) * 128], wx_ref[n]) for n in range(R_BLOCKS)], 1) + bx_ref[...]
    r = _sigmoid(pre_a)
    ig = _sigmoid(pre_x)
    sp = _softplus(-lam_ref[...])
    log_a = -R_C * r * sp
    a = jnp.exp(log_a)
    mult = jnp.sqrt(_one_minus_exp(2.0 * log_a))
    return xb, r, ig, sp, a, mult


def _rg_param_specs():
    C = R_WIDTH
    vec = pl.BlockSpec((1, C), lambda i: (0, 0))
    blk = pl.BlockSpec((R_BLOCKS, 128, 128), lambda i: (0, 0, 0))
    return [pl.BlockSpec((CONV_WIDTH, C), lambda i: (0, 0)), vec, blk, vec, blk, vec, vec]


def _rglru_fwd(proj, cw, cb, wa, ba, wx, bx, lam, *, T, name):
    S_ = proj.shape[0]
    C = R_WIDTH

    def body(rx_ref, rz_ref, cw_ref, cb_ref, wa_ref, ba_ref, wx_ref, bx_ref, lam_ref,
             h_ref, y_ref, halo, hcar):
        i = pl.program_id(0)

        @pl.when(i == 0)
        def _():
            halo[...] = jnp.zeros_like(halo)
            hcar[...] = jnp.zeros_like(hcar)

        rx = rx_ref[...]
        ext = jnp.concatenate([halo[...], rx], 0)
        halo[...] = rx[T - 8:]
        taps = _conv_taps(ext, T)
        xr = cb_ref[...] + sum(cw_ref[k:k + 1, :] * taps[k] for k in range(CONV_WIDTH))
        _, _, ig, _, a, mult = _rg_gates(xr, wa_ref, ba_ref, wx_ref, bx_ref, lam_ref)
        u = mult * (ig * xr)
        acum, hloc = _scan_lin(a, u, False)
        h = hloc + acum * hcar[0:1, :]
        hcar[...] = jnp.broadcast_to(h[T - 1:T, :], (8, C))
        h_ref[...] = h
        y_ref[...] = (h * _silu(rz_ref[...])).astype(BF16)

    row = pl.BlockSpec((T, C), lambda i: (i, 0))
    return _pcall(
        body, name=name, grid=(S_ // T,),
        in_specs=[pl.BlockSpec((T, C), lambda i: (i, OFF_RX // C)),
                  pl.BlockSpec((T, C), lambda i: (i, OFF_RZ // C))] + _rg_param_specs(),
        out_specs=[row, pl.BlockSpec((T, C), lambda i: (i, MIX_R // C))],
        out_shape=[jax.ShapeDtypeStruct((S_, C), F32), jax.ShapeDtypeStruct((S_, MIX_WIDTH), BF16)],
        scratch_shapes=[pltpu.VMEM((8, C), F32), pltpu.VMEM((8, C), F32)],
        compiler_params=_cp(("arbitrary",)),
    )(proj, proj, cw, cb.reshape(1, C), _b(wa), ba.reshape(1, C), _b(wx), bx.reshape(1, C), lam.reshape(1, C))


def _rglru_bwd(proj, h, dymix, dproj, cw, cb, wa, ba, wx, bx, lam, *, T, name):
    S_ = proj.shape[0]
    C = R_WIDTH
    nt = S_ // T
    t8 = T // 8

    def body(rx_ref, rxp_ref, rz_ref, h_ref, hp_ref, dy_ref,
             cw_ref, cb_ref, wa_ref, ba_ref, wx_ref, bx_ref, lam_ref, wat_ref, wxt_ref,
             _, dp_ref, dcw_ref, dcb_ref, dwa_ref, dba_ref, dwx_ref, dbx_ref, dlam_ref,
             afirst, gfirst, dhalo):
        i = pl.program_id(0)
        first_tile = (i == nt - 1)

        @pl.when(i == 0)
        def _():
            afirst[...] = jnp.zeros_like(afirst)
            gfirst[...] = jnp.zeros_like(gfirst)
            dhalo[...] = jnp.zeros_like(dhalo)
            for r in (dcw_ref, dcb_ref, dwa_ref, dba_ref, dwx_ref, dbx_ref, dlam_ref):
                r[...] = jnp.zeros_like(r)

        keep = jnp.where(first_tile, 0.0, 1.0)
        rx = rx_ref[...]
        ext = jnp.concatenate([rxp_ref[...] * keep, rx], 0)
        taps = _conv_taps(ext, T)
        xr = cb_ref[...] + sum(cw_ref[k:k + 1, :] * taps[k] for k in range(CONV_WIDTH))
        xb, r, ig, sp, a, mult = _rg_gates(xr, wa_ref, ba_ref, wx_ref, bx_ref, lam_ref)
        hh = h_ref[...]
        rz = rz_ref[...]
        dy = dy_ref[...]
        dp_ref[:, C:2 * C] = (dy * hh * _dsilu(rz)).astype(BF16)
        dh = dy * _silu(rz)
        row = lax.broadcasted_iota(jnp.int32, (T, 1), 0)
        c = jnp.where(row == T - 1, afirst[0:1, :], pltpu.roll(a, T - 1, 0))
        ccum, gloc = _scan_lin(c, dh, True)
        g = gloc + ccum * gfirst[0:1, :]
        afirst[...] = jnp.broadcast_to(a[0:1, :], (8, C))
        gfirst[...] = jnp.broadcast_to(g[0:1, :], (8, C))
        hprev = jnp.where(row == 0, hp_ref[7:8, :] * keep, pltpu.roll(hh, 1, 0))
        da = g * hprev
        gx = ig * xr
        dgx = g * mult
        dmult = g * gx
        dlog_a = da * a - dmult * (a * a) * lax.rsqrt(mult * mult)
        dpre_a = dlog_a * (-R_C * sp) * r * (1.0 - r)
        dpre_x = dgx * xr * ig * (1.0 - ig)
        dlam_ref[...] += jnp.sum(dlog_a * (-R_C * r), 0, keepdims=True) * (-_sigmoid(-lam_ref[...]))
        dab = _b(dpre_a)
        dxb = _b(dpre_x)
        dxr = dgx * ig + jnp.concatenate(
            [_nn(dab[:, n * 128:(n + 1) * 128], wat_ref[n]) + _nn(dxb[:, n * 128:(n + 1) * 128], wxt_ref[n])
             for n in range(R_BLOCKS)], 1)
        for n in range(R_BLOCKS):
            cs = slice(n * 128, (n + 1) * 128)
            dwa_ref[n] += _tn(xb[:, cs], dab[:, cs])
            dwx_ref[n] += _tn(xb[:, cs], dxb[:, cs])
        dba_ref[...] += jnp.sum(dpre_a, 0, keepdims=True)
        dbx_ref[...] += jnp.sum(dpre_x, 0, keepdims=True)
        dcb_ref[...] += jnp.sum(dxr, 0, keepdims=True)
        for k in range(CONV_WIDTH):
            dcw_ref[k:k + 1, :] += jnp.sum(dxr * taps[k], 0, keepdims=True)
        ext2 = jnp.concatenate([dxr, dhalo[...]], 0)
        tt = _conv_taps_t(ext2, T)
        dp_ref[:, 0:C] = sum(cw_ref[k:k + 1, :] * tt[k] for k in range(CONV_WIDTH)).astype(BF16)
        dhalo[...] = dxr[0:8]

    def rev(i):
        return nt - 1 - i

    def prev8(i):
        return jnp.maximum(rev(i) * t8 - 1, 0)

    vec = pl.BlockSpec((1, C), lambda i: (0, 0))
    blk = pl.BlockSpec((R_BLOCKS, 128, 128), lambda i: (0, 0, 0))
    row = pl.BlockSpec((T, C), lambda i: (rev(i), 0))
    wat = _b(jnp.swapaxes(wa, 1, 2))
    wxt = _b(jnp.swapaxes(wx, 1, 2))
    return _pcall(
        body, name=name, grid=(nt,),
        in_specs=[pl.BlockSpec((T, C), lambda i: (rev(i), OFF_RX // C)),
                  pl.BlockSpec((8, C), lambda i: (prev8(i), OFF_RX // C)),
                  pl.BlockSpec((T, C), lambda i: (rev(i), OFF_RZ // C)),
                  row,
                  pl.BlockSpec((8, C), lambda i: (prev8(i), 0)),
                  pl.BlockSpec((T, C), lambda i: (rev(i), MIX_R // C)),
                  ] + _rg_param_specs() + [blk, blk, ANY_SPEC],
        out_specs=[pl.BlockSpec((T, 2 * C), lambda i: (rev(i), OFF_RX // (2 * C))),
                   pl.BlockSpec((CONV_WIDTH, C), lambda i: (0, 0)), vec, blk, vec, blk, vec, vec],
        out_shape=[jax.ShapeDtypeStruct(dproj.shape, BF16),
                   jax.ShapeDtypeStruct((CONV_WIDTH, C), F32), jax.ShapeDtypeStruct((1, C), F32),
                   jax.ShapeDtypeStruct((R_BLOCKS, 128, 128), F32), jax.ShapeDtypeStruct((1, C), F32),
                   jax.ShapeDtypeStruct((R_BLOCKS, 128, 128), F32), jax.ShapeDtypeStruct((1, C), F32),
                   jax.ShapeDtypeStruct((1, C), F32)],
        input_output_aliases={15: 0},
        scratch_shapes=[pltpu.VMEM((8, C), F32)] * 3,
        compiler_params=_cp(("arbitrary",)),
    )(proj, proj, proj, h, h, dymix, cw, cb.reshape(1, C), _b(wa), ba.reshape(1, C), _b(wx), bx.reshape(1, C),
      lam.reshape(1, C), wat, wxt, dproj)


GW3 = 3 * G_WIDTH


def _lane_col(x, lane_idx):
    lane = lax.broadcasted_iota(jnp.int32, (1, x.shape[1]), 1)
    return jnp.sum(jnp.where(lane == lane_idx, x, 0.0), 1, keepdims=True)


def _gdn_pre(ext, T, cw_ref, gba, pv_ref):
    taps = _conv_taps(ext, T)
    c = sum(cw_ref[k:k + 1, :] * taps[k] for k in range(CONV_WIDTH))
    qkv = _silu(c)
    beta = _sigmoid(gba)
    sarg = gba + pv_ref[1:2, :]
    nea = -jnp.exp(pv_ref[0:1, :])
    gdec = nea * _softplus(sarg)
    ri = lax.broadcasted_iota(jnp.int32, (T, T), 0)
    cj = lax.broadcasted_iota(jnp.int32, (T, T), 1)
    same = (ri >> 6) == (cj >> 6)
    ltri = jnp.where((ri >= cj) & same, 1.0, 0.0).astype(BF16)
    gc = _dot_exact_lhs(_nn, ltri, gdec)
    return taps, c, qkv, beta, sarg, nea, gdec, gc


def _gdn_masks():
    ri = lax.broadcasted_iota(jnp.int32, (128, 128), 0)
    cj = lax.broadcasted_iota(jnp.int32, (128, 128), 1)
    same = (ri >> 6) == (cj >> 6)
    return (ri >= cj) & same, (ri > cj) & same, ri == cj


def _lockstep(gens):
    out = [None] * len(gens)
    live = list(range(len(gens)))
    while live:
        still = []
        for k in live:
            try:
                next(gens[k])
                still.append(k)
            except StopIteration as stop:
                out[k] = stop.value
        live = still
    return out


def _gdn_chunk(qkv, beta, gc, rs, h, tm=None):
    tril, strict, eye = _gdn_masks()
    rowi = lax.broadcasted_iota(jnp.int32, (128, 1), 0)
    lane = lax.broadcasted_iota(jnp.int32, (1, 128), 1)
    qh = qkv[rs, h * 128:(h + 1) * 128]
    kh = qkv[rs, 512 + h * 128:512 + (h + 1) * 128]
    vh = qkv[rs, 1024 + h * 128:1024 + (h + 1) * 128]
    rq = lax.rsqrt(jnp.sum(qh * qh, 1, keepdims=True) + RMS_EPS)
    rk = lax.rsqrt(jnp.sum(kh * kh, 1, keepdims=True) + RMS_EPS)
    qn = qh * (rq * (G_HEAD_DIM ** -0.5))
    kn = kh * rk
    gcb = gc[rs]
    gcol = _lane_col(gcb, 4 + h)
    bcol = _lane_col(beta[rs], h)
    grow = _dot_exact_lhs(_nt, jnp.ones((128, 128), BF16), jnp.where(lane == 4 + h, gcb, 0.0))
    D = jnp.where(tril, jnp.exp(jnp.minimum(gcol - grow, 0.0)), 0.0)
    kb = kn * bcol
    vb = vh * bcol
    knb = _b(kn)
    A = _nt(_b(kb), knb)
    Bm = _nt(_b(qn), knb)
    yield
    if tm is None:
        N = jnp.where(strict, -(A * D), 0.0)
        tm = jnp.where(eye, 1.0, 0.0) + N
        npow = N
        for _ in range(5):
            npow = _dot3(_nn, npow, npow)
            yield
            tm = tm + _dot3(_nn, tm, npow)
            yield
    eg = jnp.exp(gcol)
    u = _dot3(_nn, tm, vb)
    w = _dot3(_nn, tm, kb * eg)
    yield
    qk = jnp.where(tril, Bm * D, 0.0)
    qd = qn * eg
    gla = jnp.sum(jnp.where(rowi == 63, gcol, 0.0), 0, keepdims=True)
    glb = jnp.sum(jnp.where(rowi == 127, gcol, 0.0), 0, keepdims=True)
    ed = jnp.exp(jnp.where(rowi < 64, gla, glb) - gcol)
    kd = kn * ed
    return dict(qh=qh, kh=kh, vh=vh, rq=rq, rk=rk, qn=qn, kn=kn, gcol=gcol, bcol=bcol, D=D, A=A, Bm=Bm,
                tm=tm, eg=eg, ed=ed, u=u, w=w, qk=qk, qd=qd, kd=kd, kb=kb, vb=vb,
                gla=jnp.exp(gla), glb=jnp.exp(glb))


def _gdn_scan(q, sa):
    sab = _b(sa)
    wb = _b(q["w"])
    vna = q["u"] - _nn(wb, sab)
    yield
    sb = sa * q["gla"] + _tn(_b(q["kd"][0:64]), _b(vna[0:64]))
    yield
    sbb = _b(sb)
    vnb = q["u"] - _nn(wb, sbb)
    yield
    sn = sb * q["glb"] + _tn(_b(q["kd"][64:128]), _b(vnb[64:128]))
    yield
    vn = jnp.concatenate([vna[0:64], vnb[64:128]], 0)
    qdb = _b(q["qd"])
    o = jnp.concatenate([_nn(qdb[0:64], sab), _nn(qdb[64:128], sbb)], 0) + _nn(_b(q["qk"]), _b(vn))
    return sb, sn, vn, o


def _gdn_param_specs():
    return [pl.BlockSpec((CONV_WIDTH, GW3), lambda i: (0, 0)),
            pl.BlockSpec((8, 128), lambda i: (0, 0)),
            pl.BlockSpec((1, 128), lambda i: (0, 0))]


def _gdn_pvec(a_log, dt_bias):
    z = jnp.zeros((8, 128), F32)
    return z.at[0, 4:8].set(a_log).at[1, 4:8].set(dt_bias)


def _gdn_fwd(proj, cw, a_log, dt_bias, nw, ymix, *, T, name):
    S_ = proj.shape[0]
    nu = T // 128

    def body(x_ref, z_ref, g_ref, cw_ref, pv_ref, nw_ref, _, y_ref, st_ref, tm_ref, halo, state):
        i = pl.program_id(0)

        @pl.when(i == 0)
        def _():
            halo[...] = jnp.zeros_like(halo)
            state[...] = jnp.zeros_like(state)

        x = x_ref[...]
        ext = jnp.concatenate([halo[...], x], 0)
        halo[...] = x[T - 8:]
        _, _, qkv, beta, _, _, _, gc = _gdn_pre(ext, T, cw_ref, g_ref[...], pv_ref)
        items = [(dc, h) for dc in range(nu) for h in range(G_HEADS)]
        qs = _lockstep([_gdn_chunk(qkv, beta, gc, slice(dc * 128, (dc + 1) * 128), h) for dc, h in items])

        def head_chain(h):
            s = state[h]
            for dc in range(nu):
                rs = slice(dc * 128, (dc + 1) * 128)
                q = qs[dc * G_HEADS + h]
                sb, sn, _, o = yield from _gdn_scan(q, s)
                st_ref[2 * dc, h] = s
                st_ref[2 * dc + 1, h] = sb
                tm_ref[dc, h] = q["tm"]
                s = sn
                yield
                rn = lax.rsqrt(jnp.mean(o * o, 1, keepdims=True) + RMS_EPS)
                cs = slice(h * 128, (h + 1) * 128)
                y_ref[rs, cs] = (o * rn * nw_ref[...] * _silu(z_ref[rs, cs])).astype(BF16)
                yield
            state[h] = s

        _lockstep([head_chain(h) for h in range(G_HEADS)])

    return _pcall(
        body, name=name, grid=(S_ // T,),
        in_specs=[pl.BlockSpec((T, GW3), lambda i: (i, OFF_GQKV // GW3)),
                  pl.BlockSpec((T, 512), lambda i: (i, OFF_GZ // 512)),
                  pl.BlockSpec((T, 128), lambda i: (i, OFF_GBA // 128))] + _gdn_param_specs() + [ANY_SPEC],
        out_specs=[pl.BlockSpec((T, 512), lambda i: (i, MIX_G // 512)),
                   pl.BlockSpec((2 * nu, G_HEADS, 128, 128), lambda i: (i, 0, 0, 0)),
                   pl.BlockSpec((nu, G_HEADS, 128, 128), lambda i: (i, 0, 0, 0))],
        out_shape=[jax.ShapeDtypeStruct(ymix.shape, BF16),
                   jax.ShapeDtypeStruct((S_ // 64, G_HEADS, 128, 128), F32),
                   jax.ShapeDtypeStruct((S_ // 128, G_HEADS, 128, 128), F32)],
        input_output_aliases={6: 0},
        scratch_shapes=[pltpu.VMEM((8, GW3), F32), pltpu.VMEM((G_HEADS, 128, 128), F32)],
        compiler_params=_cp(("arbitrary",)),
    )(proj, proj, proj, cw, _gdn_pvec(a_log, dt_bias), nw.reshape(1, 128), ymix)


def _gdn_bwd(proj, states, tms, dymix, dproj, cw, a_log, dt_bias, nw, *, T, name):
    S_ = proj.shape[0]
    nt = S_ // T
    nu = T // 128
    t8 = T // 8

    def body(x_ref, xp_ref, z_ref, g_ref, st_ref, tm_ref, dy_ref, cw_ref, pv_ref, nw_ref, _,
             dp_ref, dg_ref, dcw_ref, dpv_ref, dnw_ref, dstate, dhalo, dqkv, dbg):
        i = pl.program_id(0)
        first_tile = (i == nt - 1)

        @pl.when(i == 0)
        def _():
            dstate[...] = jnp.zeros_like(dstate)
            dhalo[...] = jnp.zeros_like(dhalo)
            dcw_ref[...] = jnp.zeros_like(dcw_ref)
            dpv_ref[...] = jnp.zeros_like(dpv_ref)
            dnw_ref[...] = jnp.zeros_like(dnw_ref)

        keep = jnp.where(first_tile, 0.0, 1.0)
        ext = jnp.concatenate([xp_ref[...] * keep, x_ref[...]], 0)
        G = g_ref[...]
        taps, c, qkv, beta, sarg, nea, gdec, gc = _gdn_pre(ext, T, cw_ref, G, pv_ref)
        tril, strict, _ = _gdn_masks()
        rowi = lax.broadcasted_iota(jnp.int32, (128, 1), 0)
        lane = lax.broadcasted_iota(jnp.int32, (1, 128), 1)
        ones_b = jnp.ones((128, 128), BF16)
        nwv = nw_ref[...]
        items = [(dc, h) for dc in range(nu) for h in range(G_HEADS)]

        def recompute(dc, h):
            q = yield from _gdn_chunk(qkv, beta, gc, slice(dc * 128, (dc + 1) * 128), h, tm=tm_ref[dc, h])
            sa = st_ref[2 * dc, h]
            sb, _, vn, o = yield from _gdn_scan(q, sa)
            return q, sa, sb, vn, o

        fw = _lockstep([recompute(dc, h) for dc, h in items])
        chain_out = {}

        def head_chain(h):
            dS = dstate[h]
            for dc in reversed(range(nu)):
                rs = slice(dc * 128, (dc + 1) * 128)
                q, sa, sb, vn, o = fw[dc * G_HEADS + h]
                cs = slice(h * 128, (h + 1) * 128)
                zg = z_ref[rs, cs]
                dy = dy_ref[rs, cs]
                rn = lax.rsqrt(jnp.mean(o * o, 1, keepdims=True) + RMS_EPS)
                don = dy * _silu(zg)
                dp_ref[rs, GW3 + cs.start:GW3 + cs.stop] = (dy * (o * rn * nwv) * _dsilu(zg)).astype(BF16)
                dnw_ref[...] += jnp.sum(don * o * rn, 0, keepdims=True)
                tt = don * nwv
                do = rn * (tt - o * (rn * rn) * jnp.mean(tt * o, 1, keepdims=True))
                yield
                dob = _b(do)
                sab, sbb = _b(sa), _b(sb)
                vnb16 = _b(vn)
                dqk = jnp.where(tril, _nt(dob, vnb16), 0.0)
                dvn_o = _tn(_b(q["qk"]), dob)
                dSb16 = _b(dS)
                kdb = _b(q["kd"])
                wb = _b(q["w"])
                qdb = _b(q["qd"])
                yield
                dvn_b = dvn_o[64:128] + _nn(kdb[64:128], dSb16)
                dkd_b = _nt(vnb16[64:128], dSb16)
                dgl_b = jnp.sum(jnp.sum(dS * sb, 1, keepdims=True), 0, keepdims=True)
                yield
                dvn_b16 = _b(dvn_b)
                dw_b = -_nt(dvn_b16, sbb)
                dqd_b = _nt(dob[64:128], sbb)
                dSm = q["glb"] * dS + _tn(qdb[64:128], dob[64:128]) - _tn(wb[64:128], dvn_b16)
                yield
                dSm16 = _b(dSm)
                dvn_a = dvn_o[0:64] + _nn(kdb[0:64], dSm16)
                dkd_a = _nt(vnb16[0:64], dSm16)
                dgl_a = jnp.sum(jnp.sum(dSm * sa, 1, keepdims=True), 0, keepdims=True)
                yield
                dvn_a16 = _b(dvn_a)
                dw_a = -_nt(dvn_a16, sab)
                dqd_a = _nt(dob[0:64], sab)
                dS = q["gla"] * dSm + _tn(qdb[0:64], dob[0:64]) - _tn(wb[0:64], dvn_a16)
                chain_out[dc, h] = (dqk, jnp.concatenate([dvn_a, dvn_b], 0), jnp.concatenate([dw_a, dw_b], 0),
                                    jnp.concatenate([dkd_a, dkd_b], 0), jnp.concatenate([dqd_a, dqd_b], 0),
                                    dgl_a, dgl_b)
                yield
            dstate[h] = dS

        _lockstep([head_chain(h) for h in range(G_HEADS)])

        def local(dc, h):
            rs = slice(dc * 128, (dc + 1) * 128)
            q = fw[dc * G_HEADS + h][0]
            dqk, du, dw, dkd, dqd, dgl_a, dgl_b = chain_out[dc, h]
            if True:
                dvb = _dot3(_tn, q["tm"], du)
                dkbe = _dot3(_tn, q["tm"], dw)
                yield
                dM = jnp.where(strict, -(_nt(_b(dvb), _b(q["u"])) + _nt(_b(dkbe), _b(q["w"]))), 0.0)
                yield
                D = q["D"]
                dA = dM * D
                dB = dqk * D
                dDD = (dM * q["A"] + dqk * q["Bm"]) * D
                dh_, dm_, dl_ = _split3(dDD)
                colsum = _tn(dh_, ones_b) + (_tn(dm_, ones_b) + _tn(dl_, ones_b))
                dgc = jnp.sum(dDD, 1, keepdims=True) - _lane_col(colsum, 0)
                yield
                dA16, dB16 = _b(dA), _b(dB)
                knb, kbb, qnb = _b(q["kn"]), _b(q["kb"]), _b(q["qn"])
                eg, ed = q["eg"], q["ed"]
                dkb = _nn(dA16, knb) + dkbe * eg
                dkn = _tn(dA16, kbb) + _tn(dB16, qnb) + dkd * ed + dkb * q["bcol"]
                dqn = _nn(dB16, knb) + dqd * eg
                yield
                deg = jnp.sum(dkbe * q["kb"], 1, keepdims=True) + jnp.sum(dqd * q["qn"], 1, keepdims=True)
                ded = jnp.sum(dkd * q["kn"], 1, keepdims=True) * ed
                dgc = dgc + deg * eg - ded
                tail_a = jnp.sum(jnp.where(rowi < 64, ded, 0.0), 0, keepdims=True) + dgl_a * q["gla"]
                tail_b = jnp.sum(jnp.where(rowi >= 64, ded, 0.0), 0, keepdims=True) + dgl_b * q["glb"]
                dgc = dgc + jnp.where(rowi == 63, tail_a, 0.0) + jnp.where(rowi == 127, tail_b, 0.0)
                dbeta = jnp.sum(dkb * q["kn"], 1, keepdims=True) + jnp.sum(dvb * q["vh"], 1, keepdims=True)
                bcol = q["bcol"]
                blk = jnp.where(lane == h, dbeta * bcol * (1.0 - bcol), 0.0) + jnp.where(lane == 4 + h, dgc, 0.0)
                yield
                sc = G_HEAD_DIM ** -0.5
                rq, rk, qh, kh = q["rq"], q["rk"], q["qh"], q["kh"]
                dqh = sc * (dqn * rq - qh * (rq * rq * rq) * jnp.sum(dqn * qh, 1, keepdims=True))
                dkh = dkn * rk - kh * (rk * rk * rk) * jnp.sum(dkn * kh, 1, keepdims=True)
                dqkv[rs, h * 128:(h + 1) * 128] = dqh
                dqkv[rs, 512 + h * 128:512 + (h + 1) * 128] = dkh
                dqkv[rs, 1024 + h * 128:1024 + (h + 1) * 128] = dvb * bcol
            return blk

        blks = _lockstep([local(dc, h) for dc, h in items])
        for dc in range(nu):
            dbg[dc * 128:(dc + 1) * 128, :] = functools.reduce(
                lambda a, b: a + b, [blks[dc * G_HEADS + h] for h in range(G_HEADS)])
        ri = lax.broadcasted_iota(jnp.int32, (T, T), 0)
        cj = lax.broadcasted_iota(jnp.int32, (T, T), 1)
        utri = jnp.where((ri <= cj) & ((ri >> 6) == (cj >> 6)), 1.0, 0.0).astype(BF16)
        dbgv = dbg[...]
        dgd = _dot_exact_lhs(_nn, utri, dbgv)
        is_g = (lane >= 4) & (lane < 8)
        dga = jnp.where(is_g, dgd * nea * _sigmoid(sarg), 0.0)
        dg_ref[...] = jnp.where(lane < 4, dbgv, dga).astype(BF16)
        dpv_ref[0:1, :] += jnp.sum(jnp.where(is_g, dgd * gdec, 0.0), 0, keepdims=True)
        dpv_ref[1:2, :] += jnp.sum(dga, 0, keepdims=True)
        dc_ = dqkv[...] * _dsilu(c)
        for k in range(CONV_WIDTH):
            dcw_ref[k:k + 1, :] += jnp.sum(dc_ * taps[k], 0, keepdims=True)
        ext2 = jnp.concatenate([dc_, dhalo[...]], 0)
        tt2 = _conv_taps_t(ext2, T)
        dp_ref[:, 0:GW3] = sum(cw_ref[k:k + 1, :] * tt2[k] for k in range(CONV_WIDTH)).astype(BF16)
        dhalo[...] = dc_[0:8]

    def rev(i):
        return nt - 1 - i

    def prev8(i):
        return jnp.maximum(rev(i) * t8 - 1, 0)

    return _pcall(
        body, name=name, grid=(nt,),
        in_specs=[pl.BlockSpec((T, GW3), lambda i: (rev(i), OFF_GQKV // GW3)),
                  pl.BlockSpec((8, GW3), lambda i: (prev8(i), OFF_GQKV // GW3)),
                  pl.BlockSpec((T, 512), lambda i: (rev(i), OFF_GZ // 512)),
                  pl.BlockSpec((T, 128), lambda i: (rev(i), OFF_GBA // 128)),
                  pl.BlockSpec((2 * nu, G_HEADS, 128, 128), lambda i: (rev(i), 0, 0, 0)),
                  pl.BlockSpec((nu, G_HEADS, 128, 128), lambda i: (rev(i), 0, 0, 0)),
                  pl.BlockSpec((T, 512), lambda i: (rev(i), MIX_G // 512))] + _gdn_param_specs() + [ANY_SPEC],
        out_specs=[pl.BlockSpec((T, GW3 + 512), lambda i: (rev(i), OFF_GQKV // (GW3 + 512))),
                   pl.BlockSpec((T, 128), lambda i: (rev(i), 0)),
                   pl.BlockSpec((CONV_WIDTH, GW3), lambda i: (0, 0)),
                   pl.BlockSpec((8, 128), lambda i: (0, 0)),
                   pl.BlockSpec((1, 128), lambda i: (0, 0))],
        out_shape=[jax.ShapeDtypeStruct(dproj.shape, BF16),
                   jax.ShapeDtypeStruct((S_, 128), BF16), jax.ShapeDtypeStruct((CONV_WIDTH, GW3), F32),
                   jax.ShapeDtypeStruct((8, 128), F32), jax.ShapeDtypeStruct((1, 128), F32)],
        input_output_aliases={10: 0},
        scratch_shapes=[pltpu.VMEM((G_HEADS, 128, 128), F32), pltpu.VMEM((8, GW3), F32),
                        pltpu.VMEM((T, GW3), F32), pltpu.VMEM((T, 128), F32)],
        compiler_params=_cp(("arbitrary",)),
    )(proj, proj, proj, proj, states, tms, dymix, cw, _gdn_pvec(a_log, dt_bias), nw.reshape(1, 128), dproj)


def _pair_sum_windows(a, b, nsh, width, *, out_dtype, name):
    R_, C = a.shape
    hr = R_ // 2
    nb = width // 128
    assert (3 * nsh) // 128 + nb <= C // 128
    to_perm = _orig_block_to_perm()
    blocks = jnp.asarray([to_perm[(nsh * t) // 128 + j] for t in range(4) for j in range(nb)], jnp.int32)
    table = jnp.concatenate([blocks, lax.axis_index("c").astype(jnp.int32)[None]])

    def body(tab_ref, a_ref, b_ref, o_ref):
        o_ref[...] = (a_ref[...] + b_ref[...]).astype(o_ref.dtype)

    return _pcall(
        body, name=name,
        grid_spec=pltpu.PrefetchScalarGridSpec(
            num_scalar_prefetch=1, grid=(4, nb),
            in_specs=[pl.BlockSpec((hr, 128), lambda t, j, tab: (tab[4 * nb], tab[t * nb + j])),
                      pl.BlockSpec((hr, 128), lambda t, j, tab: (0, tab[t * nb + j]))],
            out_specs=pl.BlockSpec((None, hr, 128), lambda t, j, tab: (t, 0, j))),
        out_shape=jax.ShapeDtypeStruct((4, hr, width), out_dtype),
        compiler_params=_cp(("parallel", "parallel")))(table, a, b)


def _pair_sum_blocks(a, b, *, out_dtype, name):
    L, R_, C = a.shape
    hr = R_ // 2

    def body(a0_ref, a1_ref, b_ref, o_ref):
        mine = jnp.where(lax.axis_index("c") == 0, a0_ref[...], a1_ref[...])
        o_ref[...] = (mine + b_ref[...]).astype(o_ref.dtype)

    def spec(half):
        return pl.BlockSpec((None, hr, C), lambda t: (t, half, 0))

    return _pcall(body, name=name, grid=(L,), in_specs=[spec(0), spec(1), spec(0)], out_specs=spec(0),
                  out_shape=jax.ShapeDtypeStruct((L, hr, C), out_dtype),
                  compiler_params=_cp(("parallel",)))(a, a, b)


def _sum4(a, mine, *, tr, name):
    _, R_, C = a.shape

    def body(a_ref, m_ref, o_ref):
        s = 2 * lax.axis_index("x") + lax.axis_index("y")
        mv = m_ref[...].astype(F32)
        p = [jnp.where(s == t, mv, a_ref[t].astype(F32)) for t in range(4)]
        o_ref[...] = ((p[0] + p[1]) + p[2]) + p[3]

    return _pcall(body, name=name, grid=(R_ // tr,),
                  in_specs=[pl.BlockSpec((4, tr, C), lambda i: (0, i, 0)), pl.BlockSpec((tr, C), lambda i: (i, 0))],
                  out_specs=pl.BlockSpec((tr, C), lambda i: (i, 0)),
                  out_shape=jax.ShapeDtypeStruct((R_, C), F32), compiler_params=_cp(("parallel",)))(a, mine)


def _adamw_refs(w_ref, g_ref, m_ref, v_ref, d_ref, mo_ref, vo_ref):
    c1 = 1.0 / (1.0 - ADAM_B1 ** ADAM_STEP)
    c2 = 1.0 / (1.0 - ADAM_B2 ** ADAM_STEP)
    gg = g_ref[...]
    mn = ADAM_B1 * m_ref[...] + (1.0 - ADAM_B1) * gg
    vn = ADAM_B2 * v_ref[...] + (1.0 - ADAM_B2) * (gg * gg)
    mo_ref[...] = mn
    vo_ref[...] = vn
    d_ref[...] = -ADAM_LR * ((mn * c1) / (jnp.sqrt(vn * c2) + ADAM_EPS) + ADAM_WD * w_ref[...])


def _adamw_many(ws, gs, ms, vs, *, name):
    n = len(ws)

    def body(*refs):
        for k in range(n):
            _adamw_refs(*[refs[q * n + k] for q in range(7)])

    vm = pl.BlockSpec(memory_space=pltpu.VMEM)
    shp = [jax.ShapeDtypeStruct(w.shape, F32) for w in ws]
    outs = _pcall(body, name=name, in_specs=[vm] * (4 * n), out_specs=[vm] * (3 * n), out_shape=shp * 3,
                  compiler_params=pltpu.CompilerParams(vmem_limit_bytes=VMEM_LIMIT))(*ws, *gs, *ms, *vs)
    return outs[:n], outs[n:2 * n], outs[2 * n:]


def _adamw(w, g, m, v, *, tr, name):
    L, R_, C = w.shape
    body = functools.partial(_adamw_refs)

    spec = pl.BlockSpec((None, tr, C), lambda l, i: (l, i, 0))
    shp = jax.ShapeDtypeStruct((L, R_, C), F32)
    return _pcall(body, name=name, grid=(L, R_ // tr), in_specs=[spec] * 4, out_specs=[spec] * 3,
                  out_shape=[shp] * 3, compiler_params=_cp(("parallel", "parallel")))(w, g, m, v)


def _adamw_cols(w, g, m, v, *, name):
    C, L, R_ = w.shape
    tc = C // 2 if C % 2 == 0 else C

    spec = pl.BlockSpec((tc, L, 128), lambda i, j: (i, 0, j))
    shp = jax.ShapeDtypeStruct((C, L, R_), F32)
    return _pcall(functools.partial(_adamw_refs), name=name, grid=(C // tc, R_ // 128), in_specs=[spec] * 4,
                  out_specs=[spec] * 3, out_shape=[shp] * 3,
                  compiler_params=_cp(("parallel", "parallel")))(w, g, m, v)


HBM_SPEC = pl.BlockSpec(memory_space=pltpu.HBM)


def _place():
    x, y, c = lax.axis_index("x"), lax.axis_index("y"), lax.axis_index("c")
    chips = [(1 - x, y), (x, 1 - y), (1 - x, 1 - y)]
    return x, y, c, 2 * x + y, chips, [2 * cx + cy for cx, cy in chips], (x, y, 1 - c)


def _remote(src, dst, ssem, rsem, dev):
    return pltpu.make_async_remote_copy(src_ref=src, dst_ref=dst, send_sem=ssem, recv_sem=rsem,
                                        device_id=dev, device_id_type=MESH)


def _row_half(ref, lead, hc):
    hl = ref.shape[-2] // 2
    return ref.at[lead, pl.ds(hc * hl, hl), :]


def _gather_side(items):
    n = len(items)

    def copies(ins, outs, ssem, rsem):
        x, y, c, s, chips, sid, sib = _place()
        cps = [_remote(_row_half(ins[k], items[k][1], c), _row_half(outs[k], s, c),
                       ssem.at[3 * k + j], rsem.at[3 * k + j], (*chip, c))
               for k in range(n) for j, chip in enumerate(chips)]
        return cps, c, sid, sib

    def start(ins, outs, ssem, rsem):
        for cp in copies(ins, outs, ssem, rsem)[0]:
            cp.start()

    def finish(ins, outs, ssem, rsem):
        cps, c, sid, sib = copies(ins, outs, ssem, rsem)
        for k in range(n):
            for j in range(3):
                got = _row_half(outs[k], sid[j], c)
                _remote(got, got, ssem.at[3 * k + j], rsem.at[3 * k + j], sib).wait_recv()
        for cp in cps:
            cp.wait_send()

    shapes = [jax.ShapeDtypeStruct((4,) + w.shape[1:], w.dtype) for w, _ in items]
    return _Side([w for w, _ in items], shapes, 3 * n, start, finish)


def _gather_join(gathered, name):
    n = len(gathered)

    def body(*refs):
        outs, ssem, rsem = refs[n:2 * n], refs[2 * n], refs[2 * n + 1]
        x, y, c, s, chips, sid, sib = _place()
        cps = []
        for k in range(n):
            for j in range(3):
                mine = _row_half(outs[k], sid[j], c)
                cps.append(_remote(mine, mine, ssem.at[3 * k + j], rsem.at[3 * k + j], sib))
        for cp in cps:
            cp.start()
        for k in range(n):
            for j in range(3):
                other = _row_half(outs[k], sid[j], 1 - c)
                _remote(other, other, ssem.at[3 * k + j], rsem.at[3 * k + j], sib).wait_recv()
        for cp in cps:
            cp.wait_send()

    return _pcall(
        body, name=name, in_specs=[HBM_SPEC] * n, out_specs=[HBM_SPEC] * n,
        out_shape=[jax.ShapeDtypeStruct(g.shape, g.dtype) for g in gathered],
        input_output_aliases={k: k for k in range(n)},
        scratch_shapes=[pltpu.SemaphoreType.DMA((3 * n,)), pltpu.SemaphoreType.DMA((3 * n,))],
    )(*gathered)


def _gather_layer0(win, conv):
    def body(win_ref, cv_ref, gin_ref, gcv_ref, ssem, rsem):
        x, y, c, s, chips, sid, sib = _place()

        def in_half(slot, hc):
            return _row_half(gin_ref, slot, hc)

        sends = []
        for j, chip in enumerate(chips):
            dev = (*chip, c)
            sends.append(_remote(_row_half(win_ref, 0, c), in_half(s, c), ssem.at[j], rsem.at[j], dev))
            sends.append(_remote(cv_ref, gcv_ref.at[s], ssem.at[3 + j], rsem.at[3 + j], dev))
        for cp in sends:
            cp.start()
        for j in range(3):
            _remote(in_half(sid[j], c), in_half(sid[j], c), ssem.at[j], rsem.at[j], sib).wait_recv()
            f = _remote(in_half(sid[j], c), in_half(sid[j], c), ssem.at[6 + j], rsem.at[6 + j], sib)
            f.start()
            sends.append(f)
        for j in range(3):
            _remote(in_half(sid[j], 1 - c), in_half(sid[j], 1 - c), ssem.at[6 + j], rsem.at[6 + j], sib).wait_recv()
            _remote(gcv_ref.at[sid[j]], gcv_ref.at[sid[j]], ssem.at[3 + j], rsem.at[3 + j], sib).wait_recv()
        for cp in sends:
            cp.wait_send()

    return _pcall(
        body, name="gather_layer0",
        in_specs=[HBM_SPEC] * 2, out_specs=[HBM_SPEC] * 2,
        out_shape=[jax.ShapeDtypeStruct((4,) + win.shape[1:], win.dtype),
                   jax.ShapeDtypeStruct((4,) + conv.shape, conv.dtype)],
        scratch_shapes=[pltpu.SemaphoreType.DMA((9,)), pltpu.SemaphoreType.DMA((9,))],
    )(win, conv)


def _swap_halves(arrs, axes, name):
    n = len(arrs)

    def half_shape(a, ax):
        return a.shape[:ax] + (a.shape[ax] // 2,) + a.shape[ax + 1:]

    def body(*refs):
        src, dst, ssem, rsem = refs[:n], refs[n:2 * n], refs[2 * n], refs[2 * n + 1]
        x, y, c, s, chips, sid, sib = _place()
        cps = []
        for k in range(n):
            hl = src[k].shape[axes[k]] // 2
            idx = [slice(None)] * len(src[k].shape)
            idx[axes[k]] = pl.ds((1 - c) * hl, hl)
            cps.append(_remote(src[k].at[tuple(idx)], dst[k], ssem.at[k], rsem.at[k], sib))
        for cp in cps:
            cp.start()
        for cp in cps:
            cp.wait()

    return _pcall(
        body, name=name, in_specs=[HBM_SPEC] * n, out_specs=[HBM_SPEC] * n,
        out_shape=[jax.ShapeDtypeStruct(half_shape(a, ax), a.dtype) for a, ax in zip(arrs, axes)],
        scratch_shapes=[pltpu.SemaphoreType.DMA((n,)), pltpu.SemaphoreType.DMA((n,))],
    )(*arrs)


def _swap_side(arrs, axes):
    n = len(arrs)

    def copies(ins, outs, ssem, rsem):
        x, y, c, s, chips, sid, sib = _place()
        cps = []
        for k in range(n):
            hl = ins[k].shape[axes[k]] // 2
            idx = [slice(None)] * len(ins[k].shape)
            idx[axes[k]] = pl.ds((1 - c) * hl, hl)
            cps.append(_remote(ins[k].at[tuple(idx)], outs[k], ssem.at[k], rsem.at[k], sib))
        return cps

    def start(ins, outs, ssem, rsem):
        for cp in copies(ins, outs, ssem, rsem):
            cp.start()

    def finish(ins, outs, ssem, rsem):
        for cp in copies(ins, outs, ssem, rsem):
            cp.wait()

    shapes = [jax.ShapeDtypeStruct(a.shape[:ax] + (a.shape[ax] // 2,) + a.shape[ax + 1:], a.dtype)
              for a, ax in zip(arrs, axes)]
    return _Side(list(arrs), shapes, n, start, finish)


def _chips_side(arrs, per_target):
    n = len(arrs)

    def copies(ins, outs, ssem, rsem):
        x, y, c, s, chips, sid, sib = _place()
        cps = [_remote(ins[k].at[sid[j]] if per_target[k] else ins[k], outs[k].at[s],
                       ssem.at[3 * k + j], rsem.at[3 * k + j], (*chip, c))
               for k in range(n) for j, chip in enumerate(chips)]
        return cps, sid, sib

    def start(ins, outs, ssem, rsem):
        for cp in copies(ins, outs, ssem, rsem)[0]:
            cp.start()

    def finish(ins, outs, ssem, rsem):
        cps, sid, sib = copies(ins, outs, ssem, rsem)
        for k in range(n):
            for j in range(3):
                got = outs[k].at[sid[j]]
                _remote(got, got, ssem.at[3 * k + j], rsem.at[3 * k + j], sib).wait_recv()
        for cp in cps:
            cp.wait_send()

    shapes = [jax.ShapeDtypeStruct(a.shape if pt else (4,) + a.shape, a.dtype) for a, pt in zip(arrs, per_target)]
    return _Side(list(arrs), shapes, 3 * n, start, finish)


def _swap_whole(arrs, name):
    n = len(arrs)

    def body(*refs):
        src, dst, ssem, rsem = refs[:n], refs[n:2 * n], refs[2 * n], refs[2 * n + 1]
        *_, sib = _place()
        cps = [_remote(src[k], dst[k], ssem.at[k], rsem.at[k], sib) for k in range(n)]
        for cp in cps:
            cp.start()
        for cp in cps:
            cp.wait()

    return _pcall(
        body, name=name, in_specs=[HBM_SPEC] * n, out_specs=[HBM_SPEC] * n,
        out_shape=[jax.ShapeDtypeStruct(a.shape, a.dtype) for a in arrs],
        scratch_shapes=[pltpu.SemaphoreType.DMA((n,)), pltpu.SemaphoreType.DMA((n,))],
    )(*arrs)


def _perm_cols(w):
    parts = [w[..., int(_ORIG_OFF[oi]):int(_ORIG_OFF[oi]) + IN_SIZES[oi]] for oi, _ in _PIECES]
    parts.append(jnp.zeros(w.shape[:-1] + (NP - N_IN,), w.dtype))
    return jnp.concatenate(parts, -1)


def _perm_rows(w):
    return jnp.concatenate([w[..., 512:1536, :], w[..., 0:512, :], w[..., 1536:2048, :]], -2)


_SMALL = ("sinks", "r_conv_b", "r_wa", "r_ba", "r_wx", "r_bx", "r_lam", "g_a_log", "g_dt_bias", "g_norm_w",
          "ln_g", "ln_b", "r_conv_w", "g_conv_w")
_PACK_ROWS = 16


def _piece_rows(n):
    return -(-n // (128 * _PACK_ROWS)) * _PACK_ROWS


def _pack(arrs):
    parts = []
    for a in arrs:
        n = int(np.prod(a.shape))
        rows = _piece_rows(n)
        if n % 128 == 0:
            blk = a.reshape(n // 128, 128)
        else:
            blk = jnp.pad(a.reshape(1, n), ((0, 0), (0, (-n) % 128))).reshape(-1, 128)
        if blk.shape[0] < rows:
            blk = jnp.pad(blk, ((0, rows - blk.shape[0]), (0, 0)))
        parts.append(blk)
    return jnp.concatenate(parts, 0)


def _unpack(packed, shapes):
    out = []
    r = 0
    for shp in shapes:
        n = int(np.prod(shp))
        if n % 128 == 0:
            out.append(packed[r:r + n // 128].reshape(shp))
        else:
            nr = -(-n // 128)
            out.append(packed[r:r + nr].reshape(1, nr * 128)[:, :n].reshape(shp))
        r += _piece_rows(n)
    return out


def _tile(n, t):
    return min(n, t)


def _layer_fwd(l, x, xb, wb, wob, ln, rope_c, rope_s, p, side=None, target=None):
    S_ = x.shape[0]
    proj = _matmul(xb, wb, ta=False, tb=False, tm=_tile(S_, 1024), tn=NP // 4, tk=wb.shape[0], out_dtype=F32,
                   name=f"in_proj_{l}", side=side)
    side_out = None
    if side:
        proj, side_out = proj
    h, ymix = _rglru_fwd(proj, p["r_conv_w"], p["r_conv_b"], p["r_wa"], p["r_ba"], p["r_wx"], p["r_bx"], p["r_lam"],
                         T=_tile(S_, 256), name=f"rglru_fwd_{l}")
    ymix = _attn_fwd(proj, rope_c, rope_s, p["sinks"], ymix, T=_tile(S_, 1024), name=f"attn_fwd_{l}")
    ymix, st, tms = _gdn_fwd(proj, p["g_conv_w"], p["g_a_log"], p["g_dt_bias"], p["g_norm_w"], ymix,
                             T=_tile(S_, 256), name=f"gdn_fwd_{l}")
    out = _outproj(ymix, wob(side_out), x, ln[0], ln[1], tm=_tile(S_, 512), name=f"out_proj_{l}", target=target)
    sv = dict(proj=proj, h=h, st=st, tms=tms, ymix=ymix)
    if target is None:
        sv["z"], sv["y"], sv["yb"] = out
    else:
        sv["head"] = out
    return sv


def _layer_bwd(l, sv, x_b, dz, dzb, wb, wob, rope_c, rope_s, p, side_dmix=None, side_dw_in=None, side_dx=None):
    S_, D = dz.shape
    proj = sv["proj"]
    dwo = _matmul(sv["ymix"], dzb, ta=True, tb=False, tm=512, tn=_tile(D, 2048), tk=_tile(S_, 2048),
                  out_dtype=F32, name=f"dw_out_{l}",
                  out_blocks=((MIX_WIDTH, D), (512, _tile(D, 2048)),
                              lambda i, j: (jnp.where(i == 3, 3, (i + 1) % 3), j)))
    side = side_dmix(dwo) if side_dmix else None
    dymix = _matmul(dzb, wob, ta=False, tb=True, tm=_tile(S_, 1024), tn=1024, tk=D, out_dtype=F32,
                    name=f"dmix_{l}", side=side)
    out_dmix = None
    if side:
        dymix, out_dmix = dymix
    dproj, dk, dv, dkt, dvt, dsk = _attn_bwd(proj, rope_c, rope_s, p["sinks"], dymix, T=_tile(S_, 512),
                                             name=f"attn_bwd_{l}")
    (dproj, dcw_r, dcb_r, dwa, dba, dwx, dbx, dlam) = _rglru_bwd(
        proj, sv["h"], dymix, dproj, p["r_conv_w"], p["r_conv_b"], p["r_wa"], p["r_ba"], p["r_wx"], p["r_bx"],
        p["r_lam"], T=_tile(S_, 256), name=f"rglru_bwd_{l}")
    dproj, dgba, dcw_g, dpv, dnw = _gdn_bwd(proj, sv["st"], sv["tms"], dymix, dproj, p["g_conv_w"], p["g_a_log"],
                                            p["g_dt_bias"], p["g_norm_w"], T=_tile(S_, 256), name=f"gdn_bwd_{l}")
    tail = jnp.concatenate([dk[128:], dkt, dv[128:], dvt], 0).reshape(2, S_, 128)
    tail = jnp.concatenate([tail[0], tail[1], dgba, jnp.zeros((S_, NP - OFF_GBA - 128), BF16)], 1)
    dproj = lax.dynamic_update_slice(dproj, tail, (0, OFF_AK))
    small = dict(sinks=dsk[:, 0], r_conv_b=dcb_r[0], r_wa=dwa, r_ba=dba[0], r_wx=dwx, r_bx=dbx[0], r_lam=dlam[0],
                 g_a_log=dpv[0, 4:8], g_dt_bias=dpv[1, 4:8], g_norm_w=dnw[0], r_conv_w=dcw_r, g_conv_w=dcw_g)
    side = side_dw_in(small, dwo, out_dmix) if side_dw_in else None
    dwin = _matmul(x_b, dproj, ta=True, tb=False, tm=_tile(D, 1024), tn=NP // 4, tk=_tile(S_, 2048),
                   out_dtype=F32, name=f"dw_in_{l}", side=side)
    out_dw_in = None
    if side:
        dwin, out_dw_in = dwin
    side = side_dx(dwin) if side_dx else None
    dx_args = dict(ta=False, tb=True, tm=_tile(S_, 1024), tn=_tile(D, 1024), tk=NP // 2, out_dtype=F32, extra=dz,
                   alpha=DEEPNORM_ALPHA)
    out_dx = None
    if side:
        dx, out_dx = _matmul(dproj, wb, name=f"dx_{l}", side=side, **dx_args)
    else:
        dx = _matmul(dproj, wb, name=f"dx_{l}", **dx_args)
    return dx, dwin, dwo, small, out_dw_in, out_dx


def kernel(x, w_in, sinks, r_conv_w, r_conv_b, r_wa, r_ba, r_wx, r_bx, r_lam, g_conv_w, g_a_log, g_dt_bias, g_norm_w, w_out, ln_g, ln_b, loss_target, m_w_in, m_sinks, m_r_conv_w, m_r_conv_b, m_r_wa, m_r_ba, m_r_wx, m_r_bx, m_r_lam, m_g_conv_w, m_g_a_log, m_g_dt_bias, m_g_norm_w, m_w_out, m_ln_g, m_ln_b, v_w_in, v_sinks, v_r_conv_w, v_r_conv_b, v_r_wa, v_r_ba, v_r_wx, v_r_bx, v_r_lam, v_g_conv_w, v_g_a_log, v_g_dt_bias, v_g_norm_w, v_w_out, v_ln_g, v_ln_b):
    S_, D = x.shape[1], x.shape[2]
    nsh = w_in.shape[2]
    rsh = w_out.shape[1]
    cx, cy, cc = lax.axis_index("x"), lax.axis_index("y"), lax.axis_index("c")
    chip = 2 * cx + cy
    rcw_n, gcw_n = r_conv_w.shape[2], g_conv_w.shape[2]

    conv_pack = jnp.concatenate([r_conv_w, g_conv_w], 2)
    w_in_b, w_out_b = w_in.astype(BF16), w_out.astype(BF16)
    g_in0, g_conv = _gather_layer0(w_in_b, conv_pack)

    def shards(own, got):
        return [jnp.where(chip == t, own, got[t]) for t in range(4)]

    def w_in_of(l, g_in):
        return _perm_cols(jnp.concatenate(shards(w_in_b[l], g_in), 1))

    def w_out_of(l, g_out):
        return _perm_rows(jnp.concatenate(shards(w_out_b[l], g_out), 0))

    rcw = jnp.concatenate(shards(r_conv_w, g_conv[:, :, :, :rcw_n]), 2)
    gcw = jnp.concatenate(shards(g_conv_w, g_conv[:, :, :, rcw_n:]), 2)

    pos = jnp.arange(S_, dtype=F32)[:, None]
    inv = 1.0 / (ROPE_THETA ** (jnp.arange(0, A_HEAD_DIM, 2, dtype=F32) / A_HEAD_DIM))
    ang = pos * inv[None, :]
    cos, sin = jnp.cos(ang), jnp.sin(ang)
    rope_c = jnp.concatenate([cos, cos, cos, cos], 1)
    rope_s = jnp.concatenate([-sin, sin, -sin, sin], 1)

    def params(l):
        return dict(sinks=sinks[l], r_conv_w=rcw[l], r_conv_b=r_conv_b[l], r_wa=r_wa[l], r_ba=r_ba[l],
                    r_wx=r_wx[l], r_bx=r_bx[l], r_lam=r_lam[l], g_conv_w=gcw[l], g_a_log=g_a_log[l],
                    g_dt_bias=g_dt_bias[l], g_norm_w=g_norm_w[l])

    assert DEPTH == 2
    xb0 = x[0].astype(BF16)
    wb, wob = [w_in_of(0, g_in0), None], [None, None]
    late = {}

    def w_out_0(arrived):
        late["w_in_1"], g_out0 = _gather_join(arrived, "gather_join_0")
        wob[0] = w_out_of(0, g_out0)
        return wob[0]

    def w_out_1(arrived):
        wob[1] = w_out_of(1, _gather_join(arrived, "gather_join_1")[0])
        return wob[1]

    sv0 = _layer_fwd(0, x[0], xb0, wb[0], w_out_0, (ln_g[0], ln_b[0]), rope_c, rope_s, params(0),
                     side=_gather_side([(w_in_b, 1), (w_out_b, 0)]))
    wb[1] = w_in_of(1, late["w_in_1"])
    sv1 = _layer_fwd(1, sv0["y"], sv0["yb"], wb[1], w_out_1, (ln_g[1], ln_b[1]), rope_c, rope_s, params(1),
                     side=_gather_side([(w_out_b, 1)]), target=loss_target[0])
    saved, xbs = [sv0, sv1], [xb0, sv0["yb"]]

    tm_ln = _tile(S_, 512)
    dz, dzb, dg_l, db_l, loss_part = saved[-1]["head"]
    assert DEPTH == 2
    wcov = (-(-nsh // 128) + 1) * 128
    names = list(_SMALL)

    def own(a):
        return lax.dynamic_index_in_dim(a, chip, 0, keepdims=False)

    def sum_in(l, cp, arrived):
        return _sum4(arrived, own(cp), tr=_tile(D // 2, 256), name=f"chip_sum_w_in_{l}")

    def sum_out(l, cp, arrived):
        return _sum4(arrived, own(cp), tr=rsh // 2, name=f"chip_sum_w_out_{l}")

    dlng, dlnb = [None, dg_l[0]], [None, db_l[0]]
    dx, dwin1, dwo1, small1, _, _ = _layer_bwd(1, saved[1], xbs[1], dz, dzb, wb[1], wob[1], rope_c, rope_s, params(1))
    dwo1_4 = dwo1.reshape(4, rsh, D)
    dz, dzb, dg_l, db_l, _ = _ln_bwd(saved[0]["z"], ln_g[0], ln_b[0], dx, tm=tm_ln, name="ln_bwd_0")
    dlng[0], dlnb[0] = dg_l[0], db_l[0]

    held = {}

    def side_dmix(dwo0):
        return _swap_side([dwin1, dwo1_4, dwo0.reshape(4, rsh, D)], [0, 1, 1])

    def side_dw_in(small0, dwo0, got):
        sm = {k: jnp.stack([small0[k], small1[k]]) for k in small0}
        sm["ln_g"], sm["ln_b"] = jnp.stack(dlng), jnp.stack(dlnb)
        gs = _pack([sm[n] for n in names])
        (got_s,) = _swap_halves([gs], [0], "reduce_pair_small")
        held["in_cp1"] = _pair_sum_windows(dwin1, got[0], nsh, wcov, out_dtype=BF16, name="pair_sum_w_in_1")
        held["out_cp1"] = _pair_sum_blocks(dwo1_4, got[1], out_dtype=BF16, name="pair_sum_w_out_1")
        held["out_cp0"] = _pair_sum_blocks(dwo0.reshape(4, rsh, D), got[2], out_dtype=BF16, name="pair_sum_w_out_0")
        held["s_cp"] = _pair_sum_blocks(gs[None], got_s[None], out_dtype=F32, name="pair_sum_small")[0]
        held["shapes"] = [sm[n].shape for n in names]
        return _chips_side([held["in_cp1"], held["out_cp1"], held["out_cp0"]], [True, True, True])

    def side_dx(dwin0):
        got = _swap_halves([dwin0], [0], "reduce_pair_0b")
        held["in_cp0"] = _pair_sum_windows(dwin0, got[0], nsh, wcov, out_dtype=BF16, name="pair_sum_w_in_0")
        return _chips_side([held["in_cp0"], held["s_cp"]], [True, False])

    dx, _, _, _, arrived_a, arrived_b = _layer_bwd(0, saved[0], xbs[0], dz, dzb, wb[0], wob[0], rope_c, rope_s,
                                                   params(0), side_dmix=side_dmix, side_dw_in=side_dw_in,
                                                   side_dx=side_dx)
    grad_x = dx[None]
    loss = lax.psum(loss_part[0, 0], ("x", "y", "c"))
    s_cp = held["s_cp"]
    mine = [sum_in(0, held["in_cp0"], arrived_b[0]), sum_out(0, held["out_cp0"], arrived_a[2]),
            sum_in(1, held["in_cp1"], arrived_a[0]), sum_out(1, held["out_cp1"], arrived_a[1]),
            _sum4(arrived_b[1], s_cp, tr=s_cp.shape[0], name="chip_sum_small")]
    other = _swap_whole(mine, "reduce_join")

    def both(k, axis):
        return jnp.where(cc == 0, jnp.concatenate([mine[k], other[k]], axis),
                         jnp.concatenate([other[k], mine[k]], axis))

    g_w_in = lax.dynamic_slice_in_dim(jnp.stack([both(2 * l, 0) for l in range(DEPTH)]), (nsh * chip) % 128, nsh, 2)
    g_w_out = jnp.stack([both(2 * l + 1, 0) for l in range(DEPTH)])
    g_small = both(2 * DEPTH, 0)

    gsm = dict(zip(names, _unpack(g_small, held["shapes"])))
    gsm["r_conv_w"] = lax.dynamic_slice_in_dim(gsm["r_conv_w"], chip * rcw_n, rcw_n, 2)
    gsm["g_conv_w"] = lax.dynamic_slice_in_dim(gsm["g_conv_w"], chip * gcw_n, gcw_n, 2)
    wts = dict(sinks=sinks, r_conv_w=r_conv_w, r_conv_b=r_conv_b, r_wa=r_wa, r_ba=r_ba, r_wx=r_wx, r_bx=r_bx,
               r_lam=r_lam, g_conv_w=g_conv_w, g_a_log=g_a_log, g_dt_bias=g_dt_bias, g_norm_w=g_norm_w,
               ln_g=ln_g, ln_b=ln_b)
    mom = dict(sinks=m_sinks, r_conv_w=m_r_conv_w, r_conv_b=m_r_conv_b, r_wa=m_r_wa, r_ba=m_r_ba, r_wx=m_r_wx,
               r_bx=m_r_bx, r_lam=m_r_lam, g_conv_w=m_g_conv_w, g_a_log=m_g_a_log, g_dt_bias=m_g_dt_bias,
               g_norm_w=m_g_norm_w, ln_g=m_ln_g, ln_b=m_ln_b)
    vel = dict(sinks=v_sinks, r_conv_w=v_r_conv_w, r_conv_b=v_r_conv_b, r_wa=v_r_wa, r_ba=v_r_ba, r_wx=v_r_wx,
               r_bx=v_r_bx, r_lam=v_r_lam, g_conv_w=v_g_conv_w, g_a_log=v_g_a_log, g_dt_bias=v_g_dt_bias,
               g_norm_w=v_g_norm_w, ln_g=v_ln_g, ln_b=v_ln_b)
    d_s, m_s, v_s = _adamw_many(*[[d[n] for n in names] for d in (wts, gsm, mom, vel)], name="adamw_small")
    d_sm, m_sm, v_sm = (dict(zip(names, a)) for a in (d_s, m_s, v_s))
    def cols(a):
        return jnp.transpose(a, (2, 0, 1))

    g_w_in_t = cols(g_w_in)
    outs_t = _adamw_cols(cols(w_in), g_w_in_t, cols(m_w_in), cols(v_w_in), name="adamw_w_in")
    d_in, m_in, v_in = (jnp.transpose(a, (1, 2, 0)) for a in outs_t)
    g_w_in = jnp.transpose(g_w_in_t, (1, 2, 0))
    d_out, m_out, v_out = _adamw(w_out, g_w_out, m_w_out, v_w_out, tr=256, name="adamw_w_out")

    order = ["w_in", "sinks", "r_conv_w", "r_conv_b", "r_wa", "r_ba", "r_wx", "r_bx", "r_lam", "g_conv_w",
             "g_a_log", "g_dt_bias", "g_norm_w", "w_out", "ln_g", "ln_b"]
    grads = dict(gsm, w_in=g_w_in, w_out=g_w_out)
    deltas = dict(d_sm, w_in=d_in, w_out=d_out)
    new_m = dict(m_sm, w_in=m_in, w_out=m_out)
    new_v = dict(v_sm, w_in=v_in, w_out=v_out)
    return (loss, grad_x, *[grads[n] for n in order], *[deltas[n] for n in order],
            *[new_m[n] for n in order], *[new_v[n] for n in order])
```
